```python
import math
import jax, jax.numpy as jnp
from jax import lax
import numpy as np

D_MODEL = 1024
BATCH = 8
SEQ = 4096
DEPTH = 4

PLE_DIM = 256
N_A_LAYERS = DEPTH // 2
N_B_LAYERS = DEPTH - N_A_LAYERS
GDN_HEADS = 8
GDN_HEAD_DIM = 128
GDN_WIDTH = GDN_HEADS * GDN_HEAD_DIM
CONV_WIDTH = 4
GDN_CHUNK = 64
SB_HEADS = 8
SB_HEAD_DIM = 128
SB_WIDTH = SB_HEADS * SB_HEAD_DIM
SB_BLOCK = 128
FFN_HIDDEN = -(-8 * D_MODEL // (3 * 256)) * 256
EPS = 1e-6

kernel_name = "yoco_gdn_stickbreaking_hybrid"


def rms_norm(x, g):
    xf = x.astype(jnp.float32)
    y = xf * lax.rsqrt(jnp.mean(xf * xf, axis=-1, keepdims=True) + EPS)
    return (y * g.astype(jnp.float32)).astype(x.dtype)


def l2_norm(x):
    return x * lax.rsqrt(jnp.sum(x * x, axis=-1, keepdims=True) + EPS)


def causal_conv(x, w):
    k_w, c = w.shape
    return lax.conv_general_dilated(x, w[:, None, :], window_strides=(1,), padding=[(k_w - 1, 0)],
                                    dimension_numbers=('NWC', 'WIO', 'NWC'), feature_group_count=c)


def to_chunks(t):
    b, s, h = t.shape[:3]
    t = t.reshape((b, s // GDN_CHUNK, GDN_CHUNK, h) + t.shape[3:])
    return jnp.swapaxes(t, 2, 3)


def gated_delta_rule(q, k, v, g, beta):
    b_, s_, h_, dk = q.shape
    dv = v.shape[-1]
    c = GDN_CHUNK
    q, k, v, g, beta = to_chunks(q), to_chunks(k), to_chunks(v), to_chunks(g), to_chunks(beta)
    G = jnp.cumsum(g, axis=-1)
    causal = jnp.tril(jnp.ones((c, c), dtype=bool))
    strict = jnp.tril(jnp.ones((c, c), dtype=bool), -1)
    decay_mat = jnp.exp(jnp.where(causal, G[..., :, None] - G[..., None, :], -jnp.inf))
    kk = jnp.einsum('bnhrd,bnhsd->bnhrs', k, k)
    a_low = jnp.where(strict, beta[..., None] * kk * decay_mat, 0.0)
    eye = jnp.eye(c, dtype=q.dtype)
    rhs = jnp.concatenate([v * beta[..., None], k * (beta * jnp.exp(G))[..., None]], axis=-1)
    sol = lax.linalg.triangular_solve(a_low + eye, rhs, left_side=True, lower=True, unit_diagonal=True)
    u, w = sol[..., :dv], sol[..., dv:]
    qk = jnp.einsum('bnhrd,bnhsd->bnhrs', q, k) * decay_mat
    q_dec = q * jnp.exp(G)[..., None]
    k_dec = k * jnp.exp(G[..., -1:] - G)[..., None]
    g_last = jnp.exp(G[..., -1])

    def step(state, xs):
        q_c, k_c, qk_c, u_c, w_c, gl_c = xs
        v_new = u_c - jnp.einsum('bhcd,bhde->bhce', w_c, state)
        o = jnp.einsum('bhcd,bhde->bhce', q_c, state) + jnp.einsum('bhrs,bhse->bhre', qk_c, v_new)
        state = state * gl_c[..., None, None] + jnp.einsum('bhcd,bhce->bhde', k_c, v_new)
        return state, o

    xs = tuple(jnp.moveaxis(t, 1, 0) for t in (q_dec, k_dec, qk, u, w, g_last))
    s0 = jnp.zeros((b_, h_, dk, dv), q.dtype)
    _, o = lax.scan(step, s0, xs)
    return o.transpose(1, 0, 3, 2, 4).reshape(b_, s_, h_, dv)


def gdn_mixer(hn, w_in, w_conv, a_log, dt_bias, norm_g, w_out):
    b_, s_, _ = hn.shape
    proj = hn @ w_in
    qkv = jax.nn.silu(causal_conv(proj[..., :3 * GDN_WIDTH], w_conv))
    gate = proj[..., 3 * GDN_WIDTH:4 * GDN_WIDTH].reshape(b_, s_, GDN_HEADS, GDN_HEAD_DIM)
    a_in = proj[..., 4 * GDN_WIDTH:4 * GDN_WIDTH + GDN_HEADS].astype(jnp.float32)
    b_in = proj[..., 4 * GDN_WIDTH + GDN_HEADS:].astype(jnp.float32)
    qkv = qkv.astype(jnp.float32).reshape(b_, s_, 3, GDN_HEADS, GDN_HEAD_DIM)
    q = l2_norm(qkv[:, :, 0]) * (GDN_HEAD_DIM ** -0.5)
    k = l2_norm(qkv[:, :, 1])
    v = qkv[:, :, 2]
    beta = jax.nn.sigmoid(b_in)
    g = -jnp.exp(a_log.astype(jnp.float32)) * jax.nn.softplus(a_in + dt_bias.astype(jnp.float32))
    o = gated_delta_rule(q, k, v, g, beta).astype(hn.dtype)
    o = rms_norm(o, norm_g) * jax.nn.silu(gate)
    return o.reshape(b_, s_, GDN_WIDTH) @ w_out


def shared_kv(h, kv_norm, w_kv, k_norm):
    b_, s_, _ = h.shape
    kv = (rms_norm(h, kv_norm) @ w_kv).reshape(b_, s_, 2, SB_HEADS, SB_HEAD_DIM)
    k = rms_norm(kv[:, :, 0], k_norm).transpose(0, 2, 1, 3)
    v = kv[:, :, 1].transpose(0, 2, 1, 3)
    return k, v


def stick_breaking(q, k, v):
    s_ = q.shape[2]
    outs = []
    for blk in range(s_ // SB_BLOCK):
        t0, t1 = blk * SB_BLOCK, (blk + 1) * SB_BLOCK
        z = jnp.einsum('bhtd,bhsd->bhts', q[:, :, t0:t1], k[:, :, :t1]).astype(jnp.float32)
        t_idx = t0 + jnp.arange(SB_BLOCK)[:, None]
        s_idx = jnp.arange(t1)[None, :]
        mask = s_idx < t_idx
        log_not = jnp.where(mask, jax.nn.log_sigmoid(-z), 0.0)
        suffix = lax.cumsum(log_not, axis=3, reverse=True) - log_not
        log_w = jnp.where(mask, jax.nn.log_sigmoid(z) + suffix, -jnp.inf)
        wgt = jnp.exp(log_w).astype(v.dtype)
        outs.append(jnp.einsum('bhts,bhsd->bhtd', wgt, v[:, :, :t1]))
    return jnp.concatenate(outs, axis=2)


def sb_mixer(hn, k_sh, v_sh, w_q, q_norm, w_out):
    b_, s_, _ = hn.shape
    q = (hn @ w_q).reshape(b_, s_, SB_HEADS, SB_HEAD_DIM)
    q = (rms_norm(q, q_norm) * (SB_HEAD_DIM ** -0.5)).transpose(0, 2, 1, 3)
    o = stick_breaking(q, k_sh, v_sh)
    return o.transpose(0, 2, 1, 3).reshape(b_, s_, SB_WIDTH) @ w_out


def swiglu(hn, w_in, w_out):
    gu = hn @ w_in
    return (jax.nn.silu(gu[..., :FFN_HIDDEN]) * gu[..., FFN_HIDDEN:]) @ w_out


def _fwd_setup_inputs(seed: int = 0) -> dict:
    key = jax.random.key(seed)
    ks = jax.random.split(key, 24)
    f32 = jnp.float32

    def nrm(k, shape, fan_in):
        return jax.random.normal(k, shape, f32) * (fan_in ** -0.5)

    def gain(k, shape):
        return 1.0 + 0.02 * jax.random.normal(k, shape, f32)

    dt = jnp.exp(jax.random.uniform(ks[8], (N_A_LAYERS, GDN_HEADS), f32, math.log(1e-3), math.log(1e-1)))
    return {
        "x": jax.random.normal(ks[0], (BATCH, SEQ, D_MODEL), f32),
        "p": jax.random.normal(ks[1], (DEPTH, BATCH, SEQ, PLE_DIM), f32),
        "ln_mix": gain(ks[2], (DEPTH, D_MODEL)),
        "ln_ffn": gain(ks[3], (DEPTH, D_MODEL)),
        "ln_ple": gain(ks[4], (DEPTH, D_MODEL)),
        "gdn_w_in": nrm(ks[5], (N_A_LAYERS, D_MODEL, 4 * GDN_WIDTH + 2 * GDN_HEADS), D_MODEL),
        "gdn_conv": nrm(ks[6], (N_A_LAYERS, CONV_WIDTH, 3 * GDN_WIDTH), CONV_WIDTH),
        "gdn_a_log": jnp.log(jax.random.uniform(ks[7], (N_A_LAYERS, GDN_HEADS), f32, 1.0, 16.0)),
        "gdn_dt_bias": dt + jnp.log(-jnp.expm1(-dt)),
        "gdn_norm": gain(ks[9], (N_A_LAYERS, GDN_HEAD_DIM)),
        "gdn_w_out": nrm(ks[10], (N_A_LAYERS, GDN_WIDTH, D_MODEL), GDN_WIDTH),
        "kv_norm": gain(ks[11], (D_MODEL,)),
        "w_kv": nrm(ks[12], (D_MODEL, 2 * SB_WIDTH), D_MODEL),
        "k_norm": gain(ks[13], (SB_HEAD_DIM,)),
        "sb_w_q": nrm(ks[14], (N_B_LAYERS, D_MODEL, SB_WIDTH), D_MODEL),
        "sb_q_norm": gain(ks[15], (N_B_LAYERS, SB_HEAD_DIM)),
        "sb_w_out": nrm(ks[16], (N_B_LAYERS, SB_WIDTH, D_MODEL), SB_WIDTH),
        "ffn_w_in": nrm(ks[17], (DEPTH, D_MODEL, 2 * FFN_HIDDEN), D_MODEL),
        "ffn_w_out": nrm(ks[18], (DEPTH, FFN_HIDDEN, D_MODEL), FFN_HIDDEN),
        "ple_w_proj": nrm(ks[19], (DEPTH, PLE_DIM, D_MODEL), PLE_DIM),
        "ple_w_gate": nrm(ks[20], (DEPTH, D_MODEL, D_MODEL), D_MODEL),
    }


def _fwd_reference(x, p, ln_mix, ln_ffn, ln_ple, gdn_w_in, gdn_conv, gdn_a_log, gdn_dt_bias, gdn_norm,
              gdn_w_out, kv_norm, w_kv, k_norm, sb_w_q, sb_q_norm, sb_w_out, ffn_w_in, ffn_w_out,
              ple_w_proj, ple_w_gate):
    h = x
    k_sh, v_sh = None, None
    for i in range(DEPTH):
        hn = rms_norm(h, ln_mix[i])
        if i < N_A_LAYERS:
            h = h + gdn_mixer(hn, gdn_w_in[i], gdn_conv[i], gdn_a_log[i], gdn_dt_bias[i],
                              gdn_norm[i], gdn_w_out[i])
        else:
            j = i - N_A_LAYERS
            h = h + sb_mixer(hn, k_sh, v_sh, sb_w_q[j], sb_q_norm[j], sb_w_out[j])
        h = h + swiglu(rms_norm(h, ln_ffn[i]), ffn_w_in[i], ffn_w_out[i])
        h = h + (p[i] @ ple_w_proj[i]) * jax.nn.sigmoid(rms_norm(h, ln_ple[i]) @ ple_w_gate[i])
        if i == N_A_LAYERS - 1:
            k_sh, v_sh = shared_kv(h, kv_norm, w_kv, k_norm)
    return h


import jax as _jax
import jax.numpy as _jnp

TWIN_FORMAT = 'train_step'
FWD_PARAMS = ['x', 'p', 'ln_mix', 'ln_ffn', 'ln_ple', 'gdn_w_in', 'gdn_conv', 'gdn_a_log', 'gdn_dt_bias', 'gdn_norm', 'gdn_w_out', 'kv_norm', 'w_kv', 'k_norm', 'sb_w_q', 'sb_q_norm', 'sb_w_out', 'ffn_w_in', 'ffn_w_out', 'ple_w_proj', 'ple_w_gate']
TWIN_WEIGHTS = ['ln_mix', 'ln_ffn', 'ln_ple', 'gdn_w_in', 'gdn_conv', 'gdn_a_log', 'gdn_dt_bias', 'gdn_norm', 'gdn_w_out', 'kv_norm', 'w_kv', 'k_norm', 'sb_w_q', 'sb_q_norm', 'sb_w_out', 'ffn_w_in', 'ffn_w_out', 'ple_w_proj', 'ple_w_gate']
TWIN_DIFF_INPUT = 'x'
TWIN_INPUTS = ['x', 'p', 'ln_mix', 'ln_ffn', 'ln_ple', 'gdn_w_in', 'gdn_conv', 'gdn_a_log', 'gdn_dt_bias', 'gdn_norm', 'gdn_w_out', 'kv_norm', 'w_kv', 'k_norm', 'sb_w_q', 'sb_q_norm', 'sb_w_out', 'ffn_w_in', 'ffn_w_out', 'ple_w_proj', 'ple_w_gate', 'loss_target', 'm_ln_mix', 'm_ln_ffn', 'm_ln_ple', 'm_gdn_w_in', 'm_gdn_conv', 'm_gdn_a_log', 'm_gdn_dt_bias', 'm_gdn_norm', 'm_gdn_w_out', 'm_kv_norm', 'm_w_kv', 'm_k_norm', 'm_sb_w_q', 'm_sb_q_norm', 'm_sb_w_out', 'm_ffn_w_in', 'm_ffn_w_out', 'm_ple_w_proj', 'm_ple_w_gate', 'v_ln_mix', 'v_ln_ffn', 'v_ln_ple', 'v_gdn_w_in', 'v_gdn_conv', 'v_gdn_a_log', 'v_gdn_dt_bias', 'v_gdn_norm', 'v_gdn_w_out', 'v_kv_norm', 'v_w_kv', 'v_k_norm', 'v_sb_w_q', 'v_sb_q_norm', 'v_sb_w_out', 'v_ffn_w_in', 'v_ffn_w_out', 'v_ple_w_proj', 'v_ple_w_gate']
TWIN_OUTPUTS = ['loss', 'grad_x', 'grad_ln_mix', 'grad_ln_ffn', 'grad_ln_ple', 'grad_gdn_w_in', 'grad_gdn_conv', 'grad_gdn_a_log', 'grad_gdn_dt_bias', 'grad_gdn_norm', 'grad_gdn_w_out', 'grad_kv_norm', 'grad_w_kv', 'grad_k_norm', 'grad_sb_w_q', 'grad_sb_q_norm', 'grad_sb_w_out', 'grad_ffn_w_in', 'grad_ffn_w_out', 'grad_ple_w_proj', 'grad_ple_w_gate', 'delta_ln_mix', 'delta_ln_ffn', 'delta_ln_ple', 'delta_gdn_w_in', 'delta_gdn_conv', 'delta_gdn_a_log', 'delta_gdn_dt_bias', 'delta_gdn_norm', 'delta_gdn_w_out', 'delta_kv_norm', 'delta_w_kv', 'delta_k_norm', 'delta_sb_w_q', 'delta_sb_q_norm', 'delta_sb_w_out', 'delta_ffn_w_in', 'delta_ffn_w_out', 'delta_ple_w_proj', 'delta_ple_w_gate', 'new_m_ln_mix', 'new_m_ln_ffn', 'new_m_ln_ple', 'new_m_gdn_w_in', 'new_m_gdn_conv', 'new_m_gdn_a_log', 'new_m_gdn_dt_bias', 'new_m_gdn_norm', 'new_m_gdn_w_out', 'new_m_kv_norm', 'new_m_w_kv', 'new_m_k_norm', 'new_m_sb_w_q', 'new_m_sb_q_norm', 'new_m_sb_w_out', 'new_m_ffn_w_in', 'new_m_ffn_w_out', 'new_m_ple_w_proj', 'new_m_ple_w_gate', 'new_v_ln_mix', 'new_v_ln_ffn', 'new_v_ln_ple', 'new_v_gdn_w_in', 'new_v_gdn_conv', 'new_v_gdn_a_log', 'new_v_gdn_dt_bias', 'new_v_gdn_norm', 'new_v_gdn_w_out', 'new_v_kv_norm', 'new_v_w_kv', 'new_v_k_norm', 'new_v_sb_w_q', 'new_v_sb_q_norm', 'new_v_sb_w_out', 'new_v_ffn_w_in', 'new_v_ffn_w_out', 'new_v_ple_w_proj', 'new_v_ple_w_gate']
TWIN_LEAF_KINDS = {'loss': 'loss', 'grad_x': 'grad_x', 'grad_ln_mix': 'grad_w', 'grad_ln_ffn': 'grad_w', 'grad_ln_ple': 'grad_w', 'grad_gdn_w_in': 'grad_w', 'grad_gdn_conv': 'grad_w', 'grad_gdn_a_log': 'grad_w', 'grad_gdn_dt_bias': 'grad_w', 'grad_gdn_norm': 'grad_w', 'grad_gdn_w_out': 'grad_w', 'grad_kv_norm': 'grad_w', 'grad_w_kv': 'grad_w', 'grad_k_norm': 'grad_w', 'grad_sb_w_q': 'grad_w', 'grad_sb_q_norm': 'grad_w', 'grad_sb_w_out': 'grad_w', 'grad_ffn_w_in': 'grad_w', 'grad_ffn_w_out': 'grad_w', 'grad_ple_w_proj': 'grad_w', 'grad_ple_w_gate': 'grad_w', 'delta_ln_mix': 'delta_w', 'delta_ln_ffn': 'delta_w', 'delta_ln_ple': 'delta_w', 'delta_gdn_w_in': 'delta_w', 'delta_gdn_conv': 'delta_w', 'delta_gdn_a_log': 'delta_w', 'delta_gdn_dt_bias': 'delta_w', 'delta_gdn_norm': 'delta_w', 'delta_gdn_w_out': 'delta_w', 'delta_kv_norm': 'delta_w', 'delta_w_kv': 'delta_w', 'delta_k_norm': 'delta_w', 'delta_sb_w_q': 'delta_w', 'delta_sb_q_norm': 'delta_w', 'delta_sb_w_out': 'delta_w', 'delta_ffn_w_in': 'delta_w', 'delta_ffn_w_out': 'delta_w', 'delta_ple_w_proj': 'delta_w', 'delta_ple_w_gate': 'delta_w', 'new_m_ln_mix': 'new_m', 'new_m_ln_ffn': 'new_m', 'new_m_ln_ple': 'new_m', 'new_m_gdn_w_in': 'new_m', 'new_m_gdn_conv': 'new_m', 'new_m_gdn_a_log': 'new_m', 'new_m_gdn_dt_bias': 'new_m', 'new_m_gdn_norm': 'new_m', 'new_m_gdn_w_out': 'new_m', 'new_m_kv_norm': 'new_m', 'new_m_w_kv': 'new_m', 'new_m_k_norm': 'new_m', 'new_m_sb_w_q': 'new_m', 'new_m_sb_q_norm': 'new_m', 'new_m_sb_w_out': 'new_m', 'new_m_ffn_w_in': 'new_m', 'new_m_ffn_w_out': 'new_m', 'new_m_ple_w_proj': 'new_m', 'new_m_ple_w_gate': 'new_m', 'new_v_ln_mix': 'new_v', 'new_v_ln_ffn': 'new_v', 'new_v_ln_ple': 'new_v', 'new_v_gdn_w_in': 'new_v', 'new_v_gdn_conv': 'new_v', 'new_v_gdn_a_log': 'new_v', 'new_v_gdn_dt_bias': 'new_v', 'new_v_gdn_norm': 'new_v', 'new_v_gdn_w_out': 'new_v', 'new_v_kv_norm': 'new_v', 'new_v_w_kv': 'new_v', 'new_v_k_norm': 'new_v', 'new_v_sb_w_q': 'new_v', 'new_v_sb_q_norm': 'new_v', 'new_v_sb_w_out': 'new_v', 'new_v_ffn_w_in': 'new_v', 'new_v_ffn_w_out': 'new_v', 'new_v_ple_w_proj': 'new_v', 'new_v_ple_w_gate': 'new_v'}


def _forward(args):
    return _fwd_reference(*[args[k] for k in FWD_PARAMS])


def _output_shape():
    def fwd():
        inp = _fwd_setup_inputs(0)
        return _fwd_reference(*[inp[k] for k in FWD_PARAMS])
    out = _jax.eval_shape(fwd)
    return out.shape, out.dtype

N_MICROBATCH = 1
ADAM_LR = 0.001
ADAM_B1 = 0.9
ADAM_B2 = 0.999
ADAM_EPS = 1e-08
ADAM_WD = 0.01
ADAM_STEP = 10
PER_EXAMPLE_BATCH_AXIS = {'x': 0, 'p': 1, 'loss_target': 0}
SHARED_INPUTS = []
_WEIGHT_DTYPES = {'ln_mix': _jnp.float32, 'ln_ffn': _jnp.float32, 'ln_ple': _jnp.float32, 'gdn_w_in': _jnp.float32, 'gdn_conv': _jnp.float32, 'gdn_a_log': _jnp.float32, 'gdn_dt_bias': _jnp.float32, 'gdn_norm': _jnp.float32, 'gdn_w_out': _jnp.float32, 'kv_norm': _jnp.float32, 'w_kv': _jnp.float32, 'k_norm': _jnp.float32, 'sb_w_q': _jnp.float32, 'sb_q_norm': _jnp.float32, 'sb_w_out': _jnp.float32, 'ffn_w_in': _jnp.float32, 'ffn_w_out': _jnp.float32, 'ple_w_proj': _jnp.float32, 'ple_w_gate': _jnp.float32}
MOMENT_SCALE = {'ln_mix': 9.889608e+00, 'ln_ffn': 2.459412e+01, 'ln_ple': 1.020503e+00, 'gdn_w_in': 7.538218e-01, 'gdn_conv': 1.084741e+00, 'gdn_a_log': 2.319305e+01, 'gdn_dt_bias': 2.188598e+01, 'gdn_norm': 9.790059e+01, 'gdn_w_out': 2.376885e+00, 'kv_norm': 2.741779e+01, 'w_kv': 1.385754e+00, 'k_norm': 3.109008e+01, 'sb_w_q': 4.050443e-01, 'sb_q_norm': 1.553540e+01, 'sb_w_out': 1.210223e+00, 'ffn_w_in': 4.205023e-01, 'ffn_w_out': 6.434750e-01, 'ple_w_proj': 5.918271e-01, 'ple_w_gate': 3.394811e-01}


def _to_microbatches(a, axis):
    t = _jnp.moveaxis(a, axis, 0)
    t = t.reshape((N_MICROBATCH, t.shape[0] // N_MICROBATCH) + t.shape[1:])
    return _jnp.moveaxis(t, 1, axis + 1)


def setup_inputs(seed: int = 0) -> dict:
    inp = _fwd_setup_inputs(seed)
    key = _jax.random.fold_in(_jax.random.key(seed), 7919)
    shape, _ = _output_shape()
    out = dict(inp)
    out["loss_target"] = _jax.random.normal(_jax.random.fold_in(key, 0), shape, _jnp.float32)
    for i, name in enumerate(TWIN_WEIGHTS):
        w = inp[name].astype(_jnp.float32)
        if MOMENT_SCALE is None:
            s = _jnp.sqrt(_jnp.mean(_jnp.square(w)) + 1e-30)
        else:
            s = MOMENT_SCALE[name]
        km, kv = _jax.random.split(_jax.random.fold_in(key, i + 1))
        out[name] = w
        out["m_" + name] = s * _jax.random.normal(km, w.shape, _jnp.float32)
        out["v_" + name] = (s * s) * _jax.random.uniform(kv, w.shape, _jnp.float32, 0.5, 1.5)
    if N_MICROBATCH > 1:
        for name, axis in PER_EXAMPLE_BATCH_AXIS.items():
            out[name] = _to_microbatches(out[name], axis)
    return {'x': out['x'], 'p': out['p'], 'ln_mix': out['ln_mix'], 'ln_ffn': out['ln_ffn'], 'ln_ple': out['ln_ple'], 'gdn_w_in': out['gdn_w_in'], 'gdn_conv': out['gdn_conv'], 'gdn_a_log': out['gdn_a_log'], 'gdn_dt_bias': out['gdn_dt_bias'], 'gdn_norm': out['gdn_norm'], 'gdn_w_out': out['gdn_w_out'], 'kv_norm': out['kv_norm'], 'w_kv': out['w_kv'], 'k_norm': out['k_norm'], 'sb_w_q': out['sb_w_q'], 'sb_q_norm': out['sb_q_norm'], 'sb_w_out': out['sb_w_out'], 'ffn_w_in': out['ffn_w_in'], 'ffn_w_out': out['ffn_w_out'], 'ple_w_proj': out['ple_w_proj'], 'ple_w_gate': out['ple_w_gate'], 'loss_target': out['loss_target'], 'm_ln_mix': out['m_ln_mix'], 'm_ln_ffn': out['m_ln_ffn'], 'm_ln_ple': out['m_ln_ple'], 'm_gdn_w_in': out['m_gdn_w_in'], 'm_gdn_conv': out['m_gdn_conv'], 'm_gdn_a_log': out['m_gdn_a_log'], 'm_gdn_dt_bias': out['m_gdn_dt_bias'], 'm_gdn_norm': out['m_gdn_norm'], 'm_gdn_w_out': out['m_gdn_w_out'], 'm_kv_norm': out['m_kv_norm'], 'm_w_kv': out['m_w_kv'], 'm_k_norm': out['m_k_norm'], 'm_sb_w_q': out['m_sb_w_q'], 'm_sb_q_norm': out['m_sb_q_norm'], 'm_sb_w_out': out['m_sb_w_out'], 'm_ffn_w_in': out['m_ffn_w_in'], 'm_ffn_w_out': out['m_ffn_w_out'], 'm_ple_w_proj': out['m_ple_w_proj'], 'm_ple_w_gate': out['m_ple_w_gate'], 'v_ln_mix': out['v_ln_mix'], 'v_ln_ffn': out['v_ln_ffn'], 'v_ln_ple': out['v_ln_ple'], 'v_gdn_w_in': out['v_gdn_w_in'], 'v_gdn_conv': out['v_gdn_conv'], 'v_gdn_a_log': out['v_gdn_a_log'], 'v_gdn_dt_bias': out['v_gdn_dt_bias'], 'v_gdn_norm': out['v_gdn_norm'], 'v_gdn_w_out': out['v_gdn_w_out'], 'v_kv_norm': out['v_kv_norm'], 'v_w_kv': out['v_w_kv'], 'v_k_norm': out['v_k_norm'], 'v_sb_w_q': out['v_sb_w_q'], 'v_sb_q_norm': out['v_sb_q_norm'], 'v_sb_w_out': out['v_sb_w_out'], 'v_ffn_w_in': out['v_ffn_w_in'], 'v_ffn_w_out': out['v_ffn_w_out'], 'v_ple_w_proj': out['v_ple_w_proj'], 'v_ple_w_gate': out['v_ple_w_gate']}


def _loss(weights, diff, rest, loss_target):
    with _jax.named_scope("forward"):
        args = {**rest, TWIN_DIFF_INPUT: diff, **{k: w.astype(_WEIGHT_DTYPES[k]) for k, w in weights.items()}}
        y = _forward(args)
    with _jax.named_scope("loss_head"):
        err = _jnp.square(y.astype(_jnp.float32) - loss_target)
        return 0.5 * _jnp.sum(_jnp.mean(err, axis=-1)) if err.ndim else 0.5 * err


def _adamw(w, g, m, v):
    m = ADAM_B1 * m + (1.0 - ADAM_B1) * g
    v = ADAM_B2 * v + (1.0 - ADAM_B2) * _jnp.square(g)
    m_hat = m / (1.0 - ADAM_B1 ** ADAM_STEP)
    v_hat = v / (1.0 - ADAM_B2 ** ADAM_STEP)
    delta = -ADAM_LR * (m_hat / (_jnp.sqrt(v_hat) + ADAM_EPS) + ADAM_WD * w)
    return delta, m, v


def reference(x, p, ln_mix, ln_ffn, ln_ple, gdn_w_in, gdn_conv, gdn_a_log, gdn_dt_bias, gdn_norm, gdn_w_out, kv_norm, w_kv, k_norm, sb_w_q, sb_q_norm, sb_w_out, ffn_w_in, ffn_w_out, ple_w_proj, ple_w_gate, loss_target, m_ln_mix, m_ln_ffn, m_ln_ple, m_gdn_w_in, m_gdn_conv, m_gdn_a_log, m_gdn_dt_bias, m_gdn_norm, m_gdn_w_out, m_kv_norm, m_w_kv, m_k_norm, m_sb_w_q, m_sb_q_norm, m_sb_w_out, m_ffn_w_in, m_ffn_w_out, m_ple_w_proj, m_ple_w_gate, v_ln_mix, v_ln_ffn, v_ln_ple, v_gdn_w_in, v_gdn_conv, v_gdn_a_log, v_gdn_dt_bias, v_gdn_norm, v_gdn_w_out, v_kv_norm, v_w_kv, v_k_norm, v_sb_w_q, v_sb_q_norm, v_sb_w_out, v_ffn_w_in, v_ffn_w_out, v_ple_w_proj, v_ple_w_gate):
    given = dict(x=x, p=p, ln_mix=ln_mix, ln_ffn=ln_ffn, ln_ple=ln_ple, gdn_w_in=gdn_w_in, gdn_conv=gdn_conv, gdn_a_log=gdn_a_log, gdn_dt_bias=gdn_dt_bias, gdn_norm=gdn_norm, gdn_w_out=gdn_w_out, kv_norm=kv_norm, w_kv=w_kv, k_norm=k_norm, sb_w_q=sb_w_q, sb_q_norm=sb_q_norm, sb_w_out=sb_w_out, ffn_w_in=ffn_w_in, ffn_w_out=ffn_w_out, ple_w_proj=ple_w_proj, ple_w_gate=ple_w_gate, loss_target=loss_target, m_ln_mix=m_ln_mix, m_ln_ffn=m_ln_ffn, m_ln_ple=m_ln_ple, m_gdn_w_in=m_gdn_w_in, m_gdn_conv=m_gdn_conv, m_gdn_a_log=m_gdn_a_log, m_gdn_dt_bias=m_gdn_dt_bias, m_gdn_norm=m_gdn_norm, m_gdn_w_out=m_gdn_w_out, m_kv_norm=m_kv_norm, m_w_kv=m_w_kv, m_k_norm=m_k_norm, m_sb_w_q=m_sb_w_q, m_sb_q_norm=m_sb_q_norm, m_sb_w_out=m_sb_w_out, m_ffn_w_in=m_ffn_w_in, m_ffn_w_out=m_ffn_w_out, m_ple_w_proj=m_ple_w_proj, m_ple_w_gate=m_ple_w_gate, v_ln_mix=v_ln_mix, v_ln_ffn=v_ln_ffn, v_ln_ple=v_ln_ple, v_gdn_w_in=v_gdn_w_in, v_gdn_conv=v_gdn_conv, v_gdn_a_log=v_gdn_a_log, v_gdn_dt_bias=v_gdn_dt_bias, v_gdn_norm=v_gdn_norm, v_gdn_w_out=v_gdn_w_out, v_kv_norm=v_kv_norm, v_w_kv=v_w_kv, v_k_norm=v_k_norm, v_sb_w_q=v_sb_w_q, v_sb_q_norm=v_sb_q_norm, v_sb_w_out=v_sb_w_out, v_ffn_w_in=v_ffn_w_in, v_ffn_w_out=v_ffn_w_out, v_ple_w_proj=v_ple_w_proj, v_ple_w_gate=v_ple_w_gate)
    weights = {n: given[n] for n in TWIN_WEIGHTS}
    shared = {n: given[n] for n in SHARED_INPUTS}
    per_example = {n: given[n] for n in ['x', 'p']}
    grad_fn = _jax.value_and_grad(_loss, argnums=(0, 1))

    def one_microbatch(ex, loss_target):
        ex = dict(ex)
        diff = ex.pop(TWIN_DIFF_INPUT)
        return grad_fn(weights, diff, {**shared, **ex}, loss_target)

    if N_MICROBATCH == 1:
        loss, (grad_w, grad_x) = one_microbatch(per_example, given["loss_target"])
    else:
        def body(carry, xs):
            loss_sum, grad_sum = carry
            l_k, (gw_k, gx_k) = one_microbatch(xs[0], xs[1])
            with _jax.named_scope("update"):
                return (loss_sum + l_k, _jax.tree.map(_jnp.add, grad_sum, gw_k)), gx_k

        init = (_jnp.zeros((), _jnp.float32), _jax.tree.map(_jnp.zeros_like, weights))
        (loss, grad_w), grad_x = _jax.lax.scan(body, init, (per_example, given["loss_target"]))
    with _jax.named_scope("update"):
        delta_w, new_m, new_v = {}, {}, {}
        for n in TWIN_WEIGHTS:
            delta_w[n], new_m[n], new_v[n] = _adamw(weights[n], grad_w[n], given["m_" + n], given["v_" + n])
    return (loss, grad_x, *[grad_w[n] for n in TWIN_WEIGHTS], *[delta_w[n] for n in TWIN_WEIGHTS],
            *[new_m[n] for n in TWIN_WEIGHTS], *[new_v[n] for n in TWIN_WEIGHTS])
```

```python
import functools
import math

import jax
import jax.numpy as jnp
from jax import lax
from jax.experimental import pallas as pl
from jax.experimental.pallas import tpu as pltpu

F32 = jnp.float32
BF16 = jnp.bfloat16
EPS = 1e-6
HEADS = 8
HEAD_DIM = 128
WIDTH = HEADS * HEAD_DIM
CHUNK = 64
CONV_WIDTH = 4
N_CHIPS = 4
ADAM_LR, ADAM_B1, ADAM_B2, ADAM_EPS, ADAM_WD, ADAM_STEP = 0.001, 0.9, 0.999, 1e-08, 0.01, 10
V7X_VMEM_BYTES = 64 * 1024 * 1024
VMEM_LIMIT = V7X_VMEM_BYTES - 8 * 1024 * 1024
HIGHEST = lax.Precision.HIGHEST
MESH = pl.DeviceIdType.MESH


def _params(sem=None):
    return pltpu.CompilerParams(dimension_semantics=sem, vmem_limit_bytes=VMEM_LIMIT)


def _pick(n, prefs):
    for t in prefs:
        if t <= n and n % t == 0:
            return t
    return n


def _bdot(a, b, dims):
    return lax.dot_general(a.astype(BF16), b.astype(BF16), (((dims[0],), (dims[1],)), ((), ())),
                           preferred_element_type=F32)


NN, NT, TN = (1, 0), (1, 1), (0, 0)


def matmul(a, b, form, *, out_dtype=F32, add=None, name):
    if form == "nn":
        (m, k), (k2, n) = a.shape, b.shape
    elif form == "nt":
        (m, k), (n, k2) = a.shape, b.shape
    else:
        (k, m), (k2, n) = a.shape, b.shape
    assert k == k2, (a.shape, b.shape, form)
    tm = _pick(m, (512, 256, 128))
    tn = _pick(n, (512, 256, 128))
    tk = k if k <= 1024 else _pick(k, (1024, 1408, 512, 256, 128))
    nk = k // tk
    if form == "tn":
        a_spec = pl.BlockSpec((tk, tm), lambda i, j, kk: (kk, i))
    else:
        a_spec = pl.BlockSpec((tm, tk), lambda i, j, kk: (i, kk))
    if form == "nt":
        b_spec = pl.BlockSpec((tn, tk), lambda i, j, kk: (j, kk))
    else:
        b_spec = pl.BlockSpec((tk, tn), lambda i, j, kk: (kk, j))
    o_spec = pl.BlockSpec((tm, tn), lambda i, j, kk: (i, j))
    dims = {"nn": NN, "nt": NT, "tn": TN}[form]
    has_add = add is not None

    def body(*refs):
        if has_add:
            a_ref, b_ref, add_ref, o_ref, acc_ref = refs
        else:
            a_ref, b_ref, o_ref, acc_ref = refs
        kk = pl.program_id(2)

        @pl.when(kk == 0)
        def _():
            acc_ref[...] = jnp.zeros_like(acc_ref)

        acc_ref[...] += _bdot(a_ref[...], b_ref[...], dims)

        @pl.when(kk == nk - 1)
        def _():
            r = acc_ref[...]
            if has_add:
                r = r + add_ref[...].astype(F32)
            o_ref[...] = r.astype(out_dtype)

    in_specs = [a_spec, b_spec] + ([o_spec] if has_add else [])
    args = (a, b) + ((add,) if has_add else ())
    return pl.pallas_call(
        body, name=name, grid=(m // tm, n // tn, nk), in_specs=in_specs, out_specs=o_spec,
        out_shape=jax.ShapeDtypeStruct((m, n), out_dtype),
        scratch_shapes=[pltpu.VMEM((tm, tn), F32)],
        compiler_params=_params(("parallel", "parallel", "arbitrary")),
    )(*args)


def _const(c):
    return lambda j: c


def rowwise(fn, rows, params, outs, accs=(), *, name, tm, ncol=1):
    t = rows[0][0].shape[0]
    tm = min(tm, t)
    assert t % tm == 0
    n_rows, n_par, n_out, n_acc = len(rows), len(params), len(outs), len(accs)

    def body(*refs):
        j, i = pl.program_id(0), pl.program_id(1)
        ins = [r[...] for r in refs[:n_rows + n_par]]
        o_refs = refs[n_rows + n_par:n_rows + n_par + n_out]
        a_refs = refs[n_rows + n_par + n_out:]
        row_outs, acc_outs = fn(*ins)
        for r, val in zip(o_refs, row_outs):
            r[...] = val.astype(r.dtype)
        for r, val, spec in zip(a_refs, acc_outs, accs):
            first = (i == 0) & (j == 0) if spec[4] else (i == 0)

            @pl.when(first)
            def _(r=r, val=val):
                r[...] = val.astype(F32)

            @pl.when(jnp.logical_not(first))
            def _(r=r, val=val):
                r[...] += val.astype(F32)

    in_specs = [pl.BlockSpec((tm, w), lambda j, i, cf=cf: (i, cf(j))) for _, w, cf in rows]
    in_specs += [pl.BlockSpec((p.shape[0], w), lambda j, i, cf=cf: (0, cf(j))) for p, w, cf in params]
    out_specs = [pl.BlockSpec((tm, w), lambda j, i, cf=cf: (i, cf(j))) for _, _, w, cf in outs]
    out_specs += [pl.BlockSpec((r, w), lambda j, i, cf=cf: (0, cf(j))) for r, _, w, cf, _ in accs]
    out_shape = [jax.ShapeDtypeStruct((t, tw), dt) for tw, dt, _, _ in outs]
    out_shape += [jax.ShapeDtypeStruct((r, tw), F32) for r, tw, _, _, _ in accs]
    res = pl.pallas_call(
        body, name=name, grid=(ncol, t // tm), in_specs=in_specs, out_specs=out_specs, out_shape=out_shape,
        compiler_params=_params(("arbitrary", "arbitrary")),
    )(*[r[0] for r in rows], *[p[0] for p in params])
    return res[:n_out], res[n_out:]


def _full(arr):
    return (arr, arr.shape[1], _const(0))


def _rms(x, g):
    x = x.astype(F32)
    return x * lax.rsqrt(jnp.mean(x * x, axis=-1, keepdims=True) + EPS) * g.astype(F32)


def _sigmoid(x):
    return 1.0 / (1.0 + jnp.exp(-x))


def _silu(x):
    return x * _sigmoid(x)


def _softplus(x):
    return jnp.maximum(x, 0.0) + jnp.log(1.0 + jnp.exp(-jnp.abs(x)))


def rms_fwd(h, g, *, name):
    d = h.shape[1]
    (hn,), _ = rowwise(lambda x, gg: ((_rms(x, gg),), ()), [_full(h)], [_full(g)],
                       [(d, BF16, d, _const(0))], name=name, tm=512)
    return hn


def rms_bwd(h, g, dhn, dh_res, *, name):
    d = h.shape[1]

    def fn(x, ct, res, gg):
        _, vjp = jax.vjp(_rms, x.astype(F32), gg.astype(F32))
        dx, dg = vjp(ct.astype(F32))
        return (res.astype(F32) + dx,), (dg,)

    (dh,), (dg,) = rowwise(fn, [_full(h), _full(dhn), _full(dh_res)], [_full(g)],
                           [(d, F32, d, _const(0))], [(1, d, d, _const(0), True)], name=name, tm=256)
    return dh, dg


def _head_rms(x, g, scale):
    x = x.astype(F32)
    return x * lax.rsqrt(jnp.mean(x * x, axis=-1, keepdims=True) + EPS) * (g.astype(F32) * scale)


def headnorm_fwd(x, col0, g, scale, *, name):
    (y,), _ = rowwise(lambda a, gg: ((_head_rms(a, gg, scale),), ()),
                      [(x, HEAD_DIM, lambda j: col0 + j)], [_full(g)],
                      [(WIDTH, BF16, HEAD_DIM, lambda j: j)], name=name, tm=1024, ncol=HEADS)
    return y


def headnorm_bwd(x, col0, g, scale, dy, *, name, out_dtype=BF16):
    def fn(a, ct, gg):
        _, vjp = jax.vjp(lambda a_, g_: _head_rms(a_, g_, scale), a.astype(F32), gg.astype(F32))
        dx, dg = vjp(ct.astype(F32))
        return (dx,), (dg,)

    (dx,), (dg,) = rowwise(fn, [(x, HEAD_DIM, lambda j: col0 + j), (dy, HEAD_DIM, lambda j: j)], [_full(g)],
                           [(WIDTH, out_dtype, HEAD_DIM, lambda j: j)],
                           [(1, HEAD_DIM, HEAD_DIM, _const(0), True)], name=name, tm=1024, ncol=HEADS)
    return dx, dg


def _gatenorm(o, gate, g):
    return _head_rms(o, g, 1.0) * _silu(gate.astype(F32))


def gatenorm_fwd(o, proj, g, *, name):
    (y,), _ = rowwise(lambda a, gt, gg: ((_gatenorm(a, gt, gg),), ()),
                      [(o, HEAD_DIM, lambda j: j), (proj, HEAD_DIM, lambda j: 3 * HEADS + j)], [_full(g)],
                      [(WIDTH, BF16, HEAD_DIM, lambda j: j)], name=name, tm=1024, ncol=HEADS)
    return y


def gatenorm_bwd(o, proj, g, dy, *, name):
    def fn(a, gt, ct, gg):
        _, vjp = jax.vjp(_gatenorm, a.astype(F32), gt.astype(F32), gg.astype(F32))
        da, dgt, dg = vjp(ct.astype(F32))
        return (da, dgt), (dg,)

    (do, dproj), (dg,) = rowwise(
        fn, [(o, HEAD_DIM, lambda j: j), (proj, HEAD_DIM, lambda j: 3 * HEADS + j), (dy, HEAD_DIM, lambda j: j)],
        [_full(g)],
        [(WIDTH, F32, HEAD_DIM, lambda j: j), (4 * WIDTH, BF16, HEAD_DIM, lambda j: 3 * HEADS + j)],
        [(1, HEAD_DIM, HEAD_DIM, _const(0), True)], name=name, tm=1024, ncol=HEADS)
    return do, dproj, dg


def _swiglu(g, u):
    return _silu(g.astype(F32)) * u.astype(F32)


def swiglu_fwd(gu, *, name):
    f = gu.shape[1] // 2
    (act,), _ = rowwise(lambda g, u: ((_swiglu(g, u),), ()), [(gu, f, _const(0)), (gu, f, _const(1))], [],
                        [(f, BF16, f, _const(0))], name=name, tm=256)
    return act


def swiglu_bwd(gu, dact, *, name):
    f = gu.shape[1] // 2

    def fn(g, u, ct):
        _, vjp = jax.vjp(_swiglu, g.astype(F32), u.astype(F32))
        dg, du = vjp(ct.astype(F32))
        return (jnp.concatenate([dg.astype(BF16), du.astype(BF16)], axis=1),), ()

    (dgu,), _ = rowwise(fn, [(gu, f, _const(0)), (gu, f, _const(1)), _full(dact)], [],
                        [(2 * f, BF16, 2 * f, _const(0))], name=name, tm=256)
    return dgu


def ple_fwd(h, pp, gt, *, name):
    d = h.shape[1]
    (out,), _ = rowwise(lambda a, b, c: ((a + b * _sigmoid(c),), ()), [_full(h), _full(pp), _full(gt)], [],
                        [(d, F32, d, _const(0))], name=name, tm=512)
    return out


def ple_bwd(dh, pp, gt, *, name):
    d = dh.shape[1]

    def fn(ct, b, c):
        s = _sigmoid(c)
        return (ct * s, ct * b * s * (1.0 - s)), ()

    (dpp, dgt), _ = rowwise(fn, [_full(dh), _full(pp), _full(gt)], [],
                            [(d, BF16, d, _const(0)), (d, BF16, d, _const(0))], name=name, tm=512)
    return dpp, dgt


def loss_head(y, tgt, *, name):
    d = y.shape[1]

    def fn(a, b):
        e = a - b
        return (e * (1.0 / d),), (jnp.sum(e * e, axis=0, keepdims=True),)

    (dy,), (sq,) = rowwise(fn, [_full(y), _full(tgt)], [], [(d, F32, d, _const(0))],
                           [(1, d, d, _const(0), True)], name=name, tm=512)
    return dy, sq


def adamw(w, g, m, v, *, name):
    shape = w.shape
    cols = shape[-1]
    flat = lambda a: a.reshape(-1, cols)
    bc1 = 1.0 - ADAM_B1 ** ADAM_STEP
    bc2 = 1.0 - ADAM_B2 ** ADAM_STEP

    def fn(w_, g_, m_, v_):
        m_ = ADAM_B1 * m_ + (1.0 - ADAM_B1) * g_
        v_ = ADAM_B2 * v_ + (1.0 - ADAM_B2) * (g_ * g_)
        delta = -ADAM_LR * ((m_ / bc1) / (jnp.sqrt(v_ / bc2) + ADAM_EPS) + ADAM_WD * w_)
        return (delta, m_, v_), ()

    o = (cols, F32, cols, _const(0))
    (d_, m_, v_), _ = rowwise(fn, [_full(flat(w)), _full(flat(g)), _full(flat(m)), _full(flat(v))], [],
                              [o, o, o], name=name, tm=256)
    return d_.reshape(shape), m_.reshape(shape), v_.reshape(shape)


CONV_STRIP = 256


def _shift_down(x, d):
    if d == 0:
        return x
    rows = lax.broadcasted_iota(jnp.int32, x.shape, 0)
    return jnp.where(rows >= d, pltpu.roll(x, d, 0), 0.0)


def _shift_up(x, d):
    if d == 0:
        return x
    t = x.shape[0]
    rows = lax.broadcasted_iota(jnp.int32, x.shape, 0)
    return jnp.where(rows < t - d, pltpu.roll(x, t - d, 0), 0.0)


def _conv(x, w):
    acc = None
    for j in range(CONV_WIDTH):
        term = _shift_down(x, CONV_WIDTH - 1 - j) * w[j:j + 1, :]
        acc = term if acc is None else acc + term
    return acc


def conv_fwd(proj, w, *, name):
    t = proj.shape[0]
    per = WIDTH // CONV_STRIP

    def body(x_ref, w_ref, o_ref):
        o_ref[0] = _silu(_conv(x_ref[...], w_ref[...]))

    return pl.pallas_call(
        body, name=name, grid=(3 * per,),
        in_specs=[pl.BlockSpec((t, CONV_STRIP), lambda j: (0, j)), pl.BlockSpec((CONV_WIDTH, CONV_STRIP), lambda j: (0, j))],
        out_specs=pl.BlockSpec((1, t, CONV_STRIP), lambda j: (j // per, 0, j % per)),
        out_shape=jax.ShapeDtypeStruct((3, t, WIDTH), F32),
        compiler_params=_params(("parallel",)),
    )(proj, w)


def conv_bwd(proj, w, dqkv, dproj, *, name):
    t = proj.shape[0]
    per = WIDTH // CONV_STRIP

    def body(x_ref, w_ref, d_ref, _, dx_ref, dw_ref):
        x, w_ = x_ref[...], w_ref[...]
        c = _conv(x, w_)
        s = _sigmoid(c)
        dc = d_ref[0] * (s + c * s * (1.0 - s))
        dx = None
        for j in range(CONV_WIDTH):
            d = CONV_WIDTH - 1 - j
            term = _shift_up(dc, d) * w_[j:j + 1, :]
            dx = term if dx is None else dx + term
            dw_ref[j:j + 1, :] = jnp.sum(dc * _shift_down(x, d), axis=0, keepdims=True)
        dx_ref[...] = dx.astype(dx_ref.dtype)

    return pl.pallas_call(
        body, name=name, grid=(3 * per,),
        in_specs=[pl.BlockSpec((t, CONV_STRIP), lambda j: (0, j)), pl.BlockSpec((CONV_WIDTH, CONV_STRIP), lambda j: (0, j)),
                  pl.BlockSpec((1, t, CONV_STRIP), lambda j: (j // per, 0, j % per)), pl.BlockSpec(memory_space=pl.ANY)],
        out_specs=[pl.BlockSpec((t, CONV_STRIP), lambda j: (0, j)), pl.BlockSpec((CONV_WIDTH, CONV_STRIP), lambda j: (0, j))],
        out_shape=[jax.ShapeDtypeStruct(dproj.shape, dproj.dtype), jax.ShapeDtypeStruct((CONV_WIDTH, 3 * WIDTH), F32)],
        input_output_aliases={3: 0},
        compiler_params=_params(("parallel",)),
    )(proj, w, dqkv, dproj)


def _gdn_gates(ab, a_log, dt_bias):
    a_in, b_in = ab[:HEADS], ab[HEADS:]
    g = -jnp.exp(a_log) * _softplus(a_in + dt_bias)
    return g, _sigmoid(b_in)


def gates_fwd(ab, a_log, dt_bias, *, name):
    t = ab.shape[1]

    def body(ab_ref, al_ref, dt_ref, g_ref, b_ref):
        g_ref[...], b_ref[...] = _gdn_gates(ab_ref[...], al_ref[...], dt_ref[...])

    s = jax.ShapeDtypeStruct((HEADS, t), F32)
    return pl.pallas_call(body, name=name, out_shape=[s, s], compiler_params=_params())(ab, a_log, dt_bias)


def gates_bwd(ab, a_log, dt_bias, dg, dbeta, *, name):
    t = ab.shape[1]

    def body(ab_ref, al_ref, dt_ref, dg_ref, db_ref, dab_ref, dal_ref, ddt_ref):
        _, vjp = jax.vjp(_gdn_gates, ab_ref[...], al_ref[...], dt_ref[...])
        dab_ref[...], dal_ref[...], ddt_ref[...] = vjp((dg_ref[...], db_ref[...]))

    c = jax.ShapeDtypeStruct((HEADS, 1), F32)
    return pl.pallas_call(body, name=name, out_shape=[jax.ShapeDtypeStruct((2 * HEADS, t), F32), c, c],
                          compiler_params=_params())(ab, a_log, dt_bias, dg, dbeta)


def _split3(x):
    hi = x.astype(BF16)
    r1 = x - hi.astype(F32)
    mid = r1.astype(BF16)
    lo = (r1 - mid.astype(F32)).astype(BF16)
    return hi, mid, lo


def _dot01(x, m01):
    hi, mid, lo = _split3(x)
    m01 = m01.astype(BF16)
    return _bdot(hi, m01, NN) + _bdot(mid, m01, NN) + _bdot(lo, m01, NN)


def _hdot(a, b, dims=NN):
    return lax.dot_general(a, b, (((dims[0],), (dims[1],)), ((), ())), precision=HIGHEST,
                           preferred_element_type=F32)


def _rowsum(x):
    return jnp.sum(x, axis=1, keepdims=True)


def _colsum(x):
    return jnp.sum(x, axis=0, keepdims=True)


def _inv_unit_lower(a, eye):
    p = jnp.where(eye, 1.0, 0.0) - a
    ak = a
    for _ in range(int(math.log2(CHUNK)) - 1):
        ak = _hdot(ak, ak)
        p = p + _hdot(p, ak)
    return p


def _gdn_chunk(qr, kr, v, grow, brow, tinv=None):
    c = CHUNK
    ri = lax.broadcasted_iota(jnp.int32, (c, c), 0)
    ci = lax.broadcasted_iota(jnp.int32, (c, c), 1)
    eye, lower, strict = ri == ci, ri >= ci, ri > ci
    to_col = lambda row: _rowsum(jnp.where(eye, jnp.broadcast_to(row, (c, c)), 0.0))
    cum_row = _dot01(jnp.broadcast_to(grow, (8, c)), ri <= ci)[0:1]
    gcol, bcol = to_col(cum_row), to_col(brow)
    glast = _colsum(jnp.where(ri[:, 0:1] == c - 1, gcol, 0.0))
    rq = lax.rsqrt(_rowsum(qr * qr) + EPS)
    rk = lax.rsqrt(_rowsum(kr * kr) + EPS)
    scale = HEAD_DIM ** -0.5
    qn, kn = qr * (rq * scale), kr * rk
    dec = jnp.where(lower, jnp.exp(jnp.minimum(gcol - cum_row, 0.0)), 0.0)
    kk, qk = _bdot(kn, kn, NT), _bdot(qn, kn, NT)
    gam_col, e_col, gam_last = jnp.exp(gcol), jnp.exp(glast - gcol), jnp.exp(glast)
    if tinv is None:
        tinv = _inv_unit_lower(jnp.where(strict, bcol * kk * dec, 0.0), eye)
    u = _hdot(tinv, v * bcol)
    w = _hdot(tinv, kn * (bcol * gam_col))
    return dict(eye=eye, lower=lower, strict=strict, ri=ri, ci=ci, gcol=gcol, bcol=bcol, rq=rq, rk=rk, qn=qn, kn=kn,
                dec=dec, kk=kk, qk=qk, gam_col=gam_col, e_col=e_col, gam_last=gam_last, tinv=tinv, u=u, w=w,
                aqk=qk * dec, qt=qn * gam_col, kt=kn * e_col, scale=scale)


def gdn_fwd(qkv, g4, b4, *, name):
    t = qkv.shape[1]
    n = t // CHUNK
    d = HEAD_DIM

    def body(qkv_ref, g_ref, b_ref, o_ref, s0_ref, t_ref, s_ref):
        @pl.when(pl.program_id(0) == 0)
        def _():
            s_ref[...] = jnp.zeros_like(s_ref)

        for h in range(HEADS):
            cols = slice(h * d, (h + 1) * d)
            z = _gdn_chunk(qkv_ref[0, :, cols], qkv_ref[1, :, cols], qkv_ref[2, :, cols], g_ref[0, h], b_ref[0, h])
            s0 = s_ref[h]
            s0_ref[0, h] = s0
            t_ref[0, h] = z["tinv"]
            v_new = z["u"] - _bdot(z["w"], s0, NN)
            o_ref[:, cols] = _bdot(z["qt"], s0, NN) + _bdot(z["aqk"], v_new, NN)
            s_ref[h] = s0 * z["gam_last"] + _bdot(z["kt"], v_new, TN)

    gspec = pl.BlockSpec((1, HEADS, 1, CHUNK), lambda i: (i, 0, 0, 0))
    return pl.pallas_call(
        body, name=name, grid=(n,),
        in_specs=[pl.BlockSpec((3, CHUNK, WIDTH), lambda i: (0, i, 0)), gspec, gspec],
        out_specs=[pl.BlockSpec((CHUNK, WIDTH), lambda i: (i, 0)),
                   pl.BlockSpec((1, HEADS, d, d), lambda i: (i, 0, 0, 0)),
                   pl.BlockSpec((1, HEADS, CHUNK, CHUNK), lambda i: (i, 0, 0, 0))],
        out_shape=[jax.ShapeDtypeStruct((t, WIDTH), F32), jax.ShapeDtypeStruct((n, HEADS, d, d), F32),
                   jax.ShapeDtypeStruct((n, HEADS, CHUNK, CHUNK), F32)],
        scratch_shapes=[pltpu.VMEM((HEADS, d, d), F32)],
        compiler_params=_params(("arbitrary",)),
    )(qkv, g4, b4)


def gdn_bwd(qkv, g4, b4, s0_all, tinv_all, do, *, name):
    t = qkv.shape[1]
    n = t // CHUNK
    d = HEAD_DIM
    c = CHUNK

    def body(qkv_ref, g_ref, b_ref, s0_ref, t_ref, do_ref, dqkv_ref, dg_ref, db_ref, ds_ref):
        @pl.when(pl.program_id(0) == 0)
        def _():
            ds_ref[...] = jnp.zeros_like(ds_ref)

        for h in range(HEADS):
            cols = slice(h * d, (h + 1) * d)
            qr, kr, v = qkv_ref[0, :, cols], qkv_ref[1, :, cols], qkv_ref[2, :, cols]
            z = _gdn_chunk(qr, kr, v, g_ref[0, h], b_ref[0, h], tinv=t_ref[0, h])
            s0, ds, dout = s0_ref[0, h], ds_ref[h], do_ref[:, cols]
            qn, kn, u, w, dec, kk, qk = z["qn"], z["kn"], z["u"], z["w"], z["dec"], z["kk"], z["qk"]
            bcol, gam_col, e_col, gam_last = z["bcol"], z["gam_col"], z["e_col"], z["gam_last"]
            v_new = u - _bdot(w, s0, NN)
            dv_new = _bdot(z["aqk"], dout, TN) + _bdot(z["kt"], ds, NN)
            daqk = jnp.where(z["lower"], _bdot(dout, v_new, NT), 0.0)
            dqt = _bdot(dout, s0, NT)
            dkt = _bdot(v_new, ds, NT)
            dgam_last = jnp.sum(ds * s0, keepdims=True)
            ds_ref[h] = _bdot(z["qt"], dout, TN) + ds * gam_last - _bdot(w, dv_new, TN)
            dw = -_bdot(dv_new, s0, NT)
            dru = _hdot(z["tinv"], dv_new, TN)
            drw = _hdot(z["tinv"], dw, TN)
            dal = -jnp.where(z["strict"], _bdot(dru, u, NT) + _bdot(drw, w, NT), 0.0)
            t1 = dal * kk * dec
            dkk = dal * bcol * dec
            ddec = dal * bcol * kk + daqk * qk
            dqk = daqk * dec
            s_w = _rowsum(drw * kn)
            dbeta_col = _rowsum(t1) + _rowsum(dru * v) + gam_col * s_w
            dkn = (drw * (bcol * gam_col) + _bdot(dkk, kn, NN) + _bdot(dkk, kn, TN) + _bdot(dqk, qn, TN)
                   + dkt * e_col)
            dqn = _bdot(dqk, kn, NN) + dqt * gam_col
            e_mat = ddec * dec
            de_col = _rowsum(dkt * kn)
            dg_cum = (_rowsum(e_mat) + (bcol * s_w + _rowsum(dqt * qn)) * gam_col - de_col * e_col
                      - _rowsum(jnp.where(z["eye"], jnp.broadcast_to(_colsum(e_mat), (c, c)), 0.0)))
            dg_last = _colsum(de_col * e_col) + dgam_last * gam_last
            dg_ref[0, h] = _colsum(jnp.where(z["lower"], dg_cum, 0.0)) + dg_last
            db_ref[0, h] = _colsum(jnp.where(z["eye"], dbeta_col, 0.0))
            rq, rk = z["rq"], z["rk"]
            dqkv_ref[0, :, cols] = z["scale"] * (rq * dqn - qr * (rq * rq * rq) * _rowsum(qr * dqn))
            dqkv_ref[1, :, cols] = rk * dkn - kr * (rk * rk * rk) * _rowsum(kr * dkn)
            dqkv_ref[2, :, cols] = dru * bcol

    rev = lambda i: n - 1 - i
    gspec = pl.BlockSpec((1, HEADS, 1, CHUNK), lambda i: (rev(i), 0, 0, 0))
    return pl.pallas_call(
        body, name=name, grid=(n,),
        in_specs=[pl.BlockSpec((3, CHUNK, WIDTH), lambda i: (0, rev(i), 0)), gspec, gspec,
                  pl.BlockSpec((1, HEADS, d, d), lambda i: (rev(i), 0, 0, 0)),
                  pl.BlockSpec((1, HEADS, CHUNK, CHUNK), lambda i: (rev(i), 0, 0, 0)),
                  pl.BlockSpec((CHUNK, WIDTH), lambda i: (rev(i), 0))],
        out_specs=[pl.BlockSpec((3, CHUNK, WIDTH), lambda i: (0, rev(i), 0)), gspec, gspec],
        out_shape=[jax.ShapeDtypeStruct((3, t, WIDTH), F32), jax.ShapeDtypeStruct((n, HEADS, 1, CHUNK), F32),
                   jax.ShapeDtypeStruct((n, HEADS, 1, CHUNK), F32)],
        scratch_shapes=[pltpu.VMEM((HEADS, d, d), F32)],
        compiler_params=_params(("arbitrary",)),
    )(qkv, g4, b4, s0_all, tinv_all, do)


SB_BLOCK = 256


def _dot01_2(x, m01):
    hi = x.astype(BF16)
    lo = (x - hi.astype(F32)).astype(BF16)
    return _bdot(hi, m01, NN) + _bdot(lo, m01, NN)


def _sb_weights(q, kb, carry, mask, upper):
    z = _bdot(q, kb, NT)
    ls = jnp.minimum(z, 0.0) - jnp.log(1.0 + jnp.exp(-jnp.abs(z)))
    ln = jnp.where(mask, ls - z, 0.0)
    a = jnp.where(mask, jnp.exp(ls + _dot01_2(ln, upper) + carry), 0.0)
    return z, ln, a


def _sb_masks(i, jb, blk):
    ri = lax.broadcasted_iota(jnp.int32, (blk, blk), 0)
    ci = lax.broadcasted_iota(jnp.int32, (blk, blk), 1)
    return (jb * blk + ci) < (i * blk + ri)


def sb_fwd(q, k, v, *, name):
    t = q.shape[0]
    blk = min(SB_BLOCK, t)
    d = HEAD_DIM

    def body(q_ref, k_ref, v_ref, o_ref):
        i = pl.program_id(1)
        qb = q_ref[...]
        ri = lax.broadcasted_iota(jnp.int32, (blk, blk), 0)
        ci = lax.broadcasted_iota(jnp.int32, (blk, blk), 1)
        upper = (ri > ci).astype(BF16)

        def step(s, carry):
            c, acc = carry
            jb = i - s
            rows = pl.ds(pl.multiple_of(jb * blk, blk), blk)
            _, ln, a = _sb_weights(qb, k_ref[rows, :], c, _sb_masks(i, jb, blk), upper)
            return c + _rowsum(ln), acc + _bdot(a, v_ref[rows, :], NN)

        _, acc = lax.fori_loop(0, i + 1, step, (jnp.zeros((blk, 1), F32), jnp.zeros((blk, d), F32)))
        o_ref[...] = acc.astype(o_ref.dtype)

    qspec = pl.BlockSpec((blk, d), lambda h, i: (i, h))
    kspec = pl.BlockSpec((t, d), lambda h, i: (0, h))
    return pl.pallas_call(
        body, name=name, grid=(HEADS, t // blk), in_specs=[qspec, kspec, kspec], out_specs=qspec,
        out_shape=jax.ShapeDtypeStruct((t, WIDTH), BF16),
        compiler_params=_params(("parallel", "arbitrary")),
    )(q, k, v)


def sb_bwd(q, k, v, do, dk0, dv0, *, name):
    t = q.shape[0]
    blk = min(SB_BLOCK, t)
    d = HEAD_DIM
    nb = t // blk

    def body(q_ref, k_ref, v_ref, do_ref, dk0_ref, dv0_ref, dq_ref, dk_ref, dv_ref, p_buf, z_buf):
        i = pl.program_id(1)

        @pl.when(i == 0)
        def _():
            dk_ref[...] = dk0_ref[...]
            dv_ref[...] = dv0_ref[...]

        qb, dob = q_ref[...], do_ref[...]
        ri = lax.broadcasted_iota(jnp.int32, (blk, blk), 0)
        ci = lax.broadcasted_iota(jnp.int32, (blk, blk), 1)
        upper = (ri > ci).astype(BF16)
        lower = (ri < ci).astype(BF16)

        def right_to_left(s, c):
            jb = i - s
            rows = pl.ds(pl.multiple_of(jb * blk, blk), blk)
            z, ln, a = _sb_weights(qb, k_ref[rows, :], c, _sb_masks(i, jb, blk), upper)
            p_buf[jb] = a * _bdot(dob, v_ref[rows, :], NT)
            z_buf[jb] = z
            dv_ref[rows, :] += _bdot(a, dob, TN)
            return c + _rowsum(ln)

        lax.fori_loop(0, i + 1, right_to_left, jnp.zeros((blk, 1), F32))

        def left_to_right(jb, carry):
            cp, dq = carry
            rows = pl.ds(pl.multiple_of(jb * blk, blk), blk)
            p = p_buf[jb]
            sg = _sigmoid(z_buf[jb])
            dz = jnp.where(_sb_masks(i, jb, blk), p * (1.0 - sg) - sg * (_dot01_2(p, lower) + cp), 0.0)
            dk_ref[rows, :] += _bdot(dz, qb, TN)
            return cp + _rowsum(p), dq + _bdot(dz, k_ref[rows, :], NN)

        _, dq = lax.fori_loop(0, i + 1, left_to_right, (jnp.zeros((blk, 1), F32), jnp.zeros((blk, d), F32)))
        dq_ref[...] = dq

    qspec = pl.BlockSpec((blk, d), lambda h, i: (i, h))
    kspec = pl.BlockSpec((t, d), lambda h, i: (0, h))
    s = jax.ShapeDtypeStruct((t, WIDTH), F32)
    return pl.pallas_call(
        body, name=name, grid=(HEADS, nb), in_specs=[qspec, kspec, kspec, qspec, kspec, kspec],
        out_specs=[qspec, kspec, kspec], out_shape=[s, s, s],
        scratch_shapes=[pltpu.VMEM((nb, blk, blk), F32), pltpu.VMEM((nb, blk, blk), F32)],
        compiler_params=_params(("parallel", "arbitrary")),
    )(q, k, v, do, dk0, dv0)


PACK_COLS = 1024
ANY = pl.BlockSpec(memory_space=pl.ANY)


def _mesh_pos():
    return lax.axis_index("x"), lax.axis_index("y"), lax.axis_index("c")


def _other_chips(x, y):
    return [(1 - x, y), (x, 1 - y), (1 - x, 1 - y)]


def all_gather_chips(shard, *, name):
    r, w = shard.shape
    rh = r // 2

    def body(s_ref, o_ref, send_sems, recv_sems, local_sem):
        x, y, c = _mesh_pos()
        me = 2 * x + y
        chips = _other_chips(x, y)
        half = lambda cc: pl.ds(cc * rh, rh)

        def copy(k, chip, hf, to, src=None):
            dst = o_ref.at[chip, half(hf)]
            return pltpu.make_async_remote_copy(src_ref=dst if src is None else src, dst_ref=dst,
                                                send_sem=send_sems.at[k], recv_sem=recv_sems.at[k],
                                                device_id=to, device_id_type=MESH)

        own = pltpu.make_async_copy(s_ref, o_ref.at[me], local_sem)
        own.start()
        sent = [copy(k, me, c, (cx, cy, c), src=s_ref.at[half(c)]) for k, (cx, cy) in enumerate(chips)]
        for cp in sent:
            cp.start()
        for k, (cx, cy) in enumerate(chips):
            copy(k, 2 * cx + cy, c, (x, y, c)).wait_recv()
            fwd = copy(3 + k, 2 * cx + cy, c, (x, y, 1 - c))
            fwd.start()
            sent.append(fwd)
        for k, (cx, cy) in enumerate(chips):
            copy(3 + k, 2 * cx + cy, 1 - c, (x, y, c)).wait_recv()
        for cp in sent:
            cp.wait_send()
        own.wait()

    return pl.pallas_call(
        body, name=name, in_specs=[ANY], out_specs=ANY,
        out_shape=jax.ShapeDtypeStruct((N_CHIPS, r, w), shard.dtype),
        scratch_shapes=[pltpu.SemaphoreType.DMA((6,)), pltpu.SemaphoreType.DMA((6,)), pltpu.SemaphoreType.DMA],
    )(shard)


def sibling_swap(g5, *, name):
    def body(g_ref, o_ref, send_sem, recv_sem):
        x, y, c = _mesh_pos()
        cp = pltpu.make_async_remote_copy(src_ref=g_ref.at[1 - c], dst_ref=o_ref, send_sem=send_sem, recv_sem=recv_sem,
                                          device_id=(x, y, 1 - c), device_id_type=MESH)
        cp.start()
        cp.wait()

    return pl.pallas_call(
        body, name=name, in_specs=[ANY], out_specs=ANY, out_shape=jax.ShapeDtypeStruct(g5.shape[1:], g5.dtype),
        scratch_shapes=[pltpu.SemaphoreType.DMA, pltpu.SemaphoreType.DMA],
    )(g5)


def chip_exchange(s1, *, name):
    _, rh, w = s1.shape

    def body(s_ref, o_ref, send_sems, recv_sems):
        x, y, c = _mesh_pos()
        cps = [pltpu.make_async_remote_copy(src_ref=s_ref.at[2 * cx + cy], dst_ref=o_ref.at[k], send_sem=send_sems.at[k],
                                            recv_sem=recv_sems.at[k], device_id=(cx, cy, c), device_id_type=MESH)
               for k, (cx, cy) in enumerate(_other_chips(x, y))]
        for cp in cps:
            cp.start()
        for cp in cps:
            cp.wait()

    return pl.pallas_call(
        body, name=name, in_specs=[ANY], out_specs=ANY, out_shape=jax.ShapeDtypeStruct((3, rh, w), s1.dtype),
        scratch_shapes=[pltpu.SemaphoreType.DMA((3,)), pltpu.SemaphoreType.DMA((3,))],
    )(s1)


def sibling_merge(s2, *, name):
    def body(s_ref, o_ref, send_sem, recv_sem, local_sem):
        x, y, c = _mesh_pos()
        own = pltpu.make_async_copy(s_ref, o_ref.at[c], local_sem)
        own.start()
        cp = pltpu.make_async_remote_copy(src_ref=s_ref, dst_ref=o_ref.at[c], send_sem=send_sem, recv_sem=recv_sem,
                                          device_id=(x, y, 1 - c), device_id_type=MESH)
        cp.start()
        cp.wait()
        own.wait()

    return pl.pallas_call(
        body, name=name, in_specs=[ANY], out_specs=ANY, out_shape=jax.ShapeDtypeStruct((2,) + s2.shape, s2.dtype),
        scratch_shapes=[pltpu.SemaphoreType.DMA, pltpu.SemaphoreType.DMA, pltpu.SemaphoreType.DMA],
    )(s2)


def all_reduce_small(buf, *, name):
    n_dev = 8

    def body(b_ref, o_ref, recv_buf, send_sems, recv_sems):
        x, y, c = _mesh_pos()
        me = 4 * x + 2 * y + c
        pos = lambda t: (t // 4, (t // 2) % 2, t % 2)

        def copy(t, slot):
            return pltpu.make_async_remote_copy(src_ref=b_ref, dst_ref=recv_buf.at[slot], send_sem=send_sems.at[t],
                                                recv_sem=recv_sems.at[slot], device_id=pos(t), device_id_type=MESH)

        for t in range(n_dev):
            @pl.when(t != me)
            def _(t=t):
                copy(t, me).start()

        recv_buf[me] = b_ref[...]
        for t in range(n_dev):
            @pl.when(t != me)
            def _(t=t):
                copy(t, t).wait_recv()
                copy(t, me).wait_send()

        acc = recv_buf[0]
        for t in range(1, n_dev):
            acc = acc + recv_buf[t]
        o_ref[...] = acc

    return pl.pallas_call(
        body, name=name, out_shape=jax.ShapeDtypeStruct(buf.shape, F32),
        in_specs=[pl.BlockSpec(memory_space=pltpu.VMEM)], out_specs=pl.BlockSpec(memory_space=pltpu.VMEM),
        scratch_shapes=[pltpu.VMEM((n_dev,) + buf.shape, F32), pltpu.SemaphoreType.DMA((n_dev,)),
                        pltpu.SemaphoreType.DMA((n_dev,))],
    )(buf)


REDUCE_ROWS = 512


def add_selected(sel, a5, b, *, name):
    _, n, rh, w = a5.shape
    tr = REDUCE_ROWS

    def body(sel_ref, a_ref, b_ref, o_ref, ob_ref):
        s = a_ref[...] + b_ref[...]
        o_ref[...] = s
        ob_ref[...] = s.astype(BF16)

    blk = pl.BlockSpec((None, tr, w), lambda j, i, s: (j, i, 0))
    return pl.pallas_call(
        body, name=name,
        grid_spec=pltpu.PrefetchScalarGridSpec(
            num_scalar_prefetch=1, grid=(n, rh // tr),
            in_specs=[pl.BlockSpec((None, None, tr, w), lambda j, i, s: (s[0], j, i, 0)), blk], out_specs=[blk, blk]),
        out_shape=[jax.ShapeDtypeStruct((n, rh, w), F32), jax.ShapeDtypeStruct((n, rh, w), BF16)],
        compiler_params=_params(("arbitrary", "arbitrary")),
    )(sel, a5, b)


def add_chip_sums(sel, s1, b2, *, name):
    _, rh, w = s1.shape
    tr = REDUCE_ROWS

    def body(sel_ref, s_ref, b_ref, o_ref):
        o_ref[...] = ((s_ref[...] + b_ref[0].astype(F32)) + b_ref[1].astype(F32)) + b_ref[2].astype(F32)

    return pl.pallas_call(
        body, name=name,
        grid_spec=pltpu.PrefetchScalarGridSpec(
            num_scalar_prefetch=1, grid=(rh // tr,),
            in_specs=[pl.BlockSpec((None, tr, w), lambda i, s: (s[0], i, 0)), pl.BlockSpec((3, tr, w), lambda i, s: (0, i, 0))],
            out_specs=pl.BlockSpec((tr, w), lambda i, s: (i, 0))),
        out_shape=jax.ShapeDtypeStruct((rh, w), F32),
        compiler_params=_params(("arbitrary",)),
    )(sel, s1, b2)


BIG = (("gdn_w_in", 2), ("gdn_w_out", 1), ("w_kv", 1), ("sb_w_q", 1), ("sb_w_out", 1), ("ffn_w_in", 2),
       ("ffn_w_out", 1), ("ple_w_proj", 2), ("ple_w_gate", 1))
SMALL = ("ln_mix", "ln_ffn", "ln_ple", "gdn_a_log", "gdn_dt_bias", "gdn_norm", "kv_norm", "k_norm", "sb_q_norm")
WEIGHTS = ("ln_mix", "ln_ffn", "ln_ple", "gdn_w_in", "gdn_conv", "gdn_a_log", "gdn_dt_bias", "gdn_norm", "gdn_w_out",
           "kv_norm", "w_kv", "k_norm", "sb_w_q", "sb_q_norm", "sb_w_out", "ffn_w_in", "ffn_w_out", "ple_w_proj",
           "ple_w_gate")
PACK_ALIGN = 2 * 1024


def _rows_of(shape):
    return -(-math.prod(shape) // PACK_COLS)


def _pack_rows(arrs, lead):
    parts = []
    for a in arrs:
        flat = a.reshape(lead + (-1,))
        pad = _rows_of(a.shape[len(lead):]) * PACK_COLS - flat.shape[-1]
        if pad:
            flat = jnp.pad(flat, [(0, 0)] * len(lead) + [(0, pad)])
        parts.append(flat.reshape(lead + (-1, PACK_COLS)))
    out = jnp.concatenate(parts, axis=len(lead))
    pad = -out.shape[len(lead)] % PACK_ALIGN
    return jnp.pad(out, [(0, 0)] * len(lead) + [(0, pad), (0, 0)])


def _unpack_rows(buf, shapes, lead):
    out, r0 = [], 0
    for s in shapes:
        rows = _rows_of(s)
        flat = buf[(slice(None),) * len(lead) + (slice(r0, r0 + rows),)].reshape(lead + (-1,))
        out.append(flat[..., :math.prod(s)].reshape(lead + tuple(s)))
        r0 += rows
    return out


def _unshard(g, axis):
    g = jnp.moveaxis(g, 0, axis)
    s = g.shape
    return g.reshape(s[:axis] + (s[axis] * s[axis + 1],) + s[axis + 2:])


def _shard(full, axis):
    s = full.shape
    return jnp.moveaxis(full.reshape(s[:axis] + (N_CHIPS, s[axis] // N_CHIPS) + s[axis + 1:]), axis, 0)


def _to4(a):
    return a.reshape(HEADS, -1, 1, CHUNK).transpose(1, 0, 2, 3)


def _from4(a):
    return a.transpose(1, 0, 2, 3).reshape(HEADS, -1)


def _row(vec):
    flat = vec.reshape(-1)
    rows = _rows_of(flat.shape)
    return jnp.pad(flat, (0, rows * PACK_COLS - flat.shape[0])).reshape(rows, PACK_COLS)


def kernel(x, p, ln_mix, ln_ffn, ln_ple, gdn_w_in, gdn_conv, gdn_a_log, gdn_dt_bias, gdn_norm, gdn_w_out, kv_norm, w_kv, k_norm, sb_w_q, sb_q_norm, sb_w_out, ffn_w_in, ffn_w_out, ple_w_proj, ple_w_gate, loss_target, m_ln_mix, m_ln_ffn, m_ln_ple, m_gdn_w_in, m_gdn_conv, m_gdn_a_log, m_gdn_dt_bias, m_gdn_norm, m_gdn_w_out, m_kv_norm, m_w_kv, m_k_norm, m_sb_w_q, m_sb_q_norm, m_sb_w_out, m_ffn_w_in, m_ffn_w_out, m_ple_w_proj, m_ple_w_gate, v_ln_mix, v_ln_ffn, v_ln_ple, v_gdn_w_in, v_gdn_conv, v_gdn_a_log, v_gdn_dt_bias, v_gdn_norm, v_gdn_w_out, v_kv_norm, v_w_kv, v_k_norm, v_sb_w_q, v_sb_q_norm, v_sb_w_out, v_ffn_w_in, v_ffn_w_out, v_ple_w_proj, v_ple_w_gate):
    w = dict(ln_mix=ln_mix, ln_ffn=ln_ffn, ln_ple=ln_ple, gdn_w_in=gdn_w_in, gdn_conv=gdn_conv, gdn_a_log=gdn_a_log,
             gdn_dt_bias=gdn_dt_bias, gdn_norm=gdn_norm, gdn_w_out=gdn_w_out, kv_norm=kv_norm, w_kv=w_kv, k_norm=k_norm,
             sb_w_q=sb_w_q, sb_q_norm=sb_q_norm, sb_w_out=sb_w_out, ffn_w_in=ffn_w_in, ffn_w_out=ffn_w_out,
             ple_w_proj=ple_w_proj, ple_w_gate=ple_w_gate)
    mom1 = dict(ln_mix=m_ln_mix, ln_ffn=m_ln_ffn, ln_ple=m_ln_ple, gdn_w_in=m_gdn_w_in, gdn_conv=m_gdn_conv,
                gdn_a_log=m_gdn_a_log, gdn_dt_bias=m_gdn_dt_bias, gdn_norm=m_gdn_norm, gdn_w_out=m_gdn_w_out,
                kv_norm=m_kv_norm, w_kv=m_w_kv, k_norm=m_k_norm, sb_w_q=m_sb_w_q, sb_q_norm=m_sb_q_norm,
                sb_w_out=m_sb_w_out, ffn_w_in=m_ffn_w_in, ffn_w_out=m_ffn_w_out, ple_w_proj=m_ple_w_proj,
                ple_w_gate=m_ple_w_gate)
    mom2 = dict(ln_mix=v_ln_mix, ln_ffn=v_ln_ffn, ln_ple=v_ln_ple, gdn_w_in=v_gdn_w_in, gdn_conv=v_gdn_conv,
                gdn_a_log=v_gdn_a_log, gdn_dt_bias=v_gdn_dt_bias, gdn_norm=v_gdn_norm, gdn_w_out=v_gdn_w_out,
                kv_norm=v_kv_norm, w_kv=v_w_kv, k_norm=v_k_norm, sb_w_q=v_sb_w_q, sb_q_norm=v_sb_q_norm,
                sb_w_out=v_sb_w_out, ffn_w_in=v_ffn_w_in, ffn_w_out=v_ffn_w_out, ple_w_proj=v_ple_w_proj,
                ple_w_gate=v_ple_w_gate)
    depth = ln_mix.shape[0]
    n_a = gdn_w_in.shape[0]
    xi, yi, ci = _mesh_pos()
    chip = 2 * xi + yi
    sel_c = jnp.reshape(ci, (1,)).astype(jnp.int32)
    sel_chip = jnp.reshape(chip, (1,)).astype(jnp.int32)
    h = x[0]
    tgt = loss_target[0]
    t = h.shape[0]

    shard_shapes = [w[n].shape for n, _ in BIG]
    packed = _pack_rows([w[n].astype(BF16) for n, _ in BIG], ())
    gathered = all_gather_chips(packed, name="all_gather_weights")
    full = {n: _unshard(g, ax) for (n, ax), g in
            zip(BIG, _unpack_rows(gathered, shard_shapes, (N_CHIPS,)))}
    conv_rows = _rows_of(gdn_conv.shape)
    small_rows = sum(_rows_of(w[n].shape) for n in SMALL)
    buf_rows = -(-(small_rows + N_CHIPS * conv_rows) // 8) * 8
    conv_buf = jnp.zeros((buf_rows, PACK_COLS), F32)
    conv_buf = lax.dynamic_update_slice(conv_buf, _row(gdn_conv) * (ci == 0).astype(F32), (chip * conv_rows, 0))
    conv_all = all_reduce_small(conv_buf, name="all_reduce_small")[:N_CHIPS * conv_rows]
    conv_full = _unshard(conv_all.reshape(N_CHIPS, -1)[:, :math.prod(gdn_conv.shape)].reshape((N_CHIPS,) + gdn_conv.shape), 2)

    saved = []
    k_sh = v_sh = None
    mid = None
    for i in range(depth):
        s = dict(h0=h)
        s["hn"] = hn = rms_fwd(h, ln_mix[i:i + 1], name="rms_fwd")
        if i < n_a:
            w_in = full["gdn_w_in"][i]
            s["w_m"], s["w_abt"] = w_in[:, :4 * WIDTH], w_in[:, 4 * WIDTH:].T
            s["proj"] = proj = matmul(hn, s["w_m"], "nn", name="mm_gdn_in")
            s["ab"] = ab = matmul(s["w_abt"], hn, "nt", name="mm_gdn_ab")
            s["a_log"], s["dt"] = gdn_a_log[i][:, None], gdn_dt_bias[i][:, None]
            g8, b8 = gates_fwd(ab, s["a_log"], s["dt"], name="gates_fwd")
            s["g4"], s["b4"] = _to4(g8), _to4(b8)
            s["qkv"] = qkv = conv_fwd(proj, conv_full[i], name="conv_fwd")
            s["o"], s["s0"], s["tinv"] = gdn_fwd(qkv, s["g4"], s["b4"], name="gdn_fwd")
            s["y"] = y = gatenorm_fwd(s["o"], proj, gdn_norm[i:i + 1], name="gatenorm_fwd")
            h = matmul(y, full["gdn_w_out"][i], "nn", add=h, name="mm_out")
        else:
            j = i - n_a
            s["qraw"] = qraw = matmul(hn, full["sb_w_q"][j], "nn", name="mm_sq")
            s["q"] = q = headnorm_fwd(qraw, 0, sb_q_norm[j:j + 1], HEAD_DIM ** -0.5, name="headnorm_q")
            s["o"] = o = sb_fwd(q, k_sh, v_sh, name="sb_fwd")
            h = matmul(o, full["sb_w_out"][j], "nn", add=h, name="mm_out")
        s["h1"] = h
        s["hn2"] = hn2 = rms_fwd(h, ln_ffn[i:i + 1], name="rms_fwd")
        s["gu"] = gu = matmul(hn2, full["ffn_w_in"][i], "nn", name="mm_ffn_in")
        s["act"] = act = swiglu_fwd(gu, name="swiglu_fwd")
        h = matmul(act, full["ffn_w_out"][i], "nn", add=h, name="mm_ffn_out")
        s["h2"] = h
        s["hn3"] = hn3 = rms_fwd(h, ln_ple[i:i + 1], name="rms_fwd")
        s["gt"] = gt = matmul(hn3, full["ple_w_gate"][i], "nn", name="mm_sq")
        s["pp"] = pp = matmul(p[i, 0], full["ple_w_proj"][i], "nn", name="mm_ple_proj")
        h = ple_fwd(h, pp, gt, name="ple_fwd")
        saved.append(s)
        if i == n_a - 1:
            mid = dict(h=h)
            mid["hk"] = hk = rms_fwd(h, kv_norm[None, :], name="rms_fwd")
            mid["kv"] = kv = matmul(hk, full["w_kv"], "nn", name="mm_kv")
            k_sh = headnorm_fwd(kv, 0, k_norm[None, :], 1.0, name="headnorm_k")
            v_sh = kv[:, WIDTH:].astype(BF16)

    dh, sq = loss_head(h, tgt, name="loss_head")
    loss = lax.psum(0.5 * jnp.sum(sq) / h.shape[1], ("x", "y", "c"))

    gw = {n: [None] * w[n].shape[0] for n in WEIGHTS if w[n].ndim >= 2 and n not in ("w_kv",)}
    dk_acc = jnp.zeros((t, WIDTH), F32)
    dv_acc = jnp.zeros((t, WIDTH), F32)
    for i in reversed(range(depth)):
        s = saved[i]
        if i == n_a - 1:
            dkraw, gw["k_norm"] = headnorm_bwd(mid["kv"], 0, k_norm[None, :], 1.0, dk_acc, name="headnorm_k_bwd")
            dkv = jnp.concatenate([dkraw, dv_acc.astype(BF16)], axis=1)
            dhk = matmul(dkv, full["w_kv"], "nt", name="mm_kv_dx")
            gw["w_kv"] = matmul(mid["hk"], dkv, "tn", name="mm_kv_dw")
            dh, gw["kv_norm"] = rms_bwd(mid["h"], kv_norm[None, :], dhk, dh, name="rms_bwd")
        dpp, dgt = ple_bwd(dh, s["pp"], s["gt"], name="ple_bwd")
        gw["ple_w_proj"][i] = matmul(p[i, 0], dpp, "tn", name="mm_ple_proj_dw")
        gw["ple_w_gate"][i] = matmul(s["hn3"], dgt, "tn", name="mm_sq_dw")
        dhn3 = matmul(dgt, full["ple_w_gate"][i], "nt", name="mm_sq_dx")
        dh, gw["ln_ple"][i] = rms_bwd(s["h2"], ln_ple[i:i + 1], dhn3, dh, name="rms_bwd")
        dact = matmul(dh, full["ffn_w_out"][i], "nt", name="mm_ffn_out_dx")
        gw["ffn_w_out"][i] = matmul(s["act"], dh, "tn", name="mm_ffn_out_dw")
        dgu = swiglu_bwd(s["gu"], dact, name="swiglu_bwd")
        dhn2 = matmul(dgu, full["ffn_w_in"][i], "nt", name="mm_ffn_in_dx")
        gw["ffn_w_in"][i] = matmul(s["hn2"], dgu, "tn", name="mm_ffn_in_dw")
        dh, gw["ln_ffn"][i] = rms_bwd(s["h1"], ln_ffn[i:i + 1], dhn2, dh, name="rms_bwd")
        if i < n_a:
            dy = matmul(dh, full["gdn_w_out"][i], "nt", name="mm_sq_dx")
            gw["gdn_w_out"][i] = matmul(s["y"], dh, "tn", name="mm_sq_dw")
            do, dproj, gw["gdn_norm"][i] = gatenorm_bwd(s["o"], s["proj"], gdn_norm[i:i + 1], dy, name="gatenorm_bwd")
            dqkv, dg4, db4 = gdn_bwd(s["qkv"], s["g4"], s["b4"], s["s0"], s["tinv"], do, name="gdn_bwd")
            dab, dal, ddt = gates_bwd(s["ab"], s["a_log"], s["dt"], _from4(dg4), _from4(db4), name="gates_bwd")
            gw["gdn_a_log"][i], gw["gdn_dt_bias"][i] = dal[:, 0], ddt[:, 0]
            dproj, gw["gdn_conv"][i] = conv_bwd(s["proj"], conv_full[i], dqkv, dproj, name="conv_bwd")
            dhn = matmul(dproj, s["w_m"], "nt", name="mm_gdn_in_dx")
            dhn = matmul(dab, s["w_abt"], "tn", add=dhn, name="mm_gdn_ab_dx")
            dwm = matmul(s["hn"], dproj, "tn", name="mm_gdn_in_dw")
            dwab = matmul(dab, s["hn"], "nn", name="mm_gdn_ab_dw")
            gw["gdn_w_in"][i] = jnp.concatenate([dwm, dwab.T], axis=1)
        else:
            j = i - n_a
            do = matmul(dh, full["sb_w_out"][j], "nt", out_dtype=BF16, name="mm_sb_out_dx")
            gw["sb_w_out"][j] = matmul(s["o"], dh, "tn", name="mm_sq_dw")
            dq, dk_acc, dv_acc = sb_bwd(s["q"], k_sh, v_sh, do, dk_acc, dv_acc, name="sb_bwd")
            dqraw, gw["sb_q_norm"][j] = headnorm_bwd(s["qraw"], 0, sb_q_norm[j:j + 1], HEAD_DIM ** -0.5, dq,
                                                    name="headnorm_q_bwd")
            dhn = matmul(dqraw, full["sb_w_q"][j], "nt", name="mm_sq_dx")
            gw["sb_w_q"][j] = matmul(s["hn"], dqraw, "tn", name="mm_sq_dw")
        dh, gw["ln_mix"][i] = rms_bwd(s["h0"], ln_mix[i:i + 1], dhn, dh, name="rms_bwd")
    grad_x = dh[None]

    def stacked(n):
        g = gw[n]
        if isinstance(g, list):
            g = jnp.stack([a.reshape(w[n].shape[1:]) if n in SMALL else a for a in g])
        return g

    small_buf = jnp.concatenate([_row(stacked(n)) for n in SMALL] + [_row(stacked("gdn_conv"))], axis=0)
    small_buf = jnp.pad(small_buf, ((0, buf_rows - small_buf.shape[0]), (0, 0)))
    small_sum = all_reduce_small(small_buf, name="all_reduce_small")
    grads = {}
    r0 = 0
    for n in SMALL:
        rows = _rows_of(w[n].shape)
        grads[n] = small_sum[r0:r0 + rows].reshape(-1)[:math.prod(w[n].shape)].reshape(w[n].shape)
        r0 += rows
    conv_g = small_sum[r0:r0 + N_CHIPS * conv_rows].reshape(-1)[:N_CHIPS * math.prod(gdn_conv.shape)]
    conv_g = conv_g.reshape((gdn_conv.shape[0], CONV_WIDTH, N_CHIPS, gdn_conv.shape[2]))
    grads["gdn_conv"] = lax.dynamic_index_in_dim(conv_g, chip, axis=2, keepdims=False)

    g_packed = _pack_rows([_shard(stacked(n), ax) for n, ax in BIG], (N_CHIPS,))
    rh = g_packed.shape[1] // 2
    g5 = g_packed.reshape(N_CHIPS, 2, rh, PACK_COLS).transpose(1, 0, 2, 3)
    from_sibling = sibling_swap(g5, name="sibling_swap")
    s1, s1b = add_selected(sel_c, g5, from_sibling, name="add_selected")
    from_chips = chip_exchange(s1b, name="chip_exchange")
    s2 = add_chip_sums(sel_chip, s1, from_chips, name="add_chip_sums")
    reduced = sibling_merge(s2, name="sibling_merge").reshape(2 * rh, PACK_COLS)
    for (n, _), g in zip(BIG, _unpack_rows(reduced, shard_shapes, ())):
        grads[n] = g

    delta, new_m, new_v = {}, {}, {}
    for n in WEIGHTS:
        delta[n], new_m[n], new_v[n] = adamw(w[n], grads[n], mom1[n], mom2[n], name="adamw")
    return (loss, grad_x, *[grads[n] for n in WEIGHTS], *[delta[n] for n in WEIGHTS],
            *[new_m[n] for n in WEIGHTS], *[new_v[n] for n in WEIGHTS])
```

```python
import functools
import math

import jax
import jax.numpy as jnp
from jax import lax
from jax.experimental import pallas as pl
from jax.experimental.pallas import tpu as pltpu

F32 = jnp.float32
BF16 = jnp.bfloat16
EPS = 1e-6
HEADS = 8
HEAD_DIM = 128
WIDTH = HEADS * HEAD_DIM
CHUNK = 64
CONV_WIDTH = 4
N_CHIPS = 4
ADAM_LR, ADAM_B1, ADAM_B2, ADAM_EPS, ADAM_WD, ADAM_STEP = 0.001, 0.9, 0.999, 1e-08, 0.01, 10
V7X_VMEM_BYTES = 64 * 1024 * 1024
VMEM_LIMIT = V7X_VMEM_BYTES - 8 * 1024 * 1024
HIGHEST = lax.Precision.HIGHEST
MESH = pl.DeviceIdType.MESH


def _params(sem=None):
    return pltpu.CompilerParams(dimension_semantics=sem, vmem_limit_bytes=VMEM_LIMIT)


def _pick(n, prefs):
    for t in prefs:
        if t <= n and n % t == 0:
            return t
    return n


def _bdot(a, b, dims):
    return lax.dot_general(a.astype(BF16), b.astype(BF16), (((dims[0],), (dims[1],)), ((), ())),
                           preferred_element_type=F32)


NN, NT, TN = (1, 0), (1, 1), (0, 0)


def matmul(a, b, form, *, out_dtype=F32, add=None, name):
    if form == "nn":
        (m, k), (k2, n) = a.shape, b.shape
    elif form == "nt":
        (m, k), (n, k2) = a.shape, b.shape
    else:
        (k, m), (k2, n) = a.shape, b.shape
    assert k == k2, (a.shape, b.shape, form)
    tm = _pick(m, (512, 256, 128))
    tn = _pick(n, (512, 256, 128))
    tk = k if k <= 1024 else _pick(k, (1024, 1408, 512, 256, 128))
    nk = k // tk
    if form == "tn":
        a_spec = pl.BlockSpec((tk, tm), lambda i, j, kk: (kk, i))
    else:
        a_spec = pl.BlockSpec((tm, tk), lambda i, j, kk: (i, kk))
    if form == "nt":
        b_spec = pl.BlockSpec((tn, tk), lambda i, j, kk: (j, kk))
    else:
        b_spec = pl.BlockSpec((tk, tn), lambda i, j, kk: (kk, j))
    o_spec = pl.BlockSpec((tm, tn), lambda i, j, kk: (i, j))
    dims = {"nn": NN, "nt": NT, "tn": TN}[form]
    has_add = add is not None

    def body(*refs):
        if has_add:
            a_ref, b_ref, add_ref, o_ref, acc_ref = refs
        else:
            a_ref, b_ref, o_ref, acc_ref = refs
        kk = pl.program_id(2)

        @pl.when(kk == 0)
        def _():
            acc_ref[...] = jnp.zeros_like(acc_ref)

        acc_ref[...] += _bdot(a_ref[...], b_ref[...], dims)

        @pl.when(kk == nk - 1)
        def _():
            r = acc_ref[...]
            if has_add:
                r = r + add_ref[...].astype(F32)
            o_ref[...] = r.astype(out_dtype)

    in_specs = [a_spec, b_spec] + ([o_spec] if has_add else [])
    args = (a, b) + ((add,) if has_add else ())
    return pl.pallas_call(
        body, name=name, grid=(m // tm, n // tn, nk), in_specs=in_specs, out_specs=o_spec,
        out_shape=jax.ShapeDtypeStruct((m, n), out_dtype),
        scratch_shapes=[pltpu.VMEM((tm, tn), F32)],
        compiler_params=_params(("parallel", "parallel", "arbitrary")),
    )(*args)


def _const(c):
    return lambda j: c


def rowwise(fn, rows, params, outs, accs=(), *, name, tm, ncol=1):
    t = rows[0][0].shape[0]
    tm = min(tm, t)
    assert t % tm == 0
    n_rows, n_par, n_out, n_acc = len(rows), len(params), len(outs), len(accs)

    def body(*refs):
        j, i = pl.program_id(0), pl.program_id(1)
        ins = [r[...] for r in refs[:n_rows + n_par]]
        o_refs = refs[n_rows + n_par:n_rows + n_par + n_out]
        a_refs = refs[n_rows + n_par + n_out:]
        row_outs, acc_outs = fn(*ins)
        for r, val in zip(o_refs, row_outs):
            r[...] = val.astype(r.dtype)
        for r, val, spec in zip(a_refs, acc_outs, accs):
            first = (i == 0) & (j == 0) if spec[4] else (i == 0)

            @pl.when(first)
            def _(r=r, val=val):
                r[...] = val.astype(F32)

            @pl.when(jnp.logical_not(first))
            def _(r=r, val=val):
                r[...] += val.astype(F32)

    in_specs = [pl.BlockSpec((tm, w), lambda j, i, cf=cf: (i, cf(j))) for _, w, cf in rows]
    in_specs += [pl.BlockSpec((p.shape[0], w), lambda j, i, cf=cf: (0, cf(j))) for p, w, cf in params]
    out_specs = [pl.BlockSpec((tm, w), lambda j, i, cf=cf: (i, cf(j))) for _, _, w, cf in outs]
    out_specs += [pl.BlockSpec((r, w), lambda j, i, cf=cf: (0, cf(j))) for r, _, w, cf, _ in accs]
    out_shape = [jax.ShapeDtypeStruct((t, tw), dt) for tw, dt, _, _ in outs]
    out_shape += [jax.ShapeDtypeStruct((r, tw), F32) for r, tw, _, _, _ in accs]
    res = pl.pallas_call(
        body, name=name, grid=(ncol, t // tm), in_specs=in_specs, out_specs=out_specs, out_shape=out_shape,
        compiler_params=_params(("arbitrary", "arbitrary")),
    )(*[r[0] for r in rows], *[p[0] for p in params])
    return res[:n_out], res[n_out:]


def _full(arr):
    return (arr, arr.shape[1], _const(0))


def _rms(x, g):
    x = x.astype(F32)
    return x * lax.rsqrt(jnp.mean(x * x, axis=-1, keepdims=True) + EPS) * g.astype(F32)


def _sigmoid(x):
    return 1.0 / (1.0 + jnp.exp(-x))


def _silu(x):
    return x * _sigmoid(x)


def _softplus(x):
    return jnp.maximum(x, 0.0) + jnp.log(1.0 + jnp.exp(-jnp.abs(x)))


def rms_fwd(h, g, *, name):
    d = h.shape[1]
    (hn,), _ = rowwise(lambda x, gg: ((_rms(x, gg),), ()), [_full(h)], [_full(g)],
                       [(d, BF16, d, _const(0))], name=name, tm=512)
    return hn


def rms_bwd(h, g, dhn, dh_res, *, name):
    d = h.shape[1]

    def fn(x, ct, res, gg):
        _, vjp = jax.vjp(_rms, x.astype(F32), gg.astype(F32))
        dx, dg = vjp(ct.astype(F32))
        return (res.astype(F32) + dx,), (dg,)

    (dh,), (dg,) = rowwise(fn, [_full(h), _full(dhn), _full(dh_res)], [_full(g)],
                           [(d, F32, d, _const(0))], [(1, d, d, _const(0), True)], name=name, tm=256)
    return dh, dg


def _head_rms(x, g, scale):
    x = x.astype(F32)
    return x * lax.rsqrt(jnp.mean(x * x, axis=-1, keepdims=True) + EPS) * (g.astype(F32) * scale)


def headnorm_fwd(x, col0, g, scale, *, name):
    (y,), _ = rowwise(lambda a, gg: ((_head_rms(a, gg, scale),), ()),
                      [(x, HEAD_DIM, lambda j: col0 + j)], [_full(g)],
                      [(WIDTH, BF16, HEAD_DIM, lambda j: j)], name=name, tm=1024, ncol=HEADS)
    return y


def headnorm_bwd(x, col0, g, scale, dy, *, name, out_dtype=BF16):
    def fn(a, ct, gg):
        _, vjp = jax.vjp(lambda a_, g_: _head_rms(a_, g_, scale), a.astype(F32), gg.astype(F32))
        dx, dg = vjp(ct.astype(F32))
        return (dx,), (dg,)

    (dx,), (dg,) = rowwise(fn, [(x, HEAD_DIM, lambda j: col0 + j), (dy, HEAD_DIM, lambda j: j)], [_full(g)],
                           [(WIDTH, out_dtype, HEAD_DIM, lambda j: j)],
                           [(1, HEAD_DIM, HEAD_DIM, _const(0), True)], name=name, tm=1024, ncol=HEADS)
    return dx, dg


def _gatenorm(o, gate, g):
    return _head_rms(o, g, 1.0) * _silu(gate.astype(F32))


def gatenorm_fwd(o, proj, g, *, name):
    (y,), _ = rowwise(lambda a, gt, gg: ((_gatenorm(a, gt, gg),), ()),
                      [(o, HEAD_DIM, lambda j: j), (proj, HEAD_DIM, lambda j: 3 * HEADS + j)], [_full(g)],
                      [(WIDTH, BF16, HEAD_DIM, lambda j: j)], name=name, tm=1024, ncol=HEADS)
    return y


def gatenorm_bwd(o, proj, g, dy, *, name):
    def fn(a, gt, ct, gg):
        _, vjp = jax.vjp(_gatenorm, a.astype(F32), gt.astype(F32), gg.astype(F32))
        da, dgt, dg = vjp(ct.astype(F32))
        return (da, dgt), (dg,)

    (do, dproj), (dg,) = rowwise(
        fn, [(o, HEAD_DIM, lambda j: j), (proj, HEAD_DIM, lambda j: 3 * HEADS + j), (dy, HEAD_DIM, lambda j: j)],
        [_full(g)],
        [(WIDTH, F32, HEAD_DIM, lambda j: j), (4 * WIDTH, BF16, HEAD_DIM, lambda j: 3 * HEADS + j)],
        [(1, HEAD_DIM, HEAD_DIM, _const(0), True)], name=name, tm=1024, ncol=HEADS)
    return do, dproj, dg


def _swiglu(g, u):
    return _silu(g.astype(F32)) * u.astype(F32)


def swiglu_fwd(gu, *, name):
    f = gu.shape[1] // 2
    (act,), _ = rowwise(lambda g, u: ((_swiglu(g, u),), ()), [(gu, f, _const(0)), (gu, f, _const(1))], [],
                        [(f, BF16, f, _const(0))], name=name, tm=256)
    return act


def swiglu_bwd(gu, dact, *, name):
    f = gu.shape[1] // 2

    def fn(g, u, ct):
        _, vjp = jax.vjp(_swiglu, g.astype(F32), u.astype(F32))
        dg, du = vjp(ct.astype(F32))
        return (jnp.concatenate([dg.astype(BF16), du.astype(BF16)], axis=1),), ()

    (dgu,), _ = rowwise(fn, [(gu, f, _const(0)), (gu, f, _const(1)), _full(dact)], [],
                        [(2 * f, BF16, 2 * f, _const(0))], name=name, tm=256)
    return dgu


def ple_fwd(h, pp, gt, *, name):
    d = h.shape[1]
    (out,), _ = rowwise(lambda a, b, c: ((a + b * _sigmoid(c),), ()), [_full(h), _full(pp), _full(gt)], [],
                        [(d, F32, d, _const(0))], name=name, tm=512)
    return out


def ple_bwd(dh, pp, gt, *, name):
    d = dh.shape[1]

    def fn(ct, b, c):
        s = _sigmoid(c)
        return (ct * s, ct * b * s * (1.0 - s)), ()

    (dpp, dgt), _ = rowwise(fn, [_full(dh), _full(pp), _full(gt)], [],
                            [(d, BF16, d, _const(0)), (d, BF16, d, _const(0))], name=name, tm=512)
    return dpp, dgt


def loss_head(y, tgt, *, name):
    d = y.shape[1]

    def fn(a, b):
        e = a - b
        return (e * (1.0 / d),), (jnp.sum(e * e, axis=0, keepdims=True),)

    (dy,), (sq,) = rowwise(fn, [_full(y), _full(tgt)], [], [(d, F32, d, _const(0))],
                           [(1, d, d, _const(0), True)], name=name, tm=512)
    return dy, sq


def adamw(w, g, m, v, *, name):
    shape = w.shape
    cols = shape[-1]
    flat = lambda a: a.reshape(-1, cols)
    bc1 = 1.0 - ADAM_B1 ** ADAM_STEP
    bc2 = 1.0 - ADAM_B2 ** ADAM_STEP

    def fn(w_, g_, m_, v_):
        m_ = ADAM_B1 * m_ + (1.0 - ADAM_B1) * g_
        v_ = ADAM_B2 * v_ + (1.0 - ADAM_B2) * (g_ * g_)
        delta = -ADAM_LR * ((m_ / bc1) / (jnp.sqrt(v_ / bc2) + ADAM_EPS) + ADAM_WD * w_)
        return (delta, m_, v_), ()

    o = (cols, F32, cols, _const(0))
    (d_, m_, v_), _ = rowwise(fn, [_full(flat(w)), _full(flat(g)), _full(flat(m)), _full(flat(v))], [],
                              [o, o, o], name=name, tm=256)
    return d_.reshape(shape), m_.reshape(shape), v_.reshape(shape)


CONV_STRIP = 256


def _shift_down(x, d):
    if d == 0:
        return x
    rows = lax.broadcasted_iota(jnp.int32, x.shape, 0)
    return jnp.where(rows >= d, pltpu.roll(x, d, 0), 0.0)


def _shift_up(x, d):
    if d == 0:
        return x
    t = x.shape[0]
    rows = lax.broadcasted_iota(jnp.int32, x.shape, 0)
    return jnp.where(rows < t - d, pltpu.roll(x, t - d, 0), 0.0)


def _conv(x, w):
    acc = None
    for j in range(CONV_WIDTH):
        term = _shift_down(x, CONV_WIDTH - 1 - j) * w[j:j + 1, :]
        acc = term if acc is None else acc + term
    return acc


def conv_fwd(proj, w, *, name):
    t = proj.shape[0]
    per = WIDTH // CONV_STRIP

    def body(x_ref, w_ref, o_ref):
        o_ref[0] = _silu(_conv(x_ref[...], w_ref[...]))

    return pl.pallas_call(
        body, name=name, grid=(3 * per,),
        in_specs=[pl.BlockSpec((t, CONV_STRIP), lambda j: (0, j)), pl.BlockSpec((CONV_WIDTH, CONV_STRIP), lambda j: (0, j))],
        out_specs=pl.BlockSpec((1, t, CONV_STRIP), lambda j: (j // per, 0, j % per)),
        out_shape=jax.ShapeDtypeStruct((3, t, WIDTH), F32),
        compiler_params=_params(("parallel",)),
    )(proj, w)


def conv_bwd(proj, w, dqkv, dproj, *, name):
    t = proj.shape[0]
    per = WIDTH // CONV_STRIP

    def body(x_ref, w_ref, d_ref, _, dx_ref, dw_ref):
        x, w_ = x_ref[...], w_ref[...]
        c = _conv(x, w_)
        s = _sigmoid(c)
        dc = d_ref[0] * (s + c * s * (1.0 - s))
        dx = None
        for j in range(CONV_WIDTH):
            d = CONV_WIDTH - 1 - j
            term = _shift_up(dc, d) * w_[j:j + 1, :]
            dx = term if dx is None else dx + term
            dw_ref[j:j + 1, :] = jnp.sum(dc * _shift_down(x, d), axis=0, keepdims=True)
        dx_ref[...] = dx.astype(dx_ref.dtype)

    return pl.pallas_call(
        body, name=name, grid=(3 * per,),
        in_specs=[pl.BlockSpec((t, CONV_STRIP), lambda j: (0, j)), pl.BlockSpec((CONV_WIDTH, CONV_STRIP), lambda j: (0, j)),
                  pl.BlockSpec((1, t, CONV_STRIP), lambda j: (j // per, 0, j % per)), pl.BlockSpec(memory_space=pl.ANY)],
        out_specs=[pl.BlockSpec((t, CONV_STRIP), lambda j: (0, j)), pl.BlockSpec((CONV_WIDTH, CONV_STRIP), lambda j: (0, j))],
        out_shape=[jax.ShapeDtypeStruct(dproj.shape, dproj.dtype), jax.ShapeDtypeStruct((CONV_WIDTH, 3 * WIDTH), F32)],
        input_output_aliases={3: 0},
        compiler_params=_params(("parallel",)),
    )(proj, w, dqkv, dproj)


def _gdn_gates(ab, a_log, dt_bias):
    a_in, b_in = ab[:HEADS], ab[HEADS:]
    g = -jnp.exp(a_log) * _softplus(a_in + dt_bias)
    return g, _sigmoid(b_in)


def gates_fwd(ab, a_log, dt_bias, *, name):
    t = ab.shape[1]

    def body(ab_ref, al_ref, dt_ref, g_ref, b_ref):
        g_ref[...], b_ref[...] = _gdn_gates(ab_ref[...], al_ref[...], dt_ref[...])

    s = jax.ShapeDtypeStruct((HEADS, t), F32)
    return pl.pallas_call(body, name=name, out_shape=[s, s], compiler_params=_params())(ab, a_log, dt_bias)


def gates_bwd(ab, a_log, dt_bias, dg, dbeta, *, name):
    t = ab.shape[1]

    def body(ab_ref, al_ref, dt_ref, dg_ref, db_ref, dab_ref, dal_ref, ddt_ref):
        _, vjp = jax.vjp(_gdn_gates, ab_ref[...], al_ref[...], dt_ref[...])
        dab_ref[...], dal_ref[...], ddt_ref[...] = vjp((dg_ref[...], db_ref[...]))

    c = jax.ShapeDtypeStruct((HEADS, 1), F32)
    return pl.pallas_call(body, name=name, out_shape=[jax.ShapeDtypeStruct((2 * HEADS, t), F32), c, c],
                          compiler_params=_params())(ab, a_log, dt_bias, dg, dbeta)


def _split3(x):
    hi = x.astype(BF16)
    r1 = x - hi.astype(F32)
    mid = r1.astype(BF16)
    lo = (r1 - mid.astype(F32)).astype(BF16)
    return hi, mid, lo


def _dot01(x, m01):
    hi, mid, lo = _split3(x)
    m01 = m01.astype(BF16)
    return _bdot(hi, m01, NN) + _bdot(mid, m01, NN) + _bdot(lo, m01, NN)


def _hdot(a, b, dims=NN):
    return lax.dot_general(a, b, (((dims[0],), (dims[1],)), ((), ())), precision=HIGHEST,
                           preferred_element_type=F32)


def _rowsum(x):
    return jnp.sum(x, axis=1, keepdims=True)


def _colsum(x):
    return jnp.sum(x, axis=0, keepdims=True)


class _Heads:
    def __init__(self, vals):
        self.v = list(vals)

    def _bin(self, other, f):
        if isinstance(other, _Heads):
            return _Heads(f(a, b) for a, b in zip(self.v, other.v))
        return _Heads(f(a, other) for a in self.v)

    def __add__(self, o):
        return self._bin(o, lambda a, b: a + b)

    __radd__ = __add__

    def __sub__(self, o):
        return self._bin(o, lambda a, b: a - b)

    def __rsub__(self, o):
        return self._bin(o, lambda a, b: b - a)

    def __mul__(self, o):
        return self._bin(o, lambda a, b: a * b)

    __rmul__ = __mul__

    def __neg__(self):
        return _Heads(-a for a in self.v)


def _hmap(f, *args):
    n = next(len(a.v) for a in args if isinstance(a, _Heads))
    return _Heads(f(*[a.v[h] if isinstance(a, _Heads) else a for a in args]) for h in range(n))


def _inv_unit_lower(a, eye):
    p = jnp.where(eye, 1.0, 0.0) - a
    ak = a
    for _ in range(int(math.log2(CHUNK)) - 1):
        ak = _hmap(_hdot, ak, ak)
        p = p + _hmap(_hdot, p, ak)
    return p


def _gdn_chunk(qr, kr, v, grow, brow, tinv=None):
    c = CHUNK
    ri = lax.broadcasted_iota(jnp.int32, (c, c), 0)
    ci = lax.broadcasted_iota(jnp.int32, (c, c), 1)
    eye, lower, strict = ri == ci, ri >= ci, ri > ci
    where = lambda m: (lambda a: jnp.where(m, a, 0.0))
    to_col = lambda row: _hmap(lambda r: _rowsum(jnp.where(eye, jnp.broadcast_to(r, (c, c)), 0.0)), row)
    cum_row = _hmap(lambda g: _dot01(jnp.broadcast_to(g, (8, c)), ri <= ci)[0:1], grow)
    gcol, bcol = to_col(cum_row), to_col(brow)
    glast = _hmap(lambda g: _colsum(jnp.where(ri[:, 0:1] == c - 1, g, 0.0)), gcol)
    rq = _hmap(lambda a: lax.rsqrt(_rowsum(a * a) + EPS), qr)
    rk = _hmap(lambda a: lax.rsqrt(_rowsum(a * a) + EPS), kr)
    scale = HEAD_DIM ** -0.5
    qn, kn = qr * (rq * scale), kr * rk
    dec = _hmap(lambda gc, gr: jnp.where(lower, jnp.exp(jnp.minimum(gc - gr, 0.0)), 0.0), gcol, cum_row)
    kk = _hmap(lambda a: _bdot(a, a, NT), kn)
    qk = _hmap(lambda a, b: _bdot(a, b, NT), qn, kn)
    gam_col, e_col, gam_last = _hmap(jnp.exp, gcol), _hmap(jnp.exp, glast - gcol), _hmap(jnp.exp, glast)
    if tinv is None:
        tinv = _inv_unit_lower(_hmap(where(strict), bcol * kk * dec), eye)
    u = _hmap(_hdot, tinv, v * bcol)
    w = _hmap(_hdot, tinv, kn * (bcol * gam_col))
    return dict(eye=eye, lower=lower, strict=strict, gcol=gcol, bcol=bcol, rq=rq, rk=rk, qn=qn, kn=kn,
                dec=dec, kk=kk, qk=qk, gam_col=gam_col, e_col=e_col, gam_last=gam_last, tinv=tinv, u=u, w=w,
                aqk=qk * dec, qt=qn * gam_col, kt=kn * e_col, scale=scale)


def _head_cols(h):
    return slice(h * HEAD_DIM, (h + 1) * HEAD_DIM)


def _bd(dims):
    return lambda a, b: _bdot(a, b, dims)


def gdn_fwd(qkv, g4, b4, *, name):
    t = qkv.shape[1]
    n = t // CHUNK
    d = HEAD_DIM
    heads = range(HEADS)

    def body(qkv_ref, g_ref, b_ref, o_ref, s0_ref, t_ref, s_ref):
        @pl.when(pl.program_id(0) == 0)
        def _():
            s_ref[...] = jnp.zeros_like(s_ref)

        qr, kr, v = (_Heads(qkv_ref[j, :, _head_cols(h)] for h in heads) for j in range(3))
        z = _gdn_chunk(qr, kr, v, _Heads(g_ref[0, h] for h in heads), _Heads(b_ref[0, h] for h in heads))
        s0 = _Heads(s_ref[h] for h in heads)
        v_new = z["u"] - _hmap(_bd(NN), z["w"], s0)
        o = _hmap(_bd(NN), z["qt"], s0) + _hmap(_bd(NN), z["aqk"], v_new)
        s_new = s0 * z["gam_last"] + _hmap(_bd(TN), z["kt"], v_new)
        for h in heads:
            s0_ref[0, h] = s0.v[h]
            t_ref[0, h] = z["tinv"].v[h]
            o_ref[:, _head_cols(h)] = o.v[h]
            s_ref[h] = s_new.v[h]

    gspec = pl.BlockSpec((1, HEADS, 1, CHUNK), lambda i: (i, 0, 0, 0))
    return pl.pallas_call(
        body, name=name, grid=(n,),
        in_specs=[pl.BlockSpec((3, CHUNK, WIDTH), lambda i: (0, i, 0)), gspec, gspec],
        out_specs=[pl.BlockSpec((CHUNK, WIDTH), lambda i: (i, 0)),
                   pl.BlockSpec((1, HEADS, d, d), lambda i: (i, 0, 0, 0)),
                   pl.BlockSpec((1, HEADS, CHUNK, CHUNK), lambda i: (i, 0, 0, 0))],
        out_shape=[jax.ShapeDtypeStruct((t, WIDTH), F32), jax.ShapeDtypeStruct((n, HEADS, d, d), F32),
                   jax.ShapeDtypeStruct((n, HEADS, CHUNK, CHUNK), F32)],
        scratch_shapes=[pltpu.VMEM((HEADS, d, d), F32)],
        compiler_params=_params(("arbitrary",)),
    )(qkv, g4, b4)


def gdn_bwd(qkv, g4, b4, s0_all, tinv_all, do, *, name):
    t = qkv.shape[1]
    n = t // CHUNK
    d = HEAD_DIM
    c = CHUNK
    heads = range(HEADS)

    def body(qkv_ref, g_ref, b_ref, s0_ref, t_ref, do_ref, dqkv_ref, dg_ref, db_ref, ds_ref):
        @pl.when(pl.program_id(0) == 0)
        def _():
            ds_ref[...] = jnp.zeros_like(ds_ref)

        qr, kr, v = (_Heads(qkv_ref[j, :, _head_cols(h)] for h in heads) for j in range(3))
        z = _gdn_chunk(qr, kr, v, _Heads(g_ref[0, h] for h in heads), _Heads(b_ref[0, h] for h in heads),
                       tinv=_Heads(t_ref[0, h] for h in heads))
        s0 = _Heads(s0_ref[0, h] for h in heads)
        ds = _Heads(ds_ref[h] for h in heads)
        dout = _Heads(do_ref[:, _head_cols(h)] for h in heads)
        qn, kn, u, w, dec, kk, qk = z["qn"], z["kn"], z["u"], z["w"], z["dec"], z["kk"], z["qk"]
        bcol, gam_col, e_col, gam_last = z["bcol"], z["gam_col"], z["e_col"], z["gam_last"]
        low = lambda a: jnp.where(z["lower"], a, 0.0)
        strict = lambda a: jnp.where(z["strict"], a, 0.0)
        rowsum = lambda a: _hmap(_rowsum, a)
        colsum = lambda a: _hmap(_colsum, a)
        v_new = u - _hmap(_bd(NN), w, s0)
        dv_new = _hmap(_bd(TN), z["aqk"], dout) + _hmap(_bd(NN), z["kt"], ds)
        daqk = _hmap(low, _hmap(_bd(NT), dout, v_new))
        dqt = _hmap(_bd(NT), dout, s0)
        dkt = _hmap(_bd(NT), v_new, ds)
        dgam_last = _hmap(lambda a, b: jnp.sum(a * b, keepdims=True), ds, s0)
        ds_new = _hmap(_bd(TN), z["qt"], dout) + ds * gam_last - _hmap(_bd(TN), w, dv_new)
        dw = -_hmap(_bd(NT), dv_new, s0)
        hd_t = lambda a, b: _hdot(a, b, TN)
        dru = _hmap(hd_t, z["tinv"], dv_new)
        drw = _hmap(hd_t, z["tinv"], dw)
        dal = -_hmap(strict, _hmap(_bd(NT), dru, u) + _hmap(_bd(NT), drw, w))
        t1 = dal * kk * dec
        dkk = dal * bcol * dec
        ddec = dal * bcol * kk + daqk * qk
        dqk = daqk * dec
        s_w = rowsum(drw * kn)
        dbeta_col = rowsum(t1) + rowsum(dru * v) + gam_col * s_w
        dkn = (drw * (bcol * gam_col) + _hmap(_bd(NN), dkk, kn) + _hmap(_bd(TN), dkk, kn) + _hmap(_bd(TN), dqk, qn)
               + dkt * e_col)
        dqn = _hmap(_bd(NN), dqk, kn) + dqt * gam_col
        e_mat = ddec * dec
        de_col = rowsum(dkt * kn)
        diag_of_colsum = rowsum(_hmap(lambda a: jnp.where(z["eye"], jnp.broadcast_to(_colsum(a), (c, c)), 0.0), e_mat))
        dg_cum = rowsum(e_mat) + (bcol * s_w + rowsum(dqt * qn)) * gam_col - de_col * e_col - diag_of_colsum
        dg_last = colsum(de_col * e_col) + dgam_last * gam_last
        dg = colsum(_hmap(lambda a: jnp.where(z["lower"], a, 0.0), dg_cum)) + dg_last
        dbeta = colsum(_hmap(lambda a: jnp.where(z["eye"], a, 0.0), dbeta_col))
        rq, rk = z["rq"], z["rk"]
        dqr = z["scale"] * (rq * dqn - qr * (rq * rq * rq) * rowsum(qr * dqn))
        dkr = rk * dkn - kr * (rk * rk * rk) * rowsum(kr * dkn)
        dv = dru * bcol
        for h in heads:
            ds_ref[h] = ds_new.v[h]
            dg_ref[0, h] = dg.v[h]
            db_ref[0, h] = dbeta.v[h]
            dqkv_ref[0, :, _head_cols(h)] = dqr.v[h]
            dqkv_ref[1, :, _head_cols(h)] = dkr.v[h]
            dqkv_ref[2, :, _head_cols(h)] = dv.v[h]

    rev = lambda i: n - 1 - i
    gspec = pl.BlockSpec((1, HEADS, 1, CHUNK), lambda i: (rev(i), 0, 0, 0))
    return pl.pallas_call(
        body, name=name, grid=(n,),
        in_specs=[pl.BlockSpec((3, CHUNK, WIDTH), lambda i: (0, rev(i), 0)), gspec, gspec,
                  pl.BlockSpec((1, HEADS, d, d), lambda i: (rev(i), 0, 0, 0)),
                  pl.BlockSpec((1, HEADS, CHUNK, CHUNK), lambda i: (rev(i), 0, 0, 0)),
                  pl.BlockSpec((CHUNK, WIDTH), lambda i: (rev(i), 0))],
        out_specs=[pl.BlockSpec((3, CHUNK, WIDTH), lambda i: (0, rev(i), 0)), gspec, gspec],
        out_shape=[jax.ShapeDtypeStruct((3, t, WIDTH), F32), jax.ShapeDtypeStruct((n, HEADS, 1, CHUNK), F32),
                   jax.ShapeDtypeStruct((n, HEADS, 1, CHUNK), F32)],
        scratch_shapes=[pltpu.VMEM((HEADS, d, d), F32)],
        compiler_params=_params(("arbitrary",)),
    )(qkv, g4, b4, s0_all, tinv_all, do)


SB_BLOCK = 256


def _dot01_2(x, m01):
    hi = x.astype(BF16)
    lo = (x - hi.astype(F32)).astype(BF16)
    return _bdot(hi, m01, NN) + _bdot(lo, m01, NN)


def _sb_weights(q, kb, carry, mask, upper):
    z = _bdot(q, kb, NT)
    ls = jnp.minimum(z, 0.0) - jnp.log(1.0 + jnp.exp(-jnp.abs(z)))
    ln = jnp.where(mask, ls - z, 0.0)
    a = jnp.where(mask, jnp.exp(ls + _dot01_2(ln, upper) + carry), 0.0)
    return z, ln, a


SB_DEAD = -105.0


def _sb_alive(s, i, carry):
    return (s <= i) & (jnp.max(carry) > SB_DEAD)


def _sb_masks(i, jb, blk):
    ri = lax.broadcasted_iota(jnp.int32, (blk, blk), 0)
    ci = lax.broadcasted_iota(jnp.int32, (blk, blk), 1)
    return (jb * blk + ci) < (i * blk + ri)


def sb_fwd(q, k, v, *, name):
    t = q.shape[0]
    blk = min(SB_BLOCK, t)
    d = HEAD_DIM

    def body(q_ref, k_ref, v_ref, o_ref):
        i = pl.program_id(1)
        qb = q_ref[...]
        ri = lax.broadcasted_iota(jnp.int32, (blk, blk), 0)
        ci = lax.broadcasted_iota(jnp.int32, (blk, blk), 1)
        upper = (ri > ci).astype(BF16)

        def step(state):
            s, c, acc = state
            jb = i - s
            rows = pl.ds(pl.multiple_of(jb * blk, blk), blk)
            _, ln, a = _sb_weights(qb, k_ref[rows, :], c, _sb_masks(i, jb, blk), upper)
            return s + 1, c + _rowsum(ln), acc + _bdot(a, v_ref[rows, :], NN)

        _, _, acc = lax.while_loop(lambda st: _sb_alive(st[0], i, st[1]), step,
                                   (jnp.int32(0), jnp.zeros((blk, 1), F32), jnp.zeros((blk, d), F32)))
        o_ref[...] = acc.astype(o_ref.dtype)

    qspec = pl.BlockSpec((blk, d), lambda h, i: (i, h))
    kspec = pl.BlockSpec((t, d), lambda h, i: (0, h))
    return pl.pallas_call(
        body, name=name, grid=(HEADS, t // blk), in_specs=[qspec, kspec, kspec], out_specs=qspec,
        out_shape=jax.ShapeDtypeStruct((t, WIDTH), BF16),
        compiler_params=_params(("parallel", "arbitrary")),
    )(q, k, v)


def sb_bwd(q, k, v, do, dk0, dv0, *, name):
    t = q.shape[0]
    blk = min(SB_BLOCK, t)
    d = HEAD_DIM
    nb = t // blk

    def body(q_ref, k_ref, v_ref, do_ref, dk0_ref, dv0_ref, dq_ref, dk_ref, dv_ref, p_buf, z_buf):
        i = pl.program_id(1)

        @pl.when(i == 0)
        def _():
            dk_ref[...] = dk0_ref[...]
            dv_ref[...] = dv0_ref[...]

        qb, dob = q_ref[...], do_ref[...]
        ri = lax.broadcasted_iota(jnp.int32, (blk, blk), 0)
        ci = lax.broadcasted_iota(jnp.int32, (blk, blk), 1)
        upper = (ri > ci).astype(BF16)
        lower = (ri < ci).astype(BF16)

        def right_to_left(state):
            s, c = state
            jb = i - s
            rows = pl.ds(pl.multiple_of(jb * blk, blk), blk)
            z, ln, a = _sb_weights(qb, k_ref[rows, :], c, _sb_masks(i, jb, blk), upper)
            p_buf[jb] = a * _bdot(dob, v_ref[rows, :], NT)
            z_buf[jb] = z
            dv_ref[rows, :] += _bdot(a, dob, TN)
            return s + 1, c + _rowsum(ln)

        n_done, _ = lax.while_loop(lambda st: _sb_alive(st[0], i, st[1]), right_to_left,
                                   (jnp.int32(0), jnp.zeros((blk, 1), F32)))

        def left_to_right(jb, carry):
            cp, dq = carry
            rows = pl.ds(pl.multiple_of(jb * blk, blk), blk)
            p = p_buf[jb]
            sg = _sigmoid(z_buf[jb])
            dz = jnp.where(_sb_masks(i, jb, blk), p * (1.0 - sg) - sg * (_dot01_2(p, lower) + cp), 0.0)
            dk_ref[rows, :] += _bdot(dz, qb, TN)
            return cp + _rowsum(p), dq + _bdot(dz, k_ref[rows, :], NN)

        _, dq = lax.fori_loop(i + 1 - n_done, i + 1, left_to_right,
                              (jnp.zeros((blk, 1), F32), jnp.zeros((blk, d), F32)))
        dq_ref[...] = dq

    qspec = pl.BlockSpec((blk, d), lambda h, i: (i, h))
    kspec = pl.BlockSpec((t, d), lambda h, i: (0, h))
    s = jax.ShapeDtypeStruct((t, WIDTH), F32)
    return pl.pallas_call(
        body, name=name, grid=(HEADS, nb), in_specs=[qspec, kspec, kspec, qspec, kspec, kspec],
        out_specs=[qspec, kspec, kspec], out_shape=[s, s, s],
        scratch_shapes=[pltpu.VMEM((nb, blk, blk), F32), pltpu.VMEM((nb, blk, blk), F32)],
        compiler_params=_params(("parallel", "arbitrary")),
    )(q, k, v, do, dk0, dv0)


PACK_COLS = 1024
ANY = pl.BlockSpec(memory_space=pl.ANY)


def _mesh_pos():
    return lax.axis_index("x"), lax.axis_index("y"), lax.axis_index("c")


def _other_chips(x, y):
    return [(1 - x, y), (x, 1 - y), (1 - x, 1 - y)]


def all_gather_chips(slots, *, name):
    _, r, w = slots.shape
    rh = r // 2

    def body(_, o_ref, send_sems, recv_sems):
        x, y, c = _mesh_pos()
        me = 2 * x + y
        chips = _other_chips(x, y)
        half = lambda cc: pl.ds(cc * rh, rh)

        def copy(k, chip, hf, to):
            rows = o_ref.at[chip, half(hf)]
            return pltpu.make_async_remote_copy(src_ref=rows, dst_ref=rows, send_sem=send_sems.at[k],
                                                recv_sem=recv_sems.at[k], device_id=to, device_id_type=MESH)

        sent = [copy(k, me, c, (cx, cy, c)) for k, (cx, cy) in enumerate(chips)]
        for cp in sent:
            cp.start()
        for k, (cx, cy) in enumerate(chips):
            copy(k, 2 * cx + cy, c, (x, y, c)).wait_recv()
            fwd = copy(3 + k, 2 * cx + cy, c, (x, y, 1 - c))
            fwd.start()
            sent.append(fwd)
        for k, (cx, cy) in enumerate(chips):
            copy(3 + k, 2 * cx + cy, 1 - c, (x, y, c)).wait_recv()
        for cp in sent:
            cp.wait_send()

    return pl.pallas_call(
        body, name=name, in_specs=[ANY], out_specs=ANY, input_output_aliases={0: 0},
        out_shape=jax.ShapeDtypeStruct(slots.shape, slots.dtype),
        scratch_shapes=[pltpu.SemaphoreType.DMA((6,)), pltpu.SemaphoreType.DMA((6,))],
    )(slots)


def sibling_swap(g5, *, name):
    def body(g_ref, o_ref, send_sem, recv_sem):
        x, y, c = _mesh_pos()
        cp = pltpu.make_async_remote_copy(src_ref=g_ref.at[1 - c], dst_ref=o_ref, send_sem=send_sem, recv_sem=recv_sem,
                                          device_id=(x, y, 1 - c), device_id_type=MESH)
        cp.start()
        cp.wait()

    return pl.pallas_call(
        body, name=name, in_specs=[ANY], out_specs=ANY, out_shape=jax.ShapeDtypeStruct(g5.shape[1:], g5.dtype),
        scratch_shapes=[pltpu.SemaphoreType.DMA, pltpu.SemaphoreType.DMA],
    )(g5)


def chip_exchange(s1, *, name):
    _, rh, w = s1.shape

    def body(s_ref, o_ref, send_sems, recv_sems):
        x, y, c = _mesh_pos()
        cps = [pltpu.make_async_remote_copy(src_ref=s_ref.at[2 * cx + cy], dst_ref=o_ref.at[k], send_sem=send_sems.at[k],
                                            recv_sem=recv_sems.at[k], device_id=(cx, cy, c), device_id_type=MESH)
               for k, (cx, cy) in enumerate(_other_chips(x, y))]
        for cp in cps:
            cp.start()
        for cp in cps:
            cp.wait()

    return pl.pallas_call(
        body, name=name, in_specs=[ANY], out_specs=ANY, out_shape=jax.ShapeDtypeStruct((3, rh, w), s1.dtype),
        scratch_shapes=[pltpu.SemaphoreType.DMA((3,)), pltpu.SemaphoreType.DMA((3,))],
    )(s1)


def sibling_merge(halves, *, name):
    def body(_, o_ref, send_sem, recv_sem):
        x, y, c = _mesh_pos()
        cp = pltpu.make_async_remote_copy(src_ref=o_ref.at[c], dst_ref=o_ref.at[c], send_sem=send_sem,
                                          recv_sem=recv_sem, device_id=(x, y, 1 - c), device_id_type=MESH)
        cp.start()
        cp.wait()

    return pl.pallas_call(
        body, name=name, in_specs=[ANY], out_specs=ANY, input_output_aliases={0: 0},
        out_shape=jax.ShapeDtypeStruct(halves.shape, halves.dtype),
        scratch_shapes=[pltpu.SemaphoreType.DMA, pltpu.SemaphoreType.DMA],
    )(halves)


def all_reduce_small(buf, *, name):
    n_dev = 8

    def body(b_ref, o_ref, recv_buf, send_sems, recv_sems):
        x, y, c = _mesh_pos()
        me = 4 * x + 2 * y + c
        pos = lambda t: (t // 4, (t // 2) % 2, t % 2)

        def copy(t, slot):
            return pltpu.make_async_remote_copy(src_ref=b_ref, dst_ref=recv_buf.at[slot], send_sem=send_sems.at[t],
                                                recv_sem=recv_sems.at[slot], device_id=pos(t), device_id_type=MESH)

        for t in range(n_dev):
            @pl.when(t != me)
            def _(t=t):
                copy(t, me).start()

        recv_buf[me] = b_ref[...]
        for t in range(n_dev):
            @pl.when(t != me)
            def _(t=t):
                copy(t, t).wait_recv()
                copy(t, me).wait_send()

        acc = recv_buf[0]
        for t in range(1, n_dev):
            acc = acc + recv_buf[t]
        o_ref[...] = acc

    return pl.pallas_call(
        body, name=name, out_shape=jax.ShapeDtypeStruct(buf.shape, F32),
        in_specs=[pl.BlockSpec(memory_space=pltpu.VMEM)], out_specs=pl.BlockSpec(memory_space=pltpu.VMEM),
        scratch_shapes=[pltpu.VMEM((n_dev,) + buf.shape, F32), pltpu.SemaphoreType.DMA((n_dev,)),
                        pltpu.SemaphoreType.DMA((n_dev,))],
    )(buf)


REDUCE_ROWS = 512


def add_selected(sel, a5, b, *, name):
    _, n, rh, w = a5.shape
    tr = REDUCE_ROWS

    def body(sel_ref, a_ref, b_ref, o_ref, ob_ref):
        s = a_ref[...] + b_ref[...]
        o_ref[...] = s
        ob_ref[...] = s.astype(BF16)

    blk = pl.BlockSpec((None, tr, w), lambda j, i, s: (j, i, 0))
    return pl.pallas_call(
        body, name=name,
        grid_spec=pltpu.PrefetchScalarGridSpec(
            num_scalar_prefetch=1, grid=(n, rh // tr),
            in_specs=[pl.BlockSpec((None, None, tr, w), lambda j, i, s: (s[0], j, i, 0)), blk], out_specs=[blk, blk]),
        out_shape=[jax.ShapeDtypeStruct((n, rh, w), F32), jax.ShapeDtypeStruct((n, rh, w), BF16)],
        compiler_params=_params(("arbitrary", "arbitrary")),
    )(sel, a5, b)


def add_chip_sums(sel, s1, b2, *, name):
    _, rh, w = s1.shape
    tr = REDUCE_ROWS

    def body(sel_ref, s_ref, b_ref, o_ref):
        o_ref[...] = ((s_ref[...] + b_ref[0].astype(F32)) + b_ref[1].astype(F32)) + b_ref[2].astype(F32)

    return pl.pallas_call(
        body, name=name,
        grid_spec=pltpu.PrefetchScalarGridSpec(
            num_scalar_prefetch=1, grid=(rh // tr,),
            in_specs=[pl.BlockSpec((None, tr, w), lambda i, s: (s[0], i, 0)), pl.BlockSpec((3, tr, w), lambda i, s: (0, i, 0))],
            out_specs=pl.BlockSpec((None, tr, w), lambda i, s: (s[1], i, 0))),
        out_shape=jax.ShapeDtypeStruct((2, rh, w), F32),
        compiler_params=_params(("arbitrary",)),
    )(sel, s1, b2)


BIG = (("gdn_w_in", 2), ("gdn_w_out", 1), ("w_kv", 1), ("sb_w_q", 1), ("sb_w_out", 1), ("ffn_w_in", 2),
       ("ffn_w_out", 1), ("ple_w_proj", 2), ("ple_w_gate", 1))
SMALL = ("ln_mix", "ln_ffn", "ln_ple", "gdn_a_log", "gdn_dt_bias", "gdn_norm", "kv_norm", "k_norm", "sb_q_norm")
WEIGHTS = ("ln_mix", "ln_ffn", "ln_ple", "gdn_w_in", "gdn_conv", "gdn_a_log", "gdn_dt_bias", "gdn_norm", "gdn_w_out",
           "kv_norm", "w_kv", "k_norm", "sb_w_q", "sb_q_norm", "sb_w_out", "ffn_w_in", "ffn_w_out", "ple_w_proj",
           "ple_w_gate")
PACK_ALIGN = 2 * 1024


def _rows_of(shape):
    return -(-math.prod(shape) // PACK_COLS)


def _pack_rows(arrs, lead):
    parts = []
    for a in arrs:
        flat = a.reshape(lead + (-1,))
        pad = _rows_of(a.shape[len(lead):]) * PACK_COLS - flat.shape[-1]
        if pad:
            flat = jnp.pad(flat, [(0, 0)] * len(lead) + [(0, pad)])
        parts.append(flat.reshape(lead + (-1, PACK_COLS)))
    out = jnp.concatenate(parts, axis=len(lead))
    pad = -out.shape[len(lead)] % PACK_ALIGN
    return jnp.pad(out, [(0, 0)] * len(lead) + [(0, pad), (0, 0)])


def _unpack_rows(buf, shapes, lead):
    out, r0 = [], 0
    for s in shapes:
        rows = _rows_of(s)
        flat = buf[(slice(None),) * len(lead) + (slice(r0, r0 + rows),)].reshape(lead + (-1,))
        out.append(flat[..., :math.prod(s)].reshape(lead + tuple(s)))
        r0 += rows
    return out


def _unshard(g, axis):
    g = jnp.moveaxis(g, 0, axis)
    s = g.shape
    return g.reshape(s[:axis] + (s[axis] * s[axis + 1],) + s[axis + 2:])


def _shard(full, axis):
    s = full.shape
    return jnp.moveaxis(full.reshape(s[:axis] + (N_CHIPS, s[axis] // N_CHIPS) + s[axis + 1:]), axis, 0)


def _to4(a):
    return a.reshape(HEADS, -1, 1, CHUNK).transpose(1, 0, 2, 3)


def _from4(a):
    return a.transpose(1, 0, 2, 3).reshape(HEADS, -1)


def _row(vec):
    flat = vec.reshape(-1)
    rows = _rows_of(flat.shape)
    return jnp.pad(flat, (0, rows * PACK_COLS - flat.shape[0])).reshape(rows, PACK_COLS)


def kernel(x, p, ln_mix, ln_ffn, ln_ple, gdn_w_in, gdn_conv, gdn_a_log, gdn_dt_bias, gdn_norm, gdn_w_out, kv_norm, w_kv, k_norm, sb_w_q, sb_q_norm, sb_w_out, ffn_w_in, ffn_w_out, ple_w_proj, ple_w_gate, loss_target, m_ln_mix, m_ln_ffn, m_ln_ple, m_gdn_w_in, m_gdn_conv, m_gdn_a_log, m_gdn_dt_bias, m_gdn_norm, m_gdn_w_out, m_kv_norm, m_w_kv, m_k_norm, m_sb_w_q, m_sb_q_norm, m_sb_w_out, m_ffn_w_in, m_ffn_w_out, m_ple_w_proj, m_ple_w_gate, v_ln_mix, v_ln_ffn, v_ln_ple, v_gdn_w_in, v_gdn_conv, v_gdn_a_log, v_gdn_dt_bias, v_gdn_norm, v_gdn_w_out, v_kv_norm, v_w_kv, v_k_norm, v_sb_w_q, v_sb_q_norm, v_sb_w_out, v_ffn_w_in, v_ffn_w_out, v_ple_w_proj, v_ple_w_gate):
    w = dict(ln_mix=ln_mix, ln_ffn=ln_ffn, ln_ple=ln_ple, gdn_w_in=gdn_w_in, gdn_conv=gdn_conv, gdn_a_log=gdn_a_log,
             gdn_dt_bias=gdn_dt_bias, gdn_norm=gdn_norm, gdn_w_out=gdn_w_out, kv_norm=kv_norm, w_kv=w_kv, k_norm=k_norm,
             sb_w_q=sb_w_q, sb_q_norm=sb_q_norm, sb_w_out=sb_w_out, ffn_w_in=ffn_w_in, ffn_w_out=ffn_w_out,
             ple_w_proj=ple_w_proj, ple_w_gate=ple_w_gate)
    mom1 = dict(ln_mix=m_ln_mix, ln_ffn=m_ln_ffn, ln_ple=m_ln_ple, gdn_w_in=m_gdn_w_in, gdn_conv=m_gdn_conv,
                gdn_a_log=m_gdn_a_log, gdn_dt_bias=m_gdn_dt_bias, gdn_norm=m_gdn_norm, gdn_w_out=m_gdn_w_out,
                kv_norm=m_kv_norm, w_kv=m_w_kv, k_norm=m_k_norm, sb_w_q=m_sb_w_q, sb_q_norm=m_sb_q_norm,
                sb_w_out=m_sb_w_out, ffn_w_in=m_ffn_w_in, ffn_w_out=m_ffn_w_out, ple_w_proj=m_ple_w_proj,
                ple_w_gate=m_ple_w_gate)
    mom2 = dict(ln_mix=v_ln_mix, ln_ffn=v_ln_ffn, ln_ple=v_ln_ple, gdn_w_in=v_gdn_w_in, gdn_conv=v_gdn_conv,
                gdn_a_log=v_gdn_a_log, gdn_dt_bias=v_gdn_dt_bias, gdn_norm=v_gdn_norm, gdn_w_out=v_gdn_w_out,
                kv_norm=v_kv_norm, w_kv=v_w_kv, k_norm=v_k_norm, sb_w_q=v_sb_w_q, sb_q_norm=v_sb_q_norm,
                sb_w_out=v_sb_w_out, ffn_w_in=v_ffn_w_in, ffn_w_out=v_ffn_w_out, ple_w_proj=v_ple_w_proj,
                ple_w_gate=v_ple_w_gate)
    depth = ln_mix.shape[0]
    n_a = gdn_w_in.shape[0]
    xi, yi, ci = _mesh_pos()
    chip = 2 * xi + yi
    sel_c = jnp.reshape(ci, (1,)).astype(jnp.int32)
    sel_chip = jnp.stack([chip, ci]).astype(jnp.int32)
    h = x[0]
    tgt = loss_target[0]
    t = h.shape[0]

    shard_shapes = [w[n].shape for n, _ in BIG]
    packed = _pack_rows([w[n].astype(BF16) for n, _ in BIG], ())
    slots = lax.dynamic_update_slice(jnp.zeros((N_CHIPS,) + packed.shape, BF16), packed[None], (chip, 0, 0))
    gathered = all_gather_chips(slots, name="all_gather_weights")
    full = {n: _unshard(g, ax) for (n, ax), g in
            zip(BIG, _unpack_rows(gathered, shard_shapes, (N_CHIPS,)))}
    conv_rows = _rows_of(gdn_conv.shape)
    small_rows = sum(_rows_of(w[n].shape) for n in SMALL)
    buf_rows = -(-(small_rows + N_CHIPS * conv_rows) // 8) * 8
    conv_buf = jnp.zeros((buf_rows, PACK_COLS), F32)
    conv_buf = lax.dynamic_update_slice(conv_buf, _row(gdn_conv) * (ci == 0).astype(F32), (chip * conv_rows, 0))
    conv_all = all_reduce_small(conv_buf, name="all_reduce_small")[:N_CHIPS * conv_rows]
    conv_full = _unshard(conv_all.reshape(N_CHIPS, -1)[:, :math.prod(gdn_conv.shape)].reshape((N_CHIPS,) + gdn_conv.shape), 2)

    saved = []
    k_sh = v_sh = None
    mid = None
    for i in range(depth):
        s = dict(h0=h)
        s["hn"] = hn = rms_fwd(h, ln_mix[i:i + 1], name="rms_fwd")
        if i < n_a:
            w_in = full["gdn_w_in"][i]
            s["w_m"], s["w_abt"] = w_in[:, :4 * WIDTH], w_in[:, 4 * WIDTH:].T
            s["proj"] = proj = matmul(hn, s["w_m"], "nn", name="mm_gdn_in")
            s["ab"] = ab = matmul(s["w_abt"], hn, "nt", name="mm_gdn_ab")
            s["a_log"], s["dt"] = gdn_a_log[i][:, None], gdn_dt_bias[i][:, None]
            g8, b8 = gates_fwd(ab, s["a_log"], s["dt"], name="gates_fwd")
            s["g4"], s["b4"] = _to4(g8), _to4(b8)
            s["qkv"] = qkv = conv_fwd(proj, conv_full[i], name="conv_fwd")
            s["o"], s["s0"], s["tinv"] = gdn_fwd(qkv, s["g4"], s["b4"], name="gdn_fwd")
            s["y"] = y = gatenorm_fwd(s["o"], proj, gdn_norm[i:i + 1], name="gatenorm_fwd")
            h = matmul(y, full["gdn_w_out"][i], "nn", add=h, name="mm_out")
        else:
            j = i - n_a
            s["qraw"] = qraw = matmul(hn, full["sb_w_q"][j], "nn", name="mm_sq")
            s["q"] = q = headnorm_fwd(qraw, 0, sb_q_norm[j:j + 1], HEAD_DIM ** -0.5, name="headnorm_q")
            s["o"] = o = sb_fwd(q, k_sh, v_sh, name="sb_fwd")
            h = matmul(o, full["sb_w_out"][j], "nn", add=h, name="mm_out")
        s["h1"] = h
        s["hn2"] = hn2 = rms_fwd(h, ln_ffn[i:i + 1], name="rms_fwd")
        s["gu"] = gu = matmul(hn2, full["ffn_w_in"][i], "nn", name="mm_ffn_in")
        s["act"] = act = swiglu_fwd(gu, name="swiglu_fwd")
        h = matmul(act, full["ffn_w_out"][i], "nn", add=h, name="mm_ffn_out")
        s["h2"] = h
        s["hn3"] = hn3 = rms_fwd(h, ln_ple[i:i + 1], name="rms_fwd")
        s["gt"] = gt = matmul(hn3, full["ple_w_gate"][i], "nn", name="mm_sq")
        s["pp"] = pp = matmul(p[i, 0], full["ple_w_proj"][i], "nn", name="mm_ple_proj")
        h = ple_fwd(h, pp, gt, name="ple_fwd")
        saved.append(s)
        if i == n_a - 1:
            mid = dict(h=h)
            mid["hk"] = hk = rms_fwd(h, kv_norm[None, :], name="rms_fwd")
            mid["kv"] = kv = matmul(hk, full["w_kv"], "nn", name="mm_kv")
            k_sh = headnorm_fwd(kv, 0, k_norm[None, :], 1.0, name="headnorm_k")
            v_sh = kv[:, WIDTH:].astype(BF16)

    dh, sq = loss_head(h, tgt, name="loss_head")
    loss = lax.psum(0.5 * jnp.sum(sq) / h.shape[1], ("x", "y", "c"))

    gw = {n: [None] * w[n].shape[0] for n in WEIGHTS if w[n].ndim >= 2 and n not in ("w_kv",)}
    dk_acc = jnp.zeros((t, WIDTH), F32)
    dv_acc = jnp.zeros((t, WIDTH), F32)
    for i in reversed(range(depth)):
        s = saved[i]
        if i == n_a - 1:
            dkraw, gw["k_norm"] = headnorm_bwd(mid["kv"], 0, k_norm[None, :], 1.0, dk_acc, name="headnorm_k_bwd")
            dkv = jnp.concatenate([dkraw, dv_acc.astype(BF16)], axis=1)
            dhk = matmul(dkv, full["w_kv"], "nt", name="mm_kv_dx")
            gw["w_kv"] = matmul(mid["hk"], dkv, "tn", name="mm_kv_dw")
            dh, gw["kv_norm"] = rms_bwd(mid["h"], kv_norm[None, :], dhk, dh, name="rms_bwd")
        dpp, dgt = ple_bwd(dh, s["pp"], s["gt"], name="ple_bwd")
        gw["ple_w_proj"][i] = matmul(p[i, 0], dpp, "tn", name="mm_ple_proj_dw")
        gw["ple_w_gate"][i] = matmul(s["hn3"], dgt, "tn", name="mm_sq_dw")
        dhn3 = matmul(dgt, full["ple_w_gate"][i], "nt", name="mm_sq_dx")
        dh, gw["ln_ple"][i] = rms_bwd(s["h2"], ln_ple[i:i + 1], dhn3, dh, name="rms_bwd")
        dact = matmul(dh, full["ffn_w_out"][i], "nt", name="mm_ffn_out_dx")
        gw["ffn_w_out"][i] = matmul(s["act"], dh, "tn", name="mm_ffn_out_dw")
        dgu = swiglu_bwd(s["gu"], dact, name="swiglu_bwd")
        dhn2 = matmul(dgu, full["ffn_w_in"][i], "nt", name="mm_ffn_in_dx")
        gw["ffn_w_in"][i] = matmul(s["hn2"], dgu, "tn", name="mm_ffn_in_dw")
        dh, gw["ln_ffn"][i] = rms_bwd(s["h1"], ln_ffn[i:i + 1], dhn2, dh, name="rms_bwd")
        if i < n_a:
            dy = matmul(dh, full["gdn_w_out"][i], "nt", name="mm_sq_dx")
            gw["gdn_w_out"][i] = matmul(s["y"], dh, "tn", name="mm_sq_dw")
            do, dproj, gw["gdn_norm"][i] = gatenorm_bwd(s["o"], s["proj"], gdn_norm[i:i + 1], dy, name="gatenorm_bwd")
            dqkv, dg4, db4 = gdn_bwd(s["qkv"], s["g4"], s["b4"], s["s0"], s["tinv"], do, name="gdn_bwd")
            dab, dal, ddt = gates_bwd(s["ab"], s["a_log"], s["dt"], _from4(dg4), _from4(db4), name="gates_bwd")
            gw["gdn_a_log"][i], gw["gdn_dt_bias"][i] = dal[:, 0], ddt[:, 0]
            dproj, gw["gdn_conv"][i] = conv_bwd(s["proj"], conv_full[i], dqkv, dproj, name="conv_bwd")
            dhn = matmul(dproj, s["w_m"], "nt", name="mm_gdn_in_dx")
            dhn = matmul(dab, s["w_abt"], "tn", add=dhn, name="mm_gdn_ab_dx")
            dwm = matmul(s["hn"], dproj, "tn", name="mm_gdn_in_dw")
            dwab = matmul(dab, s["hn"], "nn", name="mm_gdn_ab_dw")
            gw["gdn_w_in"][i] = jnp.concatenate([dwm, dwab.T], axis=1)
        else:
            j = i - n_a
            do = matmul(dh, full["sb_w_out"][j], "nt", out_dtype=BF16, name="mm_sb_out_dx")
            gw["sb_w_out"][j] = matmul(s["o"], dh, "tn", name="mm_sq_dw")
            dq, dk_acc, dv_acc = sb_bwd(s["q"], k_sh, v_sh, do, dk_acc, dv_acc, name="sb_bwd")
            dqraw, gw["sb_q_norm"][j] = headnorm_bwd(s["qraw"], 0, sb_q_norm[j:j + 1], HEAD_DIM ** -0.5, dq,
                                                    name="headnorm_q_bwd")
            dhn = matmul(dqraw, full["sb_w_q"][j], "nt", name="mm_sq_dx")
            gw["sb_w_q"][j] = matmul(s["hn"], dqraw, "tn", name="mm_sq_dw")
        dh, gw["ln_mix"][i] = rms_bwd(s["h0"], ln_mix[i:i + 1], dhn, dh, name="rms_bwd")
    grad_x = dh[None]

    def stacked(n):
        g = gw[n]
        if isinstance(g, list):
            g = jnp.stack([a.reshape(w[n].shape[1:]) if n in SMALL else a for a in g])
        return g

    small_buf = jnp.concatenate([_row(stacked(n)) for n in SMALL] + [_row(stacked("gdn_conv"))], axis=0)
    small_buf = jnp.pad(small_buf, ((0, buf_rows - small_buf.shape[0]), (0, 0)))
    small_sum = all_reduce_small(small_buf, name="all_reduce_small")
    grads = {}
    r0 = 0
    for n in SMALL:
        rows = _rows_of(w[n].shape)
        grads[n] = small_sum[r0:r0 + rows].reshape(-1)[:math.prod(w[n].shape)].reshape(w[n].shape)
        r0 += rows
    conv_g = small_sum[r0:r0 + N_CHIPS * conv_rows].reshape(-1)[:N_CHIPS * math.prod(gdn_conv.shape)]
    conv_g = conv_g.reshape((gdn_conv.shape[0], CONV_WIDTH, N_CHIPS, gdn_conv.shape[2]))
    grads["gdn_conv"] = lax.dynamic_index_in_dim(conv_g, chip, axis=2, keepdims=False)

    g_packed = _pack_rows([_shard(stacked(n), ax) for n, ax in BIG], (N_CHIPS,))
    rh = g_packed.shape[1] // 2
    g5 = g_packed.reshape(N_CHIPS, 2, rh, PACK_COLS).transpose(1, 0, 2, 3)
    from_sibling = sibling_swap(g5, name="sibling_swap")
    s1, s1b = add_selected(sel_c, g5, from_sibling, name="add_selected")
    from_chips = chip_exchange(s1b, name="chip_exchange")
    s2 = add_chip_sums(sel_chip, s1, from_chips, name="add_chip_sums")
    reduced = sibling_merge(s2, name="sibling_merge").reshape(2 * rh, PACK_COLS)
    for (n, _), g in zip(BIG, _unpack_rows(reduced, shard_shapes, ())):
        grads[n] = g

    delta, new_m, new_v = {}, {}, {}
    for n in WEIGHTS:
        delta[n], new_m[n], new_v[n] = adamw(w[n], grads[n], mom1[n], mom2[n], name="adamw")
    return (loss, grad_x, *[grads[n] for n in WEIGHTS], *[delta[n] for n in WEIGHTS],
            *[new_m[n] for n in WEIGHTS], *[new_v[n] for n in WEIGHTS])
```

```python
import functools
import math

import jax
import jax.numpy as jnp
from jax import lax
from jax.experimental import pallas as pl
from jax.experimental.pallas import tpu as pltpu

F32 = jnp.float32
BF16 = jnp.bfloat16
EPS = 1e-6
HEADS = 8
HEAD_DIM = 128
WIDTH = HEADS * HEAD_DIM
CHUNK = 64
CONV_WIDTH = 4
N_CHIPS = 4
ADAM_LR, ADAM_B1, ADAM_B2, ADAM_EPS, ADAM_WD, ADAM_STEP = 0.001, 0.9, 0.999, 1e-08, 0.01, 10
V7X_VMEM_BYTES = 64 * 1024 * 1024
VMEM_LIMIT = V7X_VMEM_BYTES - 8 * 1024 * 1024
HIGHEST = lax.Precision.HIGHEST
MESH = pl.DeviceIdType.MESH


def _params(sem=None):
    return pltpu.CompilerParams(dimension_semantics=sem, vmem_limit_bytes=VMEM_LIMIT)


def _pick(n, prefs):
    for t in prefs:
        if t <= n and n % t == 0:
            return t
    return n


def _bdot(a, b, dims):
    return lax.dot_general(a.astype(BF16), b.astype(BF16), (((dims[0],), (dims[1],)), ((), ())),
                           preferred_element_type=F32)


NN, NT, TN = (1, 0), (1, 1), (0, 0)


MM_TILES = (1024, 1408, 512, 256, 128)


def matmul(a, b, form, *, out_dtype=F32, add=None, name):
    if form == "nn":
        (m, k), (k2, n) = a.shape, b.shape
    elif form == "nt":
        (m, k), (n, k2) = a.shape, b.shape
    else:
        (k, m), (k2, n) = a.shape, b.shape
    assert k == k2, (a.shape, b.shape, form)
    tm = _pick(m, MM_TILES)
    tn = _pick(n, MM_TILES)
    tk = k if k <= 1024 else _pick(k, MM_TILES)
    nk = k // tk
    if form == "tn":
        a_spec = pl.BlockSpec((tk, tm), lambda i, j, kk: (kk, i))
    else:
        a_spec = pl.BlockSpec((tm, tk), lambda i, j, kk: (i, kk))
    if form == "nt":
        b_spec = pl.BlockSpec((tn, tk), lambda i, j, kk: (j, kk))
    else:
        b_spec = pl.BlockSpec((tk, tn), lambda i, j, kk: (kk, j))
    o_spec = pl.BlockSpec((tm, tn), lambda i, j, kk: (i, j))
    dims = {"nn": NN, "nt": NT, "tn": TN}[form]
    has_add = add is not None

    def body(*refs):
        a_ref, b_ref = refs[:2]
        add_ref = refs[2] if has_add else None
        o_ref = refs[2 + has_add]

        def finish(r):
            if has_add:
                r = r + add_ref[...].astype(F32)
            o_ref[...] = r.astype(out_dtype)

        part = _bdot(a_ref[...], b_ref[...], dims)
        if nk == 1:
            finish(part)
            return
        acc_ref = refs[3 + has_add]
        kk = pl.program_id(2)

        @pl.when(kk == 0)
        def _():
            acc_ref[...] = part

        @pl.when(kk > 0)
        def _():
            acc_ref[...] += part

        @pl.when(kk == nk - 1)
        def _():
            finish(acc_ref[...])

    in_specs = [a_spec, b_spec] + ([o_spec] if has_add else [])
    args = (a, b) + ((add,) if has_add else ())
    return pl.pallas_call(
        body, name=name, grid=(m // tm, n // tn, nk), in_specs=in_specs, out_specs=o_spec,
        out_shape=jax.ShapeDtypeStruct((m, n), out_dtype),
        scratch_shapes=[pltpu.VMEM((tm, tn), F32)] if nk > 1 else [],
        compiler_params=_params(("parallel", "parallel", "arbitrary")),
    )(*args)


def _const(c):
    return lambda j: c


def rowwise(fn, rows, params, outs, accs=(), *, name, tm, ncol=1):
    t = rows[0][0].shape[0]
    tm = min(tm, t)
    assert t % tm == 0
    n_rows, n_par, n_out, n_acc = len(rows), len(params), len(outs), len(accs)

    def body(*refs):
        j, i = pl.program_id(0), pl.program_id(1)
        ins = [r[...] for r in refs[:n_rows + n_par]]
        o_refs = refs[n_rows + n_par:n_rows + n_par + n_out]
        a_refs = refs[n_rows + n_par + n_out:]
        row_outs, acc_outs = fn(*ins)
        for r, val in zip(o_refs, row_outs):
            r[...] = val.astype(r.dtype)
        for r, val, spec in zip(a_refs, acc_outs, accs):
            first = (i == 0) & (j == 0) if spec[4] else (i == 0)

            @pl.when(first)
            def _(r=r, val=val):
                r[...] = val.astype(F32)

            @pl.when(jnp.logical_not(first))
            def _(r=r, val=val):
                r[...] += val.astype(F32)

    in_specs = [pl.BlockSpec((tm, w), lambda j, i, cf=cf: (i, cf(j))) for _, w, cf in rows]
    in_specs += [pl.BlockSpec((p.shape[0], w), lambda j, i, cf=cf: (0, cf(j))) for p, w, cf in params]
    out_specs = [pl.BlockSpec((tm, w), lambda j, i, cf=cf: (i, cf(j))) for _, _, w, cf in outs]
    out_specs += [pl.BlockSpec((r, w), lambda j, i, cf=cf: (0, cf(j))) for r, _, w, cf, _ in accs]
    out_shape = [jax.ShapeDtypeStruct((t, tw), dt) for tw, dt, _, _ in outs]
    out_shape += [jax.ShapeDtypeStruct((r, tw), F32) for r, tw, _, _, _ in accs]
    res = pl.pallas_call(
        body, name=name, grid=(ncol, t // tm), in_specs=in_specs, out_specs=out_specs, out_shape=out_shape,
        compiler_params=_params(("arbitrary", "arbitrary")),
    )(*[r[0] for r in rows], *[p[0] for p in params])
    return res[:n_out], res[n_out:]


def _full(arr):
    return (arr, arr.shape[1], _const(0))


def _rms(x, g):
    x = x.astype(F32)
    return x * lax.rsqrt(jnp.mean(x * x, axis=-1, keepdims=True) + EPS) * g.astype(F32)


def _sigmoid(x):
    return 1.0 / (1.0 + jnp.exp(-x))


def _silu(x):
    return x * _sigmoid(x)


def _softplus(x):
    return jnp.maximum(x, 0.0) + jnp.log(1.0 + jnp.exp(-jnp.abs(x)))


def rms_fwd(h, g, *, name):
    d = h.shape[1]
    (hn,), _ = rowwise(lambda x, gg: ((_rms(x, gg),), ()), [_full(h)], [_full(g)],
                       [(d, BF16, d, _const(0))], name=name, tm=512)
    return hn


def rms_bwd(h, g, dhn, dh_res, *, name):
    d = h.shape[1]

    def fn(x, ct, res, gg):
        _, vjp = jax.vjp(_rms, x.astype(F32), gg.astype(F32))
        dx, dg = vjp(ct.astype(F32))
        return (res.astype(F32) + dx,), (dg,)

    (dh,), (dg,) = rowwise(fn, [_full(h), _full(dhn), _full(dh_res)], [_full(g)],
                           [(d, F32, d, _const(0))], [(1, d, d, _const(0), True)], name=name, tm=256)
    return dh, dg


def _head_rms(x, g, scale):
    x = x.astype(F32)
    return x * lax.rsqrt(jnp.mean(x * x, axis=-1, keepdims=True) + EPS) * (g.astype(F32) * scale)


def headnorm_fwd(x, col0, g, scale, *, name):
    (y,), _ = rowwise(lambda a, gg: ((_head_rms(a, gg, scale),), ()),
                      [(x, HEAD_DIM, lambda j: col0 + j)], [_full(g)],
                      [(WIDTH, BF16, HEAD_DIM, lambda j: j)], name=name, tm=1024, ncol=HEADS)
    return y


def headnorm_bwd(x, col0, g, scale, dy, *, name, out_dtype=BF16):
    def fn(a, ct, gg):
        _, vjp = jax.vjp(lambda a_, g_: _head_rms(a_, g_, scale), a.astype(F32), gg.astype(F32))
        dx, dg = vjp(ct.astype(F32))
        return (dx,), (dg,)

    (dx,), (dg,) = rowwise(fn, [(x, HEAD_DIM, lambda j: col0 + j), (dy, HEAD_DIM, lambda j: j)], [_full(g)],
                           [(WIDTH, out_dtype, HEAD_DIM, lambda j: j)],
                           [(1, HEAD_DIM, HEAD_DIM, _const(0), True)], name=name, tm=1024, ncol=HEADS)
    return dx, dg


def _gatenorm(o, gate, g):
    return _head_rms(o, g, 1.0) * _silu(gate.astype(F32))


def gatenorm_fwd(o, proj, g, *, name):
    (y,), _ = rowwise(lambda a, gt, gg: ((_gatenorm(a, gt, gg),), ()),
                      [(o, HEAD_DIM, lambda j: j), (proj, HEAD_DIM, lambda j: 3 * HEADS + j)], [_full(g)],
                      [(WIDTH, BF16, HEAD_DIM, lambda j: j)], name=name, tm=1024, ncol=HEADS)
    return y


def gatenorm_bwd(o, proj, g, dy, *, name):
    def fn(a, gt, ct, gg):
        _, vjp = jax.vjp(_gatenorm, a.astype(F32), gt.astype(F32), gg.astype(F32))
        da, dgt, dg = vjp(ct.astype(F32))
        return (da, dgt), (dg,)

    (do, dproj), (dg,) = rowwise(
        fn, [(o, HEAD_DIM, lambda j: j), (proj, HEAD_DIM, lambda j: 3 * HEADS + j), (dy, HEAD_DIM, lambda j: j)],
        [_full(g)],
        [(WIDTH, F32, HEAD_DIM, lambda j: j), (4 * WIDTH, BF16, HEAD_DIM, lambda j: 3 * HEADS + j)],
        [(1, HEAD_DIM, HEAD_DIM, _const(0), True)], name=name, tm=1024, ncol=HEADS)
    return do, dproj, dg


def _swiglu(g, u):
    return _silu(g.astype(F32)) * u.astype(F32)


def swiglu_fwd(gu, *, name):
    f = gu.shape[1] // 2
    (act,), _ = rowwise(lambda g, u: ((_swiglu(g, u),), ()), [(gu, f, _const(0)), (gu, f, _const(1))], [],
                        [(f, BF16, f, _const(0))], name=name, tm=256)
    return act


def swiglu_bwd(gu, dact, *, name):
    f = gu.shape[1] // 2

    def fn(g, u, ct):
        _, vjp = jax.vjp(_swiglu, g.astype(F32), u.astype(F32))
        dg, du = vjp(ct.astype(F32))
        return (jnp.concatenate([dg.astype(BF16), du.astype(BF16)], axis=1),), ()

    (dgu,), _ = rowwise(fn, [(gu, f, _const(0)), (gu, f, _const(1)), _full(dact)], [],
                        [(2 * f, BF16, 2 * f, _const(0))], name=name, tm=256)
    return dgu


def ple_fwd(h, pp, gt, *, name):
    d = h.shape[1]
    (out,), _ = rowwise(lambda a, b, c: ((a + b * _sigmoid(c),), ()), [_full(h), _full(pp), _full(gt)], [],
                        [(d, F32, d, _const(0))], name=name, tm=512)
    return out


def ple_bwd(dh, pp, gt, *, name):
    d = dh.shape[1]

    def fn(ct, b, c):
        s = _sigmoid(c)
        return (ct * s, ct * b * s * (1.0 - s)), ()

    (dpp, dgt), _ = rowwise(fn, [_full(dh), _full(pp), _full(gt)], [],
                            [(d, BF16, d, _const(0)), (d, BF16, d, _const(0))], name=name, tm=512)
    return dpp, dgt


def loss_head(y, tgt, *, name):
    d = y.shape[1]

    def fn(a, b):
        e = a - b
        return (e * (1.0 / d),), (jnp.sum(e * e, axis=0, keepdims=True),)

    (dy,), (sq,) = rowwise(fn, [_full(y), _full(tgt)], [], [(d, F32, d, _const(0))],
                           [(1, d, d, _const(0), True)], name=name, tm=512)
    return dy, sq


def adamw(w, g, m, v, *, name):
    shape = w.shape
    cols = shape[-1]
    flat = lambda a: a.reshape(-1, cols)
    bc1 = 1.0 - ADAM_B1 ** ADAM_STEP
    bc2 = 1.0 - ADAM_B2 ** ADAM_STEP

    def fn(w_, g_, m_, v_):
        m_ = ADAM_B1 * m_ + (1.0 - ADAM_B1) * g_
        v_ = ADAM_B2 * v_ + (1.0 - ADAM_B2) * (g_ * g_)
        delta = -ADAM_LR * ((m_ / bc1) / (jnp.sqrt(v_ / bc2) + ADAM_EPS) + ADAM_WD * w_)
        return (delta, m_, v_), ()

    o = (cols, F32, cols, _const(0))
    (d_, m_, v_), _ = rowwise(fn, [_full(flat(w)), _full(flat(g)), _full(flat(m)), _full(flat(v))], [],
                              [o, o, o], name=name, tm=256)
    return d_.reshape(shape), m_.reshape(shape), v_.reshape(shape)


CONV_STRIP = 256


def _shift_down(x, d):
    if d == 0:
        return x
    rows = lax.broadcasted_iota(jnp.int32, x.shape, 0)
    return jnp.where(rows >= d, pltpu.roll(x, d, 0), 0.0)


def _shift_up(x, d):
    if d == 0:
        return x
    t = x.shape[0]
    rows = lax.broadcasted_iota(jnp.int32, x.shape, 0)
    return jnp.where(rows < t - d, pltpu.roll(x, t - d, 0), 0.0)


def _conv(x, w):
    acc = None
    for j in range(CONV_WIDTH):
        term = _shift_down(x, CONV_WIDTH - 1 - j) * w[j:j + 1, :]
        acc = term if acc is None else acc + term
    return acc


def conv_fwd(proj, w, *, name):
    t = proj.shape[0]
    per = WIDTH // CONV_STRIP

    def body(x_ref, w_ref, o_ref):
        o_ref[0] = _silu(_conv(x_ref[...].astype(F32), w_ref[...]))

    return pl.pallas_call(
        body, name=name, grid=(3 * per,),
        in_specs=[pl.BlockSpec((t, CONV_STRIP), lambda j: (0, j)), pl.BlockSpec((CONV_WIDTH, CONV_STRIP), lambda j: (0, j))],
        out_specs=pl.BlockSpec((1, t, CONV_STRIP), lambda j: (j // per, 0, j % per)),
        out_shape=jax.ShapeDtypeStruct((3, t, WIDTH), F32),
        compiler_params=_params(("parallel",)),
    )(proj, w)


def conv_bwd(proj, w, dqkv, dproj, *, name):
    t = proj.shape[0]
    per = WIDTH // CONV_STRIP

    def body(x_ref, w_ref, d_ref, _, dx_ref, dw_ref):
        x, w_ = x_ref[...].astype(F32), w_ref[...]
        c = _conv(x, w_)
        s = _sigmoid(c)
        dc = d_ref[0] * (s + c * s * (1.0 - s))
        dx = None
        for j in range(CONV_WIDTH):
            d = CONV_WIDTH - 1 - j
            term = _shift_up(dc, d) * w_[j:j + 1, :]
            dx = term if dx is None else dx + term
            dw_ref[j:j + 1, :] = jnp.sum(dc * _shift_down(x, d), axis=0, keepdims=True)
        dx_ref[...] = dx.astype(dx_ref.dtype)

    return pl.pallas_call(
        body, name=name, grid=(3 * per,),
        in_specs=[pl.BlockSpec((t, CONV_STRIP), lambda j: (0, j)), pl.BlockSpec((CONV_WIDTH, CONV_STRIP), lambda j: (0, j)),
                  pl.BlockSpec((1, t, CONV_STRIP), lambda j: (j // per, 0, j % per)), pl.BlockSpec(memory_space=pl.ANY)],
        out_specs=[pl.BlockSpec((t, CONV_STRIP), lambda j: (0, j)), pl.BlockSpec((CONV_WIDTH, CONV_STRIP), lambda j: (0, j))],
        out_shape=[jax.ShapeDtypeStruct(dproj.shape, dproj.dtype), jax.ShapeDtypeStruct((CONV_WIDTH, 3 * WIDTH), F32)],
        input_output_aliases={3: 0},
        compiler_params=_params(("parallel",)),
    )(proj, w, dqkv, dproj)


def _gdn_gates(ab, a_log, dt_bias):
    a_in, b_in = ab[:HEADS], ab[HEADS:]
    g = -jnp.exp(a_log) * _softplus(a_in + dt_bias)
    return g, _sigmoid(b_in)


def gates_fwd(ab, a_log, dt_bias, *, name):
    t = ab.shape[1]

    def body(ab_ref, al_ref, dt_ref, g_ref, b_ref):
        g_ref[...], b_ref[...] = _gdn_gates(ab_ref[...], al_ref[...], dt_ref[...])

    s = jax.ShapeDtypeStruct((HEADS, t), F32)
    return pl.pallas_call(body, name=name, out_shape=[s, s], compiler_params=_params())(ab, a_log, dt_bias)


def gates_bwd(ab, a_log, dt_bias, dg, dbeta, *, name):
    t = ab.shape[1]

    def body(ab_ref, al_ref, dt_ref, dg_ref, db_ref, dab_ref, dal_ref, ddt_ref):
        _, vjp = jax.vjp(_gdn_gates, ab_ref[...], al_ref[...], dt_ref[...])
        dab_ref[...], dal_ref[...], ddt_ref[...] = vjp((dg_ref[...], db_ref[...]))

    c = jax.ShapeDtypeStruct((HEADS, 1), F32)
    return pl.pallas_call(body, name=name, out_shape=[jax.ShapeDtypeStruct((2 * HEADS, t), F32), c, c],
                          compiler_params=_params())(ab, a_log, dt_bias, dg, dbeta)


def _split3(x):
    hi = x.astype(BF16)
    r1 = x - hi.astype(F32)
    mid = r1.astype(BF16)
    lo = (r1 - mid.astype(F32)).astype(BF16)
    return hi, mid, lo


def _dot01(x, m01):
    hi, mid, lo = _split3(x)
    m01 = m01.astype(BF16)
    return _bdot(hi, m01, NN) + _bdot(mid, m01, NN) + _bdot(lo, m01, NN)


def _hdot(a, b, dims=NN):
    a_hi, b_hi = a.astype(BF16), b.astype(BF16)
    a_lo, b_lo = (a - a_hi.astype(F32)).astype(BF16), (b - b_hi.astype(F32)).astype(BF16)
    return _bdot(a_hi, b_hi, dims) + (_bdot(a_hi, b_lo, dims) + _bdot(a_lo, b_hi, dims))


def _rowsum(x):
    return jnp.sum(x, axis=1, keepdims=True)


def _colsum(x):
    return jnp.sum(x, axis=0, keepdims=True)


class _Heads:
    def __init__(self, vals):
        self.v = list(vals)

    def _bin(self, other, f):
        if isinstance(other, _Heads):
            return _Heads(f(a, b) for a, b in zip(self.v, other.v))
        return _Heads(f(a, other) for a in self.v)

    def __add__(self, o):
        return self._bin(o, lambda a, b: a + b)

    __radd__ = __add__

    def __sub__(self, o):
        return self._bin(o, lambda a, b: a - b)

    def __rsub__(self, o):
        return self._bin(o, lambda a, b: b - a)

    def __mul__(self, o):
        return self._bin(o, lambda a, b: a * b)

    __rmul__ = __mul__

    def __neg__(self):
        return _Heads(-a for a in self.v)


def _hmap(f, *args):
    n = next(len(a.v) for a in args if isinstance(a, _Heads))
    return _Heads(f(*[a.v[h] if isinstance(a, _Heads) else a for a in args]) for h in range(n))


def _inv_unit_lower(a, eye):
    p = jnp.where(eye, 1.0, 0.0) - a
    ak = a
    for _ in range(int(math.log2(CHUNK)) - 1):
        ak = _hmap(_hdot, ak, ak)
        p = p + _hmap(_hdot, p, ak)
    return p


def _gdn_chunk(qr, kr, v, grow, brow, tinv=None):
    c = CHUNK
    ri = lax.broadcasted_iota(jnp.int32, (c, c), 0)
    ci = lax.broadcasted_iota(jnp.int32, (c, c), 1)
    eye, lower, strict = ri == ci, ri >= ci, ri > ci
    where = lambda m: (lambda a: jnp.where(m, a, 0.0))
    to_col = lambda row: _hmap(lambda r: _rowsum(jnp.where(eye, jnp.broadcast_to(r, (c, c)), 0.0)), row)
    cum_row = _hmap(lambda g: _dot01(jnp.broadcast_to(g, (8, c)), ri <= ci)[0:1], grow)
    gcol, bcol = to_col(cum_row), to_col(brow)
    glast = _hmap(lambda g: _colsum(jnp.where(ri[:, 0:1] == c - 1, g, 0.0)), gcol)
    rq = _hmap(lambda a: lax.rsqrt(_rowsum(a * a) + EPS), qr)
    rk = _hmap(lambda a: lax.rsqrt(_rowsum(a * a) + EPS), kr)
    scale = HEAD_DIM ** -0.5
    qn, kn = qr * (rq * scale), kr * rk
    dec = _hmap(lambda gc, gr: jnp.where(lower, jnp.exp(jnp.minimum(gc - gr, 0.0)), 0.0), gcol, cum_row)
    kk = _hmap(lambda a: _bdot(a, a, NT), kn)
    qk = _hmap(lambda a, b: _bdot(a, b, NT), qn, kn)
    gam_col, e_col, gam_last = _hmap(jnp.exp, gcol), _hmap(jnp.exp, glast - gcol), _hmap(jnp.exp, glast)
    if tinv is None:
        tinv = _inv_unit_lower(_hmap(where(strict), bcol * kk * dec), eye)
    u = _hmap(_hdot, tinv, v * bcol)
    w = _hmap(_hdot, tinv, kn * (bcol * gam_col))
    return dict(eye=eye, lower=lower, strict=strict, gcol=gcol, bcol=bcol, rq=rq, rk=rk, qn=qn, kn=kn,
                dec=dec, kk=kk, qk=qk, gam_col=gam_col, e_col=e_col, gam_last=gam_last, tinv=tinv, u=u, w=w,
                aqk=qk * dec, qt=qn * gam_col, kt=kn * e_col, scale=scale)


def _head_cols(h):
    return slice(h * HEAD_DIM, (h + 1) * HEAD_DIM)


def _bd(dims):
    return lambda a, b: _bdot(a, b, dims)


def gdn_fwd(qkv, g4, b4, *, name):
    t = qkv.shape[1]
    n = t // CHUNK
    d = HEAD_DIM
    heads = range(HEADS)

    def body(qkv_ref, g_ref, b_ref, o_ref, s0_ref, t_ref, s_ref):
        @pl.when(pl.program_id(0) == 0)
        def _():
            s_ref[...] = jnp.zeros_like(s_ref)

        qr, kr, v = (_Heads(qkv_ref[j, :, _head_cols(h)] for h in heads) for j in range(3))
        z = _gdn_chunk(qr, kr, v, _Heads(g_ref[0, h] for h in heads), _Heads(b_ref[0, h] for h in heads))
        s0 = _Heads(s_ref[h] for h in heads)
        v_new = z["u"] - _hmap(_bd(NN), z["w"], s0)
        o = _hmap(_bd(NN), z["qt"], s0) + _hmap(_bd(NN), z["aqk"], v_new)
        s_new = s0 * z["gam_last"] + _hmap(_bd(TN), z["kt"], v_new)
        for h in heads:
            s0_ref[0, h] = s0.v[h]
            t_ref[0, h] = z["tinv"].v[h]
            o_ref[:, _head_cols(h)] = o.v[h]
            s_ref[h] = s_new.v[h]

    gspec = pl.BlockSpec((1, HEADS, 1, CHUNK), lambda i: (i, 0, 0, 0))
    return pl.pallas_call(
        body, name=name, grid=(n,),
        in_specs=[pl.BlockSpec((3, CHUNK, WIDTH), lambda i: (0, i, 0)), gspec, gspec],
        out_specs=[pl.BlockSpec((CHUNK, WIDTH), lambda i: (i, 0)),
                   pl.BlockSpec((1, HEADS, d, d), lambda i: (i, 0, 0, 0)),
                   pl.BlockSpec((1, HEADS, CHUNK, CHUNK), lambda i: (i, 0, 0, 0))],
        out_shape=[jax.ShapeDtypeStruct((t, WIDTH), F32), jax.ShapeDtypeStruct((n, HEADS, d, d), F32),
                   jax.ShapeDtypeStruct((n, HEADS, CHUNK, CHUNK), F32)],
        scratch_shapes=[pltpu.VMEM((HEADS, d, d), F32)],
        compiler_params=_params(("arbitrary",)),
    )(qkv, g4, b4)


def gdn_bwd(qkv, g4, b4, s0_all, tinv_all, do, *, name):
    t = qkv.shape[1]
    n = t // CHUNK
    d = HEAD_DIM
    c = CHUNK
    heads = range(HEADS)

    def body(qkv_ref, g_ref, b_ref, s0_ref, t_ref, do_ref, dqkv_ref, dg_ref, db_ref, ds_ref):
        @pl.when(pl.program_id(0) == 0)
        def _():
            ds_ref[...] = jnp.zeros_like(ds_ref)

        qr, kr, v = (_Heads(qkv_ref[j, :, _head_cols(h)] for h in heads) for j in range(3))
        z = _gdn_chunk(qr, kr, v, _Heads(g_ref[0, h] for h in heads), _Heads(b_ref[0, h] for h in heads),
                       tinv=_Heads(t_ref[0, h] for h in heads))
        s0 = _Heads(s0_ref[0, h] for h in heads)
        ds = _Heads(ds_ref[h] for h in heads)
        dout = _Heads(do_ref[:, _head_cols(h)] for h in heads)
        qn, kn, u, w, dec, kk, qk = z["qn"], z["kn"], z["u"], z["w"], z["dec"], z["kk"], z["qk"]
        bcol, gam_col, e_col, gam_last = z["bcol"], z["gam_col"], z["e_col"], z["gam_last"]
        low = lambda a: jnp.where(z["lower"], a, 0.0)
        strict = lambda a: jnp.where(z["strict"], a, 0.0)
        rowsum = lambda a: _hmap(_rowsum, a)
        colsum = lambda a: _hmap(_colsum, a)
        v_new = u - _hmap(_bd(NN), w, s0)
        dv_new = _hmap(_bd(TN), z["aqk"], dout) + _hmap(_bd(NN), z["kt"], ds)
        daqk = _hmap(low, _hmap(_bd(NT), dout, v_new))
        dqt = _hmap(_bd(NT), dout, s0)
        dkt = _hmap(_bd(NT), v_new, ds)
        dgam_last = _hmap(lambda a, b: jnp.sum(a * b, keepdims=True), ds, s0)
        ds_new = _hmap(_bd(TN), z["qt"], dout) + ds * gam_last - _hmap(_bd(TN), w, dv_new)
        dw = -_hmap(_bd(NT), dv_new, s0)
        hd_t = lambda a, b: _hdot(a, b, TN)
        dru = _hmap(hd_t, z["tinv"], dv_new)
        drw = _hmap(hd_t, z["tinv"], dw)
        dal = -_hmap(strict, _hmap(_bd(NT), dru, u) + _hmap(_bd(NT), drw, w))
        t1 = dal * kk * dec
        dkk = dal * bcol * dec
        ddec = dal * bcol * kk + daqk * qk
        dqk = daqk * dec
        s_w = rowsum(drw * kn)
        dbeta_col = rowsum(t1) + rowsum(dru * v) + gam_col * s_w
        dkn = (drw * (bcol * gam_col) + _hmap(_bd(NN), dkk, kn) + _hmap(_bd(TN), dkk, kn) + _hmap(_bd(TN), dqk, qn)
               + dkt * e_col)
        dqn = _hmap(_bd(NN), dqk, kn) + dqt * gam_col
        e_mat = ddec * dec
        de_col = rowsum(dkt * kn)
        diag_of_colsum = rowsum(_hmap(lambda a: jnp.where(z["eye"], jnp.broadcast_to(_colsum(a), (c, c)), 0.0), e_mat))
        dg_cum = rowsum(e_mat) + (bcol * s_w + rowsum(dqt * qn)) * gam_col - de_col * e_col - diag_of_colsum
        dg_last = colsum(de_col * e_col) + dgam_last * gam_last
        dg = colsum(_hmap(lambda a: jnp.where(z["lower"], a, 0.0), dg_cum)) + dg_last
        dbeta = colsum(_hmap(lambda a: jnp.where(z["eye"], a, 0.0), dbeta_col))
        rq, rk = z["rq"], z["rk"]
        dqr = z["scale"] * (rq * dqn - qr * (rq * rq * rq) * rowsum(qr * dqn))
        dkr = rk * dkn - kr * (rk * rk * rk) * rowsum(kr * dkn)
        dv = dru * bcol
        for h in heads:
            ds_ref[h] = ds_new.v[h]
            dg_ref[0, h] = dg.v[h]
            db_ref[0, h] = dbeta.v[h]
            dqkv_ref[0, :, _head_cols(h)] = dqr.v[h]
            dqkv_ref[1, :, _head_cols(h)] = dkr.v[h]
            dqkv_ref[2, :, _head_cols(h)] = dv.v[h]

    rev = lambda i: n - 1 - i
    gspec = pl.BlockSpec((1, HEADS, 1, CHUNK), lambda i: (rev(i), 0, 0, 0))
    return pl.pallas_call(
        body, name=name, grid=(n,),
        in_specs=[pl.BlockSpec((3, CHUNK, WIDTH), lambda i: (0, rev(i), 0)), gspec, gspec,
                  pl.BlockSpec((1, HEADS, d, d), lambda i: (rev(i), 0, 0, 0)),
                  pl.BlockSpec((1, HEADS, CHUNK, CHUNK), lambda i: (rev(i), 0, 0, 0)),
                  pl.BlockSpec((CHUNK, WIDTH), lambda i: (rev(i), 0))],
        out_specs=[pl.BlockSpec((3, CHUNK, WIDTH), lambda i: (0, rev(i), 0)), gspec, gspec],
        out_shape=[jax.ShapeDtypeStruct((3, t, WIDTH), F32), jax.ShapeDtypeStruct((n, HEADS, 1, CHUNK), F32),
                   jax.ShapeDtypeStruct((n, HEADS, 1, CHUNK), F32)],
        scratch_shapes=[pltpu.VMEM((HEADS, d, d), F32)],
        compiler_params=_params(("arbitrary",)),
    )(qkv, g4, b4, s0_all, tinv_all, do)


SB_BLOCK = 256


def _dot01_2(x, m01):
    hi = x.astype(BF16)
    lo = (x - hi.astype(F32)).astype(BF16)
    return _bdot(hi, m01, NN) + _bdot(lo, m01, NN)


def _sb_weights(q, kb, carry, mask, upper):
    z = _bdot(q, kb, NT)
    ls = jnp.minimum(z, 0.0) - jnp.log(1.0 + jnp.exp(-jnp.abs(z)))
    ln = jnp.where(mask, ls - z, 0.0)
    a = jnp.where(mask, jnp.exp(ls + _dot01_2(ln, upper) + carry), 0.0)
    return z, ln, a


SB_DEAD = -105.0


def _sb_alive(s, i, carry):
    return (s <= i) & (jnp.max(carry) > SB_DEAD)


def _sb_masks(i, jb, blk):
    ri = lax.broadcasted_iota(jnp.int32, (blk, blk), 0)
    ci = lax.broadcasted_iota(jnp.int32, (blk, blk), 1)
    return (jb * blk + ci) < (i * blk + ri)


def sb_fwd(q, k, v, *, name):
    t = q.shape[0]
    blk = min(SB_BLOCK, t)
    d = HEAD_DIM

    def body(q_ref, k_ref, v_ref, o_ref):
        i = pl.program_id(1)
        qb = q_ref[...]
        ri = lax.broadcasted_iota(jnp.int32, (blk, blk), 0)
        ci = lax.broadcasted_iota(jnp.int32, (blk, blk), 1)
        upper = (ri > ci).astype(BF16)

        def step(state):
            s, c, acc = state
            jb = i - s
            rows = pl.ds(pl.multiple_of(jb * blk, blk), blk)
            _, ln, a = _sb_weights(qb, k_ref[rows, :], c, _sb_masks(i, jb, blk), upper)
            return s + 1, c + _rowsum(ln), acc + _bdot(a, v_ref[rows, :], NN)

        _, _, acc = lax.while_loop(lambda st: _sb_alive(st[0], i, st[1]), step,
                                   (jnp.int32(0), jnp.zeros((blk, 1), F32), jnp.zeros((blk, d), F32)))
        o_ref[...] = acc.astype(o_ref.dtype)

    qspec = pl.BlockSpec((blk, d), lambda h, i: (i, h))
    kspec = pl.BlockSpec((t, d), lambda h, i: (0, h))
    return pl.pallas_call(
        body, name=name, grid=(HEADS, t // blk), in_specs=[qspec, kspec, kspec], out_specs=qspec,
        out_shape=jax.ShapeDtypeStruct((t, WIDTH), BF16),
        compiler_params=_params(("parallel", "arbitrary")),
    )(q, k, v)


def sb_bwd(q, k, v, do, dk0, dv0, *, name):
    t = q.shape[0]
    blk = min(SB_BLOCK, t)
    d = HEAD_DIM
    nb = t // blk

    def body(q_ref, k_ref, v_ref, do_ref, dk0_ref, dv0_ref, dq_ref, dk_ref, dv_ref, p_buf, z_buf):
        i = pl.program_id(1)

        @pl.when(i == 0)
        def _():
            dk_ref[...] = dk0_ref[...]
            dv_ref[...] = dv0_ref[...]

        qb, dob = q_ref[...], do_ref[...]
        ri = lax.broadcasted_iota(jnp.int32, (blk, blk), 0)
        ci = lax.broadcasted_iota(jnp.int32, (blk, blk), 1)
        upper = (ri > ci).astype(BF16)
        lower = (ri < ci).astype(BF16)

        def right_to_left(state):
            s, c = state
            jb = i - s
            rows = pl.ds(pl.multiple_of(jb * blk, blk), blk)
            z, ln, a = _sb_weights(qb, k_ref[rows, :], c, _sb_masks(i, jb, blk), upper)
            p_buf[jb] = a * _bdot(dob, v_ref[rows, :], NT)
            z_buf[jb] = z
            dv_ref[rows, :] += _bdot(a, dob, TN)
            return s + 1, c + _rowsum(ln)

        n_done, _ = lax.while_loop(lambda st: _sb_alive(st[0], i, st[1]), right_to_left,
                                   (jnp.int32(0), jnp.zeros((blk, 1), F32)))

        def left_to_right(jb, carry):
            cp, dq = carry
            rows = pl.ds(pl.multiple_of(jb * blk, blk), blk)
            p = p_buf[jb]
            sg = _sigmoid(z_buf[jb])
            dz = jnp.where(_sb_masks(i, jb, blk), p * (1.0 - sg) - sg * (_dot01_2(p, lower) + cp), 0.0)
            dk_ref[rows, :] += _bdot(dz, qb, TN)
            return cp + _rowsum(p), dq + _bdot(dz, k_ref[rows, :], NN)

        _, dq = lax.fori_loop(i + 1 - n_done, i + 1, left_to_right,
                              (jnp.zeros((blk, 1), F32), jnp.zeros((blk, d), F32)))
        dq_ref[...] = dq

    qspec = pl.BlockSpec((blk, d), lambda h, i: (i, h))
    kspec = pl.BlockSpec((t, d), lambda h, i: (0, h))
    s = jax.ShapeDtypeStruct((t, WIDTH), F32)
    return pl.pallas_call(
        body, name=name, grid=(HEADS, nb), in_specs=[qspec, kspec, kspec, qspec, kspec, kspec],
        out_specs=[qspec, kspec, kspec], out_shape=[s, s, s],
        scratch_shapes=[pltpu.VMEM((nb, blk, blk), F32), pltpu.VMEM((nb, blk, blk), F32)],
        compiler_params=_params(("parallel", "arbitrary")),
    )(q, k, v, do, dk0, dv0)


PACK_COLS = 1024
ANY = pl.BlockSpec(memory_space=pl.ANY)


def _mesh_pos():
    return lax.axis_index("x"), lax.axis_index("y"), lax.axis_index("c")


def _other_chips(x, y):
    return [(1 - x, y), (x, 1 - y), (1 - x, 1 - y)]


def all_gather_chips(slots, *, name):
    _, r, w = slots.shape
    rh = r // 2

    def body(_, o_ref, send_sems, recv_sems):
        x, y, c = _mesh_pos()
        me = 2 * x + y
        chips = _other_chips(x, y)
        half = lambda cc: pl.ds(cc * rh, rh)

        def copy(k, chip, hf, to):
            rows = o_ref.at[chip, half(hf)]
            return pltpu.make_async_remote_copy(src_ref=rows, dst_ref=rows, send_sem=send_sems.at[k],
                                                recv_sem=recv_sems.at[k], device_id=to, device_id_type=MESH)

        sent = [copy(k, me, c, (cx, cy, c)) for k, (cx, cy) in enumerate(chips)]
        for cp in sent:
            cp.start()
        for k, (cx, cy) in enumerate(chips):
            copy(k, 2 * cx + cy, c, (x, y, c)).wait_recv()
            fwd = copy(3 + k, 2 * cx + cy, c, (x, y, 1 - c))
            fwd.start()
            sent.append(fwd)
        for k, (cx, cy) in enumerate(chips):
            copy(3 + k, 2 * cx + cy, 1 - c, (x, y, c)).wait_recv()
        for cp in sent:
            cp.wait_send()

    return pl.pallas_call(
        body, name=name, in_specs=[ANY], out_specs=ANY, input_output_aliases={0: 0},
        out_shape=jax.ShapeDtypeStruct(slots.shape, slots.dtype),
        scratch_shapes=[pltpu.SemaphoreType.DMA((6,)), pltpu.SemaphoreType.DMA((6,))],
    )(slots)


def sibling_swap(g5, *, name):
    def body(g_ref, o_ref, send_sem, recv_sem):
        x, y, c = _mesh_pos()
        cp = pltpu.make_async_remote_copy(src_ref=g_ref.at[1 - c], dst_ref=o_ref, send_sem=send_sem, recv_sem=recv_sem,
                                          device_id=(x, y, 1 - c), device_id_type=MESH)
        cp.start()
        cp.wait()

    return pl.pallas_call(
        body, name=name, in_specs=[ANY], out_specs=ANY, out_shape=jax.ShapeDtypeStruct(g5.shape[1:], g5.dtype),
        scratch_shapes=[pltpu.SemaphoreType.DMA, pltpu.SemaphoreType.DMA],
    )(g5)


def chip_exchange(s1, *, name):
    _, rh, w = s1.shape

    def body(s_ref, o_ref, send_sems, recv_sems):
        x, y, c = _mesh_pos()
        cps = [pltpu.make_async_remote_copy(src_ref=s_ref.at[2 * cx + cy], dst_ref=o_ref.at[k], send_sem=send_sems.at[k],
                                            recv_sem=recv_sems.at[k], device_id=(cx, cy, c), device_id_type=MESH)
               for k, (cx, cy) in enumerate(_other_chips(x, y))]
        for cp in cps:
            cp.start()
        for cp in cps:
            cp.wait()

    return pl.pallas_call(
        body, name=name, in_specs=[ANY], out_specs=ANY, out_shape=jax.ShapeDtypeStruct((3, rh, w), s1.dtype),
        scratch_shapes=[pltpu.SemaphoreType.DMA((3,)), pltpu.SemaphoreType.DMA((3,))],
    )(s1)


def sibling_merge(halves, *, name):
    def body(_, o_ref, send_sem, recv_sem):
        x, y, c = _mesh_pos()
        cp = pltpu.make_async_remote_copy(src_ref=o_ref.at[c], dst_ref=o_ref.at[c], send_sem=send_sem,
                                          recv_sem=recv_sem, device_id=(x, y, 1 - c), device_id_type=MESH)
        cp.start()
        cp.wait()

    return pl.pallas_call(
        body, name=name, in_specs=[ANY], out_specs=ANY, input_output_aliases={0: 0},
        out_shape=jax.ShapeDtypeStruct(halves.shape, halves.dtype),
        scratch_shapes=[pltpu.SemaphoreType.DMA, pltpu.SemaphoreType.DMA],
    )(halves)


def all_reduce_small(buf, *, name):
    n_dev = 8

    def body(b_ref, o_ref, recv_buf, send_sems, recv_sems):
        x, y, c = _mesh_pos()
        me = 4 * x + 2 * y + c
        pos = lambda t: (t // 4, (t // 2) % 2, t % 2)

        def copy(t, slot):
            return pltpu.make_async_remote_copy(src_ref=b_ref, dst_ref=recv_buf.at[slot], send_sem=send_sems.at[t],
                                                recv_sem=recv_sems.at[slot], device_id=pos(t), device_id_type=MESH)

        for t in range(n_dev):
            @pl.when(t != me)
            def _(t=t):
                copy(t, me).start()

        recv_buf[me] = b_ref[...]
        for t in range(n_dev):
            @pl.when(t != me)
            def _(t=t):
                copy(t, t).wait_recv()
                copy(t, me).wait_send()

        acc = recv_buf[0]
        for t in range(1, n_dev):
            acc = acc + recv_buf[t]
        o_ref[...] = acc

    return pl.pallas_call(
        body, name=name, out_shape=jax.ShapeDtypeStruct(buf.shape, F32),
        in_specs=[pl.BlockSpec(memory_space=pltpu.VMEM)], out_specs=pl.BlockSpec(memory_space=pltpu.VMEM),
        scratch_shapes=[pltpu.VMEM((n_dev,) + buf.shape, F32), pltpu.SemaphoreType.DMA((n_dev,)),
                        pltpu.SemaphoreType.DMA((n_dev,))],
    )(buf)


REDUCE_ROWS = 512


def add_selected(sel, a5, b, *, name):
    _, n, rh, w = a5.shape
    tr = REDUCE_ROWS

    def body(sel_ref, a_ref, b_ref, o_ref, ob_ref):
        s = a_ref[...] + b_ref[...]
        o_ref[...] = s
        ob_ref[...] = s.astype(BF16)

    blk = pl.BlockSpec((None, tr, w), lambda j, i, s: (j, i, 0))
    return pl.pallas_call(
        body, name=name,
        grid_spec=pltpu.PrefetchScalarGridSpec(
            num_scalar_prefetch=1, grid=(n, rh // tr),
            in_specs=[pl.BlockSpec((None, None, tr, w), lambda j, i, s: (s[0], j, i, 0)), blk], out_specs=[blk, blk]),
        out_shape=[jax.ShapeDtypeStruct((n, rh, w), F32), jax.ShapeDtypeStruct((n, rh, w), BF16)],
        compiler_params=_params(("arbitrary", "arbitrary")),
    )(sel, a5, b)


def add_chip_sums(sel, s1, b2, *, name):
    _, rh, w = s1.shape
    tr = REDUCE_ROWS

    def body(sel_ref, s_ref, b_ref, o_ref):
        o_ref[...] = ((s_ref[...] + b_ref[0].astype(F32)) + b_ref[1].astype(F32)) + b_ref[2].astype(F32)

    return pl.pallas_call(
        body, name=name,
        grid_spec=pltpu.PrefetchScalarGridSpec(
            num_scalar_prefetch=1, grid=(rh // tr,),
            in_specs=[pl.BlockSpec((None, tr, w), lambda i, s: (s[0], i, 0)), pl.BlockSpec((3, tr, w), lambda i, s: (0, i, 0))],
            out_specs=pl.BlockSpec((None, tr, w), lambda i, s: (s[1], i, 0))),
        out_shape=jax.ShapeDtypeStruct((2, rh, w), F32),
        compiler_params=_params(("arbitrary",)),
    )(sel, s1, b2)


BIG = (("gdn_w_in", 2), ("gdn_w_out", 1), ("w_kv", 1), ("sb_w_q", 1), ("sb_w_out", 1), ("ffn_w_in", 2),
       ("ffn_w_out", 1), ("ple_w_proj", 2), ("ple_w_gate", 1))
SMALL = ("ln_mix", "ln_ffn", "ln_ple", "gdn_a_log", "gdn_dt_bias", "gdn_norm", "kv_norm", "k_norm", "sb_q_norm")
WEIGHTS = ("ln_mix", "ln_ffn", "ln_ple", "gdn_w_in", "gdn_conv", "gdn_a_log", "gdn_dt_bias", "gdn_norm", "gdn_w_out",
           "kv_norm", "w_kv", "k_norm", "sb_w_q", "sb_q_norm", "sb_w_out", "ffn_w_in", "ffn_w_out", "ple_w_proj",
           "ple_w_gate")
PACK_ALIGN = 2 * 1024


def _rows_of(shape):
    return -(-math.prod(shape) // PACK_COLS)


def _pack_rows(arrs, lead):
    parts = []
    for a in arrs:
        flat = a.reshape(lead + (-1,))
        pad = _rows_of(a.shape[len(lead):]) * PACK_COLS - flat.shape[-1]
        if pad:
            flat = jnp.pad(flat, [(0, 0)] * len(lead) + [(0, pad)])
        parts.append(flat.reshape(lead + (-1, PACK_COLS)))
    out = jnp.concatenate(parts, axis=len(lead))
    pad = -out.shape[len(lead)] % PACK_ALIGN
    return jnp.pad(out, [(0, 0)] * len(lead) + [(0, pad), (0, 0)])


def _unpack_rows(buf, shapes, lead):
    out, r0 = [], 0
    for s in shapes:
        rows = _rows_of(s)
        flat = buf[(slice(None),) * len(lead) + (slice(r0, r0 + rows),)].reshape(lead + (-1,))
        out.append(flat[..., :math.prod(s)].reshape(lead + tuple(s)))
        r0 += rows
    return out


def _unshard(g, axis):
    g = jnp.moveaxis(g, 0, axis)
    s = g.shape
    return g.reshape(s[:axis] + (s[axis] * s[axis + 1],) + s[axis + 2:])


def _shard(full, axis):
    s = full.shape
    return jnp.moveaxis(full.reshape(s[:axis] + (N_CHIPS, s[axis] // N_CHIPS) + s[axis + 1:]), axis, 0)


def _to4(a):
    return a.reshape(HEADS, -1, 1, CHUNK).transpose(1, 0, 2, 3)


def _from4(a):
    return a.transpose(1, 0, 2, 3).reshape(HEADS, -1)


def _row(vec):
    flat = vec.reshape(-1)
    rows = _rows_of(flat.shape)
    return jnp.pad(flat, (0, rows * PACK_COLS - flat.shape[0])).reshape(rows, PACK_COLS)


def kernel(x, p, ln_mix, ln_ffn, ln_ple, gdn_w_in, gdn_conv, gdn_a_log, gdn_dt_bias, gdn_norm, gdn_w_out, kv_norm, w_kv, k_norm, sb_w_q, sb_q_norm, sb_w_out, ffn_w_in, ffn_w_out, ple_w_proj, ple_w_gate, loss_target, m_ln_mix, m_ln_ffn, m_ln_ple, m_gdn_w_in, m_gdn_conv, m_gdn_a_log, m_gdn_dt_bias, m_gdn_norm, m_gdn_w_out, m_kv_norm, m_w_kv, m_k_norm, m_sb_w_q, m_sb_q_norm, m_sb_w_out, m_ffn_w_in, m_ffn_w_out, m_ple_w_proj, m_ple_w_gate, v_ln_mix, v_ln_ffn, v_ln_ple, v_gdn_w_in, v_gdn_conv, v_gdn_a_log, v_gdn_dt_bias, v_gdn_norm, v_gdn_w_out, v_kv_norm, v_w_kv, v_k_norm, v_sb_w_q, v_sb_q_norm, v_sb_w_out, v_ffn_w_in, v_ffn_w_out, v_ple_w_proj, v_ple_w_gate):
    w = dict(ln_mix=ln_mix, ln_ffn=ln_ffn, ln_ple=ln_ple, gdn_w_in=gdn_w_in, gdn_conv=gdn_conv, gdn_a_log=gdn_a_log,
             gdn_dt_bias=gdn_dt_bias, gdn_norm=gdn_norm, gdn_w_out=gdn_w_out, kv_norm=kv_norm, w_kv=w_kv, k_norm=k_norm,
             sb_w_q=sb_w_q, sb_q_norm=sb_q_norm, sb_w_out=sb_w_out, ffn_w_in=ffn_w_in, ffn_w_out=ffn_w_out,
             ple_w_proj=ple_w_proj, ple_w_gate=ple_w_gate)
    mom1 = dict(ln_mix=m_ln_mix, ln_ffn=m_ln_ffn, ln_ple=m_ln_ple, gdn_w_in=m_gdn_w_in, gdn_conv=m_gdn_conv,
                gdn_a_log=m_gdn_a_log, gdn_dt_bias=m_gdn_dt_bias, gdn_norm=m_gdn_norm, gdn_w_out=m_gdn_w_out,
                kv_norm=m_kv_norm, w_kv=m_w_kv, k_norm=m_k_norm, sb_w_q=m_sb_w_q, sb_q_norm=m_sb_q_norm,
                sb_w_out=m_sb_w_out, ffn_w_in=m_ffn_w_in, ffn_w_out=m_ffn_w_out, ple_w_proj=m_ple_w_proj,
                ple_w_gate=m_ple_w_gate)
    mom2 = dict(ln_mix=v_ln_mix, ln_ffn=v_ln_ffn, ln_ple=v_ln_ple, gdn_w_in=v_gdn_w_in, gdn_conv=v_gdn_conv,
                gdn_a_log=v_gdn_a_log, gdn_dt_bias=v_gdn_dt_bias, gdn_norm=v_gdn_norm, gdn_w_out=v_gdn_w_out,
                kv_norm=v_kv_norm, w_kv=v_w_kv, k_norm=v_k_norm, sb_w_q=v_sb_w_q, sb_q_norm=v_sb_q_norm,
                sb_w_out=v_sb_w_out, ffn_w_in=v_ffn_w_in, ffn_w_out=v_ffn_w_out, ple_w_proj=v_ple_w_proj,
                ple_w_gate=v_ple_w_gate)
    depth = ln_mix.shape[0]
    n_a = gdn_w_in.shape[0]
    xi, yi, ci = _mesh_pos()
    chip = 2 * xi + yi
    sel_c = jnp.reshape(ci, (1,)).astype(jnp.int32)
    sel_chip = jnp.stack([chip, ci]).astype(jnp.int32)
    h = x[0]
    tgt = loss_target[0]
    t = h.shape[0]

    shard_shapes = [w[n].shape for n, _ in BIG]
    packed = _pack_rows([w[n].astype(BF16) for n, _ in BIG], ())
    slots = lax.dynamic_update_slice(jnp.zeros((N_CHIPS,) + packed.shape, BF16), packed[None], (chip, 0, 0))
    gathered = all_gather_chips(slots, name="all_gather_weights")
    full = {n: _unshard(g, ax) for (n, ax), g in
            zip(BIG, _unpack_rows(gathered, shard_shapes, (N_CHIPS,)))}
    conv_rows = _rows_of(gdn_conv.shape)
    small_rows = sum(_rows_of(w[n].shape) for n in SMALL)
    buf_rows = -(-(small_rows + N_CHIPS * conv_rows) // 8) * 8
    conv_buf = jnp.zeros((buf_rows, PACK_COLS), F32)
    conv_buf = lax.dynamic_update_slice(conv_buf, _row(gdn_conv) * (ci == 0).astype(F32), (chip * conv_rows, 0))
    conv_all = all_reduce_small(conv_buf, name="all_reduce_small")[:N_CHIPS * conv_rows]
    conv_full = _unshard(conv_all.reshape(N_CHIPS, -1)[:, :math.prod(gdn_conv.shape)].reshape((N_CHIPS,) + gdn_conv.shape), 2)

    saved = []
    k_sh = v_sh = None
    mid = None
    for i in range(depth):
        s = dict(h0=h)
        s["hn"] = hn = rms_fwd(h, ln_mix[i:i + 1], name="rms_fwd")
        if i < n_a:
            w_in = full["gdn_w_in"][i]
            s["w_m"], s["w_abt"] = w_in[:, :4 * WIDTH], w_in[:, 4 * WIDTH:].T
            s["proj"] = proj = matmul(hn, s["w_m"], "nn", out_dtype=BF16, name="mm_gdn_in")
            s["ab"] = ab = matmul(s["w_abt"], hn, "nt", name="mm_gdn_ab")
            s["a_log"], s["dt"] = gdn_a_log[i][:, None], gdn_dt_bias[i][:, None]
            g8, b8 = gates_fwd(ab, s["a_log"], s["dt"], name="gates_fwd")
            s["g4"], s["b4"] = _to4(g8), _to4(b8)
            s["qkv"] = qkv = conv_fwd(proj, conv_full[i], name="conv_fwd")
            s["o"], s["s0"], s["tinv"] = gdn_fwd(qkv, s["g4"], s["b4"], name="gdn_fwd")
            s["y"] = y = gatenorm_fwd(s["o"], proj, gdn_norm[i:i + 1], name="gatenorm_fwd")
            h = matmul(y, full["gdn_w_out"][i], "nn", add=h, name="mm_out")
        else:
            j = i - n_a
            s["qraw"] = qraw = matmul(hn, full["sb_w_q"][j], "nn", name="mm_sq")
            s["q"] = q = headnorm_fwd(qraw, 0, sb_q_norm[j:j + 1], HEAD_DIM ** -0.5, name="headnorm_q")
            s["o"] = o = sb_fwd(q, k_sh, v_sh, name="sb_fwd")
            h = matmul(o, full["sb_w_out"][j], "nn", add=h, name="mm_out")
        s["h1"] = h
        s["hn2"] = hn2 = rms_fwd(h, ln_ffn[i:i + 1], name="rms_fwd")
        s["gu"] = gu = matmul(hn2, full["ffn_w_in"][i], "nn", out_dtype=BF16, name="mm_ffn_in")
        s["act"] = act = swiglu_fwd(gu, name="swiglu_fwd")
        h = matmul(act, full["ffn_w_out"][i], "nn", add=h, name="mm_ffn_out")
        s["h2"] = h
        s["hn3"] = hn3 = rms_fwd(h, ln_ple[i:i + 1], name="rms_fwd")
        s["gt"] = gt = matmul(hn3, full["ple_w_gate"][i], "nn", name="mm_sq")
        s["pp"] = pp = matmul(p[i, 0], full["ple_w_proj"][i], "nn", name="mm_ple_proj")
        h = ple_fwd(h, pp, gt, name="ple_fwd")
        saved.append(s)
        if i == n_a - 1:
            mid = dict(h=h)
            mid["hk"] = hk = rms_fwd(h, kv_norm[None, :], name="rms_fwd")
            mid["kv"] = kv = matmul(hk, full["w_kv"], "nn", name="mm_kv")
            k_sh = headnorm_fwd(kv, 0, k_norm[None, :], 1.0, name="headnorm_k")
            v_sh = kv[:, WIDTH:].astype(BF16)

    dh, sq = loss_head(h, tgt, name="loss_head")
    loss = lax.psum(0.5 * jnp.sum(sq) / h.shape[1], ("x", "y", "c"))

    gw = {n: [None] * w[n].shape[0] for n in WEIGHTS if w[n].ndim >= 2 and n not in ("w_kv",)}
    dk_acc = jnp.zeros((t, WIDTH), F32)
    dv_acc = jnp.zeros((t, WIDTH), F32)
    for i in reversed(range(depth)):
        s = saved[i]
        if i == n_a - 1:
            dkraw, gw["k_norm"] = headnorm_bwd(mid["kv"], 0, k_norm[None, :], 1.0, dk_acc, name="headnorm_k_bwd")
            dkv = jnp.concatenate([dkraw, dv_acc.astype(BF16)], axis=1)
            dhk = matmul(dkv, full["w_kv"], "nt", name="mm_kv_dx")
            gw["w_kv"] = matmul(mid["hk"], dkv, "tn", name="mm_kv_dw")
            dh, gw["kv_norm"] = rms_bwd(mid["h"], kv_norm[None, :], dhk, dh, name="rms_bwd")
        dpp, dgt = ple_bwd(dh, s["pp"], s["gt"], name="ple_bwd")
        gw["ple_w_proj"][i] = matmul(p[i, 0], dpp, "tn", name="mm_ple_proj_dw")
        gw["ple_w_gate"][i] = matmul(s["hn3"], dgt, "tn", name="mm_sq_dw")
        dhn3 = matmul(dgt, full["ple_w_gate"][i], "nt", name="mm_sq_dx")
        dh, gw["ln_ple"][i] = rms_bwd(s["h2"], ln_ple[i:i + 1], dhn3, dh, name="rms_bwd")
        dact = matmul(dh, full["ffn_w_out"][i], "nt", out_dtype=BF16, name="mm_ffn_out_dx")
        gw["ffn_w_out"][i] = matmul(s["act"], dh, "tn", name="mm_ffn_out_dw")
        dgu = swiglu_bwd(s["gu"], dact, name="swiglu_bwd")
        dhn2 = matmul(dgu, full["ffn_w_in"][i], "nt", name="mm_ffn_in_dx")
        gw["ffn_w_in"][i] = matmul(s["hn2"], dgu, "tn", name="mm_ffn_in_dw")
        dh, gw["ln_ffn"][i] = rms_bwd(s["h1"], ln_ffn[i:i + 1], dhn2, dh, name="rms_bwd")
        if i < n_a:
            dy = matmul(dh, full["gdn_w_out"][i], "nt", name="mm_sq_dx")
            gw["gdn_w_out"][i] = matmul(s["y"], dh, "tn", name="mm_sq_dw")
            do, dproj, gw["gdn_norm"][i] = gatenorm_bwd(s["o"], s["proj"], gdn_norm[i:i + 1], dy, name="gatenorm_bwd")
            dqkv, dg4, db4 = gdn_bwd(s["qkv"], s["g4"], s["b4"], s["s0"], s["tinv"], do, name="gdn_bwd")
            dab, dal, ddt = gates_bwd(s["ab"], s["a_log"], s["dt"], _from4(dg4), _from4(db4), name="gates_bwd")
            gw["gdn_a_log"][i], gw["gdn_dt_bias"][i] = dal[:, 0], ddt[:, 0]
            dproj, gw["gdn_conv"][i] = conv_bwd(s["proj"], conv_full[i], dqkv, dproj, name="conv_bwd")
            dhn = matmul(dproj, s["w_m"], "nt", name="mm_gdn_in_dx")
            dhn = matmul(dab, s["w_abt"], "tn", add=dhn, name="mm_gdn_ab_dx")
            dwm = matmul(s["hn"], dproj, "tn", name="mm_gdn_in_dw")
            dwab = matmul(dab, s["hn"], "nn", name="mm_gdn_ab_dw")
            gw["gdn_w_in"][i] = jnp.concatenate([dwm, dwab.T], axis=1)
        else:
            j = i - n_a
            do = matmul(dh, full["sb_w_out"][j], "nt", out_dtype=BF16, name="mm_sb_out_dx")
            gw["sb_w_out"][j] = matmul(s["o"], dh, "tn", name="mm_sq_dw")
            dq, dk_acc, dv_acc = sb_bwd(s["q"], k_sh, v_sh, do, dk_acc, dv_acc, name="sb_bwd")
            dqraw, gw["sb_q_norm"][j] = headnorm_bwd(s["qraw"], 0, sb_q_norm[j:j + 1], HEAD_DIM ** -0.5, dq,
                                                    name="headnorm_q_bwd")
            dhn = matmul(dqraw, full["sb_w_q"][j], "nt", name="mm_sq_dx")
            gw["sb_w_q"][j] = matmul(s["hn"], dqraw, "tn", name="mm_sq_dw")
        dh, gw["ln_mix"][i] = rms_bwd(s["h0"], ln_mix[i:i + 1], dhn, dh, name="rms_bwd")
    grad_x = dh[None]

    def stacked(n):
        g = gw[n]
        if isinstance(g, list):
            g = jnp.stack([a.reshape(w[n].shape[1:]) if n in SMALL else a for a in g])
        return g

    small_buf = jnp.concatenate([_row(stacked(n)) for n in SMALL] + [_row(stacked("gdn_conv"))], axis=0)
    small_buf = jnp.pad(small_buf, ((0, buf_rows - small_buf.shape[0]), (0, 0)))
    small_sum = all_reduce_small(small_buf, name="all_reduce_small")
    grads = {}
    r0 = 0
    for n in SMALL:
        rows = _rows_of(w[n].shape)
        grads[n] = small_sum[r0:r0 + rows].reshape(-1)[:math.prod(w[n].shape)].reshape(w[n].shape)
        r0 += rows
    conv_g = small_sum[r0:r0 + N_CHIPS * conv_rows].reshape(-1)[:N_CHIPS * math.prod(gdn_conv.shape)]
    conv_g = conv_g.reshape((gdn_conv.shape[0], CONV_WIDTH, N_CHIPS, gdn_conv.shape[2]))
    grads["gdn_conv"] = lax.dynamic_index_in_dim(conv_g, chip, axis=2, keepdims=False)

    g_packed = _pack_rows([_shard(stacked(n), ax) for n, ax in BIG], (N_CHIPS,))
    rh = g_packed.shape[1] // 2
    g5 = g_packed.reshape(N_CHIPS, 2, rh, PACK_COLS).transpose(1, 0, 2, 3)
    from_sibling = sibling_swap(g5, name="sibling_swap")
    s1, s1b = add_selected(sel_c, g5, from_sibling, name="add_selected")
    from_chips = chip_exchange(s1b, name="chip_exchange")
    s2 = add_chip_sums(sel_chip, s1, from_chips, name="add_chip_sums")
    reduced = sibling_merge(s2, name="sibling_merge").reshape(2 * rh, PACK_COLS)
    for (n, _), g in zip(BIG, _unpack_rows(reduced, shard_shapes, ())):
        grads[n] = g

    delta, new_m, new_v = {}, {}, {}
    for n in WEIGHTS:
        delta[n], new_m[n], new_v[n] = adamw(w[n], grads[n], mom1[n], mom2[n], name="adamw")
    return (loss, grad_x, *[grads[n] for n in WEIGHTS], *[delta[n] for n in WEIGHTS],
            *[new_m[n] for n in WEIGHTS], *[new_v[n] for n in WEIGHTS])
```

```python
import functools
import math

import jax
import jax.numpy as jnp
from jax import lax
from jax.experimental import pallas as pl
from jax.experimental.pallas import tpu as pltpu

F32 = jnp.float32
BF16 = jnp.bfloat16
EPS = 1e-6
HEADS = 8
HEAD_DIM = 128
WIDTH = HEADS * HEAD_DIM
CHUNK = 64
CONV_WIDTH = 4
N_CHIPS = 4
ADAM_LR, ADAM_B1, ADAM_B2, ADAM_EPS, ADAM_WD, ADAM_STEP = 0.001, 0.9, 0.999, 1e-08, 0.01, 10
V7X_VMEM_BYTES = 64 * 1024 * 1024
VMEM_LIMIT = V7X_VMEM_BYTES - 8 * 1024 * 1024
HIGHEST = lax.Precision.HIGHEST
MESH = pl.DeviceIdType.MESH


def _params(sem=None):
    return pltpu.CompilerParams(dimension_semantics=sem, vmem_limit_bytes=VMEM_LIMIT)


def _pick(n, prefs):
    for t in prefs:
        if t <= n and n % t == 0:
            return t
    return n


def _bdot(a, b, dims):
    return lax.dot_general(a.astype(BF16), b.astype(BF16), (((dims[0],), (dims[1],)), ((), ())),
                           preferred_element_type=F32)


NN, NT, TN = (1, 0), (1, 1), (0, 0)


MM_TILES = (1024, 1408, 512, 256, 128)


def matmul(a, b, form, *, out_dtype=F32, add=None, name, b_chips=None, out_chips=False):
    ns = None
    if b_chips is not None:
        ns = b.shape[3]
        b_shape = (b.shape[2], N_CHIPS * ns)
    else:
        b_shape = b.shape
    if form == "nn":
        (m, k), (k2, n) = a.shape, b_shape
    elif form == "nt":
        (m, k), (n, k2) = a.shape, b_shape
    else:
        (k, m), (k2, n) = a.shape, b_shape
    assert k == k2, (a.shape, b.shape, form)
    if out_chips:
        ns = n // N_CHIPS
    tm = _pick(m, MM_TILES)
    tn = _pick(n, MM_TILES)
    tk = k if k <= 1024 else _pick(k, MM_TILES)
    if ns is not None and (form == "nn" or out_chips):
        tn = ns
    if ns is not None and form == "nt":
        tk = ns
    nk = k // tk
    if form == "tn":
        a_spec = pl.BlockSpec((tk, tm), lambda i, j, kk: (kk, i))
    else:
        a_spec = pl.BlockSpec((tm, tk), lambda i, j, kk: (i, kk))
    if b_chips is not None and form == "nn":
        b_spec = pl.BlockSpec((None, None, tk, ns), lambda i, j, kk: (j, b_chips, kk, 0))
    elif b_chips is not None:
        b_spec = pl.BlockSpec((None, None, tn, ns), lambda i, j, kk: (kk, b_chips, j, 0))
    elif form == "nt":
        b_spec = pl.BlockSpec((tn, tk), lambda i, j, kk: (j, kk))
    else:
        b_spec = pl.BlockSpec((tk, tn), lambda i, j, kk: (kk, j))
    if out_chips:
        o_spec = pl.BlockSpec((None, tm, ns), lambda i, j, kk: (j, i, 0))
    else:
        o_spec = pl.BlockSpec((tm, tn), lambda i, j, kk: (i, j))
    dims = {"nn": NN, "nt": NT, "tn": TN}[form]
    has_add = add is not None

    def body(*refs):
        a_ref, b_ref = refs[:2]
        add_ref = refs[2] if has_add else None
        o_ref = refs[2 + has_add]

        def finish(r):
            if has_add:
                r = r + add_ref[...].astype(F32)
            o_ref[...] = r.astype(out_dtype)

        part = _bdot(a_ref[...], b_ref[...], dims)
        if nk == 1:
            finish(part)
            return
        acc_ref = refs[3 + has_add]
        kk = pl.program_id(2)

        @pl.when(kk == 0)
        def _():
            acc_ref[...] = part

        @pl.when(kk > 0)
        def _():
            acc_ref[...] += part

        @pl.when(kk == nk - 1)
        def _():
            finish(acc_ref[...])

    in_specs = [a_spec, b_spec] + ([o_spec] if has_add else [])
    args = (a, b) + ((add,) if has_add else ())
    return pl.pallas_call(
        body, name=name, grid=(m // tm, n // tn, nk), in_specs=in_specs, out_specs=o_spec,
        out_shape=jax.ShapeDtypeStruct((N_CHIPS, m, ns) if out_chips else (m, n), out_dtype),
        scratch_shapes=[pltpu.VMEM((tm, tn), F32)] if nk > 1 else [],
        compiler_params=_params(("parallel", "parallel", "arbitrary")),
    )(*args)


def _const(c):
    return lambda j: c


def rowwise(fn, rows, params, outs, accs=(), *, name, tm, ncol=1):
    t = rows[0][0].shape[0]
    tm = min(tm, t)
    assert t % tm == 0
    n_rows, n_par, n_out, n_acc = len(rows), len(params), len(outs), len(accs)

    def body(*refs):
        j, i = pl.program_id(0), pl.program_id(1)
        ins = [r[...] for r in refs[:n_rows + n_par]]
        o_refs = refs[n_rows + n_par:n_rows + n_par + n_out]
        a_refs = refs[n_rows + n_par + n_out:]
        row_outs, acc_outs = fn(*ins)
        for r, val in zip(o_refs, row_outs):
            r[...] = val.astype(r.dtype)
        for r, val, spec in zip(a_refs, acc_outs, accs):
            first = (i == 0) & (j == 0) if spec[4] else (i == 0)

            @pl.when(first)
            def _(r=r, val=val):
                r[...] = val.astype(F32)

            @pl.when(jnp.logical_not(first))
            def _(r=r, val=val):
                r[...] += val.astype(F32)

    in_specs = [pl.BlockSpec((tm, w), lambda j, i, cf=cf: (i, cf(j))) for _, w, cf in rows]
    in_specs += [pl.BlockSpec((p.shape[0], w), lambda j, i, cf=cf: (0, cf(j))) for p, w, cf in params]
    out_specs = [pl.BlockSpec((tm, w), lambda j, i, cf=cf: (i, cf(j))) for _, _, w, cf in outs]
    out_specs += [pl.BlockSpec((r, w), lambda j, i, cf=cf: (0, cf(j))) for r, _, w, cf, _ in accs]
    out_shape = [jax.ShapeDtypeStruct((t, tw), dt) for tw, dt, _, _ in outs]
    out_shape += [jax.ShapeDtypeStruct((r, tw), F32) for r, tw, _, _, _ in accs]
    res = pl.pallas_call(
        body, name=name, grid=(ncol, t // tm), in_specs=in_specs, out_specs=out_specs, out_shape=out_shape,
        compiler_params=_params(("arbitrary", "arbitrary")),
    )(*[r[0] for r in rows], *[p[0] for p in params])
    return res[:n_out], res[n_out:]


def _full(arr):
    return (arr, arr.shape[1], _const(0))


def _rms(x, g):
    x = x.astype(F32)
    return x * lax.rsqrt(jnp.mean(x * x, axis=-1, keepdims=True) + EPS) * g.astype(F32)


def _sigmoid(x):
    return 1.0 / (1.0 + jnp.exp(-x))


def _silu(x):
    return x * _sigmoid(x)


def _softplus(x):
    return jnp.maximum(x, 0.0) + jnp.log(1.0 + jnp.exp(-jnp.abs(x)))


def rms_fwd(h, g, *, name):
    d = h.shape[1]
    (hn,), _ = rowwise(lambda x, gg: ((_rms(x, gg),), ()), [_full(h)], [_full(g)],
                       [(d, BF16, d, _const(0))], name=name, tm=512)
    return hn


def rms_bwd(h, g, dhn, dh_res, *, name):
    d = h.shape[1]

    def fn(x, ct, res, gg):
        _, vjp = jax.vjp(_rms, x.astype(F32), gg.astype(F32))
        dx, dg = vjp(ct.astype(F32))
        return (res.astype(F32) + dx,), (dg,)

    (dh,), (dg,) = rowwise(fn, [_full(h), _full(dhn), _full(dh_res)], [_full(g)],
                           [(d, F32, d, _const(0))], [(1, d, d, _const(0), True)], name=name, tm=256)
    return dh, dg


def _head_rms(x, g, scale):
    x = x.astype(F32)
    return x * lax.rsqrt(jnp.mean(x * x, axis=-1, keepdims=True) + EPS) * (g.astype(F32) * scale)


def headnorm_fwd(x, col0, g, scale, *, name):
    (y,), _ = rowwise(lambda a, gg: ((_head_rms(a, gg, scale),), ()),
                      [(x, HEAD_DIM, lambda j: col0 + j)], [_full(g)],
                      [(WIDTH, BF16, HEAD_DIM, lambda j: j)], name=name, tm=1024, ncol=HEADS)
    return y


def headnorm_bwd(x, col0, g, scale, dys, *, name, out_dtype=BF16):
    def fn(a, *rest):
        cts, gg = rest[:-1], rest[-1]
        ct = sum(c.astype(F32) for c in cts)
        _, vjp = jax.vjp(lambda a_, g_: _head_rms(a_, g_, scale), a.astype(F32), gg.astype(F32))
        dx, dg = vjp(ct)
        return (dx,), (dg,)

    (dx,), (dg,) = rowwise(fn, [(x, HEAD_DIM, lambda j: col0 + j)] + [(dy, HEAD_DIM, lambda j: j) for dy in dys], [_full(g)],
                           [(WIDTH, out_dtype, HEAD_DIM, lambda j: j)],
                           [(1, HEAD_DIM, HEAD_DIM, _const(0), True)], name=name, tm=1024, ncol=HEADS)
    return dx, dg


def sum_cast(parts, dtype, *, name):
    wd = parts[0].shape[1]
    (out,), _ = rowwise(lambda *a: ((sum(b.astype(F32) for b in a),), ()), [_full(a) for a in parts], [],
                        [(wd, dtype, wd, _const(0))], name=name, tm=512)
    return out


def _gatenorm(o, gate, g):
    return _head_rms(o, g, 1.0) * _silu(gate.astype(F32))


def gatenorm_fwd(o, proj, g, *, name):
    (y,), _ = rowwise(lambda a, gt, gg: ((_gatenorm(a, gt, gg),), ()),
                      [(o, HEAD_DIM, lambda j: j), (proj, HEAD_DIM, lambda j: 3 * HEADS + j)], [_full(g)],
                      [(WIDTH, BF16, HEAD_DIM, lambda j: j)], name=name, tm=1024, ncol=HEADS)
    return y


def gatenorm_bwd(o, proj, g, dy, *, name):
    def fn(a, gt, ct, gg):
        _, vjp = jax.vjp(_gatenorm, a.astype(F32), gt.astype(F32), gg.astype(F32))
        da, dgt, dg = vjp(ct.astype(F32))
        return (da, dgt), (dg,)

    (do, dproj), (dg,) = rowwise(
        fn, [(o, HEAD_DIM, lambda j: j), (proj, HEAD_DIM, lambda j: 3 * HEADS + j), (dy, HEAD_DIM, lambda j: j)],
        [_full(g)],
        [(WIDTH, F32, HEAD_DIM, lambda j: j), (4 * WIDTH, BF16, HEAD_DIM, lambda j: 3 * HEADS + j)],
        [(1, HEAD_DIM, HEAD_DIM, _const(0), True)], name=name, tm=1024, ncol=HEADS)
    return do, dproj, dg


def _swiglu(g, u):
    return _silu(g.astype(F32)) * u.astype(F32)


def swiglu_fwd(gu, *, name):
    f = gu.shape[1] // 2
    (act,), _ = rowwise(lambda g, u: ((_swiglu(g, u),), ()), [(gu, f, _const(0)), (gu, f, _const(1))], [],
                        [(f, BF16, f, _const(0))], name=name, tm=256)
    return act


def swiglu_bwd(gu, dact, *, name):
    f = gu.shape[1] // 2

    def fn(g, u, ct):
        _, vjp = jax.vjp(_swiglu, g.astype(F32), u.astype(F32))
        dg, du = vjp(ct.astype(F32))
        return (jnp.concatenate([dg.astype(BF16), du.astype(BF16)], axis=1),), ()

    (dgu,), _ = rowwise(fn, [(gu, f, _const(0)), (gu, f, _const(1)), _full(dact)], [],
                        [(2 * f, BF16, 2 * f, _const(0))], name=name, tm=256)
    return dgu


def ple_fwd(h, pp, gt, *, name):
    d = h.shape[1]
    (out,), _ = rowwise(lambda a, b, c: ((a + b * _sigmoid(c),), ()), [_full(h), _full(pp), _full(gt)], [],
                        [(d, F32, d, _const(0))], name=name, tm=512)
    return out


def ple_bwd(dh, pp, gt, *, name):
    d = dh.shape[1]

    def fn(ct, b, c):
        s = _sigmoid(c)
        return (ct * s, ct * b * s * (1.0 - s)), ()

    (dpp, dgt), _ = rowwise(fn, [_full(dh), _full(pp), _full(gt)], [],
                            [(d, BF16, d, _const(0)), (d, BF16, d, _const(0))], name=name, tm=512)
    return dpp, dgt


def loss_head(y, tgt, *, name):
    d = y.shape[1]

    def fn(a, b):
        e = a - b
        return (e * (1.0 / d),), (jnp.sum(e * e, axis=0, keepdims=True),)

    (dy,), (sq,) = rowwise(fn, [_full(y), _full(tgt)], [], [(d, F32, d, _const(0))],
                           [(1, d, d, _const(0), True)], name=name, tm=512)
    return dy, sq


def adamw(w, g, m, v, *, name):
    shape = w.shape
    cols = shape[-1]
    flat = lambda a: a.reshape(-1, cols)
    bc1 = 1.0 - ADAM_B1 ** ADAM_STEP
    bc2 = 1.0 - ADAM_B2 ** ADAM_STEP

    def fn(w_, g_, m_, v_):
        m_ = ADAM_B1 * m_ + (1.0 - ADAM_B1) * g_
        v_ = ADAM_B2 * v_ + (1.0 - ADAM_B2) * (g_ * g_)
        delta = -ADAM_LR * ((m_ / bc1) / (jnp.sqrt(v_ / bc2) + ADAM_EPS) + ADAM_WD * w_)
        return (delta, m_, v_), ()

    o = (cols, F32, cols, _const(0))
    (d_, m_, v_), _ = rowwise(fn, [_full(flat(w)), _full(flat(g)), _full(flat(m)), _full(flat(v))], [],
                              [o, o, o], name=name, tm=256)
    return d_.reshape(shape), m_.reshape(shape), v_.reshape(shape)


CONV_STRIP = 256


def _shift_down(x, d):
    if d == 0:
        return x
    rows = lax.broadcasted_iota(jnp.int32, x.shape, 0)
    return jnp.where(rows >= d, pltpu.roll(x, d, 0), 0.0)


def _shift_up(x, d):
    if d == 0:
        return x
    t = x.shape[0]
    rows = lax.broadcasted_iota(jnp.int32, x.shape, 0)
    return jnp.where(rows < t - d, pltpu.roll(x, t - d, 0), 0.0)


def _conv(x, w):
    acc = None
    for j in range(CONV_WIDTH):
        term = _shift_down(x, CONV_WIDTH - 1 - j) * w[j:j + 1, :]
        acc = term if acc is None else acc + term
    return acc


def conv_fwd(proj, w, *, name):
    t = proj.shape[0]
    per = WIDTH // CONV_STRIP

    def body(x_ref, w_ref, o_ref):
        o_ref[0] = _silu(_conv(x_ref[...].astype(F32), w_ref[...]))

    return pl.pallas_call(
        body, name=name, grid=(3 * per,),
        in_specs=[pl.BlockSpec((t, CONV_STRIP), lambda j: (0, j)), pl.BlockSpec((CONV_WIDTH, CONV_STRIP), lambda j: (0, j))],
        out_specs=pl.BlockSpec((1, t, CONV_STRIP), lambda j: (j // per, 0, j % per)),
        out_shape=jax.ShapeDtypeStruct((3, t, WIDTH), F32),
        compiler_params=_params(("parallel",)),
    )(proj, w)


def conv_bwd(proj, w, dqkv, dproj, *, name):
    t = proj.shape[0]
    per = WIDTH // CONV_STRIP

    def body(x_ref, w_ref, d_ref, _, dx_ref, dw_ref):
        x, w_ = x_ref[...].astype(F32), w_ref[...]
        c = _conv(x, w_)
        s = _sigmoid(c)
        dc = d_ref[0] * (s + c * s * (1.0 - s))
        dx = None
        for j in range(CONV_WIDTH):
            d = CONV_WIDTH - 1 - j
            term = _shift_up(dc, d) * w_[j:j + 1, :]
            dx = term if dx is None else dx + term
            dw_ref[j:j + 1, :] = jnp.sum(dc * _shift_down(x, d), axis=0, keepdims=True)
        dx_ref[...] = dx.astype(dx_ref.dtype)

    return pl.pallas_call(
        body, name=name, grid=(3 * per,),
        in_specs=[pl.BlockSpec((t, CONV_STRIP), lambda j: (0, j)), pl.BlockSpec((CONV_WIDTH, CONV_STRIP), lambda j: (0, j)),
                  pl.BlockSpec((1, t, CONV_STRIP), lambda j: (j // per, 0, j % per)), pl.BlockSpec(memory_space=pl.ANY)],
        out_specs=[pl.BlockSpec((t, CONV_STRIP), lambda j: (0, j)), pl.BlockSpec((CONV_WIDTH, CONV_STRIP), lambda j: (0, j))],
        out_shape=[jax.ShapeDtypeStruct(dproj.shape, dproj.dtype), jax.ShapeDtypeStruct((CONV_WIDTH, 3 * WIDTH), F32)],
        input_output_aliases={3: 0},
        compiler_params=_params(("parallel",)),
    )(proj, w, dqkv, dproj)


def _gdn_gates(ab, a_log, dt_bias):
    a_in, b_in = ab[:HEADS], ab[HEADS:]
    g = -jnp.exp(a_log) * _softplus(a_in + dt_bias)
    return g, _sigmoid(b_in)


def gates_fwd(ab, a_log, dt_bias, *, name):
    t = ab.shape[1]

    def body(ab_ref, al_ref, dt_ref, g_ref, b_ref):
        g_ref[...], b_ref[...] = _gdn_gates(ab_ref[...], al_ref[...], dt_ref[...])

    s = jax.ShapeDtypeStruct((HEADS, t), F32)
    return pl.pallas_call(body, name=name, out_shape=[s, s], compiler_params=_params())(ab, a_log, dt_bias)


def gates_bwd(ab, a_log, dt_bias, dg, dbeta, *, name):
    t = ab.shape[1]

    def body(ab_ref, al_ref, dt_ref, dg_ref, db_ref, dab_ref, dal_ref, ddt_ref):
        _, vjp = jax.vjp(_gdn_gates, ab_ref[...], al_ref[...], dt_ref[...])
        dab_ref[...], dal_ref[...], ddt_ref[...] = vjp((dg_ref[...], db_ref[...]))

    c = jax.ShapeDtypeStruct((HEADS, 1), F32)
    return pl.pallas_call(body, name=name, out_shape=[jax.ShapeDtypeStruct((2 * HEADS, t), F32), c, c],
                          compiler_params=_params())(ab, a_log, dt_bias, dg, dbeta)


def _split3(x):
    hi = x.astype(BF16)
    r1 = x - hi.astype(F32)
    mid = r1.astype(BF16)
    lo = (r1 - mid.astype(F32)).astype(BF16)
    return hi, mid, lo


def _dot01(x, m01):
    hi, mid, lo = _split3(x)
    m01 = m01.astype(BF16)
    return _bdot(hi, m01, NN) + _bdot(mid, m01, NN) + _bdot(lo, m01, NN)


def _hdot(a, b, dims=NN):
    a_hi, b_hi = a.astype(BF16), b.astype(BF16)
    a_lo, b_lo = (a - a_hi.astype(F32)).astype(BF16), (b - b_hi.astype(F32)).astype(BF16)
    return _bdot(a_hi, b_hi, dims) + (_bdot(a_hi, b_lo, dims) + _bdot(a_lo, b_hi, dims))


def _rowsum(x):
    return jnp.sum(x, axis=1, keepdims=True)


def _colsum(x):
    return jnp.sum(x, axis=0, keepdims=True)


class _Heads:
    def __init__(self, vals):
        self.v = list(vals)

    def _bin(self, other, f):
        if isinstance(other, _Heads):
            return _Heads(f(a, b) for a, b in zip(self.v, other.v))
        return _Heads(f(a, other) for a in self.v)

    def __add__(self, o):
        return self._bin(o, lambda a, b: a + b)

    __radd__ = __add__

    def __sub__(self, o):
        return self._bin(o, lambda a, b: a - b)

    def __rsub__(self, o):
        return self._bin(o, lambda a, b: b - a)

    def __mul__(self, o):
        return self._bin(o, lambda a, b: a * b)

    __rmul__ = __mul__

    def __neg__(self):
        return _Heads(-a for a in self.v)


def _hmap(f, *args):
    n = next(len(a.v) for a in args if isinstance(a, _Heads))
    return _Heads(f(*[a.v[h] if isinstance(a, _Heads) else a for a in args]) for h in range(n))


def _inv_unit_lower(a, eye):
    p = jnp.where(eye, 1.0, 0.0) - a
    ak = a
    for _ in range(int(math.log2(CHUNK)) - 1):
        ak = _hmap(_hdot, ak, ak)
        p = p + _hmap(_hdot, p, ak)
    return p


def _gdn_chunk(qr, kr, v, grow, brow, tinv=None):
    c = CHUNK
    ri = lax.broadcasted_iota(jnp.int32, (c, c), 0)
    ci = lax.broadcasted_iota(jnp.int32, (c, c), 1)
    eye, lower, strict = ri == ci, ri >= ci, ri > ci
    where = lambda m: (lambda a: jnp.where(m, a, 0.0))
    to_col = lambda row: _hmap(lambda r: _rowsum(jnp.where(eye, jnp.broadcast_to(r, (c, c)), 0.0)), row)
    cum_row = _hmap(lambda g: _dot01(jnp.broadcast_to(g, (8, c)), ri <= ci)[0:1], grow)
    gcol, bcol = to_col(cum_row), to_col(brow)
    glast = _hmap(lambda g: _colsum(jnp.where(ri[:, 0:1] == c - 1, g, 0.0)), gcol)
    rq = _hmap(lambda a: lax.rsqrt(_rowsum(a * a) + EPS), qr)
    rk = _hmap(lambda a: lax.rsqrt(_rowsum(a * a) + EPS), kr)
    scale = HEAD_DIM ** -0.5
    qn, kn = qr * (rq * scale), kr * rk
    dec = _hmap(lambda gc, gr: jnp.where(lower, jnp.exp(jnp.minimum(gc - gr, 0.0)), 0.0), gcol, cum_row)
    kk = _hmap(lambda a: _bdot(a, a, NT), kn)
    qk = _hmap(lambda a, b: _bdot(a, b, NT), qn, kn)
    gam_col, e_col, gam_last = _hmap(jnp.exp, gcol), _hmap(jnp.exp, glast - gcol), _hmap(jnp.exp, glast)
    if tinv is None:
        tinv = _inv_unit_lower(_hmap(where(strict), bcol * kk * dec), eye)
    u = _hmap(_hdot, tinv, v * bcol)
    w = _hmap(_hdot, tinv, kn * (bcol * gam_col))
    return dict(eye=eye, lower=lower, strict=strict, gcol=gcol, bcol=bcol, rq=rq, rk=rk, qn=qn, kn=kn,
                dec=dec, kk=kk, qk=qk, gam_col=gam_col, e_col=e_col, gam_last=gam_last, tinv=tinv, u=u, w=w,
                aqk=qk * dec, qt=qn * gam_col, kt=kn * e_col, scale=scale)


def _head_cols(h):
    return slice(h * HEAD_DIM, (h + 1) * HEAD_DIM)


def _bd(dims):
    return lambda a, b: _bdot(a, b, dims)


def gdn_fwd(qkv, g4, b4, *, name):
    t = qkv.shape[1]
    n = t // CHUNK
    d = HEAD_DIM
    heads = range(HEADS)

    def body(qkv_ref, g_ref, b_ref, o_ref, s0_ref, t_ref, s_ref):
        @pl.when(pl.program_id(0) == 0)
        def _():
            s_ref[...] = jnp.zeros_like(s_ref)

        qr, kr, v = (_Heads(qkv_ref[j, :, _head_cols(h)] for h in heads) for j in range(3))
        z = _gdn_chunk(qr, kr, v, _Heads(g_ref[0, h] for h in heads), _Heads(b_ref[0, h] for h in heads))
        s0 = _Heads(s_ref[h] for h in heads)
        v_new = z["u"] - _hmap(_bd(NN), z["w"], s0)
        o = _hmap(_bd(NN), z["qt"], s0) + _hmap(_bd(NN), z["aqk"], v_new)
        s_new = s0 * z["gam_last"] + _hmap(_bd(TN), z["kt"], v_new)
        for h in heads:
            s0_ref[0, h] = s0.v[h]
            t_ref[0, h] = z["tinv"].v[h]
            o_ref[:, _head_cols(h)] = o.v[h]
            s_ref[h] = s_new.v[h]

    gspec = pl.BlockSpec((1, HEADS, 1, CHUNK), lambda i: (i, 0, 0, 0))
    return pl.pallas_call(
        body, name=name, grid=(n,),
        in_specs=[pl.BlockSpec((3, CHUNK, WIDTH), lambda i: (0, i, 0)), gspec, gspec],
        out_specs=[pl.BlockSpec((CHUNK, WIDTH), lambda i: (i, 0)),
                   pl.BlockSpec((1, HEADS, d, d), lambda i: (i, 0, 0, 0)),
                   pl.BlockSpec((1, HEADS, CHUNK, CHUNK), lambda i: (i, 0, 0, 0))],
        out_shape=[jax.ShapeDtypeStruct((t, WIDTH), F32), jax.ShapeDtypeStruct((n, HEADS, d, d), F32),
                   jax.ShapeDtypeStruct((n, HEADS, CHUNK, CHUNK), F32)],
        scratch_shapes=[pltpu.VMEM((HEADS, d, d), F32)],
        compiler_params=_params(("arbitrary",)),
    )(qkv, g4, b4)


def gdn_bwd(qkv, g4, b4, s0_all, tinv_all, do, *, name):
    t = qkv.shape[1]
    n = t // CHUNK
    d = HEAD_DIM
    c = CHUNK
    heads = range(HEADS)

    def body(qkv_ref, g_ref, b_ref, s0_ref, t_ref, do_ref, dqkv_ref, dg_ref, db_ref, ds_ref):
        @pl.when(pl.program_id(0) == 0)
        def _():
            ds_ref[...] = jnp.zeros_like(ds_ref)

        qr, kr, v = (_Heads(qkv_ref[j, :, _head_cols(h)] for h in heads) for j in range(3))
        z = _gdn_chunk(qr, kr, v, _Heads(g_ref[0, h] for h in heads), _Heads(b_ref[0, h] for h in heads),
                       tinv=_Heads(t_ref[0, h] for h in heads))
        s0 = _Heads(s0_ref[0, h] for h in heads)
        ds = _Heads(ds_ref[h] for h in heads)
        dout = _Heads(do_ref[:, _head_cols(h)] for h in heads)
        qn, kn, u, w, dec, kk, qk = z["qn"], z["kn"], z["u"], z["w"], z["dec"], z["kk"], z["qk"]
        bcol, gam_col, e_col, gam_last = z["bcol"], z["gam_col"], z["e_col"], z["gam_last"]
        low = lambda a: jnp.where(z["lower"], a, 0.0)
        strict = lambda a: jnp.where(z["strict"], a, 0.0)
        rowsum = lambda a: _hmap(_rowsum, a)
        colsum = lambda a: _hmap(_colsum, a)
        v_new = u - _hmap(_bd(NN), w, s0)
        dv_new = _hmap(_bd(TN), z["aqk"], dout) + _hmap(_bd(NN), z["kt"], ds)
        daqk = _hmap(low, _hmap(_bd(NT), dout, v_new))
        dqt = _hmap(_bd(NT), dout, s0)
        dkt = _hmap(_bd(NT), v_new, ds)
        dgam_last = _hmap(lambda a, b: jnp.sum(a * b, keepdims=True), ds, s0)
        ds_new = _hmap(_bd(TN), z["qt"], dout) + ds * gam_last - _hmap(_bd(TN), w, dv_new)
        dw = -_hmap(_bd(NT), dv_new, s0)
        hd_t = lambda a, b: _hdot(a, b, TN)
        dru = _hmap(hd_t, z["tinv"], dv_new)
        drw = _hmap(hd_t, z["tinv"], dw)
        dal = -_hmap(strict, _hmap(_bd(NT), dru, u) + _hmap(_bd(NT), drw, w))
        t1 = dal * kk * dec
        dkk = dal * bcol * dec
        ddec = dal * bcol * kk + daqk * qk
        dqk = daqk * dec
        s_w = rowsum(drw * kn)
        dbeta_col = rowsum(t1) + rowsum(dru * v) + gam_col * s_w
        dkn = (drw * (bcol * gam_col) + _hmap(_bd(NN), dkk, kn) + _hmap(_bd(TN), dkk, kn) + _hmap(_bd(TN), dqk, qn)
               + dkt * e_col)
        dqn = _hmap(_bd(NN), dqk, kn) + dqt * gam_col
        e_mat = ddec * dec
        de_col = rowsum(dkt * kn)
        diag_of_colsum = rowsum(_hmap(lambda a: jnp.where(z["eye"], jnp.broadcast_to(_colsum(a), (c, c)), 0.0), e_mat))
        dg_cum = rowsum(e_mat) + (bcol * s_w + rowsum(dqt * qn)) * gam_col - de_col * e_col - diag_of_colsum
        dg_last = colsum(de_col * e_col) + dgam_last * gam_last
        dg = colsum(_hmap(lambda a: jnp.where(z["lower"], a, 0.0), dg_cum)) + dg_last
        dbeta = colsum(_hmap(lambda a: jnp.where(z["eye"], a, 0.0), dbeta_col))
        rq, rk = z["rq"], z["rk"]
        dqr = z["scale"] * (rq * dqn - qr * (rq * rq * rq) * rowsum(qr * dqn))
        dkr = rk * dkn - kr * (rk * rk * rk) * rowsum(kr * dkn)
        dv = dru * bcol
        for h in heads:
            ds_ref[h] = ds_new.v[h]
            dg_ref[0, h] = dg.v[h]
            db_ref[0, h] = dbeta.v[h]
            dqkv_ref[0, :, _head_cols(h)] = dqr.v[h]
            dqkv_ref[1, :, _head_cols(h)] = dkr.v[h]
            dqkv_ref[2, :, _head_cols(h)] = dv.v[h]

    rev = lambda i: n - 1 - i
    gspec = pl.BlockSpec((1, HEADS, 1, CHUNK), lambda i: (rev(i), 0, 0, 0))
    return pl.pallas_call(
        body, name=name, grid=(n,),
        in_specs=[pl.BlockSpec((3, CHUNK, WIDTH), lambda i: (0, rev(i), 0)), gspec, gspec,
                  pl.BlockSpec((1, HEADS, d, d), lambda i: (rev(i), 0, 0, 0)),
                  pl.BlockSpec((1, HEADS, CHUNK, CHUNK), lambda i: (rev(i), 0, 0, 0)),
                  pl.BlockSpec((CHUNK, WIDTH), lambda i: (rev(i), 0))],
        out_specs=[pl.BlockSpec((3, CHUNK, WIDTH), lambda i: (0, rev(i), 0)), gspec, gspec],
        out_shape=[jax.ShapeDtypeStruct((3, t, WIDTH), F32), jax.ShapeDtypeStruct((n, HEADS, 1, CHUNK), F32),
                   jax.ShapeDtypeStruct((n, HEADS, 1, CHUNK), F32)],
        scratch_shapes=[pltpu.VMEM((HEADS, d, d), F32)],
        compiler_params=_params(("arbitrary",)),
    )(qkv, g4, b4, s0_all, tinv_all, do)


SB_BLOCK = 256


def _dot01_2(x, m01):
    hi = x.astype(BF16)
    lo = (x - hi.astype(F32)).astype(BF16)
    return _bdot(hi, m01, NN) + _bdot(lo, m01, NN)


SB_HEADS = 2


def _sb_weights(q, kb, carry, mask, upper):
    z = _hmap(_bd(NT), q, kb)
    ls = _hmap(lambda z_: jnp.minimum(z_, 0.0) - jnp.log(1.0 + jnp.exp(-jnp.abs(z_))), z)
    ln = _hmap(lambda l_, z_: jnp.where(mask, l_ - z_, 0.0), ls, z)
    suffix = _hmap(lambda l_: _dot01_2(l_, upper), ln)
    a = _hmap(lambda l_, s_, c_: jnp.where(mask, jnp.exp(l_ + s_ + c_), 0.0), ls, suffix, carry)
    return z, ln, a


SB_DEAD = -105.0


def _sb_alive(s, i, carries):
    top = jnp.max(carries[0])
    for c in carries[1:]:
        top = jnp.maximum(top, jnp.max(c))
    return (s <= i) & (top > SB_DEAD)


def _sb_masks(i, jb, blk):
    ri = lax.broadcasted_iota(jnp.int32, (blk, blk), 0)
    ci = lax.broadcasted_iota(jnp.int32, (blk, blk), 1)
    return (jb * blk + ci) < (i * blk + ri)


def sb_fwd(q, k, v, *, name):
    t = q.shape[0]
    blk = min(SB_BLOCK, t)
    d = HEAD_DIM

    hs = range(SB_HEADS)

    def body(q_ref, k_ref, v_ref, o_ref):
        i = pl.program_id(1)
        qb = _Heads(q_ref[:, _head_cols(h)] for h in hs)
        ri = lax.broadcasted_iota(jnp.int32, (blk, blk), 0)
        ci = lax.broadcasted_iota(jnp.int32, (blk, blk), 1)
        upper = (ri > ci).astype(BF16)

        def step(state):
            s, cs, accs = state
            jb = i - s
            rows = pl.ds(pl.multiple_of(jb * blk, blk), blk)
            kb = _Heads(k_ref[rows, _head_cols(h)] for h in hs)
            vb = _Heads(v_ref[rows, _head_cols(h)] for h in hs)
            _, ln, a = _sb_weights(qb, kb, _Heads(cs), _sb_masks(i, jb, blk), upper)
            cs = _Heads(cs) + _hmap(_rowsum, ln)
            accs = _Heads(accs) + _hmap(_bd(NN), a, vb)
            return s + 1, tuple(cs.v), tuple(accs.v)

        init = (jnp.int32(0), tuple(jnp.zeros((blk, 1), F32) for _ in hs), tuple(jnp.zeros((blk, d), F32) for _ in hs))
        _, _, accs = lax.while_loop(lambda st: _sb_alive(st[0], i, st[1]), step, init)
        for h in hs:
            o_ref[:, _head_cols(h)] = accs[h].astype(o_ref.dtype)

    qspec = pl.BlockSpec((blk, SB_HEADS * d), lambda g, i: (i, g))
    kspec = pl.BlockSpec((t, SB_HEADS * d), lambda g, i: (0, g))
    return pl.pallas_call(
        body, name=name, grid=(HEADS // SB_HEADS, t // blk), in_specs=[qspec, kspec, kspec], out_specs=qspec,
        out_shape=jax.ShapeDtypeStruct((t, WIDTH), BF16),
        compiler_params=_params(("parallel", "arbitrary")),
    )(q, k, v)


def sb_bwd(q, k, v, do, *, name):
    t = q.shape[0]
    blk = min(SB_BLOCK, t)
    d = HEAD_DIM
    nb = t // blk
    hs = range(SB_HEADS)

    def body(q_ref, k_ref, v_ref, do_ref, dq_ref, dk_ref, dv_ref, p_buf, z_buf):
        i = pl.program_id(1)

        @pl.when(i == 0)
        def _():
            dk_ref[...] = jnp.zeros_like(dk_ref)
            dv_ref[...] = jnp.zeros_like(dv_ref)

        qb = _Heads(q_ref[:, _head_cols(h)] for h in hs)
        dob = _Heads(do_ref[:, _head_cols(h)] for h in hs)
        ri = lax.broadcasted_iota(jnp.int32, (blk, blk), 0)
        ci = lax.broadcasted_iota(jnp.int32, (blk, blk), 1)
        upper = (ri > ci).astype(BF16)
        lower = (ri < ci).astype(BF16)

        def right_to_left(state):
            s, cs = state
            jb = i - s
            rows = pl.ds(pl.multiple_of(jb * blk, blk), blk)
            kb = _Heads(k_ref[rows, _head_cols(h)] for h in hs)
            vb = _Heads(v_ref[rows, _head_cols(h)] for h in hs)
            z, ln, a = _sb_weights(qb, kb, _Heads(cs), _sb_masks(i, jb, blk), upper)
            p = a * _hmap(_bd(NT), dob, vb)
            dv = _hmap(_bd(TN), a, dob)
            for h in hs:
                p_buf[h, jb] = p.v[h]
                z_buf[h, jb] = z.v[h]
                dv_ref[rows, _head_cols(h)] += dv.v[h]
            return s + 1, tuple((_Heads(cs) + _hmap(_rowsum, ln)).v)

        n_done, _ = lax.while_loop(lambda st: _sb_alive(st[0], i, st[1]), right_to_left,
                                   (jnp.int32(0), tuple(jnp.zeros((blk, 1), F32) for _ in hs)))

        def left_to_right(jb, carry):
            cps, dqs = carry
            rows = pl.ds(pl.multiple_of(jb * blk, blk), blk)
            mask = _sb_masks(i, jb, blk)
            kb = _Heads(k_ref[rows, _head_cols(h)] for h in hs)
            p = _Heads(p_buf[h, jb] for h in hs)
            sg = _hmap(_sigmoid, _Heads(z_buf[h, jb] for h in hs))
            prefix = _hmap(lambda a: _dot01_2(a, lower), p) + _Heads(cps)
            dz = _hmap(lambda a: jnp.where(mask, a, 0.0), p * (1.0 - sg) - sg * prefix)
            dk = _hmap(_bd(TN), dz, qb)
            for h in hs:
                dk_ref[rows, _head_cols(h)] += dk.v[h]
            return tuple((_Heads(cps) + _hmap(_rowsum, p)).v), tuple((_Heads(dqs) + _hmap(_bd(NN), dz, kb)).v)

        _, dqs = lax.fori_loop(i + 1 - n_done, i + 1, left_to_right,
                               (tuple(jnp.zeros((blk, 1), F32) for _ in hs), tuple(jnp.zeros((blk, d), F32) for _ in hs)))
        for h in hs:
            dq_ref[:, _head_cols(h)] = dqs[h]

    qspec = pl.BlockSpec((blk, SB_HEADS * d), lambda g, i: (i, g))
    kspec = pl.BlockSpec((t, SB_HEADS * d), lambda g, i: (0, g))
    s = jax.ShapeDtypeStruct((t, WIDTH), F32)
    buf = pltpu.VMEM((SB_HEADS, nb, blk, blk), F32)
    return pl.pallas_call(
        body, name=name, grid=(HEADS // SB_HEADS, nb), in_specs=[qspec, kspec, kspec, qspec],
        out_specs=[qspec, kspec, kspec], out_shape=[s, s, s], scratch_shapes=[buf, buf],
        compiler_params=_params(("parallel", "arbitrary")),
    )(q, k, v, do)


PACK_COLS = 1024
ANY = pl.BlockSpec(memory_space=pl.ANY)


def _mesh_pos():
    return lax.axis_index("x"), lax.axis_index("y"), lax.axis_index("c")


def _other_chips(x, y):
    return [(1 - x, y), (x, 1 - y), (1 - x, 1 - y)]


def _dma_sems(n):
    return [pltpu.SemaphoreType.DMA((n,)), pltpu.SemaphoreType.DMA((n,))]


def all_gather_chips(slots, *, name):
    nb = len(slots)

    def body(*refs):
        o_refs, (send_sems, recv_sems) = refs[nb:2 * nb], refs[2 * nb:]
        x, y, c = _mesh_pos()
        me = 2 * x + y
        chips = _other_chips(x, y)

        def copy(b, k, chip, hf, to):
            rows = o_refs[b].at[chip, hf]
            return pltpu.make_async_remote_copy(src_ref=rows, dst_ref=rows, send_sem=send_sems.at[6 * b + k],
                                                recv_sem=recv_sems.at[6 * b + k], device_id=to, device_id_type=MESH)

        sent = [copy(b, k, me, c, (cx, cy, c)) for b in range(nb) for k, (cx, cy) in enumerate(chips)]
        for cp in sent:
            cp.start()
        for b in range(nb):
            for k, (cx, cy) in enumerate(chips):
                copy(b, k, 2 * cx + cy, c, (x, y, c)).wait_recv()
                fwd = copy(b, 3 + k, 2 * cx + cy, c, (x, y, 1 - c))
                fwd.start()
                sent.append(fwd)
        for b in range(nb):
            for k, (cx, cy) in enumerate(chips):
                copy(b, 3 + k, 2 * cx + cy, 1 - c, (x, y, c)).wait_recv()
        for cp in sent:
            cp.wait_send()

    return pl.pallas_call(
        body, name=name, in_specs=[ANY] * nb, out_specs=[ANY] * nb, input_output_aliases={b: b for b in range(nb)},
        out_shape=[jax.ShapeDtypeStruct(s.shape, s.dtype) for s in slots], scratch_shapes=_dma_sems(6 * nb),
    )(*slots)


def sibling_swap(gs, *, name):
    nb = len(gs)

    def body(*refs):
        g_refs, o_refs, (send_sems, recv_sems) = refs[:nb], refs[nb:2 * nb], refs[2 * nb:]
        x, y, c = _mesh_pos()
        cps = [pltpu.make_async_remote_copy(src_ref=g_refs[b].at[j, 1 - c], dst_ref=o_refs[b].at[j],
                                            send_sem=send_sems.at[N_CHIPS * b + j], recv_sem=recv_sems.at[N_CHIPS * b + j],
                                            device_id=(x, y, 1 - c), device_id_type=MESH)
               for b in range(nb) for j in range(N_CHIPS)]
        for cp in cps:
            cp.start()
        for cp in cps:
            cp.wait()

    return pl.pallas_call(
        body, name=name, in_specs=[ANY] * nb, out_specs=[ANY] * nb,
        out_shape=[jax.ShapeDtypeStruct((g.shape[0],) + g.shape[2:], g.dtype) for g in gs],
        scratch_shapes=_dma_sems(N_CHIPS * nb),
    )(*gs)


def chip_exchange(s1s, *, name):
    nb = len(s1s)

    def body(*refs):
        s_refs, o_refs, (send_sems, recv_sems) = refs[:nb], refs[nb:2 * nb], refs[2 * nb:]
        x, y, c = _mesh_pos()
        cps = [pltpu.make_async_remote_copy(src_ref=s_refs[b].at[2 * cx + cy], dst_ref=o_refs[b].at[k],
                                            send_sem=send_sems.at[3 * b + k], recv_sem=recv_sems.at[3 * b + k],
                                            device_id=(cx, cy, c), device_id_type=MESH)
               for b in range(nb) for k, (cx, cy) in enumerate(_other_chips(x, y))]
        for cp in cps:
            cp.start()
        for cp in cps:
            cp.wait()

    return pl.pallas_call(
        body, name=name, in_specs=[ANY] * nb, out_specs=[ANY] * nb,
        out_shape=[jax.ShapeDtypeStruct((3,) + s.shape[1:], s.dtype) for s in s1s], scratch_shapes=_dma_sems(3 * nb),
    )(*s1s)


def sibling_merge(halves, *, name):
    nb = len(halves)

    def body(*refs):
        o_refs, (send_sems, recv_sems) = refs[nb:2 * nb], refs[2 * nb:]
        x, y, c = _mesh_pos()
        cps = [pltpu.make_async_remote_copy(src_ref=o_refs[b].at[c], dst_ref=o_refs[b].at[c], send_sem=send_sems.at[b],
                                            recv_sem=recv_sems.at[b], device_id=(x, y, 1 - c), device_id_type=MESH)
               for b in range(nb)]
        for cp in cps:
            cp.start()
        for cp in cps:
            cp.wait()

    return pl.pallas_call(
        body, name=name, in_specs=[ANY] * nb, out_specs=[ANY] * nb, input_output_aliases={b: b for b in range(nb)},
        out_shape=[jax.ShapeDtypeStruct(h.shape, h.dtype) for h in halves], scratch_shapes=_dma_sems(nb),
    )(*halves)


def all_reduce_small(buf, *, name):
    n_dev = 8

    def body(b_ref, o_ref, recv_buf, send_sems, recv_sems):
        x, y, c = _mesh_pos()
        me = 4 * x + 2 * y + c
        pos = lambda t: (t // 4, (t // 2) % 2, t % 2)

        def copy(t, slot):
            return pltpu.make_async_remote_copy(src_ref=b_ref, dst_ref=recv_buf.at[slot], send_sem=send_sems.at[t],
                                                recv_sem=recv_sems.at[slot], device_id=pos(t), device_id_type=MESH)

        for t in range(n_dev):
            @pl.when(t != me)
            def _(t=t):
                copy(t, me).start()

        recv_buf[me] = b_ref[...]
        for t in range(n_dev):
            @pl.when(t != me)
            def _(t=t):
                copy(t, t).wait_recv()
                copy(t, me).wait_send()

        acc = recv_buf[0]
        for t in range(1, n_dev):
            acc = acc + recv_buf[t]
        o_ref[...] = acc

    return pl.pallas_call(
        body, name=name, out_shape=jax.ShapeDtypeStruct(buf.shape, F32),
        in_specs=[pl.BlockSpec(memory_space=pltpu.VMEM)], out_specs=pl.BlockSpec(memory_space=pltpu.VMEM),
        scratch_shapes=[pltpu.VMEM((n_dev,) + buf.shape, F32), pltpu.SemaphoreType.DMA((n_dev,)),
                        pltpu.SemaphoreType.DMA((n_dev,))],
    )(buf)


REDUCE_ROWS = (512, 384, 256, 128)


def add_selected(sel, a5, b, *, name):
    n, _, rh, w = a5.shape
    tr = _pick(rh, REDUCE_ROWS)

    def body(sel_ref, a_ref, b_ref, o_ref, ob_ref):
        s = a_ref[...] + b_ref[...]
        o_ref[...] = s
        ob_ref[...] = s.astype(BF16)

    blk = pl.BlockSpec((None, tr, w), lambda j, i, s: (j, i, 0))
    return pl.pallas_call(
        body, name=name,
        grid_spec=pltpu.PrefetchScalarGridSpec(
            num_scalar_prefetch=1, grid=(n, rh // tr),
            in_specs=[pl.BlockSpec((None, None, tr, w), lambda j, i, s: (j, s[0], i, 0)), blk], out_specs=[blk, blk]),
        out_shape=[jax.ShapeDtypeStruct((n, rh, w), F32), jax.ShapeDtypeStruct((n, rh, w), BF16)],
        compiler_params=_params(("arbitrary", "arbitrary")),
    )(sel, a5, b)


def add_chip_sums(sel, s1, b2, *, name):
    _, rh, w = s1.shape
    tr = _pick(rh, REDUCE_ROWS)

    def body(sel_ref, s_ref, b_ref, o_ref):
        o_ref[...] = ((s_ref[...] + b_ref[0].astype(F32)) + b_ref[1].astype(F32)) + b_ref[2].astype(F32)

    return pl.pallas_call(
        body, name=name,
        grid_spec=pltpu.PrefetchScalarGridSpec(
            num_scalar_prefetch=1, grid=(rh // tr,),
            in_specs=[pl.BlockSpec((None, tr, w), lambda i, s: (s[0], i, 0)), pl.BlockSpec((3, tr, w), lambda i, s: (0, i, 0))],
            out_specs=pl.BlockSpec((None, tr, w), lambda i, s: (s[1], i, 0))),
        out_shape=jax.ShapeDtypeStruct((2, rh, w), F32),
        compiler_params=_params(("arbitrary",)),
    )(sel, s1, b2)


BIG = (("gdn_w_out", 1), ("sb_w_q", 1), ("sb_w_out", 1), ("ffn_w_out", 1), ("ple_w_gate", 1), ("gdn_w_in", 2),
       ("w_kv", 1), ("ple_w_proj", 2))
SMALL = ("ln_mix", "ln_ffn", "ln_ple", "gdn_a_log", "gdn_dt_bias", "gdn_norm", "kv_norm", "k_norm", "sb_q_norm")
WEIGHTS = ("ln_mix", "ln_ffn", "ln_ple", "gdn_w_in", "gdn_conv", "gdn_a_log", "gdn_dt_bias", "gdn_norm", "gdn_w_out",
           "kv_norm", "w_kv", "k_norm", "sb_w_q", "sb_q_norm", "sb_w_out", "ffn_w_in", "ffn_w_out", "ple_w_proj",
           "ple_w_gate")
PACK_ALIGN = 256


def _rows_of(shape):
    return -(-math.prod(shape) // PACK_COLS)


def _pack_rows(arrs, lead):
    parts = []
    for a in arrs:
        if a.shape[-1] == PACK_COLS:
            parts.append(a.reshape(lead + (-1, PACK_COLS)))
            continue
        flat = a.reshape(lead + (-1,))
        pad = _rows_of(a.shape[len(lead):]) * PACK_COLS - flat.shape[-1]
        if pad:
            flat = jnp.pad(flat, [(0, 0)] * len(lead) + [(0, pad)])
        parts.append(flat.reshape(lead + (-1, PACK_COLS)))
    rows = sum(q.shape[len(lead)] for q in parts)
    filler = -rows % PACK_ALIGN
    if filler:
        parts.append(jnp.zeros(lead + (filler, PACK_COLS), parts[0].dtype))
    return jnp.concatenate(parts, axis=len(lead))


def _own_slot(buf, chip):
    slots = lax.dynamic_update_slice(lax.empty((N_CHIPS,) + buf.shape, buf.dtype), buf[None], (chip, 0, 0))
    return slots.reshape(N_CHIPS, 2, buf.shape[0] // 2, buf.shape[1])


def _unpack_rows(buf, shapes, lead):
    out, r0 = [], 0
    for s in shapes:
        rows = _rows_of(s)
        flat = buf[(slice(None),) * len(lead) + (slice(r0, r0 + rows),)].reshape(lead + (-1,))
        out.append(flat[..., :math.prod(s)].reshape(lead + tuple(s)))
        r0 += rows
    return out


def _unshard(g, axis):
    g = jnp.moveaxis(g, 0, axis)
    s = g.shape
    return g.reshape(s[:axis] + (s[axis] * s[axis + 1],) + s[axis + 2:])


def _shard(full, axis):
    s = full.shape
    return jnp.moveaxis(full.reshape(s[:axis] + (N_CHIPS, s[axis] // N_CHIPS) + s[axis + 1:]), axis, 0)


def _to4(a):
    return a.reshape(HEADS, -1, 1, CHUNK).transpose(1, 0, 2, 3)


def _from4(a):
    return a.transpose(1, 0, 2, 3).reshape(HEADS, -1)


def _row(vec):
    flat = vec.reshape(-1)
    rows = _rows_of(flat.shape)
    return jnp.pad(flat, (0, rows * PACK_COLS - flat.shape[0])).reshape(rows, PACK_COLS)


def kernel(x, p, ln_mix, ln_ffn, ln_ple, gdn_w_in, gdn_conv, gdn_a_log, gdn_dt_bias, gdn_norm, gdn_w_out, kv_norm, w_kv, k_norm, sb_w_q, sb_q_norm, sb_w_out, ffn_w_in, ffn_w_out, ple_w_proj, ple_w_gate, loss_target, m_ln_mix, m_ln_ffn, m_ln_ple, m_gdn_w_in, m_gdn_conv, m_gdn_a_log, m_gdn_dt_bias, m_gdn_norm, m_gdn_w_out, m_kv_norm, m_w_kv, m_k_norm, m_sb_w_q, m_sb_q_norm, m_sb_w_out, m_ffn_w_in, m_ffn_w_out, m_ple_w_proj, m_ple_w_gate, v_ln_mix, v_ln_ffn, v_ln_ple, v_gdn_w_in, v_gdn_conv, v_gdn_a_log, v_gdn_dt_bias, v_gdn_norm, v_gdn_w_out, v_kv_norm, v_w_kv, v_k_norm, v_sb_w_q, v_sb_q_norm, v_sb_w_out, v_ffn_w_in, v_ffn_w_out, v_ple_w_proj, v_ple_w_gate):
    w = dict(ln_mix=ln_mix, ln_ffn=ln_ffn, ln_ple=ln_ple, gdn_w_in=gdn_w_in, gdn_conv=gdn_conv, gdn_a_log=gdn_a_log,
             gdn_dt_bias=gdn_dt_bias, gdn_norm=gdn_norm, gdn_w_out=gdn_w_out, kv_norm=kv_norm, w_kv=w_kv, k_norm=k_norm,
             sb_w_q=sb_w_q, sb_q_norm=sb_q_norm, sb_w_out=sb_w_out, ffn_w_in=ffn_w_in, ffn_w_out=ffn_w_out,
             ple_w_proj=ple_w_proj, ple_w_gate=ple_w_gate)
    mom1 = dict(ln_mix=m_ln_mix, ln_ffn=m_ln_ffn, ln_ple=m_ln_ple, gdn_w_in=m_gdn_w_in, gdn_conv=m_gdn_conv,
                gdn_a_log=m_gdn_a_log, gdn_dt_bias=m_gdn_dt_bias, gdn_norm=m_gdn_norm, gdn_w_out=m_gdn_w_out,
                kv_norm=m_kv_norm, w_kv=m_w_kv, k_norm=m_k_norm, sb_w_q=m_sb_w_q, sb_q_norm=m_sb_q_norm,
                sb_w_out=m_sb_w_out, ffn_w_in=m_ffn_w_in, ffn_w_out=m_ffn_w_out, ple_w_proj=m_ple_w_proj,
                ple_w_gate=m_ple_w_gate)
    mom2 = dict(ln_mix=v_ln_mix, ln_ffn=v_ln_ffn, ln_ple=v_ln_ple, gdn_w_in=v_gdn_w_in, gdn_conv=v_gdn_conv,
                gdn_a_log=v_gdn_a_log, gdn_dt_bias=v_gdn_dt_bias, gdn_norm=v_gdn_norm, gdn_w_out=v_gdn_w_out,
                kv_norm=v_kv_norm, w_kv=v_w_kv, k_norm=v_k_norm, sb_w_q=v_sb_w_q, sb_q_norm=v_sb_q_norm,
                sb_w_out=v_sb_w_out, ffn_w_in=v_ffn_w_in, ffn_w_out=v_ffn_w_out, ple_w_proj=v_ple_w_proj,
                ple_w_gate=v_ple_w_gate)
    depth = ln_mix.shape[0]
    n_a = gdn_w_in.shape[0]
    xi, yi, ci = _mesh_pos()
    chip = 2 * xi + yi
    sel_c = jnp.reshape(ci, (1,)).astype(jnp.int32)
    sel_chip = jnp.stack([chip, ci]).astype(jnp.int32)
    h = x[0]
    tgt = loss_target[0]
    t = h.shape[0]

    shard_shapes = [w[n].shape for n, _ in BIG]
    packed = _pack_rows([w[n].astype(BF16) for n, _ in BIG], ())
    ffn_in_shard = ffn_w_in.astype(BF16).reshape(-1, ffn_w_in.shape[2])
    gathered, ffn_in_all = all_gather_chips([_own_slot(packed, chip), _own_slot(ffn_in_shard, chip)],
                                            name="all_gather_weights")
    gathered = gathered.reshape(N_CHIPS, -1, PACK_COLS)
    ffn_in_all = ffn_in_all.reshape((N_CHIPS,) + ffn_w_in.shape)
    full = {n: _unshard(g, ax) for (n, ax), g in
            zip(BIG, _unpack_rows(gathered, shard_shapes, (N_CHIPS,)))}
    conv_rows = _rows_of(gdn_conv.shape)
    small_rows = sum(_rows_of(w[n].shape) for n in SMALL)
    buf_rows = -(-(small_rows + N_CHIPS * conv_rows) // 8) * 8
    conv_buf = jnp.zeros((buf_rows, PACK_COLS), F32)
    conv_buf = lax.dynamic_update_slice(conv_buf, _row(gdn_conv) * (ci == 0).astype(F32), (chip * conv_rows, 0))
    conv_all = all_reduce_small(conv_buf, name="all_reduce_small")[:N_CHIPS * conv_rows]
    conv_full = _unshard(conv_all.reshape(N_CHIPS, -1)[:, :math.prod(gdn_conv.shape)].reshape((N_CHIPS,) + gdn_conv.shape), 2)

    saved = []
    k_sh = v_sh = None
    mid = None
    for i in range(depth):
        s = dict(h0=h)
        s["hn"] = hn = rms_fwd(h, ln_mix[i:i + 1], name="rms_fwd")
        if i < n_a:
            w_in = full["gdn_w_in"][i]
            s["w_m"], s["w_abt"] = w_in[:, :4 * WIDTH], w_in[:, 4 * WIDTH:].T
            s["proj"] = proj = matmul(hn, s["w_m"], "nn", out_dtype=BF16, name="mm_gdn_in")
            s["ab"] = ab = matmul(s["w_abt"], hn, "nt", name="mm_gdn_ab")
            s["a_log"], s["dt"] = gdn_a_log[i][:, None], gdn_dt_bias[i][:, None]
            g8, b8 = gates_fwd(ab, s["a_log"], s["dt"], name="gates_fwd")
            s["g4"], s["b4"] = _to4(g8), _to4(b8)
            s["qkv"] = qkv = conv_fwd(proj, conv_full[i], name="conv_fwd")
            s["o"], s["s0"], s["tinv"] = gdn_fwd(qkv, s["g4"], s["b4"], name="gdn_fwd")
            s["y"] = y = gatenorm_fwd(s["o"], proj, gdn_norm[i:i + 1], name="gatenorm_fwd")
            h = matmul(y, full["gdn_w_out"][i], "nn", add=h, name="mm_out")
        else:
            j = i - n_a
            s["qraw"] = qraw = matmul(hn, full["sb_w_q"][j], "nn", name="mm_sq")
            s["q"] = q = headnorm_fwd(qraw, 0, sb_q_norm[j:j + 1], HEAD_DIM ** -0.5, name="headnorm_q")
            s["o"] = o = sb_fwd(q, k_sh, v_sh, name="sb_fwd")
            h = matmul(o, full["sb_w_out"][j], "nn", add=h, name="mm_out")
        s["h1"] = h
        s["hn2"] = hn2 = rms_fwd(h, ln_ffn[i:i + 1], name="rms_fwd")
        s["gu"] = gu = matmul(hn2, ffn_in_all, "nn", b_chips=i, out_dtype=BF16, name="mm_ffn_in")
        s["act"] = act = swiglu_fwd(gu, name="swiglu_fwd")
        h = matmul(act, full["ffn_w_out"][i], "nn", add=h, name="mm_ffn_out")
        s["h2"] = h
        s["hn3"] = hn3 = rms_fwd(h, ln_ple[i:i + 1], name="rms_fwd")
        s["gt"] = gt = matmul(hn3, full["ple_w_gate"][i], "nn", name="mm_sq")
        s["pp"] = pp = matmul(p[i, 0], full["ple_w_proj"][i], "nn", name="mm_ple_proj")
        h = ple_fwd(h, pp, gt, name="ple_fwd")
        saved.append(s)
        if i == n_a - 1:
            mid = dict(h=h)
            mid["hk"] = hk = rms_fwd(h, kv_norm[None, :], name="rms_fwd")
            mid["kv"] = kv = matmul(hk, full["w_kv"], "nn", name="mm_kv")
            k_sh = headnorm_fwd(kv, 0, k_norm[None, :], 1.0, name="headnorm_k")
            v_sh = kv[:, WIDTH:].astype(BF16)

    dh, sq = loss_head(h, tgt, name="loss_head")
    loss = lax.psum(0.5 * jnp.sum(sq) / h.shape[1], ("x", "y", "c"))

    gw = {n: [None] * w[n].shape[0] for n in WEIGHTS if w[n].ndim >= 2 and n not in ("w_kv",)}
    dks, dvs = [], []
    for i in reversed(range(depth)):
        s = saved[i]
        if i == n_a - 1:
            dkraw, gw["k_norm"] = headnorm_bwd(mid["kv"], 0, k_norm[None, :], 1.0, tuple(dks), name="headnorm_k_bwd")
            dkv = jnp.concatenate([dkraw, sum_cast(dvs, BF16, name="sum_dv")], axis=1)
            dhk = matmul(dkv, full["w_kv"], "nt", name="mm_kv_dx")
            gw["w_kv"] = matmul(mid["hk"], dkv, "tn", name="mm_kv_dw")
            dh, gw["kv_norm"] = rms_bwd(mid["h"], kv_norm[None, :], dhk, dh, name="rms_bwd")
        dpp, dgt = ple_bwd(dh, s["pp"], s["gt"], name="ple_bwd")
        gw["ple_w_proj"][i] = matmul(p[i, 0], dpp, "tn", name="mm_ple_proj_dw")
        gw["ple_w_gate"][i] = matmul(s["hn3"], dgt, "tn", name="mm_sq_dw")
        dhn3 = matmul(dgt, full["ple_w_gate"][i], "nt", name="mm_sq_dx")
        dh, gw["ln_ple"][i] = rms_bwd(s["h2"], ln_ple[i:i + 1], dhn3, dh, name="rms_bwd")
        dact = matmul(dh, full["ffn_w_out"][i], "nt", out_dtype=BF16, name="mm_ffn_out_dx")
        gw["ffn_w_out"][i] = matmul(s["act"], dh, "tn", name="mm_ffn_out_dw")
        dgu = swiglu_bwd(s["gu"], dact, name="swiglu_bwd")
        dhn2 = matmul(dgu, ffn_in_all, "nt", b_chips=i, name="mm_ffn_in_dx")
        gw["ffn_w_in"][i] = matmul(s["hn2"], dgu, "tn", out_chips=True, name="mm_ffn_in_dw")
        dh, gw["ln_ffn"][i] = rms_bwd(s["h1"], ln_ffn[i:i + 1], dhn2, dh, name="rms_bwd")
        if i < n_a:
            dy = matmul(dh, full["gdn_w_out"][i], "nt", name="mm_sq_dx")
            gw["gdn_w_out"][i] = matmul(s["y"], dh, "tn", name="mm_sq_dw")
            do, dproj, gw["gdn_norm"][i] = gatenorm_bwd(s["o"], s["proj"], gdn_norm[i:i + 1], dy, name="gatenorm_bwd")
            dqkv, dg4, db4 = gdn_bwd(s["qkv"], s["g4"], s["b4"], s["s0"], s["tinv"], do, name="gdn_bwd")
            dab, dal, ddt = gates_bwd(s["ab"], s["a_log"], s["dt"], _from4(dg4), _from4(db4), name="gates_bwd")
            gw["gdn_a_log"][i], gw["gdn_dt_bias"][i] = dal[:, 0], ddt[:, 0]
            dproj, gw["gdn_conv"][i] = conv_bwd(s["proj"], conv_full[i], dqkv, dproj, name="conv_bwd")
            dhn = matmul(dproj, s["w_m"], "nt", name="mm_gdn_in_dx")
            dhn = matmul(dab, s["w_abt"], "tn", add=dhn, name="mm_gdn_ab_dx")
            dwm = matmul(s["hn"], dproj, "tn", name="mm_gdn_in_dw")
            dwab = matmul(dab, s["hn"], "nn", name="mm_gdn_ab_dw")
            gw["gdn_w_in"][i] = jnp.concatenate([dwm, dwab.T], axis=1)
        else:
            j = i - n_a
            do = matmul(dh, full["sb_w_out"][j], "nt", out_dtype=BF16, name="mm_sb_out_dx")
            gw["sb_w_out"][j] = matmul(s["o"], dh, "tn", name="mm_sq_dw")
            dq, dk, dv = sb_bwd(s["q"], k_sh, v_sh, do, name="sb_bwd")
            dks.append(dk)
            dvs.append(dv)
            dqraw, gw["sb_q_norm"][j] = headnorm_bwd(s["qraw"], 0, sb_q_norm[j:j + 1], HEAD_DIM ** -0.5, (dq,),
                                                    name="headnorm_q_bwd")
            dhn = matmul(dqraw, full["sb_w_q"][j], "nt", name="mm_sq_dx")
            gw["sb_w_q"][j] = matmul(s["hn"], dqraw, "tn", name="mm_sq_dw")
        dh, gw["ln_mix"][i] = rms_bwd(s["h0"], ln_mix[i:i + 1], dhn, dh, name="rms_bwd")
    grad_x = dh[None]

    def stacked(n):
        g = gw[n]
        if isinstance(g, list):
            g = jnp.stack([a.reshape(w[n].shape[1:]) if n in SMALL else a for a in g])
        return g

    small_buf = jnp.concatenate([_row(stacked(n)) for n in SMALL] + [_row(stacked("gdn_conv"))], axis=0)
    small_buf = jnp.pad(small_buf, ((0, buf_rows - small_buf.shape[0]), (0, 0)))
    small_sum = all_reduce_small(small_buf, name="all_reduce_small")
    grads = {}
    r0 = 0
    for n in SMALL:
        rows = _rows_of(w[n].shape)
        grads[n] = small_sum[r0:r0 + rows].reshape(-1)[:math.prod(w[n].shape)].reshape(w[n].shape)
        r0 += rows
    conv_g = small_sum[r0:r0 + N_CHIPS * conv_rows].reshape(-1)[:N_CHIPS * math.prod(gdn_conv.shape)]
    conv_g = conv_g.reshape((gdn_conv.shape[0], CONV_WIDTH, N_CHIPS, gdn_conv.shape[2]))
    grads["gdn_conv"] = lax.dynamic_index_in_dim(conv_g, chip, axis=2, keepdims=False)

    g_packed = _pack_rows([_shard(stacked(n), ax) for n, ax in BIG], (N_CHIPS,))
    g_ffn_in = jnp.concatenate(gw["ffn_w_in"], axis=1)
    g5s = [g.reshape(N_CHIPS, 2, g.shape[1] // 2, g.shape[2]) for g in (g_packed, g_ffn_in)]
    from_sibling = sibling_swap(g5s, name="sibling_swap")
    s1s = [add_selected(sel_c, g5, fs, name="add_selected") for g5, fs in zip(g5s, from_sibling)]
    from_chips = chip_exchange([s1b for _, s1b in s1s], name="chip_exchange")
    s2s = [add_chip_sums(sel_chip, s1, fc, name="add_chip_sums") for (s1, _), fc in zip(s1s, from_chips)]
    reduced, reduced_ffn_in = sibling_merge(s2s, name="sibling_merge")
    for (n, _), g in zip(BIG, _unpack_rows(reduced.reshape(-1, PACK_COLS), shard_shapes, ())):
        grads[n] = g
    grads["ffn_w_in"] = reduced_ffn_in.reshape(ffn_w_in.shape)

    delta, new_m, new_v = {}, {}, {}
    for n in WEIGHTS:
        delta[n], new_m[n], new_v[n] = adamw(w[n], grads[n], mom1[n], mom2[n], name="adamw")
    return (loss, grad_x, *[grads[n] for n in WEIGHTS], *[delta[n] for n in WEIGHTS],
            *[new_m[n] for n in WEIGHTS], *[new_v[n] for n in WEIGHTS])
```

```python
import functools
import math

import jax
import jax.numpy as jnp
from jax import lax
from jax.experimental import pallas as pl
from jax.experimental.pallas import tpu as pltpu

F32 = jnp.float32
BF16 = jnp.bfloat16
EPS = 1e-6
HEADS = 8
HEAD_DIM = 128
WIDTH = HEADS * HEAD_DIM
CHUNK = 64
CONV_WIDTH = 4
N_CHIPS = 4
ADAM_LR, ADAM_B1, ADAM_B2, ADAM_EPS, ADAM_WD, ADAM_STEP = 0.001, 0.9, 0.999, 1e-08, 0.01, 10
V7X_VMEM_BYTES = 64 * 1024 * 1024
VMEM_LIMIT = V7X_VMEM_BYTES - 8 * 1024 * 1024
HIGHEST = lax.Precision.HIGHEST
MESH = pl.DeviceIdType.MESH


def _params(sem=None):
    return pltpu.CompilerParams(dimension_semantics=sem, vmem_limit_bytes=VMEM_LIMIT)


def _pick(n, prefs):
    for t in prefs:
        if t <= n and n % t == 0:
            return t
    return n


def _bdot(a, b, dims):
    return lax.dot_general(a.astype(BF16), b.astype(BF16), (((dims[0],), (dims[1],)), ((), ())),
                           preferred_element_type=F32)


NN, NT, TN = (1, 0), (1, 1), (0, 0)


MM_TILES = (1024, 1408, 512, 256, 128)


def matmul(a, b, form, *, out_dtype=F32, add=None, name, b_chips=None, out_chips=False):
    ns = None
    if b_chips is not None:
        ns = b.shape[3]
        b_shape = (b.shape[2], N_CHIPS * ns)
    else:
        b_shape = b.shape
    if form == "nn":
        (m, k), (k2, n) = a.shape, b_shape
    elif form == "nt":
        (m, k), (n, k2) = a.shape, b_shape
    else:
        (k, m), (k2, n) = a.shape, b_shape
    assert k == k2, (a.shape, b.shape, form)
    if out_chips:
        ns = n // N_CHIPS
    tm = _pick(m, MM_TILES)
    tn = _pick(n, MM_TILES)
    tk = k if k <= 1024 else _pick(k, MM_TILES)
    if ns is not None and (form == "nn" or out_chips):
        tn = ns
    if ns is not None and form == "nt":
        tk = ns
    nk = k // tk
    if form == "tn":
        a_spec = pl.BlockSpec((tk, tm), lambda i, j, kk: (kk, i))
    else:
        a_spec = pl.BlockSpec((tm, tk), lambda i, j, kk: (i, kk))
    if b_chips is not None and form == "nn":
        b_spec = pl.BlockSpec((None, None, tk, ns), lambda i, j, kk: (j, b_chips, kk, 0))
    elif b_chips is not None:
        b_spec = pl.BlockSpec((None, None, tn, ns), lambda i, j, kk: (kk, b_chips, j, 0))
    elif form == "nt":
        b_spec = pl.BlockSpec((tn, tk), lambda i, j, kk: (j, kk))
    else:
        b_spec = pl.BlockSpec((tk, tn), lambda i, j, kk: (kk, j))
    if out_chips:
        o_spec = pl.BlockSpec((None, tm, ns), lambda i, j, kk: (j, i, 0))
    else:
        o_spec = pl.BlockSpec((tm, tn), lambda i, j, kk: (i, j))
    dims = {"nn": NN, "nt": NT, "tn": TN}[form]
    has_add = add is not None

    def body(*refs):
        a_ref, b_ref = refs[:2]
        add_ref = refs[2] if has_add else None
        o_ref = refs[2 + has_add]

        def finish(r):
            if has_add:
                r = r + add_ref[...].astype(F32)
            o_ref[...] = r.astype(out_dtype)

        part = _bdot(a_ref[...], b_ref[...], dims)
        if nk == 1:
            finish(part)
            return
        acc_ref = refs[3 + has_add]
        kk = pl.program_id(2)

        @pl.when(kk == 0)
        def _():
            acc_ref[...] = part

        @pl.when(kk > 0)
        def _():
            acc_ref[...] += part

        @pl.when(kk == nk - 1)
        def _():
            finish(acc_ref[...])

    in_specs = [a_spec, b_spec] + ([o_spec] if has_add else [])
    args = (a, b) + ((add,) if has_add else ())
    return pl.pallas_call(
        body, name=name, grid=(m // tm, n // tn, nk), in_specs=in_specs, out_specs=o_spec,
        out_shape=jax.ShapeDtypeStruct((N_CHIPS, m, ns) if out_chips else (m, n), out_dtype),
        scratch_shapes=[pltpu.VMEM((tm, tn), F32)] if nk > 1 else [],
        compiler_params=_params(("parallel", "parallel", "arbitrary")),
    )(*args)


def _const(c):
    return lambda j: c


def rowwise(fn, rows, params, outs, accs=(), *, name, tm, ncol=1):
    t = rows[0][0].shape[0]
    tm = min(tm, t)
    assert t % tm == 0
    n_rows, n_par, n_out, n_acc = len(rows), len(params), len(outs), len(accs)

    def body(*refs):
        j, i = pl.program_id(0), pl.program_id(1)
        ins = [r[...] for r in refs[:n_rows + n_par]]
        o_refs = refs[n_rows + n_par:n_rows + n_par + n_out]
        a_refs = refs[n_rows + n_par + n_out:]
        row_outs, acc_outs = fn(*ins)
        for r, val in zip(o_refs, row_outs):
            r[...] = val.astype(r.dtype)
        for r, val, spec in zip(a_refs, acc_outs, accs):
            first = (i == 0) & (j == 0) if spec[4] else (i == 0)

            @pl.when(first)
            def _(r=r, val=val):
                r[...] = val.astype(F32)

            @pl.when(jnp.logical_not(first))
            def _(r=r, val=val):
                r[...] += val.astype(F32)

    in_specs = [pl.BlockSpec((tm, w), lambda j, i, cf=cf: (i, cf(j))) for _, w, cf in rows]
    in_specs += [pl.BlockSpec((p.shape[0], w), lambda j, i, cf=cf: (0, cf(j))) for p, w, cf in params]
    out_specs = [pl.BlockSpec((tm, w), lambda j, i, cf=cf: (i, cf(j))) for _, _, w, cf in outs]
    out_specs += [pl.BlockSpec((r, w), lambda j, i, cf=cf: (0, cf(j))) for r, _, w, cf, _ in accs]
    out_shape = [jax.ShapeDtypeStruct((t, tw), dt) for tw, dt, _, _ in outs]
    out_shape += [jax.ShapeDtypeStruct((r, tw), F32) for r, tw, _, _, _ in accs]
    res = pl.pallas_call(
        body, name=name, grid=(ncol, t // tm), in_specs=in_specs, out_specs=out_specs, out_shape=out_shape,
        compiler_params=_params(("arbitrary", "arbitrary")),
    )(*[r[0] for r in rows], *[p[0] for p in params])
    return res[:n_out], res[n_out:]


def _full(arr):
    return (arr, arr.shape[1], _const(0))


def _rms(x, g):
    x = x.astype(F32)
    return x * lax.rsqrt(jnp.mean(x * x, axis=-1, keepdims=True) + EPS) * g.astype(F32)


def _sigmoid(x):
    return 1.0 / (1.0 + jnp.exp(-x))


def _silu(x):
    return x * _sigmoid(x)


def _softplus(x):
    return jnp.maximum(x, 0.0) + jnp.log(1.0 + jnp.exp(-jnp.abs(x)))


def rms_fwd(h, g, *, name):
    d = h.shape[1]
    (hn,), _ = rowwise(lambda x, gg: ((_rms(x, gg),), ()), [_full(h)], [_full(g)],
                       [(d, BF16, d, _const(0))], name=name, tm=512)
    return hn


def rms_bwd(h, g, dhn, dh_res, *, name):
    d = h.shape[1]

    def fn(x, ct, res, gg):
        _, vjp = jax.vjp(_rms, x.astype(F32), gg.astype(F32))
        dx, dg = vjp(ct.astype(F32))
        return (res.astype(F32) + dx,), (dg,)

    (dh,), (dg,) = rowwise(fn, [_full(h), _full(dhn), _full(dh_res)], [_full(g)],
                           [(d, F32, d, _const(0))], [(1, d, d, _const(0), True)], name=name, tm=256)
    return dh, dg


def _head_rms(x, g, scale):
    x = x.astype(F32)
    return x * lax.rsqrt(jnp.mean(x * x, axis=-1, keepdims=True) + EPS) * (g.astype(F32) * scale)


def headnorm_fwd(x, col0, g, scale, *, name):
    (y,), _ = rowwise(lambda a, gg: ((_head_rms(a, gg, scale),), ()),
                      [(x, HEAD_DIM, lambda j: col0 + j)], [_full(g)],
                      [(WIDTH, BF16, HEAD_DIM, lambda j: j)], name=name, tm=1024, ncol=HEADS)
    return y


def headnorm_bwd(x, col0, g, scale, dys, *, name, out_dtype=BF16):
    def fn(a, *rest):
        cts, gg = rest[:-1], rest[-1]
        ct = sum(c.astype(F32) for c in cts)
        _, vjp = jax.vjp(lambda a_, g_: _head_rms(a_, g_, scale), a.astype(F32), gg.astype(F32))
        dx, dg = vjp(ct)
        return (dx,), (dg,)

    (dx,), (dg,) = rowwise(fn, [(x, HEAD_DIM, lambda j: col0 + j)] + [(dy, HEAD_DIM, lambda j: j) for dy in dys], [_full(g)],
                           [(WIDTH, out_dtype, HEAD_DIM, lambda j: j)],
                           [(1, HEAD_DIM, HEAD_DIM, _const(0), True)], name=name, tm=1024, ncol=HEADS)
    return dx, dg


def sum_cast(parts, dtype, *, name):
    wd = parts[0].shape[1]
    (out,), _ = rowwise(lambda *a: ((sum(b.astype(F32) for b in a),), ()), [_full(a) for a in parts], [],
                        [(wd, dtype, wd, _const(0))], name=name, tm=512)
    return out


def _gatenorm(o, gate, g):
    return _head_rms(o, g, 1.0) * _silu(gate.astype(F32))


def gatenorm_fwd(o, proj, g, *, name):
    (y,), _ = rowwise(lambda a, gt, gg: ((_gatenorm(a, gt, gg),), ()),
                      [(o, HEAD_DIM, lambda j: j), (proj, HEAD_DIM, lambda j: 3 * HEADS + j)], [_full(g)],
                      [(WIDTH, BF16, HEAD_DIM, lambda j: j)], name=name, tm=1024, ncol=HEADS)
    return y


def gatenorm_bwd(o, proj, g, dy, *, name):
    def fn(a, gt, ct, gg):
        _, vjp = jax.vjp(_gatenorm, a.astype(F32), gt.astype(F32), gg.astype(F32))
        da, dgt, dg = vjp(ct.astype(F32))
        return (da, dgt), (dg,)

    (do, dproj), (dg,) = rowwise(
        fn, [(o, HEAD_DIM, lambda j: j), (proj, HEAD_DIM, lambda j: 3 * HEADS + j), (dy, HEAD_DIM, lambda j: j)],
        [_full(g)],
        [(WIDTH, F32, HEAD_DIM, lambda j: j), (4 * WIDTH, BF16, HEAD_DIM, lambda j: 3 * HEADS + j)],
        [(1, HEAD_DIM, HEAD_DIM, _const(0), True)], name=name, tm=1024, ncol=HEADS)
    return do, dproj, dg


def _swiglu(g, u):
    return _silu(g.astype(F32)) * u.astype(F32)


def swiglu_fwd(gu, *, name):
    f = gu.shape[1] // 2
    (act,), _ = rowwise(lambda g, u: ((_swiglu(g, u),), ()), [(gu, f, _const(0)), (gu, f, _const(1))], [],
                        [(f, BF16, f, _const(0))], name=name, tm=256)
    return act


def swiglu_bwd(gu, dact, *, name):
    f = gu.shape[1] // 2

    def fn(g, u, ct):
        _, vjp = jax.vjp(_swiglu, g.astype(F32), u.astype(F32))
        dg, du = vjp(ct.astype(F32))
        return (jnp.concatenate([dg.astype(BF16), du.astype(BF16)], axis=1),), ()

    (dgu,), _ = rowwise(fn, [(gu, f, _const(0)), (gu, f, _const(1)), _full(dact)], [],
                        [(2 * f, BF16, 2 * f, _const(0))], name=name, tm=256)
    return dgu


def ple_fwd(h, pp, gt, *, name):
    d = h.shape[1]
    (out,), _ = rowwise(lambda a, b, c: ((a + b * _sigmoid(c),), ()), [_full(h), _full(pp), _full(gt)], [],
                        [(d, F32, d, _const(0))], name=name, tm=512)
    return out


def ple_bwd(dh, pp, gt, *, name):
    d = dh.shape[1]

    def fn(ct, b, c):
        s = _sigmoid(c)
        return (ct * s, ct * b * s * (1.0 - s)), ()

    (dpp, dgt), _ = rowwise(fn, [_full(dh), _full(pp), _full(gt)], [],
                            [(d, BF16, d, _const(0)), (d, BF16, d, _const(0))], name=name, tm=512)
    return dpp, dgt


def loss_head(y, tgt, *, name):
    d = y.shape[1]

    def fn(a, b):
        e = a - b
        return (e * (1.0 / d),), (jnp.sum(e * e, axis=0, keepdims=True),)

    (dy,), (sq,) = rowwise(fn, [_full(y), _full(tgt)], [], [(d, F32, d, _const(0))],
                           [(1, d, d, _const(0), True)], name=name, tm=512)
    return dy, sq


def adamw(w, g, m, v, *, name):
    shape = w.shape
    cols = shape[-1]
    flat = lambda a: a.reshape(-1, cols)
    bc1 = 1.0 - ADAM_B1 ** ADAM_STEP
    bc2 = 1.0 - ADAM_B2 ** ADAM_STEP

    def fn(w_, g_, m_, v_):
        m_ = ADAM_B1 * m_ + (1.0 - ADAM_B1) * g_
        v_ = ADAM_B2 * v_ + (1.0 - ADAM_B2) * (g_ * g_)
        delta = -ADAM_LR * ((m_ / bc1) / (jnp.sqrt(v_ / bc2) + ADAM_EPS) + ADAM_WD * w_)
        return (delta, m_, v_), ()

    o = (cols, F32, cols, _const(0))
    (d_, m_, v_), _ = rowwise(fn, [_full(flat(w)), _full(flat(g)), _full(flat(m)), _full(flat(v))], [],
                              [o, o, o], name=name, tm=256)
    return d_.reshape(shape), m_.reshape(shape), v_.reshape(shape)


CONV_STRIP = 256


def _shift_down(x, d):
    if d == 0:
        return x
    rows = lax.broadcasted_iota(jnp.int32, x.shape, 0)
    return jnp.where(rows >= d, pltpu.roll(x, d, 0), 0.0)


def _shift_up(x, d):
    if d == 0:
        return x
    t = x.shape[0]
    rows = lax.broadcasted_iota(jnp.int32, x.shape, 0)
    return jnp.where(rows < t - d, pltpu.roll(x, t - d, 0), 0.0)


def _conv(x, w):
    acc = None
    for j in range(CONV_WIDTH):
        term = _shift_down(x, CONV_WIDTH - 1 - j) * w[j:j + 1, :]
        acc = term if acc is None else acc + term
    return acc


def conv_fwd(proj, w, *, name):
    t = proj.shape[0]
    per = WIDTH // CONV_STRIP

    def body(x_ref, w_ref, o_ref):
        o_ref[0] = _silu(_conv(x_ref[...].astype(F32), w_ref[...]))

    return pl.pallas_call(
        body, name=name, grid=(3 * per,),
        in_specs=[pl.BlockSpec((t, CONV_STRIP), lambda j: (0, j)), pl.BlockSpec((CONV_WIDTH, CONV_STRIP), lambda j: (0, j))],
        out_specs=pl.BlockSpec((1, t, CONV_STRIP), lambda j: (j // per, 0, j % per)),
        out_shape=jax.ShapeDtypeStruct((3, t, WIDTH), F32),
        compiler_params=_params(("parallel",)),
    )(proj, w)


def conv_bwd(proj, w, dqkv, dproj, *, name):
    t = proj.shape[0]
    per = WIDTH // CONV_STRIP

    def body(x_ref, w_ref, d_ref, _, dx_ref, dw_ref):
        x, w_ = x_ref[...].astype(F32), w_ref[...]
        c = _conv(x, w_)
        s = _sigmoid(c)
        dc = d_ref[0] * (s + c * s * (1.0 - s))
        dx = None
        for j in range(CONV_WIDTH):
            d = CONV_WIDTH - 1 - j
            term = _shift_up(dc, d) * w_[j:j + 1, :]
            dx = term if dx is None else dx + term
            dw_ref[j:j + 1, :] = jnp.sum(dc * _shift_down(x, d), axis=0, keepdims=True)
        dx_ref[...] = dx.astype(dx_ref.dtype)

    return pl.pallas_call(
        body, name=name, grid=(3 * per,),
        in_specs=[pl.BlockSpec((t, CONV_STRIP), lambda j: (0, j)), pl.BlockSpec((CONV_WIDTH, CONV_STRIP), lambda j: (0, j)),
                  pl.BlockSpec((1, t, CONV_STRIP), lambda j: (j // per, 0, j % per)), pl.BlockSpec(memory_space=pl.ANY)],
        out_specs=[pl.BlockSpec((t, CONV_STRIP), lambda j: (0, j)), pl.BlockSpec((CONV_WIDTH, CONV_STRIP), lambda j: (0, j))],
        out_shape=[jax.ShapeDtypeStruct(dproj.shape, dproj.dtype), jax.ShapeDtypeStruct((CONV_WIDTH, 3 * WIDTH), F32)],
        input_output_aliases={3: 0},
        compiler_params=_params(("parallel",)),
    )(proj, w, dqkv, dproj)


def _gdn_gates(ab, a_log, dt_bias):
    a_in, b_in = ab[:HEADS], ab[HEADS:]
    g = -jnp.exp(a_log) * _softplus(a_in + dt_bias)
    return g, _sigmoid(b_in)


def gates_fwd(ab, a_log, dt_bias, *, name):
    t = ab.shape[1]

    def body(ab_ref, al_ref, dt_ref, g_ref, b_ref):
        g_ref[...], b_ref[...] = _gdn_gates(ab_ref[...], al_ref[...], dt_ref[...])

    s = jax.ShapeDtypeStruct((HEADS, t), F32)
    return pl.pallas_call(body, name=name, out_shape=[s, s], compiler_params=_params())(ab, a_log, dt_bias)


def gates_bwd(ab, a_log, dt_bias, dg, dbeta, *, name):
    t = ab.shape[1]

    def body(ab_ref, al_ref, dt_ref, dg_ref, db_ref, dab_ref, dal_ref, ddt_ref):
        _, vjp = jax.vjp(_gdn_gates, ab_ref[...], al_ref[...], dt_ref[...])
        dab_ref[...], dal_ref[...], ddt_ref[...] = vjp((dg_ref[...], db_ref[...]))

    c = jax.ShapeDtypeStruct((HEADS, 1), F32)
    return pl.pallas_call(body, name=name, out_shape=[jax.ShapeDtypeStruct((2 * HEADS, t), F32), c, c],
                          compiler_params=_params())(ab, a_log, dt_bias, dg, dbeta)


def _split3(x):
    hi = x.astype(BF16)
    r1 = x - hi.astype(F32)
    mid = r1.astype(BF16)
    lo = (r1 - mid.astype(F32)).astype(BF16)
    return hi, mid, lo


def _dot01(x, m01):
    hi, mid, lo = _split3(x)
    m01 = m01.astype(BF16)
    return _bdot(hi, m01, NN) + _bdot(mid, m01, NN) + _bdot(lo, m01, NN)


def _hdot(a, b, dims=NN):
    a_hi, b_hi = a.astype(BF16), b.astype(BF16)
    a_lo, b_lo = (a - a_hi.astype(F32)).astype(BF16), (b - b_hi.astype(F32)).astype(BF16)
    return _bdot(a_hi, b_hi, dims) + (_bdot(a_hi, b_lo, dims) + _bdot(a_lo, b_hi, dims))


def _rowsum(x):
    return jnp.sum(x, axis=1, keepdims=True)


def _colsum(x):
    return jnp.sum(x, axis=0, keepdims=True)


class _Heads:
    def __init__(self, vals):
        self.v = list(vals)

    def _bin(self, other, f):
        if isinstance(other, _Heads):
            return _Heads(f(a, b) for a, b in zip(self.v, other.v))
        return _Heads(f(a, other) for a in self.v)

    def __add__(self, o):
        return self._bin(o, lambda a, b: a + b)

    __radd__ = __add__

    def __sub__(self, o):
        return self._bin(o, lambda a, b: a - b)

    def __rsub__(self, o):
        return self._bin(o, lambda a, b: b - a)

    def __mul__(self, o):
        return self._bin(o, lambda a, b: a * b)

    __rmul__ = __mul__

    def __neg__(self):
        return _Heads(-a for a in self.v)


def _hmap(f, *args):
    n = next(len(a.v) for a in args if isinstance(a, _Heads))
    return _Heads(f(*[a.v[h] if isinstance(a, _Heads) else a for a in args]) for h in range(n))


def _inv_unit_lower(a, eye):
    p = jnp.where(eye, 1.0, 0.0) - a
    ak = a
    for _ in range(int(math.log2(CHUNK)) - 1):
        ak = _hmap(_hdot, ak, ak)
        p = p + _hmap(_hdot, p, ak)
    return p


def _gdn_chunk(qr, kr, v, grow, brow, tinv=None):
    c = CHUNK
    ri = lax.broadcasted_iota(jnp.int32, (c, c), 0)
    ci = lax.broadcasted_iota(jnp.int32, (c, c), 1)
    eye, lower, strict = ri == ci, ri >= ci, ri > ci
    where = lambda m: (lambda a: jnp.where(m, a, 0.0))
    to_col = lambda row: _hmap(lambda r: _rowsum(jnp.where(eye, jnp.broadcast_to(r, (c, c)), 0.0)), row)
    cum_row = _hmap(lambda g: _dot01(jnp.broadcast_to(g, (8, c)), ri <= ci)[0:1], grow)
    gcol, bcol = to_col(cum_row), to_col(brow)
    glast = _hmap(lambda g: _colsum(jnp.where(ri[:, 0:1] == c - 1, g, 0.0)), gcol)
    rq = _hmap(lambda a: lax.rsqrt(_rowsum(a * a) + EPS), qr)
    rk = _hmap(lambda a: lax.rsqrt(_rowsum(a * a) + EPS), kr)
    scale = HEAD_DIM ** -0.5
    qn, kn = qr * (rq * scale), kr * rk
    dec = _hmap(lambda gc, gr: jnp.where(lower, jnp.exp(jnp.minimum(gc - gr, 0.0)), 0.0), gcol, cum_row)
    kk = _hmap(lambda a: _bdot(a, a, NT), kn)
    qk = _hmap(lambda a, b: _bdot(a, b, NT), qn, kn)
    gam_col, e_col, gam_last = _hmap(jnp.exp, gcol), _hmap(jnp.exp, glast - gcol), _hmap(jnp.exp, glast)
    if tinv is None:
        tinv = _inv_unit_lower(_hmap(where(strict), bcol * kk * dec), eye)
    u = _hmap(_hdot, tinv, v * bcol)
    w = _hmap(_hdot, tinv, kn * (bcol * gam_col))
    return dict(eye=eye, lower=lower, strict=strict, gcol=gcol, bcol=bcol, rq=rq, rk=rk, qn=qn, kn=kn,
                dec=dec, kk=kk, qk=qk, gam_col=gam_col, e_col=e_col, gam_last=gam_last, tinv=tinv, u=u, w=w,
                aqk=qk * dec, qt=qn * gam_col, kt=kn * e_col, scale=scale)


def _head_cols(h):
    return slice(h * HEAD_DIM, (h + 1) * HEAD_DIM)


def _bd(dims):
    return lambda a, b: _bdot(a, b, dims)


def gdn_fwd(qkv, g4, b4, *, name, gather=()):
    t = qkv.shape[1]
    n = t // CHUNK
    d = HEAD_DIM
    heads = range(HEADS)
    ng = len(gather)

    def body(*refs):
        qkv_ref, g_ref, b_ref = refs[:3]
        o_ref, s0_ref, t_ref = refs[3 + ng:6 + ng]
        s_ref = refs[6 + 2 * ng]
        comm = _Gather(refs[6 + ng:6 + 2 * ng], *refs[7 + 2 * ng:]) if ng else None

        @pl.when(pl.program_id(0) == 0)
        def _():
            s_ref[...] = jnp.zeros_like(s_ref)
            if ng:
                comm.start()

        qr, kr, v = (_Heads(qkv_ref[j, :, _head_cols(h)] for h in heads) for j in range(3))
        z = _gdn_chunk(qr, kr, v, _Heads(g_ref[0, h] for h in heads), _Heads(b_ref[0, h] for h in heads))
        s0 = _Heads(s_ref[h] for h in heads)
        v_new = z["u"] - _hmap(_bd(NN), z["w"], s0)
        o = _hmap(_bd(NN), z["qt"], s0) + _hmap(_bd(NN), z["aqk"], v_new)
        s_new = s0 * z["gam_last"] + _hmap(_bd(TN), z["kt"], v_new)
        for h in heads:
            s0_ref[0, h] = s0.v[h]
            t_ref[0, h] = z["tinv"].v[h]
            o_ref[:, _head_cols(h)] = o.v[h]
            s_ref[h] = s_new.v[h]

        if ng:
            @pl.when(pl.program_id(0) == n - 1)
            def _():
                comm.finish()

    gspec = pl.BlockSpec((1, HEADS, 1, CHUNK), lambda i: (i, 0, 0, 0))
    return pl.pallas_call(
        body, name=name, grid=(n,),
        in_specs=[pl.BlockSpec((3, CHUNK, WIDTH), lambda i: (0, i, 0)), gspec, gspec] + [ANY] * ng,
        out_specs=[pl.BlockSpec((CHUNK, WIDTH), lambda i: (i, 0)),
                   pl.BlockSpec((1, HEADS, d, d), lambda i: (i, 0, 0, 0)),
                   pl.BlockSpec((1, HEADS, CHUNK, CHUNK), lambda i: (i, 0, 0, 0))] + [ANY] * ng,
        out_shape=[jax.ShapeDtypeStruct((t, WIDTH), F32), jax.ShapeDtypeStruct((n, HEADS, d, d), F32),
                   jax.ShapeDtypeStruct((n, HEADS, CHUNK, CHUNK), F32)]
        + [jax.ShapeDtypeStruct(s.shape, s.dtype) for s in gather],
        input_output_aliases={3 + b: 3 + b for b in range(ng)},
        scratch_shapes=[pltpu.VMEM((HEADS, d, d), F32)] + (_dma_sems(6 * ng) if ng else []),
        compiler_params=_params(("arbitrary",)),
    )(qkv, g4, b4, *gather)


def gdn_bwd(qkv, g4, b4, s0_all, tinv_all, do, *, name):
    t = qkv.shape[1]
    n = t // CHUNK
    d = HEAD_DIM
    c = CHUNK
    heads = range(HEADS)

    def body(qkv_ref, g_ref, b_ref, s0_ref, t_ref, do_ref, dqkv_ref, dg_ref, db_ref, ds_ref):
        @pl.when(pl.program_id(0) == 0)
        def _():
            ds_ref[...] = jnp.zeros_like(ds_ref)

        qr, kr, v = (_Heads(qkv_ref[j, :, _head_cols(h)] for h in heads) for j in range(3))
        z = _gdn_chunk(qr, kr, v, _Heads(g_ref[0, h] for h in heads), _Heads(b_ref[0, h] for h in heads),
                       tinv=_Heads(t_ref[0, h] for h in heads))
        s0 = _Heads(s0_ref[0, h] for h in heads)
        ds = _Heads(ds_ref[h] for h in heads)
        dout = _Heads(do_ref[:, _head_cols(h)] for h in heads)
        qn, kn, u, w, dec, kk, qk = z["qn"], z["kn"], z["u"], z["w"], z["dec"], z["kk"], z["qk"]
        bcol, gam_col, e_col, gam_last = z["bcol"], z["gam_col"], z["e_col"], z["gam_last"]
        low = lambda a: jnp.where(z["lower"], a, 0.0)
        strict = lambda a: jnp.where(z["strict"], a, 0.0)
        rowsum = lambda a: _hmap(_rowsum, a)
        colsum = lambda a: _hmap(_colsum, a)
        v_new = u - _hmap(_bd(NN), w, s0)
        dv_new = _hmap(_bd(TN), z["aqk"], dout) + _hmap(_bd(NN), z["kt"], ds)
        daqk = _hmap(low, _hmap(_bd(NT), dout, v_new))
        dqt = _hmap(_bd(NT), dout, s0)
        dkt = _hmap(_bd(NT), v_new, ds)
        dgam_last = _hmap(lambda a, b: jnp.sum(a * b, keepdims=True), ds, s0)
        ds_new = _hmap(_bd(TN), z["qt"], dout) + ds * gam_last - _hmap(_bd(TN), w, dv_new)
        dw = -_hmap(_bd(NT), dv_new, s0)
        hd_t = lambda a, b: _hdot(a, b, TN)
        dru = _hmap(hd_t, z["tinv"], dv_new)
        drw = _hmap(hd_t, z["tinv"], dw)
        dal = -_hmap(strict, _hmap(_bd(NT), dru, u) + _hmap(_bd(NT), drw, w))
        t1 = dal * kk * dec
        dkk = dal * bcol * dec
        ddec = dal * bcol * kk + daqk * qk
        dqk = daqk * dec
        s_w = rowsum(drw * kn)
        dbeta_col = rowsum(t1) + rowsum(dru * v) + gam_col * s_w
        dkn = (drw * (bcol * gam_col) + _hmap(_bd(NN), dkk, kn) + _hmap(_bd(TN), dkk, kn) + _hmap(_bd(TN), dqk, qn)
               + dkt * e_col)
        dqn = _hmap(_bd(NN), dqk, kn) + dqt * gam_col
        e_mat = ddec * dec
        de_col = rowsum(dkt * kn)
        diag_of_colsum = rowsum(_hmap(lambda a: jnp.where(z["eye"], jnp.broadcast_to(_colsum(a), (c, c)), 0.0), e_mat))
        dg_cum = rowsum(e_mat) + (bcol * s_w + rowsum(dqt * qn)) * gam_col - de_col * e_col - diag_of_colsum
        dg_last = colsum(de_col * e_col) + dgam_last * gam_last
        dg = colsum(_hmap(lambda a: jnp.where(z["lower"], a, 0.0), dg_cum)) + dg_last
        dbeta = colsum(_hmap(lambda a: jnp.where(z["eye"], a, 0.0), dbeta_col))
        rq, rk = z["rq"], z["rk"]
        dqr = z["scale"] * (rq * dqn - qr * (rq * rq * rq) * rowsum(qr * dqn))
        dkr = rk * dkn - kr * (rk * rk * rk) * rowsum(kr * dkn)
        dv = dru * bcol
        for h in heads:
            ds_ref[h] = ds_new.v[h]
            dg_ref[0, h] = dg.v[h]
            db_ref[0, h] = dbeta.v[h]
            dqkv_ref[0, :, _head_cols(h)] = dqr.v[h]
            dqkv_ref[1, :, _head_cols(h)] = dkr.v[h]
            dqkv_ref[2, :, _head_cols(h)] = dv.v[h]

    rev = lambda i: n - 1 - i
    gspec = pl.BlockSpec((1, HEADS, 1, CHUNK), lambda i: (rev(i), 0, 0, 0))
    return pl.pallas_call(
        body, name=name, grid=(n,),
        in_specs=[pl.BlockSpec((3, CHUNK, WIDTH), lambda i: (0, rev(i), 0)), gspec, gspec,
                  pl.BlockSpec((1, HEADS, d, d), lambda i: (rev(i), 0, 0, 0)),
                  pl.BlockSpec((1, HEADS, CHUNK, CHUNK), lambda i: (rev(i), 0, 0, 0)),
                  pl.BlockSpec((CHUNK, WIDTH), lambda i: (rev(i), 0))],
        out_specs=[pl.BlockSpec((3, CHUNK, WIDTH), lambda i: (0, rev(i), 0)), gspec, gspec],
        out_shape=[jax.ShapeDtypeStruct((3, t, WIDTH), F32), jax.ShapeDtypeStruct((n, HEADS, 1, CHUNK), F32),
                   jax.ShapeDtypeStruct((n, HEADS, 1, CHUNK), F32)],
        scratch_shapes=[pltpu.VMEM((HEADS, d, d), F32)],
        compiler_params=_params(("arbitrary",)),
    )(qkv, g4, b4, s0_all, tinv_all, do)


SB_BLOCK = 256


def _dot01_2(x, m01):
    hi = x.astype(BF16)
    lo = (x - hi.astype(F32)).astype(BF16)
    return _bdot(hi, m01, NN) + _bdot(lo, m01, NN)


SB_HEADS = 2


def _sb_weights(q, kb, carry, mask, upper):
    z = _hmap(_bd(NT), q, kb)
    ls = _hmap(lambda z_: jnp.minimum(z_, 0.0) - jnp.log(1.0 + jnp.exp(-jnp.abs(z_))), z)
    ln = _hmap(lambda l_, z_: jnp.where(mask, l_ - z_, 0.0), ls, z)
    suffix = _hmap(lambda l_: _dot01_2(l_, upper), ln)
    a = _hmap(lambda l_, s_, c_: jnp.where(mask, jnp.exp(l_ + s_ + c_), 0.0), ls, suffix, carry)
    return z, ln, a


SB_DEAD = -105.0


def _sb_alive(s, i, carries):
    top = jnp.max(carries[0])
    for c in carries[1:]:
        top = jnp.maximum(top, jnp.max(c))
    return (s <= i) & (top > SB_DEAD)


def _sb_masks(i, jb, blk):
    ri = lax.broadcasted_iota(jnp.int32, (blk, blk), 0)
    ci = lax.broadcasted_iota(jnp.int32, (blk, blk), 1)
    return (jb * blk + ci) < (i * blk + ri)


def sb_fwd(q, k, v, *, name, gather=()):
    t = q.shape[0]
    blk = min(SB_BLOCK, t)
    d = HEAD_DIM
    hs = range(SB_HEADS)
    ng = len(gather)
    groups, nb = HEADS // SB_HEADS, t // blk

    def body(*refs):
        q_ref, k_ref, v_ref = refs[:3]
        o_ref = refs[3 + ng]
        comm = _Gather(refs[4 + ng:4 + 2 * ng], *refs[4 + 2 * ng:]) if ng else None
        i = pl.program_id(1)
        if ng:
            @pl.when((pl.program_id(0) == 0) & (i == 0))
            def _():
                comm.start()

        qb = _Heads(q_ref[:, _head_cols(h)] for h in hs)
        ri = lax.broadcasted_iota(jnp.int32, (blk, blk), 0)
        ci = lax.broadcasted_iota(jnp.int32, (blk, blk), 1)
        upper = (ri > ci).astype(BF16)

        def step(state):
            s, cs, accs = state
            jb = i - s
            rows = pl.ds(pl.multiple_of(jb * blk, blk), blk)
            kb = _Heads(k_ref[rows, _head_cols(h)] for h in hs)
            vb = _Heads(v_ref[rows, _head_cols(h)] for h in hs)
            _, ln, a = _sb_weights(qb, kb, _Heads(cs), _sb_masks(i, jb, blk), upper)
            cs = _Heads(cs) + _hmap(_rowsum, ln)
            accs = _Heads(accs) + _hmap(_bd(NN), a, vb)
            return s + 1, tuple(cs.v), tuple(accs.v)

        init = (jnp.int32(0), tuple(jnp.zeros((blk, 1), F32) for _ in hs), tuple(jnp.zeros((blk, d), F32) for _ in hs))
        _, _, accs = lax.while_loop(lambda st: _sb_alive(st[0], i, st[1]), step, init)
        for h in hs:
            o_ref[:, _head_cols(h)] = accs[h].astype(o_ref.dtype)

        if ng:
            @pl.when((pl.program_id(0) == groups - 1) & (i == nb - 1))
            def _():
                comm.finish()

    qspec = pl.BlockSpec((blk, SB_HEADS * d), lambda g, i: (i, g))
    kspec = pl.BlockSpec((t, SB_HEADS * d), lambda g, i: (0, g))
    return pl.pallas_call(
        body, name=name, grid=(groups, nb), in_specs=[qspec, kspec, kspec] + [ANY] * ng,
        out_specs=[qspec] + [ANY] * ng,
        out_shape=[jax.ShapeDtypeStruct((t, WIDTH), BF16)] + [jax.ShapeDtypeStruct(s.shape, s.dtype) for s in gather],
        input_output_aliases={3 + b: 1 + b for b in range(ng)},
        scratch_shapes=_dma_sems(6 * ng) if ng else [],
        compiler_params=_params(("arbitrary", "arbitrary") if ng else ("parallel", "arbitrary")),
    )(q, k, v, *gather)


def sb_bwd(q, k, v, do, *, name):
    t = q.shape[0]
    blk = min(SB_BLOCK, t)
    d = HEAD_DIM
    nb = t // blk
    hs = range(SB_HEADS)

    def body(q_ref, k_ref, v_ref, do_ref, dq_ref, dk_ref, dv_ref, p_buf, z_buf):
        i = pl.program_id(1)

        @pl.when(i == 0)
        def _():
            dk_ref[...] = jnp.zeros_like(dk_ref)
            dv_ref[...] = jnp.zeros_like(dv_ref)

        qb = _Heads(q_ref[:, _head_cols(h)] for h in hs)
        dob = _Heads(do_ref[:, _head_cols(h)] for h in hs)
        ri = lax.broadcasted_iota(jnp.int32, (blk, blk), 0)
        ci = lax.broadcasted_iota(jnp.int32, (blk, blk), 1)
        upper = (ri > ci).astype(BF16)
        lower = (ri < ci).astype(BF16)

        def right_to_left(state):
            s, cs = state
            jb = i - s
            rows = pl.ds(pl.multiple_of(jb * blk, blk), blk)
            kb = _Heads(k_ref[rows, _head_cols(h)] for h in hs)
            vb = _Heads(v_ref[rows, _head_cols(h)] for h in hs)
            z, ln, a = _sb_weights(qb, kb, _Heads(cs), _sb_masks(i, jb, blk), upper)
            p = a * _hmap(_bd(NT), dob, vb)
            dv = _hmap(_bd(TN), a, dob)
            for h in hs:
                p_buf[h, jb] = p.v[h]
                z_buf[h, jb] = z.v[h]
                dv_ref[rows, _head_cols(h)] += dv.v[h]
            return s + 1, tuple((_Heads(cs) + _hmap(_rowsum, ln)).v)

        n_done, _ = lax.while_loop(lambda st: _sb_alive(st[0], i, st[1]), right_to_left,
                                   (jnp.int32(0), tuple(jnp.zeros((blk, 1), F32) for _ in hs)))

        def left_to_right(jb, carry):
            cps, dqs = carry
            rows = pl.ds(pl.multiple_of(jb * blk, blk), blk)
            mask = _sb_masks(i, jb, blk)
            kb = _Heads(k_ref[rows, _head_cols(h)] for h in hs)
            p = _Heads(p_buf[h, jb] for h in hs)
            sg = _hmap(_sigmoid, _Heads(z_buf[h, jb] for h in hs))
            prefix = _hmap(lambda a: _dot01_2(a, lower), p) + _Heads(cps)
            dz = _hmap(lambda a: jnp.where(mask, a, 0.0), p * (1.0 - sg) - sg * prefix)
            dk = _hmap(_bd(TN), dz, qb)
            for h in hs:
                dk_ref[rows, _head_cols(h)] += dk.v[h]
            return tuple((_Heads(cps) + _hmap(_rowsum, p)).v), tuple((_Heads(dqs) + _hmap(_bd(NN), dz, kb)).v)

        _, dqs = lax.fori_loop(i + 1 - n_done, i + 1, left_to_right,
                               (tuple(jnp.zeros((blk, 1), F32) for _ in hs), tuple(jnp.zeros((blk, d), F32) for _ in hs)))
        for h in hs:
            dq_ref[:, _head_cols(h)] = dqs[h]

    qspec = pl.BlockSpec((blk, SB_HEADS * d), lambda g, i: (i, g))
    kspec = pl.BlockSpec((t, SB_HEADS * d), lambda g, i: (0, g))
    s = jax.ShapeDtypeStruct((t, WIDTH), F32)
    buf = pltpu.VMEM((SB_HEADS, nb, blk, blk), F32)
    return pl.pallas_call(
        body, name=name, grid=(HEADS // SB_HEADS, nb), in_specs=[qspec, kspec, kspec, qspec],
        out_specs=[qspec, kspec, kspec], out_shape=[s, s, s], scratch_shapes=[buf, buf],
        compiler_params=_params(("parallel", "arbitrary")),
    )(q, k, v, do)


PACK_COLS = 1024
ANY = pl.BlockSpec(memory_space=pl.ANY)


def _mesh_pos():
    return lax.axis_index("x"), lax.axis_index("y"), lax.axis_index("c")


def _other_chips(x, y):
    return [(1 - x, y), (x, 1 - y), (1 - x, 1 - y)]


def _dma_sems(n):
    return [pltpu.SemaphoreType.DMA((n,)), pltpu.SemaphoreType.DMA((n,))]


class _Gather:
    def __init__(self, o_refs, send_sems, recv_sems):
        self.o_refs, self.send_sems, self.recv_sems = o_refs, send_sems, recv_sems

    def _copy(self, b, k, chip, hf, to):
        rows = self.o_refs[b].at[chip, hf]
        return pltpu.make_async_remote_copy(src_ref=rows, dst_ref=rows, send_sem=self.send_sems.at[6 * b + k],
                                            recv_sem=self.recv_sems.at[6 * b + k], device_id=to, device_id_type=MESH)

    def start(self):
        x, y, c = _mesh_pos()
        for b in range(len(self.o_refs)):
            for k, (cx, cy) in enumerate(_other_chips(x, y)):
                self._copy(b, k, 2 * x + y, c, (cx, cy, c)).start()

    def finish(self):
        x, y, c = _mesh_pos()
        chips = _other_chips(x, y)
        for b in range(len(self.o_refs)):
            for k, (cx, cy) in enumerate(chips):
                self._copy(b, k, 2 * cx + cy, c, (x, y, c)).wait_recv()
                self._copy(b, 3 + k, 2 * cx + cy, c, (x, y, 1 - c)).start()
        for b in range(len(self.o_refs)):
            for k, (cx, cy) in enumerate(chips):
                self._copy(b, 3 + k, 2 * cx + cy, 1 - c, (x, y, c)).wait_recv()
                self._copy(b, k, 2 * x + y, c, (cx, cy, c)).wait_send()
                self._copy(b, 3 + k, 2 * cx + cy, c, (x, y, 1 - c)).wait_send()


def all_gather_chips(slots, *, name):
    nb = len(slots)

    def body(*refs):
        g = _Gather(refs[nb:2 * nb], *refs[2 * nb:])
        g.start()
        g.finish()

    return pl.pallas_call(
        body, name=name, in_specs=[ANY] * nb, out_specs=[ANY] * nb, input_output_aliases={b: b for b in range(nb)},
        out_shape=[jax.ShapeDtypeStruct(s.shape, s.dtype) for s in slots], scratch_shapes=_dma_sems(6 * nb),
    )(*slots)


def sibling_swap(gs, *, name):
    nb = len(gs)

    def body(*refs):
        g_refs, o_refs, (send_sems, recv_sems) = refs[:nb], refs[nb:2 * nb], refs[2 * nb:]
        x, y, c = _mesh_pos()
        cps = [pltpu.make_async_remote_copy(src_ref=g_refs[b].at[j, 1 - c], dst_ref=o_refs[b].at[j],
                                            send_sem=send_sems.at[N_CHIPS * b + j], recv_sem=recv_sems.at[N_CHIPS * b + j],
                                            device_id=(x, y, 1 - c), device_id_type=MESH)
               for b in range(nb) for j in range(N_CHIPS)]
        for cp in cps:
            cp.start()
        for cp in cps:
            cp.wait()

    return pl.pallas_call(
        body, name=name, in_specs=[ANY] * nb, out_specs=[ANY] * nb,
        out_shape=[jax.ShapeDtypeStruct((g.shape[0],) + g.shape[2:], g.dtype) for g in gs],
        scratch_shapes=_dma_sems(N_CHIPS * nb),
    )(*gs)


def chip_exchange(s1s, *, name):
    nb = len(s1s)

    def body(*refs):
        s_refs, o_refs, (send_sems, recv_sems) = refs[:nb], refs[nb:2 * nb], refs[2 * nb:]
        x, y, c = _mesh_pos()
        cps = [pltpu.make_async_remote_copy(src_ref=s_refs[b].at[2 * cx + cy], dst_ref=o_refs[b].at[k],
                                            send_sem=send_sems.at[3 * b + k], recv_sem=recv_sems.at[3 * b + k],
                                            device_id=(cx, cy, c), device_id_type=MESH)
               for b in range(nb) for k, (cx, cy) in enumerate(_other_chips(x, y))]
        for cp in cps:
            cp.start()
        for cp in cps:
            cp.wait()

    return pl.pallas_call(
        body, name=name, in_specs=[ANY] * nb, out_specs=[ANY] * nb,
        out_shape=[jax.ShapeDtypeStruct((3,) + s.shape[1:], s.dtype) for s in s1s], scratch_shapes=_dma_sems(3 * nb),
    )(*s1s)


def sibling_merge(halves, *, name):
    nb = len(halves)

    def body(*refs):
        o_refs, (send_sems, recv_sems) = refs[nb:2 * nb], refs[2 * nb:]
        x, y, c = _mesh_pos()
        cps = [pltpu.make_async_remote_copy(src_ref=o_refs[b].at[c], dst_ref=o_refs[b].at[c], send_sem=send_sems.at[b],
                                            recv_sem=recv_sems.at[b], device_id=(x, y, 1 - c), device_id_type=MESH)
               for b in range(nb)]
        for cp in cps:
            cp.start()
        for cp in cps:
            cp.wait()

    return pl.pallas_call(
        body, name=name, in_specs=[ANY] * nb, out_specs=[ANY] * nb, input_output_aliases={b: b for b in range(nb)},
        out_shape=[jax.ShapeDtypeStruct(h.shape, h.dtype) for h in halves], scratch_shapes=_dma_sems(nb),
    )(*halves)


def all_reduce_small(buf, *, name):
    n_dev = 8

    def body(b_ref, o_ref, recv_buf, send_sems, recv_sems):
        x, y, c = _mesh_pos()
        me = 4 * x + 2 * y + c
        pos = lambda t: (t // 4, (t // 2) % 2, t % 2)

        def copy(t, slot):
            return pltpu.make_async_remote_copy(src_ref=b_ref, dst_ref=recv_buf.at[slot], send_sem=send_sems.at[t],
                                                recv_sem=recv_sems.at[slot], device_id=pos(t), device_id_type=MESH)

        for t in range(n_dev):
            @pl.when(t != me)
            def _(t=t):
                copy(t, me).start()

        recv_buf[me] = b_ref[...]
        for t in range(n_dev):
            @pl.when(t != me)
            def _(t=t):
                copy(t, t).wait_recv()
                copy(t, me).wait_send()

        acc = recv_buf[0]
        for t in range(1, n_dev):
            acc = acc + recv_buf[t]
        o_ref[...] = acc

    return pl.pallas_call(
        body, name=name, out_shape=jax.ShapeDtypeStruct(buf.shape, F32),
        in_specs=[pl.BlockSpec(memory_space=pltpu.VMEM)], out_specs=pl.BlockSpec(memory_space=pltpu.VMEM),
        scratch_shapes=[pltpu.VMEM((n_dev,) + buf.shape, F32), pltpu.SemaphoreType.DMA((n_dev,)),
                        pltpu.SemaphoreType.DMA((n_dev,))],
    )(buf)


REDUCE_ROWS = (512, 384, 256, 128)


def add_selected(sel, a5, b, *, name):
    n, _, rh, w = a5.shape
    tr = _pick(rh, REDUCE_ROWS)

    def body(sel_ref, a_ref, b_ref, o_ref, ob_ref):
        s = a_ref[...] + b_ref[...]
        o_ref[...] = s
        ob_ref[...] = s.astype(BF16)

    blk = pl.BlockSpec((None, tr, w), lambda j, i, s: (j, i, 0))
    return pl.pallas_call(
        body, name=name,
        grid_spec=pltpu.PrefetchScalarGridSpec(
            num_scalar_prefetch=1, grid=(n, rh // tr),
            in_specs=[pl.BlockSpec((None, None, tr, w), lambda j, i, s: (j, s[0], i, 0)), blk], out_specs=[blk, blk]),
        out_shape=[jax.ShapeDtypeStruct((n, rh, w), F32), jax.ShapeDtypeStruct((n, rh, w), BF16)],
        compiler_params=_params(("arbitrary", "arbitrary")),
    )(sel, a5, b)


def add_chip_sums(sel, s1, b2, *, name):
    _, rh, w = s1.shape
    tr = _pick(rh, REDUCE_ROWS)

    def body(sel_ref, s_ref, b_ref, o_ref):
        o_ref[...] = ((s_ref[...] + b_ref[0].astype(F32)) + b_ref[1].astype(F32)) + b_ref[2].astype(F32)

    return pl.pallas_call(
        body, name=name,
        grid_spec=pltpu.PrefetchScalarGridSpec(
            num_scalar_prefetch=1, grid=(rh // tr,),
            in_specs=[pl.BlockSpec((None, tr, w), lambda i, s: (s[0], i, 0)), pl.BlockSpec((3, tr, w), lambda i, s: (0, i, 0))],
            out_specs=pl.BlockSpec((None, tr, w), lambda i, s: (s[1], i, 0))),
        out_shape=jax.ShapeDtypeStruct((2, rh, w), F32),
        compiler_params=_params(("arbitrary",)),
    )(sel, s1, b2)


BIG = (("gdn_w_out", 1), ("sb_w_q", 1), ("sb_w_out", 1), ("ffn_w_out", 1), ("ple_w_gate", 1), ("gdn_w_in", 2),
       ("w_kv", 1), ("ple_w_proj", 2))
SMALL = ("ln_mix", "ln_ffn", "ln_ple", "gdn_a_log", "gdn_dt_bias", "gdn_norm", "kv_norm", "k_norm", "sb_q_norm")
WEIGHTS = ("ln_mix", "ln_ffn", "ln_ple", "gdn_w_in", "gdn_conv", "gdn_a_log", "gdn_dt_bias", "gdn_norm", "gdn_w_out",
           "kv_norm", "w_kv", "k_norm", "sb_w_q", "sb_q_norm", "sb_w_out", "ffn_w_in", "ffn_w_out", "ple_w_proj",
           "ple_w_gate")
PACK_ALIGN = 256


ROW_TILE = 16


def _rows_of(shape, tile=ROW_TILE):
    return -(-math.prod(shape) // (PACK_COLS * tile)) * tile


WEIGHT_ALIGN = 32


def _pack_rows(arrs, lead, align=PACK_ALIGN):
    parts = []
    for a in arrs:
        if a.shape[-1] == PACK_COLS:
            parts.append(a.reshape(lead + (-1, PACK_COLS)))
            continue
        flat = a.reshape(lead + (-1,))
        pad = _rows_of(a.shape[len(lead):]) * PACK_COLS - flat.shape[-1]
        if pad:
            flat = jnp.pad(flat, [(0, 0)] * len(lead) + [(0, pad)])
        parts.append(flat.reshape(lead + (-1, PACK_COLS)))
    rows = sum(q.shape[len(lead)] for q in parts)
    filler = -rows % align
    if filler:
        parts.append(jnp.zeros(lead + (filler, PACK_COLS), parts[0].dtype))
    return jnp.concatenate(parts, axis=len(lead))


def _own_slot(buf, chip):
    slots = lax.dynamic_update_slice(lax.empty((N_CHIPS,) + buf.shape, buf.dtype), buf[None], (chip, 0, 0))
    return slots.reshape(N_CHIPS, 2, buf.shape[0] // 2, buf.shape[1])


def _unpack_rows(buf, shapes, lead):
    out, r0 = [], 0
    for s in shapes:
        rows = _rows_of(s)
        flat = buf[(slice(None),) * len(lead) + (slice(r0, r0 + rows),)].reshape(lead + (-1,))
        out.append(flat[..., :math.prod(s)].reshape(lead + tuple(s)))
        r0 += rows
    return out


def _unshard(g, axis):
    g = jnp.moveaxis(g, 0, axis)
    s = g.shape
    return g.reshape(s[:axis] + (s[axis] * s[axis + 1],) + s[axis + 2:])


def _shard(full, axis):
    s = full.shape
    return jnp.moveaxis(full.reshape(s[:axis] + (N_CHIPS, s[axis] // N_CHIPS) + s[axis + 1:]), axis, 0)


def _to4(a):
    return a.reshape(HEADS, -1, 1, CHUNK).transpose(1, 0, 2, 3)


def _from4(a):
    return a.transpose(1, 0, 2, 3).reshape(HEADS, -1)


def _row(vec):
    flat = vec.reshape(-1)
    rows = _rows_of(flat.shape, 1)
    return jnp.pad(flat, (0, rows * PACK_COLS - flat.shape[0])).reshape(rows, PACK_COLS)


def kernel(x, p, ln_mix, ln_ffn, ln_ple, gdn_w_in, gdn_conv, gdn_a_log, gdn_dt_bias, gdn_norm, gdn_w_out, kv_norm, w_kv, k_norm, sb_w_q, sb_q_norm, sb_w_out, ffn_w_in, ffn_w_out, ple_w_proj, ple_w_gate, loss_target, m_ln_mix, m_ln_ffn, m_ln_ple, m_gdn_w_in, m_gdn_conv, m_gdn_a_log, m_gdn_dt_bias, m_gdn_norm, m_gdn_w_out, m_kv_norm, m_w_kv, m_k_norm, m_sb_w_q, m_sb_q_norm, m_sb_w_out, m_ffn_w_in, m_ffn_w_out, m_ple_w_proj, m_ple_w_gate, v_ln_mix, v_ln_ffn, v_ln_ple, v_gdn_w_in, v_gdn_conv, v_gdn_a_log, v_gdn_dt_bias, v_gdn_norm, v_gdn_w_out, v_kv_norm, v_w_kv, v_k_norm, v_sb_w_q, v_sb_q_norm, v_sb_w_out, v_ffn_w_in, v_ffn_w_out, v_ple_w_proj, v_ple_w_gate):
    w = dict(ln_mix=ln_mix, ln_ffn=ln_ffn, ln_ple=ln_ple, gdn_w_in=gdn_w_in, gdn_conv=gdn_conv, gdn_a_log=gdn_a_log,
             gdn_dt_bias=gdn_dt_bias, gdn_norm=gdn_norm, gdn_w_out=gdn_w_out, kv_norm=kv_norm, w_kv=w_kv, k_norm=k_norm,
             sb_w_q=sb_w_q, sb_q_norm=sb_q_norm, sb_w_out=sb_w_out, ffn_w_in=ffn_w_in, ffn_w_out=ffn_w_out,
             ple_w_proj=ple_w_proj, ple_w_gate=ple_w_gate)
    mom1 = dict(ln_mix=m_ln_mix, ln_ffn=m_ln_ffn, ln_ple=m_ln_ple, gdn_w_in=m_gdn_w_in, gdn_conv=m_gdn_conv,
                gdn_a_log=m_gdn_a_log, gdn_dt_bias=m_gdn_dt_bias, gdn_norm=m_gdn_norm, gdn_w_out=m_gdn_w_out,
                kv_norm=m_kv_norm, w_kv=m_w_kv, k_norm=m_k_norm, sb_w_q=m_sb_w_q, sb_q_norm=m_sb_q_norm,
                sb_w_out=m_sb_w_out, ffn_w_in=m_ffn_w_in, ffn_w_out=m_ffn_w_out, ple_w_proj=m_ple_w_proj,
                ple_w_gate=m_ple_w_gate)
    mom2 = dict(ln_mix=v_ln_mix, ln_ffn=v_ln_ffn, ln_ple=v_ln_ple, gdn_w_in=v_gdn_w_in, gdn_conv=v_gdn_conv,
                gdn_a_log=v_gdn_a_log, gdn_dt_bias=v_gdn_dt_bias, gdn_norm=v_gdn_norm, gdn_w_out=v_gdn_w_out,
                kv_norm=v_kv_norm, w_kv=v_w_kv, k_norm=v_k_norm, sb_w_q=v_sb_w_q, sb_q_norm=v_sb_q_norm,
                sb_w_out=v_sb_w_out, ffn_w_in=v_ffn_w_in, ffn_w_out=v_ffn_w_out, ple_w_proj=v_ple_w_proj,
                ple_w_gate=v_ple_w_gate)
    depth = ln_mix.shape[0]
    n_a = gdn_w_in.shape[0]
    xi, yi, ci = _mesh_pos()
    chip = 2 * xi + yi
    sel_c = jnp.reshape(ci, (1,)).astype(jnp.int32)
    sel_chip = jnp.stack([chip, ci]).astype(jnp.int32)
    h = x[0]
    tgt = loss_target[0]
    t = h.shape[0]

    shard_shapes = [w[n].shape for n, _ in BIG]

    def layer_items(i):
        if i < n_a:
            items = [("gdn_w_out", i, 0), ("ffn_w_out", i, 0), ("ple_w_gate", i, 0), ("gdn_w_in", i, 1),
                     ("ple_w_proj", i, 1)]
            return items + ([("w_kv", None, 1)] if i == n_a - 1 else [])
        j = i - n_a
        return [("sb_w_q", j, 0), ("sb_w_out", j, 0), ("ffn_w_out", i, 0), ("ple_w_gate", i, 0), ("ple_w_proj", i, 1)]

    def layer_shards(i):
        return [w[n] if idx is None else w[n][idx] for n, idx, _ in layer_items(i)]

    def layer_slots(i):
        packed = _pack_rows([q.astype(BF16) for q in layer_shards(i)], (), align=WEIGHT_ALIGN)
        return [_own_slot(packed, chip), _own_slot(ffn_w_in[i].astype(BF16), chip)]

    def layer_weights(i, got):
        parts = _unpack_rows(got[0].reshape(N_CHIPS, -1, PACK_COLS), [q.shape for q in layer_shards(i)], (N_CHIPS,))
        out = {n: _unshard(g, ax) for (n, _, ax), g in zip(layer_items(i), parts)}
        out["ffn_w_in"] = got[1].reshape((N_CHIPS, 1) + ffn_w_in.shape[1:])
        return out

    slots = [layer_slots(i) for i in range(depth)]
    wl = [layer_weights(0, all_gather_chips(slots[0], name="all_gather_weights"))]
    conv_rows = _rows_of(gdn_conv.shape, 1)
    small_rows = sum(_rows_of(w[n].shape, 1) for n in SMALL)
    buf_rows = -(-(small_rows + N_CHIPS * conv_rows) // 8) * 8
    conv_buf = jnp.zeros((buf_rows, PACK_COLS), F32)
    conv_buf = lax.dynamic_update_slice(conv_buf, _row(gdn_conv) * (ci == 0).astype(F32), (chip * conv_rows, 0))
    conv_all = all_reduce_small(conv_buf, name="all_reduce_small")[:N_CHIPS * conv_rows]
    conv_full = _unshard(conv_all.reshape(N_CHIPS, -1)[:, :math.prod(gdn_conv.shape)].reshape((N_CHIPS,) + gdn_conv.shape), 2)

    saved = []
    k_sh = v_sh = None
    mid = None
    for i in range(depth):
        s = dict(h0=h)
        wi = wl[i]
        nxt = slots[i + 1] if i + 1 < depth else ()
        s["hn"] = hn = rms_fwd(h, ln_mix[i:i + 1], name="rms_fwd")
        if i < n_a:
            w_in = wi["gdn_w_in"]
            s["w_m"], s["w_abt"] = w_in[:, :4 * WIDTH], w_in[:, 4 * WIDTH:].T
            s["proj"] = proj = matmul(hn, s["w_m"], "nn", out_dtype=BF16, name="mm_gdn_in")
            s["ab"] = ab = matmul(s["w_abt"], hn, "nt", name="mm_gdn_ab")
            s["a_log"], s["dt"] = gdn_a_log[i][:, None], gdn_dt_bias[i][:, None]
            g8, b8 = gates_fwd(ab, s["a_log"], s["dt"], name="gates_fwd")
            s["g4"], s["b4"] = _to4(g8), _to4(b8)
            s["qkv"] = qkv = conv_fwd(proj, conv_full[i], name="conv_fwd")
            s["o"], s["s0"], s["tinv"], *got = gdn_fwd(qkv, s["g4"], s["b4"], gather=nxt, name="gdn_fwd")
            s["y"] = y = gatenorm_fwd(s["o"], proj, gdn_norm[i:i + 1], name="gatenorm_fwd")
            h = matmul(y, wi["gdn_w_out"], "nn", add=h, name="mm_out")
        else:
            j = i - n_a
            s["qraw"] = qraw = matmul(hn, wi["sb_w_q"], "nn", name="mm_sq")
            s["q"] = q = headnorm_fwd(qraw, 0, sb_q_norm[j:j + 1], HEAD_DIM ** -0.5, name="headnorm_q")
            s["o"], *got = sb_fwd(q, k_sh, v_sh, gather=nxt, name="sb_fwd")
            h = matmul(s["o"], wi["sb_w_out"], "nn", add=h, name="mm_out")
        if nxt:
            wl.append(layer_weights(i + 1, got))
        s["h1"] = h
        s["hn2"] = hn2 = rms_fwd(h, ln_ffn[i:i + 1], name="rms_fwd")
        s["gu"] = gu = matmul(hn2, wi["ffn_w_in"], "nn", b_chips=0, out_dtype=BF16, name="mm_ffn_in")
        s["act"] = act = swiglu_fwd(gu, name="swiglu_fwd")
        h = matmul(act, wi["ffn_w_out"], "nn", add=h, name="mm_ffn_out")
        s["h2"] = h
        s["hn3"] = hn3 = rms_fwd(h, ln_ple[i:i + 1], name="rms_fwd")
        s["gt"] = gt = matmul(hn3, wi["ple_w_gate"], "nn", name="mm_sq")
        s["pp"] = pp = matmul(p[i, 0], wi["ple_w_proj"], "nn", name="mm_ple_proj")
        h = ple_fwd(h, pp, gt, name="ple_fwd")
        saved.append(s)
        if i == n_a - 1:
            mid = dict(h=h)
            mid["hk"] = hk = rms_fwd(h, kv_norm[None, :], name="rms_fwd")
            mid["kv"] = kv = matmul(hk, wi["w_kv"], "nn", name="mm_kv")
            k_sh = headnorm_fwd(kv, 0, k_norm[None, :], 1.0, name="headnorm_k")
            v_sh = kv[:, WIDTH:].astype(BF16)

    dh, sq = loss_head(h, tgt, name="loss_head")
    loss = lax.psum(0.5 * jnp.sum(sq) / h.shape[1], ("x", "y", "c"))

    gw = {n: [None] * w[n].shape[0] for n in WEIGHTS if w[n].ndim >= 2 and n not in ("w_kv",)}
    dks, dvs = [], []
    for i in reversed(range(depth)):
        s = saved[i]
        if i == n_a - 1:
            dkraw, gw["k_norm"] = headnorm_bwd(mid["kv"], 0, k_norm[None, :], 1.0, tuple(dks), name="headnorm_k_bwd")
            dkv = jnp.concatenate([dkraw, sum_cast(dvs, BF16, name="sum_dv")], axis=1)
            dhk = matmul(dkv, wl[i]["w_kv"], "nt", name="mm_kv_dx")
            gw["w_kv"] = matmul(mid["hk"], dkv, "tn", name="mm_kv_dw")
            dh, gw["kv_norm"] = rms_bwd(mid["h"], kv_norm[None, :], dhk, dh, name="rms_bwd")
        dpp, dgt = ple_bwd(dh, s["pp"], s["gt"], name="ple_bwd")
        gw["ple_w_proj"][i] = matmul(p[i, 0], dpp, "tn", name="mm_ple_proj_dw")
        gw["ple_w_gate"][i] = matmul(s["hn3"], dgt, "tn", name="mm_sq_dw")
        dhn3 = matmul(dgt, wl[i]["ple_w_gate"], "nt", name="mm_sq_dx")
        dh, gw["ln_ple"][i] = rms_bwd(s["h2"], ln_ple[i:i + 1], dhn3, dh, name="rms_bwd")
        dact = matmul(dh, wl[i]["ffn_w_out"], "nt", out_dtype=BF16, name="mm_ffn_out_dx")
        gw["ffn_w_out"][i] = matmul(s["act"], dh, "tn", name="mm_ffn_out_dw")
        dgu = swiglu_bwd(s["gu"], dact, name="swiglu_bwd")
        dhn2 = matmul(dgu, wl[i]["ffn_w_in"], "nt", b_chips=0, name="mm_ffn_in_dx")
        gw["ffn_w_in"][i] = matmul(s["hn2"], dgu, "tn", out_chips=True, name="mm_ffn_in_dw")
        dh, gw["ln_ffn"][i] = rms_bwd(s["h1"], ln_ffn[i:i + 1], dhn2, dh, name="rms_bwd")
        if i < n_a:
            dy = matmul(dh, wl[i]["gdn_w_out"], "nt", name="mm_sq_dx")
            gw["gdn_w_out"][i] = matmul(s["y"], dh, "tn", name="mm_sq_dw")
            do, dproj, gw["gdn_norm"][i] = gatenorm_bwd(s["o"], s["proj"], gdn_norm[i:i + 1], dy, name="gatenorm_bwd")
            dqkv, dg4, db4 = gdn_bwd(s["qkv"], s["g4"], s["b4"], s["s0"], s["tinv"], do, name="gdn_bwd")
            dab, dal, ddt = gates_bwd(s["ab"], s["a_log"], s["dt"], _from4(dg4), _from4(db4), name="gates_bwd")
            gw["gdn_a_log"][i], gw["gdn_dt_bias"][i] = dal[:, 0], ddt[:, 0]
            dproj, gw["gdn_conv"][i] = conv_bwd(s["proj"], conv_full[i], dqkv, dproj, name="conv_bwd")
            dhn = matmul(dproj, s["w_m"], "nt", name="mm_gdn_in_dx")
            dhn = matmul(dab, s["w_abt"], "tn", add=dhn, name="mm_gdn_ab_dx")
            dwm = matmul(s["hn"], dproj, "tn", name="mm_gdn_in_dw")
            dwab = matmul(dab, s["hn"], "nn", name="mm_gdn_ab_dw")
            gw["gdn_w_in"][i] = jnp.concatenate([dwm, dwab.T], axis=1)
        else:
            j = i - n_a
            do = matmul(dh, wl[i]["sb_w_out"], "nt", out_dtype=BF16, name="mm_sb_out_dx")
            gw["sb_w_out"][j] = matmul(s["o"], dh, "tn", name="mm_sq_dw")
            dq, dk, dv = sb_bwd(s["q"], k_sh, v_sh, do, name="sb_bwd")
            dks.append(dk)
            dvs.append(dv)
            dqraw, gw["sb_q_norm"][j] = headnorm_bwd(s["qraw"], 0, sb_q_norm[j:j + 1], HEAD_DIM ** -0.5, (dq,),
                                                    name="headnorm_q_bwd")
            dhn = matmul(dqraw, wl[i]["sb_w_q"], "nt", name="mm_sq_dx")
            gw["sb_w_q"][j] = matmul(s["hn"], dqraw, "tn", name="mm_sq_dw")
        dh, gw["ln_mix"][i] = rms_bwd(s["h0"], ln_mix[i:i + 1], dhn, dh, name="rms_bwd")
    grad_x = dh[None]

    def stacked(n):
        g = gw[n]
        if isinstance(g, list):
            g = jnp.stack([a.reshape(w[n].shape[1:]) if n in SMALL else a for a in g])
        return g

    small_buf = jnp.concatenate([_row(stacked(n)) for n in SMALL] + [_row(stacked("gdn_conv"))], axis=0)
    small_buf = jnp.pad(small_buf, ((0, buf_rows - small_buf.shape[0]), (0, 0)))
    small_sum = all_reduce_small(small_buf, name="all_reduce_small")
    grads = {}
    r0 = 0
    for n in SMALL:
        rows = _rows_of(w[n].shape, 1)
        grads[n] = small_sum[r0:r0 + rows].reshape(-1)[:math.prod(w[n].shape)].reshape(w[n].shape)
        r0 += rows
    conv_g = small_sum[r0:r0 + N_CHIPS * conv_rows].reshape(-1)[:N_CHIPS * math.prod(gdn_conv.shape)]
    conv_g = conv_g.reshape((gdn_conv.shape[0], CONV_WIDTH, N_CHIPS, gdn_conv.shape[2]))
    grads["gdn_conv"] = lax.dynamic_index_in_dim(conv_g, chip, axis=2, keepdims=False)

    g_packed = _pack_rows([_shard(stacked(n), ax) for n, ax in BIG], (N_CHIPS,))
    g_ffn_in = jnp.concatenate(gw["ffn_w_in"], axis=1)
    g5s = [g.reshape(N_CHIPS, 2, g.shape[1] // 2, g.shape[2]) for g in (g_packed, g_ffn_in)]
    from_sibling = sibling_swap(g5s, name="sibling_swap")
    s1s = [add_selected(sel_c, g5, fs, name="add_selected") for g5, fs in zip(g5s, from_sibling)]
    from_chips = chip_exchange([s1b for _, s1b in s1s], name="chip_exchange")
    s2s = [add_chip_sums(sel_chip, s1, fc, name="add_chip_sums") for (s1, _), fc in zip(s1s, from_chips)]
    reduced, reduced_ffn_in = sibling_merge(s2s, name="sibling_merge")
    for (n, _), g in zip(BIG, _unpack_rows(reduced.reshape(-1, PACK_COLS), shard_shapes, ())):
        grads[n] = g
    grads["ffn_w_in"] = reduced_ffn_in.reshape(ffn_w_in.shape)

    delta, new_m, new_v = {}, {}, {}
    for n in WEIGHTS:
        delta[n], new_m[n], new_v[n] = adamw(w[n], grads[n], mom1[n], mom2[n], name="adamw")
    return (loss, grad_x, *[grads[n] for n in WEIGHTS], *[delta[n] for n in WEIGHTS],
            *[new_m[n] for n in WEIGHTS], *[new_v[n] for n in WEIGHTS])
```

```python
import functools
import math

import jax
import jax.numpy as jnp
from jax import lax
from jax.experimental import pallas as pl
from jax.experimental.pallas import tpu as pltpu

F32 = jnp.float32
BF16 = jnp.bfloat16
EPS = 1e-6
HEADS = 8
HEAD_DIM = 128
WIDTH = HEADS * HEAD_DIM
CHUNK = 64
CONV_WIDTH = 4
N_CHIPS = 4
ADAM_LR, ADAM_B1, ADAM_B2, ADAM_EPS, ADAM_WD, ADAM_STEP = 0.001, 0.9, 0.999, 1e-08, 0.01, 10
V7X_VMEM_BYTES = 64 * 1024 * 1024
VMEM_LIMIT = V7X_VMEM_BYTES - 8 * 1024 * 1024
HIGHEST = lax.Precision.HIGHEST
MESH = pl.DeviceIdType.MESH


def _params(sem=None):
    return pltpu.CompilerParams(dimension_semantics=sem, vmem_limit_bytes=VMEM_LIMIT)


def _pick(n, prefs):
    for t in prefs:
        if t <= n and n % t == 0:
            return t
    return n


def _bdot(a, b, dims):
    return lax.dot_general(a.astype(BF16), b.astype(BF16), (((dims[0],), (dims[1],)), ((), ())),
                           preferred_element_type=F32)


NN, NT, TN = (1, 0), (1, 1), (0, 0)


MM_TILES = (1024, 1408, 512, 256, 128)


def matmul(a, b, form, *, out_dtype=F32, add=None, name, b_chips=None, out_chips=False):
    ns = None
    if b_chips is not None:
        ns = b.shape[3]
        b_shape = (b.shape[2], N_CHIPS * ns)
    else:
        b_shape = b.shape
    if form == "nn":
        (m, k), (k2, n) = a.shape, b_shape
    elif form == "nt":
        (m, k), (n, k2) = a.shape, b_shape
    else:
        (k, m), (k2, n) = a.shape, b_shape
    assert k == k2, (a.shape, b.shape, form)
    if out_chips:
        ns = n // N_CHIPS
    tm = _pick(m, MM_TILES)
    tn = _pick(n, MM_TILES)
    tk = k if k <= 1024 else _pick(k, MM_TILES)
    if ns is not None and (form == "nn" or out_chips):
        tn = ns
    if ns is not None and form == "nt":
        tk = ns
    nk = k // tk
    if form == "tn":
        a_spec = pl.BlockSpec((tk, tm), lambda i, j, kk: (kk, i))
    else:
        a_spec = pl.BlockSpec((tm, tk), lambda i, j, kk: (i, kk))
    if b_chips is not None and form == "nn":
        b_spec = pl.BlockSpec((None, None, tk, ns), lambda i, j, kk: (j, b_chips, kk, 0))
    elif b_chips is not None:
        b_spec = pl.BlockSpec((None, None, tn, ns), lambda i, j, kk: (kk, b_chips, j, 0))
    elif form == "nt":
        b_spec = pl.BlockSpec((tn, tk), lambda i, j, kk: (j, kk))
    else:
        b_spec = pl.BlockSpec((tk, tn), lambda i, j, kk: (kk, j))
    if out_chips:
        o_spec = pl.BlockSpec((None, tm, ns), lambda i, j, kk: (j, i, 0))
    else:
        o_spec = pl.BlockSpec((tm, tn), lambda i, j, kk: (i, j))
    dims = {"nn": NN, "nt": NT, "tn": TN}[form]
    has_add = add is not None

    def body(*refs):
        a_ref, b_ref = refs[:2]
        add_ref = refs[2] if has_add else None
        o_ref = refs[2 + has_add]

        def finish(r):
            if has_add:
                r = r + add_ref[...].astype(F32)
            o_ref[...] = r.astype(out_dtype)

        part = _bdot(a_ref[...], b_ref[...], dims)
        if nk == 1:
            finish(part)
            return
        acc_ref = refs[3 + has_add]
        kk = pl.program_id(2)

        @pl.when(kk == 0)
        def _():
            acc_ref[...] = part

        @pl.when(kk > 0)
        def _():
            acc_ref[...] += part

        @pl.when(kk == nk - 1)
        def _():
            finish(acc_ref[...])

    in_specs = [a_spec, b_spec] + ([o_spec] if has_add else [])
    args = (a, b) + ((add,) if has_add else ())
    return pl.pallas_call(
        body, name=name, grid=(m // tm, n // tn, nk), in_specs=in_specs, out_specs=o_spec,
        out_shape=jax.ShapeDtypeStruct((N_CHIPS, m, ns) if out_chips else (m, n), out_dtype),
        scratch_shapes=[pltpu.VMEM((tm, tn), F32)] if nk > 1 else [],
        compiler_params=_params(("parallel", "parallel", "arbitrary")),
    )(*args)


def _const(c):
    return lambda j: c


def rowwise(fn, rows, params, outs, accs=(), *, name, tm, ncol=1):
    t = rows[0][0].shape[0]
    tm = min(tm, t)
    assert t % tm == 0
    n_rows, n_par, n_out, n_acc = len(rows), len(params), len(outs), len(accs)

    def body(*refs):
        j, i = pl.program_id(0), pl.program_id(1)
        ins = [r[...] for r in refs[:n_rows + n_par]]
        o_refs = refs[n_rows + n_par:n_rows + n_par + n_out]
        a_refs = refs[n_rows + n_par + n_out:]
        row_outs, acc_outs = fn(*ins)
        for r, val in zip(o_refs, row_outs):
            r[...] = val.astype(r.dtype)
        for r, val, spec in zip(a_refs, acc_outs, accs):
            first = (i == 0) & (j == 0) if spec[4] else (i == 0)

            @pl.when(first)
            def _(r=r, val=val):
                r[...] = val.astype(F32)

            @pl.when(jnp.logical_not(first))
            def _(r=r, val=val):
                r[...] += val.astype(F32)

    in_specs = [pl.BlockSpec((tm, w), lambda j, i, cf=cf: (i, cf(j))) for _, w, cf in rows]
    in_specs += [pl.BlockSpec((p.shape[0], w), lambda j, i, cf=cf: (0, cf(j))) for p, w, cf in params]
    out_specs = [pl.BlockSpec((tm, w), lambda j, i, cf=cf: (i, cf(j))) for _, _, w, cf in outs]
    out_specs += [pl.BlockSpec((r, w), lambda j, i, cf=cf: (0, cf(j))) for r, _, w, cf, _ in accs]
    out_shape = [jax.ShapeDtypeStruct((t, tw), dt) for tw, dt, _, _ in outs]
    out_shape += [jax.ShapeDtypeStruct((r, tw), F32) for r, tw, _, _, _ in accs]
    res = pl.pallas_call(
        body, name=name, grid=(ncol, t // tm), in_specs=in_specs, out_specs=out_specs, out_shape=out_shape,
        compiler_params=_params(("arbitrary", "arbitrary")),
    )(*[r[0] for r in rows], *[p[0] for p in params])
    return res[:n_out], res[n_out:]


def _full(arr):
    return (arr, arr.shape[1], _const(0))


def _rms(x, g):
    x = x.astype(F32)
    return x * lax.rsqrt(jnp.mean(x * x, axis=-1, keepdims=True) + EPS) * g.astype(F32)


def _sigmoid(x):
    return 1.0 / (1.0 + jnp.exp(-x))


def _silu(x):
    return x * _sigmoid(x)


def _softplus(x):
    return jnp.maximum(x, 0.0) + jnp.log(1.0 + jnp.exp(-jnp.abs(x)))


def rms_fwd(h, g, *, name):
    d = h.shape[1]
    (hn,), _ = rowwise(lambda x, gg: ((_rms(x, gg),), ()), [_full(h)], [_full(g)],
                       [(d, BF16, d, _const(0))], name=name, tm=512)
    return hn


def rms_bwd(h, g, dhn, dh_res, *, name):
    d = h.shape[1]

    def fn(x, ct, res, gg):
        _, vjp = jax.vjp(_rms, x.astype(F32), gg.astype(F32))
        dx, dg = vjp(ct.astype(F32))
        return (res.astype(F32) + dx,), (dg,)

    (dh,), (dg,) = rowwise(fn, [_full(h), _full(dhn), _full(dh_res)], [_full(g)],
                           [(d, F32, d, _const(0))], [(1, d, d, _const(0), True)], name=name, tm=256)
    return dh, dg


def _head_rms(x, g, scale):
    x = x.astype(F32)
    return x * lax.rsqrt(jnp.mean(x * x, axis=-1, keepdims=True) + EPS) * (g.astype(F32) * scale)


def headnorm_fwd(x, col0, g, scale, *, name):
    (y,), _ = rowwise(lambda a, gg: ((_head_rms(a, gg, scale),), ()),
                      [(x, HEAD_DIM, lambda j: col0 + j)], [_full(g)],
                      [(WIDTH, BF16, HEAD_DIM, lambda j: j)], name=name, tm=1024, ncol=HEADS)
    return y


def headnorm_bwd(x, col0, g, scale, dys, *, name, out_dtype=BF16):
    def fn(a, *rest):
        cts, gg = rest[:-1], rest[-1]
        ct = sum(c.astype(F32) for c in cts)
        _, vjp = jax.vjp(lambda a_, g_: _head_rms(a_, g_, scale), a.astype(F32), gg.astype(F32))
        dx, dg = vjp(ct)
        return (dx,), (dg,)

    (dx,), (dg,) = rowwise(fn, [(x, HEAD_DIM, lambda j: col0 + j)] + [(dy, HEAD_DIM, lambda j: j) for dy in dys], [_full(g)],
                           [(WIDTH, out_dtype, HEAD_DIM, lambda j: j)],
                           [(1, HEAD_DIM, HEAD_DIM, _const(0), True)], name=name, tm=1024, ncol=HEADS)
    return dx, dg


def sum_cast(parts, dtype, *, name):
    wd = parts[0].shape[1]
    (out,), _ = rowwise(lambda *a: ((sum(b.astype(F32) for b in a),), ()), [_full(a) for a in parts], [],
                        [(wd, dtype, wd, _const(0))], name=name, tm=512)
    return out


def _gatenorm(o, gate, g):
    return _head_rms(o, g, 1.0) * _silu(gate.astype(F32))


def gatenorm_fwd(o, proj, g, *, name):
    (y,), _ = rowwise(lambda a, gt, gg: ((_gatenorm(a, gt, gg),), ()),
                      [(o, HEAD_DIM, lambda j: j), (proj, HEAD_DIM, lambda j: 3 * HEADS + j)], [_full(g)],
                      [(WIDTH, BF16, HEAD_DIM, lambda j: j)], name=name, tm=1024, ncol=HEADS)
    return y


def gatenorm_bwd(o, proj, g, dy, *, name):
    def fn(a, gt, ct, gg):
        _, vjp = jax.vjp(_gatenorm, a.astype(F32), gt.astype(F32), gg.astype(F32))
        da, dgt, dg = vjp(ct.astype(F32))
        return (da, dgt), (dg,)

    (do, dproj), (dg,) = rowwise(
        fn, [(o, HEAD_DIM, lambda j: j), (proj, HEAD_DIM, lambda j: 3 * HEADS + j), (dy, HEAD_DIM, lambda j: j)],
        [_full(g)],
        [(WIDTH, F32, HEAD_DIM, lambda j: j), (4 * WIDTH, BF16, HEAD_DIM, lambda j: 3 * HEADS + j)],
        [(1, HEAD_DIM, HEAD_DIM, _const(0), True)], name=name, tm=1024, ncol=HEADS)
    return do, dproj, dg


def _swiglu(g, u):
    return _silu(g.astype(F32)) * u.astype(F32)


def swiglu_fwd(gu, *, name):
    f = gu.shape[1] // 2
    (act,), _ = rowwise(lambda g, u: ((_swiglu(g, u),), ()), [(gu, f, _const(0)), (gu, f, _const(1))], [],
                        [(f, BF16, f, _const(0))], name=name, tm=256)
    return act


def swiglu_bwd(gu, dact, *, name):
    f = gu.shape[1] // 2

    def fn(g, u, ct):
        _, vjp = jax.vjp(_swiglu, g.astype(F32), u.astype(F32))
        dg, du = vjp(ct.astype(F32))
        return (jnp.concatenate([dg.astype(BF16), du.astype(BF16)], axis=1),), ()

    (dgu,), _ = rowwise(fn, [(gu, f, _const(0)), (gu, f, _const(1)), _full(dact)], [],
                        [(2 * f, BF16, 2 * f, _const(0))], name=name, tm=256)
    return dgu


def ple_fwd(h, pp, gt, *, name):
    d = h.shape[1]
    (out,), _ = rowwise(lambda a, b, c: ((a + b * _sigmoid(c),), ()), [_full(h), _full(pp), _full(gt)], [],
                        [(d, F32, d, _const(0))], name=name, tm=512)
    return out


def ple_bwd(dh, pp, gt, *, name):
    d = dh.shape[1]

    def fn(ct, b, c):
        s = _sigmoid(c)
        return (ct * s, ct * b * s * (1.0 - s)), ()

    (dpp, dgt), _ = rowwise(fn, [_full(dh), _full(pp), _full(gt)], [],
                            [(d, BF16, d, _const(0)), (d, BF16, d, _const(0))], name=name, tm=512)
    return dpp, dgt


def loss_head(y, tgt, *, name):
    d = y.shape[1]

    def fn(a, b):
        e = a - b
        return (e * (1.0 / d),), (jnp.sum(e * e, axis=0, keepdims=True),)

    (dy,), (sq,) = rowwise(fn, [_full(y), _full(tgt)], [], [(d, F32, d, _const(0))],
                           [(1, d, d, _const(0), True)], name=name, tm=512)
    return dy, sq


def adamw(w, g, m, v, *, name):
    shape = w.shape
    cols = shape[-1]
    flat = lambda a: a.reshape(-1, cols)
    bc1 = 1.0 - ADAM_B1 ** ADAM_STEP
    bc2 = 1.0 - ADAM_B2 ** ADAM_STEP

    def fn(w_, g_, m_, v_):
        m_ = ADAM_B1 * m_ + (1.0 - ADAM_B1) * g_
        v_ = ADAM_B2 * v_ + (1.0 - ADAM_B2) * (g_ * g_)
        delta = -ADAM_LR * ((m_ / bc1) / (jnp.sqrt(v_ / bc2) + ADAM_EPS) + ADAM_WD * w_)
        return (delta, m_, v_), ()

    o = (cols, F32, cols, _const(0))
    (d_, m_, v_), _ = rowwise(fn, [_full(flat(w)), _full(flat(g)), _full(flat(m)), _full(flat(v))], [],
                              [o, o, o], name=name, tm=256)
    return d_.reshape(shape), m_.reshape(shape), v_.reshape(shape)


CONV_STRIP = 256


def _shift_down(x, d):
    if d == 0:
        return x
    rows = lax.broadcasted_iota(jnp.int32, x.shape, 0)
    return jnp.where(rows >= d, pltpu.roll(x, d, 0), 0.0)


def _shift_up(x, d):
    if d == 0:
        return x
    t = x.shape[0]
    rows = lax.broadcasted_iota(jnp.int32, x.shape, 0)
    return jnp.where(rows < t - d, pltpu.roll(x, t - d, 0), 0.0)


def _conv(x, w):
    acc = None
    for j in range(CONV_WIDTH):
        term = _shift_down(x, CONV_WIDTH - 1 - j) * w[j:j + 1, :]
        acc = term if acc is None else acc + term
    return acc


def conv_fwd(proj, w, *, name):
    t = proj.shape[0]
    per = WIDTH // CONV_STRIP

    def body(x_ref, w_ref, o_ref):
        o_ref[0] = _silu(_conv(x_ref[...].astype(F32), w_ref[...]))

    return pl.pallas_call(
        body, name=name, grid=(3 * per,),
        in_specs=[pl.BlockSpec((t, CONV_STRIP), lambda j: (0, j)), pl.BlockSpec((CONV_WIDTH, CONV_STRIP), lambda j: (0, j))],
        out_specs=pl.BlockSpec((1, t, CONV_STRIP), lambda j: (j // per, 0, j % per)),
        out_shape=jax.ShapeDtypeStruct((3, t, WIDTH), F32),
        compiler_params=_params(("parallel",)),
    )(proj, w)


def conv_bwd(proj, w, dqkv, dproj, *, name):
    t = proj.shape[0]
    per = WIDTH // CONV_STRIP

    def body(x_ref, w_ref, d_ref, _, dx_ref, dw_ref):
        x, w_ = x_ref[...].astype(F32), w_ref[...]
        c = _conv(x, w_)
        s = _sigmoid(c)
        dc = d_ref[0] * (s + c * s * (1.0 - s))
        dx = None
        for j in range(CONV_WIDTH):
            d = CONV_WIDTH - 1 - j
            term = _shift_up(dc, d) * w_[j:j + 1, :]
            dx = term if dx is None else dx + term
            dw_ref[j:j + 1, :] = jnp.sum(dc * _shift_down(x, d), axis=0, keepdims=True)
        dx_ref[...] = dx.astype(dx_ref.dtype)

    return pl.pallas_call(
        body, name=name, grid=(3 * per,),
        in_specs=[pl.BlockSpec((t, CONV_STRIP), lambda j: (0, j)), pl.BlockSpec((CONV_WIDTH, CONV_STRIP), lambda j: (0, j)),
                  pl.BlockSpec((1, t, CONV_STRIP), lambda j: (j // per, 0, j % per)), pl.BlockSpec(memory_space=pl.ANY)],
        out_specs=[pl.BlockSpec((t, CONV_STRIP), lambda j: (0, j)), pl.BlockSpec((CONV_WIDTH, CONV_STRIP), lambda j: (0, j))],
        out_shape=[jax.ShapeDtypeStruct(dproj.shape, dproj.dtype), jax.ShapeDtypeStruct((CONV_WIDTH, 3 * WIDTH), F32)],
        input_output_aliases={3: 0},
        compiler_params=_params(("parallel",)),
    )(proj, w, dqkv, dproj)


def _gdn_gates(ab, a_log, dt_bias):
    a_in, b_in = ab[:HEADS], ab[HEADS:]
    g = -jnp.exp(a_log) * _softplus(a_in + dt_bias)
    return g, _sigmoid(b_in)


def gates_fwd(ab, a_log, dt_bias, *, name):
    t = ab.shape[1]

    def body(ab_ref, al_ref, dt_ref, g_ref, b_ref):
        g_ref[...], b_ref[...] = _gdn_gates(ab_ref[...], al_ref[...], dt_ref[...])

    s = jax.ShapeDtypeStruct((HEADS, t), F32)
    return pl.pallas_call(body, name=name, out_shape=[s, s], compiler_params=_params())(ab, a_log, dt_bias)


def gates_bwd(ab, a_log, dt_bias, dg, dbeta, *, name):
    t = ab.shape[1]

    def body(ab_ref, al_ref, dt_ref, dg_ref, db_ref, dab_ref, dal_ref, ddt_ref):
        _, vjp = jax.vjp(_gdn_gates, ab_ref[...], al_ref[...], dt_ref[...])
        dab_ref[...], dal_ref[...], ddt_ref[...] = vjp((dg_ref[...], db_ref[...]))

    c = jax.ShapeDtypeStruct((HEADS, 1), F32)
    return pl.pallas_call(body, name=name, out_shape=[jax.ShapeDtypeStruct((2 * HEADS, t), F32), c, c],
                          compiler_params=_params())(ab, a_log, dt_bias, dg, dbeta)


def _split3(x):
    hi = x.astype(BF16)
    r1 = x - hi.astype(F32)
    mid = r1.astype(BF16)
    lo = (r1 - mid.astype(F32)).astype(BF16)
    return hi, mid, lo


def _dot01(x, m01):
    hi, mid, lo = _split3(x)
    m01 = m01.astype(BF16)
    return _bdot(hi, m01, NN) + _bdot(mid, m01, NN) + _bdot(lo, m01, NN)


def _hdot(a, b, dims=NN):
    a_hi, b_hi = a.astype(BF16), b.astype(BF16)
    a_lo, b_lo = (a - a_hi.astype(F32)).astype(BF16), (b - b_hi.astype(F32)).astype(BF16)
    return _bdot(a_hi, b_hi, dims) + (_bdot(a_hi, b_lo, dims) + _bdot(a_lo, b_hi, dims))


def _rowsum(x):
    return jnp.sum(x, axis=1, keepdims=True)


def _colsum(x):
    return jnp.sum(x, axis=0, keepdims=True)


class _Heads:
    def __init__(self, vals):
        self.v = list(vals)

    def _bin(self, other, f):
        if isinstance(other, _Heads):
            return _Heads(f(a, b) for a, b in zip(self.v, other.v))
        return _Heads(f(a, other) for a in self.v)

    def __add__(self, o):
        return self._bin(o, lambda a, b: a + b)

    __radd__ = __add__

    def __sub__(self, o):
        return self._bin(o, lambda a, b: a - b)

    def __rsub__(self, o):
        return self._bin(o, lambda a, b: b - a)

    def __mul__(self, o):
        return self._bin(o, lambda a, b: a * b)

    __rmul__ = __mul__

    def __neg__(self):
        return _Heads(-a for a in self.v)


def _hmap(f, *args):
    n = next(len(a.v) for a in args if isinstance(a, _Heads))
    return _Heads(f(*[a.v[h] if isinstance(a, _Heads) else a for a in args]) for h in range(n))


def _inv_unit_lower(a, eye):
    p = jnp.where(eye, 1.0, 0.0) - a
    ak = a
    for _ in range(int(math.log2(CHUNK)) - 1):
        ak = _hmap(_hdot, ak, ak)
        p = p + _hmap(_hdot, p, ak)
    return p


def _gdn_chunk(qr, kr, v, grow, brow, tinv=None):
    c = CHUNK
    ri = lax.broadcasted_iota(jnp.int32, (c, c), 0)
    ci = lax.broadcasted_iota(jnp.int32, (c, c), 1)
    eye, lower, strict = ri == ci, ri >= ci, ri > ci
    where = lambda m: (lambda a: jnp.where(m, a, 0.0))
    to_col = lambda row: _hmap(lambda r: _rowsum(jnp.where(eye, jnp.broadcast_to(r, (c, c)), 0.0)), row)
    cum_row = _hmap(lambda g: _dot01(jnp.broadcast_to(g, (8, c)), ri <= ci)[0:1], grow)
    gcol, bcol = to_col(cum_row), to_col(brow)
    glast = _hmap(lambda g: _colsum(jnp.where(ri[:, 0:1] == c - 1, g, 0.0)), gcol)
    rq = _hmap(lambda a: lax.rsqrt(_rowsum(a * a) + EPS), qr)
    rk = _hmap(lambda a: lax.rsqrt(_rowsum(a * a) + EPS), kr)
    scale = HEAD_DIM ** -0.5
    qn, kn = qr * (rq * scale), kr * rk
    dec = _hmap(lambda gc, gr: jnp.where(lower, jnp.exp(jnp.minimum(gc - gr, 0.0)), 0.0), gcol, cum_row)
    kk = _hmap(lambda a: _bdot(a, a, NT), kn)
    qk = _hmap(lambda a, b: _bdot(a, b, NT), qn, kn)
    gam_col, e_col, gam_last = _hmap(jnp.exp, gcol), _hmap(jnp.exp, glast - gcol), _hmap(jnp.exp, glast)
    if tinv is None:
        tinv = _inv_unit_lower(_hmap(where(strict), bcol * kk * dec), eye)
    u = _hmap(_hdot, tinv, v * bcol)
    w = _hmap(_hdot, tinv, kn * (bcol * gam_col))
    return dict(eye=eye, lower=lower, strict=strict, gcol=gcol, bcol=bcol, rq=rq, rk=rk, qn=qn, kn=kn,
                dec=dec, kk=kk, qk=qk, gam_col=gam_col, e_col=e_col, gam_last=gam_last, tinv=tinv, u=u, w=w,
                aqk=qk * dec, qt=qn * gam_col, kt=kn * e_col, scale=scale)


def _head_cols(h):
    return slice(h * HEAD_DIM, (h + 1) * HEAD_DIM)


def _bd(dims):
    return lambda a, b: _bdot(a, b, dims)


def gdn_fwd(qkv, g4, b4, *, name, gather=()):
    t = qkv.shape[1]
    n = t // CHUNK
    d = HEAD_DIM
    heads = range(HEADS)
    ng = len(gather)

    def body(*refs):
        qkv_ref, g_ref, b_ref = refs[:3]
        o_ref, s0_ref, t_ref = refs[3 + ng:6 + ng]
        s_ref = refs[6 + 2 * ng]
        comm = _Gather(refs[6 + ng:6 + 2 * ng], *refs[7 + 2 * ng:]) if ng else None

        @pl.when(pl.program_id(0) == 0)
        def _():
            s_ref[...] = jnp.zeros_like(s_ref)
            if ng:
                comm.start()

        qr, kr, v = (_Heads(qkv_ref[j, :, _head_cols(h)] for h in heads) for j in range(3))
        z = _gdn_chunk(qr, kr, v, _Heads(g_ref[0, h] for h in heads), _Heads(b_ref[0, h] for h in heads))
        s0 = _Heads(s_ref[h] for h in heads)
        v_new = z["u"] - _hmap(_bd(NN), z["w"], s0)
        o = _hmap(_bd(NN), z["qt"], s0) + _hmap(_bd(NN), z["aqk"], v_new)
        s_new = s0 * z["gam_last"] + _hmap(_bd(TN), z["kt"], v_new)
        for h in heads:
            s0_ref[0, h] = s0.v[h]
            t_ref[0, h] = z["tinv"].v[h]
            o_ref[:, _head_cols(h)] = o.v[h]
            s_ref[h] = s_new.v[h]

        if ng:
            @pl.when(pl.program_id(0) == n - 1)
            def _():
                comm.finish()

    gspec = pl.BlockSpec((1, HEADS, 1, CHUNK), lambda i: (i, 0, 0, 0))
    return pl.pallas_call(
        body, name=name, grid=(n,),
        in_specs=[pl.BlockSpec((3, CHUNK, WIDTH), lambda i: (0, i, 0)), gspec, gspec] + [ANY] * ng,
        out_specs=[pl.BlockSpec((CHUNK, WIDTH), lambda i: (i, 0)),
                   pl.BlockSpec((1, HEADS, d, d), lambda i: (i, 0, 0, 0)),
                   pl.BlockSpec((1, HEADS, CHUNK, CHUNK), lambda i: (i, 0, 0, 0))] + [ANY] * ng,
        out_shape=[jax.ShapeDtypeStruct((t, WIDTH), F32), jax.ShapeDtypeStruct((n, HEADS, d, d), F32),
                   jax.ShapeDtypeStruct((n, HEADS, CHUNK, CHUNK), F32)]
        + [jax.ShapeDtypeStruct(s.shape, s.dtype) for s in gather],
        input_output_aliases={3 + b: 3 + b for b in range(ng)},
        scratch_shapes=[pltpu.VMEM((HEADS, d, d), F32)] + (_dma_sems(6 * ng) if ng else []),
        compiler_params=_params(("arbitrary",)),
    )(qkv, g4, b4, *gather)


def gdn_bwd(qkv, g4, b4, s0_all, tinv_all, do, *, name):
    t = qkv.shape[1]
    n = t // CHUNK
    d = HEAD_DIM
    c = CHUNK
    heads = range(HEADS)

    def body(qkv_ref, g_ref, b_ref, s0_ref, t_ref, do_ref, dqkv_ref, dg_ref, db_ref, ds_ref):
        @pl.when(pl.program_id(0) == 0)
        def _():
            ds_ref[...] = jnp.zeros_like(ds_ref)

        qr, kr, v = (_Heads(qkv_ref[j, :, _head_cols(h)] for h in heads) for j in range(3))
        z = _gdn_chunk(qr, kr, v, _Heads(g_ref[0, h] for h in heads), _Heads(b_ref[0, h] for h in heads),
                       tinv=_Heads(t_ref[0, h] for h in heads))
        s0 = _Heads(s0_ref[0, h] for h in heads)
        ds = _Heads(ds_ref[h] for h in heads)
        dout = _Heads(do_ref[:, _head_cols(h)] for h in heads)
        qn, kn, u, w, dec, kk, qk = z["qn"], z["kn"], z["u"], z["w"], z["dec"], z["kk"], z["qk"]
        bcol, gam_col, e_col, gam_last = z["bcol"], z["gam_col"], z["e_col"], z["gam_last"]
        low = lambda a: jnp.where(z["lower"], a, 0.0)
        strict = lambda a: jnp.where(z["strict"], a, 0.0)
        rowsum = lambda a: _hmap(_rowsum, a)
        colsum = lambda a: _hmap(_colsum, a)
        v_new = u - _hmap(_bd(NN), w, s0)
        dv_new = _hmap(_bd(TN), z["aqk"], dout) + _hmap(_bd(NN), z["kt"], ds)
        daqk = _hmap(low, _hmap(_bd(NT), dout, v_new))
        dqt = _hmap(_bd(NT), dout, s0)
        dkt = _hmap(_bd(NT), v_new, ds)
        dgam_last = _hmap(lambda a, b: jnp.sum(a * b, keepdims=True), ds, s0)
        ds_new = _hmap(_bd(TN), z["qt"], dout) + ds * gam_last - _hmap(_bd(TN), w, dv_new)
        dw = -_hmap(_bd(NT), dv_new, s0)
        hd_t = lambda a, b: _hdot(a, b, TN)
        dru = _hmap(hd_t, z["tinv"], dv_new)
        drw = _hmap(hd_t, z["tinv"], dw)
        dal = -_hmap(strict, _hmap(_bd(NT), dru, u) + _hmap(_bd(NT), drw, w))
        t1 = dal * kk * dec
        dkk = dal * bcol * dec
        ddec = dal * bcol * kk + daqk * qk
        dqk = daqk * dec
        s_w = rowsum(drw * kn)
        dbeta_col = rowsum(t1) + rowsum(dru * v) + gam_col * s_w
        dkn = (drw * (bcol * gam_col) + _hmap(_bd(NN), dkk, kn) + _hmap(_bd(TN), dkk, kn) + _hmap(_bd(TN), dqk, qn)
               + dkt * e_col)
        dqn = _hmap(_bd(NN), dqk, kn) + dqt * gam_col
        e_mat = ddec * dec
        de_col = rowsum(dkt * kn)
        diag_of_colsum = rowsum(_hmap(lambda a: jnp.where(z["eye"], jnp.broadcast_to(_colsum(a), (c, c)), 0.0), e_mat))
        dg_cum = rowsum(e_mat) + (bcol * s_w + rowsum(dqt * qn)) * gam_col - de_col * e_col - diag_of_colsum
        dg_last = colsum(de_col * e_col) + dgam_last * gam_last
        dg = colsum(_hmap(lambda a: jnp.where(z["lower"], a, 0.0), dg_cum)) + dg_last
        dbeta = colsum(_hmap(lambda a: jnp.where(z["eye"], a, 0.0), dbeta_col))
        rq, rk = z["rq"], z["rk"]
        dqr = z["scale"] * (rq * dqn - qr * (rq * rq * rq) * rowsum(qr * dqn))
        dkr = rk * dkn - kr * (rk * rk * rk) * rowsum(kr * dkn)
        dv = dru * bcol
        for h in heads:
            ds_ref[h] = ds_new.v[h]
            dg_ref[0, h] = dg.v[h]
            db_ref[0, h] = dbeta.v[h]
            dqkv_ref[0, :, _head_cols(h)] = dqr.v[h]
            dqkv_ref[1, :, _head_cols(h)] = dkr.v[h]
            dqkv_ref[2, :, _head_cols(h)] = dv.v[h]

    rev = lambda i: n - 1 - i
    gspec = pl.BlockSpec((1, HEADS, 1, CHUNK), lambda i: (rev(i), 0, 0, 0))
    return pl.pallas_call(
        body, name=name, grid=(n,),
        in_specs=[pl.BlockSpec((3, CHUNK, WIDTH), lambda i: (0, rev(i), 0)), gspec, gspec,
                  pl.BlockSpec((1, HEADS, d, d), lambda i: (rev(i), 0, 0, 0)),
                  pl.BlockSpec((1, HEADS, CHUNK, CHUNK), lambda i: (rev(i), 0, 0, 0)),
                  pl.BlockSpec((CHUNK, WIDTH), lambda i: (rev(i), 0))],
        out_specs=[pl.BlockSpec((3, CHUNK, WIDTH), lambda i: (0, rev(i), 0)), gspec, gspec],
        out_shape=[jax.ShapeDtypeStruct((3, t, WIDTH), F32), jax.ShapeDtypeStruct((n, HEADS, 1, CHUNK), F32),
                   jax.ShapeDtypeStruct((n, HEADS, 1, CHUNK), F32)],
        scratch_shapes=[pltpu.VMEM((HEADS, d, d), F32)],
        compiler_params=_params(("arbitrary",)),
    )(qkv, g4, b4, s0_all, tinv_all, do)


SB_BLOCK = 256


def _dot01_2(x, m01):
    hi = x.astype(BF16)
    lo = (x - hi.astype(F32)).astype(BF16)
    return _bdot(hi, m01, NN) + _bdot(lo, m01, NN)


SB_HEADS = 2


def _sb_weights(q, kb, carry, mask, upper):
    z = _hmap(_bd(NT), q, kb)
    ls = _hmap(lambda z_: jnp.minimum(z_, 0.0) - jnp.log(1.0 + jnp.exp(-jnp.abs(z_))), z)
    ln = _hmap(lambda l_, z_: jnp.where(mask, l_ - z_, 0.0), ls, z)
    suffix = _hmap(lambda l_: _dot01_2(l_, upper), ln)
    a = _hmap(lambda l_, s_, c_: jnp.where(mask, jnp.exp(l_ + s_ + c_), 0.0), ls, suffix, carry)
    return z, ln, a


SB_DEAD = -105.0


def _sb_alive(s, i, carries):
    top = jnp.max(carries[0])
    for c in carries[1:]:
        top = jnp.maximum(top, jnp.max(c))
    return (s <= i) & (top > SB_DEAD)


def _sb_masks(i, jb, blk):
    ri = lax.broadcasted_iota(jnp.int32, (blk, blk), 0)
    ci = lax.broadcasted_iota(jnp.int32, (blk, blk), 1)
    return (jb * blk + ci) < (i * blk + ri)


def sb_fwd(q, k, v, *, name, gather=()):
    t = q.shape[0]
    blk = min(SB_BLOCK, t)
    d = HEAD_DIM
    hs = range(SB_HEADS)
    ng = len(gather)
    groups, nb = HEADS // SB_HEADS, t // blk

    def body(*refs):
        q_ref, k_ref, v_ref = refs[:3]
        o_ref = refs[3 + ng]
        comm = _Gather(refs[4 + ng:4 + 2 * ng], *refs[4 + 2 * ng:]) if ng else None
        i = pl.program_id(1)
        if ng:
            @pl.when((pl.program_id(0) == 0) & (i == 0))
            def _():
                comm.start()

        qb = _Heads(q_ref[:, _head_cols(h)] for h in hs)
        ri = lax.broadcasted_iota(jnp.int32, (blk, blk), 0)
        ci = lax.broadcasted_iota(jnp.int32, (blk, blk), 1)
        upper = (ri > ci).astype(BF16)

        def step(state):
            s, cs, accs = state
            jb = i - s
            rows = pl.ds(pl.multiple_of(jb * blk, blk), blk)
            kb = _Heads(k_ref[rows, _head_cols(h)] for h in hs)
            vb = _Heads(v_ref[rows, _head_cols(h)] for h in hs)
            _, ln, a = _sb_weights(qb, kb, _Heads(cs), _sb_masks(i, jb, blk), upper)
            cs = _Heads(cs) + _hmap(_rowsum, ln)
            accs = _Heads(accs) + _hmap(_bd(NN), a, vb)
            return s + 1, tuple(cs.v), tuple(accs.v)

        init = (jnp.int32(0), tuple(jnp.zeros((blk, 1), F32) for _ in hs), tuple(jnp.zeros((blk, d), F32) for _ in hs))
        _, _, accs = lax.while_loop(lambda st: _sb_alive(st[0], i, st[1]), step, init)
        for h in hs:
            o_ref[:, _head_cols(h)] = accs[h].astype(o_ref.dtype)

        if ng:
            @pl.when((pl.program_id(0) == groups - 1) & (i == nb - 1))
            def _():
                comm.finish()

    qspec = pl.BlockSpec((blk, SB_HEADS * d), lambda g, i: (i, g))
    kspec = pl.BlockSpec((t, SB_HEADS * d), lambda g, i: (0, g))
    return pl.pallas_call(
        body, name=name, grid=(groups, nb), in_specs=[qspec, kspec, kspec] + [ANY] * ng,
        out_specs=[qspec] + [ANY] * ng,
        out_shape=[jax.ShapeDtypeStruct((t, WIDTH), BF16)] + [jax.ShapeDtypeStruct(s.shape, s.dtype) for s in gather],
        input_output_aliases={3 + b: 1 + b for b in range(ng)},
        scratch_shapes=_dma_sems(6 * ng) if ng else [],
        compiler_params=_params(("arbitrary", "arbitrary") if ng else ("parallel", "arbitrary")),
    )(q, k, v, *gather)


def sb_bwd(q, k, v, do, *, name):
    t = q.shape[0]
    blk = min(SB_BLOCK, t)
    d = HEAD_DIM
    nb = t // blk
    hs = range(SB_HEADS)

    def body(q_ref, k_ref, v_ref, do_ref, dq_ref, dk_ref, dv_ref, p_buf, z_buf):
        i = pl.program_id(1)

        @pl.when(i == 0)
        def _():
            dk_ref[...] = jnp.zeros_like(dk_ref)
            dv_ref[...] = jnp.zeros_like(dv_ref)

        qb = _Heads(q_ref[:, _head_cols(h)] for h in hs)
        dob = _Heads(do_ref[:, _head_cols(h)] for h in hs)
        ri = lax.broadcasted_iota(jnp.int32, (blk, blk), 0)
        ci = lax.broadcasted_iota(jnp.int32, (blk, blk), 1)
        upper = (ri > ci).astype(BF16)
        lower = (ri < ci).astype(BF16)

        def right_to_left(state):
            s, cs = state
            jb = i - s
            rows = pl.ds(pl.multiple_of(jb * blk, blk), blk)
            kb = _Heads(k_ref[rows, _head_cols(h)] for h in hs)
            vb = _Heads(v_ref[rows, _head_cols(h)] for h in hs)
            z, ln, a = _sb_weights(qb, kb, _Heads(cs), _sb_masks(i, jb, blk), upper)
            p = a * _hmap(_bd(NT), dob, vb)
            dv = _hmap(_bd(TN), a, dob)
            for h in hs:
                p_buf[h, jb] = p.v[h]
                z_buf[h, jb] = z.v[h]
                dv_ref[rows, _head_cols(h)] += dv.v[h]
            return s + 1, tuple((_Heads(cs) + _hmap(_rowsum, ln)).v)

        n_done, _ = lax.while_loop(lambda st: _sb_alive(st[0], i, st[1]), right_to_left,
                                   (jnp.int32(0), tuple(jnp.zeros((blk, 1), F32) for _ in hs)))

        def left_to_right(jb, carry):
            cps, dqs = carry
            rows = pl.ds(pl.multiple_of(jb * blk, blk), blk)
            mask = _sb_masks(i, jb, blk)
            kb = _Heads(k_ref[rows, _head_cols(h)] for h in hs)
            p = _Heads(p_buf[h, jb] for h in hs)
            sg = _hmap(_sigmoid, _Heads(z_buf[h, jb] for h in hs))
            prefix = _hmap(lambda a: _dot01_2(a, lower), p) + _Heads(cps)
            dz = _hmap(lambda a: jnp.where(mask, a, 0.0), p * (1.0 - sg) - sg * prefix)
            dk = _hmap(_bd(TN), dz, qb)
            for h in hs:
                dk_ref[rows, _head_cols(h)] += dk.v[h]
            return tuple((_Heads(cps) + _hmap(_rowsum, p)).v), tuple((_Heads(dqs) + _hmap(_bd(NN), dz, kb)).v)

        _, dqs = lax.fori_loop(i + 1 - n_done, i + 1, left_to_right,
                               (tuple(jnp.zeros((blk, 1), F32) for _ in hs), tuple(jnp.zeros((blk, d), F32) for _ in hs)))
        for h in hs:
            dq_ref[:, _head_cols(h)] = dqs[h]

    qspec = pl.BlockSpec((blk, SB_HEADS * d), lambda g, i: (i, g))
    kspec = pl.BlockSpec((t, SB_HEADS * d), lambda g, i: (0, g))
    s = jax.ShapeDtypeStruct((t, WIDTH), F32)
    buf = pltpu.VMEM((SB_HEADS, nb, blk, blk), F32)
    return pl.pallas_call(
        body, name=name, grid=(HEADS // SB_HEADS, nb), in_specs=[qspec, kspec, kspec, qspec],
        out_specs=[qspec, kspec, kspec], out_shape=[s, s, s], scratch_shapes=[buf, buf],
        compiler_params=_params(("parallel", "arbitrary")),
    )(q, k, v, do)


PACK_COLS = 1024
ANY = pl.BlockSpec(memory_space=pl.ANY)


def _mesh_pos():
    return lax.axis_index("x"), lax.axis_index("y"), lax.axis_index("c")


def _other_chips(x, y):
    return [(1 - x, y), (x, 1 - y), (1 - x, 1 - y)]


def _dma_sems(n):
    return [pltpu.SemaphoreType.DMA((n,)), pltpu.SemaphoreType.DMA((n,))]


class _Gather:
    def __init__(self, o_refs, send_sems, recv_sems):
        self.o_refs, self.send_sems, self.recv_sems = o_refs, send_sems, recv_sems

    def _copy(self, b, k, chip, hf, to):
        rows = self.o_refs[b].at[chip, hf]
        return pltpu.make_async_remote_copy(src_ref=rows, dst_ref=rows, send_sem=self.send_sems.at[6 * b + k],
                                            recv_sem=self.recv_sems.at[6 * b + k], device_id=to, device_id_type=MESH)

    def start(self):
        x, y, c = _mesh_pos()
        for b in range(len(self.o_refs)):
            for k, (cx, cy) in enumerate(_other_chips(x, y)):
                self._copy(b, k, 2 * x + y, c, (cx, cy, c)).start()

    def finish(self):
        x, y, c = _mesh_pos()
        chips = _other_chips(x, y)
        for b in range(len(self.o_refs)):
            for k, (cx, cy) in enumerate(chips):
                self._copy(b, k, 2 * cx + cy, c, (x, y, c)).wait_recv()
                self._copy(b, 3 + k, 2 * cx + cy, c, (x, y, 1 - c)).start()
        for b in range(len(self.o_refs)):
            for k, (cx, cy) in enumerate(chips):
                self._copy(b, 3 + k, 2 * cx + cy, 1 - c, (x, y, c)).wait_recv()
                self._copy(b, k, 2 * x + y, c, (cx, cy, c)).wait_send()
                self._copy(b, 3 + k, 2 * cx + cy, c, (x, y, 1 - c)).wait_send()


def all_gather_chips(slots, *, name):
    nb = len(slots)

    def body(*refs):
        g = _Gather(refs[nb:2 * nb], *refs[2 * nb:])
        g.start()
        g.finish()

    return pl.pallas_call(
        body, name=name, in_specs=[ANY] * nb, out_specs=[ANY] * nb, input_output_aliases={b: b for b in range(nb)},
        out_shape=[jax.ShapeDtypeStruct(s.shape, s.dtype) for s in slots], scratch_shapes=_dma_sems(6 * nb),
    )(*slots)


def sibling_swap(gs, *, name):
    nb = len(gs)

    def body(*refs):
        g_refs, o_refs, (send_sems, recv_sems) = refs[:nb], refs[nb:2 * nb], refs[2 * nb:]
        x, y, c = _mesh_pos()
        cps = [pltpu.make_async_remote_copy(src_ref=g_refs[b].at[j, 1 - c], dst_ref=o_refs[b].at[j],
                                            send_sem=send_sems.at[N_CHIPS * b + j], recv_sem=recv_sems.at[N_CHIPS * b + j],
                                            device_id=(x, y, 1 - c), device_id_type=MESH)
               for b in range(nb) for j in range(N_CHIPS)]
        for cp in cps:
            cp.start()
        for cp in cps:
            cp.wait()

    return pl.pallas_call(
        body, name=name, in_specs=[ANY] * nb, out_specs=[ANY] * nb,
        out_shape=[jax.ShapeDtypeStruct((g.shape[0],) + g.shape[2:], g.dtype) for g in gs],
        scratch_shapes=_dma_sems(N_CHIPS * nb),
    )(*gs)


def chip_exchange(s1s, *, name):
    nb = len(s1s)

    def body(*refs):
        s_refs, o_refs, (send_sems, recv_sems) = refs[:nb], refs[nb:2 * nb], refs[2 * nb:]
        x, y, c = _mesh_pos()
        cps = [pltpu.make_async_remote_copy(src_ref=s_refs[b].at[2 * cx + cy], dst_ref=o_refs[b].at[k],
                                            send_sem=send_sems.at[3 * b + k], recv_sem=recv_sems.at[3 * b + k],
                                            device_id=(cx, cy, c), device_id_type=MESH)
               for b in range(nb) for k, (cx, cy) in enumerate(_other_chips(x, y))]
        for cp in cps:
            cp.start()
        for cp in cps:
            cp.wait()

    return pl.pallas_call(
        body, name=name, in_specs=[ANY] * nb, out_specs=[ANY] * nb,
        out_shape=[jax.ShapeDtypeStruct((3,) + s.shape[1:], s.dtype) for s in s1s], scratch_shapes=_dma_sems(3 * nb),
    )(*s1s)


def sibling_merge(halves, *, name):
    nb = len(halves)

    def body(*refs):
        o_refs, (send_sems, recv_sems) = refs[nb:2 * nb], refs[2 * nb:]
        x, y, c = _mesh_pos()
        cps = [pltpu.make_async_remote_copy(src_ref=o_refs[b].at[c], dst_ref=o_refs[b].at[c], send_sem=send_sems.at[b],
                                            recv_sem=recv_sems.at[b], device_id=(x, y, 1 - c), device_id_type=MESH)
               for b in range(nb)]
        for cp in cps:
            cp.start()
        for cp in cps:
            cp.wait()

    return pl.pallas_call(
        body, name=name, in_specs=[ANY] * nb, out_specs=[ANY] * nb, input_output_aliases={b: b for b in range(nb)},
        out_shape=[jax.ShapeDtypeStruct(h.shape, h.dtype) for h in halves], scratch_shapes=_dma_sems(nb),
    )(*halves)


def all_reduce_small(buf, *, name):
    n_dev = 8

    def body(b_ref, o_ref, recv_buf, send_sems, recv_sems):
        x, y, c = _mesh_pos()
        me = 4 * x + 2 * y + c
        pos = lambda t: (t // 4, (t // 2) % 2, t % 2)

        def copy(t, slot):
            return pltpu.make_async_remote_copy(src_ref=b_ref, dst_ref=recv_buf.at[slot], send_sem=send_sems.at[t],
                                                recv_sem=recv_sems.at[slot], device_id=pos(t), device_id_type=MESH)

        for t in range(n_dev):
            @pl.when(t != me)
            def _(t=t):
                copy(t, me).start()

        recv_buf[me] = b_ref[...]
        for t in range(n_dev):
            @pl.when(t != me)
            def _(t=t):
                copy(t, t).wait_recv()
                copy(t, me).wait_send()

        acc = recv_buf[0]
        for t in range(1, n_dev):
            acc = acc + recv_buf[t]
        o_ref[...] = acc

    return pl.pallas_call(
        body, name=name, out_shape=jax.ShapeDtypeStruct(buf.shape, F32),
        in_specs=[pl.BlockSpec(memory_space=pltpu.VMEM)], out_specs=pl.BlockSpec(memory_space=pltpu.VMEM),
        scratch_shapes=[pltpu.VMEM((n_dev,) + buf.shape, F32), pltpu.SemaphoreType.DMA((n_dev,)),
                        pltpu.SemaphoreType.DMA((n_dev,))],
    )(buf)


REDUCE_ROWS = (512, 384, 256, 128)


def add_selected(sel, a5, b, *, name):
    n, _, rh, w = a5.shape
    tr = _pick(rh, REDUCE_ROWS)

    def body(sel_ref, a_ref, b_ref, o_ref, ob_ref):
        s = a_ref[...] + b_ref[...]
        o_ref[...] = s
        ob_ref[...] = s.astype(BF16)

    blk = pl.BlockSpec((None, tr, w), lambda j, i, s: (j, i, 0))
    return pl.pallas_call(
        body, name=name,
        grid_spec=pltpu.PrefetchScalarGridSpec(
            num_scalar_prefetch=1, grid=(n, rh // tr),
            in_specs=[pl.BlockSpec((None, None, tr, w), lambda j, i, s: (j, s[0], i, 0)), blk], out_specs=[blk, blk]),
        out_shape=[jax.ShapeDtypeStruct((n, rh, w), F32), jax.ShapeDtypeStruct((n, rh, w), BF16)],
        compiler_params=_params(("arbitrary", "arbitrary")),
    )(sel, a5, b)


def add_chip_sums(sel, s1, b2, *, name):
    _, rh, w = s1.shape
    tr = _pick(rh, REDUCE_ROWS)

    def body(sel_ref, s_ref, b_ref, o_ref):
        o_ref[...] = ((s_ref[...] + b_ref[0].astype(F32)) + b_ref[1].astype(F32)) + b_ref[2].astype(F32)

    return pl.pallas_call(
        body, name=name,
        grid_spec=pltpu.PrefetchScalarGridSpec(
            num_scalar_prefetch=1, grid=(rh // tr,),
            in_specs=[pl.BlockSpec((None, tr, w), lambda i, s: (s[0], i, 0)), pl.BlockSpec((3, tr, w), lambda i, s: (0, i, 0))],
            out_specs=pl.BlockSpec((None, tr, w), lambda i, s: (s[1], i, 0))),
        out_shape=jax.ShapeDtypeStruct((2, rh, w), F32),
        compiler_params=_params(("arbitrary",)),
    )(sel, s1, b2)


BIG = (("gdn_w_out", 1), ("sb_w_q", 1), ("sb_w_out", 1), ("ffn_w_out", 1), ("ple_w_gate", 1), ("w_kv", 1),
       ("ple_w_proj", 2))
SMALL = ("ln_mix", "ln_ffn", "ln_ple", "gdn_a_log", "gdn_dt_bias", "gdn_norm", "kv_norm", "k_norm", "sb_q_norm")
WEIGHTS = ("ln_mix", "ln_ffn", "ln_ple", "gdn_w_in", "gdn_conv", "gdn_a_log", "gdn_dt_bias", "gdn_norm", "gdn_w_out",
           "kv_norm", "w_kv", "k_norm", "sb_w_q", "sb_q_norm", "sb_w_out", "ffn_w_in", "ffn_w_out", "ple_w_proj",
           "ple_w_gate")
PACK_ALIGN = 256


ROW_TILE = 16


def _rows_of(shape, tile=ROW_TILE):
    return -(-math.prod(shape) // (PACK_COLS * tile)) * tile


WEIGHT_ALIGN = 32


def _pack_rows(arrs, lead, align=PACK_ALIGN):
    parts = []
    for a in arrs:
        if a.shape[-1] == PACK_COLS:
            parts.append(a.reshape(lead + (-1, PACK_COLS)))
            continue
        flat = a.reshape(lead + (-1,))
        pad = _rows_of(a.shape[len(lead):]) * PACK_COLS - flat.shape[-1]
        if pad:
            flat = jnp.pad(flat, [(0, 0)] * len(lead) + [(0, pad)])
        parts.append(flat.reshape(lead + (-1, PACK_COLS)))
    rows = sum(q.shape[len(lead)] for q in parts)
    filler = -rows % align
    if filler:
        parts.append(jnp.zeros(lead + (filler, PACK_COLS), parts[0].dtype))
    return jnp.concatenate(parts, axis=len(lead))


def _own_slot(buf, chip):
    mine = lax.broadcasted_iota(jnp.int32, (N_CHIPS, 1, 1), 0) == chip
    slots = jnp.where(mine, buf[None], jnp.zeros((), buf.dtype))
    return slots.reshape(N_CHIPS, 2, buf.shape[0] // 2, buf.shape[1])


def _unpack_rows(buf, shapes, lead):
    out, r0 = [], 0
    for s in shapes:
        rows = _rows_of(s)
        flat = buf[(slice(None),) * len(lead) + (slice(r0, r0 + rows),)].reshape(lead + (-1,))
        out.append(flat[..., :math.prod(s)].reshape(lead + tuple(s)))
        r0 += rows
    return out


def _unshard(g, axis):
    g = jnp.moveaxis(g, 0, axis)
    s = g.shape
    return g.reshape(s[:axis] + (s[axis] * s[axis + 1],) + s[axis + 2:])


def _shard(full, axis):
    s = full.shape
    return jnp.moveaxis(full.reshape(s[:axis] + (N_CHIPS, s[axis] // N_CHIPS) + s[axis + 1:]), axis, 0)


def _to4(a):
    return a.reshape(HEADS, -1, 1, CHUNK).transpose(1, 0, 2, 3)


def _from4(a):
    return a.transpose(1, 0, 2, 3).reshape(HEADS, -1)


def _row(vec):
    flat = vec.reshape(-1)
    rows = _rows_of(flat.shape, 1)
    return jnp.pad(flat, (0, rows * PACK_COLS - flat.shape[0])).reshape(rows, PACK_COLS)


def kernel(x, p, ln_mix, ln_ffn, ln_ple, gdn_w_in, gdn_conv, gdn_a_log, gdn_dt_bias, gdn_norm, gdn_w_out, kv_norm, w_kv, k_norm, sb_w_q, sb_q_norm, sb_w_out, ffn_w_in, ffn_w_out, ple_w_proj, ple_w_gate, loss_target, m_ln_mix, m_ln_ffn, m_ln_ple, m_gdn_w_in, m_gdn_conv, m_gdn_a_log, m_gdn_dt_bias, m_gdn_norm, m_gdn_w_out, m_kv_norm, m_w_kv, m_k_norm, m_sb_w_q, m_sb_q_norm, m_sb_w_out, m_ffn_w_in, m_ffn_w_out, m_ple_w_proj, m_ple_w_gate, v_ln_mix, v_ln_ffn, v_ln_ple, v_gdn_w_in, v_gdn_conv, v_gdn_a_log, v_gdn_dt_bias, v_gdn_norm, v_gdn_w_out, v_kv_norm, v_w_kv, v_k_norm, v_sb_w_q, v_sb_q_norm, v_sb_w_out, v_ffn_w_in, v_ffn_w_out, v_ple_w_proj, v_ple_w_gate):
    w = dict(ln_mix=ln_mix, ln_ffn=ln_ffn, ln_ple=ln_ple, gdn_w_in=gdn_w_in, gdn_conv=gdn_conv, gdn_a_log=gdn_a_log,
             gdn_dt_bias=gdn_dt_bias, gdn_norm=gdn_norm, gdn_w_out=gdn_w_out, kv_norm=kv_norm, w_kv=w_kv, k_norm=k_norm,
             sb_w_q=sb_w_q, sb_q_norm=sb_q_norm, sb_w_out=sb_w_out, ffn_w_in=ffn_w_in, ffn_w_out=ffn_w_out,
             ple_w_proj=ple_w_proj, ple_w_gate=ple_w_gate)
    mom1 = dict(ln_mix=m_ln_mix, ln_ffn=m_ln_ffn, ln_ple=m_ln_ple, gdn_w_in=m_gdn_w_in, gdn_conv=m_gdn_conv,
                gdn_a_log=m_gdn_a_log, gdn_dt_bias=m_gdn_dt_bias, gdn_norm=m_gdn_norm, gdn_w_out=m_gdn_w_out,
                kv_norm=m_kv_norm, w_kv=m_w_kv, k_norm=m_k_norm, sb_w_q=m_sb_w_q, sb_q_norm=m_sb_q_norm,
                sb_w_out=m_sb_w_out, ffn_w_in=m_ffn_w_in, ffn_w_out=m_ffn_w_out, ple_w_proj=m_ple_w_proj,
                ple_w_gate=m_ple_w_gate)
    mom2 = dict(ln_mix=v_ln_mix, ln_ffn=v_ln_ffn, ln_ple=v_ln_ple, gdn_w_in=v_gdn_w_in, gdn_conv=v_gdn_conv,
                gdn_a_log=v_gdn_a_log, gdn_dt_bias=v_gdn_dt_bias, gdn_norm=v_gdn_norm, gdn_w_out=v_gdn_w_out,
                kv_norm=v_kv_norm, w_kv=v_w_kv, k_norm=v_k_norm, sb_w_q=v_sb_w_q, sb_q_norm=v_sb_q_norm,
                sb_w_out=v_sb_w_out, ffn_w_in=v_ffn_w_in, ffn_w_out=v_ffn_w_out, ple_w_proj=v_ple_w_proj,
                ple_w_gate=v_ple_w_gate)
    depth = ln_mix.shape[0]
    n_a = gdn_w_in.shape[0]
    xi, yi, ci = _mesh_pos()
    chip = 2 * xi + yi
    sel_c = jnp.reshape(ci, (1,)).astype(jnp.int32)
    sel_chip = jnp.stack([chip, ci]).astype(jnp.int32)
    h = x[0]
    tgt = loss_target[0]
    t = h.shape[0]

    shard_shapes = [w[n].shape for n, _ in BIG]

    def layer_items(i):
        if i < n_a:
            items = [("gdn_w_out", i, 0), ("ffn_w_out", i, 0), ("ple_w_gate", i, 0), ("ple_w_proj", i, 1)]
            return items + ([("w_kv", None, 1)] if i == n_a - 1 else [])
        j = i - n_a
        return [("sb_w_q", j, 0), ("sb_w_out", j, 0), ("ffn_w_out", i, 0), ("ple_w_gate", i, 0), ("ple_w_proj", i, 1)]

    def layer_shards(i):
        return [w[n] if idx is None else w[n][idx] for n, idx, _ in layer_items(i)]

    def layer_slots(i):
        packed = _pack_rows([q.astype(BF16) for q in layer_shards(i)], (), align=WEIGHT_ALIGN)
        own = [packed, ffn_w_in[i].astype(BF16)] + ([gdn_w_in[i].astype(BF16)] if i < n_a else [])
        return [_own_slot(b, chip) for b in own]

    def layer_weights(i, got):
        parts = _unpack_rows(got[0].reshape(N_CHIPS, -1, PACK_COLS), [q.shape for q in layer_shards(i)], (N_CHIPS,))
        out = {n: _unshard(g, ax) for (n, _, ax), g in zip(layer_items(i), parts)}
        out["ffn_w_in"] = got[1].reshape((N_CHIPS, 1) + ffn_w_in.shape[1:])
        if i < n_a:
            out["gdn_w_in"] = _unshard(got[2].reshape((N_CHIPS,) + gdn_w_in.shape[1:]), 1)
        return out

    slots = [layer_slots(i) for i in range(depth)]
    wl = [layer_weights(0, all_gather_chips(slots[0], name="all_gather_weights"))]
    conv_rows = _rows_of(gdn_conv.shape, 1)
    small_rows = sum(_rows_of(w[n].shape, 1) for n in SMALL)
    buf_rows = -(-(small_rows + N_CHIPS * conv_rows) // 8) * 8
    conv_buf = jnp.zeros((buf_rows, PACK_COLS), F32)
    conv_buf = lax.dynamic_update_slice(conv_buf, _row(gdn_conv) * (ci == 0).astype(F32), (chip * conv_rows, 0))
    conv_all = all_reduce_small(conv_buf, name="all_reduce_small")[:N_CHIPS * conv_rows]
    conv_full = _unshard(conv_all.reshape(N_CHIPS, -1)[:, :math.prod(gdn_conv.shape)].reshape((N_CHIPS,) + gdn_conv.shape), 2)

    saved = []
    k_sh = v_sh = None
    mid = None
    for i in range(depth):
        s = dict(h0=h)
        wi = wl[i]
        nxt = slots[i + 1] if i + 1 < depth else ()
        s["hn"] = hn = rms_fwd(h, ln_mix[i:i + 1], name="rms_fwd")
        if i < n_a:
            w_in = wi["gdn_w_in"]
            s["w_m"], s["w_abt"] = w_in[:, :4 * WIDTH], w_in[:, 4 * WIDTH:].T
            s["proj"] = proj = matmul(hn, s["w_m"], "nn", out_dtype=BF16, name="mm_gdn_in")
            s["ab"] = ab = matmul(s["w_abt"], hn, "nt", name="mm_gdn_ab")
            s["a_log"], s["dt"] = gdn_a_log[i][:, None], gdn_dt_bias[i][:, None]
            g8, b8 = gates_fwd(ab, s["a_log"], s["dt"], name="gates_fwd")
            s["g4"], s["b4"] = _to4(g8), _to4(b8)
            s["qkv"] = qkv = conv_fwd(proj, conv_full[i], name="conv_fwd")
            s["o"], s["s0"], s["tinv"], *got = gdn_fwd(qkv, s["g4"], s["b4"], gather=nxt, name="gdn_fwd")
            s["y"] = y = gatenorm_fwd(s["o"], proj, gdn_norm[i:i + 1], name="gatenorm_fwd")
            h = matmul(y, wi["gdn_w_out"], "nn", add=h, name="mm_out")
        else:
            j = i - n_a
            s["qraw"] = qraw = matmul(hn, wi["sb_w_q"], "nn", name="mm_sq")
            s["q"] = q = headnorm_fwd(qraw, 0, sb_q_norm[j:j + 1], HEAD_DIM ** -0.5, name="headnorm_q")
            s["o"], *got = sb_fwd(q, k_sh, v_sh, gather=nxt, name="sb_fwd")
            h = matmul(s["o"], wi["sb_w_out"], "nn", add=h, name="mm_out")
        if nxt:
            wl.append(layer_weights(i + 1, got))
        s["h1"] = h
        s["hn2"] = hn2 = rms_fwd(h, ln_ffn[i:i + 1], name="rms_fwd")
        s["gu"] = gu = matmul(hn2, wi["ffn_w_in"], "nn", b_chips=0, out_dtype=BF16, name="mm_ffn_in")
        s["act"] = act = swiglu_fwd(gu, name="swiglu_fwd")
        h = matmul(act, wi["ffn_w_out"], "nn", add=h, name="mm_ffn_out")
        s["h2"] = h
        s["hn3"] = hn3 = rms_fwd(h, ln_ple[i:i + 1], name="rms_fwd")
        s["gt"] = gt = matmul(hn3, wi["ple_w_gate"], "nn", name="mm_sq")
        s["pp"] = pp = matmul(p[i, 0], wi["ple_w_proj"], "nn", name="mm_ple_proj")
        h = ple_fwd(h, pp, gt, name="ple_fwd")
        saved.append(s)
        if i == n_a - 1:
            mid = dict(h=h)
            mid["hk"] = hk = rms_fwd(h, kv_norm[None, :], name="rms_fwd")
            mid["kv"] = kv = matmul(hk, wi["w_kv"], "nn", name="mm_kv")
            k_sh = headnorm_fwd(kv, 0, k_norm[None, :], 1.0, name="headnorm_k")
            v_sh = kv[:, WIDTH:].astype(BF16)

    dh, sq = loss_head(h, tgt, name="loss_head")
    loss = lax.psum(0.5 * jnp.sum(sq) / h.shape[1], ("x", "y", "c"))

    gw = {n: [None] * w[n].shape[0] for n in WEIGHTS if w[n].ndim >= 2 and n not in ("w_kv",)}
    dks, dvs = [], []
    for i in reversed(range(depth)):
        s = saved[i]
        if i == n_a - 1:
            dkraw, gw["k_norm"] = headnorm_bwd(mid["kv"], 0, k_norm[None, :], 1.0, tuple(dks), name="headnorm_k_bwd")
            dkv = jnp.concatenate([dkraw, sum_cast(dvs, BF16, name="sum_dv")], axis=1)
            dhk = matmul(dkv, wl[i]["w_kv"], "nt", name="mm_kv_dx")
            gw["w_kv"] = matmul(mid["hk"], dkv, "tn", name="mm_kv_dw")
            dh, gw["kv_norm"] = rms_bwd(mid["h"], kv_norm[None, :], dhk, dh, name="rms_bwd")
        dpp, dgt = ple_bwd(dh, s["pp"], s["gt"], name="ple_bwd")
        gw["ple_w_proj"][i] = matmul(p[i, 0], dpp, "tn", name="mm_ple_proj_dw")
        gw["ple_w_gate"][i] = matmul(s["hn3"], dgt, "tn", name="mm_sq_dw")
        dhn3 = matmul(dgt, wl[i]["ple_w_gate"], "nt", name="mm_sq_dx")
        dh, gw["ln_ple"][i] = rms_bwd(s["h2"], ln_ple[i:i + 1], dhn3, dh, name="rms_bwd")
        dact = matmul(dh, wl[i]["ffn_w_out"], "nt", out_dtype=BF16, name="mm_ffn_out_dx")
        gw["ffn_w_out"][i] = matmul(s["act"], dh, "tn", name="mm_ffn_out_dw")
        dgu = swiglu_bwd(s["gu"], dact, name="swiglu_bwd")
        dhn2 = matmul(dgu, wl[i]["ffn_w_in"], "nt", b_chips=0, name="mm_ffn_in_dx")
        gw["ffn_w_in"][i] = matmul(s["hn2"], dgu, "tn", out_chips=True, name="mm_ffn_in_dw")
        dh, gw["ln_ffn"][i] = rms_bwd(s["h1"], ln_ffn[i:i + 1], dhn2, dh, name="rms_bwd")
        if i < n_a:
            dy = matmul(dh, wl[i]["gdn_w_out"], "nt", name="mm_sq_dx")
            gw["gdn_w_out"][i] = matmul(s["y"], dh, "tn", name="mm_sq_dw")
            do, dproj, gw["gdn_norm"][i] = gatenorm_bwd(s["o"], s["proj"], gdn_norm[i:i + 1], dy, name="gatenorm_bwd")
            dqkv, dg4, db4 = gdn_bwd(s["qkv"], s["g4"], s["b4"], s["s0"], s["tinv"], do, name="gdn_bwd")
            dab, dal, ddt = gates_bwd(s["ab"], s["a_log"], s["dt"], _from4(dg4), _from4(db4), name="gates_bwd")
            gw["gdn_a_log"][i], gw["gdn_dt_bias"][i] = dal[:, 0], ddt[:, 0]
            dproj, gw["gdn_conv"][i] = conv_bwd(s["proj"], conv_full[i], dqkv, dproj, name="conv_bwd")
            dhn = matmul(dproj, s["w_m"], "nt", name="mm_gdn_in_dx")
            dhn = matmul(dab, s["w_abt"], "tn", add=dhn, name="mm_gdn_ab_dx")
            dwm = matmul(s["hn"], dproj, "tn", name="mm_gdn_in_dw")
            dwab = matmul(dab, s["hn"], "nn", name="mm_gdn_ab_dw")
            gw["gdn_w_in"][i] = jnp.concatenate([dwm, dwab.T], axis=1)
        else:
            j = i - n_a
            do = matmul(dh, wl[i]["sb_w_out"], "nt", out_dtype=BF16, name="mm_sb_out_dx")
            gw["sb_w_out"][j] = matmul(s["o"], dh, "tn", name="mm_sq_dw")
            dq, dk, dv = sb_bwd(s["q"], k_sh, v_sh, do, name="sb_bwd")
            dks.append(dk)
            dvs.append(dv)
            dqraw, gw["sb_q_norm"][j] = headnorm_bwd(s["qraw"], 0, sb_q_norm[j:j + 1], HEAD_DIM ** -0.5, (dq,),
                                                    name="headnorm_q_bwd")
            dhn = matmul(dqraw, wl[i]["sb_w_q"], "nt", name="mm_sq_dx")
            gw["sb_w_q"][j] = matmul(s["hn"], dqraw, "tn", name="mm_sq_dw")
        dh, gw["ln_mix"][i] = rms_bwd(s["h0"], ln_mix[i:i + 1], dhn, dh, name="rms_bwd")
    grad_x = dh[None]

    def stacked(n):
        g = gw[n]
        if isinstance(g, list):
            g = jnp.stack([a.reshape(w[n].shape[1:]) if n in SMALL else a for a in g])
        return g

    small_buf = jnp.concatenate([_row(stacked(n)) for n in SMALL] + [_row(stacked("gdn_conv"))], axis=0)
    small_buf = jnp.pad(small_buf, ((0, buf_rows - small_buf.shape[0]), (0, 0)))
    small_sum = all_reduce_small(small_buf, name="all_reduce_small")
    grads = {}
    r0 = 0
    for n in SMALL:
        rows = _rows_of(w[n].shape, 1)
        grads[n] = small_sum[r0:r0 + rows].reshape(-1)[:math.prod(w[n].shape)].reshape(w[n].shape)
        r0 += rows
    conv_g = small_sum[r0:r0 + N_CHIPS * conv_rows].reshape(-1)[:N_CHIPS * math.prod(gdn_conv.shape)]
    conv_g = conv_g.reshape((gdn_conv.shape[0], CONV_WIDTH, N_CHIPS, gdn_conv.shape[2]))
    grads["gdn_conv"] = lax.dynamic_index_in_dim(conv_g, chip, axis=2, keepdims=False)

    g_packed = _pack_rows([_shard(stacked(n), ax) for n, ax in BIG], (N_CHIPS,))
    g_ffn_in = jnp.concatenate(gw["ffn_w_in"], axis=1)
    g_gdn_in = _shard(stacked("gdn_w_in"), 2).reshape(N_CHIPS, -1, gdn_w_in.shape[2])
    g5s = [g.reshape(N_CHIPS, 2, g.shape[1] // 2, g.shape[2]) for g in (g_packed, g_ffn_in, g_gdn_in)]
    from_sibling = sibling_swap(g5s, name="sibling_swap")
    s1s = [add_selected(sel_c, g5, fs, name="add_selected") for g5, fs in zip(g5s, from_sibling)]
    from_chips = chip_exchange([s1b for _, s1b in s1s], name="chip_exchange")
    s2s = [add_chip_sums(sel_chip, s1, fc, name="add_chip_sums") for (s1, _), fc in zip(s1s, from_chips)]
    reduced, reduced_ffn_in, reduced_gdn_in = sibling_merge(s2s, name="sibling_merge")
    grads["gdn_w_in"] = reduced_gdn_in.reshape(gdn_w_in.shape)
    for (n, _), g in zip(BIG, _unpack_rows(reduced.reshape(-1, PACK_COLS), shard_shapes, ())):
        grads[n] = g
    grads["ffn_w_in"] = reduced_ffn_in.reshape(ffn_w_in.shape)

    delta, new_m, new_v = {}, {}, {}
    for n in WEIGHTS:
        delta[n], new_m[n], new_v[n] = adamw(w[n], grads[n], mom1[n], mom2[n], name="adamw")
    return (loss, grad_x, *[grads[n] for n in WEIGHTS], *[delta[n] for n in WEIGHTS],
            *[new_m[n] for n in WEIGHTS], *[new_v[n] for n in WEIGHTS])
```

```python
import functools
import math

import jax
import jax.numpy as jnp
from jax import lax
from jax.experimental import pallas as pl
from jax.experimental.pallas import tpu as pltpu

F32 = jnp.float32
BF16 = jnp.bfloat16
EPS = 1e-6
HEADS = 8
HEAD_DIM = 128
WIDTH = HEADS * HEAD_DIM
CHUNK = 64
CONV_WIDTH = 4
N_CHIPS = 4
ADAM_LR, ADAM_B1, ADAM_B2, ADAM_EPS, ADAM_WD, ADAM_STEP = 0.001, 0.9, 0.999, 1e-08, 0.01, 10
V7X_VMEM_BYTES = 64 * 1024 * 1024
VMEM_LIMIT = V7X_VMEM_BYTES - 8 * 1024 * 1024
HIGHEST = lax.Precision.HIGHEST
MESH = pl.DeviceIdType.MESH


def _params(sem=None):
    return pltpu.CompilerParams(dimension_semantics=sem, vmem_limit_bytes=VMEM_LIMIT)


def _pick(n, prefs):
    for t in prefs:
        if t <= n and n % t == 0:
            return t
    return n


def _bdot(a, b, dims):
    return lax.dot_general(a.astype(BF16), b.astype(BF16), (((dims[0],), (dims[1],)), ((), ())),
                           preferred_element_type=F32)


NN, NT, TN = (1, 0), (1, 1), (0, 0)


MM_TILES = (1024, 1408, 512, 256, 128)


def matmul(a, b, form, *, out_dtype=F32, add=None, name, b_chips=None, out_chips=False):
    ns = None
    if b_chips is not None:
        ns = b.shape[3]
        b_shape = (b.shape[2], N_CHIPS * ns)
    else:
        b_shape = b.shape
    if form == "nn":
        (m, k), (k2, n) = a.shape, b_shape
    elif form == "nt":
        (m, k), (n, k2) = a.shape, b_shape
    else:
        (k, m), (k2, n) = a.shape, b_shape
    assert k == k2, (a.shape, b.shape, form)
    if out_chips:
        ns = n // N_CHIPS
    tm = _pick(m, MM_TILES)
    tn = _pick(n, MM_TILES)
    tk = k if k <= 1024 else _pick(k, MM_TILES)
    if ns is not None and (form == "nn" or out_chips):
        tn = ns
    if ns is not None and form == "nt":
        tk = ns
    nk = k // tk
    if form == "tn":
        a_spec = pl.BlockSpec((tk, tm), lambda i, j, kk: (kk, i))
    else:
        a_spec = pl.BlockSpec((tm, tk), lambda i, j, kk: (i, kk))
    if b_chips is not None and form == "nn":
        b_spec = pl.BlockSpec((None, None, tk, ns), lambda i, j, kk: (j, b_chips, kk, 0))
    elif b_chips is not None:
        b_spec = pl.BlockSpec((None, None, tn, ns), lambda i, j, kk: (kk, b_chips, j, 0))
    elif form == "nt":
        b_spec = pl.BlockSpec((tn, tk), lambda i, j, kk: (j, kk))
    else:
        b_spec = pl.BlockSpec((tk, tn), lambda i, j, kk: (kk, j))
    if out_chips:
        o_spec = pl.BlockSpec((None, tm, ns), lambda i, j, kk: (j, i, 0))
    else:
        o_spec = pl.BlockSpec((tm, tn), lambda i, j, kk: (i, j))
    dims = {"nn": NN, "nt": NT, "tn": TN}[form]
    has_add = add is not None

    def body(*refs):
        a_ref, b_ref = refs[:2]
        add_ref = refs[2] if has_add else None
        o_ref = refs[2 + has_add]

        def finish(r):
            if has_add:
                r = r + add_ref[...].astype(F32)
            o_ref[...] = r.astype(out_dtype)

        part = _bdot(a_ref[...], b_ref[...], dims)
        if nk == 1:
            finish(part)
            return
        acc_ref = refs[3 + has_add]
        kk = pl.program_id(2)

        @pl.when(kk == 0)
        def _():
            acc_ref[...] = part

        @pl.when(kk > 0)
        def _():
            acc_ref[...] += part

        @pl.when(kk == nk - 1)
        def _():
            finish(acc_ref[...])

    in_specs = [a_spec, b_spec] + ([o_spec] if has_add else [])
    args = (a, b) + ((add,) if has_add else ())
    return pl.pallas_call(
        body, name=name, grid=(m // tm, n // tn, nk), in_specs=in_specs, out_specs=o_spec,
        out_shape=jax.ShapeDtypeStruct((N_CHIPS, m, ns) if out_chips else (m, n), out_dtype),
        scratch_shapes=[pltpu.VMEM((tm, tn), F32)] if nk > 1 else [],
        compiler_params=_params(("parallel", "parallel", "arbitrary")),
    )(*args)


def _const(c):
    return lambda j: c


def rowwise(fn, rows, params, outs, accs=(), *, name, tm, ncol=1):
    t = rows[0][0].shape[0]
    tm = min(tm, t)
    assert t % tm == 0
    n_rows, n_par, n_out, n_acc = len(rows), len(params), len(outs), len(accs)

    def body(*refs):
        j, i = pl.program_id(0), pl.program_id(1)
        ins = [r[...] for r in refs[:n_rows + n_par]]
        o_refs = refs[n_rows + n_par:n_rows + n_par + n_out]
        a_refs = refs[n_rows + n_par + n_out:]
        row_outs, acc_outs = fn(*ins)
        for r, val in zip(o_refs, row_outs):
            r[...] = val.astype(r.dtype)
        for r, val, spec in zip(a_refs, acc_outs, accs):
            first = (i == 0) & (j == 0) if spec[4] else (i == 0)

            @pl.when(first)
            def _(r=r, val=val):
                r[...] = val.astype(F32)

            @pl.when(jnp.logical_not(first))
            def _(r=r, val=val):
                r[...] += val.astype(F32)

    in_specs = [pl.BlockSpec((tm, w), lambda j, i, cf=cf: (i, cf(j))) for _, w, cf in rows]
    in_specs += [pl.BlockSpec((p.shape[0], w), lambda j, i, cf=cf: (0, cf(j))) for p, w, cf in params]
    out_specs = [pl.BlockSpec((tm, w), lambda j, i, cf=cf: (i, cf(j))) for _, _, w, cf in outs]
    out_specs += [pl.BlockSpec((r, w), lambda j, i, cf=cf: (0, cf(j))) for r, _, w, cf, _ in accs]
    out_shape = [jax.ShapeDtypeStruct((t, tw), dt) for tw, dt, _, _ in outs]
    out_shape += [jax.ShapeDtypeStruct((r, tw), F32) for r, tw, _, _, _ in accs]
    res = pl.pallas_call(
        body, name=name, grid=(ncol, t // tm), in_specs=in_specs, out_specs=out_specs, out_shape=out_shape,
        compiler_params=_params(("arbitrary", "arbitrary")),
    )(*[r[0] for r in rows], *[p[0] for p in params])
    return res[:n_out], res[n_out:]


def _full(arr):
    return (arr, arr.shape[1], _const(0))


def _rms(x, g):
    x = x.astype(F32)
    return x * lax.rsqrt(jnp.mean(x * x, axis=-1, keepdims=True) + EPS) * g.astype(F32)


def _sigmoid(x):
    return 1.0 / (1.0 + jnp.exp(-x))


def _silu(x):
    return x * _sigmoid(x)


def _softplus(x):
    return jnp.maximum(x, 0.0) + jnp.log(1.0 + jnp.exp(-jnp.abs(x)))


def rms_fwd(h, g, *, name):
    d = h.shape[1]
    (hn,), _ = rowwise(lambda x, gg: ((_rms(x, gg),), ()), [_full(h)], [_full(g)],
                       [(d, BF16, d, _const(0))], name=name, tm=512)
    return hn


def rms_bwd(h, g, dhn, dh_res, *, name):
    d = h.shape[1]

    def fn(x, ct, res, gg):
        _, vjp = jax.vjp(_rms, x.astype(F32), gg.astype(F32))
        dx, dg = vjp(ct.astype(F32))
        return (res.astype(F32) + dx,), (dg,)

    (dh,), (dg,) = rowwise(fn, [_full(h), _full(dhn), _full(dh_res)], [_full(g)],
                           [(d, F32, d, _const(0))], [(1, d, d, _const(0), True)], name=name, tm=256)
    return dh, dg


def _head_rms(x, g, scale):
    x = x.astype(F32)
    return x * lax.rsqrt(jnp.mean(x * x, axis=-1, keepdims=True) + EPS) * (g.astype(F32) * scale)


def headnorm_fwd(x, col0, g, scale, *, name):
    (y,), _ = rowwise(lambda a, gg: ((_head_rms(a, gg, scale),), ()),
                      [(x, HEAD_DIM, lambda j: col0 + j)], [_full(g)],
                      [(WIDTH, BF16, HEAD_DIM, lambda j: j)], name=name, tm=1024, ncol=HEADS)
    return y


def headnorm_bwd(x, col0, g, scale, dys, *, name, out_dtype=BF16):
    def fn(a, *rest):
        cts, gg = rest[:-1], rest[-1]
        ct = sum(c.astype(F32) for c in cts)
        _, vjp = jax.vjp(lambda a_, g_: _head_rms(a_, g_, scale), a.astype(F32), gg.astype(F32))
        dx, dg = vjp(ct)
        return (dx,), (dg,)

    (dx,), (dg,) = rowwise(fn, [(x, HEAD_DIM, lambda j: col0 + j)] + [(dy, HEAD_DIM, lambda j: j) for dy in dys], [_full(g)],
                           [(WIDTH, out_dtype, HEAD_DIM, lambda j: j)],
                           [(1, HEAD_DIM, HEAD_DIM, _const(0), True)], name=name, tm=1024, ncol=HEADS)
    return dx, dg


def sum_cast(parts, dtype, *, name):
    wd = parts[0].shape[1]
    (out,), _ = rowwise(lambda *a: ((sum(b.astype(F32) for b in a),), ()), [_full(a) for a in parts], [],
                        [(wd, dtype, wd, _const(0))], name=name, tm=512)
    return out


def _gatenorm(o, gate, g):
    return _head_rms(o, g, 1.0) * _silu(gate.astype(F32))


def gatenorm_fwd(o, proj, g, *, name):
    (y,), _ = rowwise(lambda a, gt, gg: ((_gatenorm(a, gt, gg),), ()),
                      [(o, HEAD_DIM, lambda j: j), (proj, HEAD_DIM, lambda j: 3 * HEADS + j)], [_full(g)],
                      [(WIDTH, BF16, HEAD_DIM, lambda j: j)], name=name, tm=1024, ncol=HEADS)
    return y


def gatenorm_bwd(o, proj, g, dy, *, name):
    def fn(a, gt, ct, gg):
        _, vjp = jax.vjp(_gatenorm, a.astype(F32), gt.astype(F32), gg.astype(F32))
        da, dgt, dg = vjp(ct.astype(F32))
        return (da, dgt), (dg,)

    (do, dproj), (dg,) = rowwise(
        fn, [(o, HEAD_DIM, lambda j: j), (proj, HEAD_DIM, lambda j: 3 * HEADS + j), (dy, HEAD_DIM, lambda j: j)],
        [_full(g)],
        [(WIDTH, F32, HEAD_DIM, lambda j: j), (4 * WIDTH, BF16, HEAD_DIM, lambda j: 3 * HEADS + j)],
        [(1, HEAD_DIM, HEAD_DIM, _const(0), True)], name=name, tm=1024, ncol=HEADS)
    return do, dproj, dg


def _swiglu(g, u):
    return _silu(g.astype(F32)) * u.astype(F32)


def swiglu_fwd(gu, *, name):
    f = gu.shape[1] // 2
    (act,), _ = rowwise(lambda g, u: ((_swiglu(g, u),), ()), [(gu, f, _const(0)), (gu, f, _const(1))], [],
                        [(f, BF16, f, _const(0))], name=name, tm=256)
    return act


def swiglu_bwd(gu, dact, *, name):
    f = gu.shape[1] // 2

    def fn(g, u, ct):
        _, vjp = jax.vjp(_swiglu, g.astype(F32), u.astype(F32))
        dg, du = vjp(ct.astype(F32))
        return (jnp.concatenate([dg.astype(BF16), du.astype(BF16)], axis=1),), ()

    (dgu,), _ = rowwise(fn, [(gu, f, _const(0)), (gu, f, _const(1)), _full(dact)], [],
                        [(2 * f, BF16, 2 * f, _const(0))], name=name, tm=256)
    return dgu


def ple_fwd(h, pp, gt, *, name):
    d = h.shape[1]
    (out,), _ = rowwise(lambda a, b, c: ((a + b * _sigmoid(c),), ()), [_full(h), _full(pp), _full(gt)], [],
                        [(d, F32, d, _const(0))], name=name, tm=512)
    return out


def ple_bwd(dh, pp, gt, *, name):
    d = dh.shape[1]

    def fn(ct, b, c):
        s = _sigmoid(c)
        return (ct * s, ct * b * s * (1.0 - s)), ()

    (dpp, dgt), _ = rowwise(fn, [_full(dh), _full(pp), _full(gt)], [],
                            [(d, BF16, d, _const(0)), (d, BF16, d, _const(0))], name=name, tm=512)
    return dpp, dgt


def loss_head(y, tgt, *, name):
    d = y.shape[1]

    def fn(a, b):
        e = a - b
        return (e * (1.0 / d),), (jnp.sum(e * e, axis=0, keepdims=True),)

    (dy,), (sq,) = rowwise(fn, [_full(y), _full(tgt)], [], [(d, F32, d, _const(0))],
                           [(1, d, d, _const(0), True)], name=name, tm=512)
    return dy, sq


def adamw(w, g, m, v, *, name):
    shape = w.shape
    cols = shape[-1]
    flat = lambda a: a.reshape(-1, cols)
    bc1 = 1.0 - ADAM_B1 ** ADAM_STEP
    bc2 = 1.0 - ADAM_B2 ** ADAM_STEP

    def fn(w_, g_, m_, v_):
        m_ = ADAM_B1 * m_ + (1.0 - ADAM_B1) * g_
        v_ = ADAM_B2 * v_ + (1.0 - ADAM_B2) * (g_ * g_)
        delta = -ADAM_LR * ((m_ / bc1) / (jnp.sqrt(v_ / bc2) + ADAM_EPS) + ADAM_WD * w_)
        return (delta, m_, v_), ()

    o = (cols, F32, cols, _const(0))
    (d_, m_, v_), _ = rowwise(fn, [_full(flat(w)), _full(flat(g)), _full(flat(m)), _full(flat(v))], [],
                              [o, o, o], name=name, tm=256)
    return d_.reshape(shape), m_.reshape(shape), v_.reshape(shape)


CONV_STRIP = 256


def _shift_down(x, d):
    if d == 0:
        return x
    rows = lax.broadcasted_iota(jnp.int32, x.shape, 0)
    return jnp.where(rows >= d, pltpu.roll(x, d, 0), 0.0)


def _shift_up(x, d):
    if d == 0:
        return x
    t = x.shape[0]
    rows = lax.broadcasted_iota(jnp.int32, x.shape, 0)
    return jnp.where(rows < t - d, pltpu.roll(x, t - d, 0), 0.0)


def _conv(x, w):
    acc = None
    for j in range(CONV_WIDTH):
        term = _shift_down(x, CONV_WIDTH - 1 - j) * w[j:j + 1, :]
        acc = term if acc is None else acc + term
    return acc


def conv_fwd(proj, w, *, name):
    t = proj.shape[0]
    per = WIDTH // CONV_STRIP

    def body(x_ref, w_ref, o_ref):
        o_ref[0] = _silu(_conv(x_ref[...].astype(F32), w_ref[...]))

    return pl.pallas_call(
        body, name=name, grid=(3 * per,),
        in_specs=[pl.BlockSpec((t, CONV_STRIP), lambda j: (0, j)), pl.BlockSpec((CONV_WIDTH, CONV_STRIP), lambda j: (0, j))],
        out_specs=pl.BlockSpec((1, t, CONV_STRIP), lambda j: (j // per, 0, j % per)),
        out_shape=jax.ShapeDtypeStruct((3, t, WIDTH), F32),
        compiler_params=_params(("parallel",)),
    )(proj, w)


def conv_bwd(proj, w, dqkv, dproj, *, name):
    t = proj.shape[0]
    per = WIDTH // CONV_STRIP

    def body(x_ref, w_ref, d_ref, _, dx_ref, dw_ref):
        x, w_ = x_ref[...].astype(F32), w_ref[...]
        c = _conv(x, w_)
        s = _sigmoid(c)
        dc = d_ref[0] * (s + c * s * (1.0 - s))
        dx = None
        for j in range(CONV_WIDTH):
            d = CONV_WIDTH - 1 - j
            term = _shift_up(dc, d) * w_[j:j + 1, :]
            dx = term if dx is None else dx + term
            dw_ref[j:j + 1, :] = jnp.sum(dc * _shift_down(x, d), axis=0, keepdims=True)
        dx_ref[...] = dx.astype(dx_ref.dtype)

    return pl.pallas_call(
        body, name=name, grid=(3 * per,),
        in_specs=[pl.BlockSpec((t, CONV_STRIP), lambda j: (0, j)), pl.BlockSpec((CONV_WIDTH, CONV_STRIP), lambda j: (0, j)),
                  pl.BlockSpec((1, t, CONV_STRIP), lambda j: (j // per, 0, j % per)), pl.BlockSpec(memory_space=pl.ANY)],
        out_specs=[pl.BlockSpec((t, CONV_STRIP), lambda j: (0, j)), pl.BlockSpec((CONV_WIDTH, CONV_STRIP), lambda j: (0, j))],
        out_shape=[jax.ShapeDtypeStruct(dproj.shape, dproj.dtype), jax.ShapeDtypeStruct((CONV_WIDTH, 3 * WIDTH), F32)],
        input_output_aliases={3: 0},
        compiler_params=_params(("parallel",)),
    )(proj, w, dqkv, dproj)


def _gdn_gates(ab, a_log, dt_bias):
    a_in, b_in = ab[:HEADS], ab[HEADS:]
    g = -jnp.exp(a_log) * _softplus(a_in + dt_bias)
    return g, _sigmoid(b_in)


def gates_fwd(ab, a_log, dt_bias, *, name):
    t = ab.shape[1]

    def body(ab_ref, al_ref, dt_ref, g_ref, b_ref):
        g_ref[...], b_ref[...] = _gdn_gates(ab_ref[...], al_ref[...], dt_ref[...])

    s = jax.ShapeDtypeStruct((HEADS, t), F32)
    return pl.pallas_call(body, name=name, out_shape=[s, s], compiler_params=_params())(ab, a_log, dt_bias)


def gates_bwd(ab, a_log, dt_bias, dg, dbeta, *, name):
    t = ab.shape[1]

    def body(ab_ref, al_ref, dt_ref, dg_ref, db_ref, dab_ref, dal_ref, ddt_ref):
        _, vjp = jax.vjp(_gdn_gates, ab_ref[...], al_ref[...], dt_ref[...])
        dab_ref[...], dal_ref[...], ddt_ref[...] = vjp((dg_ref[...], db_ref[...]))

    c = jax.ShapeDtypeStruct((HEADS, 1), F32)
    return pl.pallas_call(body, name=name, out_shape=[jax.ShapeDtypeStruct((2 * HEADS, t), F32), c, c],
                          compiler_params=_params())(ab, a_log, dt_bias, dg, dbeta)


def _split3(x):
    hi = x.astype(BF16)
    r1 = x - hi.astype(F32)
    mid = r1.astype(BF16)
    lo = (r1 - mid.astype(F32)).astype(BF16)
    return hi, mid, lo


def _dot01(x, m01):
    hi, mid, lo = _split3(x)
    m01 = m01.astype(BF16)
    return _bdot(hi, m01, NN) + _bdot(mid, m01, NN) + _bdot(lo, m01, NN)


def _hdot(a, b, dims=NN):
    a_hi, b_hi = a.astype(BF16), b.astype(BF16)
    a_lo, b_lo = (a - a_hi.astype(F32)).astype(BF16), (b - b_hi.astype(F32)).astype(BF16)
    return _bdot(a_hi, b_hi, dims) + (_bdot(a_hi, b_lo, dims) + _bdot(a_lo, b_hi, dims))


def _rowsum(x):
    return jnp.sum(x, axis=1, keepdims=True)


def _colsum(x):
    return jnp.sum(x, axis=0, keepdims=True)


class _Heads:
    def __init__(self, vals):
        self.v = list(vals)

    def _bin(self, other, f):
        if isinstance(other, _Heads):
            return _Heads(f(a, b) for a, b in zip(self.v, other.v))
        return _Heads(f(a, other) for a in self.v)

    def __add__(self, o):
        return self._bin(o, lambda a, b: a + b)

    __radd__ = __add__

    def __sub__(self, o):
        return self._bin(o, lambda a, b: a - b)

    def __rsub__(self, o):
        return self._bin(o, lambda a, b: b - a)

    def __mul__(self, o):
        return self._bin(o, lambda a, b: a * b)

    __rmul__ = __mul__

    def __neg__(self):
        return _Heads(-a for a in self.v)


def _hmap(f, *args):
    n = next(len(a.v) for a in args if isinstance(a, _Heads))
    return _Heads(f(*[a.v[h] if isinstance(a, _Heads) else a for a in args]) for h in range(n))


def _inv_unit_lower(a, eye):
    p = jnp.where(eye, 1.0, 0.0) - a
    ak = a
    for _ in range(int(math.log2(CHUNK)) - 1):
        ak = _hmap(_hdot, ak, ak)
        p = p + _hmap(_hdot, p, ak)
    return p


def _gdn_chunk(qr, kr, v, grow, brow, tinv=None):
    c = CHUNK
    ri = lax.broadcasted_iota(jnp.int32, (c, c), 0)
    ci = lax.broadcasted_iota(jnp.int32, (c, c), 1)
    eye, lower, strict = ri == ci, ri >= ci, ri > ci
    where = lambda m: (lambda a: jnp.where(m, a, 0.0))
    to_col = lambda row: _hmap(lambda r: _rowsum(jnp.where(eye, jnp.broadcast_to(r, (c, c)), 0.0)), row)
    cum_row = _hmap(lambda g: _dot01(jnp.broadcast_to(g, (8, c)), ri <= ci)[0:1], grow)
    gcol, bcol = to_col(cum_row), to_col(brow)
    glast = _hmap(lambda g: _colsum(jnp.where(ri[:, 0:1] == c - 1, g, 0.0)), gcol)
    rq = _hmap(lambda a: lax.rsqrt(_rowsum(a * a) + EPS), qr)
    rk = _hmap(lambda a: lax.rsqrt(_rowsum(a * a) + EPS), kr)
    scale = HEAD_DIM ** -0.5
    qn, kn = qr * (rq * scale), kr * rk
    dec = _hmap(lambda gc, gr: jnp.where(lower, jnp.exp(jnp.minimum(gc - gr, 0.0)), 0.0), gcol, cum_row)
    kk = _hmap(lambda a: _bdot(a, a, NT), kn)
    qk = _hmap(lambda a, b: _bdot(a, b, NT), qn, kn)
    gam_col, e_col, gam_last = _hmap(jnp.exp, gcol), _hmap(jnp.exp, glast - gcol), _hmap(jnp.exp, glast)
    if tinv is None:
        tinv = _inv_unit_lower(_hmap(where(strict), bcol * kk * dec), eye)
    u = _hmap(_hdot, tinv, v * bcol)
    w = _hmap(_hdot, tinv, kn * (bcol * gam_col))
    return dict(eye=eye, lower=lower, strict=strict, gcol=gcol, bcol=bcol, rq=rq, rk=rk, qn=qn, kn=kn,
                dec=dec, kk=kk, qk=qk, gam_col=gam_col, e_col=e_col, gam_last=gam_last, tinv=tinv, u=u, w=w,
                aqk=qk * dec, qt=qn * gam_col, kt=kn * e_col, scale=scale)


def _head_cols(h):
    return slice(h * HEAD_DIM, (h + 1) * HEAD_DIM)


def _bd(dims):
    return lambda a, b: _bdot(a, b, dims)


def gdn_fwd(qkv, g4, b4, *, name, gather=()):
    t = qkv.shape[1]
    n = t // CHUNK
    d = HEAD_DIM
    heads = range(HEADS)
    ng = len(gather)

    def body(*refs):
        qkv_ref, g_ref, b_ref = refs[:3]
        o_ref, s0_ref, t_ref = refs[3 + ng:6 + ng]
        s_ref = refs[6 + 2 * ng]
        comm = _Gather(refs[6 + ng:6 + 2 * ng], *refs[7 + 2 * ng:]) if ng else None

        @pl.when(pl.program_id(0) == 0)
        def _():
            s_ref[...] = jnp.zeros_like(s_ref)
            if ng:
                comm.start()

        qr, kr, v = (_Heads(qkv_ref[j, :, _head_cols(h)] for h in heads) for j in range(3))
        z = _gdn_chunk(qr, kr, v, _Heads(g_ref[0, h] for h in heads), _Heads(b_ref[0, h] for h in heads))
        s0 = _Heads(s_ref[h] for h in heads)
        v_new = z["u"] - _hmap(_bd(NN), z["w"], s0)
        o = _hmap(_bd(NN), z["qt"], s0) + _hmap(_bd(NN), z["aqk"], v_new)
        s_new = s0 * z["gam_last"] + _hmap(_bd(TN), z["kt"], v_new)
        for h in heads:
            s0_ref[0, h] = s0.v[h]
            t_ref[0, h] = z["tinv"].v[h]
            o_ref[:, _head_cols(h)] = o.v[h]
            s_ref[h] = s_new.v[h]

        if ng:
            @pl.when(pl.program_id(0) == n - 1)
            def _():
                comm.finish()

    gspec = pl.BlockSpec((1, HEADS, 1, CHUNK), lambda i: (i, 0, 0, 0))
    return pl.pallas_call(
        body, name=name, grid=(n,),
        in_specs=[pl.BlockSpec((3, CHUNK, WIDTH), lambda i: (0, i, 0)), gspec, gspec] + [ANY] * ng,
        out_specs=[pl.BlockSpec((CHUNK, WIDTH), lambda i: (i, 0)),
                   pl.BlockSpec((1, HEADS, d, d), lambda i: (i, 0, 0, 0)),
                   pl.BlockSpec((1, HEADS, CHUNK, CHUNK), lambda i: (i, 0, 0, 0))] + [ANY] * ng,
        out_shape=[jax.ShapeDtypeStruct((t, WIDTH), F32), jax.ShapeDtypeStruct((n, HEADS, d, d), F32),
                   jax.ShapeDtypeStruct((n, HEADS, CHUNK, CHUNK), F32)]
        + [jax.ShapeDtypeStruct(s.shape, s.dtype) for s in gather],
        input_output_aliases={3 + b: 3 + b for b in range(ng)},
        scratch_shapes=[pltpu.VMEM((HEADS, d, d), F32)] + (_dma_sems(6 * ng) if ng else []),
        compiler_params=_params(("arbitrary",)),
    )(qkv, g4, b4, *gather)


def gdn_bwd(qkv, g4, b4, s0_all, tinv_all, do, *, name, exchange=()):
    t = qkv.shape[1]
    n = t // CHUNK
    d = HEAD_DIM
    c = CHUNK
    heads = range(HEADS)

    ne = len(exchange)

    def body(*refs):
        qkv_ref, g_ref, b_ref, s0_ref, t_ref, do_ref = refs[:6]
        dqkv_ref, dg_ref, db_ref = refs[6 + ne:9 + ne]
        ds_ref = refs[9 + 2 * ne]
        comm = _Exchange(refs[6:6 + ne], refs[9 + ne:9 + 2 * ne], *refs[10 + 2 * ne:]) if ne else None

        @pl.when(pl.program_id(0) == 0)
        def _():
            ds_ref[...] = jnp.zeros_like(ds_ref)
            if ne:
                comm.start()

        qr, kr, v = (_Heads(qkv_ref[j, :, _head_cols(h)] for h in heads) for j in range(3))
        z = _gdn_chunk(qr, kr, v, _Heads(g_ref[0, h] for h in heads), _Heads(b_ref[0, h] for h in heads),
                       tinv=_Heads(t_ref[0, h] for h in heads))
        s0 = _Heads(s0_ref[0, h] for h in heads)
        ds = _Heads(ds_ref[h] for h in heads)
        dout = _Heads(do_ref[:, _head_cols(h)] for h in heads)
        qn, kn, u, w, dec, kk, qk = z["qn"], z["kn"], z["u"], z["w"], z["dec"], z["kk"], z["qk"]
        bcol, gam_col, e_col, gam_last = z["bcol"], z["gam_col"], z["e_col"], z["gam_last"]
        low = lambda a: jnp.where(z["lower"], a, 0.0)
        strict = lambda a: jnp.where(z["strict"], a, 0.0)
        rowsum = lambda a: _hmap(_rowsum, a)
        colsum = lambda a: _hmap(_colsum, a)
        v_new = u - _hmap(_bd(NN), w, s0)
        dv_new = _hmap(_bd(TN), z["aqk"], dout) + _hmap(_bd(NN), z["kt"], ds)
        daqk = _hmap(low, _hmap(_bd(NT), dout, v_new))
        dqt = _hmap(_bd(NT), dout, s0)
        dkt = _hmap(_bd(NT), v_new, ds)
        dgam_last = _hmap(lambda a, b: jnp.sum(a * b, keepdims=True), ds, s0)
        ds_new = _hmap(_bd(TN), z["qt"], dout) + ds * gam_last - _hmap(_bd(TN), w, dv_new)
        dw = -_hmap(_bd(NT), dv_new, s0)
        hd_t = lambda a, b: _hdot(a, b, TN)
        dru = _hmap(hd_t, z["tinv"], dv_new)
        drw = _hmap(hd_t, z["tinv"], dw)
        dal = -_hmap(strict, _hmap(_bd(NT), dru, u) + _hmap(_bd(NT), drw, w))
        t1 = dal * kk * dec
        dkk = dal * bcol * dec
        ddec = dal * bcol * kk + daqk * qk
        dqk = daqk * dec
        s_w = rowsum(drw * kn)
        dbeta_col = rowsum(t1) + rowsum(dru * v) + gam_col * s_w
        dkn = (drw * (bcol * gam_col) + _hmap(_bd(NN), dkk, kn) + _hmap(_bd(TN), dkk, kn) + _hmap(_bd(TN), dqk, qn)
               + dkt * e_col)
        dqn = _hmap(_bd(NN), dqk, kn) + dqt * gam_col
        e_mat = ddec * dec
        de_col = rowsum(dkt * kn)
        diag_of_colsum = rowsum(_hmap(lambda a: jnp.where(z["eye"], jnp.broadcast_to(_colsum(a), (c, c)), 0.0), e_mat))
        dg_cum = rowsum(e_mat) + (bcol * s_w + rowsum(dqt * qn)) * gam_col - de_col * e_col - diag_of_colsum
        dg_last = colsum(de_col * e_col) + dgam_last * gam_last
        dg = colsum(_hmap(lambda a: jnp.where(z["lower"], a, 0.0), dg_cum)) + dg_last
        dbeta = colsum(_hmap(lambda a: jnp.where(z["eye"], a, 0.0), dbeta_col))
        rq, rk = z["rq"], z["rk"]
        dqr = z["scale"] * (rq * dqn - qr * (rq * rq * rq) * rowsum(qr * dqn))
        dkr = rk * dkn - kr * (rk * rk * rk) * rowsum(kr * dkn)
        dv = dru * bcol
        for h in heads:
            ds_ref[h] = ds_new.v[h]
            dg_ref[0, h] = dg.v[h]
            db_ref[0, h] = dbeta.v[h]
            dqkv_ref[0, :, _head_cols(h)] = dqr.v[h]
            dqkv_ref[1, :, _head_cols(h)] = dkr.v[h]
            dqkv_ref[2, :, _head_cols(h)] = dv.v[h]

        if ne:
            @pl.when(pl.program_id(0) == n - 1)
            def _():
                comm.finish()

    rev = lambda i: n - 1 - i
    gspec = pl.BlockSpec((1, HEADS, 1, CHUNK), lambda i: (rev(i), 0, 0, 0))
    return pl.pallas_call(
        body, name=name, grid=(n,),
        in_specs=[pl.BlockSpec((3, CHUNK, WIDTH), lambda i: (0, rev(i), 0)), gspec, gspec,
                  pl.BlockSpec((1, HEADS, d, d), lambda i: (rev(i), 0, 0, 0)),
                  pl.BlockSpec((1, HEADS, CHUNK, CHUNK), lambda i: (rev(i), 0, 0, 0)),
                  pl.BlockSpec((CHUNK, WIDTH), lambda i: (rev(i), 0))] + [ANY] * ne,
        out_specs=[pl.BlockSpec((3, CHUNK, WIDTH), lambda i: (0, rev(i), 0)), gspec, gspec] + [ANY] * ne,
        out_shape=[jax.ShapeDtypeStruct((3, t, WIDTH), F32), jax.ShapeDtypeStruct((n, HEADS, 1, CHUNK), F32),
                   jax.ShapeDtypeStruct((n, HEADS, 1, CHUNK), F32)] + _exchange_shapes(exchange),
        scratch_shapes=[pltpu.VMEM((HEADS, d, d), F32)] + (_dma_sems(3 * ne) if ne else []),
        compiler_params=_params(("arbitrary",)),
    )(qkv, g4, b4, s0_all, tinv_all, do, *exchange)


SB_BLOCK = 256


def _dot01_2(x, m01):
    hi = x.astype(BF16)
    lo = (x - hi.astype(F32)).astype(BF16)
    return _bdot(hi, m01, NN) + _bdot(lo, m01, NN)


SB_HEADS = 2


def _sb_weights(q, kb, carry, mask, upper):
    z = _hmap(_bd(NT), q, kb)
    ls = _hmap(lambda z_: jnp.minimum(z_, 0.0) - jnp.log(1.0 + jnp.exp(-jnp.abs(z_))), z)
    ln = _hmap(lambda l_, z_: jnp.where(mask, l_ - z_, 0.0), ls, z)
    suffix = _hmap(lambda l_: _dot01_2(l_, upper), ln)
    a = _hmap(lambda l_, s_, c_: jnp.where(mask, jnp.exp(l_ + s_ + c_), 0.0), ls, suffix, carry)
    return z, ln, a


SB_DEAD = -105.0


def _sb_alive(s, i, carries):
    top = jnp.max(carries[0])
    for c in carries[1:]:
        top = jnp.maximum(top, jnp.max(c))
    return (s <= i) & (top > SB_DEAD)


def _sb_masks(i, jb, blk):
    ri = lax.broadcasted_iota(jnp.int32, (blk, blk), 0)
    ci = lax.broadcasted_iota(jnp.int32, (blk, blk), 1)
    return (jb * blk + ci) < (i * blk + ri)


def sb_fwd(q, k, v, *, name, gather=()):
    t = q.shape[0]
    blk = min(SB_BLOCK, t)
    d = HEAD_DIM
    hs = range(SB_HEADS)
    ng = len(gather)
    groups, nb = HEADS // SB_HEADS, t // blk

    def body(*refs):
        q_ref, k_ref, v_ref = refs[:3]
        o_ref = refs[3 + ng]
        comm = _Gather(refs[4 + ng:4 + 2 * ng], *refs[4 + 2 * ng:]) if ng else None
        i = pl.program_id(1)
        if ng:
            @pl.when((pl.program_id(0) == 0) & (i == 0))
            def _():
                comm.start()

        qb = _Heads(q_ref[:, _head_cols(h)] for h in hs)
        ri = lax.broadcasted_iota(jnp.int32, (blk, blk), 0)
        ci = lax.broadcasted_iota(jnp.int32, (blk, blk), 1)
        upper = (ri > ci).astype(BF16)

        def step(state):
            s, cs, accs = state
            jb = i - s
            rows = pl.ds(pl.multiple_of(jb * blk, blk), blk)
            kb = _Heads(k_ref[rows, _head_cols(h)] for h in hs)
            vb = _Heads(v_ref[rows, _head_cols(h)] for h in hs)
            _, ln, a = _sb_weights(qb, kb, _Heads(cs), _sb_masks(i, jb, blk), upper)
            cs = _Heads(cs) + _hmap(_rowsum, ln)
            accs = _Heads(accs) + _hmap(_bd(NN), a, vb)
            return s + 1, tuple(cs.v), tuple(accs.v)

        init = (jnp.int32(0), tuple(jnp.zeros((blk, 1), F32) for _ in hs), tuple(jnp.zeros((blk, d), F32) for _ in hs))
        _, _, accs = lax.while_loop(lambda st: _sb_alive(st[0], i, st[1]), step, init)
        for h in hs:
            o_ref[:, _head_cols(h)] = accs[h].astype(o_ref.dtype)

        if ng:
            @pl.when((pl.program_id(0) == groups - 1) & (i == nb - 1))
            def _():
                comm.finish()

    qspec = pl.BlockSpec((blk, SB_HEADS * d), lambda g, i: (i, g))
    kspec = pl.BlockSpec((t, SB_HEADS * d), lambda g, i: (0, g))
    return pl.pallas_call(
        body, name=name, grid=(groups, nb), in_specs=[qspec, kspec, kspec] + [ANY] * ng,
        out_specs=[qspec] + [ANY] * ng,
        out_shape=[jax.ShapeDtypeStruct((t, WIDTH), BF16)] + [jax.ShapeDtypeStruct(s.shape, s.dtype) for s in gather],
        input_output_aliases={3 + b: 1 + b for b in range(ng)},
        scratch_shapes=_dma_sems(6 * ng) if ng else [],
        compiler_params=_params(("arbitrary", "arbitrary") if ng else ("parallel", "arbitrary")),
    )(q, k, v, *gather)


def sb_bwd(q, k, v, do, *, name, exchange=()):
    t = q.shape[0]
    blk = min(SB_BLOCK, t)
    d = HEAD_DIM
    nb = t // blk
    hs = range(SB_HEADS)
    ne = len(exchange)
    groups = HEADS // SB_HEADS

    def body(*refs):
        q_ref, k_ref, v_ref, do_ref = refs[:4]
        dq_ref, dk_ref, dv_ref = refs[4 + ne:7 + ne]
        p_buf, z_buf = refs[7 + 2 * ne:9 + 2 * ne]
        comm = _Exchange(refs[4:4 + ne], refs[7 + ne:7 + 2 * ne], *refs[9 + 2 * ne:]) if ne else None
        i = pl.program_id(1)
        if ne:
            @pl.when((pl.program_id(0) == 0) & (i == 0))
            def _():
                comm.start()

        @pl.when(i == 0)
        def _():
            dk_ref[...] = jnp.zeros_like(dk_ref)
            dv_ref[...] = jnp.zeros_like(dv_ref)

        qb = _Heads(q_ref[:, _head_cols(h)] for h in hs)
        dob = _Heads(do_ref[:, _head_cols(h)] for h in hs)
        ri = lax.broadcasted_iota(jnp.int32, (blk, blk), 0)
        ci = lax.broadcasted_iota(jnp.int32, (blk, blk), 1)
        upper = (ri > ci).astype(BF16)
        lower = (ri < ci).astype(BF16)

        def right_to_left(state):
            s, cs = state
            jb = i - s
            rows = pl.ds(pl.multiple_of(jb * blk, blk), blk)
            kb = _Heads(k_ref[rows, _head_cols(h)] for h in hs)
            vb = _Heads(v_ref[rows, _head_cols(h)] for h in hs)
            z, ln, a = _sb_weights(qb, kb, _Heads(cs), _sb_masks(i, jb, blk), upper)
            p = a * _hmap(_bd(NT), dob, vb)
            dv = _hmap(_bd(TN), a, dob)
            for h in hs:
                p_buf[h, jb] = p.v[h]
                z_buf[h, jb] = z.v[h]
                dv_ref[rows, _head_cols(h)] += dv.v[h]
            return s + 1, tuple((_Heads(cs) + _hmap(_rowsum, ln)).v)

        n_done, _ = lax.while_loop(lambda st: _sb_alive(st[0], i, st[1]), right_to_left,
                                   (jnp.int32(0), tuple(jnp.zeros((blk, 1), F32) for _ in hs)))

        def left_to_right(jb, carry):
            cps, dqs = carry
            rows = pl.ds(pl.multiple_of(jb * blk, blk), blk)
            mask = _sb_masks(i, jb, blk)
            kb = _Heads(k_ref[rows, _head_cols(h)] for h in hs)
            p = _Heads(p_buf[h, jb] for h in hs)
            sg = _hmap(_sigmoid, _Heads(z_buf[h, jb] for h in hs))
            prefix = _hmap(lambda a: _dot01_2(a, lower), p) + _Heads(cps)
            dz = _hmap(lambda a: jnp.where(mask, a, 0.0), p * (1.0 - sg) - sg * prefix)
            dk = _hmap(_bd(TN), dz, qb)
            for h in hs:
                dk_ref[rows, _head_cols(h)] += dk.v[h]
            return tuple((_Heads(cps) + _hmap(_rowsum, p)).v), tuple((_Heads(dqs) + _hmap(_bd(NN), dz, kb)).v)

        _, dqs = lax.fori_loop(i + 1 - n_done, i + 1, left_to_right,
                               (tuple(jnp.zeros((blk, 1), F32) for _ in hs), tuple(jnp.zeros((blk, d), F32) for _ in hs)))
        for h in hs:
            dq_ref[:, _head_cols(h)] = dqs[h]

        if ne:
            @pl.when((pl.program_id(0) == groups - 1) & (i == nb - 1))
            def _():
                comm.finish()

    qspec = pl.BlockSpec((blk, SB_HEADS * d), lambda g, i: (i, g))
    kspec = pl.BlockSpec((t, SB_HEADS * d), lambda g, i: (0, g))
    s = jax.ShapeDtypeStruct((t, WIDTH), F32)
    buf = pltpu.VMEM((SB_HEADS, nb, blk, blk), F32)
    return pl.pallas_call(
        body, name=name, grid=(groups, nb), in_specs=[qspec, kspec, kspec, qspec] + [ANY] * ne,
        out_specs=[qspec, kspec, kspec] + [ANY] * ne, out_shape=[s, s, s] + _exchange_shapes(exchange),
        scratch_shapes=[buf, buf] + (_dma_sems(3 * ne) if ne else []),
        compiler_params=_params(("arbitrary", "arbitrary") if ne else ("parallel", "arbitrary")),
    )(q, k, v, do, *exchange)


PACK_COLS = 1024
ANY = pl.BlockSpec(memory_space=pl.ANY)


def _mesh_pos():
    return lax.axis_index("x"), lax.axis_index("y"), lax.axis_index("c")


def _other_chips(x, y):
    return [(1 - x, y), (x, 1 - y), (1 - x, 1 - y)]


def _dma_sems(n):
    return [pltpu.SemaphoreType.DMA((n,)), pltpu.SemaphoreType.DMA((n,))]


class _Gather:
    def __init__(self, o_refs, send_sems, recv_sems):
        self.o_refs, self.send_sems, self.recv_sems = o_refs, send_sems, recv_sems

    def _copy(self, b, k, chip, hf, to):
        rows = self.o_refs[b].at[chip, hf]
        return pltpu.make_async_remote_copy(src_ref=rows, dst_ref=rows, send_sem=self.send_sems.at[6 * b + k],
                                            recv_sem=self.recv_sems.at[6 * b + k], device_id=to, device_id_type=MESH)

    def start(self):
        x, y, c = _mesh_pos()
        for b in range(len(self.o_refs)):
            for k, (cx, cy) in enumerate(_other_chips(x, y)):
                self._copy(b, k, 2 * x + y, c, (cx, cy, c)).start()

    def finish(self):
        x, y, c = _mesh_pos()
        chips = _other_chips(x, y)
        for b in range(len(self.o_refs)):
            for k, (cx, cy) in enumerate(chips):
                self._copy(b, k, 2 * cx + cy, c, (x, y, c)).wait_recv()
                self._copy(b, 3 + k, 2 * cx + cy, c, (x, y, 1 - c)).start()
        for b in range(len(self.o_refs)):
            for k, (cx, cy) in enumerate(chips):
                self._copy(b, 3 + k, 2 * cx + cy, 1 - c, (x, y, c)).wait_recv()
                self._copy(b, k, 2 * x + y, c, (cx, cy, c)).wait_send()
                self._copy(b, 3 + k, 2 * cx + cy, c, (x, y, 1 - c)).wait_send()


def all_gather_chips(slots, *, name):
    nb = len(slots)

    def body(*refs):
        g = _Gather(refs[nb:2 * nb], *refs[2 * nb:])
        g.start()
        g.finish()

    return pl.pallas_call(
        body, name=name, in_specs=[ANY] * nb, out_specs=[ANY] * nb, input_output_aliases={b: b for b in range(nb)},
        out_shape=[jax.ShapeDtypeStruct(s.shape, s.dtype) for s in slots], scratch_shapes=_dma_sems(6 * nb),
    )(*slots)


def sibling_swap(gs, *, name):
    nb = len(gs)

    def body(*refs):
        g_refs, o_refs, (send_sems, recv_sems) = refs[:nb], refs[nb:2 * nb], refs[2 * nb:]
        x, y, c = _mesh_pos()
        cps = [pltpu.make_async_remote_copy(src_ref=g_refs[b].at[j, 1 - c], dst_ref=o_refs[b].at[j],
                                            send_sem=send_sems.at[N_CHIPS * b + j], recv_sem=recv_sems.at[N_CHIPS * b + j],
                                            device_id=(x, y, 1 - c), device_id_type=MESH)
               for b in range(nb) for j in range(N_CHIPS)]
        for cp in cps:
            cp.start()
        for cp in cps:
            cp.wait()

    return pl.pallas_call(
        body, name=name, in_specs=[ANY] * nb, out_specs=[ANY] * nb,
        out_shape=[jax.ShapeDtypeStruct((g.shape[0],) + g.shape[2:], g.dtype) for g in gs],
        scratch_shapes=_dma_sems(N_CHIPS * nb),
    )(*gs)


class _Exchange:
    def __init__(self, s_refs, o_refs, send_sems, recv_sems):
        self.s_refs, self.o_refs, self.send_sems, self.recv_sems = s_refs, o_refs, send_sems, recv_sems

    def _copies(self):
        x, y, c = _mesh_pos()
        return [pltpu.make_async_remote_copy(src_ref=self.s_refs[b].at[2 * cx + cy], dst_ref=self.o_refs[b].at[k],
                                             send_sem=self.send_sems.at[3 * b + k], recv_sem=self.recv_sems.at[3 * b + k],
                                             device_id=(cx, cy, c), device_id_type=MESH)
                for b in range(len(self.s_refs)) for k, (cx, cy) in enumerate(_other_chips(x, y))]

    def start(self):
        for cp in self._copies():
            cp.start()

    def finish(self):
        for cp in self._copies():
            cp.wait()


def _exchange_shapes(s1s):
    return [jax.ShapeDtypeStruct((3,) + s.shape[1:], s.dtype) for s in s1s]


def chip_exchange(s1s, *, name):
    nb = len(s1s)

    def body(*refs):
        comm = _Exchange(refs[:nb], refs[nb:2 * nb], *refs[2 * nb:])
        comm.start()
        comm.finish()

    return pl.pallas_call(
        body, name=name, in_specs=[ANY] * nb, out_specs=[ANY] * nb, out_shape=_exchange_shapes(s1s),
        scratch_shapes=_dma_sems(3 * nb),
    )(*s1s)


def sibling_merge(halves, *, name):
    nb = len(halves)

    def body(*refs):
        o_refs, (send_sems, recv_sems) = refs[nb:2 * nb], refs[2 * nb:]
        x, y, c = _mesh_pos()
        cps = [pltpu.make_async_remote_copy(src_ref=o_refs[b].at[c], dst_ref=o_refs[b].at[c], send_sem=send_sems.at[b],
                                            recv_sem=recv_sems.at[b], device_id=(x, y, 1 - c), device_id_type=MESH)
               for b in range(nb)]
        for cp in cps:
            cp.start()
        for cp in cps:
            cp.wait()

    return pl.pallas_call(
        body, name=name, in_specs=[ANY] * nb, out_specs=[ANY] * nb, input_output_aliases={b: b for b in range(nb)},
        out_shape=[jax.ShapeDtypeStruct(h.shape, h.dtype) for h in halves], scratch_shapes=_dma_sems(nb),
    )(*halves)


def all_reduce_small(buf, *, name):
    n_dev = 8

    def body(b_ref, o_ref, recv_buf, send_sems, recv_sems):
        x, y, c = _mesh_pos()
        me = 4 * x + 2 * y + c
        pos = lambda t: (t // 4, (t // 2) % 2, t % 2)

        def copy(t, slot):
            return pltpu.make_async_remote_copy(src_ref=b_ref, dst_ref=recv_buf.at[slot], send_sem=send_sems.at[t],
                                                recv_sem=recv_sems.at[slot], device_id=pos(t), device_id_type=MESH)

        for t in range(n_dev):
            @pl.when(t != me)
            def _(t=t):
                copy(t, me).start()

        recv_buf[me] = b_ref[...]
        for t in range(n_dev):
            @pl.when(t != me)
            def _(t=t):
                copy(t, t).wait_recv()
                copy(t, me).wait_send()

        acc = recv_buf[0]
        for t in range(1, n_dev):
            acc = acc + recv_buf[t]
        o_ref[...] = acc

    return pl.pallas_call(
        body, name=name, out_shape=jax.ShapeDtypeStruct(buf.shape, F32),
        in_specs=[pl.BlockSpec(memory_space=pltpu.VMEM)], out_specs=pl.BlockSpec(memory_space=pltpu.VMEM),
        scratch_shapes=[pltpu.VMEM((n_dev,) + buf.shape, F32), pltpu.SemaphoreType.DMA((n_dev,)),
                        pltpu.SemaphoreType.DMA((n_dev,))],
    )(buf)


REDUCE_ROWS = (512, 384, 256, 128)


def add_selected(sel, a5, b, *, name):
    n, _, rh, w = a5.shape
    tr = _pick(rh, REDUCE_ROWS)

    def body(sel_ref, a_ref, b_ref, o_ref, ob_ref):
        s = a_ref[...] + b_ref[...]
        o_ref[...] = s
        ob_ref[...] = s.astype(BF16)

    blk = pl.BlockSpec((None, tr, w), lambda j, i, s: (j, i, 0))
    return pl.pallas_call(
        body, name=name,
        grid_spec=pltpu.PrefetchScalarGridSpec(
            num_scalar_prefetch=1, grid=(n, rh // tr),
            in_specs=[pl.BlockSpec((None, None, tr, w), lambda j, i, s: (j, s[0], i, 0)), blk], out_specs=[blk, blk]),
        out_shape=[jax.ShapeDtypeStruct((n, rh, w), F32), jax.ShapeDtypeStruct((n, rh, w), BF16)],
        compiler_params=_params(("arbitrary", "arbitrary")),
    )(sel, a5, b)


def add_chip_sums(sel, s1, b2, *, name):
    _, rh, w = s1.shape
    tr = _pick(rh, REDUCE_ROWS)

    def body(sel_ref, s_ref, b_ref, o_ref):
        o_ref[...] = ((s_ref[...] + b_ref[0].astype(F32)) + b_ref[1].astype(F32)) + b_ref[2].astype(F32)

    return pl.pallas_call(
        body, name=name,
        grid_spec=pltpu.PrefetchScalarGridSpec(
            num_scalar_prefetch=1, grid=(rh // tr,),
            in_specs=[pl.BlockSpec((None, tr, w), lambda i, s: (s[0], i, 0)), pl.BlockSpec((3, tr, w), lambda i, s: (0, i, 0))],
            out_specs=pl.BlockSpec((None, tr, w), lambda i, s: (s[1], i, 0))),
        out_shape=jax.ShapeDtypeStruct((2, rh, w), F32),
        compiler_params=_params(("arbitrary",)),
    )(sel, s1, b2)


BIG = (("gdn_w_out", 1), ("sb_w_q", 1), ("sb_w_out", 1), ("ffn_w_out", 1), ("ple_w_gate", 1), ("w_kv", 1),
       ("ple_w_proj", 2))
SMALL = ("ln_mix", "ln_ffn", "ln_ple", "gdn_a_log", "gdn_dt_bias", "gdn_norm", "kv_norm", "k_norm", "sb_q_norm")
WEIGHTS = ("ln_mix", "ln_ffn", "ln_ple", "gdn_w_in", "gdn_conv", "gdn_a_log", "gdn_dt_bias", "gdn_norm", "gdn_w_out",
           "kv_norm", "w_kv", "k_norm", "sb_w_q", "sb_q_norm", "sb_w_out", "ffn_w_in", "ffn_w_out", "ple_w_proj",
           "ple_w_gate")
PACK_ALIGN = 256


ROW_TILE = 16


def _rows_of(shape, tile=ROW_TILE):
    return -(-math.prod(shape) // (PACK_COLS * tile)) * tile


WEIGHT_ALIGN = 32


def _pack_rows(arrs, lead, align=PACK_ALIGN):
    parts = []
    for a in arrs:
        if a.shape[-1] == PACK_COLS:
            parts.append(a.reshape(lead + (-1, PACK_COLS)))
            continue
        flat = a.reshape(lead + (-1,))
        pad = _rows_of(a.shape[len(lead):]) * PACK_COLS - flat.shape[-1]
        if pad:
            flat = jnp.pad(flat, [(0, 0)] * len(lead) + [(0, pad)])
        parts.append(flat.reshape(lead + (-1, PACK_COLS)))
    rows = sum(q.shape[len(lead)] for q in parts)
    filler = -rows % align
    if filler:
        parts.append(jnp.zeros(lead + (filler, PACK_COLS), parts[0].dtype))
    return jnp.concatenate(parts, axis=len(lead))


def _own_slot(buf, chip):
    mine = lax.broadcasted_iota(jnp.int32, (N_CHIPS, 1, 1), 0) == chip
    slots = jnp.where(mine, buf[None], jnp.zeros((), buf.dtype))
    return slots.reshape(N_CHIPS, 2, buf.shape[0] // 2, buf.shape[1])


def _unpack_rows(buf, shapes, lead):
    out, r0 = [], 0
    for s in shapes:
        rows = _rows_of(s)
        flat = buf[(slice(None),) * len(lead) + (slice(r0, r0 + rows),)].reshape(lead + (-1,))
        out.append(flat[..., :math.prod(s)].reshape(lead + tuple(s)))
        r0 += rows
    return out


def _unshard(g, axis):
    g = jnp.moveaxis(g, 0, axis)
    s = g.shape
    return g.reshape(s[:axis] + (s[axis] * s[axis + 1],) + s[axis + 2:])


def _shard(full, axis):
    s = full.shape
    return jnp.moveaxis(full.reshape(s[:axis] + (N_CHIPS, s[axis] // N_CHIPS) + s[axis + 1:]), axis, 0)


def _to4(a):
    return a.reshape(HEADS, -1, 1, CHUNK).transpose(1, 0, 2, 3)


def _from4(a):
    return a.transpose(1, 0, 2, 3).reshape(HEADS, -1)


def _row(vec):
    flat = vec.reshape(-1)
    rows = _rows_of(flat.shape, 1)
    return jnp.pad(flat, (0, rows * PACK_COLS - flat.shape[0])).reshape(rows, PACK_COLS)


def kernel(x, p, ln_mix, ln_ffn, ln_ple, gdn_w_in, gdn_conv, gdn_a_log, gdn_dt_bias, gdn_norm, gdn_w_out, kv_norm, w_kv, k_norm, sb_w_q, sb_q_norm, sb_w_out, ffn_w_in, ffn_w_out, ple_w_proj, ple_w_gate, loss_target, m_ln_mix, m_ln_ffn, m_ln_ple, m_gdn_w_in, m_gdn_conv, m_gdn_a_log, m_gdn_dt_bias, m_gdn_norm, m_gdn_w_out, m_kv_norm, m_w_kv, m_k_norm, m_sb_w_q, m_sb_q_norm, m_sb_w_out, m_ffn_w_in, m_ffn_w_out, m_ple_w_proj, m_ple_w_gate, v_ln_mix, v_ln_ffn, v_ln_ple, v_gdn_w_in, v_gdn_conv, v_gdn_a_log, v_gdn_dt_bias, v_gdn_norm, v_gdn_w_out, v_kv_norm, v_w_kv, v_k_norm, v_sb_w_q, v_sb_q_norm, v_sb_w_out, v_ffn_w_in, v_ffn_w_out, v_ple_w_proj, v_ple_w_gate):
    w = dict(ln_mix=ln_mix, ln_ffn=ln_ffn, ln_ple=ln_ple, gdn_w_in=gdn_w_in, gdn_conv=gdn_conv, gdn_a_log=gdn_a_log,
             gdn_dt_bias=gdn_dt_bias, gdn_norm=gdn_norm, gdn_w_out=gdn_w_out, kv_norm=kv_norm, w_kv=w_kv, k_norm=k_norm,
             sb_w_q=sb_w_q, sb_q_norm=sb_q_norm, sb_w_out=sb_w_out, ffn_w_in=ffn_w_in, ffn_w_out=ffn_w_out,
             ple_w_proj=ple_w_proj, ple_w_gate=ple_w_gate)
    mom1 = dict(ln_mix=m_ln_mix, ln_ffn=m_ln_ffn, ln_ple=m_ln_ple, gdn_w_in=m_gdn_w_in, gdn_conv=m_gdn_conv,
                gdn_a_log=m_gdn_a_log, gdn_dt_bias=m_gdn_dt_bias, gdn_norm=m_gdn_norm, gdn_w_out=m_gdn_w_out,
                kv_norm=m_kv_norm, w_kv=m_w_kv, k_norm=m_k_norm, sb_w_q=m_sb_w_q, sb_q_norm=m_sb_q_norm,
                sb_w_out=m_sb_w_out, ffn_w_in=m_ffn_w_in, ffn_w_out=m_ffn_w_out, ple_w_proj=m_ple_w_proj,
                ple_w_gate=m_ple_w_gate)
    mom2 = dict(ln_mix=v_ln_mix, ln_ffn=v_ln_ffn, ln_ple=v_ln_ple, gdn_w_in=v_gdn_w_in, gdn_conv=v_gdn_conv,
                gdn_a_log=v_gdn_a_log, gdn_dt_bias=v_gdn_dt_bias, gdn_norm=v_gdn_norm, gdn_w_out=v_gdn_w_out,
                kv_norm=v_kv_norm, w_kv=v_w_kv, k_norm=v_k_norm, sb_w_q=v_sb_w_q, sb_q_norm=v_sb_q_norm,
                sb_w_out=v_sb_w_out, ffn_w_in=v_ffn_w_in, ffn_w_out=v_ffn_w_out, ple_w_proj=v_ple_w_proj,
                ple_w_gate=v_ple_w_gate)
    depth = ln_mix.shape[0]
    n_a = gdn_w_in.shape[0]
    xi, yi, ci = _mesh_pos()
    chip = 2 * xi + yi
    sel_c = jnp.reshape(ci, (1,)).astype(jnp.int32)
    sel_chip = jnp.stack([chip, ci]).astype(jnp.int32)
    h = x[0]
    tgt = loss_target[0]
    t = h.shape[0]

    shard_shapes = [w[n].shape for n, _ in BIG]

    def layer_items(i):
        if i < n_a:
            items = [("gdn_w_out", i, 0), ("ffn_w_out", i, 0), ("ple_w_gate", i, 0), ("ple_w_proj", i, 1)]
            return items + ([("w_kv", None, 1)] if i == n_a - 1 else [])
        j = i - n_a
        return [("sb_w_q", j, 0), ("sb_w_out", j, 0), ("ffn_w_out", i, 0), ("ple_w_gate", i, 0), ("ple_w_proj", i, 1)]

    def layer_shards(i):
        return [w[n] if idx is None else w[n][idx] for n, idx, _ in layer_items(i)]

    def layer_slots(i):
        packed = _pack_rows([q.astype(BF16) for q in layer_shards(i)], (), align=WEIGHT_ALIGN)
        own = [packed, ffn_w_in[i].astype(BF16)] + ([gdn_w_in[i].astype(BF16)] if i < n_a else [])
        return [_own_slot(b, chip) for b in own]

    def layer_weights(i, got):
        parts = _unpack_rows(got[0].reshape(N_CHIPS, -1, PACK_COLS), [q.shape for q in layer_shards(i)], (N_CHIPS,))
        out = {n: _unshard(g, ax) for (n, _, ax), g in zip(layer_items(i), parts)}
        out["ffn_w_in"] = got[1].reshape((N_CHIPS, 1) + ffn_w_in.shape[1:])
        if i < n_a:
            out["gdn_w_in"] = _unshard(got[2].reshape((N_CHIPS,) + gdn_w_in.shape[1:]), 1)
        return out

    slots = [layer_slots(i) for i in range(depth)]
    wl = [layer_weights(0, all_gather_chips(slots[0], name="all_gather_weights"))]
    conv_rows = _rows_of(gdn_conv.shape, 1)
    small_rows = sum(_rows_of(w[n].shape, 1) for n in SMALL)
    buf_rows = -(-(small_rows + N_CHIPS * conv_rows) // 8) * 8
    conv_buf = jnp.zeros((buf_rows, PACK_COLS), F32)
    conv_buf = lax.dynamic_update_slice(conv_buf, _row(gdn_conv) * (ci == 0).astype(F32), (chip * conv_rows, 0))
    conv_all = all_reduce_small(conv_buf, name="all_reduce_small")[:N_CHIPS * conv_rows]
    conv_full = _unshard(conv_all.reshape(N_CHIPS, -1)[:, :math.prod(gdn_conv.shape)].reshape((N_CHIPS,) + gdn_conv.shape), 2)

    saved = []
    k_sh = v_sh = None
    mid = None
    for i in range(depth):
        s = dict(h0=h)
        wi = wl[i]
        nxt = slots[i + 1] if i + 1 < depth else ()
        s["hn"] = hn = rms_fwd(h, ln_mix[i:i + 1], name="rms_fwd")
        if i < n_a:
            w_in = wi["gdn_w_in"]
            s["w_m"], s["w_abt"] = w_in[:, :4 * WIDTH], w_in[:, 4 * WIDTH:].T
            s["proj"] = proj = matmul(hn, s["w_m"], "nn", out_dtype=BF16, name="mm_gdn_in")
            s["ab"] = ab = matmul(s["w_abt"], hn, "nt", name="mm_gdn_ab")
            s["a_log"], s["dt"] = gdn_a_log[i][:, None], gdn_dt_bias[i][:, None]
            g8, b8 = gates_fwd(ab, s["a_log"], s["dt"], name="gates_fwd")
            s["g4"], s["b4"] = _to4(g8), _to4(b8)
            s["qkv"] = qkv = conv_fwd(proj, conv_full[i], name="conv_fwd")
            s["o"], s["s0"], s["tinv"], *got = gdn_fwd(qkv, s["g4"], s["b4"], gather=nxt, name="gdn_fwd")
            s["y"] = y = gatenorm_fwd(s["o"], proj, gdn_norm[i:i + 1], name="gatenorm_fwd")
            h = matmul(y, wi["gdn_w_out"], "nn", add=h, name="mm_out")
        else:
            j = i - n_a
            s["qraw"] = qraw = matmul(hn, wi["sb_w_q"], "nn", name="mm_sq")
            s["q"] = q = headnorm_fwd(qraw, 0, sb_q_norm[j:j + 1], HEAD_DIM ** -0.5, name="headnorm_q")
            s["o"], *got = sb_fwd(q, k_sh, v_sh, gather=nxt, name="sb_fwd")
            h = matmul(s["o"], wi["sb_w_out"], "nn", add=h, name="mm_out")
        if nxt:
            wl.append(layer_weights(i + 1, got))
        s["h1"] = h
        s["hn2"] = hn2 = rms_fwd(h, ln_ffn[i:i + 1], name="rms_fwd")
        s["gu"] = gu = matmul(hn2, wi["ffn_w_in"], "nn", b_chips=0, out_dtype=BF16, name="mm_ffn_in")
        s["act"] = act = swiglu_fwd(gu, name="swiglu_fwd")
        h = matmul(act, wi["ffn_w_out"], "nn", add=h, name="mm_ffn_out")
        s["h2"] = h
        s["hn3"] = hn3 = rms_fwd(h, ln_ple[i:i + 1], name="rms_fwd")
        s["gt"] = gt = matmul(hn3, wi["ple_w_gate"], "nn", name="mm_sq")
        s["pp"] = pp = matmul(p[i, 0], wi["ple_w_proj"], "nn", name="mm_ple_proj")
        h = ple_fwd(h, pp, gt, name="ple_fwd")
        saved.append(s)
        if i == n_a - 1:
            mid = dict(h=h)
            mid["hk"] = hk = rms_fwd(h, kv_norm[None, :], name="rms_fwd")
            mid["kv"] = kv = matmul(hk, wi["w_kv"], "nn", name="mm_kv")
            k_sh = headnorm_fwd(kv, 0, k_norm[None, :], 1.0, name="headnorm_k")
            v_sh = kv[:, WIDTH:].astype(BF16)

    dh, sq = loss_head(h, tgt, name="loss_head")
    loss = lax.psum(0.5 * jnp.sum(sq) / h.shape[1], ("x", "y", "c"))

    gw = {n: [None] * w[n].shape[0] for n in WEIGHTS if w[n].ndim >= 2 and n not in ("w_kv",)}
    dks, dvs = [], []
    reduced = [None] * depth

    def reduce_begin(i):
        pieces = [_shard(gw[n] if idx is None else gw[n][idx], ax) for n, idx, ax in layer_items(i)]
        bufs = [_pack_rows(pieces, (N_CHIPS,)), gw["ffn_w_in"][i]] + ([_shard(gw["gdn_w_in"][i], 1)] if i < n_a else [])
        g5s = [g.reshape(N_CHIPS, 2, g.shape[1] // 2, g.shape[2]) for g in bufs]
        from_sibling = sibling_swap(g5s, name="sibling_swap")
        return [add_selected(sel_c, g5, fs, name="add_selected") for g5, fs in zip(g5s, from_sibling)]

    def reduce_end(i, s1s, from_chips):
        s2s = [add_chip_sums(sel_chip, s1, fc, name="add_chip_sums") for (s1, _), fc in zip(s1s, from_chips)]
        reduced[i] = sibling_merge(s2s, name="sibling_merge")

    pending = None
    for i in reversed(range(depth)):
        s = saved[i]
        riding = [s1b for _, s1b in pending[1]] if pending else []
        if i == n_a - 1:
            dkraw, gw["k_norm"] = headnorm_bwd(mid["kv"], 0, k_norm[None, :], 1.0, tuple(dks), name="headnorm_k_bwd")
            dkv = jnp.concatenate([dkraw, sum_cast(dvs, BF16, name="sum_dv")], axis=1)
            dhk = matmul(dkv, wl[i]["w_kv"], "nt", name="mm_kv_dx")
            gw["w_kv"] = matmul(mid["hk"], dkv, "tn", name="mm_kv_dw")
            dh, gw["kv_norm"] = rms_bwd(mid["h"], kv_norm[None, :], dhk, dh, name="rms_bwd")
        dpp, dgt = ple_bwd(dh, s["pp"], s["gt"], name="ple_bwd")
        gw["ple_w_proj"][i] = matmul(p[i, 0], dpp, "tn", name="mm_ple_proj_dw")
        gw["ple_w_gate"][i] = matmul(s["hn3"], dgt, "tn", name="mm_sq_dw")
        dhn3 = matmul(dgt, wl[i]["ple_w_gate"], "nt", name="mm_sq_dx")
        dh, gw["ln_ple"][i] = rms_bwd(s["h2"], ln_ple[i:i + 1], dhn3, dh, name="rms_bwd")
        dact = matmul(dh, wl[i]["ffn_w_out"], "nt", out_dtype=BF16, name="mm_ffn_out_dx")
        gw["ffn_w_out"][i] = matmul(s["act"], dh, "tn", name="mm_ffn_out_dw")
        dgu = swiglu_bwd(s["gu"], dact, name="swiglu_bwd")
        dhn2 = matmul(dgu, wl[i]["ffn_w_in"], "nt", b_chips=0, name="mm_ffn_in_dx")
        gw["ffn_w_in"][i] = matmul(s["hn2"], dgu, "tn", out_chips=True, name="mm_ffn_in_dw")
        dh, gw["ln_ffn"][i] = rms_bwd(s["h1"], ln_ffn[i:i + 1], dhn2, dh, name="rms_bwd")
        if i < n_a:
            dy = matmul(dh, wl[i]["gdn_w_out"], "nt", name="mm_sq_dx")
            gw["gdn_w_out"][i] = matmul(s["y"], dh, "tn", name="mm_sq_dw")
            do, dproj, gw["gdn_norm"][i] = gatenorm_bwd(s["o"], s["proj"], gdn_norm[i:i + 1], dy, name="gatenorm_bwd")
            dqkv, dg4, db4, *from_chips = gdn_bwd(s["qkv"], s["g4"], s["b4"], s["s0"], s["tinv"], do, exchange=riding,
                                                  name="gdn_bwd")
            dab, dal, ddt = gates_bwd(s["ab"], s["a_log"], s["dt"], _from4(dg4), _from4(db4), name="gates_bwd")
            gw["gdn_a_log"][i], gw["gdn_dt_bias"][i] = dal[:, 0], ddt[:, 0]
            dproj, gw["gdn_conv"][i] = conv_bwd(s["proj"], conv_full[i], dqkv, dproj, name="conv_bwd")
            dhn = matmul(dproj, s["w_m"], "nt", name="mm_gdn_in_dx")
            dhn = matmul(dab, s["w_abt"], "tn", add=dhn, name="mm_gdn_ab_dx")
            dwm = matmul(s["hn"], dproj, "tn", name="mm_gdn_in_dw")
            dwab = matmul(dab, s["hn"], "nn", name="mm_gdn_ab_dw")
            gw["gdn_w_in"][i] = jnp.concatenate([dwm, dwab.T], axis=1)
        else:
            j = i - n_a
            do = matmul(dh, wl[i]["sb_w_out"], "nt", out_dtype=BF16, name="mm_sb_out_dx")
            gw["sb_w_out"][j] = matmul(s["o"], dh, "tn", name="mm_sq_dw")
            dq, dk, dv, *from_chips = sb_bwd(s["q"], k_sh, v_sh, do, exchange=riding, name="sb_bwd")
            dks.append(dk)
            dvs.append(dv)
            dqraw, gw["sb_q_norm"][j] = headnorm_bwd(s["qraw"], 0, sb_q_norm[j:j + 1], HEAD_DIM ** -0.5, (dq,),
                                                    name="headnorm_q_bwd")
            dhn = matmul(dqraw, wl[i]["sb_w_q"], "nt", name="mm_sq_dx")
            gw["sb_w_q"][j] = matmul(s["hn"], dqraw, "tn", name="mm_sq_dw")
        dh, gw["ln_mix"][i] = rms_bwd(s["h0"], ln_mix[i:i + 1], dhn, dh, name="rms_bwd")
        if pending:
            reduce_end(pending[0], pending[1], from_chips)
        pending = (i, reduce_begin(i))
    reduce_end(pending[0], pending[1], chip_exchange([s1b for _, s1b in pending[1]], name="chip_exchange"))
    grad_x = dh[None]

    def stacked(n):
        g = gw[n]
        if isinstance(g, list):
            g = jnp.stack([a.reshape(w[n].shape[1:]) if n in SMALL else a for a in g])
        return g

    small_buf = jnp.concatenate([_row(stacked(n)) for n in SMALL] + [_row(stacked("gdn_conv"))], axis=0)
    small_buf = jnp.pad(small_buf, ((0, buf_rows - small_buf.shape[0]), (0, 0)))
    small_sum = all_reduce_small(small_buf, name="all_reduce_small")
    grads = {}
    r0 = 0
    for n in SMALL:
        rows = _rows_of(w[n].shape, 1)
        grads[n] = small_sum[r0:r0 + rows].reshape(-1)[:math.prod(w[n].shape)].reshape(w[n].shape)
        r0 += rows
    conv_g = small_sum[r0:r0 + N_CHIPS * conv_rows].reshape(-1)[:N_CHIPS * math.prod(gdn_conv.shape)]
    conv_g = conv_g.reshape((gdn_conv.shape[0], CONV_WIDTH, N_CHIPS, gdn_conv.shape[2]))
    grads["gdn_conv"] = lax.dynamic_index_in_dim(conv_g, chip, axis=2, keepdims=False)

    per_layer = {n: [None] * w[n].shape[0] for n, _ in BIG if n != "w_kv"}
    for i in range(depth):
        parts = _unpack_rows(reduced[i][0].reshape(-1, PACK_COLS), [q.shape for q in layer_shards(i)], ())
        for (n, idx, _), g in zip(layer_items(i), parts):
            if idx is None:
                grads[n] = g
            else:
                per_layer[n][idx] = g
    for n, parts in per_layer.items():
        grads[n] = jnp.stack(parts)
    grads["ffn_w_in"] = jnp.stack([reduced[i][1].reshape(ffn_w_in.shape[1:]) for i in range(depth)])
    grads["gdn_w_in"] = jnp.stack([reduced[i][2].reshape(gdn_w_in.shape[1:]) for i in range(n_a)])

    delta, new_m, new_v = {}, {}, {}
    for n in WEIGHTS:
        delta[n], new_m[n], new_v[n] = adamw(w[n], grads[n], mom1[n], mom2[n], name="adamw")
    return (loss, grad_x, *[grads[n] for n in WEIGHTS], *[delta[n] for n in WEIGHTS],
            *[new_m[n] for n in WEIGHTS], *[new_v[n] for n in WEIGHTS])
```

```python
import functools
import math

import jax
import jax.numpy as jnp
from jax import lax
from jax.experimental import pallas as pl
from jax.experimental.pallas import tpu as pltpu

F32 = jnp.float32
BF16 = jnp.bfloat16
EPS = 1e-6
HEADS = 8
HEAD_DIM = 128
WIDTH = HEADS * HEAD_DIM
CHUNK = 64
CONV_WIDTH = 4
N_CHIPS = 4
ADAM_LR, ADAM_B1, ADAM_B2, ADAM_EPS, ADAM_WD, ADAM_STEP = 0.001, 0.9, 0.999, 1e-08, 0.01, 10
V7X_VMEM_BYTES = 64 * 1024 * 1024
VMEM_LIMIT = V7X_VMEM_BYTES - 8 * 1024 * 1024
HIGHEST = lax.Precision.HIGHEST
MESH = pl.DeviceIdType.MESH


def _params(sem=None):
    return pltpu.CompilerParams(dimension_semantics=sem, vmem_limit_bytes=VMEM_LIMIT)


def _pick(n, prefs):
    for t in prefs:
        if t <= n and n % t == 0:
            return t
    return n


def _bdot(a, b, dims):
    return lax.dot_general(a.astype(BF16), b.astype(BF16), (((dims[0],), (dims[1],)), ((), ())),
                           preferred_element_type=F32)


NN, NT, TN = (1, 0), (1, 1), (0, 0)


MM_TILES = (1024, 1408, 512, 256, 128)


def matmul(a, b, form, *, out_dtype=F32, add=None, name, b_chips=None, out_chips=False):
    ns = None
    if b_chips is not None:
        ns = b.shape[3]
        b_shape = (b.shape[2], N_CHIPS * ns)
    else:
        b_shape = b.shape
    if form == "nn":
        (m, k), (k2, n) = a.shape, b_shape
    elif form == "nt":
        (m, k), (n, k2) = a.shape, b_shape
    else:
        (k, m), (k2, n) = a.shape, b_shape
    assert k == k2, (a.shape, b.shape, form)
    if out_chips:
        ns = n // N_CHIPS
    tm = _pick(m, MM_TILES)
    tn = _pick(n, MM_TILES)
    tk = k if k <= 1024 else _pick(k, MM_TILES)
    if ns is not None and (form == "nn" or out_chips):
        tn = ns
    if ns is not None and form == "nt":
        tk = ns
    nk = k // tk
    if form == "tn":
        a_spec = pl.BlockSpec((tk, tm), lambda i, j, kk: (kk, i))
    else:
        a_spec = pl.BlockSpec((tm, tk), lambda i, j, kk: (i, kk))
    if b_chips is not None and form == "nn":
        b_spec = pl.BlockSpec((None, None, tk, ns), lambda i, j, kk: (j, b_chips, kk, 0))
    elif b_chips is not None:
        b_spec = pl.BlockSpec((None, None, tn, ns), lambda i, j, kk: (kk, b_chips, j, 0))
    elif form == "nt":
        b_spec = pl.BlockSpec((tn, tk), lambda i, j, kk: (j, kk))
    else:
        b_spec = pl.BlockSpec((tk, tn), lambda i, j, kk: (kk, j))
    if out_chips:
        o_spec = pl.BlockSpec((None, tm, ns), lambda i, j, kk: (j, i, 0))
    else:
        o_spec = pl.BlockSpec((tm, tn), lambda i, j, kk: (i, j))
    dims = {"nn": NN, "nt": NT, "tn": TN}[form]
    has_add = add is not None

    def body(*refs):
        a_ref, b_ref = refs[:2]
        add_ref = refs[2] if has_add else None
        o_ref = refs[2 + has_add]

        def finish(r):
            if has_add:
                r = r + add_ref[...].astype(F32)
            o_ref[...] = r.astype(out_dtype)

        part = _bdot(a_ref[...], b_ref[...], dims)
        if nk == 1:
            finish(part)
            return
        acc_ref = refs[3 + has_add]
        kk = pl.program_id(2)

        @pl.when(kk == 0)
        def _():
            acc_ref[...] = part

        @pl.when(kk > 0)
        def _():
            acc_ref[...] += part

        @pl.when(kk == nk - 1)
        def _():
            finish(acc_ref[...])

    in_specs = [a_spec, b_spec] + ([o_spec] if has_add else [])
    args = (a, b) + ((add,) if has_add else ())
    return pl.pallas_call(
        body, name=name, grid=(m // tm, n // tn, nk), in_specs=in_specs, out_specs=o_spec,
        out_shape=jax.ShapeDtypeStruct((N_CHIPS, m, ns) if out_chips else (m, n), out_dtype),
        scratch_shapes=[pltpu.VMEM((tm, tn), F32)] if nk > 1 else [],
        compiler_params=_params(("parallel", "parallel", "arbitrary")),
    )(*args)


def matmul_rows(a, b, extra, params, epilogue, out_dtypes, *, name):
    (m, k), (k2, n) = a.shape, b.shape
    assert k == k2, (a.shape, b.shape)
    tm = _pick(m, (512, 256, 128))
    tk = k if k <= 1024 else _pick(k, MM_TILES)
    nk = k // tk
    ne, npar, no = len(extra), len(params), len(out_dtypes)

    def body(*refs):
        a_ref, b_ref = refs[:2]
        e_refs, p_refs = refs[2:2 + ne], refs[2 + ne:2 + ne + npar]
        o_refs = refs[2 + ne + npar:2 + ne + npar + no]

        def finish(r):
            outs = epilogue(r, *[e[...] for e in e_refs], *[q[...] for q in p_refs])
            for o_ref, val in zip(o_refs, outs):
                o_ref[...] = val.astype(o_ref.dtype)

        part = _bdot(a_ref[...], b_ref[...], NN)
        if nk == 1:
            finish(part)
            return
        acc_ref = refs[-1]
        kk = pl.program_id(1)

        @pl.when(kk == 0)
        def _():
            acc_ref[...] = part

        @pl.when(kk > 0)
        def _():
            acc_ref[...] += part

        @pl.when(kk == nk - 1)
        def _():
            finish(acc_ref[...])

    row = pl.BlockSpec((tm, n), lambda i, kk: (i, 0))
    in_specs = ([pl.BlockSpec((tm, tk), lambda i, kk: (i, kk)), pl.BlockSpec((tk, n), lambda i, kk: (kk, 0))]
                + [row] * ne + [pl.BlockSpec((1, n), lambda i, kk: (0, 0))] * npar)
    return pl.pallas_call(
        body, name=name, grid=(m // tm, nk), in_specs=in_specs, out_specs=[row] * no,
        out_shape=[jax.ShapeDtypeStruct((m, n), dt) for dt in out_dtypes],
        scratch_shapes=[pltpu.VMEM((tm, n), F32)] if nk > 1 else [],
        compiler_params=_params(("parallel", "arbitrary")),
    )(a, b, *extra, *params)


def _const(c):
    return lambda j: c


def rowwise(fn, rows, params, outs, accs=(), *, name, tm, ncol=1):
    t = rows[0][0].shape[0]
    tm = min(tm, t)
    assert t % tm == 0
    n_rows, n_par, n_out, n_acc = len(rows), len(params), len(outs), len(accs)

    def body(*refs):
        j, i = pl.program_id(0), pl.program_id(1)
        ins = [r[...] for r in refs[:n_rows + n_par]]
        o_refs = refs[n_rows + n_par:n_rows + n_par + n_out]
        a_refs = refs[n_rows + n_par + n_out:]
        row_outs, acc_outs = fn(*ins)
        for r, val in zip(o_refs, row_outs):
            r[...] = val.astype(r.dtype)
        for r, val, spec in zip(a_refs, acc_outs, accs):
            first = (i == 0) & (j == 0) if spec[4] else (i == 0)

            @pl.when(first)
            def _(r=r, val=val):
                r[...] = val.astype(F32)

            @pl.when(jnp.logical_not(first))
            def _(r=r, val=val):
                r[...] += val.astype(F32)

    in_specs = [pl.BlockSpec((tm, w), lambda j, i, cf=cf: (i, cf(j))) for _, w, cf in rows]
    in_specs += [pl.BlockSpec((p.shape[0], w), lambda j, i, cf=cf: (0, cf(j))) for p, w, cf in params]
    out_specs = [pl.BlockSpec((tm, w), lambda j, i, cf=cf: (i, cf(j))) for _, _, w, cf in outs]
    out_specs += [pl.BlockSpec((r, w), lambda j, i, cf=cf: (0, cf(j))) for r, _, w, cf, _ in accs]
    out_shape = [jax.ShapeDtypeStruct((t, tw), dt) for tw, dt, _, _ in outs]
    out_shape += [jax.ShapeDtypeStruct((r, tw), F32) for r, tw, _, _, _ in accs]
    res = pl.pallas_call(
        body, name=name, grid=(ncol, t // tm), in_specs=in_specs, out_specs=out_specs, out_shape=out_shape,
        compiler_params=_params(("arbitrary", "arbitrary")),
    )(*[r[0] for r in rows], *[p[0] for p in params])
    return res[:n_out], res[n_out:]


def _full(arr):
    return (arr, arr.shape[1], _const(0))


def _rms(x, g):
    x = x.astype(F32)
    return x * lax.rsqrt(jnp.mean(x * x, axis=-1, keepdims=True) + EPS) * g.astype(F32)


def _sigmoid(x):
    return 1.0 / (1.0 + jnp.exp(-x))


def _silu(x):
    return x * _sigmoid(x)


def _softplus(x):
    return jnp.maximum(x, 0.0) + jnp.log(1.0 + jnp.exp(-jnp.abs(x)))


def rms_fwd(h, g, *, name):
    d = h.shape[1]
    (hn,), _ = rowwise(lambda x, gg: ((_rms(x, gg),), ()), [_full(h)], [_full(g)],
                       [(d, BF16, d, _const(0))], name=name, tm=512)
    return hn


def rms_bwd(h, g, dhn, dh_res, *, name):
    d = h.shape[1]

    def fn(x, ct, res, gg):
        _, vjp = jax.vjp(_rms, x.astype(F32), gg.astype(F32))
        dx, dg = vjp(ct.astype(F32))
        return (res.astype(F32) + dx,), (dg,)

    (dh,), (dg,) = rowwise(fn, [_full(h), _full(dhn), _full(dh_res)], [_full(g)],
                           [(d, F32, d, _const(0))], [(1, d, d, _const(0), True)], name=name, tm=256)
    return dh, dg


def _head_rms(x, g, scale):
    x = x.astype(F32)
    return x * lax.rsqrt(jnp.mean(x * x, axis=-1, keepdims=True) + EPS) * (g.astype(F32) * scale)


def headnorm_fwd(x, col0, g, scale, *, name):
    (y,), _ = rowwise(lambda a, gg: ((_head_rms(a, gg, scale),), ()),
                      [(x, HEAD_DIM, lambda j: col0 + j)], [_full(g)],
                      [(WIDTH, BF16, HEAD_DIM, lambda j: j)], name=name, tm=1024, ncol=HEADS)
    return y


def headnorm_bwd(x, col0, g, scale, dys, *, name, out_dtype=BF16):
    def fn(a, *rest):
        cts, gg = rest[:-1], rest[-1]
        ct = sum(c.astype(F32) for c in cts)
        _, vjp = jax.vjp(lambda a_, g_: _head_rms(a_, g_, scale), a.astype(F32), gg.astype(F32))
        dx, dg = vjp(ct)
        return (dx,), (dg,)

    (dx,), (dg,) = rowwise(fn, [(x, HEAD_DIM, lambda j: col0 + j)] + [(dy, HEAD_DIM, lambda j: j) for dy in dys], [_full(g)],
                           [(WIDTH, out_dtype, HEAD_DIM, lambda j: j)],
                           [(1, HEAD_DIM, HEAD_DIM, _const(0), True)], name=name, tm=1024, ncol=HEADS)
    return dx, dg


def sum_cast(parts, dtype, *, name):
    wd = parts[0].shape[1]
    (out,), _ = rowwise(lambda *a: ((sum(b.astype(F32) for b in a),), ()), [_full(a) for a in parts], [],
                        [(wd, dtype, wd, _const(0))], name=name, tm=512)
    return out


def _gatenorm(o, gate, g):
    return _head_rms(o, g, 1.0) * _silu(gate.astype(F32))


def gatenorm_fwd(o, proj, g, *, name):
    (y,), _ = rowwise(lambda a, gt, gg: ((_gatenorm(a, gt, gg),), ()),
                      [(o, HEAD_DIM, lambda j: j), (proj, HEAD_DIM, lambda j: 3 * HEADS + j)], [_full(g)],
                      [(WIDTH, BF16, HEAD_DIM, lambda j: j)], name=name, tm=1024, ncol=HEADS)
    return y


def gatenorm_bwd(o, proj, g, dy, *, name):
    def fn(a, gt, ct, gg):
        _, vjp = jax.vjp(_gatenorm, a.astype(F32), gt.astype(F32), gg.astype(F32))
        da, dgt, dg = vjp(ct.astype(F32))
        return (da, dgt), (dg,)

    (do, dproj), (dg,) = rowwise(
        fn, [(o, HEAD_DIM, lambda j: j), (proj, HEAD_DIM, lambda j: 3 * HEADS + j), (dy, HEAD_DIM, lambda j: j)],
        [_full(g)],
        [(WIDTH, F32, HEAD_DIM, lambda j: j), (4 * WIDTH, BF16, HEAD_DIM, lambda j: 3 * HEADS + j)],
        [(1, HEAD_DIM, HEAD_DIM, _const(0), True)], name=name, tm=1024, ncol=HEADS)
    return do, dproj, dg


def _swiglu(g, u):
    return _silu(g.astype(F32)) * u.astype(F32)


def ffn_in_swiglu(hn, w4, *, name):
    t, dm = hn.shape
    ns = w4.shape[3]
    half = N_CHIPS // 2
    tm = min(512, t)

    def body(a_ref, w_ref, gu_ref, act_ref):
        a = a_ref[...]
        for j in range(half):
            g = _bdot(a, w_ref[j, 0], NN)
            u = _bdot(a, w_ref[half + j, 0], NN)
            gu_ref[:, j * ns:(j + 1) * ns] = g.astype(BF16)
            gu_ref[:, (half + j) * ns:(half + j + 1) * ns] = u.astype(BF16)
            act_ref[:, j * ns:(j + 1) * ns] = (_silu(g) * u).astype(BF16)

    return pl.pallas_call(
        body, name=name, grid=(t // tm,),
        in_specs=[pl.BlockSpec((tm, dm), lambda i: (i, 0)), pl.BlockSpec(w4.shape, lambda i: (0, 0, 0, 0))],
        out_specs=[pl.BlockSpec((tm, N_CHIPS * ns), lambda i: (i, 0)), pl.BlockSpec((tm, half * ns), lambda i: (i, 0))],
        out_shape=[jax.ShapeDtypeStruct((t, N_CHIPS * ns), BF16), jax.ShapeDtypeStruct((t, half * ns), BF16)],
        compiler_params=_params(("parallel",)),
    )(hn, w4)


def ffn_out_dx_swiglu(dh, w_out, gu, *, name):
    t, dm = dh.shape
    f = w_out.shape[0]
    tm = min(256, t)

    def body(dh_ref, w_ref, gu_ref, o_ref):
        dact = _bdot(dh_ref[...], w_ref[...], NT)
        _, vjp = jax.vjp(_swiglu, gu_ref[:, :f].astype(F32), gu_ref[:, f:].astype(F32))
        dg, du = vjp(dact)
        o_ref[:, :f] = dg.astype(BF16)
        o_ref[:, f:] = du.astype(BF16)

    return pl.pallas_call(
        body, name=name, grid=(t // tm,),
        in_specs=[pl.BlockSpec((tm, dm), lambda i: (i, 0)), pl.BlockSpec((f, dm), lambda i: (0, 0)),
                  pl.BlockSpec((tm, 2 * f), lambda i: (i, 0))],
        out_specs=pl.BlockSpec((tm, 2 * f), lambda i: (i, 0)),
        out_shape=jax.ShapeDtypeStruct((t, 2 * f), BF16),
        compiler_params=_params(("parallel",)),
    )(dh, w_out, gu)


def ple_bwd(dh, pp, gt, *, name):
    d = dh.shape[1]

    def fn(ct, b, c):
        s = _sigmoid(c)
        return (ct * s, ct * b * s * (1.0 - s)), ()

    (dpp, dgt), _ = rowwise(fn, [_full(dh), _full(pp), _full(gt)], [],
                            [(d, BF16, d, _const(0)), (d, BF16, d, _const(0))], name=name, tm=512)
    return dpp, dgt


def loss_head(y, tgt, *, name):
    d = y.shape[1]

    def fn(a, b):
        e = a - b
        return (e * (1.0 / d),), (jnp.sum(e * e, axis=0, keepdims=True),)

    (dy,), (sq,) = rowwise(fn, [_full(y), _full(tgt)], [], [(d, F32, d, _const(0))],
                           [(1, d, d, _const(0), True)], name=name, tm=512)
    return dy, sq


def adamw(w, g, m, v, *, name):
    shape = w.shape
    cols = shape[-1]
    flat = lambda a: a.reshape(-1, cols)
    bc1 = 1.0 - ADAM_B1 ** ADAM_STEP
    bc2 = 1.0 - ADAM_B2 ** ADAM_STEP

    def fn(w_, g_, m_, v_):
        m_ = ADAM_B1 * m_ + (1.0 - ADAM_B1) * g_
        v_ = ADAM_B2 * v_ + (1.0 - ADAM_B2) * (g_ * g_)
        delta = -ADAM_LR * ((m_ / bc1) / (jnp.sqrt(v_ / bc2) + ADAM_EPS) + ADAM_WD * w_)
        return (delta, m_, v_), ()

    o = (cols, F32, cols, _const(0))
    (d_, m_, v_), _ = rowwise(fn, [_full(flat(w)), _full(flat(g)), _full(flat(m)), _full(flat(v))], [],
                              [o, o, o], name=name, tm=256)
    return d_.reshape(shape), m_.reshape(shape), v_.reshape(shape)


CONV_STRIP = 256


def _shift_down(x, d):
    if d == 0:
        return x
    rows = lax.broadcasted_iota(jnp.int32, x.shape, 0)
    return jnp.where(rows >= d, pltpu.roll(x, d, 0), 0.0)


def _shift_up(x, d):
    if d == 0:
        return x
    t = x.shape[0]
    rows = lax.broadcasted_iota(jnp.int32, x.shape, 0)
    return jnp.where(rows < t - d, pltpu.roll(x, t - d, 0), 0.0)


def _conv(x, w):
    acc = None
    for j in range(CONV_WIDTH):
        term = _shift_down(x, CONV_WIDTH - 1 - j) * w[j:j + 1, :]
        acc = term if acc is None else acc + term
    return acc


def conv_fwd(proj, w, *, name):
    t = proj.shape[0]
    per = WIDTH // CONV_STRIP

    def body(x_ref, w_ref, o_ref):
        o_ref[0] = _silu(_conv(x_ref[...].astype(F32), w_ref[...]))

    return pl.pallas_call(
        body, name=name, grid=(3 * per,),
        in_specs=[pl.BlockSpec((t, CONV_STRIP), lambda j: (0, j)), pl.BlockSpec((CONV_WIDTH, CONV_STRIP), lambda j: (0, j))],
        out_specs=pl.BlockSpec((1, t, CONV_STRIP), lambda j: (j // per, 0, j % per)),
        out_shape=jax.ShapeDtypeStruct((3, t, WIDTH), F32),
        compiler_params=_params(("parallel",)),
    )(proj, w)


def conv_bwd(proj, w, dqkv, dproj, *, name):
    t = proj.shape[0]
    per = WIDTH // CONV_STRIP

    def body(x_ref, w_ref, d_ref, _, dx_ref, dw_ref):
        x, w_ = x_ref[...].astype(F32), w_ref[...]
        c = _conv(x, w_)
        s = _sigmoid(c)
        dc = d_ref[0] * (s + c * s * (1.0 - s))
        dx = None
        for j in range(CONV_WIDTH):
            d = CONV_WIDTH - 1 - j
            term = _shift_up(dc, d) * w_[j:j + 1, :]
            dx = term if dx is None else dx + term
            dw_ref[j:j + 1, :] = jnp.sum(dc * _shift_down(x, d), axis=0, keepdims=True)
        dx_ref[...] = dx.astype(dx_ref.dtype)

    return pl.pallas_call(
        body, name=name, grid=(3 * per,),
        in_specs=[pl.BlockSpec((t, CONV_STRIP), lambda j: (0, j)), pl.BlockSpec((CONV_WIDTH, CONV_STRIP), lambda j: (0, j)),
                  pl.BlockSpec((1, t, CONV_STRIP), lambda j: (j // per, 0, j % per)), pl.BlockSpec(memory_space=pl.ANY)],
        out_specs=[pl.BlockSpec((t, CONV_STRIP), lambda j: (0, j)), pl.BlockSpec((CONV_WIDTH, CONV_STRIP), lambda j: (0, j))],
        out_shape=[jax.ShapeDtypeStruct(dproj.shape, dproj.dtype), jax.ShapeDtypeStruct((CONV_WIDTH, 3 * WIDTH), F32)],
        input_output_aliases={3: 0},
        compiler_params=_params(("parallel",)),
    )(proj, w, dqkv, dproj)


def _gdn_gates(ab, a_log, dt_bias):
    a_in, b_in = ab[:HEADS], ab[HEADS:]
    g = -jnp.exp(a_log) * _softplus(a_in + dt_bias)
    return g, _sigmoid(b_in)


def gates_fwd(ab, a_log, dt_bias, *, name):
    t = ab.shape[1]

    def body(ab_ref, al_ref, dt_ref, g_ref, b_ref):
        g_ref[...], b_ref[...] = _gdn_gates(ab_ref[...], al_ref[...], dt_ref[...])

    s = jax.ShapeDtypeStruct((HEADS, t), F32)
    return pl.pallas_call(body, name=name, out_shape=[s, s], compiler_params=_params())(ab, a_log, dt_bias)


def gates_bwd(ab, a_log, dt_bias, dg, dbeta, *, name):
    t = ab.shape[1]

    def body(ab_ref, al_ref, dt_ref, dg_ref, db_ref, dab_ref, dal_ref, ddt_ref):
        _, vjp = jax.vjp(_gdn_gates, ab_ref[...], al_ref[...], dt_ref[...])
        dab_ref[...], dal_ref[...], ddt_ref[...] = vjp((dg_ref[...], db_ref[...]))

    c = jax.ShapeDtypeStruct((HEADS, 1), F32)
    return pl.pallas_call(body, name=name, out_shape=[jax.ShapeDtypeStruct((2 * HEADS, t), F32), c, c],
                          compiler_params=_params())(ab, a_log, dt_bias, dg, dbeta)


def _split3(x):
    hi = x.astype(BF16)
    r1 = x - hi.astype(F32)
    mid = r1.astype(BF16)
    lo = (r1 - mid.astype(F32)).astype(BF16)
    return hi, mid, lo


def _dot01(x, m01):
    hi, mid, lo = _split3(x)
    m01 = m01.astype(BF16)
    return _bdot(hi, m01, NN) + _bdot(mid, m01, NN) + _bdot(lo, m01, NN)


def _hdot(a, b, dims=NN):
    a_hi, b_hi = a.astype(BF16), b.astype(BF16)
    a_lo, b_lo = (a - a_hi.astype(F32)).astype(BF16), (b - b_hi.astype(F32)).astype(BF16)
    return _bdot(a_hi, b_hi, dims) + (_bdot(a_hi, b_lo, dims) + _bdot(a_lo, b_hi, dims))


def _rowsum(x):
    return jnp.sum(x, axis=1, keepdims=True)


def _colsum(x):
    return jnp.sum(x, axis=0, keepdims=True)


class _Heads:
    def __init__(self, vals):
        self.v = list(vals)

    def _bin(self, other, f):
        if isinstance(other, _Heads):
            return _Heads(f(a, b) for a, b in zip(self.v, other.v))
        return _Heads(f(a, other) for a in self.v)

    def __add__(self, o):
        return self._bin(o, lambda a, b: a + b)

    __radd__ = __add__

    def __sub__(self, o):
        return self._bin(o, lambda a, b: a - b)

    def __rsub__(self, o):
        return self._bin(o, lambda a, b: b - a)

    def __mul__(self, o):
        return self._bin(o, lambda a, b: a * b)

    __rmul__ = __mul__

    def __neg__(self):
        return _Heads(-a for a in self.v)


def _hmap(f, *args):
    n = next(len(a.v) for a in args if isinstance(a, _Heads))
    return _Heads(f(*[a.v[h] if isinstance(a, _Heads) else a for a in args]) for h in range(n))


def _inv_unit_lower(a, eye):
    p = jnp.where(eye, 1.0, 0.0) - a
    ak = a
    for _ in range(int(math.log2(CHUNK)) - 1):
        ak = _hmap(_hdot, ak, ak)
        p = p + _hmap(_hdot, p, ak)
    return p


def _gdn_chunk(qr, kr, v, grow, brow, tinv=None):
    c = CHUNK
    ri = lax.broadcasted_iota(jnp.int32, (c, c), 0)
    ci = lax.broadcasted_iota(jnp.int32, (c, c), 1)
    eye, lower, strict = ri == ci, ri >= ci, ri > ci
    where = lambda m: (lambda a: jnp.where(m, a, 0.0))
    to_col = lambda row: _hmap(lambda r: _rowsum(jnp.where(eye, jnp.broadcast_to(r, (c, c)), 0.0)), row)
    cum_row = _hmap(lambda g: _dot01(jnp.broadcast_to(g, (8, c)), ri <= ci)[0:1], grow)
    gcol, bcol = to_col(cum_row), to_col(brow)
    glast = _hmap(lambda g: _colsum(jnp.where(ri[:, 0:1] == c - 1, g, 0.0)), gcol)
    rq = _hmap(lambda a: lax.rsqrt(_rowsum(a * a) + EPS), qr)
    rk = _hmap(lambda a: lax.rsqrt(_rowsum(a * a) + EPS), kr)
    scale = HEAD_DIM ** -0.5
    qn, kn = qr * (rq * scale), kr * rk
    dec = _hmap(lambda gc, gr: jnp.where(lower, jnp.exp(jnp.minimum(gc - gr, 0.0)), 0.0), gcol, cum_row)
    kk = _hmap(lambda a: _bdot(a, a, NT), kn)
    qk = _hmap(lambda a, b: _bdot(a, b, NT), qn, kn)
    gam_col, e_col, gam_last = _hmap(jnp.exp, gcol), _hmap(jnp.exp, glast - gcol), _hmap(jnp.exp, glast)
    if tinv is None:
        tinv = _inv_unit_lower(_hmap(where(strict), bcol * kk * dec), eye)
    u = _hmap(_hdot, tinv, v * bcol)
    w = _hmap(_hdot, tinv, kn * (bcol * gam_col))
    return dict(eye=eye, lower=lower, strict=strict, gcol=gcol, bcol=bcol, rq=rq, rk=rk, qn=qn, kn=kn,
                dec=dec, kk=kk, qk=qk, gam_col=gam_col, e_col=e_col, gam_last=gam_last, tinv=tinv, u=u, w=w,
                aqk=qk * dec, qt=qn * gam_col, kt=kn * e_col, scale=scale)


def _head_cols(h):
    return slice(h * HEAD_DIM, (h + 1) * HEAD_DIM)


def _bd(dims):
    return lambda a, b: _bdot(a, b, dims)


def gdn_fwd(qkv, g4, b4, *, name, gather=()):
    t = qkv.shape[1]
    n = t // CHUNK
    d = HEAD_DIM
    heads = range(HEADS)
    ng = len(gather)

    def body(*refs):
        qkv_ref, g_ref, b_ref = refs[:3]
        o_ref, s0_ref, t_ref = refs[3 + ng:6 + ng]
        s_ref = refs[6 + 2 * ng]
        comm = _Gather(refs[6 + ng:6 + 2 * ng], *refs[7 + 2 * ng:]) if ng else None

        @pl.when(pl.program_id(0) == 0)
        def _():
            s_ref[...] = jnp.zeros_like(s_ref)
            if ng:
                comm.start()

        qr, kr, v = (_Heads(qkv_ref[j, :, _head_cols(h)] for h in heads) for j in range(3))
        z = _gdn_chunk(qr, kr, v, _Heads(g_ref[0, h] for h in heads), _Heads(b_ref[0, h] for h in heads))
        s0 = _Heads(s_ref[h] for h in heads)
        v_new = z["u"] - _hmap(_bd(NN), z["w"], s0)
        o = _hmap(_bd(NN), z["qt"], s0) + _hmap(_bd(NN), z["aqk"], v_new)
        s_new = s0 * z["gam_last"] + _hmap(_bd(TN), z["kt"], v_new)
        for h in heads:
            s0_ref[0, h] = s0.v[h]
            t_ref[0, h] = z["tinv"].v[h]
            o_ref[:, _head_cols(h)] = o.v[h]
            s_ref[h] = s_new.v[h]

        if ng:
            @pl.when(pl.program_id(0) == n - 1)
            def _():
                comm.finish()

    gspec = pl.BlockSpec((1, HEADS, 1, CHUNK), lambda i: (i, 0, 0, 0))
    return pl.pallas_call(
        body, name=name, grid=(n,),
        in_specs=[pl.BlockSpec((3, CHUNK, WIDTH), lambda i: (0, i, 0)), gspec, gspec] + [ANY] * ng,
        out_specs=[pl.BlockSpec((CHUNK, WIDTH), lambda i: (i, 0)),
                   pl.BlockSpec((1, HEADS, d, d), lambda i: (i, 0, 0, 0)),
                   pl.BlockSpec((1, HEADS, CHUNK, CHUNK), lambda i: (i, 0, 0, 0))] + [ANY] * ng,
        out_shape=[jax.ShapeDtypeStruct((t, WIDTH), F32), jax.ShapeDtypeStruct((n, HEADS, d, d), F32),
                   jax.ShapeDtypeStruct((n, HEADS, CHUNK, CHUNK), F32)]
        + [jax.ShapeDtypeStruct(s.shape, s.dtype) for s in gather],
        input_output_aliases={3 + b: 3 + b for b in range(ng)},
        scratch_shapes=[pltpu.VMEM((HEADS, d, d), F32)] + (_dma_sems(6 * ng) if ng else []),
        compiler_params=_params(("arbitrary",)),
    )(qkv, g4, b4, *gather)


def gdn_bwd(qkv, g4, b4, s0_all, tinv_all, do, *, name, exchange=()):
    t = qkv.shape[1]
    n = t // CHUNK
    d = HEAD_DIM
    c = CHUNK
    heads = range(HEADS)

    ne = len(exchange)

    def body(*refs):
        qkv_ref, g_ref, b_ref, s0_ref, t_ref, do_ref = refs[:6]
        dqkv_ref, dg_ref, db_ref = refs[6 + ne:9 + ne]
        ds_ref = refs[9 + 2 * ne]
        comm = _Exchange(refs[6:6 + ne], refs[9 + ne:9 + 2 * ne], *refs[10 + 2 * ne:]) if ne else None

        @pl.when(pl.program_id(0) == 0)
        def _():
            ds_ref[...] = jnp.zeros_like(ds_ref)
            if ne:
                comm.start()

        qr, kr, v = (_Heads(qkv_ref[j, :, _head_cols(h)] for h in heads) for j in range(3))
        z = _gdn_chunk(qr, kr, v, _Heads(g_ref[0, h] for h in heads), _Heads(b_ref[0, h] for h in heads),
                       tinv=_Heads(t_ref[0, h] for h in heads))
        s0 = _Heads(s0_ref[0, h] for h in heads)
        ds = _Heads(ds_ref[h] for h in heads)
        dout = _Heads(do_ref[:, _head_cols(h)] for h in heads)
        qn, kn, u, w, dec, kk, qk = z["qn"], z["kn"], z["u"], z["w"], z["dec"], z["kk"], z["qk"]
        bcol, gam_col, e_col, gam_last = z["bcol"], z["gam_col"], z["e_col"], z["gam_last"]
        low = lambda a: jnp.where(z["lower"], a, 0.0)
        strict = lambda a: jnp.where(z["strict"], a, 0.0)
        rowsum = lambda a: _hmap(_rowsum, a)
        colsum = lambda a: _hmap(_colsum, a)
        v_new = u - _hmap(_bd(NN), w, s0)
        dv_new = _hmap(_bd(TN), z["aqk"], dout) + _hmap(_bd(NN), z["kt"], ds)
        daqk = _hmap(low, _hmap(_bd(NT), dout, v_new))
        dqt = _hmap(_bd(NT), dout, s0)
        dkt = _hmap(_bd(NT), v_new, ds)
        dgam_last = _hmap(lambda a, b: jnp.sum(a * b, keepdims=True), ds, s0)
        ds_new = _hmap(_bd(TN), z["qt"], dout) + ds * gam_last - _hmap(_bd(TN), w, dv_new)
        dw = -_hmap(_bd(NT), dv_new, s0)
        hd_t = lambda a, b: _hdot(a, b, TN)
        dru = _hmap(hd_t, z["tinv"], dv_new)
        drw = _hmap(hd_t, z["tinv"], dw)
        dal = -_hmap(strict, _hmap(_bd(NT), dru, u) + _hmap(_bd(NT), drw, w))
        t1 = dal * kk * dec
        dkk = dal * bcol * dec
        ddec = dal * bcol * kk + daqk * qk
        dqk = daqk * dec
        s_w = rowsum(drw * kn)
        dbeta_col = rowsum(t1) + rowsum(dru * v) + gam_col * s_w
        dkn = (drw * (bcol * gam_col) + _hmap(_bd(NN), dkk, kn) + _hmap(_bd(TN), dkk, kn) + _hmap(_bd(TN), dqk, qn)
               + dkt * e_col)
        dqn = _hmap(_bd(NN), dqk, kn) + dqt * gam_col
        e_mat = ddec * dec
        de_col = rowsum(dkt * kn)
        diag_of_colsum = rowsum(_hmap(lambda a: jnp.where(z["eye"], jnp.broadcast_to(_colsum(a), (c, c)), 0.0), e_mat))
        dg_cum = rowsum(e_mat) + (bcol * s_w + rowsum(dqt * qn)) * gam_col - de_col * e_col - diag_of_colsum
        dg_last = colsum(de_col * e_col) + dgam_last * gam_last
        dg = colsum(_hmap(lambda a: jnp.where(z["lower"], a, 0.0), dg_cum)) + dg_last
        dbeta = colsum(_hmap(lambda a: jnp.where(z["eye"], a, 0.0), dbeta_col))
        rq, rk = z["rq"], z["rk"]
        dqr = z["scale"] * (rq * dqn - qr * (rq * rq * rq) * rowsum(qr * dqn))
        dkr = rk * dkn - kr * (rk * rk * rk) * rowsum(kr * dkn)
        dv = dru * bcol
        for h in heads:
            ds_ref[h] = ds_new.v[h]
            dg_ref[0, h] = dg.v[h]
            db_ref[0, h] = dbeta.v[h]
            dqkv_ref[0, :, _head_cols(h)] = dqr.v[h]
            dqkv_ref[1, :, _head_cols(h)] = dkr.v[h]
            dqkv_ref[2, :, _head_cols(h)] = dv.v[h]

        if ne:
            @pl.when(pl.program_id(0) == n - 1)
            def _():
                comm.finish()

    rev = lambda i: n - 1 - i
    gspec = pl.BlockSpec((1, HEADS, 1, CHUNK), lambda i: (rev(i), 0, 0, 0))
    return pl.pallas_call(
        body, name=name, grid=(n,),
        in_specs=[pl.BlockSpec((3, CHUNK, WIDTH), lambda i: (0, rev(i), 0)), gspec, gspec,
                  pl.BlockSpec((1, HEADS, d, d), lambda i: (rev(i), 0, 0, 0)),
                  pl.BlockSpec((1, HEADS, CHUNK, CHUNK), lambda i: (rev(i), 0, 0, 0)),
                  pl.BlockSpec((CHUNK, WIDTH), lambda i: (rev(i), 0))] + [ANY] * ne,
        out_specs=[pl.BlockSpec((3, CHUNK, WIDTH), lambda i: (0, rev(i), 0)), gspec, gspec] + [ANY] * ne,
        out_shape=[jax.ShapeDtypeStruct((3, t, WIDTH), F32), jax.ShapeDtypeStruct((n, HEADS, 1, CHUNK), F32),
                   jax.ShapeDtypeStruct((n, HEADS, 1, CHUNK), F32)] + _exchange_shapes(exchange),
        scratch_shapes=[pltpu.VMEM((HEADS, d, d), F32)] + (_dma_sems(3 * ne) if ne else []),
        compiler_params=_params(("arbitrary",)),
    )(qkv, g4, b4, s0_all, tinv_all, do, *exchange)


SB_BLOCK = 256


def _dot01_2(x, m01):
    hi = x.astype(BF16)
    lo = (x - hi.astype(F32)).astype(BF16)
    return _bdot(hi, m01, NN) + _bdot(lo, m01, NN)


SB_HEADS = 2


def _sb_weights(q, kb, carry, mask, upper):
    z = _hmap(_bd(NT), q, kb)
    ls = _hmap(lambda z_: jnp.minimum(z_, 0.0) - jnp.log(1.0 + jnp.exp(-jnp.abs(z_))), z)
    ln = _hmap(lambda l_, z_: jnp.where(mask, l_ - z_, 0.0), ls, z)
    suffix = _hmap(lambda l_: _dot01_2(l_, upper), ln)
    a = _hmap(lambda l_, s_, c_: jnp.where(mask, jnp.exp(l_ + s_ + c_), 0.0), ls, suffix, carry)
    return z, ln, a


SB_DEAD = -105.0


def _sb_alive(s, i, carries):
    top = jnp.max(carries[0])
    for c in carries[1:]:
        top = jnp.maximum(top, jnp.max(c))
    return (s <= i) & (top > SB_DEAD)


def _sb_masks(i, jb, blk):
    ri = lax.broadcasted_iota(jnp.int32, (blk, blk), 0)
    ci = lax.broadcasted_iota(jnp.int32, (blk, blk), 1)
    return (jb * blk + ci) < (i * blk + ri)


def sb_fwd(q, k, v, *, name, gather=()):
    t = q.shape[0]
    blk = min(SB_BLOCK, t)
    d = HEAD_DIM
    hs = range(SB_HEADS)
    ng = len(gather)
    groups, nb = HEADS // SB_HEADS, t // blk

    def body(*refs):
        q_ref, k_ref, v_ref = refs[:3]
        o_ref = refs[3 + ng]
        comm = _Gather(refs[4 + ng:4 + 2 * ng], *refs[4 + 2 * ng:]) if ng else None
        i = pl.program_id(1)
        if ng:
            @pl.when((pl.program_id(0) == 0) & (i == 0))
            def _():
                comm.start()

        qb = _Heads(q_ref[:, _head_cols(h)] for h in hs)
        ri = lax.broadcasted_iota(jnp.int32, (blk, blk), 0)
        ci = lax.broadcasted_iota(jnp.int32, (blk, blk), 1)
        upper = (ri > ci).astype(BF16)

        def step(state):
            s, cs, accs = state
            jb = i - s
            rows = pl.ds(pl.multiple_of(jb * blk, blk), blk)
            kb = _Heads(k_ref[rows, _head_cols(h)] for h in hs)
            vb = _Heads(v_ref[rows, _head_cols(h)] for h in hs)
            _, ln, a = _sb_weights(qb, kb, _Heads(cs), _sb_masks(i, jb, blk), upper)
            cs = _Heads(cs) + _hmap(_rowsum, ln)
            accs = _Heads(accs) + _hmap(_bd(NN), a, vb)
            return s + 1, tuple(cs.v), tuple(accs.v)

        init = (jnp.int32(0), tuple(jnp.zeros((blk, 1), F32) for _ in hs), tuple(jnp.zeros((blk, d), F32) for _ in hs))
        _, _, accs = lax.while_loop(lambda st: _sb_alive(st[0], i, st[1]), step, init)
        for h in hs:
            o_ref[:, _head_cols(h)] = accs[h].astype(o_ref.dtype)

        if ng:
            @pl.when((pl.program_id(0) == groups - 1) & (i == nb - 1))
            def _():
                comm.finish()

    qspec = pl.BlockSpec((blk, SB_HEADS * d), lambda g, i: (i, g))
    kspec = pl.BlockSpec((t, SB_HEADS * d), lambda g, i: (0, g))
    return pl.pallas_call(
        body, name=name, grid=(groups, nb), in_specs=[qspec, kspec, kspec] + [ANY] * ng,
        out_specs=[qspec] + [ANY] * ng,
        out_shape=[jax.ShapeDtypeStruct((t, WIDTH), BF16)] + [jax.ShapeDtypeStruct(s.shape, s.dtype) for s in gather],
        input_output_aliases={3 + b: 1 + b for b in range(ng)},
        scratch_shapes=_dma_sems(6 * ng) if ng else [],
        compiler_params=_params(("arbitrary", "arbitrary") if ng else ("parallel", "arbitrary")),
    )(q, k, v, *gather)


def sb_bwd(q, k, v, do, *, name, exchange=()):
    t = q.shape[0]
    blk = min(SB_BLOCK, t)
    d = HEAD_DIM
    nb = t // blk
    hs = range(SB_HEADS)
    ne = len(exchange)
    groups = HEADS // SB_HEADS

    def body(*refs):
        q_ref, k_ref, v_ref, do_ref = refs[:4]
        dq_ref, dk_ref, dv_ref = refs[4 + ne:7 + ne]
        p_buf, z_buf = refs[7 + 2 * ne:9 + 2 * ne]
        comm = _Exchange(refs[4:4 + ne], refs[7 + ne:7 + 2 * ne], *refs[9 + 2 * ne:]) if ne else None
        i = pl.program_id(1)
        if ne:
            @pl.when((pl.program_id(0) == 0) & (i == 0))
            def _():
                comm.start()

        @pl.when(i == 0)
        def _():
            dk_ref[...] = jnp.zeros_like(dk_ref)
            dv_ref[...] = jnp.zeros_like(dv_ref)

        qb = _Heads(q_ref[:, _head_cols(h)] for h in hs)
        dob = _Heads(do_ref[:, _head_cols(h)] for h in hs)
        ri = lax.broadcasted_iota(jnp.int32, (blk, blk), 0)
        ci = lax.broadcasted_iota(jnp.int32, (blk, blk), 1)
        upper = (ri > ci).astype(BF16)
        lower = (ri < ci).astype(BF16)

        def right_to_left(state):
            s, cs = state
            jb = i - s
            rows = pl.ds(pl.multiple_of(jb * blk, blk), blk)
            kb = _Heads(k_ref[rows, _head_cols(h)] for h in hs)
            vb = _Heads(v_ref[rows, _head_cols(h)] for h in hs)
            z, ln, a = _sb_weights(qb, kb, _Heads(cs), _sb_masks(i, jb, blk), upper)
            p = a * _hmap(_bd(NT), dob, vb)
            dv = _hmap(_bd(TN), a, dob)
            for h in hs:
                p_buf[h, jb] = p.v[h]
                z_buf[h, jb] = z.v[h]
                dv_ref[rows, _head_cols(h)] += dv.v[h]
            return s + 1, tuple((_Heads(cs) + _hmap(_rowsum, ln)).v)

        n_done, _ = lax.while_loop(lambda st: _sb_alive(st[0], i, st[1]), right_to_left,
                                   (jnp.int32(0), tuple(jnp.zeros((blk, 1), F32) for _ in hs)))

        def left_to_right(jb, carry):
            cps, dqs = carry
            rows = pl.ds(pl.multiple_of(jb * blk, blk), blk)
            mask = _sb_masks(i, jb, blk)
            kb = _Heads(k_ref[rows, _head_cols(h)] for h in hs)
            p = _Heads(p_buf[h, jb] for h in hs)
            sg = _hmap(_sigmoid, _Heads(z_buf[h, jb] for h in hs))
            prefix = _hmap(lambda a: _dot01_2(a, lower), p) + _Heads(cps)
            dz = _hmap(lambda a: jnp.where(mask, a, 0.0), p * (1.0 - sg) - sg * prefix)
            dk = _hmap(_bd(TN), dz, qb)
            for h in hs:
                dk_ref[rows, _head_cols(h)] += dk.v[h]
            return tuple((_Heads(cps) + _hmap(_rowsum, p)).v), tuple((_Heads(dqs) + _hmap(_bd(NN), dz, kb)).v)

        _, dqs = lax.fori_loop(i + 1 - n_done, i + 1, left_to_right,
                               (tuple(jnp.zeros((blk, 1), F32) for _ in hs), tuple(jnp.zeros((blk, d), F32) for _ in hs)))
        for h in hs:
            dq_ref[:, _head_cols(h)] = dqs[h]

        if ne:
            @pl.when((pl.program_id(0) == groups - 1) & (i == nb - 1))
            def _():
                comm.finish()

    qspec = pl.BlockSpec((blk, SB_HEADS * d), lambda g, i: (i, g))
    kspec = pl.BlockSpec((t, SB_HEADS * d), lambda g, i: (0, g))
    s = jax.ShapeDtypeStruct((t, WIDTH), F32)
    buf = pltpu.VMEM((SB_HEADS, nb, blk, blk), F32)
    return pl.pallas_call(
        body, name=name, grid=(groups, nb), in_specs=[qspec, kspec, kspec, qspec] + [ANY] * ne,
        out_specs=[qspec, kspec, kspec] + [ANY] * ne, out_shape=[s, s, s] + _exchange_shapes(exchange),
        scratch_shapes=[buf, buf] + (_dma_sems(3 * ne) if ne else []),
        compiler_params=_params(("arbitrary", "arbitrary") if ne else ("parallel", "arbitrary")),
    )(q, k, v, do, *exchange)


PACK_COLS = 1024
ANY = pl.BlockSpec(memory_space=pl.ANY)


def _mesh_pos():
    return lax.axis_index("x"), lax.axis_index("y"), lax.axis_index("c")


def _other_chips(x, y):
    return [(1 - x, y), (x, 1 - y), (1 - x, 1 - y)]


def _dma_sems(n):
    return [pltpu.SemaphoreType.DMA((n,)), pltpu.SemaphoreType.DMA((n,))]


class _Gather:
    def __init__(self, o_refs, send_sems, recv_sems):
        self.o_refs, self.send_sems, self.recv_sems = o_refs, send_sems, recv_sems

    def _copy(self, b, k, chip, hf, to):
        rows = self.o_refs[b].at[chip, hf]
        return pltpu.make_async_remote_copy(src_ref=rows, dst_ref=rows, send_sem=self.send_sems.at[6 * b + k],
                                            recv_sem=self.recv_sems.at[6 * b + k], device_id=to, device_id_type=MESH)

    def start(self):
        x, y, c = _mesh_pos()
        for b in range(len(self.o_refs)):
            for k, (cx, cy) in enumerate(_other_chips(x, y)):
                self._copy(b, k, 2 * x + y, c, (cx, cy, c)).start()

    def finish(self):
        x, y, c = _mesh_pos()
        chips = _other_chips(x, y)
        for b in range(len(self.o_refs)):
            for k, (cx, cy) in enumerate(chips):
                self._copy(b, k, 2 * cx + cy, c, (x, y, c)).wait_recv()
                self._copy(b, 3 + k, 2 * cx + cy, c, (x, y, 1 - c)).start()
        for b in range(len(self.o_refs)):
            for k, (cx, cy) in enumerate(chips):
                self._copy(b, 3 + k, 2 * cx + cy, 1 - c, (x, y, c)).wait_recv()
                self._copy(b, k, 2 * x + y, c, (cx, cy, c)).wait_send()
                self._copy(b, 3 + k, 2 * cx + cy, c, (x, y, 1 - c)).wait_send()


def all_gather_chips(slots, *, name):
    nb = len(slots)

    def body(*refs):
        g = _Gather(refs[nb:2 * nb], *refs[2 * nb:])
        g.start()
        g.finish()

    return pl.pallas_call(
        body, name=name, in_specs=[ANY] * nb, out_specs=[ANY] * nb, input_output_aliases={b: b for b in range(nb)},
        out_shape=[jax.ShapeDtypeStruct(s.shape, s.dtype) for s in slots], scratch_shapes=_dma_sems(6 * nb),
    )(*slots)


def sibling_swap(gs, *, name):
    nb = len(gs)

    def body(*refs):
        g_refs, o_refs, (send_sems, recv_sems) = refs[:nb], refs[nb:2 * nb], refs[2 * nb:]
        x, y, c = _mesh_pos()
        cps = [pltpu.make_async_remote_copy(src_ref=g_refs[b].at[j, 1 - c], dst_ref=o_refs[b].at[j],
                                            send_sem=send_sems.at[N_CHIPS * b + j], recv_sem=recv_sems.at[N_CHIPS * b + j],
                                            device_id=(x, y, 1 - c), device_id_type=MESH)
               for b in range(nb) for j in range(N_CHIPS)]
        for cp in cps:
            cp.start()
        for cp in cps:
            cp.wait()

    return pl.pallas_call(
        body, name=name, in_specs=[ANY] * nb, out_specs=[ANY] * nb,
        out_shape=[jax.ShapeDtypeStruct((g.shape[0],) + g.shape[2:], g.dtype) for g in gs],
        scratch_shapes=_dma_sems(N_CHIPS * nb),
    )(*gs)


class _Exchange:
    def __init__(self, s_refs, o_refs, send_sems, recv_sems):
        self.s_refs, self.o_refs, self.send_sems, self.recv_sems = s_refs, o_refs, send_sems, recv_sems

    def _copies(self):
        x, y, c = _mesh_pos()
        return [pltpu.make_async_remote_copy(src_ref=self.s_refs[b].at[2 * cx + cy], dst_ref=self.o_refs[b].at[k],
                                             send_sem=self.send_sems.at[3 * b + k], recv_sem=self.recv_sems.at[3 * b + k],
                                             device_id=(cx, cy, c), device_id_type=MESH)
                for b in range(len(self.s_refs)) for k, (cx, cy) in enumerate(_other_chips(x, y))]

    def start(self):
        for cp in self._copies():
            cp.start()

    def finish(self):
        for cp in self._copies():
            cp.wait()


def _exchange_shapes(s1s):
    return [jax.ShapeDtypeStruct((3,) + s.shape[1:], s.dtype) for s in s1s]


def chip_exchange(s1s, *, name):
    nb = len(s1s)

    def body(*refs):
        comm = _Exchange(refs[:nb], refs[nb:2 * nb], *refs[2 * nb:])
        comm.start()
        comm.finish()

    return pl.pallas_call(
        body, name=name, in_specs=[ANY] * nb, out_specs=[ANY] * nb, out_shape=_exchange_shapes(s1s),
        scratch_shapes=_dma_sems(3 * nb),
    )(*s1s)


def sibling_merge(halves, *, name):
    nb = len(halves)

    def body(*refs):
        o_refs, (send_sems, recv_sems) = refs[nb:2 * nb], refs[2 * nb:]
        x, y, c = _mesh_pos()
        cps = [pltpu.make_async_remote_copy(src_ref=o_refs[b].at[c], dst_ref=o_refs[b].at[c], send_sem=send_sems.at[b],
                                            recv_sem=recv_sems.at[b], device_id=(x, y, 1 - c), device_id_type=MESH)
               for b in range(nb)]
        for cp in cps:
            cp.start()
        for cp in cps:
            cp.wait()

    return pl.pallas_call(
        body, name=name, in_specs=[ANY] * nb, out_specs=[ANY] * nb, input_output_aliases={b: b for b in range(nb)},
        out_shape=[jax.ShapeDtypeStruct(h.shape, h.dtype) for h in halves], scratch_shapes=_dma_sems(nb),
    )(*halves)


def all_reduce_small(buf, *, name):
    n_dev = 8

    def body(b_ref, o_ref, recv_buf, send_sems, recv_sems):
        x, y, c = _mesh_pos()
        me = 4 * x + 2 * y + c
        pos = lambda t: (t // 4, (t // 2) % 2, t % 2)

        def copy(t, slot):
            return pltpu.make_async_remote_copy(src_ref=b_ref, dst_ref=recv_buf.at[slot], send_sem=send_sems.at[t],
                                                recv_sem=recv_sems.at[slot], device_id=pos(t), device_id_type=MESH)

        for t in range(n_dev):
            @pl.when(t != me)
            def _(t=t):
                copy(t, me).start()

        recv_buf[me] = b_ref[...]
        for t in range(n_dev):
            @pl.when(t != me)
            def _(t=t):
                copy(t, t).wait_recv()
                copy(t, me).wait_send()

        acc = recv_buf[0]
        for t in range(1, n_dev):
            acc = acc + recv_buf[t]
        o_ref[...] = acc

    return pl.pallas_call(
        body, name=name, out_shape=jax.ShapeDtypeStruct(buf.shape, F32),
        in_specs=[pl.BlockSpec(memory_space=pltpu.VMEM)], out_specs=pl.BlockSpec(memory_space=pltpu.VMEM),
        scratch_shapes=[pltpu.VMEM((n_dev,) + buf.shape, F32), pltpu.SemaphoreType.DMA((n_dev,)),
                        pltpu.SemaphoreType.DMA((n_dev,))],
    )(buf)


REDUCE_ROWS = (512, 384, 256, 128)


def add_selected(sel, a5, b, *, name):
    n, _, rh, w = a5.shape
    tr = _pick(rh, REDUCE_ROWS)

    def body(sel_ref, a_ref, b_ref, o_ref, ob_ref):
        s = a_ref[...] + b_ref[...]
        o_ref[...] = s
        ob_ref[...] = s.astype(BF16)

    blk = pl.BlockSpec((None, tr, w), lambda j, i, s: (j, i, 0))
    return pl.pallas_call(
        body, name=name,
        grid_spec=pltpu.PrefetchScalarGridSpec(
            num_scalar_prefetch=1, grid=(n, rh // tr),
            in_specs=[pl.BlockSpec((None, None, tr, w), lambda j, i, s: (j, s[0], i, 0)), blk], out_specs=[blk, blk]),
        out_shape=[jax.ShapeDtypeStruct((n, rh, w), F32), jax.ShapeDtypeStruct((n, rh, w), BF16)],
        compiler_params=_params(("arbitrary", "arbitrary")),
    )(sel, a5, b)


def add_chip_sums(sel, s1, b2, *, name):
    _, rh, w = s1.shape
    tr = _pick(rh, REDUCE_ROWS)

    def body(sel_ref, s_ref, b_ref, o_ref):
        o_ref[...] = ((s_ref[...] + b_ref[0].astype(F32)) + b_ref[1].astype(F32)) + b_ref[2].astype(F32)

    return pl.pallas_call(
        body, name=name,
        grid_spec=pltpu.PrefetchScalarGridSpec(
            num_scalar_prefetch=1, grid=(rh // tr,),
            in_specs=[pl.BlockSpec((None, tr, w), lambda i, s: (s[0], i, 0)), pl.BlockSpec((3, tr, w), lambda i, s: (0, i, 0))],
            out_specs=pl.BlockSpec((None, tr, w), lambda i, s: (s[1], i, 0))),
        out_shape=jax.ShapeDtypeStruct((2, rh, w), F32),
        compiler_params=_params(("arbitrary",)),
    )(sel, s1, b2)


BIG = (("gdn_w_out", 1), ("sb_w_q", 1), ("sb_w_out", 1), ("ffn_w_out", 1), ("ple_w_gate", 1), ("w_kv", 1),
       ("ple_w_proj", 2))
SMALL = ("ln_mix", "ln_ffn", "ln_ple", "gdn_a_log", "gdn_dt_bias", "gdn_norm", "kv_norm", "k_norm", "sb_q_norm")
WEIGHTS = ("ln_mix", "ln_ffn", "ln_ple", "gdn_w_in", "gdn_conv", "gdn_a_log", "gdn_dt_bias", "gdn_norm", "gdn_w_out",
           "kv_norm", "w_kv", "k_norm", "sb_w_q", "sb_q_norm", "sb_w_out", "ffn_w_in", "ffn_w_out", "ple_w_proj",
           "ple_w_gate")
PACK_ALIGN = 256


ROW_TILE = 16


def _rows_of(shape, tile=ROW_TILE):
    return -(-math.prod(shape) // (PACK_COLS * tile)) * tile


WEIGHT_ALIGN = 32


def _pack_rows(arrs, lead, align=PACK_ALIGN):
    parts = []
    for a in arrs:
        if a.shape[-1] == PACK_COLS:
            parts.append(a.reshape(lead + (-1, PACK_COLS)))
            continue
        flat = a.reshape(lead + (-1,))
        pad = _rows_of(a.shape[len(lead):]) * PACK_COLS - flat.shape[-1]
        if pad:
            flat = jnp.pad(flat, [(0, 0)] * len(lead) + [(0, pad)])
        parts.append(flat.reshape(lead + (-1, PACK_COLS)))
    rows = sum(q.shape[len(lead)] for q in parts)
    filler = -rows % align
    if filler:
        parts.append(jnp.zeros(lead + (filler, PACK_COLS), parts[0].dtype))
    return jnp.concatenate(parts, axis=len(lead))


def _own_slot(buf, chip):
    mine = lax.broadcasted_iota(jnp.int32, (N_CHIPS, 1, 1), 0) == chip
    slots = jnp.where(mine, buf[None], jnp.zeros((), buf.dtype))
    return slots.reshape(N_CHIPS, 2, buf.shape[0] // 2, buf.shape[1])


def _unpack_rows(buf, shapes, lead):
    out, r0 = [], 0
    for s in shapes:
        rows = _rows_of(s)
        flat = buf[(slice(None),) * len(lead) + (slice(r0, r0 + rows),)].reshape(lead + (-1,))
        out.append(flat[..., :math.prod(s)].reshape(lead + tuple(s)))
        r0 += rows
    return out


def _unshard(g, axis):
    g = jnp.moveaxis(g, 0, axis)
    s = g.shape
    return g.reshape(s[:axis] + (s[axis] * s[axis + 1],) + s[axis + 2:])


def _shard(full, axis):
    s = full.shape
    return jnp.moveaxis(full.reshape(s[:axis] + (N_CHIPS, s[axis] // N_CHIPS) + s[axis + 1:]), axis, 0)


def _to4(a):
    return a.reshape(HEADS, -1, 1, CHUNK).transpose(1, 0, 2, 3)


def _from4(a):
    return a.transpose(1, 0, 2, 3).reshape(HEADS, -1)


def _row(vec):
    flat = vec.reshape(-1)
    rows = _rows_of(flat.shape, 1)
    return jnp.pad(flat, (0, rows * PACK_COLS - flat.shape[0])).reshape(rows, PACK_COLS)


def kernel(x, p, ln_mix, ln_ffn, ln_ple, gdn_w_in, gdn_conv, gdn_a_log, gdn_dt_bias, gdn_norm, gdn_w_out, kv_norm, w_kv, k_norm, sb_w_q, sb_q_norm, sb_w_out, ffn_w_in, ffn_w_out, ple_w_proj, ple_w_gate, loss_target, m_ln_mix, m_ln_ffn, m_ln_ple, m_gdn_w_in, m_gdn_conv, m_gdn_a_log, m_gdn_dt_bias, m_gdn_norm, m_gdn_w_out, m_kv_norm, m_w_kv, m_k_norm, m_sb_w_q, m_sb_q_norm, m_sb_w_out, m_ffn_w_in, m_ffn_w_out, m_ple_w_proj, m_ple_w_gate, v_ln_mix, v_ln_ffn, v_ln_ple, v_gdn_w_in, v_gdn_conv, v_gdn_a_log, v_gdn_dt_bias, v_gdn_norm, v_gdn_w_out, v_kv_norm, v_w_kv, v_k_norm, v_sb_w_q, v_sb_q_norm, v_sb_w_out, v_ffn_w_in, v_ffn_w_out, v_ple_w_proj, v_ple_w_gate):
    w = dict(ln_mix=ln_mix, ln_ffn=ln_ffn, ln_ple=ln_ple, gdn_w_in=gdn_w_in, gdn_conv=gdn_conv, gdn_a_log=gdn_a_log,
             gdn_dt_bias=gdn_dt_bias, gdn_norm=gdn_norm, gdn_w_out=gdn_w_out, kv_norm=kv_norm, w_kv=w_kv, k_norm=k_norm,
             sb_w_q=sb_w_q, sb_q_norm=sb_q_norm, sb_w_out=sb_w_out, ffn_w_in=ffn_w_in, ffn_w_out=ffn_w_out,
             ple_w_proj=ple_w_proj, ple_w_gate=ple_w_gate)
    mom1 = dict(ln_mix=m_ln_mix, ln_ffn=m_ln_ffn, ln_ple=m_ln_ple, gdn_w_in=m_gdn_w_in, gdn_conv=m_gdn_conv,
                gdn_a_log=m_gdn_a_log, gdn_dt_bias=m_gdn_dt_bias, gdn_norm=m_gdn_norm, gdn_w_out=m_gdn_w_out,
                kv_norm=m_kv_norm, w_kv=m_w_kv, k_norm=m_k_norm, sb_w_q=m_sb_w_q, sb_q_norm=m_sb_q_norm,
                sb_w_out=m_sb_w_out, ffn_w_in=m_ffn_w_in, ffn_w_out=m_ffn_w_out, ple_w_proj=m_ple_w_proj,
                ple_w_gate=m_ple_w_gate)
    mom2 = dict(ln_mix=v_ln_mix, ln_ffn=v_ln_ffn, ln_ple=v_ln_ple, gdn_w_in=v_gdn_w_in, gdn_conv=v_gdn_conv,
                gdn_a_log=v_gdn_a_log, gdn_dt_bias=v_gdn_dt_bias, gdn_norm=v_gdn_norm, gdn_w_out=v_gdn_w_out,
                kv_norm=v_kv_norm, w_kv=v_w_kv, k_norm=v_k_norm, sb_w_q=v_sb_w_q, sb_q_norm=v_sb_q_norm,
                sb_w_out=v_sb_w_out, ffn_w_in=v_ffn_w_in, ffn_w_out=v_ffn_w_out, ple_w_proj=v_ple_w_proj,
                ple_w_gate=v_ple_w_gate)
    depth = ln_mix.shape[0]
    n_a = gdn_w_in.shape[0]
    xi, yi, ci = _mesh_pos()
    chip = 2 * xi + yi
    sel_c = jnp.reshape(ci, (1,)).astype(jnp.int32)
    sel_chip = jnp.stack([chip, ci]).astype(jnp.int32)
    h = x[0]
    tgt = loss_target[0]
    t = h.shape[0]

    shard_shapes = [w[n].shape for n, _ in BIG]

    def layer_items(i):
        if i < n_a:
            items = [("gdn_w_out", i, 0), ("ffn_w_out", i, 0), ("ple_w_gate", i, 0), ("ple_w_proj", i, 1)]
            return items + ([("w_kv", None, 1)] if i == n_a - 1 else [])
        j = i - n_a
        return [("sb_w_q", j, 0), ("sb_w_out", j, 0), ("ffn_w_out", i, 0), ("ple_w_gate", i, 0), ("ple_w_proj", i, 1)]

    def layer_shards(i):
        return [w[n] if idx is None else w[n][idx] for n, idx, _ in layer_items(i)]

    def layer_slots(i):
        packed = _pack_rows([q.astype(BF16) for q in layer_shards(i)], (), align=WEIGHT_ALIGN)
        own = [packed, ffn_w_in[i].astype(BF16)] + ([gdn_w_in[i].astype(BF16)] if i < n_a else [])
        return [_own_slot(b, chip) for b in own]

    def layer_weights(i, got):
        parts = _unpack_rows(got[0].reshape(N_CHIPS, -1, PACK_COLS), [q.shape for q in layer_shards(i)], (N_CHIPS,))
        out = {n: _unshard(g, ax) for (n, _, ax), g in zip(layer_items(i), parts)}
        out["ffn_w_in"] = got[1].reshape((N_CHIPS, 1) + ffn_w_in.shape[1:])
        if i < n_a:
            out["gdn_w_in"] = _unshard(got[2].reshape((N_CHIPS,) + gdn_w_in.shape[1:]), 1)
        return out

    slots = [layer_slots(i) for i in range(depth)]
    wl = [layer_weights(0, all_gather_chips(slots[0], name="all_gather_weights"))]
    conv_rows = _rows_of(gdn_conv.shape, 1)
    small_rows = sum(_rows_of(w[n].shape, 1) for n in SMALL)
    buf_rows = -(-(small_rows + N_CHIPS * conv_rows) // 8) * 8
    conv_buf = jnp.zeros((buf_rows, PACK_COLS), F32)
    conv_buf = lax.dynamic_update_slice(conv_buf, _row(gdn_conv) * (ci == 0).astype(F32), (chip * conv_rows, 0))
    conv_all = all_reduce_small(conv_buf, name="all_reduce_small")[:N_CHIPS * conv_rows]
    conv_full = _unshard(conv_all.reshape(N_CHIPS, -1)[:, :math.prod(gdn_conv.shape)].reshape((N_CHIPS,) + gdn_conv.shape), 2)

    def resid_rms(r, res, g):
        out = r + res
        return out, _rms(out, g)

    def ple_rms(r, res, emb, g):
        out = res + emb * _sigmoid(r)
        return r, out, (None if g is None else _rms(out, g))

    hn_next = None
    saved = []
    k_sh = v_sh = None
    mid = None
    for i in range(depth):
        s = dict(h0=h)
        wi = wl[i]
        nxt = slots[i + 1] if i + 1 < depth else ()
        s["hn"] = hn = rms_fwd(h, ln_mix[i:i + 1], name="rms_fwd") if hn_next is None else hn_next
        if i < n_a:
            w_in = wi["gdn_w_in"]
            s["w_m"], s["w_abt"] = w_in[:, :4 * WIDTH], w_in[:, 4 * WIDTH:].T
            s["proj"] = proj = matmul(hn, s["w_m"], "nn", out_dtype=BF16, name="mm_gdn_in")
            s["ab"] = ab = matmul(s["w_abt"], hn, "nt", name="mm_gdn_ab")
            s["a_log"], s["dt"] = gdn_a_log[i][:, None], gdn_dt_bias[i][:, None]
            g8, b8 = gates_fwd(ab, s["a_log"], s["dt"], name="gates_fwd")
            s["g4"], s["b4"] = _to4(g8), _to4(b8)
            s["qkv"] = qkv = conv_fwd(proj, conv_full[i], name="conv_fwd")
            s["o"], s["s0"], s["tinv"], *got = gdn_fwd(qkv, s["g4"], s["b4"], gather=nxt, name="gdn_fwd")
            s["y"] = mixed = gatenorm_fwd(s["o"], proj, gdn_norm[i:i + 1], name="gatenorm_fwd")
            w_mix_out = wi["gdn_w_out"]
        else:
            j = i - n_a
            s["qraw"] = qraw = matmul(hn, wi["sb_w_q"], "nn", name="mm_sq")
            s["q"] = q = headnorm_fwd(qraw, 0, sb_q_norm[j:j + 1], HEAD_DIM ** -0.5, name="headnorm_q")
            s["o"], *got = sb_fwd(q, k_sh, v_sh, gather=nxt, name="sb_fwd")
            mixed, w_mix_out = s["o"], wi["sb_w_out"]
        if nxt:
            wl.append(layer_weights(i + 1, got))
        s["h1"], s["hn2"] = h, hn2 = matmul_rows(mixed, w_mix_out, [h], [ln_ffn[i:i + 1]], resid_rms, (F32, BF16),
                                                 name="mm_out_rms")
        s["gu"], s["act"] = ffn_in_swiglu(hn2, wi["ffn_w_in"], name="ffn_in_swiglu")
        s["h2"], s["hn3"] = h, hn3 = matmul_rows(s["act"], wi["ffn_w_out"], [h], [ln_ple[i:i + 1]], resid_rms,
                                                 (F32, BF16), name="mm_ffn_out_rms")
        s["pp"] = pp = matmul(p[i, 0], wi["ple_w_proj"], "nn", name="mm_ple_proj")
        if i + 1 < depth:
            s["gt"], h, hn_next = matmul_rows(hn3, wi["ple_w_gate"], [h, pp], [ln_mix[i + 1:i + 2]], ple_rms,
                                              (F32, F32, BF16), name="mm_ple_gate_rms")
        else:
            s["gt"], h = matmul_rows(hn3, wi["ple_w_gate"], [h, pp], [], lambda r, res, e: ple_rms(r, res, e, None)[:2],
                                     (F32, F32), name="mm_ple_gate")
        saved.append(s)
        if i == n_a - 1:
            mid = dict(h=h)
            mid["hk"] = hk = rms_fwd(h, kv_norm[None, :], name="rms_fwd")
            mid["kv"] = kv = matmul(hk, wi["w_kv"], "nn", name="mm_kv")
            k_sh = headnorm_fwd(kv, 0, k_norm[None, :], 1.0, name="headnorm_k")
            v_sh = kv[:, WIDTH:].astype(BF16)

    dh, sq = loss_head(h, tgt, name="loss_head")
    loss = lax.psum(0.5 * jnp.sum(sq) / h.shape[1], ("x", "y", "c"))

    gw = {n: [None] * w[n].shape[0] for n in WEIGHTS if w[n].ndim >= 2 and n not in ("w_kv",)}
    dks, dvs = [], []
    reduced = [None] * depth

    def reduce_begin(i):
        pieces = [_shard(gw[n] if idx is None else gw[n][idx], ax) for n, idx, ax in layer_items(i)]
        bufs = [_pack_rows(pieces, (N_CHIPS,)), gw["ffn_w_in"][i]] + ([_shard(gw["gdn_w_in"][i], 1)] if i < n_a else [])
        g5s = [g.reshape(N_CHIPS, 2, g.shape[1] // 2, g.shape[2]) for g in bufs]
        from_sibling = sibling_swap(g5s, name="sibling_swap")
        return [add_selected(sel_c, g5, fs, name="add_selected") for g5, fs in zip(g5s, from_sibling)]

    def reduce_end(i, s1s, from_chips):
        s2s = [add_chip_sums(sel_chip, s1, fc, name="add_chip_sums") for (s1, _), fc in zip(s1s, from_chips)]
        reduced[i] = sibling_merge(s2s, name="sibling_merge")

    pending = None
    for i in reversed(range(depth)):
        s = saved[i]
        riding = [s1b for _, s1b in pending[1]] if pending else []
        if i == n_a - 1:
            dkraw, gw["k_norm"] = headnorm_bwd(mid["kv"], 0, k_norm[None, :], 1.0, tuple(dks), name="headnorm_k_bwd")
            dkv = jnp.concatenate([dkraw, sum_cast(dvs, BF16, name="sum_dv")], axis=1)
            dhk = matmul(dkv, wl[i]["w_kv"], "nt", name="mm_kv_dx")
            gw["w_kv"] = matmul(mid["hk"], dkv, "tn", name="mm_kv_dw")
            dh, gw["kv_norm"] = rms_bwd(mid["h"], kv_norm[None, :], dhk, dh, name="rms_bwd")
        dpp, dgt = ple_bwd(dh, s["pp"], s["gt"], name="ple_bwd")
        gw["ple_w_proj"][i] = matmul(p[i, 0], dpp, "tn", name="mm_ple_proj_dw")
        gw["ple_w_gate"][i] = matmul(s["hn3"], dgt, "tn", name="mm_sq_dw")
        dhn3 = matmul(dgt, wl[i]["ple_w_gate"], "nt", name="mm_sq_dx")
        dh, gw["ln_ple"][i] = rms_bwd(s["h2"], ln_ple[i:i + 1], dhn3, dh, name="rms_bwd")
        dgu = ffn_out_dx_swiglu(dh, wl[i]["ffn_w_out"], s["gu"], name="ffn_out_dx_swiglu")
        gw["ffn_w_out"][i] = matmul(s["act"], dh, "tn", name="mm_ffn_out_dw")
        dhn2 = matmul(dgu, wl[i]["ffn_w_in"], "nt", b_chips=0, name="mm_ffn_in_dx")
        gw["ffn_w_in"][i] = matmul(s["hn2"], dgu, "tn", out_chips=True, name="mm_ffn_in_dw")
        dh, gw["ln_ffn"][i] = rms_bwd(s["h1"], ln_ffn[i:i + 1], dhn2, dh, name="rms_bwd")
        if i < n_a:
            dy = matmul(dh, wl[i]["gdn_w_out"], "nt", name="mm_sq_dx")
            gw["gdn_w_out"][i] = matmul(s["y"], dh, "tn", name="mm_sq_dw")
            do, dproj, gw["gdn_norm"][i] = gatenorm_bwd(s["o"], s["proj"], gdn_norm[i:i + 1], dy, name="gatenorm_bwd")
            dqkv, dg4, db4, *from_chips = gdn_bwd(s["qkv"], s["g4"], s["b4"], s["s0"], s["tinv"], do, exchange=riding,
                                                  name="gdn_bwd")
            dab, dal, ddt = gates_bwd(s["ab"], s["a_log"], s["dt"], _from4(dg4), _from4(db4), name="gates_bwd")
            gw["gdn_a_log"][i], gw["gdn_dt_bias"][i] = dal[:, 0], ddt[:, 0]
            dproj, gw["gdn_conv"][i] = conv_bwd(s["proj"], conv_full[i], dqkv, dproj, name="conv_bwd")
            dhn = matmul(dproj, s["w_m"], "nt", name="mm_gdn_in_dx")
            dhn = matmul(dab, s["w_abt"], "tn", add=dhn, name="mm_gdn_ab_dx")
            dwm = matmul(s["hn"], dproj, "tn", name="mm_gdn_in_dw")
            dwab = matmul(dab, s["hn"], "nn", name="mm_gdn_ab_dw")
            gw["gdn_w_in"][i] = jnp.concatenate([dwm, dwab.T], axis=1)
        else:
            j = i - n_a
            do = matmul(dh, wl[i]["sb_w_out"], "nt", out_dtype=BF16, name="mm_sb_out_dx")
            gw["sb_w_out"][j] = matmul(s["o"], dh, "tn", name="mm_sq_dw")
            dq, dk, dv, *from_chips = sb_bwd(s["q"], k_sh, v_sh, do, exchange=riding, name="sb_bwd")
            dks.append(dk)
            dvs.append(dv)
            dqraw, gw["sb_q_norm"][j] = headnorm_bwd(s["qraw"], 0, sb_q_norm[j:j + 1], HEAD_DIM ** -0.5, (dq,),
                                                    name="headnorm_q_bwd")
            dhn = matmul(dqraw, wl[i]["sb_w_q"], "nt", name="mm_sq_dx")
            gw["sb_w_q"][j] = matmul(s["hn"], dqraw, "tn", name="mm_sq_dw")
        dh, gw["ln_mix"][i] = rms_bwd(s["h0"], ln_mix[i:i + 1], dhn, dh, name="rms_bwd")
        if pending:
            reduce_end(pending[0], pending[1], from_chips)
        pending = (i, reduce_begin(i))
    reduce_end(pending[0], pending[1], chip_exchange([s1b for _, s1b in pending[1]], name="chip_exchange"))
    grad_x = dh[None]

    def stacked(n):
        g = gw[n]
        if isinstance(g, list):
            g = jnp.stack([a.reshape(w[n].shape[1:]) if n in SMALL else a for a in g])
        return g

    small_buf = jnp.concatenate([_row(stacked(n)) for n in SMALL] + [_row(stacked("gdn_conv"))], axis=0)
    small_buf = jnp.pad(small_buf, ((0, buf_rows - small_buf.shape[0]), (0, 0)))
    small_sum = all_reduce_small(small_buf, name="all_reduce_small")
    grads = {}
    r0 = 0
    for n in SMALL:
        rows = _rows_of(w[n].shape, 1)
        grads[n] = small_sum[r0:r0 + rows].reshape(-1)[:math.prod(w[n].shape)].reshape(w[n].shape)
        r0 += rows
    conv_g = small_sum[r0:r0 + N_CHIPS * conv_rows].reshape(-1)[:N_CHIPS * math.prod(gdn_conv.shape)]
    conv_g = conv_g.reshape((gdn_conv.shape[0], CONV_WIDTH, N_CHIPS, gdn_conv.shape[2]))
    grads["gdn_conv"] = lax.dynamic_index_in_dim(conv_g, chip, axis=2, keepdims=False)

    per_layer = {n: [None] * w[n].shape[0] for n, _ in BIG if n != "w_kv"}
    for i in range(depth):
        parts = _unpack_rows(reduced[i][0].reshape(-1, PACK_COLS), [q.shape for q in layer_shards(i)], ())
        for (n, idx, _), g in zip(layer_items(i), parts):
            if idx is None:
                grads[n] = g
            else:
                per_layer[n][idx] = g
    for n, parts in per_layer.items():
        grads[n] = jnp.stack(parts)
    grads["ffn_w_in"] = jnp.stack([reduced[i][1].reshape(ffn_w_in.shape[1:]) for i in range(depth)])
    grads["gdn_w_in"] = jnp.stack([reduced[i][2].reshape(gdn_w_in.shape[1:]) for i in range(n_a)])

    delta, new_m, new_v = {}, {}, {}
    for n in WEIGHTS:
        delta[n], new_m[n], new_v[n] = adamw(w[n], grads[n], mom1[n], mom2[n], name="adamw")
    return (loss, grad_x, *[grads[n] for n in WEIGHTS], *[delta[n] for n in WEIGHTS],
            *[new_m[n] for n in WEIGHTS], *[new_v[n] for n in WEIGHTS])
```

```python
import functools
import math

import jax
import jax.numpy as jnp
from jax import lax
from jax.experimental import pallas as pl
from jax.experimental.pallas import tpu as pltpu

F32 = jnp.float32
BF16 = jnp.bfloat16
EPS = 1e-6
HEADS = 8
HEAD_DIM = 128
WIDTH = HEADS * HEAD_DIM
CHUNK = 64
CONV_WIDTH = 4
N_CHIPS = 4
ADAM_LR, ADAM_B1, ADAM_B2, ADAM_EPS, ADAM_WD, ADAM_STEP = 0.001, 0.9, 0.999, 1e-08, 0.01, 10
V7X_VMEM_BYTES = 64 * 1024 * 1024
VMEM_LIMIT = V7X_VMEM_BYTES - 8 * 1024 * 1024
HIGHEST = lax.Precision.HIGHEST
MESH = pl.DeviceIdType.MESH


def _params(sem=None):
    return pltpu.CompilerParams(dimension_semantics=sem, vmem_limit_bytes=VMEM_LIMIT)


def _pick(n, prefs):
    for t in prefs:
        if t <= n and n % t == 0:
            return t
    return n


def _bdot(a, b, dims):
    return lax.dot_general(a.astype(BF16), b.astype(BF16), (((dims[0],), (dims[1],)), ((), ())),
                           preferred_element_type=F32)


NN, NT, TN = (1, 0), (1, 1), (0, 0)


MM_TILES = (1024, 1408, 512, 256, 128)


def matmul(a, b, form, *, out_dtype=F32, add=None, name, b_chips=None, out_chips=False):
    ns = None
    if b_chips is not None:
        ns = b.shape[3]
        b_shape = (b.shape[2], N_CHIPS * ns)
    else:
        b_shape = b.shape
    if form == "nn":
        (m, k), (k2, n) = a.shape, b_shape
    elif form == "nt":
        (m, k), (n, k2) = a.shape, b_shape
    else:
        (k, m), (k2, n) = a.shape, b_shape
    assert k == k2, (a.shape, b.shape, form)
    if out_chips:
        ns = n // N_CHIPS
    tm = _pick(m, MM_TILES)
    tn = _pick(n, MM_TILES)
    tk = k if k <= 1024 else _pick(k, MM_TILES)
    if ns is not None and (form == "nn" or out_chips):
        tn = ns
    if ns is not None and form == "nt":
        tk = ns
    nk = k // tk
    if form == "tn":
        a_spec = pl.BlockSpec((tk, tm), lambda i, j, kk: (kk, i))
    else:
        a_spec = pl.BlockSpec((tm, tk), lambda i, j, kk: (i, kk))
    if b_chips is not None and form == "nn":
        b_spec = pl.BlockSpec((None, None, tk, ns), lambda i, j, kk: (j, b_chips, kk, 0))
    elif b_chips is not None:
        b_spec = pl.BlockSpec((None, None, tn, ns), lambda i, j, kk: (kk, b_chips, j, 0))
    elif form == "nt":
        b_spec = pl.BlockSpec((tn, tk), lambda i, j, kk: (j, kk))
    else:
        b_spec = pl.BlockSpec((tk, tn), lambda i, j, kk: (kk, j))
    if out_chips:
        o_spec = pl.BlockSpec((None, tm, ns), lambda i, j, kk: (j, i, 0))
    else:
        o_spec = pl.BlockSpec((tm, tn), lambda i, j, kk: (i, j))
    dims = {"nn": NN, "nt": NT, "tn": TN}[form]
    has_add = add is not None

    def body(*refs):
        a_ref, b_ref = refs[:2]
        add_ref = refs[2] if has_add else None
        o_ref = refs[2 + has_add]

        def finish(r):
            if has_add:
                r = r + add_ref[...].astype(F32)
            o_ref[...] = r.astype(out_dtype)

        part = _bdot(a_ref[...], b_ref[...], dims)
        if nk == 1:
            finish(part)
            return
        acc_ref = refs[3 + has_add]
        kk = pl.program_id(2)

        @pl.when(kk == 0)
        def _():
            acc_ref[...] = part

        @pl.when(kk > 0)
        def _():
            acc_ref[...] += part

        @pl.when(kk == nk - 1)
        def _():
            finish(acc_ref[...])

    in_specs = [a_spec, b_spec] + ([o_spec] if has_add else [])
    args = (a, b) + ((add,) if has_add else ())
    return pl.pallas_call(
        body, name=name, grid=(m // tm, n // tn, nk), in_specs=in_specs, out_specs=o_spec,
        out_shape=jax.ShapeDtypeStruct((N_CHIPS, m, ns) if out_chips else (m, n), out_dtype),
        scratch_shapes=[pltpu.VMEM((tm, tn), F32)] if nk > 1 else [],
        compiler_params=_params(("parallel", "parallel", "arbitrary")),
    )(*args)


def matmul_rows(a, b, extra, params, epilogue, out_dtypes, *, name):
    (m, k), (k2, n) = a.shape, b.shape
    assert k == k2, (a.shape, b.shape)
    tm = _pick(m, (1024, 512, 256, 128))
    tk = k if k <= 1024 else _pick(k, MM_TILES)
    nk = k // tk
    ne, npar, no = len(extra), len(params), len(out_dtypes)

    def body(*refs):
        a_ref, b_ref = refs[:2]
        e_refs, p_refs = refs[2:2 + ne], refs[2 + ne:2 + ne + npar]
        o_refs = refs[2 + ne + npar:2 + ne + npar + no]

        def finish(r):
            outs = epilogue(r, *[e[...] for e in e_refs], *[q[...] for q in p_refs])
            for o_ref, val in zip(o_refs, outs):
                o_ref[...] = val.astype(o_ref.dtype)

        part = _bdot(a_ref[...], b_ref[...], NN)
        if nk == 1:
            finish(part)
            return
        acc_ref = refs[-1]
        kk = pl.program_id(1)

        @pl.when(kk == 0)
        def _():
            acc_ref[...] = part

        @pl.when(kk > 0)
        def _():
            acc_ref[...] += part

        @pl.when(kk == nk - 1)
        def _():
            finish(acc_ref[...])

    row = pl.BlockSpec((tm, n), lambda i, kk: (i, 0))
    in_specs = ([pl.BlockSpec((tm, tk), lambda i, kk: (i, kk)), pl.BlockSpec((tk, n), lambda i, kk: (kk, 0))]
                + [row] * ne + [pl.BlockSpec((1, n), lambda i, kk: (0, 0))] * npar)
    return pl.pallas_call(
        body, name=name, grid=(m // tm, nk), in_specs=in_specs, out_specs=[row] * no,
        out_shape=[jax.ShapeDtypeStruct((m, n), dt) for dt in out_dtypes],
        scratch_shapes=[pltpu.VMEM((tm, n), F32)] if nk > 1 else [],
        compiler_params=_params(("parallel", "arbitrary")),
    )(a, b, *extra, *params)


def _const(c):
    return lambda j: c


def rowwise(fn, rows, params, outs, accs=(), *, name, tm, ncol=1):
    t = rows[0][0].shape[0]
    tm = min(tm, t)
    assert t % tm == 0
    n_rows, n_par, n_out, n_acc = len(rows), len(params), len(outs), len(accs)

    def body(*refs):
        j, i = pl.program_id(0), pl.program_id(1)
        ins = [r[...] for r in refs[:n_rows + n_par]]
        o_refs = refs[n_rows + n_par:n_rows + n_par + n_out]
        a_refs = refs[n_rows + n_par + n_out:]
        row_outs, acc_outs = fn(*ins)
        for r, val in zip(o_refs, row_outs):
            r[...] = val.astype(r.dtype)
        for r, val, spec in zip(a_refs, acc_outs, accs):
            first = (i == 0) & (j == 0) if spec[4] else (i == 0)

            @pl.when(first)
            def _(r=r, val=val):
                r[...] = val.astype(F32)

            @pl.when(jnp.logical_not(first))
            def _(r=r, val=val):
                r[...] += val.astype(F32)

    in_specs = [pl.BlockSpec((tm, w), lambda j, i, cf=cf: (i, cf(j))) for _, w, cf in rows]
    in_specs += [pl.BlockSpec((p.shape[0], w), lambda j, i, cf=cf: (0, cf(j))) for p, w, cf in params]
    out_specs = [pl.BlockSpec((tm, w), lambda j, i, cf=cf: (i, cf(j))) for _, _, w, cf in outs]
    out_specs += [pl.BlockSpec((r, w), lambda j, i, cf=cf: (0, cf(j))) for r, _, w, cf, _ in accs]
    out_shape = [jax.ShapeDtypeStruct((t, tw), dt) for tw, dt, _, _ in outs]
    out_shape += [jax.ShapeDtypeStruct((r, tw), F32) for r, tw, _, _, _ in accs]
    res = pl.pallas_call(
        body, name=name, grid=(ncol, t // tm), in_specs=in_specs, out_specs=out_specs, out_shape=out_shape,
        compiler_params=_params(("arbitrary", "arbitrary")),
    )(*[r[0] for r in rows], *[p[0] for p in params])
    return res[:n_out], res[n_out:]


def _full(arr):
    return (arr, arr.shape[1], _const(0))


def _rms(x, g):
    x = x.astype(F32)
    return x * lax.rsqrt(jnp.mean(x * x, axis=-1, keepdims=True) + EPS) * g.astype(F32)


def _sigmoid(x):
    return 1.0 / (1.0 + jnp.exp(-x))


def _silu(x):
    return x * _sigmoid(x)


def _softplus(x):
    return jnp.maximum(x, 0.0) + jnp.log(1.0 + jnp.exp(-jnp.abs(x)))


def rms_fwd(h, g, *, name):
    d = h.shape[1]
    (hn,), _ = rowwise(lambda x, gg: ((_rms(x, gg),), ()), [_full(h)], [_full(g)],
                       [(d, BF16, d, _const(0))], name=name, tm=512)
    return hn


def rms_bwd(h, g, dhn, dh_res, *, name):
    d = h.shape[1]

    def fn(x, ct, res, gg):
        _, vjp = jax.vjp(_rms, x.astype(F32), gg.astype(F32))
        dx, dg = vjp(ct.astype(F32))
        return (res.astype(F32) + dx,), (dg,)

    (dh,), (dg,) = rowwise(fn, [_full(h), _full(dhn), _full(dh_res)], [_full(g)],
                           [(d, F32, d, _const(0))], [(1, d, d, _const(0), True)], name=name, tm=256)
    return dh, dg


def _head_rms(x, g, scale):
    x = x.astype(F32)
    return x * lax.rsqrt(jnp.mean(x * x, axis=-1, keepdims=True) + EPS) * (g.astype(F32) * scale)


def headnorm_fwd(x, col0, g, scale, *, name):
    (y,), _ = rowwise(lambda a, gg: ((_head_rms(a, gg, scale),), ()),
                      [(x, HEAD_DIM, lambda j: col0 + j)], [_full(g)],
                      [(WIDTH, BF16, HEAD_DIM, lambda j: j)], name=name, tm=1024, ncol=HEADS)
    return y


def headnorm_bwd(x, col0, g, scale, dys, *, name, out_dtype=BF16):
    def fn(a, *rest):
        cts, gg = rest[:-1], rest[-1]
        ct = sum(c.astype(F32) for c in cts)
        _, vjp = jax.vjp(lambda a_, g_: _head_rms(a_, g_, scale), a.astype(F32), gg.astype(F32))
        dx, dg = vjp(ct)
        return (dx,), (dg,)

    (dx,), (dg,) = rowwise(fn, [(x, HEAD_DIM, lambda j: col0 + j)] + [(dy, HEAD_DIM, lambda j: j) for dy in dys], [_full(g)],
                           [(WIDTH, out_dtype, HEAD_DIM, lambda j: j)],
                           [(1, HEAD_DIM, HEAD_DIM, _const(0), True)], name=name, tm=1024, ncol=HEADS)
    return dx, dg


def sum_cast(parts, dtype, *, name):
    wd = parts[0].shape[1]
    (out,), _ = rowwise(lambda *a: ((sum(b.astype(F32) for b in a),), ()), [_full(a) for a in parts], [],
                        [(wd, dtype, wd, _const(0))], name=name, tm=512)
    return out


def _gatenorm(o, gate, g):
    return _head_rms(o, g, 1.0) * _silu(gate.astype(F32))


def gatenorm_fwd(o, proj, g, *, name):
    (y,), _ = rowwise(lambda a, gt, gg: ((_gatenorm(a, gt, gg),), ()),
                      [(o, HEAD_DIM, lambda j: j), (proj, HEAD_DIM, lambda j: 3 * HEADS + j)], [_full(g)],
                      [(WIDTH, BF16, HEAD_DIM, lambda j: j)], name=name, tm=1024, ncol=HEADS)
    return y


def gatenorm_bwd(o, proj, g, dy, *, name):
    def fn(a, gt, ct, gg):
        _, vjp = jax.vjp(_gatenorm, a.astype(F32), gt.astype(F32), gg.astype(F32))
        da, dgt, dg = vjp(ct.astype(F32))
        return (da, dgt), (dg,)

    (do, dproj), (dg,) = rowwise(
        fn, [(o, HEAD_DIM, lambda j: j), (proj, HEAD_DIM, lambda j: 3 * HEADS + j), (dy, HEAD_DIM, lambda j: j)],
        [_full(g)],
        [(WIDTH, F32, HEAD_DIM, lambda j: j), (4 * WIDTH, BF16, HEAD_DIM, lambda j: 3 * HEADS + j)],
        [(1, HEAD_DIM, HEAD_DIM, _const(0), True)], name=name, tm=1024, ncol=HEADS)
    return do, dproj, dg


def _swiglu(g, u):
    return _silu(g.astype(F32)) * u.astype(F32)


def ffn_in_swiglu(hn, w4, *, name):
    t, dm = hn.shape
    ns = w4.shape[3]
    half = N_CHIPS // 2
    tm = min(512, t)

    def body(a_ref, w_ref, gu_ref, act_ref):
        a = a_ref[...]
        for j in range(half):
            g = _bdot(a, w_ref[j, 0], NN)
            u = _bdot(a, w_ref[half + j, 0], NN)
            gu_ref[:, j * ns:(j + 1) * ns] = g.astype(BF16)
            gu_ref[:, (half + j) * ns:(half + j + 1) * ns] = u.astype(BF16)
            act_ref[:, j * ns:(j + 1) * ns] = (_silu(g) * u).astype(BF16)

    return pl.pallas_call(
        body, name=name, grid=(t // tm,),
        in_specs=[pl.BlockSpec((tm, dm), lambda i: (i, 0)), pl.BlockSpec(w4.shape, lambda i: (0, 0, 0, 0))],
        out_specs=[pl.BlockSpec((tm, N_CHIPS * ns), lambda i: (i, 0)), pl.BlockSpec((tm, half * ns), lambda i: (i, 0))],
        out_shape=[jax.ShapeDtypeStruct((t, N_CHIPS * ns), BF16), jax.ShapeDtypeStruct((t, half * ns), BF16)],
        compiler_params=_params(("parallel",)),
    )(hn, w4)


def ffn_out_dx_swiglu(dh, w_out, gu, *, name, swap=()):
    t, dm = dh.shape
    f = w_out.shape[0]
    tm = min(256, t)
    n = t // tm
    nw = len(swap)

    def body(*refs):
        dh_ref, w_ref, gu_ref = refs[:3]
        o_ref = refs[3 + nw]
        comm = _Swap(refs[3:3 + nw], refs[4 + nw:4 + 2 * nw], *refs[4 + 2 * nw:]) if nw else None
        if nw:
            @pl.when(pl.program_id(0) == 0)
            def _():
                comm.start()

        dact = _bdot(dh_ref[...], w_ref[...], NT)
        g, u = gu_ref[:, :f].astype(F32), gu_ref[:, f:].astype(F32)
        sg = _sigmoid(g)
        gs = g * sg
        o_ref[:, :f] = (dact * u * (sg + gs * (1.0 - sg))).astype(BF16)
        o_ref[:, f:] = (dact * gs).astype(BF16)

        if nw:
            @pl.when(pl.program_id(0) == n - 1)
            def _():
                comm.finish()

    return pl.pallas_call(
        body, name=name, grid=(n,),
        in_specs=[pl.BlockSpec((tm, dm), lambda i: (i, 0)), pl.BlockSpec((f, dm), lambda i: (0, 0)),
                  pl.BlockSpec((tm, 2 * f), lambda i: (i, 0))] + [ANY] * nw,
        out_specs=[pl.BlockSpec((tm, 2 * f), lambda i: (i, 0))] + [ANY] * nw,
        out_shape=[jax.ShapeDtypeStruct((t, 2 * f), BF16)] + _swap_shapes(swap),
        scratch_shapes=_dma_sems(N_CHIPS * nw) if nw else [],
        compiler_params=_params(("arbitrary",) if nw else ("parallel",)),
    )(dh, w_out, gu, *swap)


def ple_bwd(dh, pp, gt, *, name):
    d = dh.shape[1]

    def fn(ct, b, c):
        s = _sigmoid(c)
        return (ct * s, ct * b * s * (1.0 - s)), ()

    (dpp, dgt), _ = rowwise(fn, [_full(dh), _full(pp), _full(gt)], [],
                            [(d, BF16, d, _const(0)), (d, BF16, d, _const(0))], name=name, tm=512)
    return dpp, dgt


def loss_head(y, tgt, *, name):
    d = y.shape[1]

    def fn(a, b):
        e = a - b
        return (e * (1.0 / d),), (jnp.sum(e * e, axis=0, keepdims=True),)

    (dy,), (sq,) = rowwise(fn, [_full(y), _full(tgt)], [], [(d, F32, d, _const(0))],
                           [(1, d, d, _const(0), True)], name=name, tm=512)
    return dy, sq


def adamw(w, g, m, v, *, name):
    shape = w.shape
    cols = shape[-1]
    flat = lambda a: a.reshape(-1, cols)
    bc1 = 1.0 - ADAM_B1 ** ADAM_STEP
    bc2 = 1.0 - ADAM_B2 ** ADAM_STEP

    def fn(w_, g_, m_, v_):
        m_ = ADAM_B1 * m_ + (1.0 - ADAM_B1) * g_
        v_ = ADAM_B2 * v_ + (1.0 - ADAM_B2) * (g_ * g_)
        delta = -ADAM_LR * ((m_ / bc1) / (jnp.sqrt(v_ / bc2) + ADAM_EPS) + ADAM_WD * w_)
        return (delta, m_, v_), ()

    o = (cols, F32, cols, _const(0))
    (d_, m_, v_), _ = rowwise(fn, [_full(flat(w)), _full(flat(g)), _full(flat(m)), _full(flat(v))], [],
                              [o, o, o], name=name, tm=256)
    return d_.reshape(shape), m_.reshape(shape), v_.reshape(shape)


CONV_STRIP = 256


def _shift_down(x, d):
    if d == 0:
        return x
    rows = lax.broadcasted_iota(jnp.int32, x.shape, 0)
    return jnp.where(rows >= d, pltpu.roll(x, d, 0), 0.0)


def _shift_up(x, d):
    if d == 0:
        return x
    t = x.shape[0]
    rows = lax.broadcasted_iota(jnp.int32, x.shape, 0)
    return jnp.where(rows < t - d, pltpu.roll(x, t - d, 0), 0.0)


def _conv(x, w):
    acc = None
    for j in range(CONV_WIDTH):
        term = _shift_down(x, CONV_WIDTH - 1 - j) * w[j:j + 1, :]
        acc = term if acc is None else acc + term
    return acc


def conv_fwd(proj, w, *, name):
    t = proj.shape[0]
    per = WIDTH // CONV_STRIP

    def body(x_ref, w_ref, o_ref):
        o_ref[0] = _silu(_conv(x_ref[...].astype(F32), w_ref[...]))

    return pl.pallas_call(
        body, name=name, grid=(3 * per,),
        in_specs=[pl.BlockSpec((t, CONV_STRIP), lambda j: (0, j)), pl.BlockSpec((CONV_WIDTH, CONV_STRIP), lambda j: (0, j))],
        out_specs=pl.BlockSpec((1, t, CONV_STRIP), lambda j: (j // per, 0, j % per)),
        out_shape=jax.ShapeDtypeStruct((3, t, WIDTH), F32),
        compiler_params=_params(("parallel",)),
    )(proj, w)


def conv_bwd(proj, w, dqkv, dproj, *, name):
    t = proj.shape[0]
    per = WIDTH // CONV_STRIP

    def body(x_ref, w_ref, d_ref, _, dx_ref, dw_ref):
        x, w_ = x_ref[...].astype(F32), w_ref[...]
        c = _conv(x, w_)
        s = _sigmoid(c)
        dc = d_ref[0] * (s + c * s * (1.0 - s))
        dx = None
        for j in range(CONV_WIDTH):
            d = CONV_WIDTH - 1 - j
            term = _shift_up(dc, d) * w_[j:j + 1, :]
            dx = term if dx is None else dx + term
            dw_ref[j:j + 1, :] = jnp.sum(dc * _shift_down(x, d), axis=0, keepdims=True)
        dx_ref[...] = dx.astype(dx_ref.dtype)

    return pl.pallas_call(
        body, name=name, grid=(3 * per,),
        in_specs=[pl.BlockSpec((t, CONV_STRIP), lambda j: (0, j)), pl.BlockSpec((CONV_WIDTH, CONV_STRIP), lambda j: (0, j)),
                  pl.BlockSpec((1, t, CONV_STRIP), lambda j: (j // per, 0, j % per)), pl.BlockSpec(memory_space=pl.ANY)],
        out_specs=[pl.BlockSpec((t, CONV_STRIP), lambda j: (0, j)), pl.BlockSpec((CONV_WIDTH, CONV_STRIP), lambda j: (0, j))],
        out_shape=[jax.ShapeDtypeStruct(dproj.shape, dproj.dtype), jax.ShapeDtypeStruct((CONV_WIDTH, 3 * WIDTH), F32)],
        input_output_aliases={3: 0},
        compiler_params=_params(("parallel",)),
    )(proj, w, dqkv, dproj)


def _gdn_gates(ab, a_log, dt_bias):
    a_in, b_in = ab[:HEADS], ab[HEADS:]
    g = -jnp.exp(a_log) * _softplus(a_in + dt_bias)
    return g, _sigmoid(b_in)


def gates_fwd(ab, a_log, dt_bias, *, name):
    t = ab.shape[1]

    def body(ab_ref, al_ref, dt_ref, g_ref, b_ref):
        g_ref[...], b_ref[...] = _gdn_gates(ab_ref[...], al_ref[...], dt_ref[...])

    s = jax.ShapeDtypeStruct((HEADS, t), F32)
    return pl.pallas_call(body, name=name, out_shape=[s, s], compiler_params=_params())(ab, a_log, dt_bias)


def gates_bwd(ab, a_log, dt_bias, dg, dbeta, *, name):
    t = ab.shape[1]

    def body(ab_ref, al_ref, dt_ref, dg_ref, db_ref, dab_ref, dal_ref, ddt_ref):
        _, vjp = jax.vjp(_gdn_gates, ab_ref[...], al_ref[...], dt_ref[...])
        dab_ref[...], dal_ref[...], ddt_ref[...] = vjp((dg_ref[...], db_ref[...]))

    c = jax.ShapeDtypeStruct((HEADS, 1), F32)
    return pl.pallas_call(body, name=name, out_shape=[jax.ShapeDtypeStruct((2 * HEADS, t), F32), c, c],
                          compiler_params=_params())(ab, a_log, dt_bias, dg, dbeta)


def _split3(x):
    hi = x.astype(BF16)
    r1 = x - hi.astype(F32)
    mid = r1.astype(BF16)
    lo = (r1 - mid.astype(F32)).astype(BF16)
    return hi, mid, lo


def _dot01(x, m01):
    hi, mid, lo = _split3(x)
    m01 = m01.astype(BF16)
    return _bdot(hi, m01, NN) + _bdot(mid, m01, NN) + _bdot(lo, m01, NN)


def _hdot(a, b, dims=NN):
    a_hi, b_hi = a.astype(BF16), b.astype(BF16)
    a_lo, b_lo = (a - a_hi.astype(F32)).astype(BF16), (b - b_hi.astype(F32)).astype(BF16)
    return _bdot(a_hi, b_hi, dims) + (_bdot(a_hi, b_lo, dims) + _bdot(a_lo, b_hi, dims))


def _rowsum(x):
    return jnp.sum(x, axis=1, keepdims=True)


def _colsum(x):
    return jnp.sum(x, axis=0, keepdims=True)


class _Heads:
    def __init__(self, vals):
        self.v = list(vals)

    def _bin(self, other, f):
        if isinstance(other, _Heads):
            return _Heads(f(a, b) for a, b in zip(self.v, other.v))
        return _Heads(f(a, other) for a in self.v)

    def __add__(self, o):
        return self._bin(o, lambda a, b: a + b)

    __radd__ = __add__

    def __sub__(self, o):
        return self._bin(o, lambda a, b: a - b)

    def __rsub__(self, o):
        return self._bin(o, lambda a, b: b - a)

    def __mul__(self, o):
        return self._bin(o, lambda a, b: a * b)

    __rmul__ = __mul__

    def __neg__(self):
        return _Heads(-a for a in self.v)


def _hmap(f, *args):
    n = next(len(a.v) for a in args if isinstance(a, _Heads))
    return _Heads(f(*[a.v[h] if isinstance(a, _Heads) else a for a in args]) for h in range(n))


def _inv_unit_lower(a, eye):
    p = jnp.where(eye, 1.0, 0.0) - a
    ak = a
    for _ in range(int(math.log2(CHUNK)) - 1):
        ak = _hmap(_hdot, ak, ak)
        p = p + _hmap(_hdot, p, ak)
    return p


def _gdn_chunk(qr, kr, v, grow, brow, tinv=None):
    c = CHUNK
    ri = lax.broadcasted_iota(jnp.int32, (c, c), 0)
    ci = lax.broadcasted_iota(jnp.int32, (c, c), 1)
    eye, lower, strict = ri == ci, ri >= ci, ri > ci
    where = lambda m: (lambda a: jnp.where(m, a, 0.0))
    to_col = lambda row: _hmap(lambda r: _rowsum(jnp.where(eye, jnp.broadcast_to(r, (c, c)), 0.0)), row)
    cum_row = _hmap(lambda g: _dot01(jnp.broadcast_to(g, (8, c)), ri <= ci)[0:1], grow)
    gcol, bcol = to_col(cum_row), to_col(brow)
    glast = _hmap(lambda g: _colsum(jnp.where(ri[:, 0:1] == c - 1, g, 0.0)), gcol)
    rq = _hmap(lambda a: lax.rsqrt(_rowsum(a * a) + EPS), qr)
    rk = _hmap(lambda a: lax.rsqrt(_rowsum(a * a) + EPS), kr)
    scale = HEAD_DIM ** -0.5
    qn, kn = qr * (rq * scale), kr * rk
    dec = _hmap(lambda gc, gr: jnp.where(lower, jnp.exp(jnp.minimum(gc - gr, 0.0)), 0.0), gcol, cum_row)
    kk = _hmap(lambda a: _bdot(a, a, NT), kn)
    qk = _hmap(lambda a, b: _bdot(a, b, NT), qn, kn)
    gam_col, e_col, gam_last = _hmap(jnp.exp, gcol), _hmap(jnp.exp, glast - gcol), _hmap(jnp.exp, glast)
    if tinv is None:
        tinv = _inv_unit_lower(_hmap(where(strict), bcol * kk * dec), eye)
    u = _hmap(_hdot, tinv, v * bcol)
    w = _hmap(_hdot, tinv, kn * (bcol * gam_col))
    return dict(eye=eye, lower=lower, strict=strict, gcol=gcol, bcol=bcol, rq=rq, rk=rk, qn=qn, kn=kn,
                dec=dec, kk=kk, qk=qk, gam_col=gam_col, e_col=e_col, gam_last=gam_last, tinv=tinv, u=u, w=w,
                aqk=qk * dec, qt=qn * gam_col, kt=kn * e_col, scale=scale)


def _head_cols(h):
    return slice(h * HEAD_DIM, (h + 1) * HEAD_DIM)


def _bd(dims):
    return lambda a, b: _bdot(a, b, dims)


def gdn_fwd(qkv, g4, b4, *, name, gather=()):
    t = qkv.shape[1]
    n = t // CHUNK
    d = HEAD_DIM
    heads = range(HEADS)
    ng = len(gather)

    def body(*refs):
        qkv_ref, g_ref, b_ref = refs[:3]
        o_ref, s0_ref, t_ref = refs[3 + ng:6 + ng]
        s_ref = refs[6 + 2 * ng]
        comm = _Gather(refs[6 + ng:6 + 2 * ng], *refs[7 + 2 * ng:]) if ng else None

        @pl.when(pl.program_id(0) == 0)
        def _():
            s_ref[...] = jnp.zeros_like(s_ref)
            if ng:
                comm.start()

        qr, kr, v = (_Heads(qkv_ref[j, :, _head_cols(h)] for h in heads) for j in range(3))
        z = _gdn_chunk(qr, kr, v, _Heads(g_ref[0, h] for h in heads), _Heads(b_ref[0, h] for h in heads))
        s0 = _Heads(s_ref[h] for h in heads)
        v_new = z["u"] - _hmap(_bd(NN), z["w"], s0)
        o = _hmap(_bd(NN), z["qt"], s0) + _hmap(_bd(NN), z["aqk"], v_new)
        s_new = s0 * z["gam_last"] + _hmap(_bd(TN), z["kt"], v_new)
        for h in heads:
            s0_ref[0, h] = s0.v[h]
            t_ref[0, h] = z["tinv"].v[h]
            o_ref[:, _head_cols(h)] = o.v[h]
            s_ref[h] = s_new.v[h]

        if ng:
            @pl.when(pl.program_id(0) == n - 1)
            def _():
                comm.finish()

    gspec = pl.BlockSpec((1, HEADS, 1, CHUNK), lambda i: (i, 0, 0, 0))
    return pl.pallas_call(
        body, name=name, grid=(n,),
        in_specs=[pl.BlockSpec((3, CHUNK, WIDTH), lambda i: (0, i, 0)), gspec, gspec] + [ANY] * ng,
        out_specs=[pl.BlockSpec((CHUNK, WIDTH), lambda i: (i, 0)),
                   pl.BlockSpec((1, HEADS, d, d), lambda i: (i, 0, 0, 0)),
                   pl.BlockSpec((1, HEADS, CHUNK, CHUNK), lambda i: (i, 0, 0, 0))] + [ANY] * ng,
        out_shape=[jax.ShapeDtypeStruct((t, WIDTH), F32), jax.ShapeDtypeStruct((n, HEADS, d, d), F32),
                   jax.ShapeDtypeStruct((n, HEADS, CHUNK, CHUNK), F32)]
        + [jax.ShapeDtypeStruct(s.shape, s.dtype) for s in gather],
        input_output_aliases={3 + b: 3 + b for b in range(ng)},
        scratch_shapes=[pltpu.VMEM((HEADS, d, d), F32)] + (_dma_sems(6 * ng) if ng else []),
        compiler_params=_params(("arbitrary",)),
    )(qkv, g4, b4, *gather)


def gdn_bwd(qkv, g4, b4, s0_all, tinv_all, do, *, name, exchange=()):
    t = qkv.shape[1]
    n = t // CHUNK
    d = HEAD_DIM
    c = CHUNK
    heads = range(HEADS)

    ne = len(exchange)

    def body(*refs):
        qkv_ref, g_ref, b_ref, s0_ref, t_ref, do_ref = refs[:6]
        dqkv_ref, dg_ref, db_ref = refs[6 + ne:9 + ne]
        ds_ref = refs[9 + 2 * ne]
        comm = _Exchange(refs[6:6 + ne], refs[9 + ne:9 + 2 * ne], *refs[10 + 2 * ne:]) if ne else None

        @pl.when(pl.program_id(0) == 0)
        def _():
            ds_ref[...] = jnp.zeros_like(ds_ref)
            if ne:
                comm.start()

        qr, kr, v = (_Heads(qkv_ref[j, :, _head_cols(h)] for h in heads) for j in range(3))
        z = _gdn_chunk(qr, kr, v, _Heads(g_ref[0, h] for h in heads), _Heads(b_ref[0, h] for h in heads),
                       tinv=_Heads(t_ref[0, h] for h in heads))
        s0 = _Heads(s0_ref[0, h] for h in heads)
        ds = _Heads(ds_ref[h] for h in heads)
        dout = _Heads(do_ref[:, _head_cols(h)] for h in heads)
        qn, kn, u, w, dec, kk, qk = z["qn"], z["kn"], z["u"], z["w"], z["dec"], z["kk"], z["qk"]
        bcol, gam_col, e_col, gam_last = z["bcol"], z["gam_col"], z["e_col"], z["gam_last"]
        low = lambda a: jnp.where(z["lower"], a, 0.0)
        strict = lambda a: jnp.where(z["strict"], a, 0.0)
        rowsum = lambda a: _hmap(_rowsum, a)
        colsum = lambda a: _hmap(_colsum, a)
        v_new = u - _hmap(_bd(NN), w, s0)
        dv_new = _hmap(_bd(TN), z["aqk"], dout) + _hmap(_bd(NN), z["kt"], ds)
        daqk = _hmap(low, _hmap(_bd(NT), dout, v_new))
        dqt = _hmap(_bd(NT), dout, s0)
        dkt = _hmap(_bd(NT), v_new, ds)
        dgam_last = _hmap(lambda a, b: jnp.sum(a * b, keepdims=True), ds, s0)
        ds_new = _hmap(_bd(TN), z["qt"], dout) + ds * gam_last - _hmap(_bd(TN), w, dv_new)
        dw = -_hmap(_bd(NT), dv_new, s0)
        hd_t = lambda a, b: _hdot(a, b, TN)
        dru = _hmap(hd_t, z["tinv"], dv_new)
        drw = _hmap(hd_t, z["tinv"], dw)
        dal = -_hmap(strict, _hmap(_bd(NT), dru, u) + _hmap(_bd(NT), drw, w))
        t1 = dal * kk * dec
        dkk = dal * bcol * dec
        ddec = dal * bcol * kk + daqk * qk
        dqk = daqk * dec
        s_w = rowsum(drw * kn)
        dbeta_col = rowsum(t1) + rowsum(dru * v) + gam_col * s_w
        dkn = (drw * (bcol * gam_col) + _hmap(_bd(NN), dkk, kn) + _hmap(_bd(TN), dkk, kn) + _hmap(_bd(TN), dqk, qn)
               + dkt * e_col)
        dqn = _hmap(_bd(NN), dqk, kn) + dqt * gam_col
        e_mat = ddec * dec
        de_col = rowsum(dkt * kn)
        diag_of_colsum = rowsum(_hmap(lambda a: jnp.where(z["eye"], jnp.broadcast_to(_colsum(a), (c, c)), 0.0), e_mat))
        dg_cum = rowsum(e_mat) + (bcol * s_w + rowsum(dqt * qn)) * gam_col - de_col * e_col - diag_of_colsum
        dg_last = colsum(de_col * e_col) + dgam_last * gam_last
        dg = colsum(_hmap(lambda a: jnp.where(z["lower"], a, 0.0), dg_cum)) + dg_last
        dbeta = colsum(_hmap(lambda a: jnp.where(z["eye"], a, 0.0), dbeta_col))
        rq, rk = z["rq"], z["rk"]
        dqr = z["scale"] * (rq * dqn - qr * (rq * rq * rq) * rowsum(qr * dqn))
        dkr = rk * dkn - kr * (rk * rk * rk) * rowsum(kr * dkn)
        dv = dru * bcol
        for h in heads:
            ds_ref[h] = ds_new.v[h]
            dg_ref[0, h] = dg.v[h]
            db_ref[0, h] = dbeta.v[h]
            dqkv_ref[0, :, _head_cols(h)] = dqr.v[h]
            dqkv_ref[1, :, _head_cols(h)] = dkr.v[h]
            dqkv_ref[2, :, _head_cols(h)] = dv.v[h]

        if ne:
            @pl.when(pl.program_id(0) == n - 1)
            def _():
                comm.finish()

    rev = lambda i: n - 1 - i
    gspec = pl.BlockSpec((1, HEADS, 1, CHUNK), lambda i: (rev(i), 0, 0, 0))
    return pl.pallas_call(
        body, name=name, grid=(n,),
        in_specs=[pl.BlockSpec((3, CHUNK, WIDTH), lambda i: (0, rev(i), 0)), gspec, gspec,
                  pl.BlockSpec((1, HEADS, d, d), lambda i: (rev(i), 0, 0, 0)),
                  pl.BlockSpec((1, HEADS, CHUNK, CHUNK), lambda i: (rev(i), 0, 0, 0)),
                  pl.BlockSpec((CHUNK, WIDTH), lambda i: (rev(i), 0))] + [ANY] * ne,
        out_specs=[pl.BlockSpec((3, CHUNK, WIDTH), lambda i: (0, rev(i), 0)), gspec, gspec] + [ANY] * ne,
        out_shape=[jax.ShapeDtypeStruct((3, t, WIDTH), F32), jax.ShapeDtypeStruct((n, HEADS, 1, CHUNK), F32),
                   jax.ShapeDtypeStruct((n, HEADS, 1, CHUNK), F32)] + _exchange_shapes(exchange),
        scratch_shapes=[pltpu.VMEM((HEADS, d, d), F32)] + (_dma_sems(3 * ne) if ne else []),
        compiler_params=_params(("arbitrary",)),
    )(qkv, g4, b4, s0_all, tinv_all, do, *exchange)


SB_BLOCK = 256


def _dot01_2(x, m01):
    hi = x.astype(BF16)
    lo = (x - hi.astype(F32)).astype(BF16)
    return _bdot(hi, m01, NN) + _bdot(lo, m01, NN)


SB_HEADS = 2


def _sb_weights(q, kb, carry, mask, upper):
    z = _hmap(_bd(NT), q, kb)
    ls = _hmap(lambda z_: jnp.minimum(z_, 0.0) - jnp.log(1.0 + jnp.exp(-jnp.abs(z_))), z)
    ln = _hmap(lambda l_, z_: jnp.where(mask, l_ - z_, 0.0), ls, z)
    suffix = _hmap(lambda l_: _dot01_2(l_, upper), ln)
    a = _hmap(lambda l_, s_, c_: jnp.where(mask, jnp.exp(l_ + s_ + c_), 0.0), ls, suffix, carry)
    return z, ln, a


SB_DEAD = -105.0


def _sb_alive(s, i, carries):
    top = jnp.max(carries[0])
    for c in carries[1:]:
        top = jnp.maximum(top, jnp.max(c))
    return (s <= i) & (top > SB_DEAD)


def _sb_masks(i, jb, blk):
    ri = lax.broadcasted_iota(jnp.int32, (blk, blk), 0)
    ci = lax.broadcasted_iota(jnp.int32, (blk, blk), 1)
    return (jb * blk + ci) < (i * blk + ri)


def sb_fwd(q, k, v, *, name, gather=()):
    t = q.shape[0]
    blk = min(SB_BLOCK, t)
    d = HEAD_DIM
    hs = range(SB_HEADS)
    ng = len(gather)
    groups, nb = HEADS // SB_HEADS, t // blk

    def body(*refs):
        q_ref, k_ref, v_ref = refs[:3]
        o_ref = refs[3 + ng]
        comm = _Gather(refs[4 + ng:4 + 2 * ng], *refs[4 + 2 * ng:]) if ng else None
        i = pl.program_id(1)
        if ng:
            @pl.when((pl.program_id(0) == 0) & (i == 0))
            def _():
                comm.start()

        qb = _Heads(q_ref[:, _head_cols(h)] for h in hs)
        ri = lax.broadcasted_iota(jnp.int32, (blk, blk), 0)
        ci = lax.broadcasted_iota(jnp.int32, (blk, blk), 1)
        upper = (ri > ci).astype(BF16)

        def step(state):
            s, cs, accs = state
            jb = i - s
            rows = pl.ds(pl.multiple_of(jb * blk, blk), blk)
            kb = _Heads(k_ref[rows, _head_cols(h)] for h in hs)
            vb = _Heads(v_ref[rows, _head_cols(h)] for h in hs)
            _, ln, a = _sb_weights(qb, kb, _Heads(cs), _sb_masks(i, jb, blk), upper)
            cs = _Heads(cs) + _hmap(_rowsum, ln)
            accs = _Heads(accs) + _hmap(_bd(NN), a, vb)
            return s + 1, tuple(cs.v), tuple(accs.v)

        init = (jnp.int32(0), tuple(jnp.zeros((blk, 1), F32) for _ in hs), tuple(jnp.zeros((blk, d), F32) for _ in hs))
        _, _, accs = lax.while_loop(lambda st: _sb_alive(st[0], i, st[1]), step, init)
        for h in hs:
            o_ref[:, _head_cols(h)] = accs[h].astype(o_ref.dtype)

        if ng:
            @pl.when((pl.program_id(0) == groups - 1) & (i == nb - 1))
            def _():
                comm.finish()

    qspec = pl.BlockSpec((blk, SB_HEADS * d), lambda g, i: (i, g))
    kspec = pl.BlockSpec((t, SB_HEADS * d), lambda g, i: (0, g))
    return pl.pallas_call(
        body, name=name, grid=(groups, nb), in_specs=[qspec, kspec, kspec] + [ANY] * ng,
        out_specs=[qspec] + [ANY] * ng,
        out_shape=[jax.ShapeDtypeStruct((t, WIDTH), BF16)] + [jax.ShapeDtypeStruct(s.shape, s.dtype) for s in gather],
        input_output_aliases={3 + b: 1 + b for b in range(ng)},
        scratch_shapes=_dma_sems(6 * ng) if ng else [],
        compiler_params=_params(("arbitrary", "arbitrary") if ng else ("parallel", "arbitrary")),
    )(q, k, v, *gather)


def sb_bwd(q, k, v, do, *, name, exchange=()):
    t = q.shape[0]
    blk = min(SB_BLOCK, t)
    d = HEAD_DIM
    nb = t // blk
    hs = range(SB_HEADS)
    ne = len(exchange)
    groups = HEADS // SB_HEADS

    def body(*refs):
        q_ref, k_ref, v_ref, do_ref = refs[:4]
        dq_ref, dk_ref, dv_ref = refs[4 + ne:7 + ne]
        p_buf, z_buf = refs[7 + 2 * ne:9 + 2 * ne]
        comm = _Exchange(refs[4:4 + ne], refs[7 + ne:7 + 2 * ne], *refs[9 + 2 * ne:]) if ne else None
        i = pl.program_id(1)
        if ne:
            @pl.when((pl.program_id(0) == 0) & (i == 0))
            def _():
                comm.start()

        @pl.when(i == 0)
        def _():
            dk_ref[...] = jnp.zeros_like(dk_ref)
            dv_ref[...] = jnp.zeros_like(dv_ref)

        qb = _Heads(q_ref[:, _head_cols(h)] for h in hs)
        dob = _Heads(do_ref[:, _head_cols(h)] for h in hs)
        ri = lax.broadcasted_iota(jnp.int32, (blk, blk), 0)
        ci = lax.broadcasted_iota(jnp.int32, (blk, blk), 1)
        upper = (ri > ci).astype(BF16)
        lower = (ri < ci).astype(BF16)

        def right_to_left(state):
            s, cs = state
            jb = i - s
            rows = pl.ds(pl.multiple_of(jb * blk, blk), blk)
            kb = _Heads(k_ref[rows, _head_cols(h)] for h in hs)
            vb = _Heads(v_ref[rows, _head_cols(h)] for h in hs)
            z, ln, a = _sb_weights(qb, kb, _Heads(cs), _sb_masks(i, jb, blk), upper)
            p = a * _hmap(_bd(NT), dob, vb)
            dv = _hmap(_bd(TN), a, dob)
            for h in hs:
                p_buf[h, jb] = p.v[h]
                z_buf[h, jb] = z.v[h]
                dv_ref[rows, _head_cols(h)] += dv.v[h]
            return s + 1, tuple((_Heads(cs) + _hmap(_rowsum, ln)).v)

        n_done, _ = lax.while_loop(lambda st: _sb_alive(st[0], i, st[1]), right_to_left,
                                   (jnp.int32(0), tuple(jnp.zeros((blk, 1), F32) for _ in hs)))

        def left_to_right(jb, carry):
            cps, dqs = carry
            rows = pl.ds(pl.multiple_of(jb * blk, blk), blk)
            mask = _sb_masks(i, jb, blk)
            kb = _Heads(k_ref[rows, _head_cols(h)] for h in hs)
            p = _Heads(p_buf[h, jb] for h in hs)
            sg = _hmap(_sigmoid, _Heads(z_buf[h, jb] for h in hs))
            prefix = _hmap(lambda a: _dot01_2(a, lower), p) + _Heads(cps)
            dz = _hmap(lambda a: jnp.where(mask, a, 0.0), p * (1.0 - sg) - sg * prefix)
            dk = _hmap(_bd(TN), dz, qb)
            for h in hs:
                dk_ref[rows, _head_cols(h)] += dk.v[h]
            return tuple((_Heads(cps) + _hmap(_rowsum, p)).v), tuple((_Heads(dqs) + _hmap(_bd(NN), dz, kb)).v)

        _, dqs = lax.fori_loop(i + 1 - n_done, i + 1, left_to_right,
                               (tuple(jnp.zeros((blk, 1), F32) for _ in hs), tuple(jnp.zeros((blk, d), F32) for _ in hs)))
        for h in hs:
            dq_ref[:, _head_cols(h)] = dqs[h]

        if ne:
            @pl.when((pl.program_id(0) == groups - 1) & (i == nb - 1))
            def _():
                comm.finish()

    qspec = pl.BlockSpec((blk, SB_HEADS * d), lambda g, i: (i, g))
    kspec = pl.BlockSpec((t, SB_HEADS * d), lambda g, i: (0, g))
    s = jax.ShapeDtypeStruct((t, WIDTH), F32)
    buf = pltpu.VMEM((SB_HEADS, nb, blk, blk), F32)
    return pl.pallas_call(
        body, name=name, grid=(groups, nb), in_specs=[qspec, kspec, kspec, qspec] + [ANY] * ne,
        out_specs=[qspec, kspec, kspec] + [ANY] * ne, out_shape=[s, s, s] + _exchange_shapes(exchange),
        scratch_shapes=[buf, buf] + (_dma_sems(3 * ne) if ne else []),
        compiler_params=_params(("arbitrary", "arbitrary") if ne else ("parallel", "arbitrary")),
    )(q, k, v, do, *exchange)


PACK_COLS = 1024
ANY = pl.BlockSpec(memory_space=pl.ANY)


def _mesh_pos():
    return lax.axis_index("x"), lax.axis_index("y"), lax.axis_index("c")


def _other_chips(x, y):
    return [(1 - x, y), (x, 1 - y), (1 - x, 1 - y)]


def _dma_sems(n):
    return [pltpu.SemaphoreType.DMA((n,)), pltpu.SemaphoreType.DMA((n,))]


class _Gather:
    def __init__(self, o_refs, send_sems, recv_sems):
        self.o_refs, self.send_sems, self.recv_sems = o_refs, send_sems, recv_sems

    def _copy(self, b, k, chip, hf, to):
        rows = self.o_refs[b].at[chip, hf]
        return pltpu.make_async_remote_copy(src_ref=rows, dst_ref=rows, send_sem=self.send_sems.at[6 * b + k],
                                            recv_sem=self.recv_sems.at[6 * b + k], device_id=to, device_id_type=MESH)

    def start(self):
        x, y, c = _mesh_pos()
        for b in range(len(self.o_refs)):
            for k, (cx, cy) in enumerate(_other_chips(x, y)):
                self._copy(b, k, 2 * x + y, c, (cx, cy, c)).start()

    def finish(self):
        x, y, c = _mesh_pos()
        chips = _other_chips(x, y)
        for b in range(len(self.o_refs)):
            for k, (cx, cy) in enumerate(chips):
                self._copy(b, k, 2 * cx + cy, c, (x, y, c)).wait_recv()
                self._copy(b, 3 + k, 2 * cx + cy, c, (x, y, 1 - c)).start()
        for b in range(len(self.o_refs)):
            for k, (cx, cy) in enumerate(chips):
                self._copy(b, 3 + k, 2 * cx + cy, 1 - c, (x, y, c)).wait_recv()
                self._copy(b, k, 2 * x + y, c, (cx, cy, c)).wait_send()
                self._copy(b, 3 + k, 2 * cx + cy, c, (x, y, 1 - c)).wait_send()


def all_gather_chips(slots, *, name):
    nb = len(slots)

    def body(*refs):
        g = _Gather(refs[nb:2 * nb], *refs[2 * nb:])
        g.start()
        g.finish()

    return pl.pallas_call(
        body, name=name, in_specs=[ANY] * nb, out_specs=[ANY] * nb, input_output_aliases={b: b for b in range(nb)},
        out_shape=[jax.ShapeDtypeStruct(s.shape, s.dtype) for s in slots], scratch_shapes=_dma_sems(6 * nb),
    )(*slots)


class _Swap:
    def __init__(self, g_refs, o_refs, send_sems, recv_sems):
        self.g_refs, self.o_refs, self.send_sems, self.recv_sems = g_refs, o_refs, send_sems, recv_sems

    def _copies(self):
        x, y, c = _mesh_pos()
        return [pltpu.make_async_remote_copy(src_ref=self.g_refs[b].at[j, 1 - c], dst_ref=self.o_refs[b].at[j],
                                             send_sem=self.send_sems.at[N_CHIPS * b + j],
                                             recv_sem=self.recv_sems.at[N_CHIPS * b + j],
                                             device_id=(x, y, 1 - c), device_id_type=MESH)
                for b in range(len(self.g_refs)) for j in range(N_CHIPS)]

    def start(self):
        for cp in self._copies():
            cp.start()

    def finish(self):
        for cp in self._copies():
            cp.wait()


def _swap_shapes(gs):
    return [jax.ShapeDtypeStruct((g.shape[0],) + g.shape[2:], g.dtype) for g in gs]


def sibling_swap(gs, *, name):
    nb = len(gs)

    def body(*refs):
        comm = _Swap(refs[:nb], refs[nb:2 * nb], *refs[2 * nb:])
        comm.start()
        comm.finish()

    return pl.pallas_call(
        body, name=name, in_specs=[ANY] * nb, out_specs=[ANY] * nb, out_shape=_swap_shapes(gs),
        scratch_shapes=_dma_sems(N_CHIPS * nb),
    )(*gs)


class _Exchange:
    def __init__(self, s_refs, o_refs, send_sems, recv_sems):
        self.s_refs, self.o_refs, self.send_sems, self.recv_sems = s_refs, o_refs, send_sems, recv_sems

    def _copies(self):
        x, y, c = _mesh_pos()
        return [pltpu.make_async_remote_copy(src_ref=self.s_refs[b].at[2 * cx + cy], dst_ref=self.o_refs[b].at[k],
                                             send_sem=self.send_sems.at[3 * b + k], recv_sem=self.recv_sems.at[3 * b + k],
                                             device_id=(cx, cy, c), device_id_type=MESH)
                for b in range(len(self.s_refs)) for k, (cx, cy) in enumerate(_other_chips(x, y))]

    def start(self):
        for cp in self._copies():
            cp.start()

    def finish(self):
        for cp in self._copies():
            cp.wait()


def _exchange_shapes(s1s):
    return [jax.ShapeDtypeStruct((3,) + s.shape[1:], s.dtype) for s in s1s]


def chip_exchange(s1s, *, name):
    nb = len(s1s)

    def body(*refs):
        comm = _Exchange(refs[:nb], refs[nb:2 * nb], *refs[2 * nb:])
        comm.start()
        comm.finish()

    return pl.pallas_call(
        body, name=name, in_specs=[ANY] * nb, out_specs=[ANY] * nb, out_shape=_exchange_shapes(s1s),
        scratch_shapes=_dma_sems(3 * nb),
    )(*s1s)


def sibling_merge(halves, *, name):
    nb = len(halves)

    def body(*refs):
        o_refs, (send_sems, recv_sems) = refs[nb:2 * nb], refs[2 * nb:]
        x, y, c = _mesh_pos()
        cps = [pltpu.make_async_remote_copy(src_ref=o_refs[b].at[c], dst_ref=o_refs[b].at[c], send_sem=send_sems.at[b],
                                            recv_sem=recv_sems.at[b], device_id=(x, y, 1 - c), device_id_type=MESH)
               for b in range(nb)]
        for cp in cps:
            cp.start()
        for cp in cps:
            cp.wait()

    return pl.pallas_call(
        body, name=name, in_specs=[ANY] * nb, out_specs=[ANY] * nb, input_output_aliases={b: b for b in range(nb)},
        out_shape=[jax.ShapeDtypeStruct(h.shape, h.dtype) for h in halves], scratch_shapes=_dma_sems(nb),
    )(*halves)


def all_reduce_small(buf, *, name):
    n_dev = 8

    def body(b_ref, o_ref, recv_buf, send_sems, recv_sems):
        x, y, c = _mesh_pos()
        me = 4 * x + 2 * y + c
        pos = lambda t: (t // 4, (t // 2) % 2, t % 2)

        def copy(t, slot):
            return pltpu.make_async_remote_copy(src_ref=b_ref, dst_ref=recv_buf.at[slot], send_sem=send_sems.at[t],
                                                recv_sem=recv_sems.at[slot], device_id=pos(t), device_id_type=MESH)

        for t in range(n_dev):
            @pl.when(t != me)
            def _(t=t):
                copy(t, me).start()

        recv_buf[me] = b_ref[...]
        for t in range(n_dev):
            @pl.when(t != me)
            def _(t=t):
                copy(t, t).wait_recv()
                copy(t, me).wait_send()

        acc = recv_buf[0]
        for t in range(1, n_dev):
            acc = acc + recv_buf[t]
        o_ref[...] = acc

    return pl.pallas_call(
        body, name=name, out_shape=jax.ShapeDtypeStruct(buf.shape, F32),
        in_specs=[pl.BlockSpec(memory_space=pltpu.VMEM)], out_specs=pl.BlockSpec(memory_space=pltpu.VMEM),
        scratch_shapes=[pltpu.VMEM((n_dev,) + buf.shape, F32), pltpu.SemaphoreType.DMA((n_dev,)),
                        pltpu.SemaphoreType.DMA((n_dev,))],
    )(buf)


REDUCE_ROWS = (512, 384, 256, 128)


def add_selected(sel, a5, b, *, name):
    n, _, rh, w = a5.shape
    tr = _pick(rh, REDUCE_ROWS)

    def body(sel_ref, a_ref, b_ref, o_ref, ob_ref):
        s = a_ref[...] + b_ref[...]
        o_ref[...] = s
        ob_ref[...] = s.astype(BF16)

    blk = pl.BlockSpec((None, tr, w), lambda j, i, s: (j, i, 0))
    return pl.pallas_call(
        body, name=name,
        grid_spec=pltpu.PrefetchScalarGridSpec(
            num_scalar_prefetch=1, grid=(n, rh // tr),
            in_specs=[pl.BlockSpec((None, None, tr, w), lambda j, i, s: (j, s[0], i, 0)), blk], out_specs=[blk, blk]),
        out_shape=[jax.ShapeDtypeStruct((n, rh, w), F32), jax.ShapeDtypeStruct((n, rh, w), BF16)],
        compiler_params=_params(("arbitrary", "arbitrary")),
    )(sel, a5, b)


def add_chip_sums(sel, s1, b2, *, name):
    _, rh, w = s1.shape
    tr = _pick(rh, REDUCE_ROWS)

    def body(sel_ref, s_ref, b_ref, o_ref):
        o_ref[...] = ((s_ref[...] + b_ref[0].astype(F32)) + b_ref[1].astype(F32)) + b_ref[2].astype(F32)

    return pl.pallas_call(
        body, name=name,
        grid_spec=pltpu.PrefetchScalarGridSpec(
            num_scalar_prefetch=1, grid=(rh // tr,),
            in_specs=[pl.BlockSpec((None, tr, w), lambda i, s: (s[0], i, 0)), pl.BlockSpec((3, tr, w), lambda i, s: (0, i, 0))],
            out_specs=pl.BlockSpec((None, tr, w), lambda i, s: (s[1], i, 0))),
        out_shape=jax.ShapeDtypeStruct((2, rh, w), F32),
        compiler_params=_params(("arbitrary",)),
    )(sel, s1, b2)


BIG = (("gdn_w_out", 1), ("sb_w_q", 1), ("sb_w_out", 1), ("ffn_w_out", 1), ("ple_w_gate", 1), ("w_kv", 1),
       ("ple_w_proj", 2))
SMALL = ("ln_mix", "ln_ffn", "ln_ple", "gdn_a_log", "gdn_dt_bias", "gdn_norm", "kv_norm", "k_norm", "sb_q_norm")
WEIGHTS = ("ln_mix", "ln_ffn", "ln_ple", "gdn_w_in", "gdn_conv", "gdn_a_log", "gdn_dt_bias", "gdn_norm", "gdn_w_out",
           "kv_norm", "w_kv", "k_norm", "sb_w_q", "sb_q_norm", "sb_w_out", "ffn_w_in", "ffn_w_out", "ple_w_proj",
           "ple_w_gate")
PACK_ALIGN = 256


ROW_TILE = 16


def _rows_of(shape, tile=ROW_TILE):
    return -(-math.prod(shape) // (PACK_COLS * tile)) * tile


WEIGHT_ALIGN = 32


def _pack_rows(arrs, lead, align=PACK_ALIGN):
    parts = []
    for a in arrs:
        if a.shape[-1] == PACK_COLS:
            parts.append(a.reshape(lead + (-1, PACK_COLS)))
            continue
        flat = a.reshape(lead + (-1,))
        pad = _rows_of(a.shape[len(lead):]) * PACK_COLS - flat.shape[-1]
        if pad:
            flat = jnp.pad(flat, [(0, 0)] * len(lead) + [(0, pad)])
        parts.append(flat.reshape(lead + (-1, PACK_COLS)))
    rows = sum(q.shape[len(lead)] for q in parts)
    filler = -rows % align
    if filler:
        parts.append(jnp.zeros(lead + (filler, PACK_COLS), parts[0].dtype))
    return jnp.concatenate(parts, axis=len(lead))


def _own_slot(buf, chip):
    mine = lax.broadcasted_iota(jnp.int32, (N_CHIPS, 1, 1), 0) == chip
    slots = jnp.where(mine, buf[None], jnp.zeros((), buf.dtype))
    return slots.reshape(N_CHIPS, 2, buf.shape[0] // 2, buf.shape[1])


def _unpack_rows(buf, shapes, lead):
    out, r0 = [], 0
    for s in shapes:
        rows = _rows_of(s)
        flat = buf[(slice(None),) * len(lead) + (slice(r0, r0 + rows),)].reshape(lead + (-1,))
        out.append(flat[..., :math.prod(s)].reshape(lead + tuple(s)))
        r0 += rows
    return out


def _unshard(g, axis):
    g = jnp.moveaxis(g, 0, axis)
    s = g.shape
    return g.reshape(s[:axis] + (s[axis] * s[axis + 1],) + s[axis + 2:])


def _shard(full, axis):
    s = full.shape
    return jnp.moveaxis(full.reshape(s[:axis] + (N_CHIPS, s[axis] // N_CHIPS) + s[axis + 1:]), axis, 0)


def _to4(a):
    return a.reshape(HEADS, -1, 1, CHUNK).transpose(1, 0, 2, 3)


def _from4(a):
    return a.transpose(1, 0, 2, 3).reshape(HEADS, -1)


def _row(vec):
    flat = vec.reshape(-1)
    rows = _rows_of(flat.shape, 1)
    return jnp.pad(flat, (0, rows * PACK_COLS - flat.shape[0])).reshape(rows, PACK_COLS)


def kernel(x, p, ln_mix, ln_ffn, ln_ple, gdn_w_in, gdn_conv, gdn_a_log, gdn_dt_bias, gdn_norm, gdn_w_out, kv_norm, w_kv, k_norm, sb_w_q, sb_q_norm, sb_w_out, ffn_w_in, ffn_w_out, ple_w_proj, ple_w_gate, loss_target, m_ln_mix, m_ln_ffn, m_ln_ple, m_gdn_w_in, m_gdn_conv, m_gdn_a_log, m_gdn_dt_bias, m_gdn_norm, m_gdn_w_out, m_kv_norm, m_w_kv, m_k_norm, m_sb_w_q, m_sb_q_norm, m_sb_w_out, m_ffn_w_in, m_ffn_w_out, m_ple_w_proj, m_ple_w_gate, v_ln_mix, v_ln_ffn, v_ln_ple, v_gdn_w_in, v_gdn_conv, v_gdn_a_log, v_gdn_dt_bias, v_gdn_norm, v_gdn_w_out, v_kv_norm, v_w_kv, v_k_norm, v_sb_w_q, v_sb_q_norm, v_sb_w_out, v_ffn_w_in, v_ffn_w_out, v_ple_w_proj, v_ple_w_gate):
    w = dict(ln_mix=ln_mix, ln_ffn=ln_ffn, ln_ple=ln_ple, gdn_w_in=gdn_w_in, gdn_conv=gdn_conv, gdn_a_log=gdn_a_log,
             gdn_dt_bias=gdn_dt_bias, gdn_norm=gdn_norm, gdn_w_out=gdn_w_out, kv_norm=kv_norm, w_kv=w_kv, k_norm=k_norm,
             sb_w_q=sb_w_q, sb_q_norm=sb_q_norm, sb_w_out=sb_w_out, ffn_w_in=ffn_w_in, ffn_w_out=ffn_w_out,
             ple_w_proj=ple_w_proj, ple_w_gate=ple_w_gate)
    mom1 = dict(ln_mix=m_ln_mix, ln_ffn=m_ln_ffn, ln_ple=m_ln_ple, gdn_w_in=m_gdn_w_in, gdn_conv=m_gdn_conv,
                gdn_a_log=m_gdn_a_log, gdn_dt_bias=m_gdn_dt_bias, gdn_norm=m_gdn_norm, gdn_w_out=m_gdn_w_out,
                kv_norm=m_kv_norm, w_kv=m_w_kv, k_norm=m_k_norm, sb_w_q=m_sb_w_q, sb_q_norm=m_sb_q_norm,
                sb_w_out=m_sb_w_out, ffn_w_in=m_ffn_w_in, ffn_w_out=m_ffn_w_out, ple_w_proj=m_ple_w_proj,
                ple_w_gate=m_ple_w_gate)
    mom2 = dict(ln_mix=v_ln_mix, ln_ffn=v_ln_ffn, ln_ple=v_ln_ple, gdn_w_in=v_gdn_w_in, gdn_conv=v_gdn_conv,
                gdn_a_log=v_gdn_a_log, gdn_dt_bias=v_gdn_dt_bias, gdn_norm=v_gdn_norm, gdn_w_out=v_gdn_w_out,
                kv_norm=v_kv_norm, w_kv=v_w_kv, k_norm=v_k_norm, sb_w_q=v_sb_w_q, sb_q_norm=v_sb_q_norm,
                sb_w_out=v_sb_w_out, ffn_w_in=v_ffn_w_in, ffn_w_out=v_ffn_w_out, ple_w_proj=v_ple_w_proj,
                ple_w_gate=v_ple_w_gate)
    depth = ln_mix.shape[0]
    n_a = gdn_w_in.shape[0]
    xi, yi, ci = _mesh_pos()
    chip = 2 * xi + yi
    sel_c = jnp.reshape(ci, (1,)).astype(jnp.int32)
    sel_chip = jnp.stack([chip, ci]).astype(jnp.int32)
    h = x[0]
    tgt = loss_target[0]
    t = h.shape[0]

    shard_shapes = [w[n].shape for n, _ in BIG]

    def layer_items(i):
        if i < n_a:
            items = [("gdn_w_out", i, 0), ("ffn_w_out", i, 0), ("ple_w_gate", i, 0), ("ple_w_proj", i, 1)]
            return items + ([("w_kv", None, 1)] if i == n_a - 1 else [])
        j = i - n_a
        return [("sb_w_q", j, 0), ("sb_w_out", j, 0), ("ffn_w_out", i, 0), ("ple_w_gate", i, 0), ("ple_w_proj", i, 1)]

    def layer_shards(i):
        return [w[n] if idx is None else w[n][idx] for n, idx, _ in layer_items(i)]

    def layer_slots(i):
        packed = _pack_rows([q.astype(BF16) for q in layer_shards(i)], (), align=WEIGHT_ALIGN)
        own = [packed, ffn_w_in[i].astype(BF16)] + ([gdn_w_in[i].astype(BF16)] if i < n_a else [])
        return [_own_slot(b, chip) for b in own]

    def layer_weights(i, got):
        parts = _unpack_rows(got[0].reshape(N_CHIPS, -1, PACK_COLS), [q.shape for q in layer_shards(i)], (N_CHIPS,))
        out = {n: _unshard(g, ax) for (n, _, ax), g in zip(layer_items(i), parts)}
        out["ffn_w_in"] = got[1].reshape((N_CHIPS, 1) + ffn_w_in.shape[1:])
        if i < n_a:
            out["gdn_w_in"] = _unshard(got[2].reshape((N_CHIPS,) + gdn_w_in.shape[1:]), 1)
        return out

    slots = [layer_slots(i) for i in range(depth)]
    wl = [layer_weights(0, all_gather_chips(slots[0], name="all_gather_weights"))]
    conv_rows = _rows_of(gdn_conv.shape, 1)
    small_rows = sum(_rows_of(w[n].shape, 1) for n in SMALL)
    buf_rows = -(-(small_rows + N_CHIPS * conv_rows) // 8) * 8
    conv_buf = jnp.zeros((buf_rows, PACK_COLS), F32)
    conv_buf = lax.dynamic_update_slice(conv_buf, _row(gdn_conv) * (ci == 0).astype(F32), (chip * conv_rows, 0))
    conv_all = all_reduce_small(conv_buf, name="all_reduce_small")[:N_CHIPS * conv_rows]
    conv_full = _unshard(conv_all.reshape(N_CHIPS, -1)[:, :math.prod(gdn_conv.shape)].reshape((N_CHIPS,) + gdn_conv.shape), 2)

    def resid_rms(r, res, g):
        out = r + res
        return out, _rms(out, g)

    def ple_rms(r, res, emb, g):
        out = res + emb * _sigmoid(r)
        return r, out, (None if g is None else _rms(out, g))

    hn_next = None
    saved = []
    k_sh = v_sh = None
    mid = None
    for i in range(depth):
        s = dict(h0=h)
        wi = wl[i]
        nxt = slots[i + 1] if i + 1 < depth else ()
        s["hn"] = hn = rms_fwd(h, ln_mix[i:i + 1], name="rms_fwd") if hn_next is None else hn_next
        if i < n_a:
            w_in = wi["gdn_w_in"]
            s["w_m"], s["w_abt"] = w_in[:, :4 * WIDTH], w_in[:, 4 * WIDTH:].T
            s["proj"] = proj = matmul(hn, s["w_m"], "nn", out_dtype=BF16, name="mm_gdn_in")
            s["ab"] = ab = matmul(s["w_abt"], hn, "nt", name="mm_gdn_ab")
            s["a_log"], s["dt"] = gdn_a_log[i][:, None], gdn_dt_bias[i][:, None]
            g8, b8 = gates_fwd(ab, s["a_log"], s["dt"], name="gates_fwd")
            s["g4"], s["b4"] = _to4(g8), _to4(b8)
            s["qkv"] = qkv = conv_fwd(proj, conv_full[i], name="conv_fwd")
            s["o"], s["s0"], s["tinv"], *got = gdn_fwd(qkv, s["g4"], s["b4"], gather=nxt, name="gdn_fwd")
            s["y"] = mixed = gatenorm_fwd(s["o"], proj, gdn_norm[i:i + 1], name="gatenorm_fwd")
            w_mix_out = wi["gdn_w_out"]
        else:
            j = i - n_a
            s["qraw"] = qraw = matmul(hn, wi["sb_w_q"], "nn", name="mm_sq")
            s["q"] = q = headnorm_fwd(qraw, 0, sb_q_norm[j:j + 1], HEAD_DIM ** -0.5, name="headnorm_q")
            s["o"], *got = sb_fwd(q, k_sh, v_sh, gather=nxt, name="sb_fwd")
            mixed, w_mix_out = s["o"], wi["sb_w_out"]
        if nxt:
            wl.append(layer_weights(i + 1, got))
        s["h1"], s["hn2"] = h, hn2 = matmul_rows(mixed, w_mix_out, [h], [ln_ffn[i:i + 1]], resid_rms, (F32, BF16),
                                                 name="mm_out_rms")
        s["gu"], s["act"] = ffn_in_swiglu(hn2, wi["ffn_w_in"], name="ffn_in_swiglu")
        s["h2"], s["hn3"] = h, hn3 = matmul_rows(s["act"], wi["ffn_w_out"], [h], [ln_ple[i:i + 1]], resid_rms,
                                                 (F32, BF16), name="mm_ffn_out_rms")
        s["pp"] = pp = matmul(p[i, 0], wi["ple_w_proj"], "nn", name="mm_ple_proj")
        if i + 1 < depth:
            s["gt"], h, hn_next = matmul_rows(hn3, wi["ple_w_gate"], [h, pp], [ln_mix[i + 1:i + 2]], ple_rms,
                                              (F32, F32, BF16), name="mm_ple_gate_rms")
        else:
            s["gt"], h = matmul_rows(hn3, wi["ple_w_gate"], [h, pp], [], lambda r, res, e: ple_rms(r, res, e, None)[:2],
                                     (F32, F32), name="mm_ple_gate")
        saved.append(s)
        if i == n_a - 1:
            mid = dict(h=h)
            mid["hk"] = hk = rms_fwd(h, kv_norm[None, :], name="rms_fwd")
            mid["kv"] = kv = matmul(hk, wi["w_kv"], "nn", name="mm_kv")
            k_sh = headnorm_fwd(kv, 0, k_norm[None, :], 1.0, name="headnorm_k")
            v_sh = kv[:, WIDTH:].astype(BF16)

    dh, sq = loss_head(h, tgt, name="loss_head")
    loss = lax.psum(0.5 * jnp.sum(sq) / h.shape[1], ("x", "y", "c"))

    gw = {n: [None] * w[n].shape[0] for n in WEIGHTS if w[n].ndim >= 2 and n not in ("w_kv",)}
    dks, dvs = [], []
    reduced = [None] * depth

    def grad_buffers(i):
        pieces = [_shard(gw[n] if idx is None else gw[n][idx], ax) for n, idx, ax in layer_items(i)]
        bufs = [_pack_rows(pieces, (N_CHIPS,)), gw["ffn_w_in"][i]] + ([_shard(gw["gdn_w_in"][i], 1)] if i < n_a else [])
        return [g.reshape(N_CHIPS, 2, g.shape[1] // 2, g.shape[2]) for g in bufs]

    def chip_sums(g5s, from_sibling):
        return [add_selected(sel_c, g5, fs, name="add_selected") for g5, fs in zip(g5s, from_sibling)]

    def reduce_end(i, s1s, from_chips):
        s2s = [add_chip_sums(sel_chip, s1, fc, name="add_chip_sums") for (s1, _), fc in zip(s1s, from_chips)]
        reduced[i] = sibling_merge(s2s, name="sibling_merge")

    waiting = None
    for i in reversed(range(depth)):
        s = saved[i]
        if i == n_a - 1:
            dkraw, gw["k_norm"] = headnorm_bwd(mid["kv"], 0, k_norm[None, :], 1.0, tuple(dks), name="headnorm_k_bwd")
            dkv = jnp.concatenate([dkraw, sum_cast(dvs, BF16, name="sum_dv")], axis=1)
            dhk = matmul(dkv, wl[i]["w_kv"], "nt", name="mm_kv_dx")
            gw["w_kv"] = matmul(mid["hk"], dkv, "tn", name="mm_kv_dw")
            dh, gw["kv_norm"] = rms_bwd(mid["h"], kv_norm[None, :], dhk, dh, name="rms_bwd")
        dpp, dgt = ple_bwd(dh, s["pp"], s["gt"], name="ple_bwd")
        gw["ple_w_proj"][i] = matmul(p[i, 0], dpp, "tn", name="mm_ple_proj_dw")
        gw["ple_w_gate"][i] = matmul(s["hn3"], dgt, "tn", name="mm_sq_dw")
        dhn3 = matmul(dgt, wl[i]["ple_w_gate"], "nt", name="mm_sq_dx")
        dh, gw["ln_ple"][i] = rms_bwd(s["h2"], ln_ple[i:i + 1], dhn3, dh, name="rms_bwd")
        dgu, *from_sibling = ffn_out_dx_swiglu(dh, wl[i]["ffn_w_out"], s["gu"], swap=waiting[1] if waiting else (),
                                               name="ffn_out_dx_swiglu")
        sums = chip_sums(waiting[1], from_sibling) if waiting else []
        riding = [s1b for _, s1b in sums]
        gw["ffn_w_out"][i] = matmul(s["act"], dh, "tn", name="mm_ffn_out_dw")
        dhn2 = matmul(dgu, wl[i]["ffn_w_in"], "nt", b_chips=0, name="mm_ffn_in_dx")
        gw["ffn_w_in"][i] = matmul(s["hn2"], dgu, "tn", out_chips=True, name="mm_ffn_in_dw")
        dh, gw["ln_ffn"][i] = rms_bwd(s["h1"], ln_ffn[i:i + 1], dhn2, dh, name="rms_bwd")
        if i < n_a:
            dy = matmul(dh, wl[i]["gdn_w_out"], "nt", name="mm_sq_dx")
            gw["gdn_w_out"][i] = matmul(s["y"], dh, "tn", name="mm_sq_dw")
            do, dproj, gw["gdn_norm"][i] = gatenorm_bwd(s["o"], s["proj"], gdn_norm[i:i + 1], dy, name="gatenorm_bwd")
            dqkv, dg4, db4, *from_chips = gdn_bwd(s["qkv"], s["g4"], s["b4"], s["s0"], s["tinv"], do, exchange=riding,
                                                  name="gdn_bwd")
            dab, dal, ddt = gates_bwd(s["ab"], s["a_log"], s["dt"], _from4(dg4), _from4(db4), name="gates_bwd")
            gw["gdn_a_log"][i], gw["gdn_dt_bias"][i] = dal[:, 0], ddt[:, 0]
            dproj, gw["gdn_conv"][i] = conv_bwd(s["proj"], conv_full[i], dqkv, dproj, name="conv_bwd")
            dhn = matmul(dproj, s["w_m"], "nt", name="mm_gdn_in_dx")
            dhn = matmul(dab, s["w_abt"], "tn", add=dhn, name="mm_gdn_ab_dx")
            dwm = matmul(s["hn"], dproj, "tn", name="mm_gdn_in_dw")
            dwab = matmul(dab, s["hn"], "nn", name="mm_gdn_ab_dw")
            gw["gdn_w_in"][i] = jnp.concatenate([dwm, dwab.T], axis=1)
        else:
            j = i - n_a
            do = matmul(dh, wl[i]["sb_w_out"], "nt", out_dtype=BF16, name="mm_sb_out_dx")
            gw["sb_w_out"][j] = matmul(s["o"], dh, "tn", name="mm_sq_dw")
            dq, dk, dv, *from_chips = sb_bwd(s["q"], k_sh, v_sh, do, exchange=riding, name="sb_bwd")
            dks.append(dk)
            dvs.append(dv)
            dqraw, gw["sb_q_norm"][j] = headnorm_bwd(s["qraw"], 0, sb_q_norm[j:j + 1], HEAD_DIM ** -0.5, (dq,),
                                                    name="headnorm_q_bwd")
            dhn = matmul(dqraw, wl[i]["sb_w_q"], "nt", name="mm_sq_dx")
            gw["sb_w_q"][j] = matmul(s["hn"], dqraw, "tn", name="mm_sq_dw")
        dh, gw["ln_mix"][i] = rms_bwd(s["h0"], ln_mix[i:i + 1], dhn, dh, name="rms_bwd")
        if waiting:
            reduce_end(waiting[0], sums, from_chips)
        waiting = (i, grad_buffers(i))
    sums = chip_sums(waiting[1], sibling_swap(waiting[1], name="sibling_swap"))
    reduce_end(waiting[0], sums, chip_exchange([s1b for _, s1b in sums], name="chip_exchange"))
    grad_x = dh[None]

    def stacked(n):
        g = gw[n]
        if isinstance(g, list):
            g = jnp.stack([a.reshape(w[n].shape[1:]) if n in SMALL else a for a in g])
        return g

    small_buf = jnp.concatenate([_row(stacked(n)) for n in SMALL] + [_row(stacked("gdn_conv"))], axis=0)
    small_buf = jnp.pad(small_buf, ((0, buf_rows - small_buf.shape[0]), (0, 0)))
    small_sum = all_reduce_small(small_buf, name="all_reduce_small")
    grads = {}
    r0 = 0
    for n in SMALL:
        rows = _rows_of(w[n].shape, 1)
        grads[n] = small_sum[r0:r0 + rows].reshape(-1)[:math.prod(w[n].shape)].reshape(w[n].shape)
        r0 += rows
    conv_g = small_sum[r0:r0 + N_CHIPS * conv_rows].reshape(-1)[:N_CHIPS * math.prod(gdn_conv.shape)]
    conv_g = conv_g.reshape((gdn_conv.shape[0], CONV_WIDTH, N_CHIPS, gdn_conv.shape[2]))
    grads["gdn_conv"] = lax.dynamic_index_in_dim(conv_g, chip, axis=2, keepdims=False)

    per_layer = {n: [None] * w[n].shape[0] for n, _ in BIG if n != "w_kv"}
    for i in range(depth):
        parts = _unpack_rows(reduced[i][0].reshape(-1, PACK_COLS), [q.shape for q in layer_shards(i)], ())
        for (n, idx, _), g in zip(layer_items(i), parts):
            if idx is None:
                grads[n] = g
            else:
                per_layer[n][idx] = g
    for n, parts in per_layer.items():
        grads[n] = jnp.stack(parts)
    grads["ffn_w_in"] = jnp.stack([reduced[i][1].reshape(ffn_w_in.shape[1:]) for i in range(depth)])
    grads["gdn_w_in"] = jnp.stack([reduced[i][2].reshape(gdn_w_in.shape[1:]) for i in range(n_a)])

    delta, new_m, new_v = {}, {}, {}
    for n in WEIGHTS:
        delta[n], new_m[n], new_v[n] = adamw(w[n], grads[n], mom1[n], mom2[n], name="adamw")
    return (loss, grad_x, *[grads[n] for n in WEIGHTS], *[delta[n] for n in WEIGHTS],
            *[new_m[n] for n in WEIGHTS], *[new_v[n] for n in WEIGHTS])
```

```python
import math

import jax
import jax.numpy as jnp
from jax import lax
from jax.experimental import pallas as pl
from jax.experimental.pallas import tpu as pltpu

F32 = jnp.float32
BF16 = jnp.bfloat16
EPS = 1e-6
HEADS = 8
HEAD_DIM = 128
WIDTH = HEADS * HEAD_DIM
CHUNK = 64
CONV_WIDTH = 4
N_CHIPS = 4
ADAM_LR, ADAM_B1, ADAM_B2, ADAM_EPS, ADAM_WD, ADAM_STEP = 0.001, 0.9, 0.999, 1e-08, 0.01, 10
V7X_VMEM_BYTES = 64 * 1024 * 1024
VMEM_LIMIT = V7X_VMEM_BYTES - 8 * 1024 * 1024
MESH = pl.DeviceIdType.MESH


def _params(sem=None):
    return pltpu.CompilerParams(dimension_semantics=sem, vmem_limit_bytes=VMEM_LIMIT)


def _pick(n, prefs):
    for t in prefs:
        if t <= n and n % t == 0:
            return t
    return n


def _bdot(a, b, dims):
    return lax.dot_general(a.astype(BF16), b.astype(BF16), (((dims[0],), (dims[1],)), ((), ())),
                           preferred_element_type=F32)


NN, NT, TN = (1, 0), (1, 1), (0, 0)


MM_TILES = (1024, 1408, 512, 256, 128)


def matmul(a, b, form, *, out_dtype=F32, add=None, name, b_chips=None, out_chips=False):
    ns = None
    if b_chips is not None:
        ns = b.shape[3]
        b_shape = (b.shape[2], N_CHIPS * ns)
    else:
        b_shape = b.shape
    if form == "nn":
        (m, k), (k2, n) = a.shape, b_shape
    elif form == "nt":
        (m, k), (n, k2) = a.shape, b_shape
    else:
        (k, m), (k2, n) = a.shape, b_shape
    assert k == k2, (a.shape, b.shape, form)
    if out_chips:
        ns = n // N_CHIPS
    tm = _pick(m, MM_TILES)
    tn = _pick(n, MM_TILES)
    tk = k if k <= 1024 else _pick(k, MM_TILES)
    if ns is not None and (form == "nn" or out_chips):
        tn = ns
    if ns is not None and form == "nt":
        tk = ns
    nk = k // tk
    if form == "tn":
        a_spec = pl.BlockSpec((tk, tm), lambda i, j, kk: (kk, i))
    else:
        a_spec = pl.BlockSpec((tm, tk), lambda i, j, kk: (i, kk))
    if b_chips is not None and form == "nn":
        b_spec = pl.BlockSpec((None, None, tk, ns), lambda i, j, kk: (j, b_chips, kk, 0))
    elif b_chips is not None:
        b_spec = pl.BlockSpec((None, None, tn, ns), lambda i, j, kk: (kk, b_chips, j, 0))
    elif form == "nt":
        b_spec = pl.BlockSpec((tn, tk), lambda i, j, kk: (j, kk))
    else:
        b_spec = pl.BlockSpec((tk, tn), lambda i, j, kk: (kk, j))
    if out_chips:
        o_spec = pl.BlockSpec((None, tm, ns), lambda i, j, kk: (j, i, 0))
    else:
        o_spec = pl.BlockSpec((tm, tn), lambda i, j, kk: (i, j))
    dims = {"nn": NN, "nt": NT, "tn": TN}[form]
    has_add = add is not None

    def body(*refs):
        a_ref, b_ref = refs[:2]
        add_ref = refs[2] if has_add else None
        o_ref = refs[2 + has_add]

        def finish(r):
            if has_add:
                r = r + add_ref[...].astype(F32)
            o_ref[...] = r.astype(out_dtype)

        part = _bdot(a_ref[...], b_ref[...], dims)
        if nk == 1:
            finish(part)
            return
        acc_ref = refs[3 + has_add]
        kk = pl.program_id(2)

        @pl.when(kk == 0)
        def _():
            acc_ref[...] = part

        @pl.when(kk > 0)
        def _():
            acc_ref[...] += part

        @pl.when(kk == nk - 1)
        def _():
            finish(acc_ref[...])

    in_specs = [a_spec, b_spec] + ([o_spec] if has_add else [])
    args = (a, b) + ((add,) if has_add else ())
    return pl.pallas_call(
        body, name=name, grid=(m // tm, n // tn, nk), in_specs=in_specs, out_specs=o_spec,
        out_shape=jax.ShapeDtypeStruct((N_CHIPS, m, ns) if out_chips else (m, n), out_dtype),
        scratch_shapes=[pltpu.VMEM((tm, tn), F32)] if nk > 1 else [],
        compiler_params=_params(("parallel", "parallel", "arbitrary")),
    )(*args)


def matmul_rows(a, b, extra, params, epilogue, out_dtypes, *, name):
    (m, k), (k2, n) = a.shape, b.shape
    assert k == k2, (a.shape, b.shape)
    tm = _pick(m, (1024, 512, 256, 128))
    tk = k if k <= 1024 else _pick(k, MM_TILES)
    nk = k // tk
    ne, npar, no = len(extra), len(params), len(out_dtypes)

    def body(*refs):
        a_ref, b_ref = refs[:2]
        e_refs, p_refs = refs[2:2 + ne], refs[2 + ne:2 + ne + npar]
        o_refs = refs[2 + ne + npar:2 + ne + npar + no]

        def finish(r):
            outs = epilogue(r, *[e[...] for e in e_refs], *[q[...] for q in p_refs])
            for o_ref, val in zip(o_refs, outs):
                o_ref[...] = val.astype(o_ref.dtype)

        part = _bdot(a_ref[...], b_ref[...], NN)
        if nk == 1:
            finish(part)
            return
        acc_ref = refs[-1]
        kk = pl.program_id(1)

        @pl.when(kk == 0)
        def _():
            acc_ref[...] = part

        @pl.when(kk > 0)
        def _():
            acc_ref[...] += part

        @pl.when(kk == nk - 1)
        def _():
            finish(acc_ref[...])

    row = pl.BlockSpec((tm, n), lambda i, kk: (i, 0))
    in_specs = ([pl.BlockSpec((tm, tk), lambda i, kk: (i, kk)), pl.BlockSpec((tk, n), lambda i, kk: (kk, 0))]
                + [row] * ne + [pl.BlockSpec((1, n), lambda i, kk: (0, 0))] * npar)
    return pl.pallas_call(
        body, name=name, grid=(m // tm, nk), in_specs=in_specs, out_specs=[row] * no,
        out_shape=[jax.ShapeDtypeStruct((m, n), dt) for dt in out_dtypes],
        scratch_shapes=[pltpu.VMEM((tm, n), F32)] if nk > 1 else [],
        compiler_params=_params(("parallel", "arbitrary")),
    )(a, b, *extra, *params)


def _const(c):
    return lambda j: c


def rowwise(fn, rows, params, outs, accs=(), *, name, tm, ncol=1):
    t = rows[0][0].shape[0]
    tm = min(tm, t)
    assert t % tm == 0
    n_rows, n_par, n_out, n_acc = len(rows), len(params), len(outs), len(accs)

    def body(*refs):
        j, i = pl.program_id(0), pl.program_id(1)
        ins = [r[...] for r in refs[:n_rows + n_par]]
        o_refs = refs[n_rows + n_par:n_rows + n_par + n_out]
        a_refs = refs[n_rows + n_par + n_out:]
        row_outs, acc_outs = fn(*ins)
        for r, val in zip(o_refs, row_outs):
            r[...] = val.astype(r.dtype)
        for r, val, spec in zip(a_refs, acc_outs, accs):
            first = (i == 0) & (j == 0) if spec[4] else (i == 0)

            @pl.when(first)
            def _(r=r, val=val):
                r[...] = val.astype(F32)

            @pl.when(jnp.logical_not(first))
            def _(r=r, val=val):
                r[...] += val.astype(F32)

    in_specs = [pl.BlockSpec((tm, w), lambda j, i, cf=cf: (i, cf(j))) for _, w, cf in rows]
    in_specs += [pl.BlockSpec((p.shape[0], w), lambda j, i, cf=cf: (0, cf(j))) for p, w, cf in params]
    out_specs = [pl.BlockSpec((tm, w), lambda j, i, cf=cf: (i, cf(j))) for _, _, w, cf in outs]
    out_specs += [pl.BlockSpec((r, w), lambda j, i, cf=cf: (0, cf(j))) for r, _, w, cf, _ in accs]
    out_shape = [jax.ShapeDtypeStruct((t, tw), dt) for tw, dt, _, _ in outs]
    out_shape += [jax.ShapeDtypeStruct((r, tw), F32) for r, tw, _, _, _ in accs]
    res = pl.pallas_call(
        body, name=name, grid=(ncol, t // tm), in_specs=in_specs, out_specs=out_specs, out_shape=out_shape,
        compiler_params=_params(("arbitrary", "arbitrary")),
    )(*[r[0] for r in rows], *[p[0] for p in params])
    return res[:n_out], res[n_out:]


def _full(arr):
    return (arr, arr.shape[1], _const(0))


def _rms(x, g):
    x = x.astype(F32)
    return x * lax.rsqrt(jnp.mean(x * x, axis=-1, keepdims=True) + EPS) * g.astype(F32)


def _sigmoid(x):
    return 1.0 / (1.0 + jnp.exp(-x))


def _silu(x):
    return x * _sigmoid(x)


def _softplus(x):
    return jnp.maximum(x, 0.0) + jnp.log(1.0 + jnp.exp(-jnp.abs(x)))


def rms_fwd(h, g, *, name):
    d = h.shape[1]
    (hn,), _ = rowwise(lambda x, gg: ((_rms(x, gg),), ()), [_full(h)], [_full(g)],
                       [(d, BF16, d, _const(0))], name=name, tm=512)
    return hn


def rms_bwd(h, g, dhn, dh_res, *, name):
    d = h.shape[1]

    def fn(x, ct, res, gg):
        _, vjp = jax.vjp(_rms, x.astype(F32), gg.astype(F32))
        dx, dg = vjp(ct.astype(F32))
        return (res.astype(F32) + dx,), (dg,)

    (dh,), (dg,) = rowwise(fn, [_full(h), _full(dhn), _full(dh_res)], [_full(g)],
                           [(d, F32, d, _const(0))], [(1, d, d, _const(0), True)], name=name, tm=256)
    return dh, dg


def _head_rms(x, g, scale):
    x = x.astype(F32)
    return x * lax.rsqrt(jnp.mean(x * x, axis=-1, keepdims=True) + EPS) * (g.astype(F32) * scale)


def headnorm_fwd(x, col0, g, scale, *, name):
    (y,), _ = rowwise(lambda a, gg: ((_head_rms(a, gg, scale),), ()),
                      [(x, HEAD_DIM, lambda j: col0 + j)], [_full(g)],
                      [(WIDTH, BF16, HEAD_DIM, lambda j: j)], name=name, tm=1024, ncol=HEADS)
    return y


def headnorm_bwd(x, col0, g, scale, dys, *, name, out_dtype=BF16):
    def fn(a, *rest):
        cts, gg = rest[:-1], rest[-1]
        ct = sum(c.astype(F32) for c in cts)
        _, vjp = jax.vjp(lambda a_, g_: _head_rms(a_, g_, scale), a.astype(F32), gg.astype(F32))
        dx, dg = vjp(ct)
        return (dx,), (dg,)

    (dx,), (dg,) = rowwise(fn, [(x, HEAD_DIM, lambda j: col0 + j)] + [(dy, HEAD_DIM, lambda j: j) for dy in dys], [_full(g)],
                           [(WIDTH, out_dtype, HEAD_DIM, lambda j: j)],
                           [(1, HEAD_DIM, HEAD_DIM, _const(0), True)], name=name, tm=1024, ncol=HEADS)
    return dx, dg


def sum_cast(parts, dtype, *, name):
    wd = parts[0].shape[1]
    (out,), _ = rowwise(lambda *a: ((sum(b.astype(F32) for b in a),), ()), [_full(a) for a in parts], [],
                        [(wd, dtype, wd, _const(0))], name=name, tm=512)
    return out


def _gatenorm(o, gate, g):
    return _head_rms(o, g, 1.0) * _silu(gate.astype(F32))


def gatenorm_fwd(o, proj, g, *, name):
    (y,), _ = rowwise(lambda a, gt, gg: ((_gatenorm(a, gt, gg),), ()),
                      [(o, HEAD_DIM, lambda j: j), (proj, HEAD_DIM, lambda j: 3 * HEADS + j)], [_full(g)],
                      [(WIDTH, BF16, HEAD_DIM, lambda j: j)], name=name, tm=1024, ncol=HEADS)
    return y


def gatenorm_bwd(o, proj, g, dy, *, name):
    def fn(a, gt, ct, gg):
        _, vjp = jax.vjp(_gatenorm, a.astype(F32), gt.astype(F32), gg.astype(F32))
        da, dgt, dg = vjp(ct.astype(F32))
        return (da, dgt), (dg,)

    (do, dproj), (dg,) = rowwise(
        fn, [(o, HEAD_DIM, lambda j: j), (proj, HEAD_DIM, lambda j: 3 * HEADS + j), (dy, HEAD_DIM, lambda j: j)],
        [_full(g)],
        [(WIDTH, F32, HEAD_DIM, lambda j: j), (4 * WIDTH, BF16, HEAD_DIM, lambda j: 3 * HEADS + j)],
        [(1, HEAD_DIM, HEAD_DIM, _const(0), True)], name=name, tm=1024, ncol=HEADS)
    return do, dproj, dg


def _swiglu(g, u):
    return _silu(g.astype(F32)) * u.astype(F32)


def ffn_in_swiglu(hn, w4, *, name):
    t, dm = hn.shape
    ns = w4.shape[3]
    half = N_CHIPS // 2
    tm = min(512, t)

    def body(a_ref, w_ref, gu_ref, act_ref):
        a = a_ref[...]
        for j in range(half):
            g = _bdot(a, w_ref[j, 0], NN)
            u = _bdot(a, w_ref[half + j, 0], NN)
            gu_ref[:, j * ns:(j + 1) * ns] = g.astype(BF16)
            gu_ref[:, (half + j) * ns:(half + j + 1) * ns] = u.astype(BF16)
            act_ref[:, j * ns:(j + 1) * ns] = (_silu(g) * u).astype(BF16)

    return pl.pallas_call(
        body, name=name, grid=(t // tm,),
        in_specs=[pl.BlockSpec((tm, dm), lambda i: (i, 0)), pl.BlockSpec(w4.shape, lambda i: (0, 0, 0, 0))],
        out_specs=[pl.BlockSpec((tm, N_CHIPS * ns), lambda i: (i, 0)), pl.BlockSpec((tm, half * ns), lambda i: (i, 0))],
        out_shape=[jax.ShapeDtypeStruct((t, N_CHIPS * ns), BF16), jax.ShapeDtypeStruct((t, half * ns), BF16)],
        compiler_params=_params(("parallel",)),
    )(hn, w4)


def ffn_out_dx_swiglu(dh, w_out, gu, *, name, swap=()):
    t, dm = dh.shape
    f = w_out.shape[0]
    tm = min(256, t)
    n = t // tm
    nw = len(swap)

    def body(*refs):
        dh_ref, w_ref, gu_ref = refs[:3]
        o_ref = refs[3 + nw]
        comm = _Swap(refs[3:3 + nw], refs[4 + nw:4 + 2 * nw], *refs[4 + 2 * nw:]) if nw else None
        if nw:
            @pl.when(pl.program_id(0) == 0)
            def _():
                comm.start()

        dact = _bdot(dh_ref[...], w_ref[...], NT)
        g, u = gu_ref[:, :f].astype(F32), gu_ref[:, f:].astype(F32)
        sg = _sigmoid(g)
        gs = g * sg
        o_ref[:, :f] = (dact * u * (sg + gs * (1.0 - sg))).astype(BF16)
        o_ref[:, f:] = (dact * gs).astype(BF16)

        if nw:
            @pl.when(pl.program_id(0) == n - 1)
            def _():
                comm.finish()

    return pl.pallas_call(
        body, name=name, grid=(n,),
        in_specs=[pl.BlockSpec((tm, dm), lambda i: (i, 0)), pl.BlockSpec((f, dm), lambda i: (0, 0)),
                  pl.BlockSpec((tm, 2 * f), lambda i: (i, 0))] + [ANY] * nw,
        out_specs=[pl.BlockSpec((tm, 2 * f), lambda i: (i, 0))] + [ANY] * nw,
        out_shape=[jax.ShapeDtypeStruct((t, 2 * f), BF16)] + _swap_shapes(swap),
        scratch_shapes=_dma_sems(N_CHIPS * nw) if nw else [],
        compiler_params=_params(("arbitrary",) if nw else ("parallel",)),
    )(dh, w_out, gu, *swap)


def ple_bwd(dh, pp, gt, *, name):
    d = dh.shape[1]

    def fn(ct, b, c):
        s = _sigmoid(c)
        return (ct * s, ct * b * s * (1.0 - s)), ()

    (dpp, dgt), _ = rowwise(fn, [_full(dh), _full(pp), _full(gt)], [],
                            [(d, BF16, d, _const(0)), (d, BF16, d, _const(0))], name=name, tm=512)
    return dpp, dgt


def loss_head(y, tgt, *, name):
    d = y.shape[1]

    def fn(a, b):
        e = a - b
        return (e * (1.0 / d),), (jnp.sum(e * e, axis=0, keepdims=True),)

    (dy,), (sq,) = rowwise(fn, [_full(y), _full(tgt)], [], [(d, F32, d, _const(0))],
                           [(1, d, d, _const(0), True)], name=name, tm=512)
    return dy, sq


def adamw(w, g, m, v, *, name):
    shape = w.shape
    cols = shape[-1]
    flat = lambda a: a.reshape(-1, cols)
    bc1 = 1.0 - ADAM_B1 ** ADAM_STEP
    bc2 = 1.0 - ADAM_B2 ** ADAM_STEP

    def fn(w_, g_, m_, v_):
        m_ = ADAM_B1 * m_ + (1.0 - ADAM_B1) * g_
        v_ = ADAM_B2 * v_ + (1.0 - ADAM_B2) * (g_ * g_)
        delta = -ADAM_LR * ((m_ / bc1) / (jnp.sqrt(v_ / bc2) + ADAM_EPS) + ADAM_WD * w_)
        return (delta, m_, v_), ()

    o = (cols, F32, cols, _const(0))
    (d_, m_, v_), _ = rowwise(fn, [_full(flat(w)), _full(flat(g)), _full(flat(m)), _full(flat(v))], [],
                              [o, o, o], name=name, tm=256)
    return d_.reshape(shape), m_.reshape(shape), v_.reshape(shape)


CONV_STRIP = 256


def _shift_down(x, d):
    if d == 0:
        return x
    rows = lax.broadcasted_iota(jnp.int32, x.shape, 0)
    return jnp.where(rows >= d, pltpu.roll(x, d, 0), 0.0)


def _shift_up(x, d):
    if d == 0:
        return x
    t = x.shape[0]
    rows = lax.broadcasted_iota(jnp.int32, x.shape, 0)
    return jnp.where(rows < t - d, pltpu.roll(x, t - d, 0), 0.0)


def _conv(x, w):
    acc = None
    for j in range(CONV_WIDTH):
        term = _shift_down(x, CONV_WIDTH - 1 - j) * w[j:j + 1, :]
        acc = term if acc is None else acc + term
    return acc


def conv_fwd(proj, w, *, name):
    t = proj.shape[0]
    per = WIDTH // CONV_STRIP

    def body(x_ref, w_ref, o_ref):
        o_ref[0] = _silu(_conv(x_ref[...].astype(F32), w_ref[...]))

    return pl.pallas_call(
        body, name=name, grid=(3 * per,),
        in_specs=[pl.BlockSpec((t, CONV_STRIP), lambda j: (0, j)), pl.BlockSpec((CONV_WIDTH, CONV_STRIP), lambda j: (0, j))],
        out_specs=pl.BlockSpec((1, t, CONV_STRIP), lambda j: (j // per, 0, j % per)),
        out_shape=jax.ShapeDtypeStruct((3, t, WIDTH), F32),
        compiler_params=_params(("parallel",)),
    )(proj, w)


def conv_bwd(proj, w, dqkv, dproj, *, name):
    t = proj.shape[0]
    per = WIDTH // CONV_STRIP

    def body(x_ref, w_ref, d_ref, _, dx_ref, dw_ref):
        x, w_ = x_ref[...].astype(F32), w_ref[...]
        c = _conv(x, w_)
        s = _sigmoid(c)
        dc = d_ref[0] * (s + c * s * (1.0 - s))
        dx = None
        for j in range(CONV_WIDTH):
            d = CONV_WIDTH - 1 - j
            term = _shift_up(dc, d) * w_[j:j + 1, :]
            dx = term if dx is None else dx + term
            dw_ref[j:j + 1, :] = jnp.sum(dc * _shift_down(x, d), axis=0, keepdims=True)
        dx_ref[...] = dx.astype(dx_ref.dtype)

    return pl.pallas_call(
        body, name=name, grid=(3 * per,),
        in_specs=[pl.BlockSpec((t, CONV_STRIP), lambda j: (0, j)), pl.BlockSpec((CONV_WIDTH, CONV_STRIP), lambda j: (0, j)),
                  pl.BlockSpec((1, t, CONV_STRIP), lambda j: (j // per, 0, j % per)), pl.BlockSpec(memory_space=pl.ANY)],
        out_specs=[pl.BlockSpec((t, CONV_STRIP), lambda j: (0, j)), pl.BlockSpec((CONV_WIDTH, CONV_STRIP), lambda j: (0, j))],
        out_shape=[jax.ShapeDtypeStruct(dproj.shape, dproj.dtype), jax.ShapeDtypeStruct((CONV_WIDTH, 3 * WIDTH), F32)],
        input_output_aliases={3: 0},
        compiler_params=_params(("parallel",)),
    )(proj, w, dqkv, dproj)


def _gdn_gates(ab, a_log, dt_bias):
    a_in, b_in = ab[:HEADS], ab[HEADS:]
    g = -jnp.exp(a_log) * _softplus(a_in + dt_bias)
    return g, _sigmoid(b_in)


def gates_fwd(ab, a_log, dt_bias, *, name):
    t = ab.shape[1]

    def body(ab_ref, al_ref, dt_ref, g_ref, b_ref):
        g_ref[...], b_ref[...] = _gdn_gates(ab_ref[...], al_ref[...], dt_ref[...])

    s = jax.ShapeDtypeStruct((HEADS, t), F32)
    return pl.pallas_call(body, name=name, out_shape=[s, s], compiler_params=_params())(ab, a_log, dt_bias)


def gates_bwd(ab, a_log, dt_bias, dg, dbeta, *, name):
    t = ab.shape[1]

    def body(ab_ref, al_ref, dt_ref, dg_ref, db_ref, dab_ref, dal_ref, ddt_ref):
        _, vjp = jax.vjp(_gdn_gates, ab_ref[...], al_ref[...], dt_ref[...])
        dab_ref[...], dal_ref[...], ddt_ref[...] = vjp((dg_ref[...], db_ref[...]))

    c = jax.ShapeDtypeStruct((HEADS, 1), F32)
    return pl.pallas_call(body, name=name, out_shape=[jax.ShapeDtypeStruct((2 * HEADS, t), F32), c, c],
                          compiler_params=_params())(ab, a_log, dt_bias, dg, dbeta)


def _split3(x):
    hi = x.astype(BF16)
    r1 = x - hi.astype(F32)
    mid = r1.astype(BF16)
    lo = (r1 - mid.astype(F32)).astype(BF16)
    return hi, mid, lo


def _dot01(x, m01):
    hi, mid, lo = _split3(x)
    m01 = m01.astype(BF16)
    return _bdot(hi, m01, NN) + _bdot(mid, m01, NN) + _bdot(lo, m01, NN)


def _hdot(a, b, dims=NN):
    a_hi, b_hi = a.astype(BF16), b.astype(BF16)
    a_lo, b_lo = (a - a_hi.astype(F32)).astype(BF16), (b - b_hi.astype(F32)).astype(BF16)
    return _bdot(a_hi, b_hi, dims) + (_bdot(a_hi, b_lo, dims) + _bdot(a_lo, b_hi, dims))


def _rowsum(x):
    return jnp.sum(x, axis=1, keepdims=True)


def _colsum(x):
    return jnp.sum(x, axis=0, keepdims=True)


class _Heads:
    def __init__(self, vals):
        self.v = list(vals)

    def _bin(self, other, f):
        if isinstance(other, _Heads):
            return _Heads(f(a, b) for a, b in zip(self.v, other.v))
        return _Heads(f(a, other) for a in self.v)

    def __add__(self, o):
        return self._bin(o, lambda a, b: a + b)

    __radd__ = __add__

    def __sub__(self, o):
        return self._bin(o, lambda a, b: a - b)

    def __rsub__(self, o):
        return self._bin(o, lambda a, b: b - a)

    def __mul__(self, o):
        return self._bin(o, lambda a, b: a * b)

    __rmul__ = __mul__

    def __neg__(self):
        return _Heads(-a for a in self.v)


def _hmap(f, *args):
    n = next(len(a.v) for a in args if isinstance(a, _Heads))
    return _Heads(f(*[a.v[h] if isinstance(a, _Heads) else a for a in args]) for h in range(n))


def _inv_unit_lower(a, eye):
    p = jnp.where(eye, 1.0, 0.0) - a
    ak = a
    for _ in range(int(math.log2(CHUNK)) - 1):
        ak = _hmap(_hdot, ak, ak)
        p = p + _hmap(_hdot, p, ak)
    return p


def _gdn_chunk(qr, kr, v, grow, brow, tinv=None):
    c = CHUNK
    ri = lax.broadcasted_iota(jnp.int32, (c, c), 0)
    ci = lax.broadcasted_iota(jnp.int32, (c, c), 1)
    eye, lower, strict = ri == ci, ri >= ci, ri > ci
    where = lambda m: (lambda a: jnp.where(m, a, 0.0))
    to_col = lambda row: _hmap(lambda r: _rowsum(jnp.where(eye, jnp.broadcast_to(r, (c, c)), 0.0)), row)
    cum_row = _hmap(lambda g: _dot01(jnp.broadcast_to(g, (8, c)), ri <= ci)[0:1], grow)
    gcol, bcol = to_col(cum_row), to_col(brow)
    glast = _hmap(lambda g: _colsum(jnp.where(ri[:, 0:1] == c - 1, g, 0.0)), gcol)
    rq = _hmap(lambda a: lax.rsqrt(_rowsum(a * a) + EPS), qr)
    rk = _hmap(lambda a: lax.rsqrt(_rowsum(a * a) + EPS), kr)
    scale = HEAD_DIM ** -0.5
    qn, kn = qr * (rq * scale), kr * rk
    dec = _hmap(lambda gc, gr: jnp.where(lower, jnp.exp(jnp.minimum(gc - gr, 0.0)), 0.0), gcol, cum_row)
    kk = _hmap(lambda a: _bdot(a, a, NT), kn)
    qk = _hmap(lambda a, b: _bdot(a, b, NT), qn, kn)
    gam_col, e_col, gam_last = _hmap(jnp.exp, gcol), _hmap(jnp.exp, glast - gcol), _hmap(jnp.exp, glast)
    if tinv is None:
        tinv = _inv_unit_lower(_hmap(where(strict), bcol * kk * dec), eye)
    u = _hmap(_hdot, tinv, v * bcol)
    w = _hmap(_hdot, tinv, kn * (bcol * gam_col))
    return dict(eye=eye, lower=lower, strict=strict, gcol=gcol, bcol=bcol, rq=rq, rk=rk, qn=qn, kn=kn,
                dec=dec, kk=kk, qk=qk, gam_col=gam_col, e_col=e_col, gam_last=gam_last, tinv=tinv, u=u, w=w,
                aqk=qk * dec, qt=qn * gam_col, kt=kn * e_col, scale=scale)


def _head_cols(h):
    return slice(h * HEAD_DIM, (h + 1) * HEAD_DIM)


def _bd(dims):
    return lambda a, b: _bdot(a, b, dims)


def gdn_fwd(qkv, g4, b4, *, name, gather=()):
    t = qkv.shape[1]
    n = t // CHUNK
    d = HEAD_DIM
    heads = range(HEADS)
    ng = len(gather)

    def body(*refs):
        qkv_ref, g_ref, b_ref = refs[:3]
        o_ref, s0_ref, t_ref = refs[3 + ng:6 + ng]
        s_ref = refs[6 + 2 * ng]
        comm = _Gather(refs[6 + ng:6 + 2 * ng], *refs[7 + 2 * ng:]) if ng else None

        @pl.when(pl.program_id(0) == 0)
        def _():
            s_ref[...] = jnp.zeros_like(s_ref)
            if ng:
                comm.start()

        qr, kr, v = (_Heads(qkv_ref[j, :, _head_cols(h)] for h in heads) for j in range(3))
        z = _gdn_chunk(qr, kr, v, _Heads(g_ref[0, h] for h in heads), _Heads(b_ref[0, h] for h in heads))
        s0 = _Heads(s_ref[h] for h in heads)
        v_new = z["u"] - _hmap(_bd(NN), z["w"], s0)
        o = _hmap(_bd(NN), z["qt"], s0) + _hmap(_bd(NN), z["aqk"], v_new)
        s_new = s0 * z["gam_last"] + _hmap(_bd(TN), z["kt"], v_new)
        for h in heads:
            s0_ref[0, h] = s0.v[h]
            t_ref[0, h] = z["tinv"].v[h]
            o_ref[:, _head_cols(h)] = o.v[h]
            s_ref[h] = s_new.v[h]

        if ng:
            @pl.when(pl.program_id(0) == n - 1)
            def _():
                comm.finish()

    gspec = pl.BlockSpec((1, HEADS, 1, CHUNK), lambda i: (i, 0, 0, 0))
    return pl.pallas_call(
        body, name=name, grid=(n,),
        in_specs=[pl.BlockSpec((3, CHUNK, WIDTH), lambda i: (0, i, 0)), gspec, gspec] + [ANY] * ng,
        out_specs=[pl.BlockSpec((CHUNK, WIDTH), lambda i: (i, 0)),
                   pl.BlockSpec((1, HEADS, d, d), lambda i: (i, 0, 0, 0)),
                   pl.BlockSpec((1, HEADS, CHUNK, CHUNK), lambda i: (i, 0, 0, 0))] + [ANY] * ng,
        out_shape=[jax.ShapeDtypeStruct((t, WIDTH), F32), jax.ShapeDtypeStruct((n, HEADS, d, d), F32),
                   jax.ShapeDtypeStruct((n, HEADS, CHUNK, CHUNK), F32)]
        + [jax.ShapeDtypeStruct(s.shape, s.dtype) for s in gather],
        input_output_aliases={3 + b: 3 + b for b in range(ng)},
        scratch_shapes=[pltpu.VMEM((HEADS, d, d), F32)] + (_dma_sems(6 * ng) if ng else []),
        compiler_params=_params(("arbitrary",)),
    )(qkv, g4, b4, *gather)


def gdn_bwd(qkv, g4, b4, s0_all, tinv_all, do, *, name, exchange=()):
    t = qkv.shape[1]
    n = t // CHUNK
    d = HEAD_DIM
    c = CHUNK
    heads = range(HEADS)

    ne = len(exchange)

    def body(*refs):
        qkv_ref, g_ref, b_ref, s0_ref, t_ref, do_ref = refs[:6]
        dqkv_ref, dg_ref, db_ref = refs[6 + ne:9 + ne]
        ds_ref = refs[9 + 2 * ne]
        comm = _Exchange(refs[6:6 + ne], refs[9 + ne:9 + 2 * ne], *refs[10 + 2 * ne:]) if ne else None

        @pl.when(pl.program_id(0) == 0)
        def _():
            ds_ref[...] = jnp.zeros_like(ds_ref)
            if ne:
                comm.start()

        qr, kr, v = (_Heads(qkv_ref[j, :, _head_cols(h)] for h in heads) for j in range(3))
        z = _gdn_chunk(qr, kr, v, _Heads(g_ref[0, h] for h in heads), _Heads(b_ref[0, h] for h in heads),
                       tinv=_Heads(t_ref[0, h] for h in heads))
        s0 = _Heads(s0_ref[0, h] for h in heads)
        ds = _Heads(ds_ref[h] for h in heads)
        dout = _Heads(do_ref[:, _head_cols(h)] for h in heads)
        qn, kn, u, w, dec, kk, qk = z["qn"], z["kn"], z["u"], z["w"], z["dec"], z["kk"], z["qk"]
        bcol, gam_col, e_col, gam_last = z["bcol"], z["gam_col"], z["e_col"], z["gam_last"]
        low = lambda a: jnp.where(z["lower"], a, 0.0)
        strict = lambda a: jnp.where(z["strict"], a, 0.0)
        rowsum = lambda a: _hmap(_rowsum, a)
        colsum = lambda a: _hmap(_colsum, a)
        v_new = u - _hmap(_bd(NN), w, s0)
        dv_new = _hmap(_bd(TN), z["aqk"], dout) + _hmap(_bd(NN), z["kt"], ds)
        daqk = _hmap(low, _hmap(_bd(NT), dout, v_new))
        dqt = _hmap(_bd(NT), dout, s0)
        dkt = _hmap(_bd(NT), v_new, ds)
        dgam_last = _hmap(lambda a, b: jnp.sum(a * b, keepdims=True), ds, s0)
        ds_new = _hmap(_bd(TN), z["qt"], dout) + ds * gam_last - _hmap(_bd(TN), w, dv_new)
        dw = -_hmap(_bd(NT), dv_new, s0)
        hd_t = lambda a, b: _hdot(a, b, TN)
        dru = _hmap(hd_t, z["tinv"], dv_new)
        drw = _hmap(hd_t, z["tinv"], dw)
        dal = -_hmap(strict, _hmap(_bd(NT), dru, u) + _hmap(_bd(NT), drw, w))
        t1 = dal * kk * dec
        dkk = dal * bcol * dec
        ddec = dal * bcol * kk + daqk * qk
        dqk = daqk * dec
        s_w = rowsum(drw * kn)
        dbeta_col = rowsum(t1) + rowsum(dru * v) + gam_col * s_w
        dkn = (drw * (bcol * gam_col) + _hmap(_bd(NN), dkk, kn) + _hmap(_bd(TN), dkk, kn) + _hmap(_bd(TN), dqk, qn)
               + dkt * e_col)
        dqn = _hmap(_bd(NN), dqk, kn) + dqt * gam_col
        e_mat = ddec * dec
        de_col = rowsum(dkt * kn)
        diag_of_colsum = rowsum(_hmap(lambda a: jnp.where(z["eye"], jnp.broadcast_to(_colsum(a), (c, c)), 0.0), e_mat))
        dg_cum = rowsum(e_mat) + (bcol * s_w + rowsum(dqt * qn)) * gam_col - de_col * e_col - diag_of_colsum
        dg_last = colsum(de_col * e_col) + dgam_last * gam_last
        dg = colsum(_hmap(lambda a: jnp.where(z["lower"], a, 0.0), dg_cum)) + dg_last
        dbeta = colsum(_hmap(lambda a: jnp.where(z["eye"], a, 0.0), dbeta_col))
        rq, rk = z["rq"], z["rk"]
        dqr = z["scale"] * (rq * dqn - qr * (rq * rq * rq) * rowsum(qr * dqn))
        dkr = rk * dkn - kr * (rk * rk * rk) * rowsum(kr * dkn)
        dv = dru * bcol
        for h in heads:
            ds_ref[h] = ds_new.v[h]
            dg_ref[0, h] = dg.v[h]
            db_ref[0, h] = dbeta.v[h]
            dqkv_ref[0, :, _head_cols(h)] = dqr.v[h]
            dqkv_ref[1, :, _head_cols(h)] = dkr.v[h]
            dqkv_ref[2, :, _head_cols(h)] = dv.v[h]

        if ne:
            @pl.when(pl.program_id(0) == n - 1)
            def _():
                comm.finish()

    rev = lambda i: n - 1 - i
    gspec = pl.BlockSpec((1, HEADS, 1, CHUNK), lambda i: (rev(i), 0, 0, 0))
    return pl.pallas_call(
        body, name=name, grid=(n,),
        in_specs=[pl.BlockSpec((3, CHUNK, WIDTH), lambda i: (0, rev(i), 0)), gspec, gspec,
                  pl.BlockSpec((1, HEADS, d, d), lambda i: (rev(i), 0, 0, 0)),
                  pl.BlockSpec((1, HEADS, CHUNK, CHUNK), lambda i: (rev(i), 0, 0, 0)),
                  pl.BlockSpec((CHUNK, WIDTH), lambda i: (rev(i), 0))] + [ANY] * ne,
        out_specs=[pl.BlockSpec((3, CHUNK, WIDTH), lambda i: (0, rev(i), 0)), gspec, gspec] + [ANY] * ne,
        out_shape=[jax.ShapeDtypeStruct((3, t, WIDTH), F32), jax.ShapeDtypeStruct((n, HEADS, 1, CHUNK), F32),
                   jax.ShapeDtypeStruct((n, HEADS, 1, CHUNK), F32)] + _exchange_shapes(exchange),
        scratch_shapes=[pltpu.VMEM((HEADS, d, d), F32)] + (_dma_sems(3 * ne) if ne else []),
        compiler_params=_params(("arbitrary",)),
    )(qkv, g4, b4, s0_all, tinv_all, do, *exchange)


SB_BLOCK = 256


def _dot01_2(x, m01):
    hi = x.astype(BF16)
    lo = (x - hi.astype(F32)).astype(BF16)
    return _bdot(hi, m01, NN) + _bdot(lo, m01, NN)


SB_HEADS = 2


def _sb_weights(q, kb, carry, mask, upper):
    z = _hmap(_bd(NT), q, kb)
    ls = _hmap(lambda z_: jnp.minimum(z_, 0.0) - jnp.log(1.0 + jnp.exp(-jnp.abs(z_))), z)
    ln = _hmap(lambda l_, z_: jnp.where(mask, l_ - z_, 0.0), ls, z)
    suffix = _hmap(lambda l_: _dot01_2(l_, upper), ln)
    a = _hmap(lambda l_, s_, c_: jnp.where(mask, jnp.exp(l_ + s_ + c_), 0.0), ls, suffix, carry)
    return z, ln, a


SB_DEAD = -105.0


def _sb_alive(s, i, carries):
    top = jnp.max(carries[0])
    for c in carries[1:]:
        top = jnp.maximum(top, jnp.max(c))
    return (s <= i) & (top > SB_DEAD)


def _sb_masks(i, jb, blk):
    ri = lax.broadcasted_iota(jnp.int32, (blk, blk), 0)
    ci = lax.broadcasted_iota(jnp.int32, (blk, blk), 1)
    return (jb * blk + ci) < (i * blk + ri)


def sb_fwd(q, k, v, *, name, gather=()):
    t = q.shape[0]
    blk = min(SB_BLOCK, t)
    d = HEAD_DIM
    hs = range(SB_HEADS)
    ng = len(gather)
    groups, nb = HEADS // SB_HEADS, t // blk

    def body(*refs):
        q_ref, k_ref, v_ref = refs[:3]
        o_ref = refs[3 + ng]
        comm = _Gather(refs[4 + ng:4 + 2 * ng], *refs[4 + 2 * ng:]) if ng else None
        i = pl.program_id(1)
        if ng:
            @pl.when((pl.program_id(0) == 0) & (i == 0))
            def _():
                comm.start()

        qb = _Heads(q_ref[:, _head_cols(h)] for h in hs)
        ri = lax.broadcasted_iota(jnp.int32, (blk, blk), 0)
        ci = lax.broadcasted_iota(jnp.int32, (blk, blk), 1)
        upper = (ri > ci).astype(BF16)

        def step(state):
            s, cs, accs = state
            jb = i - s
            rows = pl.ds(pl.multiple_of(jb * blk, blk), blk)
            kb = _Heads(k_ref[rows, _head_cols(h)] for h in hs)
            vb = _Heads(v_ref[rows, _head_cols(h)] for h in hs)
            _, ln, a = _sb_weights(qb, kb, _Heads(cs), _sb_masks(i, jb, blk), upper)
            cs = _Heads(cs) + _hmap(_rowsum, ln)
            accs = _Heads(accs) + _hmap(_bd(NN), a, vb)
            return s + 1, tuple(cs.v), tuple(accs.v)

        init = (jnp.int32(0), tuple(jnp.zeros((blk, 1), F32) for _ in hs), tuple(jnp.zeros((blk, d), F32) for _ in hs))
        _, _, accs = lax.while_loop(lambda st: _sb_alive(st[0], i, st[1]), step, init)
        for h in hs:
            o_ref[:, _head_cols(h)] = accs[h].astype(o_ref.dtype)

        if ng:
            @pl.when((pl.program_id(0) == groups - 1) & (i == nb - 1))
            def _():
                comm.finish()

    qspec = pl.BlockSpec((blk, SB_HEADS * d), lambda g, i: (i, g))
    kspec = pl.BlockSpec((t, SB_HEADS * d), lambda g, i: (0, g))
    return pl.pallas_call(
        body, name=name, grid=(groups, nb), in_specs=[qspec, kspec, kspec] + [ANY] * ng,
        out_specs=[qspec] + [ANY] * ng,
        out_shape=[jax.ShapeDtypeStruct((t, WIDTH), BF16)] + [jax.ShapeDtypeStruct(s.shape, s.dtype) for s in gather],
        input_output_aliases={3 + b: 1 + b for b in range(ng)},
        scratch_shapes=_dma_sems(6 * ng) if ng else [],
        compiler_params=_params(("arbitrary", "arbitrary") if ng else ("parallel", "arbitrary")),
    )(q, k, v, *gather)


def sb_bwd(q, k, v, do, *, name, exchange=()):
    t = q.shape[0]
    blk = min(SB_BLOCK, t)
    d = HEAD_DIM
    nb = t // blk
    hs = range(SB_HEADS)
    ne = len(exchange)
    groups = HEADS // SB_HEADS

    def body(*refs):
        q_ref, k_ref, v_ref, do_ref = refs[:4]
        dq_ref, dk_ref, dv_ref = refs[4 + ne:7 + ne]
        p_buf, z_buf = refs[7 + 2 * ne:9 + 2 * ne]
        comm = _Exchange(refs[4:4 + ne], refs[7 + ne:7 + 2 * ne], *refs[9 + 2 * ne:]) if ne else None
        i = pl.program_id(1)
        if ne:
            @pl.when((pl.program_id(0) == 0) & (i == 0))
            def _():
                comm.start()

        @pl.when(i == 0)
        def _():
            dk_ref[...] = jnp.zeros_like(dk_ref)
            dv_ref[...] = jnp.zeros_like(dv_ref)

        qb = _Heads(q_ref[:, _head_cols(h)] for h in hs)
        dob = _Heads(do_ref[:, _head_cols(h)] for h in hs)
        ri = lax.broadcasted_iota(jnp.int32, (blk, blk), 0)
        ci = lax.broadcasted_iota(jnp.int32, (blk, blk), 1)
        upper = (ri > ci).astype(BF16)
        lower = (ri < ci).astype(BF16)

        def right_to_left(state):
            s, cs = state
            jb = i - s
            rows = pl.ds(pl.multiple_of(jb * blk, blk), blk)
            kb = _Heads(k_ref[rows, _head_cols(h)] for h in hs)
            vb = _Heads(v_ref[rows, _head_cols(h)] for h in hs)
            z, ln, a = _sb_weights(qb, kb, _Heads(cs), _sb_masks(i, jb, blk), upper)
            p = a * _hmap(_bd(NT), dob, vb)
            dv = _hmap(_bd(TN), a, dob)
            for h in hs:
                p_buf[h, jb] = p.v[h]
                z_buf[h, jb] = z.v[h]
                dv_ref[rows, _head_cols(h)] += dv.v[h]
            return s + 1, tuple((_Heads(cs) + _hmap(_rowsum, ln)).v)

        n_done, _ = lax.while_loop(lambda st: _sb_alive(st[0], i, st[1]), right_to_left,
                                   (jnp.int32(0), tuple(jnp.zeros((blk, 1), F32) for _ in hs)))

        def left_to_right(jb, carry):
            cps, dqs = carry
            rows = pl.ds(pl.multiple_of(jb * blk, blk), blk)
            mask = _sb_masks(i, jb, blk)
            kb = _Heads(k_ref[rows, _head_cols(h)] for h in hs)
            p = _Heads(p_buf[h, jb] for h in hs)
            sg = _hmap(_sigmoid, _Heads(z_buf[h, jb] for h in hs))
            prefix = _hmap(lambda a: _dot01_2(a, lower), p) + _Heads(cps)
            dz = _hmap(lambda a: jnp.where(mask, a, 0.0), p * (1.0 - sg) - sg * prefix)
            dk = _hmap(_bd(TN), dz, qb)
            for h in hs:
                dk_ref[rows, _head_cols(h)] += dk.v[h]
            return tuple((_Heads(cps) + _hmap(_rowsum, p)).v), tuple((_Heads(dqs) + _hmap(_bd(NN), dz, kb)).v)

        _, dqs = lax.fori_loop(i + 1 - n_done, i + 1, left_to_right,
                               (tuple(jnp.zeros((blk, 1), F32) for _ in hs), tuple(jnp.zeros((blk, d), F32) for _ in hs)))
        for h in hs:
            dq_ref[:, _head_cols(h)] = dqs[h]

        if ne:
            @pl.when((pl.program_id(0) == groups - 1) & (i == nb - 1))
            def _():
                comm.finish()

    qspec = pl.BlockSpec((blk, SB_HEADS * d), lambda g, i: (i, g))
    kspec = pl.BlockSpec((t, SB_HEADS * d), lambda g, i: (0, g))
    s = jax.ShapeDtypeStruct((t, WIDTH), F32)
    buf = pltpu.VMEM((SB_HEADS, nb, blk, blk), F32)
    return pl.pallas_call(
        body, name=name, grid=(groups, nb), in_specs=[qspec, kspec, kspec, qspec] + [ANY] * ne,
        out_specs=[qspec, kspec, kspec] + [ANY] * ne, out_shape=[s, s, s] + _exchange_shapes(exchange),
        scratch_shapes=[buf, buf] + (_dma_sems(3 * ne) if ne else []),
        compiler_params=_params(("arbitrary", "arbitrary") if ne else ("parallel", "arbitrary")),
    )(q, k, v, do, *exchange)


PACK_COLS = 1024
ANY = pl.BlockSpec(memory_space=pl.ANY)


def _mesh_pos():
    return lax.axis_index("x"), lax.axis_index("y"), lax.axis_index("c")


def _other_chips(x, y):
    return [(1 - x, y), (x, 1 - y), (1 - x, 1 - y)]


def _dma_sems(n):
    return [pltpu.SemaphoreType.DMA((n,)), pltpu.SemaphoreType.DMA((n,))]


class _Gather:
    def __init__(self, o_refs, send_sems, recv_sems):
        self.o_refs, self.send_sems, self.recv_sems = o_refs, send_sems, recv_sems

    def _copy(self, b, k, chip, hf, to):
        rows = self.o_refs[b].at[chip, hf]
        return pltpu.make_async_remote_copy(src_ref=rows, dst_ref=rows, send_sem=self.send_sems.at[6 * b + k],
                                            recv_sem=self.recv_sems.at[6 * b + k], device_id=to, device_id_type=MESH)

    def start(self):
        x, y, c = _mesh_pos()
        for b in range(len(self.o_refs)):
            for k, (cx, cy) in enumerate(_other_chips(x, y)):
                self._copy(b, k, 2 * x + y, c, (cx, cy, c)).start()

    def finish(self):
        x, y, c = _mesh_pos()
        chips = _other_chips(x, y)
        for b in range(len(self.o_refs)):
            for k, (cx, cy) in enumerate(chips):
                self._copy(b, k, 2 * cx + cy, c, (x, y, c)).wait_recv()
                self._copy(b, 3 + k, 2 * cx + cy, c, (x, y, 1 - c)).start()
        for b in range(len(self.o_refs)):
            for k, (cx, cy) in enumerate(chips):
                self._copy(b, 3 + k, 2 * cx + cy, 1 - c, (x, y, c)).wait_recv()
                self._copy(b, k, 2 * x + y, c, (cx, cy, c)).wait_send()
                self._copy(b, 3 + k, 2 * cx + cy, c, (x, y, 1 - c)).wait_send()


def all_gather_chips(slots, *, name):
    nb = len(slots)

    def body(*refs):
        g = _Gather(refs[nb:2 * nb], *refs[2 * nb:])
        g.start()
        g.finish()

    return pl.pallas_call(
        body, name=name, in_specs=[ANY] * nb, out_specs=[ANY] * nb, input_output_aliases={b: b for b in range(nb)},
        out_shape=[jax.ShapeDtypeStruct(s.shape, s.dtype) for s in slots], scratch_shapes=_dma_sems(6 * nb),
    )(*slots)


class _Swap:
    def __init__(self, g_refs, o_refs, send_sems, recv_sems):
        self.g_refs, self.o_refs, self.send_sems, self.recv_sems = g_refs, o_refs, send_sems, recv_sems

    def _copies(self):
        x, y, c = _mesh_pos()
        return [pltpu.make_async_remote_copy(src_ref=self.g_refs[b].at[j, 1 - c], dst_ref=self.o_refs[b].at[j],
                                             send_sem=self.send_sems.at[N_CHIPS * b + j],
                                             recv_sem=self.recv_sems.at[N_CHIPS * b + j],
                                             device_id=(x, y, 1 - c), device_id_type=MESH)
                for b in range(len(self.g_refs)) for j in range(N_CHIPS)]

    def start(self):
        for cp in self._copies():
            cp.start()

    def finish(self):
        for cp in self._copies():
            cp.wait()


def _swap_shapes(gs):
    return [jax.ShapeDtypeStruct((g.shape[0],) + g.shape[2:], g.dtype) for g in gs]


def sibling_swap(gs, *, name):
    nb = len(gs)

    def body(*refs):
        comm = _Swap(refs[:nb], refs[nb:2 * nb], *refs[2 * nb:])
        comm.start()
        comm.finish()

    return pl.pallas_call(
        body, name=name, in_specs=[ANY] * nb, out_specs=[ANY] * nb, out_shape=_swap_shapes(gs),
        scratch_shapes=_dma_sems(N_CHIPS * nb),
    )(*gs)


class _Exchange:
    def __init__(self, s_refs, o_refs, send_sems, recv_sems):
        self.s_refs, self.o_refs, self.send_sems, self.recv_sems = s_refs, o_refs, send_sems, recv_sems

    def _copies(self):
        x, y, c = _mesh_pos()
        return [pltpu.make_async_remote_copy(src_ref=self.s_refs[b].at[2 * cx + cy], dst_ref=self.o_refs[b].at[k],
                                             send_sem=self.send_sems.at[3 * b + k], recv_sem=self.recv_sems.at[3 * b + k],
                                             device_id=(cx, cy, c), device_id_type=MESH)
                for b in range(len(self.s_refs)) for k, (cx, cy) in enumerate(_other_chips(x, y))]

    def start(self):
        for cp in self._copies():
            cp.start()

    def finish(self):
        for cp in self._copies():
            cp.wait()


def _exchange_shapes(s1s):
    return [jax.ShapeDtypeStruct((3,) + s.shape[1:], s.dtype) for s in s1s]


def chip_exchange(s1s, *, name):
    nb = len(s1s)

    def body(*refs):
        comm = _Exchange(refs[:nb], refs[nb:2 * nb], *refs[2 * nb:])
        comm.start()
        comm.finish()

    return pl.pallas_call(
        body, name=name, in_specs=[ANY] * nb, out_specs=[ANY] * nb, out_shape=_exchange_shapes(s1s),
        scratch_shapes=_dma_sems(3 * nb),
    )(*s1s)


def sibling_merge(halves, *, name):
    nb = len(halves)

    def body(*refs):
        o_refs, (send_sems, recv_sems) = refs[nb:2 * nb], refs[2 * nb:]
        x, y, c = _mesh_pos()
        cps = [pltpu.make_async_remote_copy(src_ref=o_refs[b].at[c], dst_ref=o_refs[b].at[c], send_sem=send_sems.at[b],
                                            recv_sem=recv_sems.at[b], device_id=(x, y, 1 - c), device_id_type=MESH)
               for b in range(nb)]
        for cp in cps:
            cp.start()
        for cp in cps:
            cp.wait()

    return pl.pallas_call(
        body, name=name, in_specs=[ANY] * nb, out_specs=[ANY] * nb, input_output_aliases={b: b for b in range(nb)},
        out_shape=[jax.ShapeDtypeStruct(h.shape, h.dtype) for h in halves], scratch_shapes=_dma_sems(nb),
    )(*halves)


def all_reduce_small(buf, *, name):
    n_dev = 8

    def body(b_ref, o_ref, recv_buf, send_sems, recv_sems):
        x, y, c = _mesh_pos()
        me = 4 * x + 2 * y + c
        pos = lambda t: (t // 4, (t // 2) % 2, t % 2)

        def copy(t, slot):
            return pltpu.make_async_remote_copy(src_ref=b_ref, dst_ref=recv_buf.at[slot], send_sem=send_sems.at[t],
                                                recv_sem=recv_sems.at[slot], device_id=pos(t), device_id_type=MESH)

        for t in range(n_dev):
            @pl.when(t != me)
            def _(t=t):
                copy(t, me).start()

        recv_buf[me] = b_ref[...]
        for t in range(n_dev):
            @pl.when(t != me)
            def _(t=t):
                copy(t, t).wait_recv()
                copy(t, me).wait_send()

        acc = recv_buf[0]
        for t in range(1, n_dev):
            acc = acc + recv_buf[t]
        o_ref[...] = acc

    return pl.pallas_call(
        body, name=name, out_shape=jax.ShapeDtypeStruct(buf.shape, F32),
        in_specs=[pl.BlockSpec(memory_space=pltpu.VMEM)], out_specs=pl.BlockSpec(memory_space=pltpu.VMEM),
        scratch_shapes=[pltpu.VMEM((n_dev,) + buf.shape, F32), pltpu.SemaphoreType.DMA((n_dev,)),
                        pltpu.SemaphoreType.DMA((n_dev,))],
    )(buf)


REDUCE_ROWS = (512, 384, 256, 128)


def add_selected(sel, a5, b, *, name):
    n, _, rh, w = a5.shape
    tr = _pick(rh, REDUCE_ROWS)

    def body(sel_ref, a_ref, b_ref, own_ref, ob_ref):
        s = a_ref[...] + b_ref[...]
        ob_ref[...] = s.astype(BF16)

        @pl.when(pl.program_id(1) == sel_ref[0])
        def _():
            own_ref[...] = s

    blk = pl.BlockSpec((None, tr, w), lambda i, j, s: (j, i, 0))
    return pl.pallas_call(
        body, name=name,
        grid_spec=pltpu.PrefetchScalarGridSpec(
            num_scalar_prefetch=1, grid=(rh // tr, n),
            in_specs=[pl.BlockSpec((None, None, tr, w), lambda i, j, s: (j, s[1], i, 0)), blk],
            out_specs=[pl.BlockSpec((tr, w), lambda i, j, s: (i, 0)), blk]),
        out_shape=[jax.ShapeDtypeStruct((rh, w), F32), jax.ShapeDtypeStruct((n, rh, w), BF16)],
        compiler_params=_params(("arbitrary", "arbitrary")),
    )(sel, a5, b)


def add_chip_sums(sel, s1, b2, *, name):
    rh, w = s1.shape
    tr = _pick(rh, REDUCE_ROWS)

    def body(sel_ref, s_ref, b_ref, o_ref):
        o_ref[...] = ((s_ref[...] + b_ref[0].astype(F32)) + b_ref[1].astype(F32)) + b_ref[2].astype(F32)

    return pl.pallas_call(
        body, name=name,
        grid_spec=pltpu.PrefetchScalarGridSpec(
            num_scalar_prefetch=1, grid=(rh // tr,),
            in_specs=[pl.BlockSpec((tr, w), lambda i, s: (i, 0)), pl.BlockSpec((3, tr, w), lambda i, s: (0, i, 0))],
            out_specs=pl.BlockSpec((None, tr, w), lambda i, s: (s[1], i, 0))),
        out_shape=jax.ShapeDtypeStruct((2, rh, w), F32),
        compiler_params=_params(("arbitrary",)),
    )(sel, s1, b2)


BIG = (("gdn_w_out", 1), ("sb_w_q", 1), ("sb_w_out", 1), ("ffn_w_out", 1), ("ple_w_gate", 1), ("w_kv", 1),
       ("ple_w_proj", 2))
SMALL = ("ln_mix", "ln_ffn", "ln_ple", "gdn_a_log", "gdn_dt_bias", "gdn_norm", "kv_norm", "k_norm", "sb_q_norm")
WEIGHTS = ("ln_mix", "ln_ffn", "ln_ple", "gdn_w_in", "gdn_conv", "gdn_a_log", "gdn_dt_bias", "gdn_norm", "gdn_w_out",
           "kv_norm", "w_kv", "k_norm", "sb_w_q", "sb_q_norm", "sb_w_out", "ffn_w_in", "ffn_w_out", "ple_w_proj",
           "ple_w_gate")
PACK_ALIGN = 256


ROW_TILE = 16


def _rows_of(shape, tile=ROW_TILE):
    return -(-math.prod(shape) // (PACK_COLS * tile)) * tile


WEIGHT_ALIGN = 32


def _pack_rows(arrs, lead, align=PACK_ALIGN):
    parts = []
    for a in arrs:
        if a.shape[-1] == PACK_COLS:
            parts.append(a.reshape(lead + (-1, PACK_COLS)))
            continue
        flat = a.reshape(lead + (-1,))
        pad = _rows_of(a.shape[len(lead):]) * PACK_COLS - flat.shape[-1]
        if pad:
            flat = jnp.pad(flat, [(0, 0)] * len(lead) + [(0, pad)])
        parts.append(flat.reshape(lead + (-1, PACK_COLS)))
    rows = sum(q.shape[len(lead)] for q in parts)
    filler = -rows % align
    if filler:
        parts.append(jnp.zeros(lead + (filler, PACK_COLS), parts[0].dtype))
    return jnp.concatenate(parts, axis=len(lead))


def _own_slot(buf, chip):
    mine = lax.broadcasted_iota(jnp.int32, (N_CHIPS, 1, 1), 0) == chip
    slots = jnp.where(mine, buf[None], jnp.zeros((), buf.dtype))
    return slots.reshape(N_CHIPS, 2, buf.shape[0] // 2, buf.shape[1])


def _unpack_rows(buf, shapes, lead):
    out, r0 = [], 0
    for s in shapes:
        rows = _rows_of(s)
        flat = buf[(slice(None),) * len(lead) + (slice(r0, r0 + rows),)].reshape(lead + (-1,))
        out.append(flat[..., :math.prod(s)].reshape(lead + tuple(s)))
        r0 += rows
    return out


def _unshard(g, axis):
    g = jnp.moveaxis(g, 0, axis)
    s = g.shape
    return g.reshape(s[:axis] + (s[axis] * s[axis + 1],) + s[axis + 2:])


def _shard(full, axis):
    s = full.shape
    return jnp.moveaxis(full.reshape(s[:axis] + (N_CHIPS, s[axis] // N_CHIPS) + s[axis + 1:]), axis, 0)


def _to4(a):
    return a.reshape(HEADS, -1, 1, CHUNK).transpose(1, 0, 2, 3)


def _from4(a):
    return a.transpose(1, 0, 2, 3).reshape(HEADS, -1)


def _row(vec):
    flat = vec.reshape(-1)
    rows = _rows_of(flat.shape, 1)
    return jnp.pad(flat, (0, rows * PACK_COLS - flat.shape[0])).reshape(rows, PACK_COLS)


def kernel(x, p, ln_mix, ln_ffn, ln_ple, gdn_w_in, gdn_conv, gdn_a_log, gdn_dt_bias, gdn_norm, gdn_w_out, kv_norm, w_kv, k_norm, sb_w_q, sb_q_norm, sb_w_out, ffn_w_in, ffn_w_out, ple_w_proj, ple_w_gate, loss_target, m_ln_mix, m_ln_ffn, m_ln_ple, m_gdn_w_in, m_gdn_conv, m_gdn_a_log, m_gdn_dt_bias, m_gdn_norm, m_gdn_w_out, m_kv_norm, m_w_kv, m_k_norm, m_sb_w_q, m_sb_q_norm, m_sb_w_out, m_ffn_w_in, m_ffn_w_out, m_ple_w_proj, m_ple_w_gate, v_ln_mix, v_ln_ffn, v_ln_ple, v_gdn_w_in, v_gdn_conv, v_gdn_a_log, v_gdn_dt_bias, v_gdn_norm, v_gdn_w_out, v_kv_norm, v_w_kv, v_k_norm, v_sb_w_q, v_sb_q_norm, v_sb_w_out, v_ffn_w_in, v_ffn_w_out, v_ple_w_proj, v_ple_w_gate):
    w = dict(ln_mix=ln_mix, ln_ffn=ln_ffn, ln_ple=ln_ple, gdn_w_in=gdn_w_in, gdn_conv=gdn_conv, gdn_a_log=gdn_a_log,
             gdn_dt_bias=gdn_dt_bias, gdn_norm=gdn_norm, gdn_w_out=gdn_w_out, kv_norm=kv_norm, w_kv=w_kv, k_norm=k_norm,
             sb_w_q=sb_w_q, sb_q_norm=sb_q_norm, sb_w_out=sb_w_out, ffn_w_in=ffn_w_in, ffn_w_out=ffn_w_out,
             ple_w_proj=ple_w_proj, ple_w_gate=ple_w_gate)
    mom1 = dict(ln_mix=m_ln_mix, ln_ffn=m_ln_ffn, ln_ple=m_ln_ple, gdn_w_in=m_gdn_w_in, gdn_conv=m_gdn_conv,
                gdn_a_log=m_gdn_a_log, gdn_dt_bias=m_gdn_dt_bias, gdn_norm=m_gdn_norm, gdn_w_out=m_gdn_w_out,
                kv_norm=m_kv_norm, w_kv=m_w_kv, k_norm=m_k_norm, sb_w_q=m_sb_w_q, sb_q_norm=m_sb_q_norm,
                sb_w_out=m_sb_w_out, ffn_w_in=m_ffn_w_in, ffn_w_out=m_ffn_w_out, ple_w_proj=m_ple_w_proj,
                ple_w_gate=m_ple_w_gate)
    mom2 = dict(ln_mix=v_ln_mix, ln_ffn=v_ln_ffn, ln_ple=v_ln_ple, gdn_w_in=v_gdn_w_in, gdn_conv=v_gdn_conv,
                gdn_a_log=v_gdn_a_log, gdn_dt_bias=v_gdn_dt_bias, gdn_norm=v_gdn_norm, gdn_w_out=v_gdn_w_out,
                kv_norm=v_kv_norm, w_kv=v_w_kv, k_norm=v_k_norm, sb_w_q=v_sb_w_q, sb_q_norm=v_sb_q_norm,
                sb_w_out=v_sb_w_out, ffn_w_in=v_ffn_w_in, ffn_w_out=v_ffn_w_out, ple_w_proj=v_ple_w_proj,
                ple_w_gate=v_ple_w_gate)
    depth = ln_mix.shape[0]
    n_a = gdn_w_in.shape[0]
    xi, yi, ci = _mesh_pos()
    chip = 2 * xi + yi
    sel_chip = jnp.stack([chip, ci]).astype(jnp.int32)
    h = x[0]
    tgt = loss_target[0]
    t = h.shape[0]


    def layer_items(i):
        if i < n_a:
            items = [("gdn_w_out", i, 0), ("ffn_w_out", i, 0), ("ple_w_gate", i, 0), ("ple_w_proj", i, 1)]
            return items + ([("w_kv", None, 1)] if i == n_a - 1 else [])
        j = i - n_a
        return [("sb_w_q", j, 0), ("sb_w_out", j, 0), ("ffn_w_out", i, 0), ("ple_w_gate", i, 0), ("ple_w_proj", i, 1)]

    def layer_shards(i):
        return [w[n] if idx is None else w[n][idx] for n, idx, _ in layer_items(i)]

    def layer_slots(i):
        packed = _pack_rows([q.astype(BF16) for q in layer_shards(i)], (), align=WEIGHT_ALIGN)
        own = [packed, ffn_w_in[i].astype(BF16)] + ([gdn_w_in[i].astype(BF16)] if i < n_a else [])
        return [_own_slot(b, chip) for b in own]

    def layer_weights(i, got):
        parts = _unpack_rows(got[0].reshape(N_CHIPS, -1, PACK_COLS), [q.shape for q in layer_shards(i)], (N_CHIPS,))
        out = {n: _unshard(g, ax) for (n, _, ax), g in zip(layer_items(i), parts)}
        out["ffn_w_in"] = got[1].reshape((N_CHIPS, 1) + ffn_w_in.shape[1:])
        if i < n_a:
            out["gdn_w_in"] = _unshard(got[2].reshape((N_CHIPS,) + gdn_w_in.shape[1:]), 1)
        return out

    slots = [layer_slots(i) for i in range(depth)]
    wl = [layer_weights(0, all_gather_chips(slots[0], name="all_gather_weights"))]
    conv_rows = _rows_of(gdn_conv.shape, 1)
    small_rows = sum(_rows_of(w[n].shape, 1) for n in SMALL)
    buf_rows = -(-(small_rows + N_CHIPS * conv_rows) // 8) * 8
    conv_buf = jnp.zeros((buf_rows, PACK_COLS), F32)
    conv_buf = lax.dynamic_update_slice(conv_buf, _row(gdn_conv) * (ci == 0).astype(F32), (chip * conv_rows, 0))
    conv_all = all_reduce_small(conv_buf, name="all_reduce_small")[:N_CHIPS * conv_rows]
    conv_full = _unshard(conv_all.reshape(N_CHIPS, -1)[:, :math.prod(gdn_conv.shape)].reshape((N_CHIPS,) + gdn_conv.shape), 2)

    def resid_rms(r, res, g):
        out = r + res
        return out, _rms(out, g)

    def ple_rms(r, res, emb, g):
        out = res + emb * _sigmoid(r)
        return r, out, (None if g is None else _rms(out, g))

    hn_next = None
    saved = []
    k_sh = v_sh = None
    mid = None
    for i in range(depth):
        s = dict(h0=h)
        wi = wl[i]
        nxt = slots[i + 1] if i + 1 < depth else ()
        s["hn"] = hn = rms_fwd(h, ln_mix[i:i + 1], name="rms_fwd") if hn_next is None else hn_next
        if i < n_a:
            w_in = wi["gdn_w_in"]
            s["w_m"], s["w_abt"] = w_in[:, :4 * WIDTH], w_in[:, 4 * WIDTH:].T
            s["proj"] = proj = matmul(hn, s["w_m"], "nn", out_dtype=BF16, name="mm_gdn_in")
            s["ab"] = ab = matmul(s["w_abt"], hn, "nt", name="mm_gdn_ab")
            s["a_log"], s["dt"] = gdn_a_log[i][:, None], gdn_dt_bias[i][:, None]
            g8, b8 = gates_fwd(ab, s["a_log"], s["dt"], name="gates_fwd")
            s["g4"], s["b4"] = _to4(g8), _to4(b8)
            s["qkv"] = qkv = conv_fwd(proj, conv_full[i], name="conv_fwd")
            s["o"], s["s0"], s["tinv"], *got = gdn_fwd(qkv, s["g4"], s["b4"], gather=nxt, name="gdn_fwd")
            s["y"] = mixed = gatenorm_fwd(s["o"], proj, gdn_norm[i:i + 1], name="gatenorm_fwd")
            w_mix_out = wi["gdn_w_out"]
        else:
            j = i - n_a
            s["qraw"] = qraw = matmul(hn, wi["sb_w_q"], "nn", name="mm_sq")
            s["q"] = q = headnorm_fwd(qraw, 0, sb_q_norm[j:j + 1], HEAD_DIM ** -0.5, name="headnorm_q")
            s["o"], *got = sb_fwd(q, k_sh, v_sh, gather=nxt, name="sb_fwd")
            mixed, w_mix_out = s["o"], wi["sb_w_out"]
        if nxt:
            wl.append(layer_weights(i + 1, got))
        s["h1"], s["hn2"] = h, hn2 = matmul_rows(mixed, w_mix_out, [h], [ln_ffn[i:i + 1]], resid_rms, (F32, BF16),
                                                 name="mm_out_rms")
        s["gu"], s["act"] = ffn_in_swiglu(hn2, wi["ffn_w_in"], name="ffn_in_swiglu")
        s["h2"], s["hn3"] = h, hn3 = matmul_rows(s["act"], wi["ffn_w_out"], [h], [ln_ple[i:i + 1]], resid_rms,
                                                 (F32, BF16), name="mm_ffn_out_rms")
        s["pp"] = pp = matmul(p[i, 0], wi["ple_w_proj"], "nn", name="mm_ple_proj")
        if i + 1 < depth:
            s["gt"], h, hn_next = matmul_rows(hn3, wi["ple_w_gate"], [h, pp], [ln_mix[i + 1:i + 2]], ple_rms,
                                              (F32, F32, BF16), name="mm_ple_gate_rms")
        else:
            s["gt"], h = matmul_rows(hn3, wi["ple_w_gate"], [h, pp], [], lambda r, res, e: ple_rms(r, res, e, None)[:2],
                                     (F32, F32), name="mm_ple_gate")
        saved.append(s)
        if i == n_a - 1:
            mid = dict(h=h)
            mid["hk"] = hk = rms_fwd(h, kv_norm[None, :], name="rms_fwd")
            mid["kv"] = kv = matmul(hk, wi["w_kv"], "nn", name="mm_kv")
            k_sh = headnorm_fwd(kv, 0, k_norm[None, :], 1.0, name="headnorm_k")
            v_sh = kv[:, WIDTH:].astype(BF16)

    dh, sq = loss_head(h, tgt, name="loss_head")
    loss = lax.psum(0.5 * jnp.sum(sq) / h.shape[1], ("x", "y", "c"))

    gw = {n: [None] * w[n].shape[0] for n in WEIGHTS if w[n].ndim >= 2 and n not in ("w_kv",)}
    dks, dvs = [], []
    reduced = [None] * depth

    def grad_buffers(i):
        pieces = [_shard(gw[n] if idx is None else gw[n][idx], ax) for n, idx, ax in layer_items(i)]
        bufs = [_pack_rows(pieces, (N_CHIPS,)), gw["ffn_w_in"][i]] + ([_shard(gw["gdn_w_in"][i], 1)] if i < n_a else [])
        return [g.reshape(N_CHIPS, 2, g.shape[1] // 2, g.shape[2]) for g in bufs]

    def chip_sums(g5s, from_sibling):
        return [add_selected(sel_chip, g5, fs, name="add_selected") for g5, fs in zip(g5s, from_sibling)]

    def reduce_end(i, s1s, from_chips):
        s2s = [add_chip_sums(sel_chip, s1, fc, name="add_chip_sums") for (s1, _), fc in zip(s1s, from_chips)]
        reduced[i] = sibling_merge(s2s, name="sibling_merge")

    waiting = None
    for i in reversed(range(depth)):
        s = saved[i]
        if i == n_a - 1:
            dkraw, gw["k_norm"] = headnorm_bwd(mid["kv"], 0, k_norm[None, :], 1.0, tuple(dks), name="headnorm_k_bwd")
            dkv = jnp.concatenate([dkraw, sum_cast(dvs, BF16, name="sum_dv")], axis=1)
            dhk = matmul(dkv, wl[i]["w_kv"], "nt", name="mm_kv_dx")
            gw["w_kv"] = matmul(mid["hk"], dkv, "tn", name="mm_kv_dw")
            dh, gw["kv_norm"] = rms_bwd(mid["h"], kv_norm[None, :], dhk, dh, name="rms_bwd")
        dpp, dgt = ple_bwd(dh, s["pp"], s["gt"], name="ple_bwd")
        gw["ple_w_proj"][i] = matmul(p[i, 0], dpp, "tn", name="mm_ple_proj_dw")
        gw["ple_w_gate"][i] = matmul(s["hn3"], dgt, "tn", name="mm_sq_dw")
        dhn3 = matmul(dgt, wl[i]["ple_w_gate"], "nt", name="mm_sq_dx")
        dh, gw["ln_ple"][i] = rms_bwd(s["h2"], ln_ple[i:i + 1], dhn3, dh, name="rms_bwd")
        dgu, *from_sibling = ffn_out_dx_swiglu(dh, wl[i]["ffn_w_out"], s["gu"], swap=waiting[1] if waiting else (),
                                               name="ffn_out_dx_swiglu")
        sums = chip_sums(waiting[1], from_sibling) if waiting else []
        riding = [s1b for _, s1b in sums]
        gw["ffn_w_out"][i] = matmul(s["act"], dh, "tn", name="mm_ffn_out_dw")
        dhn2 = matmul(dgu, wl[i]["ffn_w_in"], "nt", b_chips=0, name="mm_ffn_in_dx")
        gw["ffn_w_in"][i] = matmul(s["hn2"], dgu, "tn", out_chips=True, name="mm_ffn_in_dw")
        dh, gw["ln_ffn"][i] = rms_bwd(s["h1"], ln_ffn[i:i + 1], dhn2, dh, name="rms_bwd")
        if i < n_a:
            dy = matmul(dh, wl[i]["gdn_w_out"], "nt", name="mm_sq_dx")
            gw["gdn_w_out"][i] = matmul(s["y"], dh, "tn", name="mm_sq_dw")
            do, dproj, gw["gdn_norm"][i] = gatenorm_bwd(s["o"], s["proj"], gdn_norm[i:i + 1], dy, name="gatenorm_bwd")
            dqkv, dg4, db4, *from_chips = gdn_bwd(s["qkv"], s["g4"], s["b4"], s["s0"], s["tinv"], do, exchange=riding,
                                                  name="gdn_bwd")
            dab, dal, ddt = gates_bwd(s["ab"], s["a_log"], s["dt"], _from4(dg4), _from4(db4), name="gates_bwd")
            gw["gdn_a_log"][i], gw["gdn_dt_bias"][i] = dal[:, 0], ddt[:, 0]
            dproj, gw["gdn_conv"][i] = conv_bwd(s["proj"], conv_full[i], dqkv, dproj, name="conv_bwd")
            dhn = matmul(dproj, s["w_m"], "nt", name="mm_gdn_in_dx")
            dhn = matmul(dab, s["w_abt"], "tn", add=dhn, name="mm_gdn_ab_dx")
            dwm = matmul(s["hn"], dproj, "tn", name="mm_gdn_in_dw")
            dwab = matmul(dab, s["hn"], "nn", name="mm_gdn_ab_dw")
            gw["gdn_w_in"][i] = jnp.concatenate([dwm, dwab.T], axis=1)
        else:
            j = i - n_a
            do = matmul(dh, wl[i]["sb_w_out"], "nt", out_dtype=BF16, name="mm_sb_out_dx")
            gw["sb_w_out"][j] = matmul(s["o"], dh, "tn", name="mm_sq_dw")
            dq, dk, dv, *from_chips = sb_bwd(s["q"], k_sh, v_sh, do, exchange=riding, name="sb_bwd")
            dks.append(dk)
            dvs.append(dv)
            dqraw, gw["sb_q_norm"][j] = headnorm_bwd(s["qraw"], 0, sb_q_norm[j:j + 1], HEAD_DIM ** -0.5, (dq,),
                                                    name="headnorm_q_bwd")
            dhn = matmul(dqraw, wl[i]["sb_w_q"], "nt", name="mm_sq_dx")
            gw["sb_w_q"][j] = matmul(s["hn"], dqraw, "tn", name="mm_sq_dw")
        dh, gw["ln_mix"][i] = rms_bwd(s["h0"], ln_mix[i:i + 1], dhn, dh, name="rms_bwd")
        if waiting:
            reduce_end(waiting[0], sums, from_chips)
        waiting = (i, grad_buffers(i))
    sums = chip_sums(waiting[1], sibling_swap(waiting[1], name="sibling_swap"))
    reduce_end(waiting[0], sums, chip_exchange([s1b for _, s1b in sums], name="chip_exchange"))
    grad_x = dh[None]

    def stacked(n):
        g = gw[n]
        if isinstance(g, list):
            g = jnp.stack([a.reshape(w[n].shape[1:]) if n in SMALL else a for a in g])
        return g

    small_buf = jnp.concatenate([_row(stacked(n)) for n in SMALL] + [_row(stacked("gdn_conv"))], axis=0)
    small_buf = jnp.pad(small_buf, ((0, buf_rows - small_buf.shape[0]), (0, 0)))
    small_sum = all_reduce_small(small_buf, name="all_reduce_small")
    grads = {}
    r0 = 0
    for n in SMALL:
        rows = _rows_of(w[n].shape, 1)
        grads[n] = small_sum[r0:r0 + rows].reshape(-1)[:math.prod(w[n].shape)].reshape(w[n].shape)
        r0 += rows
    conv_g = small_sum[r0:r0 + N_CHIPS * conv_rows].reshape(-1)[:N_CHIPS * math.prod(gdn_conv.shape)]
    conv_g = conv_g.reshape((gdn_conv.shape[0], CONV_WIDTH, N_CHIPS, gdn_conv.shape[2]))
    grads["gdn_conv"] = lax.dynamic_index_in_dim(conv_g, chip, axis=2, keepdims=False)

    per_layer = {n: [None] * w[n].shape[0] for n, _ in BIG if n != "w_kv"}
    for i in range(depth):
        parts = _unpack_rows(reduced[i][0].reshape(-1, PACK_COLS), [q.shape for q in layer_shards(i)], ())
        for (n, idx, _), g in zip(layer_items(i), parts):
            if idx is None:
                grads[n] = g
            else:
                per_layer[n][idx] = g
    for n, parts in per_layer.items():
        grads[n] = jnp.stack(parts)
    grads["ffn_w_in"] = jnp.stack([reduced[i][1].reshape(ffn_w_in.shape[1:]) for i in range(depth)])
    grads["gdn_w_in"] = jnp.stack([reduced[i][2].reshape(gdn_w_in.shape[1:]) for i in range(n_a)])

    delta, new_m, new_v = {}, {}, {}
    for n in WEIGHTS:
        delta[n], new_m[n], new_v[n] = adamw(w[n], grads[n], mom1[n], mom2[n], name="adamw")
    return (loss, grad_x, *[grads[n] for n in WEIGHTS], *[delta[n] for n in WEIGHTS],
            *[new_m[n] for n in WEIGHTS], *[new_v[n] for n in WEIGHTS])
```

```python
import math

import jax
import jax.numpy as jnp
from jax import lax
from jax.experimental import pallas as pl
from jax.experimental.pallas import tpu as pltpu

F32 = jnp.float32
BF16 = jnp.bfloat16
EPS = 1e-6
HEADS = 8
HEAD_DIM = 128
WIDTH = HEADS * HEAD_DIM
CHUNK = 64
CONV_WIDTH = 4
N_CHIPS = 4
ADAM_LR, ADAM_B1, ADAM_B2, ADAM_EPS, ADAM_WD, ADAM_STEP = 0.001, 0.9, 0.999, 1e-08, 0.01, 10
V7X_VMEM_BYTES = 64 * 1024 * 1024
VMEM_LIMIT = V7X_VMEM_BYTES - 8 * 1024 * 1024
MESH = pl.DeviceIdType.MESH


def _params(sem=None):
    return pltpu.CompilerParams(dimension_semantics=sem, vmem_limit_bytes=VMEM_LIMIT)


def _pick(n, prefs):
    for t in prefs:
        if t <= n and n % t == 0:
            return t
    return n


def _bdot(a, b, dims):
    return lax.dot_general(a.astype(BF16), b.astype(BF16), (((dims[0],), (dims[1],)), ((), ())),
                           preferred_element_type=F32)


NN, NT, TN = (1, 0), (1, 1), (0, 0)


MM_TILES = (1024, 1408, 512, 256, 128)


def matmul(a, b, form, *, out_dtype=F32, add=None, name, b_chips=None, out_chips=False):
    ns = None
    if b_chips is not None:
        ns = b.shape[3]
        b_shape = (b.shape[2], N_CHIPS * ns)
    else:
        b_shape = b.shape
    if form == "nn":
        (m, k), (k2, n) = a.shape, b_shape
    elif form == "nt":
        (m, k), (n, k2) = a.shape, b_shape
    else:
        (k, m), (k2, n) = a.shape, b_shape
    assert k == k2, (a.shape, b.shape, form)
    if out_chips:
        ns = n // N_CHIPS
    tm = _pick(m, MM_TILES)
    tn = _pick(n, MM_TILES)
    tk = k if k <= 1024 else _pick(k, MM_TILES)
    if ns is not None and (form == "nn" or out_chips):
        tn = ns
    if ns is not None and form == "nt":
        tk = ns
    nk = k // tk
    if form == "tn":
        a_spec = pl.BlockSpec((tk, tm), lambda i, j, kk: (kk, i))
    else:
        a_spec = pl.BlockSpec((tm, tk), lambda i, j, kk: (i, kk))
    if b_chips is not None and form == "nn":
        b_spec = pl.BlockSpec((None, None, tk, ns), lambda i, j, kk: (j, b_chips, kk, 0))
    elif b_chips is not None:
        b_spec = pl.BlockSpec((None, None, tn, ns), lambda i, j, kk: (kk, b_chips, j, 0))
    elif form == "nt":
        b_spec = pl.BlockSpec((tn, tk), lambda i, j, kk: (j, kk))
    else:
        b_spec = pl.BlockSpec((tk, tn), lambda i, j, kk: (kk, j))
    if out_chips:
        o_spec = pl.BlockSpec((None, tm, ns), lambda i, j, kk: (j, i, 0))
    else:
        o_spec = pl.BlockSpec((tm, tn), lambda i, j, kk: (i, j))
    dims = {"nn": NN, "nt": NT, "tn": TN}[form]
    has_add = add is not None

    def body(*refs):
        a_ref, b_ref = refs[:2]
        add_ref = refs[2] if has_add else None
        o_ref = refs[2 + has_add]

        def finish(r):
            if has_add:
                r = r + add_ref[...].astype(F32)
            o_ref[...] = r.astype(out_dtype)

        part = _bdot(a_ref[...], b_ref[...], dims)
        if nk == 1:
            finish(part)
            return
        acc_ref = refs[3 + has_add]
        kk = pl.program_id(2)

        @pl.when(kk == 0)
        def _():
            acc_ref[...] = part

        @pl.when(kk > 0)
        def _():
            acc_ref[...] += part

        @pl.when(kk == nk - 1)
        def _():
            finish(acc_ref[...])

    in_specs = [a_spec, b_spec] + ([o_spec] if has_add else [])
    args = (a, b) + ((add,) if has_add else ())
    return pl.pallas_call(
        body, name=name, grid=(m // tm, n // tn, nk), in_specs=in_specs, out_specs=o_spec,
        out_shape=jax.ShapeDtypeStruct((N_CHIPS, m, ns) if out_chips else (m, n), out_dtype),
        scratch_shapes=[pltpu.VMEM((tm, tn), F32)] if nk > 1 else [],
        compiler_params=_params(("parallel", "parallel", "arbitrary")),
    )(*args)


def matmul_rows(a, b, extra, params, epilogue, out_dtypes, *, name):
    (m, k), (k2, n) = a.shape, b.shape
    assert k == k2, (a.shape, b.shape)
    tm = _pick(m, (1024, 512, 256, 128))
    tk = k if k <= 1024 else _pick(k, MM_TILES)
    nk = k // tk
    ne, npar, no = len(extra), len(params), len(out_dtypes)

    def body(*refs):
        a_ref, b_ref = refs[:2]
        e_refs, p_refs = refs[2:2 + ne], refs[2 + ne:2 + ne + npar]
        o_refs = refs[2 + ne + npar:2 + ne + npar + no]

        def finish(r):
            outs = epilogue(r, *[e[...] for e in e_refs], *[q[...] for q in p_refs])
            for o_ref, val in zip(o_refs, outs):
                o_ref[...] = val.astype(o_ref.dtype)

        part = _bdot(a_ref[...], b_ref[...], NN)
        if nk == 1:
            finish(part)
            return
        acc_ref = refs[-1]
        kk = pl.program_id(1)

        @pl.when(kk == 0)
        def _():
            acc_ref[...] = part

        @pl.when(kk > 0)
        def _():
            acc_ref[...] += part

        @pl.when(kk == nk - 1)
        def _():
            finish(acc_ref[...])

    row = pl.BlockSpec((tm, n), lambda i, kk: (i, 0))
    in_specs = ([pl.BlockSpec((tm, tk), lambda i, kk: (i, kk)), pl.BlockSpec((tk, n), lambda i, kk: (kk, 0))]
                + [row] * ne + [pl.BlockSpec((1, n), lambda i, kk: (0, 0))] * npar)
    return pl.pallas_call(
        body, name=name, grid=(m // tm, nk), in_specs=in_specs, out_specs=[row] * no,
        out_shape=[jax.ShapeDtypeStruct((m, n), dt) for dt in out_dtypes],
        scratch_shapes=[pltpu.VMEM((tm, n), F32)] if nk > 1 else [],
        compiler_params=_params(("parallel", "arbitrary")),
    )(a, b, *extra, *params)


def _const(c):
    return lambda j: c


def rowwise(fn, rows, params, outs, accs=(), *, name, tm, ncol=1):
    t = rows[0][0].shape[0]
    tm = min(tm, t)
    assert t % tm == 0
    n_rows, n_par, n_out, n_acc = len(rows), len(params), len(outs), len(accs)

    def body(*refs):
        j, i = pl.program_id(0), pl.program_id(1)
        ins = [r[...] for r in refs[:n_rows + n_par]]
        o_refs = refs[n_rows + n_par:n_rows + n_par + n_out]
        a_refs = refs[n_rows + n_par + n_out:]
        row_outs, acc_outs = fn(*ins)
        for r, val in zip(o_refs, row_outs):
            r[...] = val.astype(r.dtype)
        for r, val, spec in zip(a_refs, acc_outs, accs):
            first = (i == 0) & (j == 0) if spec[4] else (i == 0)

            @pl.when(first)
            def _(r=r, val=val):
                r[...] = val.astype(F32)

            @pl.when(jnp.logical_not(first))
            def _(r=r, val=val):
                r[...] += val.astype(F32)

    in_specs = [pl.BlockSpec((tm, w), lambda j, i, cf=cf: (i, cf(j))) for _, w, cf in rows]
    in_specs += [pl.BlockSpec((p.shape[0], w), lambda j, i, cf=cf: (0, cf(j))) for p, w, cf in params]
    out_specs = [pl.BlockSpec((tm, w), lambda j, i, cf=cf: (i, cf(j))) for _, _, w, cf in outs]
    out_specs += [pl.BlockSpec((r, w), lambda j, i, cf=cf: (0, cf(j))) for r, _, w, cf, _ in accs]
    out_shape = [jax.ShapeDtypeStruct((t, tw), dt) for tw, dt, _, _ in outs]
    out_shape += [jax.ShapeDtypeStruct((r, tw), F32) for r, tw, _, _, _ in accs]
    res = pl.pallas_call(
        body, name=name, grid=(ncol, t // tm), in_specs=in_specs, out_specs=out_specs, out_shape=out_shape,
        compiler_params=_params(("arbitrary", "arbitrary")),
    )(*[r[0] for r in rows], *[p[0] for p in params])
    return res[:n_out], res[n_out:]


def _full(arr):
    return (arr, arr.shape[1], _const(0))


def _rms(x, g):
    x = x.astype(F32)
    return x * lax.rsqrt(jnp.mean(x * x, axis=-1, keepdims=True) + EPS) * g.astype(F32)


def _sigmoid(x):
    return 1.0 / (1.0 + jnp.exp(-x))


def _silu(x):
    return x * _sigmoid(x)


def _softplus(x):
    return jnp.maximum(x, 0.0) + jnp.log(1.0 + jnp.exp(-jnp.abs(x)))


def rms_fwd(h, g, *, name):
    d = h.shape[1]
    (hn,), _ = rowwise(lambda x, gg: ((_rms(x, gg),), ()), [_full(h)], [_full(g)],
                       [(d, BF16, d, _const(0))], name=name, tm=512)
    return hn


def rms_bwd(h, g, dhn, dh_res, *, name, matmul_copy=False):
    d = h.shape[1]

    def fn(x, ct, res, gg):
        _, vjp = jax.vjp(_rms, x.astype(F32), gg.astype(F32))
        dx, dg = vjp(ct.astype(F32))
        dh = res.astype(F32) + dx
        return (dh, dh)[:1 + matmul_copy], (dg,)

    outs = [(d, F32, d, _const(0)), (d, BF16, d, _const(0))][:1 + matmul_copy]
    dhs, (dg,) = rowwise(fn, [_full(h), _full(dhn), _full(dh_res)], [_full(g)], outs,
                         [(1, d, d, _const(0), True)], name=name, tm=256)
    return (*dhs, dg)


def _head_rms(x, g, scale):
    x = x.astype(F32)
    return x * lax.rsqrt(jnp.mean(x * x, axis=-1, keepdims=True) + EPS) * (g.astype(F32) * scale)


def headnorm_fwd(x, col0, g, scale, *, name):
    (y,), _ = rowwise(lambda a, gg: ((_head_rms(a, gg, scale),), ()),
                      [(x, HEAD_DIM, lambda j: col0 + j)], [_full(g)],
                      [(WIDTH, BF16, HEAD_DIM, lambda j: j)], name=name, tm=1024, ncol=HEADS)
    return y


def headnorm_bwd(x, col0, g, scale, dys, *, name, out_dtype=BF16):
    def fn(a, *rest):
        cts, gg = rest[:-1], rest[-1]
        ct = sum(c.astype(F32) for c in cts)
        _, vjp = jax.vjp(lambda a_, g_: _head_rms(a_, g_, scale), a.astype(F32), gg.astype(F32))
        dx, dg = vjp(ct)
        return (dx,), (dg,)

    (dx,), (dg,) = rowwise(fn, [(x, HEAD_DIM, lambda j: col0 + j)] + [(dy, HEAD_DIM, lambda j: j) for dy in dys], [_full(g)],
                           [(WIDTH, out_dtype, HEAD_DIM, lambda j: j)],
                           [(1, HEAD_DIM, HEAD_DIM, _const(0), True)], name=name, tm=1024, ncol=HEADS)
    return dx, dg


def sum_cast(parts, dtype, *, name):
    wd = parts[0].shape[1]
    (out,), _ = rowwise(lambda *a: ((sum(b.astype(F32) for b in a),), ()), [_full(a) for a in parts], [],
                        [(wd, dtype, wd, _const(0))], name=name, tm=512)
    return out


def _gatenorm(o, gate, g):
    return _head_rms(o, g, 1.0) * _silu(gate.astype(F32))


def gatenorm_fwd(o, proj, g, *, name):
    (y,), _ = rowwise(lambda a, gt, gg: ((_gatenorm(a, gt, gg),), ()),
                      [(o, HEAD_DIM, lambda j: j), (proj, HEAD_DIM, lambda j: 3 * HEADS + j)], [_full(g)],
                      [(WIDTH, BF16, HEAD_DIM, lambda j: j)], name=name, tm=1024, ncol=HEADS)
    return y


def gatenorm_bwd(o, proj, g, dy, *, name):
    def fn(a, gt, ct, gg):
        _, vjp = jax.vjp(_gatenorm, a.astype(F32), gt.astype(F32), gg.astype(F32))
        da, dgt, dg = vjp(ct.astype(F32))
        return (da, dgt), (dg,)

    (do, dproj), (dg,) = rowwise(
        fn, [(o, HEAD_DIM, lambda j: j), (proj, HEAD_DIM, lambda j: 3 * HEADS + j), (dy, HEAD_DIM, lambda j: j)],
        [_full(g)],
        [(WIDTH, F32, HEAD_DIM, lambda j: j), (4 * WIDTH, BF16, HEAD_DIM, lambda j: 3 * HEADS + j)],
        [(1, HEAD_DIM, HEAD_DIM, _const(0), True)], name=name, tm=1024, ncol=HEADS)
    return do, dproj, dg


def _swiglu(g, u):
    return _silu(g.astype(F32)) * u.astype(F32)


def ffn_in_swiglu(hn, w4, *, name):
    t, dm = hn.shape
    ns = w4.shape[3]
    half = N_CHIPS // 2
    tm = min(512, t)

    def body(a_ref, w_ref, gu_ref, act_ref):
        a = a_ref[...]
        for j in range(half):
            g = _bdot(a, w_ref[j, 0], NN)
            u = _bdot(a, w_ref[half + j, 0], NN)
            gu_ref[:, j * ns:(j + 1) * ns] = g.astype(BF16)
            gu_ref[:, (half + j) * ns:(half + j + 1) * ns] = u.astype(BF16)
            act_ref[:, j * ns:(j + 1) * ns] = (_silu(g) * u).astype(BF16)

    return pl.pallas_call(
        body, name=name, grid=(t // tm,),
        in_specs=[pl.BlockSpec((tm, dm), lambda i: (i, 0)), pl.BlockSpec(w4.shape, lambda i: (0, 0, 0, 0))],
        out_specs=[pl.BlockSpec((tm, N_CHIPS * ns), lambda i: (i, 0)), pl.BlockSpec((tm, half * ns), lambda i: (i, 0))],
        out_shape=[jax.ShapeDtypeStruct((t, N_CHIPS * ns), BF16), jax.ShapeDtypeStruct((t, half * ns), BF16)],
        compiler_params=_params(("parallel",)),
    )(hn, w4)


def ffn_out_dx_swiglu(dh, w_out, gu, *, name, swap=()):
    t, dm = dh.shape
    f = w_out.shape[0]
    tm = min(256, t)
    n = t // tm
    nw = len(swap)

    def body(*refs):
        dh_ref, w_ref, gu_ref = refs[:3]
        o_ref = refs[3 + nw]
        comm = _Swap(refs[3:3 + nw], refs[4 + nw:4 + 2 * nw], *refs[4 + 2 * nw:]) if nw else None
        if nw:
            @pl.when(pl.program_id(0) == 0)
            def _():
                comm.start()

        dact = _bdot(dh_ref[...], w_ref[...], NT)
        g, u = gu_ref[:, :f].astype(F32), gu_ref[:, f:].astype(F32)
        sg = _sigmoid(g)
        gs = g * sg
        o_ref[:, :f] = (dact * u * (sg + gs * (1.0 - sg))).astype(BF16)
        o_ref[:, f:] = (dact * gs).astype(BF16)

        if nw:
            @pl.when(pl.program_id(0) == n - 1)
            def _():
                comm.finish()

    return pl.pallas_call(
        body, name=name, grid=(n,),
        in_specs=[pl.BlockSpec((tm, dm), lambda i: (i, 0)), pl.BlockSpec((f, dm), lambda i: (0, 0)),
                  pl.BlockSpec((tm, 2 * f), lambda i: (i, 0))] + [ANY] * nw,
        out_specs=[pl.BlockSpec((tm, 2 * f), lambda i: (i, 0))] + [ANY] * nw,
        out_shape=[jax.ShapeDtypeStruct((t, 2 * f), BF16)] + _swap_shapes(swap),
        scratch_shapes=_dma_sems(N_CHIPS * nw) if nw else [],
        compiler_params=_params(("arbitrary",) if nw else ("parallel",)),
    )(dh, w_out, gu, *swap)


def ple_bwd(dh, pp, gt, *, name):
    d = dh.shape[1]

    def fn(ct, b, c):
        s = _sigmoid(c)
        return (ct * s, ct * b * s * (1.0 - s)), ()

    (dpp, dgt), _ = rowwise(fn, [_full(dh), _full(pp), _full(gt)], [],
                            [(d, BF16, d, _const(0)), (d, BF16, d, _const(0))], name=name, tm=512)
    return dpp, dgt


def loss_head(y, tgt, *, name):
    d = y.shape[1]

    def fn(a, b):
        e = a - b
        return (e * (1.0 / d),), (jnp.sum(e * e, axis=0, keepdims=True),)

    (dy,), (sq,) = rowwise(fn, [_full(y), _full(tgt)], [], [(d, F32, d, _const(0))],
                           [(1, d, d, _const(0), True)], name=name, tm=512)
    return dy, sq


def adamw(w, g, m, v, *, name):
    shape = w.shape
    cols = shape[-1]
    flat = lambda a: a.reshape(-1, cols)
    bc1 = 1.0 - ADAM_B1 ** ADAM_STEP
    bc2 = 1.0 - ADAM_B2 ** ADAM_STEP

    def fn(w_, g_, m_, v_):
        m_ = ADAM_B1 * m_ + (1.0 - ADAM_B1) * g_
        v_ = ADAM_B2 * v_ + (1.0 - ADAM_B2) * (g_ * g_)
        delta = -ADAM_LR * ((m_ / bc1) / (jnp.sqrt(v_ / bc2) + ADAM_EPS) + ADAM_WD * w_)
        return (delta, m_, v_), ()

    o = (cols, F32, cols, _const(0))
    (d_, m_, v_), _ = rowwise(fn, [_full(flat(w)), _full(flat(g)), _full(flat(m)), _full(flat(v))], [],
                              [o, o, o], name=name, tm=256)
    return d_.reshape(shape), m_.reshape(shape), v_.reshape(shape)


CONV_STRIP = 256


def _shift_down(x, d):
    if d == 0:
        return x
    rows = lax.broadcasted_iota(jnp.int32, x.shape, 0)
    return jnp.where(rows >= d, pltpu.roll(x, d, 0), 0.0)


def _shift_up(x, d):
    if d == 0:
        return x
    t = x.shape[0]
    rows = lax.broadcasted_iota(jnp.int32, x.shape, 0)
    return jnp.where(rows < t - d, pltpu.roll(x, t - d, 0), 0.0)


def _conv(x, w):
    acc = None
    for j in range(CONV_WIDTH):
        term = _shift_down(x, CONV_WIDTH - 1 - j) * w[j:j + 1, :]
        acc = term if acc is None else acc + term
    return acc


def conv_fwd(proj, w, *, name):
    t = proj.shape[0]
    per = WIDTH // CONV_STRIP

    def body(x_ref, w_ref, o_ref):
        o_ref[0] = _silu(_conv(x_ref[...].astype(F32), w_ref[...]))

    return pl.pallas_call(
        body, name=name, grid=(3 * per,),
        in_specs=[pl.BlockSpec((t, CONV_STRIP), lambda j: (0, j)), pl.BlockSpec((CONV_WIDTH, CONV_STRIP), lambda j: (0, j))],
        out_specs=pl.BlockSpec((1, t, CONV_STRIP), lambda j: (j // per, 0, j % per)),
        out_shape=jax.ShapeDtypeStruct((3, t, WIDTH), F32),
        compiler_params=_params(("parallel",)),
    )(proj, w)


def conv_bwd(proj, w, dqkv, dproj, *, name):
    t = proj.shape[0]
    per = WIDTH // CONV_STRIP

    def body(x_ref, w_ref, d_ref, _, dx_ref, dw_ref):
        x, w_ = x_ref[...].astype(F32), w_ref[...]
        c = _conv(x, w_)
        s = _sigmoid(c)
        dc = d_ref[0] * (s + c * s * (1.0 - s))
        dx = None
        for j in range(CONV_WIDTH):
            d = CONV_WIDTH - 1 - j
            term = _shift_up(dc, d) * w_[j:j + 1, :]
            dx = term if dx is None else dx + term
            dw_ref[j:j + 1, :] = jnp.sum(dc * _shift_down(x, d), axis=0, keepdims=True)
        dx_ref[...] = dx.astype(dx_ref.dtype)

    return pl.pallas_call(
        body, name=name, grid=(3 * per,),
        in_specs=[pl.BlockSpec((t, CONV_STRIP), lambda j: (0, j)), pl.BlockSpec((CONV_WIDTH, CONV_STRIP), lambda j: (0, j)),
                  pl.BlockSpec((1, t, CONV_STRIP), lambda j: (j // per, 0, j % per)), pl.BlockSpec(memory_space=pl.ANY)],
        out_specs=[pl.BlockSpec((t, CONV_STRIP), lambda j: (0, j)), pl.BlockSpec((CONV_WIDTH, CONV_STRIP), lambda j: (0, j))],
        out_shape=[jax.ShapeDtypeStruct(dproj.shape, dproj.dtype), jax.ShapeDtypeStruct((CONV_WIDTH, 3 * WIDTH), F32)],
        input_output_aliases={3: 0},
        compiler_params=_params(("parallel",)),
    )(proj, w, dqkv, dproj)


def _gdn_gates(ab, a_log, dt_bias):
    a_in, b_in = ab[:HEADS], ab[HEADS:]
    g = -jnp.exp(a_log) * _softplus(a_in + dt_bias)
    return g, _sigmoid(b_in)


def gates_fwd(ab, a_log, dt_bias, *, name):
    t = ab.shape[1]

    def body(ab_ref, al_ref, dt_ref, g_ref, b_ref):
        g_ref[...], b_ref[...] = _gdn_gates(ab_ref[...], al_ref[...], dt_ref[...])

    s = jax.ShapeDtypeStruct((HEADS, t), F32)
    return pl.pallas_call(body, name=name, out_shape=[s, s], compiler_params=_params())(ab, a_log, dt_bias)


def gates_bwd(ab, a_log, dt_bias, dg, dbeta, *, name):
    t = ab.shape[1]

    def body(ab_ref, al_ref, dt_ref, dg_ref, db_ref, dab_ref, dal_ref, ddt_ref):
        _, vjp = jax.vjp(_gdn_gates, ab_ref[...], al_ref[...], dt_ref[...])
        dab_ref[...], dal_ref[...], ddt_ref[...] = vjp((dg_ref[...], db_ref[...]))

    c = jax.ShapeDtypeStruct((HEADS, 1), F32)
    return pl.pallas_call(body, name=name, out_shape=[jax.ShapeDtypeStruct((2 * HEADS, t), F32), c, c],
                          compiler_params=_params())(ab, a_log, dt_bias, dg, dbeta)


def _split3(x):
    hi = x.astype(BF16)
    r1 = x - hi.astype(F32)
    mid = r1.astype(BF16)
    lo = (r1 - mid.astype(F32)).astype(BF16)
    return hi, mid, lo


def _dot01(x, m01):
    hi, mid, lo = _split3(x)
    m01 = m01.astype(BF16)
    return _bdot(hi, m01, NN) + _bdot(mid, m01, NN) + _bdot(lo, m01, NN)


def _hdot(a, b, dims=NN):
    a_hi, b_hi = a.astype(BF16), b.astype(BF16)
    a_lo, b_lo = (a - a_hi.astype(F32)).astype(BF16), (b - b_hi.astype(F32)).astype(BF16)
    return _bdot(a_hi, b_hi, dims) + (_bdot(a_hi, b_lo, dims) + _bdot(a_lo, b_hi, dims))


def _rowsum(x):
    return jnp.sum(x, axis=1, keepdims=True)


def _colsum(x):
    return jnp.sum(x, axis=0, keepdims=True)


class _Heads:
    def __init__(self, vals):
        self.v = list(vals)

    def _bin(self, other, f):
        if isinstance(other, _Heads):
            return _Heads(f(a, b) for a, b in zip(self.v, other.v))
        return _Heads(f(a, other) for a in self.v)

    def __add__(self, o):
        return self._bin(o, lambda a, b: a + b)

    __radd__ = __add__

    def __sub__(self, o):
        return self._bin(o, lambda a, b: a - b)

    def __rsub__(self, o):
        return self._bin(o, lambda a, b: b - a)

    def __mul__(self, o):
        return self._bin(o, lambda a, b: a * b)

    __rmul__ = __mul__

    def __neg__(self):
        return _Heads(-a for a in self.v)


def _hmap(f, *args):
    n = next(len(a.v) for a in args if isinstance(a, _Heads))
    return _Heads(f(*[a.v[h] if isinstance(a, _Heads) else a for a in args]) for h in range(n))


def _inv_unit_lower(a, eye):
    p = jnp.where(eye, 1.0, 0.0) - a
    ak = a
    for _ in range(int(math.log2(CHUNK)) - 1):
        ak = _hmap(_hdot, ak, ak)
        p = p + _hmap(_hdot, p, ak)
    return p


def _gdn_chunk(qr, kr, v, grow, brow, tinv=None):
    c = CHUNK
    ri = lax.broadcasted_iota(jnp.int32, (c, c), 0)
    ci = lax.broadcasted_iota(jnp.int32, (c, c), 1)
    eye, lower, strict = ri == ci, ri >= ci, ri > ci
    where = lambda m: (lambda a: jnp.where(m, a, 0.0))
    to_col = lambda row: _hmap(lambda r: _rowsum(jnp.where(eye, jnp.broadcast_to(r, (c, c)), 0.0)), row)
    cum_row = _hmap(lambda g: _dot01(jnp.broadcast_to(g, (8, c)), ri <= ci)[0:1], grow)
    gcol, bcol = to_col(cum_row), to_col(brow)
    glast = _hmap(lambda g: _colsum(jnp.where(ri[:, 0:1] == c - 1, g, 0.0)), gcol)
    rq = _hmap(lambda a: lax.rsqrt(_rowsum(a * a) + EPS), qr)
    rk = _hmap(lambda a: lax.rsqrt(_rowsum(a * a) + EPS), kr)
    scale = HEAD_DIM ** -0.5
    qn, kn = qr * (rq * scale), kr * rk
    dec = _hmap(lambda gc, gr: jnp.where(lower, jnp.exp(jnp.minimum(gc - gr, 0.0)), 0.0), gcol, cum_row)
    kk = _hmap(lambda a: _bdot(a, a, NT), kn)
    qk = _hmap(lambda a, b: _bdot(a, b, NT), qn, kn)
    gam_col, e_col, gam_last = _hmap(jnp.exp, gcol), _hmap(jnp.exp, glast - gcol), _hmap(jnp.exp, glast)
    if tinv is None:
        tinv = _inv_unit_lower(_hmap(where(strict), bcol * kk * dec), eye)
    u = _hmap(_hdot, tinv, v * bcol)
    w = _hmap(_hdot, tinv, kn * (bcol * gam_col))
    return dict(eye=eye, lower=lower, strict=strict, gcol=gcol, bcol=bcol, rq=rq, rk=rk, qn=qn, kn=kn,
                dec=dec, kk=kk, qk=qk, gam_col=gam_col, e_col=e_col, gam_last=gam_last, tinv=tinv, u=u, w=w,
                aqk=qk * dec, qt=qn * gam_col, kt=kn * e_col, scale=scale)


def _head_cols(h):
    return slice(h * HEAD_DIM, (h + 1) * HEAD_DIM)


def _bd(dims):
    return lambda a, b: _bdot(a, b, dims)


def gdn_fwd(qkv, g4, b4, *, name, gather=()):
    t = qkv.shape[1]
    n = t // CHUNK
    d = HEAD_DIM
    heads = range(HEADS)
    ng = len(gather)

    def body(*refs):
        qkv_ref, g_ref, b_ref = refs[:3]
        o_ref, s0_ref, t_ref = refs[3 + ng:6 + ng]
        s_ref = refs[6 + 2 * ng]
        comm = _Gather(refs[6 + ng:6 + 2 * ng], *refs[7 + 2 * ng:]) if ng else None

        @pl.when(pl.program_id(0) == 0)
        def _():
            s_ref[...] = jnp.zeros_like(s_ref)
            if ng:
                comm.start()

        qr, kr, v = (_Heads(qkv_ref[j, :, _head_cols(h)] for h in heads) for j in range(3))
        z = _gdn_chunk(qr, kr, v, _Heads(g_ref[0, h] for h in heads), _Heads(b_ref[0, h] for h in heads))
        s0 = _Heads(s_ref[h] for h in heads)
        v_new = z["u"] - _hmap(_bd(NN), z["w"], s0)
        o = _hmap(_bd(NN), z["qt"], s0) + _hmap(_bd(NN), z["aqk"], v_new)
        s_new = s0 * z["gam_last"] + _hmap(_bd(TN), z["kt"], v_new)
        for h in heads:
            s0_ref[0, h] = s0.v[h]
            t_ref[0, h] = z["tinv"].v[h]
            o_ref[:, _head_cols(h)] = o.v[h]
            s_ref[h] = s_new.v[h]

        if ng:
            @pl.when(pl.program_id(0) == n - 1)
            def _():
                comm.finish()

    gspec = pl.BlockSpec((1, HEADS, 1, CHUNK), lambda i: (i, 0, 0, 0))
    return pl.pallas_call(
        body, name=name, grid=(n,),
        in_specs=[pl.BlockSpec((3, CHUNK, WIDTH), lambda i: (0, i, 0)), gspec, gspec] + [ANY] * ng,
        out_specs=[pl.BlockSpec((CHUNK, WIDTH), lambda i: (i, 0)),
                   pl.BlockSpec((1, HEADS, d, d), lambda i: (i, 0, 0, 0)),
                   pl.BlockSpec((1, HEADS, CHUNK, CHUNK), lambda i: (i, 0, 0, 0))] + [ANY] * ng,
        out_shape=[jax.ShapeDtypeStruct((t, WIDTH), F32), jax.ShapeDtypeStruct((n, HEADS, d, d), F32),
                   jax.ShapeDtypeStruct((n, HEADS, CHUNK, CHUNK), F32)]
        + [jax.ShapeDtypeStruct(s.shape, s.dtype) for s in gather],
        input_output_aliases={3 + b: 3 + b for b in range(ng)},
        scratch_shapes=[pltpu.VMEM((HEADS, d, d), F32)] + (_dma_sems(6 * ng) if ng else []),
        compiler_params=_params(("arbitrary",)),
    )(qkv, g4, b4, *gather)


def gdn_bwd(qkv, g4, b4, s0_all, tinv_all, do, *, name, exchange=()):
    t = qkv.shape[1]
    n = t // CHUNK
    d = HEAD_DIM
    c = CHUNK
    heads = range(HEADS)

    ne = len(exchange)

    def body(*refs):
        qkv_ref, g_ref, b_ref, s0_ref, t_ref, do_ref = refs[:6]
        dqkv_ref, dg_ref, db_ref = refs[6 + ne:9 + ne]
        ds_ref = refs[9 + 2 * ne]
        comm = _Exchange(refs[6:6 + ne], refs[9 + ne:9 + 2 * ne], *refs[10 + 2 * ne:]) if ne else None

        @pl.when(pl.program_id(0) == 0)
        def _():
            ds_ref[...] = jnp.zeros_like(ds_ref)
            if ne:
                comm.start()

        qr, kr, v = (_Heads(qkv_ref[j, :, _head_cols(h)] for h in heads) for j in range(3))
        z = _gdn_chunk(qr, kr, v, _Heads(g_ref[0, h] for h in heads), _Heads(b_ref[0, h] for h in heads),
                       tinv=_Heads(t_ref[0, h] for h in heads))
        s0 = _Heads(s0_ref[0, h] for h in heads)
        ds = _Heads(ds_ref[h] for h in heads)
        dout = _Heads(do_ref[:, _head_cols(h)] for h in heads)
        qn, kn, u, w, dec, kk, qk = z["qn"], z["kn"], z["u"], z["w"], z["dec"], z["kk"], z["qk"]
        bcol, gam_col, e_col, gam_last = z["bcol"], z["gam_col"], z["e_col"], z["gam_last"]
        low = lambda a: jnp.where(z["lower"], a, 0.0)
        strict = lambda a: jnp.where(z["strict"], a, 0.0)
        rowsum = lambda a: _hmap(_rowsum, a)
        colsum = lambda a: _hmap(_colsum, a)
        v_new = u - _hmap(_bd(NN), w, s0)
        dv_new = _hmap(_bd(TN), z["aqk"], dout) + _hmap(_bd(NN), z["kt"], ds)
        daqk = _hmap(low, _hmap(_bd(NT), dout, v_new))
        dqt = _hmap(_bd(NT), dout, s0)
        dkt = _hmap(_bd(NT), v_new, ds)
        dgam_last = _hmap(lambda a, b: jnp.sum(a * b, keepdims=True), ds, s0)
        ds_new = _hmap(_bd(TN), z["qt"], dout) + ds * gam_last - _hmap(_bd(TN), w, dv_new)
        dw = -_hmap(_bd(NT), dv_new, s0)
        hd_t = lambda a, b: _hdot(a, b, TN)
        dru = _hmap(hd_t, z["tinv"], dv_new)
        drw = _hmap(hd_t, z["tinv"], dw)
        dal = -_hmap(strict, _hmap(_bd(NT), dru, u) + _hmap(_bd(NT), drw, w))
        t1 = dal * kk * dec
        dkk = dal * bcol * dec
        ddec = dal * bcol * kk + daqk * qk
        dqk = daqk * dec
        s_w = rowsum(drw * kn)
        dbeta_col = rowsum(t1) + rowsum(dru * v) + gam_col * s_w
        dkn = (drw * (bcol * gam_col) + _hmap(_bd(NN), dkk, kn) + _hmap(_bd(TN), dkk, kn) + _hmap(_bd(TN), dqk, qn)
               + dkt * e_col)
        dqn = _hmap(_bd(NN), dqk, kn) + dqt * gam_col
        e_mat = ddec * dec
        de_col = rowsum(dkt * kn)
        diag_of_colsum = rowsum(_hmap(lambda a: jnp.where(z["eye"], jnp.broadcast_to(_colsum(a), (c, c)), 0.0), e_mat))
        dg_cum = rowsum(e_mat) + (bcol * s_w + rowsum(dqt * qn)) * gam_col - de_col * e_col - diag_of_colsum
        dg_last = colsum(de_col * e_col) + dgam_last * gam_last
        dg = colsum(_hmap(lambda a: jnp.where(z["lower"], a, 0.0), dg_cum)) + dg_last
        dbeta = colsum(_hmap(lambda a: jnp.where(z["eye"], a, 0.0), dbeta_col))
        rq, rk = z["rq"], z["rk"]
        dqr = z["scale"] * (rq * dqn - qr * (rq * rq * rq) * rowsum(qr * dqn))
        dkr = rk * dkn - kr * (rk * rk * rk) * rowsum(kr * dkn)
        dv = dru * bcol
        for h in heads:
            ds_ref[h] = ds_new.v[h]
            dg_ref[0, h] = dg.v[h]
            db_ref[0, h] = dbeta.v[h]
            dqkv_ref[0, :, _head_cols(h)] = dqr.v[h]
            dqkv_ref[1, :, _head_cols(h)] = dkr.v[h]
            dqkv_ref[2, :, _head_cols(h)] = dv.v[h]

        if ne:
            @pl.when(pl.program_id(0) == n - 1)
            def _():
                comm.finish()

    rev = lambda i: n - 1 - i
    gspec = pl.BlockSpec((1, HEADS, 1, CHUNK), lambda i: (rev(i), 0, 0, 0))
    return pl.pallas_call(
        body, name=name, grid=(n,),
        in_specs=[pl.BlockSpec((3, CHUNK, WIDTH), lambda i: (0, rev(i), 0)), gspec, gspec,
                  pl.BlockSpec((1, HEADS, d, d), lambda i: (rev(i), 0, 0, 0)),
                  pl.BlockSpec((1, HEADS, CHUNK, CHUNK), lambda i: (rev(i), 0, 0, 0)),
                  pl.BlockSpec((CHUNK, WIDTH), lambda i: (rev(i), 0))] + [ANY] * ne,
        out_specs=[pl.BlockSpec((3, CHUNK, WIDTH), lambda i: (0, rev(i), 0)), gspec, gspec] + [ANY] * ne,
        out_shape=[jax.ShapeDtypeStruct((3, t, WIDTH), F32), jax.ShapeDtypeStruct((n, HEADS, 1, CHUNK), F32),
                   jax.ShapeDtypeStruct((n, HEADS, 1, CHUNK), F32)] + _exchange_shapes(exchange),
        scratch_shapes=[pltpu.VMEM((HEADS, d, d), F32)] + (_dma_sems(3 * ne) if ne else []),
        compiler_params=_params(("arbitrary",)),
    )(qkv, g4, b4, s0_all, tinv_all, do, *exchange)


SB_BLOCK = 256


def _dot01_2(x, m01):
    hi = x.astype(BF16)
    lo = (x - hi.astype(F32)).astype(BF16)
    return _bdot(hi, m01, NN) + _bdot(lo, m01, NN)


SB_HEADS = 2


def _sb_weights(q, kb, carry, mask, upper):
    z = _hmap(_bd(NT), q, kb)
    ls = _hmap(lambda z_: jnp.minimum(z_, 0.0) - jnp.log(1.0 + jnp.exp(-jnp.abs(z_))), z)
    ln = _hmap(lambda l_, z_: jnp.where(mask, l_ - z_, 0.0), ls, z)
    suffix = _hmap(lambda l_: _dot01_2(l_, upper), ln)
    a = _hmap(lambda l_, s_, c_: jnp.where(mask, jnp.exp(l_ + s_ + c_), 0.0), ls, suffix, carry)
    return z, ln, a


SB_DEAD = -105.0


def _sb_alive(s, i, carries):
    top = jnp.max(carries[0])
    for c in carries[1:]:
        top = jnp.maximum(top, jnp.max(c))
    return (s <= i) & (top > SB_DEAD)


def _sb_masks(i, jb, blk):
    ri = lax.broadcasted_iota(jnp.int32, (blk, blk), 0)
    ci = lax.broadcasted_iota(jnp.int32, (blk, blk), 1)
    return (jb * blk + ci) < (i * blk + ri)


def sb_fwd(q, k, v, *, name, gather=()):
    t = q.shape[0]
    blk = min(SB_BLOCK, t)
    d = HEAD_DIM
    hs = range(SB_HEADS)
    ng = len(gather)
    groups, nb = HEADS // SB_HEADS, t // blk

    def body(*refs):
        q_ref, k_ref, v_ref = refs[:3]
        o_ref = refs[3 + ng]
        comm = _Gather(refs[4 + ng:4 + 2 * ng], *refs[4 + 2 * ng:]) if ng else None
        i = pl.program_id(1)
        if ng:
            @pl.when((pl.program_id(0) == 0) & (i == 0))
            def _():
                comm.start()

        qb = _Heads(q_ref[:, _head_cols(h)] for h in hs)
        ri = lax.broadcasted_iota(jnp.int32, (blk, blk), 0)
        ci = lax.broadcasted_iota(jnp.int32, (blk, blk), 1)
        upper = (ri > ci).astype(BF16)

        def step(state):
            s, cs, accs = state
            jb = i - s
            rows = pl.ds(pl.multiple_of(jb * blk, blk), blk)
            kb = _Heads(k_ref[rows, _head_cols(h)] for h in hs)
            vb = _Heads(v_ref[rows, _head_cols(h)] for h in hs)
            _, ln, a = _sb_weights(qb, kb, _Heads(cs), _sb_masks(i, jb, blk), upper)
            cs = _Heads(cs) + _hmap(_rowsum, ln)
            accs = _Heads(accs) + _hmap(_bd(NN), a, vb)
            return s + 1, tuple(cs.v), tuple(accs.v)

        init = (jnp.int32(0), tuple(jnp.zeros((blk, 1), F32) for _ in hs), tuple(jnp.zeros((blk, d), F32) for _ in hs))
        _, _, accs = lax.while_loop(lambda st: _sb_alive(st[0], i, st[1]), step, init)
        for h in hs:
            o_ref[:, _head_cols(h)] = accs[h].astype(o_ref.dtype)

        if ng:
            @pl.when((pl.program_id(0) == groups - 1) & (i == nb - 1))
            def _():
                comm.finish()

    qspec = pl.BlockSpec((blk, SB_HEADS * d), lambda g, i: (i, g))
    kspec = pl.BlockSpec((t, SB_HEADS * d), lambda g, i: (0, g))
    return pl.pallas_call(
        body, name=name, grid=(groups, nb), in_specs=[qspec, kspec, kspec] + [ANY] * ng,
        out_specs=[qspec] + [ANY] * ng,
        out_shape=[jax.ShapeDtypeStruct((t, WIDTH), BF16)] + [jax.ShapeDtypeStruct(s.shape, s.dtype) for s in gather],
        input_output_aliases={3 + b: 1 + b for b in range(ng)},
        scratch_shapes=_dma_sems(6 * ng) if ng else [],
        compiler_params=_params(("arbitrary", "arbitrary") if ng else ("parallel", "arbitrary")),
    )(q, k, v, *gather)


def sb_bwd(q, k, v, do, *, name, exchange=()):
    t = q.shape[0]
    blk = min(SB_BLOCK, t)
    d = HEAD_DIM
    nb = t // blk
    hs = range(SB_HEADS)
    ne = len(exchange)
    groups = HEADS // SB_HEADS

    def body(*refs):
        q_ref, k_ref, v_ref, do_ref = refs[:4]
        dq_ref, dk_ref, dv_ref = refs[4 + ne:7 + ne]
        p_buf, z_buf = refs[7 + 2 * ne:9 + 2 * ne]
        comm = _Exchange(refs[4:4 + ne], refs[7 + ne:7 + 2 * ne], *refs[9 + 2 * ne:]) if ne else None
        i = pl.program_id(1)
        if ne:
            @pl.when((pl.program_id(0) == 0) & (i == 0))
            def _():
                comm.start()

        @pl.when(i == 0)
        def _():
            dk_ref[...] = jnp.zeros_like(dk_ref)
            dv_ref[...] = jnp.zeros_like(dv_ref)

        qb = _Heads(q_ref[:, _head_cols(h)] for h in hs)
        dob = _Heads(do_ref[:, _head_cols(h)] for h in hs)
        ri = lax.broadcasted_iota(jnp.int32, (blk, blk), 0)
        ci = lax.broadcasted_iota(jnp.int32, (blk, blk), 1)
        upper = (ri > ci).astype(BF16)
        lower = (ri < ci).astype(BF16)

        def right_to_left(state):
            s, cs = state
            jb = i - s
            rows = pl.ds(pl.multiple_of(jb * blk, blk), blk)
            kb = _Heads(k_ref[rows, _head_cols(h)] for h in hs)
            vb = _Heads(v_ref[rows, _head_cols(h)] for h in hs)
            z, ln, a = _sb_weights(qb, kb, _Heads(cs), _sb_masks(i, jb, blk), upper)
            p = a * _hmap(_bd(NT), dob, vb)
            dv = _hmap(_bd(TN), a, dob)
            for h in hs:
                p_buf[h, jb] = p.v[h]
                z_buf[h, jb] = z.v[h]
                dv_ref[rows, _head_cols(h)] += dv.v[h]
            return s + 1, tuple((_Heads(cs) + _hmap(_rowsum, ln)).v)

        n_done, _ = lax.while_loop(lambda st: _sb_alive(st[0], i, st[1]), right_to_left,
                                   (jnp.int32(0), tuple(jnp.zeros((blk, 1), F32) for _ in hs)))

        def left_to_right(jb, carry):
            cps, dqs = carry
            rows = pl.ds(pl.multiple_of(jb * blk, blk), blk)
            mask = _sb_masks(i, jb, blk)
            kb = _Heads(k_ref[rows, _head_cols(h)] for h in hs)
            p = _Heads(p_buf[h, jb] for h in hs)
            sg = _hmap(_sigmoid, _Heads(z_buf[h, jb] for h in hs))
            prefix = _hmap(lambda a: _dot01_2(a, lower), p) + _Heads(cps)
            dz = _hmap(lambda a: jnp.where(mask, a, 0.0), p * (1.0 - sg) - sg * prefix)
            dk = _hmap(_bd(TN), dz, qb)
            for h in hs:
                dk_ref[rows, _head_cols(h)] += dk.v[h]
            return tuple((_Heads(cps) + _hmap(_rowsum, p)).v), tuple((_Heads(dqs) + _hmap(_bd(NN), dz, kb)).v)

        _, dqs = lax.fori_loop(i + 1 - n_done, i + 1, left_to_right,
                               (tuple(jnp.zeros((blk, 1), F32) for _ in hs), tuple(jnp.zeros((blk, d), F32) for _ in hs)))
        for h in hs:
            dq_ref[:, _head_cols(h)] = dqs[h]

        if ne:
            @pl.when((pl.program_id(0) == groups - 1) & (i == nb - 1))
            def _():
                comm.finish()

    qspec = pl.BlockSpec((blk, SB_HEADS * d), lambda g, i: (i, g))
    kspec = pl.BlockSpec((t, SB_HEADS * d), lambda g, i: (0, g))
    s = jax.ShapeDtypeStruct((t, WIDTH), F32)
    buf = pltpu.VMEM((SB_HEADS, nb, blk, blk), F32)
    return pl.pallas_call(
        body, name=name, grid=(groups, nb), in_specs=[qspec, kspec, kspec, qspec] + [ANY] * ne,
        out_specs=[qspec, kspec, kspec] + [ANY] * ne, out_shape=[s, s, s] + _exchange_shapes(exchange),
        scratch_shapes=[buf, buf] + (_dma_sems(3 * ne) if ne else []),
        compiler_params=_params(("arbitrary", "arbitrary") if ne else ("parallel", "arbitrary")),
    )(q, k, v, do, *exchange)


PACK_COLS = 1024
ANY = pl.BlockSpec(memory_space=pl.ANY)


def _mesh_pos():
    return lax.axis_index("x"), lax.axis_index("y"), lax.axis_index("c")


def _other_chips(x, y):
    return [(1 - x, y), (x, 1 - y), (1 - x, 1 - y)]


def _dma_sems(n):
    return [pltpu.SemaphoreType.DMA((n,)), pltpu.SemaphoreType.DMA((n,))]


class _Gather:
    def __init__(self, o_refs, send_sems, recv_sems):
        self.o_refs, self.send_sems, self.recv_sems = o_refs, send_sems, recv_sems

    def _copy(self, b, k, chip, hf, to):
        rows = self.o_refs[b].at[chip, hf]
        return pltpu.make_async_remote_copy(src_ref=rows, dst_ref=rows, send_sem=self.send_sems.at[6 * b + k],
                                            recv_sem=self.recv_sems.at[6 * b + k], device_id=to, device_id_type=MESH)

    def start(self):
        x, y, c = _mesh_pos()
        for b in range(len(self.o_refs)):
            for k, (cx, cy) in enumerate(_other_chips(x, y)):
                self._copy(b, k, 2 * x + y, c, (cx, cy, c)).start()

    def finish(self):
        x, y, c = _mesh_pos()
        chips = _other_chips(x, y)
        for b in range(len(self.o_refs)):
            for k, (cx, cy) in enumerate(chips):
                self._copy(b, k, 2 * cx + cy, c, (x, y, c)).wait_recv()
                self._copy(b, 3 + k, 2 * cx + cy, c, (x, y, 1 - c)).start()
        for b in range(len(self.o_refs)):
            for k, (cx, cy) in enumerate(chips):
                self._copy(b, 3 + k, 2 * cx + cy, 1 - c, (x, y, c)).wait_recv()
                self._copy(b, k, 2 * x + y, c, (cx, cy, c)).wait_send()
                self._copy(b, 3 + k, 2 * cx + cy, c, (x, y, 1 - c)).wait_send()


def all_gather_chips(slots, *, name):
    nb = len(slots)

    def body(*refs):
        g = _Gather(refs[nb:2 * nb], *refs[2 * nb:])
        g.start()
        g.finish()

    return pl.pallas_call(
        body, name=name, in_specs=[ANY] * nb, out_specs=[ANY] * nb, input_output_aliases={b: b for b in range(nb)},
        out_shape=[jax.ShapeDtypeStruct(s.shape, s.dtype) for s in slots], scratch_shapes=_dma_sems(6 * nb),
    )(*slots)


class _Swap:
    def __init__(self, g_refs, o_refs, send_sems, recv_sems):
        self.g_refs, self.o_refs, self.send_sems, self.recv_sems = g_refs, o_refs, send_sems, recv_sems

    def _copies(self):
        x, y, c = _mesh_pos()
        return [pltpu.make_async_remote_copy(src_ref=self.g_refs[b].at[j, 1 - c], dst_ref=self.o_refs[b].at[j],
                                             send_sem=self.send_sems.at[N_CHIPS * b + j],
                                             recv_sem=self.recv_sems.at[N_CHIPS * b + j],
                                             device_id=(x, y, 1 - c), device_id_type=MESH)
                for b in range(len(self.g_refs)) for j in range(N_CHIPS)]

    def start(self):
        for cp in self._copies():
            cp.start()

    def finish(self):
        for cp in self._copies():
            cp.wait()


def _swap_shapes(gs):
    return [jax.ShapeDtypeStruct((g.shape[0],) + g.shape[2:], g.dtype) for g in gs]


def sibling_swap(gs, *, name):
    nb = len(gs)

    def body(*refs):
        comm = _Swap(refs[:nb], refs[nb:2 * nb], *refs[2 * nb:])
        comm.start()
        comm.finish()

    return pl.pallas_call(
        body, name=name, in_specs=[ANY] * nb, out_specs=[ANY] * nb, out_shape=_swap_shapes(gs),
        scratch_shapes=_dma_sems(N_CHIPS * nb),
    )(*gs)


class _Exchange:
    def __init__(self, s_refs, o_refs, send_sems, recv_sems):
        self.s_refs, self.o_refs, self.send_sems, self.recv_sems = s_refs, o_refs, send_sems, recv_sems

    def _copies(self):
        x, y, c = _mesh_pos()
        return [pltpu.make_async_remote_copy(src_ref=self.s_refs[b].at[2 * cx + cy], dst_ref=self.o_refs[b].at[k],
                                             send_sem=self.send_sems.at[3 * b + k], recv_sem=self.recv_sems.at[3 * b + k],
                                             device_id=(cx, cy, c), device_id_type=MESH)
                for b in range(len(self.s_refs)) for k, (cx, cy) in enumerate(_other_chips(x, y))]

    def start(self):
        for cp in self._copies():
            cp.start()

    def finish(self):
        for cp in self._copies():
            cp.wait()


def _exchange_shapes(s1s):
    return [jax.ShapeDtypeStruct((3,) + s.shape[1:], s.dtype) for s in s1s]


def chip_exchange(s1s, *, name):
    nb = len(s1s)

    def body(*refs):
        comm = _Exchange(refs[:nb], refs[nb:2 * nb], *refs[2 * nb:])
        comm.start()
        comm.finish()

    return pl.pallas_call(
        body, name=name, in_specs=[ANY] * nb, out_specs=[ANY] * nb, out_shape=_exchange_shapes(s1s),
        scratch_shapes=_dma_sems(3 * nb),
    )(*s1s)


def sibling_merge(halves, *, name):
    nb = len(halves)

    def body(*refs):
        o_refs, (send_sems, recv_sems) = refs[nb:2 * nb], refs[2 * nb:]
        x, y, c = _mesh_pos()
        cps = [pltpu.make_async_remote_copy(src_ref=o_refs[b].at[c], dst_ref=o_refs[b].at[c], send_sem=send_sems.at[b],
                                            recv_sem=recv_sems.at[b], device_id=(x, y, 1 - c), device_id_type=MESH)
               for b in range(nb)]
        for cp in cps:
            cp.start()
        for cp in cps:
            cp.wait()

    return pl.pallas_call(
        body, name=name, in_specs=[ANY] * nb, out_specs=[ANY] * nb, input_output_aliases={b: b for b in range(nb)},
        out_shape=[jax.ShapeDtypeStruct(h.shape, h.dtype) for h in halves], scratch_shapes=_dma_sems(nb),
    )(*halves)


def all_reduce_small(buf, *, name):
    n_dev = 8

    def body(b_ref, o_ref, recv_buf, send_sems, recv_sems):
        x, y, c = _mesh_pos()
        me = 4 * x + 2 * y + c
        pos = lambda t: (t // 4, (t // 2) % 2, t % 2)

        def copy(t, slot):
            return pltpu.make_async_remote_copy(src_ref=b_ref, dst_ref=recv_buf.at[slot], send_sem=send_sems.at[t],
                                                recv_sem=recv_sems.at[slot], device_id=pos(t), device_id_type=MESH)

        for t in range(n_dev):
            @pl.when(t != me)
            def _(t=t):
                copy(t, me).start()

        recv_buf[me] = b_ref[...]
        for t in range(n_dev):
            @pl.when(t != me)
            def _(t=t):
                copy(t, t).wait_recv()
                copy(t, me).wait_send()

        acc = recv_buf[0]
        for t in range(1, n_dev):
            acc = acc + recv_buf[t]
        o_ref[...] = acc

    return pl.pallas_call(
        body, name=name, out_shape=jax.ShapeDtypeStruct(buf.shape, F32),
        in_specs=[pl.BlockSpec(memory_space=pltpu.VMEM)], out_specs=pl.BlockSpec(memory_space=pltpu.VMEM),
        scratch_shapes=[pltpu.VMEM((n_dev,) + buf.shape, F32), pltpu.SemaphoreType.DMA((n_dev,)),
                        pltpu.SemaphoreType.DMA((n_dev,))],
    )(buf)


REDUCE_ROWS = (512, 384, 256, 128)


def add_selected(sel, a5, b, *, name):
    n, _, rh, w = a5.shape
    tr = _pick(rh, REDUCE_ROWS)

    def body(sel_ref, a_ref, b_ref, own_ref, ob_ref):
        s = a_ref[...] + b_ref[...]
        ob_ref[...] = s.astype(BF16)

        @pl.when(pl.program_id(1) == sel_ref[0])
        def _():
            own_ref[...] = s

    blk = pl.BlockSpec((None, tr, w), lambda i, j, s: (j, i, 0))
    return pl.pallas_call(
        body, name=name,
        grid_spec=pltpu.PrefetchScalarGridSpec(
            num_scalar_prefetch=1, grid=(rh // tr, n),
            in_specs=[pl.BlockSpec((None, None, tr, w), lambda i, j, s: (j, s[1], i, 0)), blk],
            out_specs=[pl.BlockSpec((tr, w), lambda i, j, s: (i, 0)), blk]),
        out_shape=[jax.ShapeDtypeStruct((rh, w), F32), jax.ShapeDtypeStruct((n, rh, w), BF16)],
        compiler_params=_params(("arbitrary", "arbitrary")),
    )(sel, a5, b)


def add_chip_sums(sel, s1, b2, *, name):
    rh, w = s1.shape
    tr = _pick(rh, REDUCE_ROWS)

    def body(sel_ref, s_ref, b_ref, o_ref):
        o_ref[...] = ((s_ref[...] + b_ref[0].astype(F32)) + b_ref[1].astype(F32)) + b_ref[2].astype(F32)

    return pl.pallas_call(
        body, name=name,
        grid_spec=pltpu.PrefetchScalarGridSpec(
            num_scalar_prefetch=1, grid=(rh // tr,),
            in_specs=[pl.BlockSpec((tr, w), lambda i, s: (i, 0)), pl.BlockSpec((3, tr, w), lambda i, s: (0, i, 0))],
            out_specs=pl.BlockSpec((None, tr, w), lambda i, s: (s[1], i, 0))),
        out_shape=jax.ShapeDtypeStruct((2, rh, w), F32),
        compiler_params=_params(("arbitrary",)),
    )(sel, s1, b2)


BIG = (("gdn_w_out", 1), ("sb_w_q", 1), ("sb_w_out", 1), ("ffn_w_out", 1), ("ple_w_gate", 1), ("w_kv", 1),
       ("ple_w_proj", 2))
SMALL = ("ln_mix", "ln_ffn", "ln_ple", "gdn_a_log", "gdn_dt_bias", "gdn_norm", "kv_norm", "k_norm", "sb_q_norm")
WEIGHTS = ("ln_mix", "ln_ffn", "ln_ple", "gdn_w_in", "gdn_conv", "gdn_a_log", "gdn_dt_bias", "gdn_norm", "gdn_w_out",
           "kv_norm", "w_kv", "k_norm", "sb_w_q", "sb_q_norm", "sb_w_out", "ffn_w_in", "ffn_w_out", "ple_w_proj",
           "ple_w_gate")
PACK_ALIGN = 256


ROW_TILE = 16


def _rows_of(shape, tile=ROW_TILE):
    return -(-math.prod(shape) // (PACK_COLS * tile)) * tile


WEIGHT_ALIGN = 32


def _pack_rows(arrs, lead, align=PACK_ALIGN):
    parts = []
    for a in arrs:
        if a.shape[-1] == PACK_COLS:
            parts.append(a.reshape(lead + (-1, PACK_COLS)))
            continue
        flat = a.reshape(lead + (-1,))
        pad = _rows_of(a.shape[len(lead):]) * PACK_COLS - flat.shape[-1]
        if pad:
            flat = jnp.pad(flat, [(0, 0)] * len(lead) + [(0, pad)])
        parts.append(flat.reshape(lead + (-1, PACK_COLS)))
    rows = sum(q.shape[len(lead)] for q in parts)
    filler = -rows % align
    if filler:
        parts.append(jnp.zeros(lead + (filler, PACK_COLS), parts[0].dtype))
    return jnp.concatenate(parts, axis=len(lead))


def _own_slot(buf, chip):
    mine = lax.broadcasted_iota(jnp.int32, (N_CHIPS, 1, 1), 0) == chip
    slots = jnp.where(mine, buf[None], jnp.zeros((), buf.dtype))
    return slots.reshape(N_CHIPS, 2, buf.shape[0] // 2, buf.shape[1])


def _unpack_rows(buf, shapes, lead):
    out, r0 = [], 0
    for s in shapes:
        rows = _rows_of(s)
        flat = buf[(slice(None),) * len(lead) + (slice(r0, r0 + rows),)].reshape(lead + (-1,))
        out.append(flat[..., :math.prod(s)].reshape(lead + tuple(s)))
        r0 += rows
    return out


def _unshard(g, axis):
    g = jnp.moveaxis(g, 0, axis)
    s = g.shape
    return g.reshape(s[:axis] + (s[axis] * s[axis + 1],) + s[axis + 2:])


def _shard(full, axis):
    s = full.shape
    return jnp.moveaxis(full.reshape(s[:axis] + (N_CHIPS, s[axis] // N_CHIPS) + s[axis + 1:]), axis, 0)


def _to4(a):
    return a.reshape(HEADS, -1, 1, CHUNK).transpose(1, 0, 2, 3)


def _from4(a):
    return a.transpose(1, 0, 2, 3).reshape(HEADS, -1)


def _row(vec):
    flat = vec.reshape(-1)
    rows = _rows_of(flat.shape, 1)
    return jnp.pad(flat, (0, rows * PACK_COLS - flat.shape[0])).reshape(rows, PACK_COLS)


def kernel(x, p, ln_mix, ln_ffn, ln_ple, gdn_w_in, gdn_conv, gdn_a_log, gdn_dt_bias, gdn_norm, gdn_w_out, kv_norm, w_kv, k_norm, sb_w_q, sb_q_norm, sb_w_out, ffn_w_in, ffn_w_out, ple_w_proj, ple_w_gate, loss_target, m_ln_mix, m_ln_ffn, m_ln_ple, m_gdn_w_in, m_gdn_conv, m_gdn_a_log, m_gdn_dt_bias, m_gdn_norm, m_gdn_w_out, m_kv_norm, m_w_kv, m_k_norm, m_sb_w_q, m_sb_q_norm, m_sb_w_out, m_ffn_w_in, m_ffn_w_out, m_ple_w_proj, m_ple_w_gate, v_ln_mix, v_ln_ffn, v_ln_ple, v_gdn_w_in, v_gdn_conv, v_gdn_a_log, v_gdn_dt_bias, v_gdn_norm, v_gdn_w_out, v_kv_norm, v_w_kv, v_k_norm, v_sb_w_q, v_sb_q_norm, v_sb_w_out, v_ffn_w_in, v_ffn_w_out, v_ple_w_proj, v_ple_w_gate):
    w = dict(ln_mix=ln_mix, ln_ffn=ln_ffn, ln_ple=ln_ple, gdn_w_in=gdn_w_in, gdn_conv=gdn_conv, gdn_a_log=gdn_a_log,
             gdn_dt_bias=gdn_dt_bias, gdn_norm=gdn_norm, gdn_w_out=gdn_w_out, kv_norm=kv_norm, w_kv=w_kv, k_norm=k_norm,
             sb_w_q=sb_w_q, sb_q_norm=sb_q_norm, sb_w_out=sb_w_out, ffn_w_in=ffn_w_in, ffn_w_out=ffn_w_out,
             ple_w_proj=ple_w_proj, ple_w_gate=ple_w_gate)
    mom1 = dict(ln_mix=m_ln_mix, ln_ffn=m_ln_ffn, ln_ple=m_ln_ple, gdn_w_in=m_gdn_w_in, gdn_conv=m_gdn_conv,
                gdn_a_log=m_gdn_a_log, gdn_dt_bias=m_gdn_dt_bias, gdn_norm=m_gdn_norm, gdn_w_out=m_gdn_w_out,
                kv_norm=m_kv_norm, w_kv=m_w_kv, k_norm=m_k_norm, sb_w_q=m_sb_w_q, sb_q_norm=m_sb_q_norm,
                sb_w_out=m_sb_w_out, ffn_w_in=m_ffn_w_in, ffn_w_out=m_ffn_w_out, ple_w_proj=m_ple_w_proj,
                ple_w_gate=m_ple_w_gate)
    mom2 = dict(ln_mix=v_ln_mix, ln_ffn=v_ln_ffn, ln_ple=v_ln_ple, gdn_w_in=v_gdn_w_in, gdn_conv=v_gdn_conv,
                gdn_a_log=v_gdn_a_log, gdn_dt_bias=v_gdn_dt_bias, gdn_norm=v_gdn_norm, gdn_w_out=v_gdn_w_out,
                kv_norm=v_kv_norm, w_kv=v_w_kv, k_norm=v_k_norm, sb_w_q=v_sb_w_q, sb_q_norm=v_sb_q_norm,
                sb_w_out=v_sb_w_out, ffn_w_in=v_ffn_w_in, ffn_w_out=v_ffn_w_out, ple_w_proj=v_ple_w_proj,
                ple_w_gate=v_ple_w_gate)
    depth = ln_mix.shape[0]
    n_a = gdn_w_in.shape[0]
    xi, yi, ci = _mesh_pos()
    chip = 2 * xi + yi
    sel_chip = jnp.stack([chip, ci]).astype(jnp.int32)
    h = x[0]
    tgt = loss_target[0]
    t = h.shape[0]


    def layer_items(i):
        if i < n_a:
            items = [("gdn_w_out", i, 0), ("ffn_w_out", i, 0), ("ple_w_gate", i, 0), ("ple_w_proj", i, 1)]
            return items + ([("w_kv", None, 1)] if i == n_a - 1 else [])
        j = i - n_a
        return [("sb_w_q", j, 0), ("sb_w_out", j, 0), ("ffn_w_out", i, 0), ("ple_w_gate", i, 0), ("ple_w_proj", i, 1)]

    def layer_shards(i):
        return [w[n] if idx is None else w[n][idx] for n, idx, _ in layer_items(i)]

    def layer_slots(i):
        packed = _pack_rows([q.astype(BF16) for q in layer_shards(i)], (), align=WEIGHT_ALIGN)
        own = [packed, ffn_w_in[i].astype(BF16)] + ([gdn_w_in[i].astype(BF16)] if i < n_a else [])
        return [_own_slot(b, chip) for b in own]

    def layer_weights(i, got):
        parts = _unpack_rows(got[0].reshape(N_CHIPS, -1, PACK_COLS), [q.shape for q in layer_shards(i)], (N_CHIPS,))
        out = {n: _unshard(g, ax) for (n, _, ax), g in zip(layer_items(i), parts)}
        out["ffn_w_in"] = got[1].reshape((N_CHIPS, 1) + ffn_w_in.shape[1:])
        if i < n_a:
            out["gdn_w_in"] = _unshard(got[2].reshape((N_CHIPS,) + gdn_w_in.shape[1:]), 1)
        return out

    slots = [layer_slots(i) for i in range(depth)]
    wl = [layer_weights(0, all_gather_chips(slots[0], name="all_gather_weights"))]
    conv_rows = _rows_of(gdn_conv.shape, 1)
    small_rows = sum(_rows_of(w[n].shape, 1) for n in SMALL)
    buf_rows = -(-(small_rows + N_CHIPS * conv_rows) // 8) * 8
    conv_buf = jnp.zeros((buf_rows, PACK_COLS), F32)
    conv_buf = lax.dynamic_update_slice(conv_buf, _row(gdn_conv) * (ci == 0).astype(F32), (chip * conv_rows, 0))
    conv_all = all_reduce_small(conv_buf, name="all_reduce_small")[:N_CHIPS * conv_rows]
    conv_full = _unshard(conv_all.reshape(N_CHIPS, -1)[:, :math.prod(gdn_conv.shape)].reshape((N_CHIPS,) + gdn_conv.shape), 2)

    def resid_rms(r, res, g):
        out = r + res
        return out, _rms(out, g)

    def ple_rms(r, res, emb, g):
        out = res + emb * _sigmoid(r)
        return r, out, (None if g is None else _rms(out, g))

    hn_next = None
    saved = []
    k_sh = v_sh = None
    mid = None
    for i in range(depth):
        s = dict(h0=h)
        wi = wl[i]
        nxt = slots[i + 1] if i + 1 < depth else ()
        s["hn"] = hn = rms_fwd(h, ln_mix[i:i + 1], name="rms_fwd") if hn_next is None else hn_next
        if i < n_a:
            w_in = wi["gdn_w_in"]
            s["w_m"], s["w_abt"] = w_in[:, :4 * WIDTH], w_in[:, 4 * WIDTH:].T
            s["proj"] = proj = matmul(hn, s["w_m"], "nn", out_dtype=BF16, name="mm_gdn_in")
            s["ab"] = ab = matmul(s["w_abt"], hn, "nt", name="mm_gdn_ab")
            s["a_log"], s["dt"] = gdn_a_log[i][:, None], gdn_dt_bias[i][:, None]
            g8, b8 = gates_fwd(ab, s["a_log"], s["dt"], name="gates_fwd")
            s["g4"], s["b4"] = _to4(g8), _to4(b8)
            s["qkv"] = qkv = conv_fwd(proj, conv_full[i], name="conv_fwd")
            s["o"], s["s0"], s["tinv"], *got = gdn_fwd(qkv, s["g4"], s["b4"], gather=nxt, name="gdn_fwd")
            s["y"] = mixed = gatenorm_fwd(s["o"], proj, gdn_norm[i:i + 1], name="gatenorm_fwd")
            w_mix_out = wi["gdn_w_out"]
        else:
            j = i - n_a
            s["qraw"] = qraw = matmul(hn, wi["sb_w_q"], "nn", name="mm_sq")
            s["q"] = q = headnorm_fwd(qraw, 0, sb_q_norm[j:j + 1], HEAD_DIM ** -0.5, name="headnorm_q")
            s["o"], *got = sb_fwd(q, k_sh, v_sh, gather=nxt, name="sb_fwd")
            mixed, w_mix_out = s["o"], wi["sb_w_out"]
        if nxt:
            wl.append(layer_weights(i + 1, got))
        s["h1"], s["hn2"] = h, hn2 = matmul_rows(mixed, w_mix_out, [h], [ln_ffn[i:i + 1]], resid_rms, (F32, BF16),
                                                 name="mm_out_rms")
        s["gu"], s["act"] = ffn_in_swiglu(hn2, wi["ffn_w_in"], name="ffn_in_swiglu")
        s["h2"], s["hn3"] = h, hn3 = matmul_rows(s["act"], wi["ffn_w_out"], [h], [ln_ple[i:i + 1]], resid_rms,
                                                 (F32, BF16), name="mm_ffn_out_rms")
        s["pp"] = pp = matmul(p[i, 0], wi["ple_w_proj"], "nn", name="mm_ple_proj")
        if i + 1 < depth:
            s["gt"], h, hn_next = matmul_rows(hn3, wi["ple_w_gate"], [h, pp], [ln_mix[i + 1:i + 2]], ple_rms,
                                              (F32, F32, BF16), name="mm_ple_gate_rms")
        else:
            s["gt"], h = matmul_rows(hn3, wi["ple_w_gate"], [h, pp], [], lambda r, res, e: ple_rms(r, res, e, None)[:2],
                                     (F32, F32), name="mm_ple_gate")
        saved.append(s)
        if i == n_a - 1:
            mid = dict(h=h)
            mid["hk"] = hk = rms_fwd(h, kv_norm[None, :], name="rms_fwd")
            mid["kv"] = kv = matmul(hk, wi["w_kv"], "nn", name="mm_kv")
            k_sh = headnorm_fwd(kv, 0, k_norm[None, :], 1.0, name="headnorm_k")
            v_sh = kv[:, WIDTH:].astype(BF16)

    dh, sq = loss_head(h, tgt, name="loss_head")
    loss = lax.psum(0.5 * jnp.sum(sq) / h.shape[1], ("x", "y", "c"))

    gw = {n: [None] * w[n].shape[0] for n in WEIGHTS if w[n].ndim >= 2 and n not in ("w_kv",)}
    dks, dvs = [], []
    reduced = [None] * depth

    def grad_buffers(i):
        pieces = [_shard(gw[n] if idx is None else gw[n][idx], ax) for n, idx, ax in layer_items(i)]
        bufs = [_pack_rows(pieces, (N_CHIPS,)), gw["ffn_w_in"][i]] + ([_shard(gw["gdn_w_in"][i], 1)] if i < n_a else [])
        return [g.reshape(N_CHIPS, 2, g.shape[1] // 2, g.shape[2]) for g in bufs]

    def chip_sums(g5s, from_sibling):
        return [add_selected(sel_chip, g5, fs, name="add_selected") for g5, fs in zip(g5s, from_sibling)]

    def reduce_end(i, s1s, from_chips):
        s2s = [add_chip_sums(sel_chip, s1, fc, name="add_chip_sums") for (s1, _), fc in zip(s1s, from_chips)]
        reduced[i] = sibling_merge(s2s, name="sibling_merge")

    waiting = None
    for i in reversed(range(depth)):
        s = saved[i]
        if i == n_a - 1:
            dkraw, gw["k_norm"] = headnorm_bwd(mid["kv"], 0, k_norm[None, :], 1.0, tuple(dks), name="headnorm_k_bwd")
            dkv = jnp.concatenate([dkraw, sum_cast(dvs, BF16, name="sum_dv")], axis=1)
            dhk = matmul(dkv, wl[i]["w_kv"], "nt", name="mm_kv_dx")
            gw["w_kv"] = matmul(mid["hk"], dkv, "tn", name="mm_kv_dw")
            dh, gw["kv_norm"] = rms_bwd(mid["h"], kv_norm[None, :], dhk, dh, name="rms_bwd")
        dpp, dgt = ple_bwd(dh, s["pp"], s["gt"], name="ple_bwd")
        gw["ple_w_proj"][i] = matmul(p[i, 0], dpp, "tn", name="mm_ple_proj_dw")
        gw["ple_w_gate"][i] = matmul(s["hn3"], dgt, "tn", name="mm_sq_dw")
        dhn3 = matmul(dgt, wl[i]["ple_w_gate"], "nt", name="mm_sq_dx")
        dh, dhb, gw["ln_ple"][i] = rms_bwd(s["h2"], ln_ple[i:i + 1], dhn3, dh, matmul_copy=True, name="rms_bwd_copy")
        dgu, *from_sibling = ffn_out_dx_swiglu(dhb, wl[i]["ffn_w_out"], s["gu"], swap=waiting[1] if waiting else (),
                                               name="ffn_out_dx_swiglu")
        sums = chip_sums(waiting[1], from_sibling) if waiting else []
        riding = [s1b for _, s1b in sums]
        gw["ffn_w_out"][i] = matmul(s["act"], dhb, "tn", name="mm_ffn_out_dw")
        dhn2 = matmul(dgu, wl[i]["ffn_w_in"], "nt", b_chips=0, name="mm_ffn_in_dx")
        gw["ffn_w_in"][i] = matmul(s["hn2"], dgu, "tn", out_chips=True, name="mm_ffn_in_dw")
        dh, dhb, gw["ln_ffn"][i] = rms_bwd(s["h1"], ln_ffn[i:i + 1], dhn2, dh, matmul_copy=True, name="rms_bwd_copy")
        if i < n_a:
            dy = matmul(dhb, wl[i]["gdn_w_out"], "nt", name="mm_sq_dx")
            gw["gdn_w_out"][i] = matmul(s["y"], dhb, "tn", name="mm_sq_dw")
            do, dproj, gw["gdn_norm"][i] = gatenorm_bwd(s["o"], s["proj"], gdn_norm[i:i + 1], dy, name="gatenorm_bwd")
            dqkv, dg4, db4, *from_chips = gdn_bwd(s["qkv"], s["g4"], s["b4"], s["s0"], s["tinv"], do, exchange=riding,
                                                  name="gdn_bwd")
            dab, dal, ddt = gates_bwd(s["ab"], s["a_log"], s["dt"], _from4(dg4), _from4(db4), name="gates_bwd")
            gw["gdn_a_log"][i], gw["gdn_dt_bias"][i] = dal[:, 0], ddt[:, 0]
            dproj, gw["gdn_conv"][i] = conv_bwd(s["proj"], conv_full[i], dqkv, dproj, name="conv_bwd")
            dhn = matmul(dproj, s["w_m"], "nt", name="mm_gdn_in_dx")
            dhn = matmul(dab, s["w_abt"], "tn", add=dhn, name="mm_gdn_ab_dx")
            dwm = matmul(s["hn"], dproj, "tn", name="mm_gdn_in_dw")
            dwab = matmul(dab, s["hn"], "nn", name="mm_gdn_ab_dw")
            gw["gdn_w_in"][i] = jnp.concatenate([dwm, dwab.T], axis=1)
        else:
            j = i - n_a
            do = matmul(dhb, wl[i]["sb_w_out"], "nt", out_dtype=BF16, name="mm_sb_out_dx")
            gw["sb_w_out"][j] = matmul(s["o"], dhb, "tn", name="mm_sq_dw")
            dq, dk, dv, *from_chips = sb_bwd(s["q"], k_sh, v_sh, do, exchange=riding, name="sb_bwd")
            dks.append(dk)
            dvs.append(dv)
            dqraw, gw["sb_q_norm"][j] = headnorm_bwd(s["qraw"], 0, sb_q_norm[j:j + 1], HEAD_DIM ** -0.5, (dq,),
                                                    name="headnorm_q_bwd")
            dhn = matmul(dqraw, wl[i]["sb_w_q"], "nt", name="mm_sq_dx")
            gw["sb_w_q"][j] = matmul(s["hn"], dqraw, "tn", name="mm_sq_dw")
        dh, gw["ln_mix"][i] = rms_bwd(s["h0"], ln_mix[i:i + 1], dhn, dh, name="rms_bwd")
        if waiting:
            reduce_end(waiting[0], sums, from_chips)
        waiting = (i, grad_buffers(i))
    sums = chip_sums(waiting[1], sibling_swap(waiting[1], name="sibling_swap"))
    reduce_end(waiting[0], sums, chip_exchange([s1b for _, s1b in sums], name="chip_exchange"))
    grad_x = dh[None]

    def stacked(n):
        g = gw[n]
        if isinstance(g, list):
            g = jnp.stack([a.reshape(w[n].shape[1:]) if n in SMALL else a for a in g])
        return g

    small_buf = jnp.concatenate([_row(stacked(n)) for n in SMALL] + [_row(stacked("gdn_conv"))], axis=0)
    small_buf = jnp.pad(small_buf, ((0, buf_rows - small_buf.shape[0]), (0, 0)))
    small_sum = all_reduce_small(small_buf, name="all_reduce_small")
    grads = {}
    r0 = 0
    for n in SMALL:
        rows = _rows_of(w[n].shape, 1)
        grads[n] = small_sum[r0:r0 + rows].reshape(-1)[:math.prod(w[n].shape)].reshape(w[n].shape)
        r0 += rows
    conv_g = small_sum[r0:r0 + N_CHIPS * conv_rows].reshape(-1)[:N_CHIPS * math.prod(gdn_conv.shape)]
    conv_g = conv_g.reshape((gdn_conv.shape[0], CONV_WIDTH, N_CHIPS, gdn_conv.shape[2]))
    grads["gdn_conv"] = lax.dynamic_index_in_dim(conv_g, chip, axis=2, keepdims=False)

    per_layer = {n: [None] * w[n].shape[0] for n, _ in BIG if n != "w_kv"}
    for i in range(depth):
        parts = _unpack_rows(reduced[i][0].reshape(-1, PACK_COLS), [q.shape for q in layer_shards(i)], ())
        for (n, idx, _), g in zip(layer_items(i), parts):
            if idx is None:
                grads[n] = g
            else:
                per_layer[n][idx] = g
    for n, parts in per_layer.items():
        grads[n] = jnp.stack(parts)
    grads["ffn_w_in"] = jnp.stack([reduced[i][1].reshape(ffn_w_in.shape[1:]) for i in range(depth)])
    grads["gdn_w_in"] = jnp.stack([reduced[i][2].reshape(gdn_w_in.shape[1:]) for i in range(n_a)])

    delta, new_m, new_v = {}, {}, {}
    for n in WEIGHTS:
        delta[n], new_m[n], new_v[n] = adamw(w[n], grads[n], mom1[n], mom2[n], name="adamw")
    return (loss, grad_x, *[grads[n] for n in WEIGHTS], *[delta[n] for n in WEIGHTS],
            *[new_m[n] for n in WEIGHTS], *[new_v[n] for n in WEIGHTS])
```

```python
import math

import jax
import jax.numpy as jnp
from jax import lax
from jax.experimental import pallas as pl
from jax.experimental.pallas import tpu as pltpu

F32 = jnp.float32
BF16 = jnp.bfloat16
EPS = 1e-6
HEADS = 8
HEAD_DIM = 128
WIDTH = HEADS * HEAD_DIM
CHUNK = 64
CONV_WIDTH = 4
N_CHIPS = 4
ADAM_LR, ADAM_B1, ADAM_B2, ADAM_EPS, ADAM_WD, ADAM_STEP = 0.001, 0.9, 0.999, 1e-08, 0.01, 10
V7X_VMEM_BYTES = 64 * 1024 * 1024
VMEM_LIMIT = V7X_VMEM_BYTES - 8 * 1024 * 1024
MESH = pl.DeviceIdType.MESH


def _params(sem=None):
    return pltpu.CompilerParams(dimension_semantics=sem, vmem_limit_bytes=VMEM_LIMIT)


def _pick(n, prefs):
    for t in prefs:
        if t <= n and n % t == 0:
            return t
    return n


def _bdot(a, b, dims):
    return lax.dot_general(a.astype(BF16), b.astype(BF16), (((dims[0],), (dims[1],)), ((), ())),
                           preferred_element_type=F32)


NN, NT, TN = (1, 0), (1, 1), (0, 0)


MM_TILES = (1024, 1408, 512, 256, 128)


def matmul(a, b, form, *, out_dtype=F32, add=None, name, b_chips=None, out_chips=False):
    ns = None
    if b_chips is not None:
        ns = b.shape[3]
        b_shape = (b.shape[2], N_CHIPS * ns)
    else:
        b_shape = b.shape
    if form == "nn":
        (m, k), (k2, n) = a.shape, b_shape
    elif form == "nt":
        (m, k), (n, k2) = a.shape, b_shape
    else:
        (k, m), (k2, n) = a.shape, b_shape
    assert k == k2, (a.shape, b.shape, form)
    if out_chips:
        ns = n // N_CHIPS
    tm = _pick(m, MM_TILES)
    tn = _pick(n, MM_TILES)
    tk = k if k <= 1024 else _pick(k, (2048,) + MM_TILES if form == "tn" else MM_TILES)
    if ns is not None and (form == "nn" or out_chips):
        tn = ns
    if ns is not None and form == "nt":
        tk = ns
    nk = k // tk
    if form == "tn":
        a_spec = pl.BlockSpec((tk, tm), lambda i, j, kk: (kk, i))
    else:
        a_spec = pl.BlockSpec((tm, tk), lambda i, j, kk: (i, kk))
    if b_chips is not None and form == "nn":
        b_spec = pl.BlockSpec((None, None, tk, ns), lambda i, j, kk: (j, b_chips, kk, 0))
    elif b_chips is not None:
        b_spec = pl.BlockSpec((None, None, tn, ns), lambda i, j, kk: (kk, b_chips, j, 0))
    elif form == "nt":
        b_spec = pl.BlockSpec((tn, tk), lambda i, j, kk: (j, kk))
    else:
        b_spec = pl.BlockSpec((tk, tn), lambda i, j, kk: (kk, j))
    if out_chips:
        o_spec = pl.BlockSpec((None, tm, ns), lambda i, j, kk: (j, i, 0))
    else:
        o_spec = pl.BlockSpec((tm, tn), lambda i, j, kk: (i, j))
    dims = {"nn": NN, "nt": NT, "tn": TN}[form]
    has_add = add is not None

    def body(*refs):
        a_ref, b_ref = refs[:2]
        add_ref = refs[2] if has_add else None
        o_ref = refs[2 + has_add]

        def finish(r):
            if has_add:
                r = r + add_ref[...].astype(F32)
            o_ref[...] = r.astype(out_dtype)

        part = _bdot(a_ref[...], b_ref[...], dims)
        if nk == 1:
            finish(part)
            return
        acc_ref = refs[3 + has_add]
        kk = pl.program_id(2)

        @pl.when(kk == 0)
        def _():
            acc_ref[...] = part

        @pl.when(kk > 0)
        def _():
            acc_ref[...] += part

        @pl.when(kk == nk - 1)
        def _():
            finish(acc_ref[...])

    in_specs = [a_spec, b_spec] + ([o_spec] if has_add else [])
    args = (a, b) + ((add,) if has_add else ())
    return pl.pallas_call(
        body, name=name, grid=(m // tm, n // tn, nk), in_specs=in_specs, out_specs=o_spec,
        out_shape=jax.ShapeDtypeStruct((N_CHIPS, m, ns) if out_chips else (m, n), out_dtype),
        scratch_shapes=[pltpu.VMEM((tm, tn), F32)] if nk > 1 else [],
        compiler_params=_params(("parallel", "parallel", "arbitrary")),
    )(*args)


def matmul_rows(a, b, extra, params, epilogue, out_dtypes, *, name):
    (m, k), (k2, n) = a.shape, b.shape
    assert k == k2, (a.shape, b.shape)
    tm = _pick(m, (1024, 512, 256, 128))
    tk = k if k <= 1024 else _pick(k, MM_TILES)
    nk = k // tk
    ne, npar, no = len(extra), len(params), len(out_dtypes)

    def body(*refs):
        a_ref, b_ref = refs[:2]
        e_refs, p_refs = refs[2:2 + ne], refs[2 + ne:2 + ne + npar]
        o_refs = refs[2 + ne + npar:2 + ne + npar + no]

        def finish(r):
            outs = epilogue(r, *[e[...] for e in e_refs], *[q[...] for q in p_refs])
            for o_ref, val in zip(o_refs, outs):
                o_ref[...] = val.astype(o_ref.dtype)

        part = _bdot(a_ref[...], b_ref[...], NN)
        if nk == 1:
            finish(part)
            return
        acc_ref = refs[-1]
        kk = pl.program_id(1)

        @pl.when(kk == 0)
        def _():
            acc_ref[...] = part

        @pl.when(kk > 0)
        def _():
            acc_ref[...] += part

        @pl.when(kk == nk - 1)
        def _():
            finish(acc_ref[...])

    row = pl.BlockSpec((tm, n), lambda i, kk: (i, 0))
    in_specs = ([pl.BlockSpec((tm, tk), lambda i, kk: (i, kk)), pl.BlockSpec((tk, n), lambda i, kk: (kk, 0))]
                + [row] * ne + [pl.BlockSpec((1, n), lambda i, kk: (0, 0))] * npar)
    return pl.pallas_call(
        body, name=name, grid=(m // tm, nk), in_specs=in_specs, out_specs=[row] * no,
        out_shape=[jax.ShapeDtypeStruct((m, n), dt) for dt in out_dtypes],
        scratch_shapes=[pltpu.VMEM((tm, n), F32)] if nk > 1 else [],
        compiler_params=_params(("parallel", "arbitrary")),
    )(a, b, *extra, *params)


def _const(c):
    return lambda j: c


def rowwise(fn, rows, params, outs, accs=(), *, name, tm, ncol=1):
    t = rows[0][0].shape[0]
    tm = min(tm, t)
    assert t % tm == 0
    n_rows, n_par, n_out, n_acc = len(rows), len(params), len(outs), len(accs)

    def body(*refs):
        j, i = pl.program_id(0), pl.program_id(1)
        ins = [r[...] for r in refs[:n_rows + n_par]]
        o_refs = refs[n_rows + n_par:n_rows + n_par + n_out]
        a_refs = refs[n_rows + n_par + n_out:]
        row_outs, acc_outs = fn(*ins)
        for r, val in zip(o_refs, row_outs):
            r[...] = val.astype(r.dtype)
        for r, val, spec in zip(a_refs, acc_outs, accs):
            first = (i == 0) & (j == 0) if spec[4] else (i == 0)

            @pl.when(first)
            def _(r=r, val=val):
                r[...] = val.astype(F32)

            @pl.when(jnp.logical_not(first))
            def _(r=r, val=val):
                r[...] += val.astype(F32)

    in_specs = [pl.BlockSpec((tm, w), lambda j, i, cf=cf: (i, cf(j))) for _, w, cf in rows]
    in_specs += [pl.BlockSpec((p.shape[0], w), lambda j, i, cf=cf: (0, cf(j))) for p, w, cf in params]
    out_specs = [pl.BlockSpec((tm, w), lambda j, i, cf=cf: (i, cf(j))) for _, _, w, cf in outs]
    out_specs += [pl.BlockSpec((r, w), lambda j, i, cf=cf: (0, cf(j))) for r, _, w, cf, _ in accs]
    out_shape = [jax.ShapeDtypeStruct((t, tw), dt) for tw, dt, _, _ in outs]
    out_shape += [jax.ShapeDtypeStruct((r, tw), F32) for r, tw, _, _, _ in accs]
    res = pl.pallas_call(
        body, name=name, grid=(ncol, t // tm), in_specs=in_specs, out_specs=out_specs, out_shape=out_shape,
        compiler_params=_params(("arbitrary", "arbitrary")),
    )(*[r[0] for r in rows], *[p[0] for p in params])
    return res[:n_out], res[n_out:]


def _full(arr):
    return (arr, arr.shape[1], _const(0))


def _rms(x, g):
    x = x.astype(F32)
    return x * lax.rsqrt(jnp.mean(x * x, axis=-1, keepdims=True) + EPS) * g.astype(F32)


def _sigmoid(x):
    return 1.0 / (1.0 + jnp.exp(-x))


def _silu(x):
    return x * _sigmoid(x)


def _softplus(x):
    return jnp.maximum(x, 0.0) + jnp.log(1.0 + jnp.exp(-jnp.abs(x)))


def rms_fwd(h, g, *, name):
    d = h.shape[1]
    (hn,), _ = rowwise(lambda x, gg: ((_rms(x, gg),), ()), [_full(h)], [_full(g)],
                       [(d, BF16, d, _const(0))], name=name, tm=512)
    return hn


def rms_bwd(h, g, dhn, dh_res, *, name):
    d = h.shape[1]

    def fn(x, ct, res, gg):
        _, vjp = jax.vjp(_rms, x.astype(F32), gg.astype(F32))
        dx, dg = vjp(ct.astype(F32))
        return (res.astype(F32) + dx,), (dg,)

    (dh,), (dg,) = rowwise(fn, [_full(h), _full(dhn), _full(dh_res)], [_full(g)],
                           [(d, F32, d, _const(0))], [(1, d, d, _const(0), True)], name=name, tm=256)
    return dh, dg


def _head_rms(x, g, scale):
    x = x.astype(F32)
    return x * lax.rsqrt(jnp.mean(x * x, axis=-1, keepdims=True) + EPS) * (g.astype(F32) * scale)


def headnorm_fwd(x, col0, g, scale, *, name):
    (y,), _ = rowwise(lambda a, gg: ((_head_rms(a, gg, scale),), ()),
                      [(x, HEAD_DIM, lambda j: col0 + j)], [_full(g)],
                      [(WIDTH, BF16, HEAD_DIM, lambda j: j)], name=name, tm=1024, ncol=HEADS)
    return y


def headnorm_bwd(x, col0, g, scale, dys, *, name, out_dtype=BF16):
    def fn(a, *rest):
        cts, gg = rest[:-1], rest[-1]
        ct = sum(c.astype(F32) for c in cts)
        _, vjp = jax.vjp(lambda a_, g_: _head_rms(a_, g_, scale), a.astype(F32), gg.astype(F32))
        dx, dg = vjp(ct)
        return (dx,), (dg,)

    (dx,), (dg,) = rowwise(fn, [(x, HEAD_DIM, lambda j: col0 + j)] + [(dy, HEAD_DIM, lambda j: j) for dy in dys], [_full(g)],
                           [(WIDTH, out_dtype, HEAD_DIM, lambda j: j)],
                           [(1, HEAD_DIM, HEAD_DIM, _const(0), True)], name=name, tm=1024, ncol=HEADS)
    return dx, dg


def sum_cast(parts, dtype, *, name):
    wd = parts[0].shape[1]
    (out,), _ = rowwise(lambda *a: ((sum(b.astype(F32) for b in a),), ()), [_full(a) for a in parts], [],
                        [(wd, dtype, wd, _const(0))], name=name, tm=512)
    return out


def _gatenorm(o, gate, g):
    return _head_rms(o, g, 1.0) * _silu(gate.astype(F32))


def gatenorm_fwd(o, proj, g, *, name):
    (y,), _ = rowwise(lambda a, gt, gg: ((_gatenorm(a, gt, gg),), ()),
                      [(o, HEAD_DIM, lambda j: j), (proj, HEAD_DIM, lambda j: 3 * HEADS + j)], [_full(g)],
                      [(WIDTH, BF16, HEAD_DIM, lambda j: j)], name=name, tm=1024, ncol=HEADS)
    return y


def gatenorm_bwd(o, proj, g, dy, *, name):
    def fn(a, gt, ct, gg):
        _, vjp = jax.vjp(_gatenorm, a.astype(F32), gt.astype(F32), gg.astype(F32))
        da, dgt, dg = vjp(ct.astype(F32))
        return (da, dgt), (dg,)

    (do, dproj), (dg,) = rowwise(
        fn, [(o, HEAD_DIM, lambda j: j), (proj, HEAD_DIM, lambda j: 3 * HEADS + j), (dy, HEAD_DIM, lambda j: j)],
        [_full(g)],
        [(WIDTH, F32, HEAD_DIM, lambda j: j), (4 * WIDTH, BF16, HEAD_DIM, lambda j: 3 * HEADS + j)],
        [(1, HEAD_DIM, HEAD_DIM, _const(0), True)], name=name, tm=1024, ncol=HEADS)
    return do, dproj, dg


def _swiglu(g, u):
    return _silu(g.astype(F32)) * u.astype(F32)


def ffn_in_swiglu(hn, w4, *, name):
    t, dm = hn.shape
    ns = w4.shape[3]
    half = N_CHIPS // 2
    tm = min(512, t)

    def body(a_ref, w_ref, gu_ref, act_ref):
        a = a_ref[...]
        for j in range(half):
            g = _bdot(a, w_ref[j, 0], NN)
            u = _bdot(a, w_ref[half + j, 0], NN)
            gu_ref[:, j * ns:(j + 1) * ns] = g.astype(BF16)
            gu_ref[:, (half + j) * ns:(half + j + 1) * ns] = u.astype(BF16)
            act_ref[:, j * ns:(j + 1) * ns] = (_silu(g) * u).astype(BF16)

    return pl.pallas_call(
        body, name=name, grid=(t // tm,),
        in_specs=[pl.BlockSpec((tm, dm), lambda i: (i, 0)), pl.BlockSpec(w4.shape, lambda i: (0, 0, 0, 0))],
        out_specs=[pl.BlockSpec((tm, N_CHIPS * ns), lambda i: (i, 0)), pl.BlockSpec((tm, half * ns), lambda i: (i, 0))],
        out_shape=[jax.ShapeDtypeStruct((t, N_CHIPS * ns), BF16), jax.ShapeDtypeStruct((t, half * ns), BF16)],
        compiler_params=_params(("parallel",)),
    )(hn, w4)


def ffn_out_dx_swiglu(dh, w_out_t, gu, *, name, swap=()):
    t, dm = dh.shape
    f = w_out_t.shape[1]
    tm = min(256, t)
    n = t // tm
    nw = len(swap)

    def body(*refs):
        dh_ref, w_ref, gu_ref = refs[:3]
        o_ref = refs[3 + nw]
        comm = _Swap(refs[3:3 + nw], refs[4 + nw:4 + 2 * nw], *refs[4 + 2 * nw:]) if nw else None
        if nw:
            @pl.when(pl.program_id(0) == 0)
            def _():
                comm.start()

        dact = _bdot(dh_ref[...], w_ref[...], NN)
        g, u = gu_ref[:, :f].astype(F32), gu_ref[:, f:].astype(F32)
        sg = _sigmoid(g)
        gs = g * sg
        o_ref[:, :f] = (dact * u * (sg + gs * (1.0 - sg))).astype(BF16)
        o_ref[:, f:] = (dact * gs).astype(BF16)

        if nw:
            @pl.when(pl.program_id(0) == n - 1)
            def _():
                comm.finish()

    return pl.pallas_call(
        body, name=name, grid=(n,),
        in_specs=[pl.BlockSpec((tm, dm), lambda i: (i, 0)), pl.BlockSpec((dm, f), lambda i: (0, 0)),
                  pl.BlockSpec((tm, 2 * f), lambda i: (i, 0))] + [ANY] * nw,
        out_specs=[pl.BlockSpec((tm, 2 * f), lambda i: (i, 0))] + [ANY] * nw,
        out_shape=[jax.ShapeDtypeStruct((t, 2 * f), BF16)] + _swap_shapes(swap),
        scratch_shapes=_dma_sems(N_CHIPS * nw) if nw else [],
        compiler_params=_params(("arbitrary",) if nw else ("parallel",)),
    )(dh, w_out_t, gu, *swap)


def ple_bwd(dh, pp, gt, *, name):
    d = dh.shape[1]

    def fn(ct, b, c):
        s = _sigmoid(c)
        return (ct * s, ct * b * s * (1.0 - s)), ()

    (dpp, dgt), _ = rowwise(fn, [_full(dh), _full(pp), _full(gt)], [],
                            [(d, BF16, d, _const(0)), (d, BF16, d, _const(0))], name=name, tm=512)
    return dpp, dgt


def loss_head(y, tgt, *, name):
    d = y.shape[1]

    def fn(a, b):
        e = a - b
        return (e * (1.0 / d),), (jnp.sum(e * e, axis=0, keepdims=True),)

    (dy,), (sq,) = rowwise(fn, [_full(y), _full(tgt)], [], [(d, F32, d, _const(0))],
                           [(1, d, d, _const(0), True)], name=name, tm=512)
    return dy, sq


def adamw(w, g, m, v, *, name):
    shape = w.shape
    cols = shape[-1]
    flat = lambda a: a.reshape(-1, cols)
    bc1 = 1.0 - ADAM_B1 ** ADAM_STEP
    bc2 = 1.0 - ADAM_B2 ** ADAM_STEP

    def fn(w_, g_, m_, v_):
        m_ = ADAM_B1 * m_ + (1.0 - ADAM_B1) * g_
        v_ = ADAM_B2 * v_ + (1.0 - ADAM_B2) * (g_ * g_)
        delta = -ADAM_LR * ((m_ / bc1) / (jnp.sqrt(v_ / bc2) + ADAM_EPS) + ADAM_WD * w_)
        return (delta, m_, v_), ()

    o = (cols, F32, cols, _const(0))
    (d_, m_, v_), _ = rowwise(fn, [_full(flat(w)), _full(flat(g)), _full(flat(m)), _full(flat(v))], [],
                              [o, o, o], name=name, tm=256)
    return d_.reshape(shape), m_.reshape(shape), v_.reshape(shape)


CONV_STRIP = 256


def _shift_down(x, d):
    if d == 0:
        return x
    rows = lax.broadcasted_iota(jnp.int32, x.shape, 0)
    return jnp.where(rows >= d, pltpu.roll(x, d, 0), 0.0)


def _shift_up(x, d):
    if d == 0:
        return x
    t = x.shape[0]
    rows = lax.broadcasted_iota(jnp.int32, x.shape, 0)
    return jnp.where(rows < t - d, pltpu.roll(x, t - d, 0), 0.0)


def _conv(x, w):
    acc = None
    for j in range(CONV_WIDTH):
        term = _shift_down(x, CONV_WIDTH - 1 - j) * w[j:j + 1, :]
        acc = term if acc is None else acc + term
    return acc


def conv_fwd(proj, w, *, name):
    t = proj.shape[0]
    per = WIDTH // CONV_STRIP

    def body(x_ref, w_ref, o_ref):
        o_ref[0] = _silu(_conv(x_ref[...].astype(F32), w_ref[...]))

    return pl.pallas_call(
        body, name=name, grid=(3 * per,),
        in_specs=[pl.BlockSpec((t, CONV_STRIP), lambda j: (0, j)), pl.BlockSpec((CONV_WIDTH, CONV_STRIP), lambda j: (0, j))],
        out_specs=pl.BlockSpec((1, t, CONV_STRIP), lambda j: (j // per, 0, j % per)),
        out_shape=jax.ShapeDtypeStruct((3, t, WIDTH), F32),
        compiler_params=_params(("parallel",)),
    )(proj, w)


def conv_bwd(proj, w, dqkv, dproj, *, name):
    t = proj.shape[0]
    per = WIDTH // CONV_STRIP

    def body(x_ref, w_ref, d_ref, _, dx_ref, dw_ref):
        x, w_ = x_ref[...].astype(F32), w_ref[...]
        c = _conv(x, w_)
        s = _sigmoid(c)
        dc = d_ref[0] * (s + c * s * (1.0 - s))
        dx = None
        for j in range(CONV_WIDTH):
            d = CONV_WIDTH - 1 - j
            term = _shift_up(dc, d) * w_[j:j + 1, :]
            dx = term if dx is None else dx + term
            dw_ref[j:j + 1, :] = jnp.sum(dc * _shift_down(x, d), axis=0, keepdims=True)
        dx_ref[...] = dx.astype(dx_ref.dtype)

    return pl.pallas_call(
        body, name=name, grid=(3 * per,),
        in_specs=[pl.BlockSpec((t, CONV_STRIP), lambda j: (0, j)), pl.BlockSpec((CONV_WIDTH, CONV_STRIP), lambda j: (0, j)),
                  pl.BlockSpec((1, t, CONV_STRIP), lambda j: (j // per, 0, j % per)), pl.BlockSpec(memory_space=pl.ANY)],
        out_specs=[pl.BlockSpec((t, CONV_STRIP), lambda j: (0, j)), pl.BlockSpec((CONV_WIDTH, CONV_STRIP), lambda j: (0, j))],
        out_shape=[jax.ShapeDtypeStruct(dproj.shape, dproj.dtype), jax.ShapeDtypeStruct((CONV_WIDTH, 3 * WIDTH), F32)],
        input_output_aliases={3: 0},
        compiler_params=_params(("parallel",)),
    )(proj, w, dqkv, dproj)


def _gdn_gates(ab, a_log, dt_bias):
    a_in, b_in = ab[:HEADS], ab[HEADS:]
    g = -jnp.exp(a_log) * _softplus(a_in + dt_bias)
    return g, _sigmoid(b_in)


def gates_fwd(ab, a_log, dt_bias, *, name):
    t = ab.shape[1]

    def body(ab_ref, al_ref, dt_ref, g_ref, b_ref):
        g_ref[...], b_ref[...] = _gdn_gates(ab_ref[...], al_ref[...], dt_ref[...])

    s = jax.ShapeDtypeStruct((HEADS, t), F32)
    return pl.pallas_call(body, name=name, out_shape=[s, s], compiler_params=_params())(ab, a_log, dt_bias)


def gates_bwd(ab, a_log, dt_bias, dg, dbeta, *, name):
    t = ab.shape[1]

    def body(ab_ref, al_ref, dt_ref, dg_ref, db_ref, dab_ref, dal_ref, ddt_ref):
        _, vjp = jax.vjp(_gdn_gates, ab_ref[...], al_ref[...], dt_ref[...])
        dab_ref[...], dal_ref[...], ddt_ref[...] = vjp((dg_ref[...], db_ref[...]))

    c = jax.ShapeDtypeStruct((HEADS, 1), F32)
    return pl.pallas_call(body, name=name, out_shape=[jax.ShapeDtypeStruct((2 * HEADS, t), F32), c, c],
                          compiler_params=_params())(ab, a_log, dt_bias, dg, dbeta)


def _split3(x):
    hi = x.astype(BF16)
    r1 = x - hi.astype(F32)
    mid = r1.astype(BF16)
    lo = (r1 - mid.astype(F32)).astype(BF16)
    return hi, mid, lo


def _dot01(x, m01):
    hi, mid, lo = _split3(x)
    m01 = m01.astype(BF16)
    return _bdot(hi, m01, NN) + _bdot(mid, m01, NN) + _bdot(lo, m01, NN)


def _hdot(a, b, dims=NN):
    a_hi, b_hi = a.astype(BF16), b.astype(BF16)
    a_lo, b_lo = (a - a_hi.astype(F32)).astype(BF16), (b - b_hi.astype(F32)).astype(BF16)
    return _bdot(a_hi, b_hi, dims) + (_bdot(a_hi, b_lo, dims) + _bdot(a_lo, b_hi, dims))


def _rowsum(x):
    return jnp.sum(x, axis=1, keepdims=True)


def _colsum(x):
    return jnp.sum(x, axis=0, keepdims=True)


class _Heads:
    def __init__(self, vals):
        self.v = list(vals)

    def _bin(self, other, f):
        if isinstance(other, _Heads):
            return _Heads(f(a, b) for a, b in zip(self.v, other.v))
        return _Heads(f(a, other) for a in self.v)

    def __add__(self, o):
        return self._bin(o, lambda a, b: a + b)

    __radd__ = __add__

    def __sub__(self, o):
        return self._bin(o, lambda a, b: a - b)

    def __rsub__(self, o):
        return self._bin(o, lambda a, b: b - a)

    def __mul__(self, o):
        return self._bin(o, lambda a, b: a * b)

    __rmul__ = __mul__

    def __neg__(self):
        return _Heads(-a for a in self.v)


def _hmap(f, *args):
    n = next(len(a.v) for a in args if isinstance(a, _Heads))
    return _Heads(f(*[a.v[h] if isinstance(a, _Heads) else a for a in args]) for h in range(n))


def _inv_unit_lower(a, eye):
    p = jnp.where(eye, 1.0, 0.0) - a
    ak = a
    for _ in range(int(math.log2(CHUNK)) - 1):
        ak = _hmap(_hdot, ak, ak)
        p = p + _hmap(_hdot, p, ak)
    return p


def _gdn_chunk(qr, kr, v, grow, brow, tinv=None):
    c = CHUNK
    ri = lax.broadcasted_iota(jnp.int32, (c, c), 0)
    ci = lax.broadcasted_iota(jnp.int32, (c, c), 1)
    eye, lower, strict = ri == ci, ri >= ci, ri > ci
    where = lambda m: (lambda a: jnp.where(m, a, 0.0))
    to_col = lambda row: _hmap(lambda r: _rowsum(jnp.where(eye, jnp.broadcast_to(r, (c, c)), 0.0)), row)
    cum_row = _hmap(lambda g: _dot01(jnp.broadcast_to(g, (8, c)), ri <= ci)[0:1], grow)
    gcol, bcol = to_col(cum_row), to_col(brow)
    glast = _hmap(lambda g: _colsum(jnp.where(ri[:, 0:1] == c - 1, g, 0.0)), gcol)
    rq = _hmap(lambda a: lax.rsqrt(_rowsum(a * a) + EPS), qr)
    rk = _hmap(lambda a: lax.rsqrt(_rowsum(a * a) + EPS), kr)
    scale = HEAD_DIM ** -0.5
    qn, kn = qr * (rq * scale), kr * rk
    dec = _hmap(lambda gc, gr: jnp.where(lower, jnp.exp(jnp.minimum(gc - gr, 0.0)), 0.0), gcol, cum_row)
    kk = _hmap(lambda a: _bdot(a, a, NT), kn)
    qk = _hmap(lambda a, b: _bdot(a, b, NT), qn, kn)
    gam_col, e_col, gam_last = _hmap(jnp.exp, gcol), _hmap(jnp.exp, glast - gcol), _hmap(jnp.exp, glast)
    if tinv is None:
        tinv = _inv_unit_lower(_hmap(where(strict), bcol * kk * dec), eye)
    u = _hmap(_hdot, tinv, v * bcol)
    w = _hmap(_hdot, tinv, kn * (bcol * gam_col))
    return dict(eye=eye, lower=lower, strict=strict, gcol=gcol, bcol=bcol, rq=rq, rk=rk, qn=qn, kn=kn,
                dec=dec, kk=kk, qk=qk, gam_col=gam_col, e_col=e_col, gam_last=gam_last, tinv=tinv, u=u, w=w,
                aqk=qk * dec, qt=qn * gam_col, kt=kn * e_col, scale=scale)


def _head_cols(h):
    return slice(h * HEAD_DIM, (h + 1) * HEAD_DIM)


def _bd(dims):
    return lambda a, b: _bdot(a, b, dims)


def gdn_fwd(qkv, g4, b4, *, name, gather=()):
    t = qkv.shape[1]
    n = t // CHUNK
    d = HEAD_DIM
    heads = range(HEADS)
    ng = len(gather)

    def body(*refs):
        qkv_ref, g_ref, b_ref = refs[:3]
        o_ref, s0_ref, t_ref = refs[3 + ng:6 + ng]
        s_ref = refs[6 + 2 * ng]
        comm = _Gather(refs[6 + ng:6 + 2 * ng], *refs[7 + 2 * ng:]) if ng else None

        @pl.when(pl.program_id(0) == 0)
        def _():
            s_ref[...] = jnp.zeros_like(s_ref)
            if ng:
                comm.start()

        qr, kr, v = (_Heads(qkv_ref[j, :, _head_cols(h)] for h in heads) for j in range(3))
        z = _gdn_chunk(qr, kr, v, _Heads(g_ref[0, h] for h in heads), _Heads(b_ref[0, h] for h in heads))
        s0 = _Heads(s_ref[h] for h in heads)
        v_new = z["u"] - _hmap(_bd(NN), z["w"], s0)
        o = _hmap(_bd(NN), z["qt"], s0) + _hmap(_bd(NN), z["aqk"], v_new)
        s_new = s0 * z["gam_last"] + _hmap(_bd(TN), z["kt"], v_new)
        for h in heads:
            s0_ref[0, h] = s0.v[h]
            t_ref[0, h] = z["tinv"].v[h]
            o_ref[:, _head_cols(h)] = o.v[h]
            s_ref[h] = s_new.v[h]

        if ng:
            @pl.when(pl.program_id(0) == n - 1)
            def _():
                comm.finish()

    gspec = pl.BlockSpec((1, HEADS, 1, CHUNK), lambda i: (i, 0, 0, 0))
    return pl.pallas_call(
        body, name=name, grid=(n,),
        in_specs=[pl.BlockSpec((3, CHUNK, WIDTH), lambda i: (0, i, 0)), gspec, gspec] + [ANY] * ng,
        out_specs=[pl.BlockSpec((CHUNK, WIDTH), lambda i: (i, 0)),
                   pl.BlockSpec((1, HEADS, d, d), lambda i: (i, 0, 0, 0)),
                   pl.BlockSpec((1, HEADS, CHUNK, CHUNK), lambda i: (i, 0, 0, 0))] + [ANY] * ng,
        out_shape=[jax.ShapeDtypeStruct((t, WIDTH), F32), jax.ShapeDtypeStruct((n, HEADS, d, d), F32),
                   jax.ShapeDtypeStruct((n, HEADS, CHUNK, CHUNK), F32)]
        + [jax.ShapeDtypeStruct(s.shape, s.dtype) for s in gather],
        input_output_aliases={3 + b: 3 + b for b in range(ng)},
        scratch_shapes=[pltpu.VMEM((HEADS, d, d), F32)] + (_dma_sems(6 * ng) if ng else []),
        compiler_params=_params(("arbitrary",)),
    )(qkv, g4, b4, *gather)


def gdn_bwd(qkv, g4, b4, s0_all, tinv_all, do, *, name, exchange=()):
    t = qkv.shape[1]
    n = t // CHUNK
    d = HEAD_DIM
    c = CHUNK
    heads = range(HEADS)

    ne = len(exchange)

    def body(*refs):
        qkv_ref, g_ref, b_ref, s0_ref, t_ref, do_ref = refs[:6]
        dqkv_ref, dg_ref, db_ref = refs[6 + ne:9 + ne]
        ds_ref = refs[9 + 2 * ne]
        comm = _Exchange(refs[6:6 + ne], refs[9 + ne:9 + 2 * ne], *refs[10 + 2 * ne:]) if ne else None

        @pl.when(pl.program_id(0) == 0)
        def _():
            ds_ref[...] = jnp.zeros_like(ds_ref)
            if ne:
                comm.start()

        qr, kr, v = (_Heads(qkv_ref[j, :, _head_cols(h)] for h in heads) for j in range(3))
        z = _gdn_chunk(qr, kr, v, _Heads(g_ref[0, h] for h in heads), _Heads(b_ref[0, h] for h in heads),
                       tinv=_Heads(t_ref[0, h] for h in heads))
        s0 = _Heads(s0_ref[0, h] for h in heads)
        ds = _Heads(ds_ref[h] for h in heads)
        dout = _Heads(do_ref[:, _head_cols(h)] for h in heads)
        qn, kn, u, w, dec, kk, qk = z["qn"], z["kn"], z["u"], z["w"], z["dec"], z["kk"], z["qk"]
        bcol, gam_col, e_col, gam_last = z["bcol"], z["gam_col"], z["e_col"], z["gam_last"]
        low = lambda a: jnp.where(z["lower"], a, 0.0)
        strict = lambda a: jnp.where(z["strict"], a, 0.0)
        rowsum = lambda a: _hmap(_rowsum, a)
        colsum = lambda a: _hmap(_colsum, a)
        v_new = u - _hmap(_bd(NN), w, s0)
        dv_new = _hmap(_bd(TN), z["aqk"], dout) + _hmap(_bd(NN), z["kt"], ds)
        daqk = _hmap(low, _hmap(_bd(NT), dout, v_new))
        dqt = _hmap(_bd(NT), dout, s0)
        dkt = _hmap(_bd(NT), v_new, ds)
        dgam_last = _hmap(lambda a, b: jnp.sum(a * b, keepdims=True), ds, s0)
        ds_new = _hmap(_bd(TN), z["qt"], dout) + ds * gam_last - _hmap(_bd(TN), w, dv_new)
        dw = -_hmap(_bd(NT), dv_new, s0)
        hd_t = lambda a, b: _hdot(a, b, TN)
        dru = _hmap(hd_t, z["tinv"], dv_new)
        drw = _hmap(hd_t, z["tinv"], dw)
        dal = -_hmap(strict, _hmap(_bd(NT), dru, u) + _hmap(_bd(NT), drw, w))
        t1 = dal * kk * dec
        dkk = dal * bcol * dec
        ddec = dal * bcol * kk + daqk * qk
        dqk = daqk * dec
        s_w = rowsum(drw * kn)
        dbeta_col = rowsum(t1) + rowsum(dru * v) + gam_col * s_w
        dkn = (drw * (bcol * gam_col) + _hmap(_bd(NN), dkk, kn) + _hmap(_bd(TN), dkk, kn) + _hmap(_bd(TN), dqk, qn)
               + dkt * e_col)
        dqn = _hmap(_bd(NN), dqk, kn) + dqt * gam_col
        e_mat = ddec * dec
        de_col = rowsum(dkt * kn)
        diag_of_colsum = rowsum(_hmap(lambda a: jnp.where(z["eye"], jnp.broadcast_to(_colsum(a), (c, c)), 0.0), e_mat))
        dg_cum = rowsum(e_mat) + (bcol * s_w + rowsum(dqt * qn)) * gam_col - de_col * e_col - diag_of_colsum
        dg_last = colsum(de_col * e_col) + dgam_last * gam_last
        dg = colsum(_hmap(lambda a: jnp.where(z["lower"], a, 0.0), dg_cum)) + dg_last
        dbeta = colsum(_hmap(lambda a: jnp.where(z["eye"], a, 0.0), dbeta_col))
        rq, rk = z["rq"], z["rk"]
        dqr = z["scale"] * (rq * dqn - qr * (rq * rq * rq) * rowsum(qr * dqn))
        dkr = rk * dkn - kr * (rk * rk * rk) * rowsum(kr * dkn)
        dv = dru * bcol
        for h in heads:
            ds_ref[h] = ds_new.v[h]
            dg_ref[0, h] = dg.v[h]
            db_ref[0, h] = dbeta.v[h]
            dqkv_ref[0, :, _head_cols(h)] = dqr.v[h]
            dqkv_ref[1, :, _head_cols(h)] = dkr.v[h]
            dqkv_ref[2, :, _head_cols(h)] = dv.v[h]

        if ne:
            @pl.when(pl.program_id(0) == n - 1)
            def _():
                comm.finish()

    rev = lambda i: n - 1 - i
    gspec = pl.BlockSpec((1, HEADS, 1, CHUNK), lambda i: (rev(i), 0, 0, 0))
    return pl.pallas_call(
        body, name=name, grid=(n,),
        in_specs=[pl.BlockSpec((3, CHUNK, WIDTH), lambda i: (0, rev(i), 0)), gspec, gspec,
                  pl.BlockSpec((1, HEADS, d, d), lambda i: (rev(i), 0, 0, 0)),
                  pl.BlockSpec((1, HEADS, CHUNK, CHUNK), lambda i: (rev(i), 0, 0, 0)),
                  pl.BlockSpec((CHUNK, WIDTH), lambda i: (rev(i), 0))] + [ANY] * ne,
        out_specs=[pl.BlockSpec((3, CHUNK, WIDTH), lambda i: (0, rev(i), 0)), gspec, gspec] + [ANY] * ne,
        out_shape=[jax.ShapeDtypeStruct((3, t, WIDTH), F32), jax.ShapeDtypeStruct((n, HEADS, 1, CHUNK), F32),
                   jax.ShapeDtypeStruct((n, HEADS, 1, CHUNK), F32)] + _exchange_shapes(exchange),
        scratch_shapes=[pltpu.VMEM((HEADS, d, d), F32)] + (_dma_sems(3 * ne) if ne else []),
        compiler_params=_params(("arbitrary",)),
    )(qkv, g4, b4, s0_all, tinv_all, do, *exchange)


SB_BLOCK = 256


def _dot01_2(x, m01):
    hi = x.astype(BF16)
    lo = (x - hi.astype(F32)).astype(BF16)
    return _bdot(hi, m01, NN) + _bdot(lo, m01, NN)


SB_HEADS = 2


def _sb_weights(q, kb, carry, mask, upper):
    z = _hmap(_bd(NT), q, kb)
    ls = _hmap(lambda z_: jnp.minimum(z_, 0.0) - jnp.log(1.0 + jnp.exp(-jnp.abs(z_))), z)
    ln = _hmap(lambda l_, z_: jnp.where(mask, l_ - z_, 0.0), ls, z)
    suffix = _hmap(lambda l_: _dot01_2(l_, upper), ln)
    a = _hmap(lambda l_, s_, c_: jnp.where(mask, jnp.exp(l_ + s_ + c_), 0.0), ls, suffix, carry)
    return z, ln, a


SB_DEAD = -105.0


def _sb_alive(s, i, carries):
    top = jnp.max(carries[0])
    for c in carries[1:]:
        top = jnp.maximum(top, jnp.max(c))
    return (s <= i) & (top > SB_DEAD)


def _sb_masks(i, jb, blk):
    ri = lax.broadcasted_iota(jnp.int32, (blk, blk), 0)
    ci = lax.broadcasted_iota(jnp.int32, (blk, blk), 1)
    return (jb * blk + ci) < (i * blk + ri)


def sb_fwd(q, k, v, *, name, gather=()):
    t = q.shape[0]
    blk = min(SB_BLOCK, t)
    d = HEAD_DIM
    hs = range(SB_HEADS)
    ng = len(gather)
    groups, nb = HEADS // SB_HEADS, t // blk

    def body(*refs):
        q_ref, k_ref, v_ref = refs[:3]
        o_ref = refs[3 + ng]
        comm = _Gather(refs[4 + ng:4 + 2 * ng], *refs[4 + 2 * ng:]) if ng else None
        i = pl.program_id(1)
        if ng:
            @pl.when((pl.program_id(0) == 0) & (i == 0))
            def _():
                comm.start()

        qb = _Heads(q_ref[:, _head_cols(h)] for h in hs)
        ri = lax.broadcasted_iota(jnp.int32, (blk, blk), 0)
        ci = lax.broadcasted_iota(jnp.int32, (blk, blk), 1)
        upper = (ri > ci).astype(BF16)

        def step(state):
            s, cs, accs = state
            jb = i - s
            rows = pl.ds(pl.multiple_of(jb * blk, blk), blk)
            kb = _Heads(k_ref[rows, _head_cols(h)] for h in hs)
            vb = _Heads(v_ref[rows, _head_cols(h)] for h in hs)
            _, ln, a = _sb_weights(qb, kb, _Heads(cs), _sb_masks(i, jb, blk), upper)
            cs = _Heads(cs) + _hmap(_rowsum, ln)
            accs = _Heads(accs) + _hmap(_bd(NN), a, vb)
            return s + 1, tuple(cs.v), tuple(accs.v)

        init = (jnp.int32(0), tuple(jnp.zeros((blk, 1), F32) for _ in hs), tuple(jnp.zeros((blk, d), F32) for _ in hs))
        _, _, accs = lax.while_loop(lambda st: _sb_alive(st[0], i, st[1]), step, init)
        for h in hs:
            o_ref[:, _head_cols(h)] = accs[h].astype(o_ref.dtype)

        if ng:
            @pl.when((pl.program_id(0) == groups - 1) & (i == nb - 1))
            def _():
                comm.finish()

    qspec = pl.BlockSpec((blk, SB_HEADS * d), lambda g, i: (i, g))
    kspec = pl.BlockSpec((t, SB_HEADS * d), lambda g, i: (0, g))
    return pl.pallas_call(
        body, name=name, grid=(groups, nb), in_specs=[qspec, kspec, kspec] + [ANY] * ng,
        out_specs=[qspec] + [ANY] * ng,
        out_shape=[jax.ShapeDtypeStruct((t, WIDTH), BF16)] + [jax.ShapeDtypeStruct(s.shape, s.dtype) for s in gather],
        input_output_aliases={3 + b: 1 + b for b in range(ng)},
        scratch_shapes=_dma_sems(6 * ng) if ng else [],
        compiler_params=_params(("arbitrary", "arbitrary") if ng else ("parallel", "arbitrary")),
    )(q, k, v, *gather)


def sb_bwd(q, k, v, do, *, name, exchange=()):
    t = q.shape[0]
    blk = min(SB_BLOCK, t)
    d = HEAD_DIM
    nb = t // blk
    hs = range(SB_HEADS)
    ne = len(exchange)
    groups = HEADS // SB_HEADS

    def body(*refs):
        q_ref, k_ref, v_ref, do_ref = refs[:4]
        dq_ref, dk_ref, dv_ref = refs[4 + ne:7 + ne]
        p_buf, z_buf = refs[7 + 2 * ne:9 + 2 * ne]
        comm = _Exchange(refs[4:4 + ne], refs[7 + ne:7 + 2 * ne], *refs[9 + 2 * ne:]) if ne else None
        i = pl.program_id(1)
        if ne:
            @pl.when((pl.program_id(0) == 0) & (i == 0))
            def _():
                comm.start()

        @pl.when(i == 0)
        def _():
            dk_ref[...] = jnp.zeros_like(dk_ref)
            dv_ref[...] = jnp.zeros_like(dv_ref)

        qb = _Heads(q_ref[:, _head_cols(h)] for h in hs)
        dob = _Heads(do_ref[:, _head_cols(h)] for h in hs)
        ri = lax.broadcasted_iota(jnp.int32, (blk, blk), 0)
        ci = lax.broadcasted_iota(jnp.int32, (blk, blk), 1)
        upper = (ri > ci).astype(BF16)
        lower = (ri < ci).astype(BF16)

        def right_to_left(state):
            s, cs = state
            jb = i - s
            rows = pl.ds(pl.multiple_of(jb * blk, blk), blk)
            kb = _Heads(k_ref[rows, _head_cols(h)] for h in hs)
            vb = _Heads(v_ref[rows, _head_cols(h)] for h in hs)
            z, ln, a = _sb_weights(qb, kb, _Heads(cs), _sb_masks(i, jb, blk), upper)
            p = a * _hmap(_bd(NT), dob, vb)
            dv = _hmap(_bd(TN), a, dob)
            for h in hs:
                p_buf[h, jb] = p.v[h]
                z_buf[h, jb] = z.v[h]
                dv_ref[rows, _head_cols(h)] += dv.v[h]
            return s + 1, tuple((_Heads(cs) + _hmap(_rowsum, ln)).v)

        n_done, _ = lax.while_loop(lambda st: _sb_alive(st[0], i, st[1]), right_to_left,
                                   (jnp.int32(0), tuple(jnp.zeros((blk, 1), F32) for _ in hs)))

        def left_to_right(jb, carry):
            cps, dqs = carry
            rows = pl.ds(pl.multiple_of(jb * blk, blk), blk)
            mask = _sb_masks(i, jb, blk)
            kb = _Heads(k_ref[rows, _head_cols(h)] for h in hs)
            p = _Heads(p_buf[h, jb] for h in hs)
            sg = _hmap(_sigmoid, _Heads(z_buf[h, jb] for h in hs))
            prefix = _hmap(lambda a: _dot01_2(a, lower), p) + _Heads(cps)
            dz = _hmap(lambda a: jnp.where(mask, a, 0.0), p * (1.0 - sg) - sg * prefix)
            dk = _hmap(_bd(TN), dz, qb)
            for h in hs:
                dk_ref[rows, _head_cols(h)] += dk.v[h]
            return tuple((_Heads(cps) + _hmap(_rowsum, p)).v), tuple((_Heads(dqs) + _hmap(_bd(NN), dz, kb)).v)

        _, dqs = lax.fori_loop(i + 1 - n_done, i + 1, left_to_right,
                               (tuple(jnp.zeros((blk, 1), F32) for _ in hs), tuple(jnp.zeros((blk, d), F32) for _ in hs)))
        for h in hs:
            dq_ref[:, _head_cols(h)] = dqs[h]

        if ne:
            @pl.when((pl.program_id(0) == groups - 1) & (i == nb - 1))
            def _():
                comm.finish()

    qspec = pl.BlockSpec((blk, SB_HEADS * d), lambda g, i: (i, g))
    kspec = pl.BlockSpec((t, SB_HEADS * d), lambda g, i: (0, g))
    s = jax.ShapeDtypeStruct((t, WIDTH), F32)
    buf = pltpu.VMEM((SB_HEADS, nb, blk, blk), F32)
    return pl.pallas_call(
        body, name=name, grid=(groups, nb), in_specs=[qspec, kspec, kspec, qspec] + [ANY] * ne,
        out_specs=[qspec, kspec, kspec] + [ANY] * ne, out_shape=[s, s, s] + _exchange_shapes(exchange),
        scratch_shapes=[buf, buf] + (_dma_sems(3 * ne) if ne else []),
        compiler_params=_params(("arbitrary", "arbitrary") if ne else ("parallel", "arbitrary")),
    )(q, k, v, do, *exchange)


PACK_COLS = 1024
ANY = pl.BlockSpec(memory_space=pl.ANY)


def _mesh_pos():
    return lax.axis_index("x"), lax.axis_index("y"), lax.axis_index("c")


def _other_chips(x, y):
    return [(1 - x, y), (x, 1 - y), (1 - x, 1 - y)]


def _dma_sems(n):
    return [pltpu.SemaphoreType.DMA((n,)), pltpu.SemaphoreType.DMA((n,))]


class _Gather:
    def __init__(self, o_refs, send_sems, recv_sems):
        self.o_refs, self.send_sems, self.recv_sems = o_refs, send_sems, recv_sems

    def _copy(self, b, k, chip, hf, to):
        rows = self.o_refs[b].at[chip, hf]
        return pltpu.make_async_remote_copy(src_ref=rows, dst_ref=rows, send_sem=self.send_sems.at[6 * b + k],
                                            recv_sem=self.recv_sems.at[6 * b + k], device_id=to, device_id_type=MESH)

    def start(self):
        x, y, c = _mesh_pos()
        for b in range(len(self.o_refs)):
            for k, (cx, cy) in enumerate(_other_chips(x, y)):
                self._copy(b, k, 2 * x + y, c, (cx, cy, c)).start()

    def finish(self):
        x, y, c = _mesh_pos()
        chips = _other_chips(x, y)
        for b in range(len(self.o_refs)):
            for k, (cx, cy) in enumerate(chips):
                self._copy(b, k, 2 * cx + cy, c, (x, y, c)).wait_recv()
                self._copy(b, 3 + k, 2 * cx + cy, c, (x, y, 1 - c)).start()
        for b in range(len(self.o_refs)):
            for k, (cx, cy) in enumerate(chips):
                self._copy(b, 3 + k, 2 * cx + cy, 1 - c, (x, y, c)).wait_recv()
                self._copy(b, k, 2 * x + y, c, (cx, cy, c)).wait_send()
                self._copy(b, 3 + k, 2 * cx + cy, c, (x, y, 1 - c)).wait_send()


def all_gather_chips(slots, *, name):
    nb = len(slots)

    def body(*refs):
        g = _Gather(refs[nb:2 * nb], *refs[2 * nb:])
        g.start()
        g.finish()

    return pl.pallas_call(
        body, name=name, in_specs=[ANY] * nb, out_specs=[ANY] * nb, input_output_aliases={b: b for b in range(nb)},
        out_shape=[jax.ShapeDtypeStruct(s.shape, s.dtype) for s in slots], scratch_shapes=_dma_sems(6 * nb),
    )(*slots)


class _Swap:
    def __init__(self, g_refs, o_refs, send_sems, recv_sems):
        self.g_refs, self.o_refs, self.send_sems, self.recv_sems = g_refs, o_refs, send_sems, recv_sems

    def _copies(self):
        x, y, c = _mesh_pos()
        return [pltpu.make_async_remote_copy(src_ref=self.g_refs[b].at[j, 1 - c], dst_ref=self.o_refs[b].at[j],
                                             send_sem=self.send_sems.at[N_CHIPS * b + j],
                                             recv_sem=self.recv_sems.at[N_CHIPS * b + j],
                                             device_id=(x, y, 1 - c), device_id_type=MESH)
                for b in range(len(self.g_refs)) for j in range(N_CHIPS)]

    def start(self):
        for cp in self._copies():
            cp.start()

    def finish(self):
        for cp in self._copies():
            cp.wait()


def _swap_shapes(gs):
    return [jax.ShapeDtypeStruct((g.shape[0],) + g.shape[2:], g.dtype) for g in gs]


def sibling_swap(gs, *, name):
    nb = len(gs)

    def body(*refs):
        comm = _Swap(refs[:nb], refs[nb:2 * nb], *refs[2 * nb:])
        comm.start()
        comm.finish()

    return pl.pallas_call(
        body, name=name, in_specs=[ANY] * nb, out_specs=[ANY] * nb, out_shape=_swap_shapes(gs),
        scratch_shapes=_dma_sems(N_CHIPS * nb),
    )(*gs)


class _Exchange:
    def __init__(self, s_refs, o_refs, send_sems, recv_sems):
        self.s_refs, self.o_refs, self.send_sems, self.recv_sems = s_refs, o_refs, send_sems, recv_sems

    def _copies(self):
        x, y, c = _mesh_pos()
        return [pltpu.make_async_remote_copy(src_ref=self.s_refs[b].at[2 * cx + cy], dst_ref=self.o_refs[b].at[k],
                                             send_sem=self.send_sems.at[3 * b + k], recv_sem=self.recv_sems.at[3 * b + k],
                                             device_id=(cx, cy, c), device_id_type=MESH)
                for b in range(len(self.s_refs)) for k, (cx, cy) in enumerate(_other_chips(x, y))]

    def start(self):
        for cp in self._copies():
            cp.start()

    def finish(self):
        for cp in self._copies():
            cp.wait()


def _exchange_shapes(s1s):
    return [jax.ShapeDtypeStruct((3,) + s.shape[1:], s.dtype) for s in s1s]


def chip_exchange(s1s, *, name):
    nb = len(s1s)

    def body(*refs):
        comm = _Exchange(refs[:nb], refs[nb:2 * nb], *refs[2 * nb:])
        comm.start()
        comm.finish()

    return pl.pallas_call(
        body, name=name, in_specs=[ANY] * nb, out_specs=[ANY] * nb, out_shape=_exchange_shapes(s1s),
        scratch_shapes=_dma_sems(3 * nb),
    )(*s1s)


def sibling_merge(halves, *, name):
    nb = len(halves)

    def body(*refs):
        o_refs, (send_sems, recv_sems) = refs[nb:2 * nb], refs[2 * nb:]
        x, y, c = _mesh_pos()
        cps = [pltpu.make_async_remote_copy(src_ref=o_refs[b].at[c], dst_ref=o_refs[b].at[c], send_sem=send_sems.at[b],
                                            recv_sem=recv_sems.at[b], device_id=(x, y, 1 - c), device_id_type=MESH)
               for b in range(nb)]
        for cp in cps:
            cp.start()
        for cp in cps:
            cp.wait()

    return pl.pallas_call(
        body, name=name, in_specs=[ANY] * nb, out_specs=[ANY] * nb, input_output_aliases={b: b for b in range(nb)},
        out_shape=[jax.ShapeDtypeStruct(h.shape, h.dtype) for h in halves], scratch_shapes=_dma_sems(nb),
    )(*halves)


def all_reduce_small(buf, *, name):
    n_dev = 8

    def body(b_ref, o_ref, recv_buf, send_sems, recv_sems):
        x, y, c = _mesh_pos()
        me = 4 * x + 2 * y + c
        pos = lambda t: (t // 4, (t // 2) % 2, t % 2)

        def copy(t, slot):
            return pltpu.make_async_remote_copy(src_ref=b_ref, dst_ref=recv_buf.at[slot], send_sem=send_sems.at[t],
                                                recv_sem=recv_sems.at[slot], device_id=pos(t), device_id_type=MESH)

        for t in range(n_dev):
            @pl.when(t != me)
            def _(t=t):
                copy(t, me).start()

        recv_buf[me] = b_ref[...]
        for t in range(n_dev):
            @pl.when(t != me)
            def _(t=t):
                copy(t, t).wait_recv()
                copy(t, me).wait_send()

        acc = recv_buf[0]
        for t in range(1, n_dev):
            acc = acc + recv_buf[t]
        o_ref[...] = acc

    return pl.pallas_call(
        body, name=name, out_shape=jax.ShapeDtypeStruct(buf.shape, F32),
        in_specs=[pl.BlockSpec(memory_space=pltpu.VMEM)], out_specs=pl.BlockSpec(memory_space=pltpu.VMEM),
        scratch_shapes=[pltpu.VMEM((n_dev,) + buf.shape, F32), pltpu.SemaphoreType.DMA((n_dev,)),
                        pltpu.SemaphoreType.DMA((n_dev,))],
    )(buf)


REDUCE_ROWS = (512, 384, 256, 128)


def add_selected(sel, a5, b, *, name):
    n, _, rh, w = a5.shape
    tr = _pick(rh, REDUCE_ROWS)

    def body(sel_ref, a_ref, b_ref, own_ref, ob_ref):
        s = a_ref[...] + b_ref[...]
        ob_ref[...] = s.astype(BF16)

        @pl.when(pl.program_id(1) == sel_ref[0])
        def _():
            own_ref[...] = s

    blk = pl.BlockSpec((None, tr, w), lambda i, j, s: (j, i, 0))
    return pl.pallas_call(
        body, name=name,
        grid_spec=pltpu.PrefetchScalarGridSpec(
            num_scalar_prefetch=1, grid=(rh // tr, n),
            in_specs=[pl.BlockSpec((None, None, tr, w), lambda i, j, s: (j, s[1], i, 0)), blk],
            out_specs=[pl.BlockSpec((tr, w), lambda i, j, s: (i, 0)), blk]),
        out_shape=[jax.ShapeDtypeStruct((rh, w), F32), jax.ShapeDtypeStruct((n, rh, w), BF16)],
        compiler_params=_params(("arbitrary", "arbitrary")),
    )(sel, a5, b)


def add_chip_sums(sel, s1, b2, *, name):
    rh, w = s1.shape
    tr = _pick(rh, REDUCE_ROWS)

    def body(sel_ref, s_ref, b_ref, o_ref):
        o_ref[...] = ((s_ref[...] + b_ref[0].astype(F32)) + b_ref[1].astype(F32)) + b_ref[2].astype(F32)

    return pl.pallas_call(
        body, name=name,
        grid_spec=pltpu.PrefetchScalarGridSpec(
            num_scalar_prefetch=1, grid=(rh // tr,),
            in_specs=[pl.BlockSpec((tr, w), lambda i, s: (i, 0)), pl.BlockSpec((3, tr, w), lambda i, s: (0, i, 0))],
            out_specs=pl.BlockSpec((None, tr, w), lambda i, s: (s[1], i, 0))),
        out_shape=jax.ShapeDtypeStruct((2, rh, w), F32),
        compiler_params=_params(("arbitrary",)),
    )(sel, s1, b2)


BIG = (("gdn_w_out", 1), ("sb_w_q", 1), ("sb_w_out", 1), ("ffn_w_out", 1), ("ple_w_gate", 1), ("w_kv", 1),
       ("ple_w_proj", 2))
SMALL = ("ln_mix", "ln_ffn", "ln_ple", "gdn_a_log", "gdn_dt_bias", "gdn_norm", "kv_norm", "k_norm", "sb_q_norm")
WEIGHTS = ("ln_mix", "ln_ffn", "ln_ple", "gdn_w_in", "gdn_conv", "gdn_a_log", "gdn_dt_bias", "gdn_norm", "gdn_w_out",
           "kv_norm", "w_kv", "k_norm", "sb_w_q", "sb_q_norm", "sb_w_out", "ffn_w_in", "ffn_w_out", "ple_w_proj",
           "ple_w_gate")
PACK_ALIGN = 256


ROW_TILE = 16


def _rows_of(shape, tile=ROW_TILE):
    return -(-math.prod(shape) // (PACK_COLS * tile)) * tile


WEIGHT_ALIGN = 32


def _pack_rows(arrs, lead, align=PACK_ALIGN):
    parts = []
    for a in arrs:
        if a.shape[-1] == PACK_COLS:
            parts.append(a.reshape(lead + (-1, PACK_COLS)))
            continue
        flat = a.reshape(lead + (-1,))
        pad = _rows_of(a.shape[len(lead):]) * PACK_COLS - flat.shape[-1]
        if pad:
            flat = jnp.pad(flat, [(0, 0)] * len(lead) + [(0, pad)])
        parts.append(flat.reshape(lead + (-1, PACK_COLS)))
    rows = sum(q.shape[len(lead)] for q in parts)
    filler = -rows % align
    if filler:
        parts.append(jnp.zeros(lead + (filler, PACK_COLS), parts[0].dtype))
    return jnp.concatenate(parts, axis=len(lead))


def _own_slot(buf, chip):
    mine = lax.broadcasted_iota(jnp.int32, (N_CHIPS, 1, 1), 0) == chip
    slots = jnp.where(mine, buf[None], jnp.zeros((), buf.dtype))
    return slots.reshape(N_CHIPS, 2, buf.shape[0] // 2, buf.shape[1])


def _unpack_rows(buf, shapes, lead):
    out, r0 = [], 0
    for s in shapes:
        rows = _rows_of(s)
        flat = buf[(slice(None),) * len(lead) + (slice(r0, r0 + rows),)].reshape(lead + (-1,))
        out.append(flat[..., :math.prod(s)].reshape(lead + tuple(s)))
        r0 += rows
    return out


def _unshard(g, axis):
    g = jnp.moveaxis(g, 0, axis)
    s = g.shape
    return g.reshape(s[:axis] + (s[axis] * s[axis + 1],) + s[axis + 2:])


def _shard(full, axis):
    s = full.shape
    return jnp.moveaxis(full.reshape(s[:axis] + (N_CHIPS, s[axis] // N_CHIPS) + s[axis + 1:]), axis, 0)


def _to4(a):
    return a.reshape(HEADS, -1, 1, CHUNK).transpose(1, 0, 2, 3)


def _from4(a):
    return a.transpose(1, 0, 2, 3).reshape(HEADS, -1)


def _row(vec):
    flat = vec.reshape(-1)
    rows = _rows_of(flat.shape, 1)
    return jnp.pad(flat, (0, rows * PACK_COLS - flat.shape[0])).reshape(rows, PACK_COLS)


def kernel(x, p, ln_mix, ln_ffn, ln_ple, gdn_w_in, gdn_conv, gdn_a_log, gdn_dt_bias, gdn_norm, gdn_w_out, kv_norm, w_kv, k_norm, sb_w_q, sb_q_norm, sb_w_out, ffn_w_in, ffn_w_out, ple_w_proj, ple_w_gate, loss_target, m_ln_mix, m_ln_ffn, m_ln_ple, m_gdn_w_in, m_gdn_conv, m_gdn_a_log, m_gdn_dt_bias, m_gdn_norm, m_gdn_w_out, m_kv_norm, m_w_kv, m_k_norm, m_sb_w_q, m_sb_q_norm, m_sb_w_out, m_ffn_w_in, m_ffn_w_out, m_ple_w_proj, m_ple_w_gate, v_ln_mix, v_ln_ffn, v_ln_ple, v_gdn_w_in, v_gdn_conv, v_gdn_a_log, v_gdn_dt_bias, v_gdn_norm, v_gdn_w_out, v_kv_norm, v_w_kv, v_k_norm, v_sb_w_q, v_sb_q_norm, v_sb_w_out, v_ffn_w_in, v_ffn_w_out, v_ple_w_proj, v_ple_w_gate):
    w = dict(ln_mix=ln_mix, ln_ffn=ln_ffn, ln_ple=ln_ple, gdn_w_in=gdn_w_in, gdn_conv=gdn_conv, gdn_a_log=gdn_a_log,
             gdn_dt_bias=gdn_dt_bias, gdn_norm=gdn_norm, gdn_w_out=gdn_w_out, kv_norm=kv_norm, w_kv=w_kv, k_norm=k_norm,
             sb_w_q=sb_w_q, sb_q_norm=sb_q_norm, sb_w_out=sb_w_out, ffn_w_in=ffn_w_in, ffn_w_out=ffn_w_out,
             ple_w_proj=ple_w_proj, ple_w_gate=ple_w_gate)
    mom1 = dict(ln_mix=m_ln_mix, ln_ffn=m_ln_ffn, ln_ple=m_ln_ple, gdn_w_in=m_gdn_w_in, gdn_conv=m_gdn_conv,
                gdn_a_log=m_gdn_a_log, gdn_dt_bias=m_gdn_dt_bias, gdn_norm=m_gdn_norm, gdn_w_out=m_gdn_w_out,
                kv_norm=m_kv_norm, w_kv=m_w_kv, k_norm=m_k_norm, sb_w_q=m_sb_w_q, sb_q_norm=m_sb_q_norm,
                sb_w_out=m_sb_w_out, ffn_w_in=m_ffn_w_in, ffn_w_out=m_ffn_w_out, ple_w_proj=m_ple_w_proj,
                ple_w_gate=m_ple_w_gate)
    mom2 = dict(ln_mix=v_ln_mix, ln_ffn=v_ln_ffn, ln_ple=v_ln_ple, gdn_w_in=v_gdn_w_in, gdn_conv=v_gdn_conv,
                gdn_a_log=v_gdn_a_log, gdn_dt_bias=v_gdn_dt_bias, gdn_norm=v_gdn_norm, gdn_w_out=v_gdn_w_out,
                kv_norm=v_kv_norm, w_kv=v_w_kv, k_norm=v_k_norm, sb_w_q=v_sb_w_q, sb_q_norm=v_sb_q_norm,
                sb_w_out=v_sb_w_out, ffn_w_in=v_ffn_w_in, ffn_w_out=v_ffn_w_out, ple_w_proj=v_ple_w_proj,
                ple_w_gate=v_ple_w_gate)
    depth = ln_mix.shape[0]
    n_a = gdn_w_in.shape[0]
    xi, yi, ci = _mesh_pos()
    chip = 2 * xi + yi
    sel_chip = jnp.stack([chip, ci]).astype(jnp.int32)
    h = x[0]
    tgt = loss_target[0]
    t = h.shape[0]


    def layer_items(i):
        if i < n_a:
            items = [("gdn_w_out", i, 0), ("ffn_w_out", i, 0), ("ple_w_gate", i, 0), ("ple_w_proj", i, 1)]
            return items + ([("w_kv", None, 1)] if i == n_a - 1 else [])
        j = i - n_a
        return [("sb_w_q", j, 0), ("sb_w_out", j, 0), ("ffn_w_out", i, 0), ("ple_w_gate", i, 0), ("ple_w_proj", i, 1)]

    def layer_shards(i):
        return [w[n] if idx is None else w[n][idx] for n, idx, _ in layer_items(i)]

    def layer_slots(i):
        packed = _pack_rows([q.astype(BF16) for q in layer_shards(i)], (), align=WEIGHT_ALIGN)
        own = [packed, ffn_w_in[i].astype(BF16)] + ([gdn_w_in[i].astype(BF16)] if i < n_a else [])
        return [_own_slot(b, chip) for b in own]

    def layer_weights(i, got):
        parts = _unpack_rows(got[0].reshape(N_CHIPS, -1, PACK_COLS), [q.shape for q in layer_shards(i)], (N_CHIPS,))
        out = {n: _unshard(g, ax) for (n, _, ax), g in zip(layer_items(i), parts)}
        out["ffn_w_in"] = got[1].reshape((N_CHIPS, 1) + ffn_w_in.shape[1:])
        if i < n_a:
            out["gdn_w_in"] = _unshard(got[2].reshape((N_CHIPS,) + gdn_w_in.shape[1:]), 1)
        return out

    slots = [layer_slots(i) for i in range(depth)]
    wl = [layer_weights(0, all_gather_chips(slots[0], name="all_gather_weights"))]
    conv_rows = _rows_of(gdn_conv.shape, 1)
    small_rows = sum(_rows_of(w[n].shape, 1) for n in SMALL)
    buf_rows = -(-(small_rows + N_CHIPS * conv_rows) // 8) * 8
    conv_buf = jnp.zeros((buf_rows, PACK_COLS), F32)
    conv_buf = lax.dynamic_update_slice(conv_buf, _row(gdn_conv) * (ci == 0).astype(F32), (chip * conv_rows, 0))
    conv_all = all_reduce_small(conv_buf, name="all_reduce_small")[:N_CHIPS * conv_rows]
    conv_full = _unshard(conv_all.reshape(N_CHIPS, -1)[:, :math.prod(gdn_conv.shape)].reshape((N_CHIPS,) + gdn_conv.shape), 2)

    def resid_rms(r, res, g):
        out = r + res
        return out, _rms(out, g)

    def ple_rms(r, res, emb, g):
        out = res + emb * _sigmoid(r)
        return r, out, (None if g is None else _rms(out, g))

    hn_next = None
    saved = []
    k_sh = v_sh = None
    mid = None
    for i in range(depth):
        s = dict(h0=h)
        wi = wl[i]
        nxt = slots[i + 1] if i + 1 < depth else ()
        s["hn"] = hn = rms_fwd(h, ln_mix[i:i + 1], name="rms_fwd") if hn_next is None else hn_next
        if i < n_a:
            w_in = wi["gdn_w_in"]
            s["w_m"], s["w_abt"] = w_in[:, :4 * WIDTH], w_in[:, 4 * WIDTH:].T
            s["proj"] = proj = matmul(hn, s["w_m"], "nn", out_dtype=BF16, name="mm_gdn_in")
            s["ab"] = ab = matmul(s["w_abt"], hn, "nt", name="mm_gdn_ab")
            s["a_log"], s["dt"] = gdn_a_log[i][:, None], gdn_dt_bias[i][:, None]
            g8, b8 = gates_fwd(ab, s["a_log"], s["dt"], name="gates_fwd")
            s["g4"], s["b4"] = _to4(g8), _to4(b8)
            s["qkv"] = qkv = conv_fwd(proj, conv_full[i], name="conv_fwd")
            s["o"], s["s0"], s["tinv"], *got = gdn_fwd(qkv, s["g4"], s["b4"], gather=nxt, name="gdn_fwd")
            s["y"] = mixed = gatenorm_fwd(s["o"], proj, gdn_norm[i:i + 1], name="gatenorm_fwd")
            w_mix_out = wi["gdn_w_out"]
        else:
            j = i - n_a
            s["qraw"] = qraw = matmul(hn, wi["sb_w_q"], "nn", name="mm_sq")
            s["q"] = q = headnorm_fwd(qraw, 0, sb_q_norm[j:j + 1], HEAD_DIM ** -0.5, name="headnorm_q")
            s["o"], *got = sb_fwd(q, k_sh, v_sh, gather=nxt, name="sb_fwd")
            mixed, w_mix_out = s["o"], wi["sb_w_out"]
        if nxt:
            wl.append(layer_weights(i + 1, got))
        s["h1"], s["hn2"] = h, hn2 = matmul_rows(mixed, w_mix_out, [h], [ln_ffn[i:i + 1]], resid_rms, (F32, BF16),
                                                 name="mm_out_rms")
        s["gu"], s["act"] = ffn_in_swiglu(hn2, wi["ffn_w_in"], name="ffn_in_swiglu")
        s["h2"], s["hn3"] = h, hn3 = matmul_rows(s["act"], wi["ffn_w_out"], [h], [ln_ple[i:i + 1]], resid_rms,
                                                 (F32, BF16), name="mm_ffn_out_rms")
        s["pp"] = pp = matmul(p[i, 0], wi["ple_w_proj"], "nn", name="mm_ple_proj")
        if i + 1 < depth:
            s["gt"], h, hn_next = matmul_rows(hn3, wi["ple_w_gate"], [h, pp], [ln_mix[i + 1:i + 2]], ple_rms,
                                              (F32, F32, BF16), name="mm_ple_gate_rms")
        else:
            s["gt"], h = matmul_rows(hn3, wi["ple_w_gate"], [h, pp], [], lambda r, res, e: ple_rms(r, res, e, None)[:2],
                                     (F32, F32), name="mm_ple_gate")
        saved.append(s)
        if i == n_a - 1:
            mid = dict(h=h)
            mid["hk"] = hk = rms_fwd(h, kv_norm[None, :], name="rms_fwd")
            mid["kv"] = kv = matmul(hk, wi["w_kv"], "nn", name="mm_kv")
            k_sh = headnorm_fwd(kv, 0, k_norm[None, :], 1.0, name="headnorm_k")
            v_sh = kv[:, WIDTH:].astype(BF16)

    dh, sq = loss_head(h, tgt, name="loss_head")
    loss = lax.psum(0.5 * jnp.sum(sq) / h.shape[1], ("x", "y", "c"))

    gw = {n: [None] * w[n].shape[0] for n in WEIGHTS if w[n].ndim >= 2 and n not in ("w_kv",)}
    dks, dvs = [], []
    reduced = [None] * depth

    def grad_buffers(i):
        pieces = [_shard(gw[n] if idx is None else gw[n][idx], ax) for n, idx, ax in layer_items(i)]
        bufs = [_pack_rows(pieces, (N_CHIPS,)), gw["ffn_w_in"][i]] + ([_shard(gw["gdn_w_in"][i], 1)] if i < n_a else [])
        return [g.reshape(N_CHIPS, 2, g.shape[1] // 2, g.shape[2]) for g in bufs]

    def chip_sums(g5s, from_sibling):
        return [add_selected(sel_chip, g5, fs, name="add_selected") for g5, fs in zip(g5s, from_sibling)]

    def reduce_end(i, s1s, from_chips):
        s2s = [add_chip_sums(sel_chip, s1, fc, name="add_chip_sums") for (s1, _), fc in zip(s1s, from_chips)]
        reduced[i] = sibling_merge(s2s, name="sibling_merge")

    waiting = None
    for i in reversed(range(depth)):
        s = saved[i]
        if i == n_a - 1:
            dkraw, gw["k_norm"] = headnorm_bwd(mid["kv"], 0, k_norm[None, :], 1.0, tuple(dks), name="headnorm_k_bwd")
            dkv = jnp.concatenate([dkraw, sum_cast(dvs, BF16, name="sum_dv")], axis=1)
            dhk = matmul(dkv, wl[i]["w_kv"], "nt", name="mm_kv_dx")
            gw["w_kv"] = matmul(mid["hk"], dkv, "tn", name="mm_kv_dw")
            dh, gw["kv_norm"] = rms_bwd(mid["h"], kv_norm[None, :], dhk, dh, name="rms_bwd")
        dpp, dgt = ple_bwd(dh, s["pp"], s["gt"], name="ple_bwd")
        gw["ple_w_proj"][i] = matmul(p[i, 0], dpp, "tn", name="mm_ple_proj_dw")
        gw["ple_w_gate"][i] = matmul(s["hn3"], dgt, "tn", name="mm_sq_dw")
        dhn3 = matmul(dgt, wl[i]["ple_w_gate"], "nt", name="mm_sq_dx")
        dh, gw["ln_ple"][i] = rms_bwd(s["h2"], ln_ple[i:i + 1], dhn3, dh, name="rms_bwd")
        dgu, *from_sibling = ffn_out_dx_swiglu(dh, wl[i]["ffn_w_out"].T, s["gu"], swap=waiting[1] if waiting else (),
                                               name="ffn_out_dx_swiglu")
        sums = chip_sums(waiting[1], from_sibling) if waiting else []
        riding = [s1b for _, s1b in sums]
        gw["ffn_w_out"][i] = matmul(s["act"], dh, "tn", name="mm_ffn_out_dw")
        dhn2 = matmul(dgu, wl[i]["ffn_w_in"], "nt", b_chips=0, name="mm_ffn_in_dx")
        gw["ffn_w_in"][i] = matmul(s["hn2"], dgu, "tn", out_chips=True, name="mm_ffn_in_dw")
        dh, gw["ln_ffn"][i] = rms_bwd(s["h1"], ln_ffn[i:i + 1], dhn2, dh, name="rms_bwd")
        if i < n_a:
            dy = matmul(dh, wl[i]["gdn_w_out"], "nt", name="mm_sq_dx")
            gw["gdn_w_out"][i] = matmul(s["y"], dh, "tn", name="mm_sq_dw")
            do, dproj, gw["gdn_norm"][i] = gatenorm_bwd(s["o"], s["proj"], gdn_norm[i:i + 1], dy, name="gatenorm_bwd")
            dqkv, dg4, db4, *from_chips = gdn_bwd(s["qkv"], s["g4"], s["b4"], s["s0"], s["tinv"], do, exchange=riding,
                                                  name="gdn_bwd")
            dab, dal, ddt = gates_bwd(s["ab"], s["a_log"], s["dt"], _from4(dg4), _from4(db4), name="gates_bwd")
            gw["gdn_a_log"][i], gw["gdn_dt_bias"][i] = dal[:, 0], ddt[:, 0]
            dproj, gw["gdn_conv"][i] = conv_bwd(s["proj"], conv_full[i], dqkv, dproj, name="conv_bwd")
            dhn = matmul(dproj, s["w_m"], "nt", name="mm_gdn_in_dx")
            dhn = matmul(dab, s["w_abt"], "tn", add=dhn, name="mm_gdn_ab_dx")
            dwm = matmul(s["hn"], dproj, "tn", name="mm_gdn_in_dw")
            dwab = matmul(dab, s["hn"], "nn", name="mm_gdn_ab_dw")
            gw["gdn_w_in"][i] = jnp.concatenate([dwm, dwab.T], axis=1)
        else:
            j = i - n_a
            do = matmul(dh, wl[i]["sb_w_out"], "nt", out_dtype=BF16, name="mm_sb_out_dx")
            gw["sb_w_out"][j] = matmul(s["o"], dh, "tn", name="mm_sq_dw")
            dq, dk, dv, *from_chips = sb_bwd(s["q"], k_sh, v_sh, do, exchange=riding, name="sb_bwd")
            dks.append(dk)
            dvs.append(dv)
            dqraw, gw["sb_q_norm"][j] = headnorm_bwd(s["qraw"], 0, sb_q_norm[j:j + 1], HEAD_DIM ** -0.5, (dq,),
                                                    name="headnorm_q_bwd")
            dhn = matmul(dqraw, wl[i]["sb_w_q"], "nt", name="mm_sq_dx")
            gw["sb_w_q"][j] = matmul(s["hn"], dqraw, "tn", name="mm_sq_dw")
        dh, gw["ln_mix"][i] = rms_bwd(s["h0"], ln_mix[i:i + 1], dhn, dh, name="rms_bwd")
        if waiting:
            reduce_end(waiting[0], sums, from_chips)
        waiting = (i, grad_buffers(i))
    sums = chip_sums(waiting[1], sibling_swap(waiting[1], name="sibling_swap"))
    reduce_end(waiting[0], sums, chip_exchange([s1b for _, s1b in sums], name="chip_exchange"))
    grad_x = dh[None]

    def stacked(n):
        g = gw[n]
        if isinstance(g, list):
            g = jnp.stack([a.reshape(w[n].shape[1:]) if n in SMALL else a for a in g])
        return g

    small_buf = jnp.concatenate([_row(stacked(n)) for n in SMALL] + [_row(stacked("gdn_conv"))], axis=0)
    small_buf = jnp.pad(small_buf, ((0, buf_rows - small_buf.shape[0]), (0, 0)))
    small_sum = all_reduce_small(small_buf, name="all_reduce_small")
    grads = {}
    r0 = 0
    for n in SMALL:
        rows = _rows_of(w[n].shape, 1)
        grads[n] = small_sum[r0:r0 + rows].reshape(-1)[:math.prod(w[n].shape)].reshape(w[n].shape)
        r0 += rows
    conv_g = small_sum[r0:r0 + N_CHIPS * conv_rows].reshape(-1)[:N_CHIPS * math.prod(gdn_conv.shape)]
    conv_g = conv_g.reshape((gdn_conv.shape[0], CONV_WIDTH, N_CHIPS, gdn_conv.shape[2]))
    grads["gdn_conv"] = lax.dynamic_index_in_dim(conv_g, chip, axis=2, keepdims=False)

    per_layer = {n: [None] * w[n].shape[0] for n, _ in BIG if n != "w_kv"}
    for i in range(depth):
        parts = _unpack_rows(reduced[i][0].reshape(-1, PACK_COLS), [q.shape for q in layer_shards(i)], ())
        for (n, idx, _), g in zip(layer_items(i), parts):
            if idx is None:
                grads[n] = g
            else:
                per_layer[n][idx] = g
    for n, parts in per_layer.items():
        grads[n] = jnp.stack(parts)
    grads["ffn_w_in"] = jnp.stack([reduced[i][1].reshape(ffn_w_in.shape[1:]) for i in range(depth)])
    grads["gdn_w_in"] = jnp.stack([reduced[i][2].reshape(gdn_w_in.shape[1:]) for i in range(n_a)])

    delta, new_m, new_v = {}, {}, {}
    for n in WEIGHTS:
        delta[n], new_m[n], new_v[n] = adamw(w[n], grads[n], mom1[n], mom2[n], name="adamw")
    return (loss, grad_x, *[grads[n] for n in WEIGHTS], *[delta[n] for n in WEIGHTS],
            *[new_m[n] for n in WEIGHTS], *[new_v[n] for n in WEIGHTS])
```
